```python
import math
import jax, jax.numpy as jnp
from jax import lax
import numpy as np

D_MODEL = 1024
BATCH = 8
SEQ = 8192
DEPTH = 4

CHUNK = 64
N_A_LAYERS = DEPTH // 2
N_B_LAYERS = DEPTH - N_A_LAYERS

A_HEADS = 8
A_HEAD_K = 128
A_HEAD_V = 128
A_QK_WIDTH = A_HEADS * A_HEAD_K
A_V_WIDTH = A_HEADS * A_HEAD_V
A_CONV = 4
A_CONV_WIDTH = 2 * A_QK_WIDTH + A_V_WIDTH
A_IN_WIDTH = 2 * A_QK_WIDTH + 2 * A_V_WIDTH + 2 * A_HEADS

B_HEADS = 16
B_HEAD_DIM = 64
B_WIDTH = B_HEADS * B_HEAD_DIM
LEFT_CHUNKS = 8
BAND = (LEFT_CHUNKS + 1) * CHUNK
REL_CLIP = 256

FFN_DIM = 2816
FFN_CONV = 3

EPS = 1e-6
NEG_INF = -1e30

kernel_name = "yoco_gdn_chunkattn_convffn"


def rmsnorm(x, g):
    xf = x.astype(jnp.float32)
    y = xf * lax.rsqrt(jnp.mean(xf * xf, axis=-1, keepdims=True) + EPS)
    return (y * g.astype(jnp.float32)).astype(x.dtype)


def causal_dwconv(x, w):
    width = w.shape[0]
    return lax.conv_general_dilated(
        x, w[:, None, :].astype(x.dtype), window_strides=(1,), padding=[(width - 1, 0)],
        dimension_numbers=("NWC", "WIO", "NWC"), feature_group_count=x.shape[-1])


def _l2norm(t):
    return t * lax.rsqrt(jnp.sum(t * t, axis=-1, keepdims=True) + EPS)


def chunk_gated_delta_rule(q, k, v, beta, g):
    bsz, seq, nh, dk = q.shape
    dv = v.shape[-1]
    nc = seq // CHUNK

    def chunks(t):
        return t.reshape(bsz, nc, CHUNK, nh, -1).transpose(0, 3, 1, 2, 4)

    q, k, v = chunks(q), chunks(k), chunks(v)
    beta = chunks(beta[..., None])[..., 0]
    gcum = jnp.cumsum(chunks(g[..., None])[..., 0], axis=-1)

    causal = jnp.tril(jnp.ones((CHUNK, CHUNK), dtype=bool))
    strict = jnp.tril(jnp.ones((CHUNK, CHUNK), dtype=bool), k=-1)
    diff = gcum[..., :, None] - gcum[..., None, :]
    decay = jnp.where(causal, jnp.exp(jnp.where(causal, diff, 0.0)), 0.0)

    k_beta = k * beta[..., None]
    m = jnp.where(strict, jnp.einsum("bhnid,bhnjd->bhnij", k_beta, k) * decay, 0.0)
    eye = jnp.eye(CHUNK, dtype=m.dtype)
    rhs = jnp.concatenate([v * beta[..., None], k_beta * jnp.exp(gcum)[..., None]], axis=-1)
    uw = lax.linalg.triangular_solve(m + eye, rhs, left_side=True, lower=True, unit_diagonal=True)
    u, w = uw[..., :dv], uw[..., dv:]

    attn_qk = jnp.einsum("bhnid,bhnjd->bhnij", q, k) * decay
    q_dec = q * jnp.exp(gcum)[..., None]
    k_end = k * jnp.exp(gcum[..., -1:] - gcum)[..., None]
    chunk_decay = jnp.exp(gcum[..., -1])

    xs = tuple(jnp.moveaxis(t, 2, 0) for t in (q_dec, k_end, u, w, attn_qk, chunk_decay))

    def step(state, inp):
        qd, ke, u_c, w_c, a_c, dec = inp
        v_new = u_c - jnp.einsum("bhcd,bhdv->bhcv", w_c, state)
        o_c = jnp.einsum("bhcd,bhdv->bhcv", qd, state) + jnp.einsum("bhcj,bhjv->bhcv", a_c, v_new)
        state = state * dec[..., None, None] + jnp.einsum("bhcd,bhcv->bhdv", ke, v_new)
        return state, o_c

    state0 = jnp.zeros((bsz, nh, dk, dv), jnp.float32)
    _, o = lax.scan(step, state0, xs)
    return o.transpose(1, 0, 3, 2, 4).reshape(bsz, seq, nh, dv)


def gated_deltanet(xn, w_in, conv_w, A_log, dt_bias, out_norm_w, w_out):
    bsz, seq, _ = xn.shape
    proj = xn @ w_in
    qkv = jax.nn.silu(causal_dwconv(proj[..., :A_CONV_WIDTH], conv_w))
    z = proj[..., A_CONV_WIDTH:A_CONV_WIDTH + A_V_WIDTH]
    b_raw = proj[..., A_CONV_WIDTH + A_V_WIDTH:A_CONV_WIDTH + A_V_WIDTH + A_HEADS]
    a_raw = proj[..., A_CONV_WIDTH + A_V_WIDTH + A_HEADS:]
    f32 = jnp.float32
    q = _l2norm(qkv[..., :A_QK_WIDTH].reshape(bsz, seq, A_HEADS, A_HEAD_K).astype(f32)) * (A_HEAD_K ** -0.5)
    k = _l2norm(qkv[..., A_QK_WIDTH:2 * A_QK_WIDTH].reshape(bsz, seq, A_HEADS, A_HEAD_K).astype(f32))
    v = qkv[..., 2 * A_QK_WIDTH:].reshape(bsz, seq, A_HEADS, A_HEAD_V).astype(f32)
    beta = jax.nn.sigmoid(b_raw.astype(f32))
    g = -jnp.exp(A_log.astype(f32)) * jax.nn.softplus(a_raw.astype(f32) + dt_bias.astype(f32))
    o = chunk_gated_delta_rule(q, k, v, beta, g).astype(xn.dtype)
    o = rmsnorm(o, out_norm_w) * jax.nn.silu(z.reshape(bsz, seq, A_HEADS, A_HEAD_V))
    return o.reshape(bsz, seq, A_V_WIDTH) @ w_out


def chunk_attention(xn, w_q, rel_bias, w_out, k_pad, v_pad):
    bsz, seq, _ = xn.shape
    nc = seq // CHUNK
    q = (xn @ w_q).reshape(bsz, nc, CHUNK, B_HEADS, B_HEAD_DIM).transpose(1, 0, 2, 3, 4)
    rel = jnp.arange(CHUNK)[:, None] + LEFT_CHUNKS * CHUNK - jnp.arange(BAND)[None, :]
    bias = rel_bias[:, jnp.clip(rel, -REL_CLIP, REL_CLIP) + REL_CLIP].astype(jnp.float32)
    scale = B_HEAD_DIM ** -0.5

    def one_chunk(args):
        n, q_n = args
        k_band = lax.dynamic_slice_in_dim(k_pad, n * CHUNK, BAND, axis=1)
        v_band = lax.dynamic_slice_in_dim(v_pad, n * CHUNK, BAND, axis=1)
        s = jnp.einsum("bqhd,bkhd->bhqk", q_n, k_band).astype(jnp.float32) * scale + bias
        valid = jnp.arange(BAND) >= (LEFT_CHUNKS - n) * CHUNK
        s = jnp.where(valid, s, NEG_INF)
        p = jax.nn.softmax(s, axis=-1).astype(v_band.dtype)
        return jnp.einsum("bhqk,bkhd->bqhd", p, v_band)

    o = lax.map(one_chunk, (jnp.arange(nc, dtype=jnp.int32), q))
    o = o.transpose(1, 0, 2, 3, 4).reshape(bsz, seq, B_WIDTH)
    return o @ w_out


def conv_ffn(xn, w_up, conv_w, conv_b, w_down):
    h = causal_dwconv(xn @ w_up, conv_w) + conv_b
    gate, val = h[..., :FFN_DIM], h[..., FFN_DIM:]
    return (jax.nn.silu(gate) * val) @ w_down


def _fwd_setup_inputs(seed: int = 0) -> dict:
    key = jax.random.key(seed)
    ks = jax.random.split(key, 22)

    def nrm(k, shape, scale):
        return jax.random.normal(k, shape, jnp.float32) * scale

    dt = jnp.exp(jax.random.uniform(ks[5], (N_A_LAYERS, A_HEADS), jnp.float32,
                                    minval=math.log(1e-3), maxval=math.log(1e-1)))
    return {
        "x": nrm(ks[0], (BATCH, SEQ, D_MODEL), 1.0),
        "a_norm": 1.0 + nrm(ks[1], (N_A_LAYERS, D_MODEL), 0.02),
        "a_w_in": nrm(ks[2], (N_A_LAYERS, D_MODEL, A_IN_WIDTH), D_MODEL ** -0.5),
        "a_conv": nrm(ks[3], (N_A_LAYERS, A_CONV, A_CONV_WIDTH), A_CONV ** -0.5),
        "a_A_log": jnp.log(jax.random.uniform(ks[4], (N_A_LAYERS, A_HEADS), jnp.float32, minval=1.0, maxval=16.0)),
        "a_dt_bias": dt + jnp.log(-jnp.expm1(-dt)),
        "a_out_norm": 1.0 + nrm(ks[6], (N_A_LAYERS, A_HEAD_V), 0.02),
        "a_w_out": nrm(ks[7], (N_A_LAYERS, A_V_WIDTH, D_MODEL), A_V_WIDTH ** -0.5),
        "kv_norm": 1.0 + nrm(ks[8], (D_MODEL,), 0.02),
        "w_kv": nrm(ks[9], (D_MODEL, 2 * B_WIDTH), D_MODEL ** -0.5),
        "b_norm": 1.0 + nrm(ks[10], (N_B_LAYERS, D_MODEL), 0.02),
        "b_w_q": nrm(ks[11], (N_B_LAYERS, D_MODEL, B_WIDTH), D_MODEL ** -0.5),
        "b_rel_bias": nrm(ks[12], (N_B_LAYERS, B_HEADS, 2 * REL_CLIP + 1), 0.1),
        "b_w_out": nrm(ks[13], (N_B_LAYERS, B_WIDTH, D_MODEL), B_WIDTH ** -0.5),
        "f_norm": 1.0 + nrm(ks[14], (DEPTH, D_MODEL), 0.02),
        "f_w_up": nrm(ks[15], (DEPTH, D_MODEL, 2 * FFN_DIM), D_MODEL ** -0.5),
        "f_conv": nrm(ks[16], (DEPTH, FFN_CONV, 2 * FFN_DIM), FFN_CONV ** -0.5),
        "f_conv_b": nrm(ks[17], (DEPTH, 2 * FFN_DIM), 0.01),
        "f_w_down": nrm(ks[18], (DEPTH, FFN_DIM, D_MODEL), FFN_DIM ** -0.5),
        "final_norm": 1.0 + nrm(ks[19], (D_MODEL,), 0.02),
    }


def _fwd_reference(x, a_norm, a_w_in, a_conv, a_A_log, a_dt_bias, a_out_norm, a_w_out,
              kv_norm, w_kv, b_norm, b_w_q, b_rel_bias, b_w_out,
              f_norm, f_w_up, f_conv, f_conv_b, f_w_down, final_norm):
    bsz, seq, _ = x.shape
    h = x
    k_pad = None
    v_pad = None
    for layer in range(DEPTH):
        if layer < N_A_LAYERS:
            i = layer
            h = h + gated_deltanet(rmsnorm(h, a_norm[i]), a_w_in[i], a_conv[i], a_A_log[i],
                                   a_dt_bias[i], a_out_norm[i], a_w_out[i])
        else:
            if layer == N_A_LAYERS:
                kv = rmsnorm(h, kv_norm) @ w_kv
                pad = ((0, 0), (LEFT_CHUNKS * CHUNK, 0), (0, 0), (0, 0))
                k_pad = jnp.pad(kv[..., :B_WIDTH].reshape(bsz, seq, B_HEADS, B_HEAD_DIM), pad)
                v_pad = jnp.pad(kv[..., B_WIDTH:].reshape(bsz, seq, B_HEADS, B_HEAD_DIM), pad)
            j = layer - N_A_LAYERS
            h = h + chunk_attention(rmsnorm(h, b_norm[j]), b_w_q[j], b_rel_bias[j], b_w_out[j], k_pad, v_pad)
        h = h + conv_ffn(rmsnorm(h, f_norm[layer]), f_w_up[layer], f_conv[layer], f_conv_b[layer], f_w_down[layer])
    return rmsnorm(h, final_norm)


import jax as _jax
import jax.numpy as _jnp

TWIN_FORMAT = 'train_step'
FWD_PARAMS = ['x', 'a_norm', 'a_w_in', 'a_conv', 'a_A_log', 'a_dt_bias', 'a_out_norm', 'a_w_out', 'kv_norm', 'w_kv', 'b_norm', 'b_w_q', 'b_rel_bias', 'b_w_out', 'f_norm', 'f_w_up', 'f_conv', 'f_conv_b', 'f_w_down', 'final_norm']
TWIN_WEIGHTS = ['a_norm', 'a_w_in', 'a_conv', 'a_A_log', 'a_dt_bias', 'a_out_norm', 'a_w_out', 'kv_norm', 'w_kv', 'b_norm', 'b_w_q', 'b_rel_bias', 'b_w_out', 'f_norm', 'f_w_up', 'f_conv', 'f_conv_b', 'f_w_down', 'final_norm']
TWIN_DIFF_INPUT = 'x'
TWIN_INPUTS = ['x', 'a_norm', 'a_w_in', 'a_conv', 'a_A_log', 'a_dt_bias', 'a_out_norm', 'a_w_out', 'kv_norm', 'w_kv', 'b_norm', 'b_w_q', 'b_rel_bias', 'b_w_out', 'f_norm', 'f_w_up', 'f_conv', 'f_conv_b', 'f_w_down', 'final_norm', 'loss_target', 'm_a_norm', 'm_a_w_in', 'm_a_conv', 'm_a_A_log', 'm_a_dt_bias', 'm_a_out_norm', 'm_a_w_out', 'm_kv_norm', 'm_w_kv', 'm_b_norm', 'm_b_w_q', 'm_b_rel_bias', 'm_b_w_out', 'm_f_norm', 'm_f_w_up', 'm_f_conv', 'm_f_conv_b', 'm_f_w_down', 'm_final_norm', 'v_a_norm', 'v_a_w_in', 'v_a_conv', 'v_a_A_log', 'v_a_dt_bias', 'v_a_out_norm', 'v_a_w_out', 'v_kv_norm', 'v_w_kv', 'v_b_norm', 'v_b_w_q', 'v_b_rel_bias', 'v_b_w_out', 'v_f_norm', 'v_f_w_up', 'v_f_conv', 'v_f_conv_b', 'v_f_w_down', 'v_final_norm']
TWIN_OUTPUTS = ['loss', 'grad_x', 'grad_a_norm', 'grad_a_w_in', 'grad_a_conv', 'grad_a_A_log', 'grad_a_dt_bias', 'grad_a_out_norm', 'grad_a_w_out', 'grad_kv_norm', 'grad_w_kv', 'grad_b_norm', 'grad_b_w_q', 'grad_b_rel_bias', 'grad_b_w_out', 'grad_f_norm', 'grad_f_w_up', 'grad_f_conv', 'grad_f_conv_b', 'grad_f_w_down', 'grad_final_norm', 'delta_a_norm', 'delta_a_w_in', 'delta_a_conv', 'delta_a_A_log', 'delta_a_dt_bias', 'delta_a_out_norm', 'delta_a_w_out', 'delta_kv_norm', 'delta_w_kv', 'delta_b_norm', 'delta_b_w_q', 'delta_b_rel_bias', 'delta_b_w_out', 'delta_f_norm', 'delta_f_w_up', 'delta_f_conv', 'delta_f_conv_b', 'delta_f_w_down', 'delta_final_norm', 'new_m_a_norm', 'new_m_a_w_in', 'new_m_a_conv', 'new_m_a_A_log', 'new_m_a_dt_bias', 'new_m_a_out_norm', 'new_m_a_w_out', 'new_m_kv_norm', 'new_m_w_kv', 'new_m_b_norm', 'new_m_b_w_q', 'new_m_b_rel_bias', 'new_m_b_w_out', 'new_m_f_norm', 'new_m_f_w_up', 'new_m_f_conv', 'new_m_f_conv_b', 'new_m_f_w_down', 'new_m_final_norm', 'new_v_a_norm', 'new_v_a_w_in', 'new_v_a_conv', 'new_v_a_A_log', 'new_v_a_dt_bias', 'new_v_a_out_norm', 'new_v_a_w_out', 'new_v_kv_norm', 'new_v_w_kv', 'new_v_b_norm', 'new_v_b_w_q', 'new_v_b_rel_bias', 'new_v_b_w_out', 'new_v_f_norm', 'new_v_f_w_up', 'new_v_f_conv', 'new_v_f_conv_b', 'new_v_f_w_down', 'new_v_final_norm']
TWIN_LEAF_KINDS = {'loss': 'loss', 'grad_x': 'grad_x', 'grad_a_norm': 'grad_w', 'grad_a_w_in': 'grad_w', 'grad_a_conv': 'grad_w', 'grad_a_A_log': 'grad_w', 'grad_a_dt_bias': 'grad_w', 'grad_a_out_norm': 'grad_w', 'grad_a_w_out': 'grad_w', 'grad_kv_norm': 'grad_w', 'grad_w_kv': 'grad_w', 'grad_b_norm': 'grad_w', 'grad_b_w_q': 'grad_w', 'grad_b_rel_bias': 'grad_w', 'grad_b_w_out': 'grad_w', 'grad_f_norm': 'grad_w', 'grad_f_w_up': 'grad_w', 'grad_f_conv': 'grad_w', 'grad_f_conv_b': 'grad_w', 'grad_f_w_down': 'grad_w', 'grad_final_norm': 'grad_w', 'delta_a_norm': 'delta_w', 'delta_a_w_in': 'delta_w', 'delta_a_conv': 'delta_w', 'delta_a_A_log': 'delta_w', 'delta_a_dt_bias': 'delta_w', 'delta_a_out_norm': 'delta_w', 'delta_a_w_out': 'delta_w', 'delta_kv_norm': 'delta_w', 'delta_w_kv': 'delta_w', 'delta_b_norm': 'delta_w', 'delta_b_w_q': 'delta_w', 'delta_b_rel_bias': 'delta_w', 'delta_b_w_out': 'delta_w', 'delta_f_norm': 'delta_w', 'delta_f_w_up': 'delta_w', 'delta_f_conv': 'delta_w', 'delta_f_conv_b': 'delta_w', 'delta_f_w_down': 'delta_w', 'delta_final_norm': 'delta_w', 'new_m_a_norm': 'new_m', 'new_m_a_w_in': 'new_m', 'new_m_a_conv': 'new_m', 'new_m_a_A_log': 'new_m', 'new_m_a_dt_bias': 'new_m', 'new_m_a_out_norm': 'new_m', 'new_m_a_w_out': 'new_m', 'new_m_kv_norm': 'new_m', 'new_m_w_kv': 'new_m', 'new_m_b_norm': 'new_m', 'new_m_b_w_q': 'new_m', 'new_m_b_rel_bias': 'new_m', 'new_m_b_w_out': 'new_m', 'new_m_f_norm': 'new_m', 'new_m_f_w_up': 'new_m', 'new_m_f_conv': 'new_m', 'new_m_f_conv_b': 'new_m', 'new_m_f_w_down': 'new_m', 'new_m_final_norm': 'new_m', 'new_v_a_norm': 'new_v', 'new_v_a_w_in': 'new_v', 'new_v_a_conv': 'new_v', 'new_v_a_A_log': 'new_v', 'new_v_a_dt_bias': 'new_v', 'new_v_a_out_norm': 'new_v', 'new_v_a_w_out': 'new_v', 'new_v_kv_norm': 'new_v', 'new_v_w_kv': 'new_v', 'new_v_b_norm': 'new_v', 'new_v_b_w_q': 'new_v', 'new_v_b_rel_bias': 'new_v', 'new_v_b_w_out': 'new_v', 'new_v_f_norm': 'new_v', 'new_v_f_w_up': 'new_v', 'new_v_f_conv': 'new_v', 'new_v_f_conv_b': 'new_v', 'new_v_f_w_down': 'new_v', 'new_v_final_norm': 'new_v'}


def _forward(args):
    return _fwd_reference(*[args[k] for k in FWD_PARAMS])


def _output_shape():
    def fwd():
        inp = _fwd_setup_inputs(0)
        return _fwd_reference(*[inp[k] for k in FWD_PARAMS])
    out = _jax.eval_shape(fwd)
    return out.shape, out.dtype

N_MICROBATCH = 1
ADAM_LR = 0.001
ADAM_B1 = 0.9
ADAM_B2 = 0.999
ADAM_EPS = 1e-08
ADAM_WD = 0.01
ADAM_STEP = 10
PER_EXAMPLE_BATCH_AXIS = {'x': 0, 'loss_target': 0}
SHARED_INPUTS = []
_WEIGHT_DTYPES = {'a_norm': _jnp.float32, 'a_w_in': _jnp.float32, 'a_conv': _jnp.float32, 'a_A_log': _jnp.float32, 'a_dt_bias': _jnp.float32, 'a_out_norm': _jnp.float32, 'a_w_out': _jnp.float32, 'kv_norm': _jnp.float32, 'w_kv': _jnp.float32, 'b_norm': _jnp.float32, 'b_w_q': _jnp.float32, 'b_rel_bias': _jnp.float32, 'b_w_out': _jnp.float32, 'f_norm': _jnp.float32, 'f_w_up': _jnp.float32, 'f_conv': _jnp.float32, 'f_conv_b': _jnp.float32, 'f_w_down': _jnp.float32, 'final_norm': _jnp.float32}
MOMENT_SCALE = {'a_norm': 2.828885e-01, 'a_w_in': 1.397760e-01, 'a_conv': 1.279276e-01, 'a_A_log': 8.365698e-01, 'a_dt_bias': 7.884399e-01, 'a_out_norm': 4.889830e-01, 'a_w_out': 1.672425e-01, 'kv_norm': 5.655929e-02, 'w_kv': 3.977320e-02, 'b_norm': 2.378925e-02, 'b_w_q': 2.402158e-02, 'b_rel_bias': 8.113066e-03, 'b_w_out': 3.133135e-02, 'f_norm': 1.762333e-01, 'f_w_up': 7.232529e-02, 'f_conv': 7.262539e-02, 'f_conv_b': 7.299731e-02, 'f_w_down': 1.182996e-01, 'final_norm': 6.400963e+01}


def _to_microbatches(a, axis):
    t = _jnp.moveaxis(a, axis, 0)
    t = t.reshape((N_MICROBATCH, t.shape[0] // N_MICROBATCH) + t.shape[1:])
    return _jnp.moveaxis(t, 1, axis + 1)


def setup_inputs(seed: int = 0) -> dict:
    inp = _fwd_setup_inputs(seed)
    key = _jax.random.fold_in(_jax.random.key(seed), 7919)
    shape, _ = _output_shape()
    out = dict(inp)
    out["loss_target"] = _jax.random.normal(_jax.random.fold_in(key, 0), shape, _jnp.float32)
    for i, name in enumerate(TWIN_WEIGHTS):
        w = inp[name].astype(_jnp.float32)
        if MOMENT_SCALE is None:
            s = _jnp.sqrt(_jnp.mean(_jnp.square(w)) + 1e-30)
        else:
            s = MOMENT_SCALE[name]
        km, kv = _jax.random.split(_jax.random.fold_in(key, i + 1))
        out[name] = w
        out["m_" + name] = s * _jax.random.normal(km, w.shape, _jnp.float32)
        out["v_" + name] = (s * s) * _jax.random.uniform(kv, w.shape, _jnp.float32, 0.5, 1.5)
    if N_MICROBATCH > 1:
        for name, axis in PER_EXAMPLE_BATCH_AXIS.items():
            out[name] = _to_microbatches(out[name], axis)
    return {'x': out['x'], 'a_norm': out['a_norm'], 'a_w_in': out['a_w_in'], 'a_conv': out['a_conv'], 'a_A_log': out['a_A_log'], 'a_dt_bias': out['a_dt_bias'], 'a_out_norm': out['a_out_norm'], 'a_w_out': out['a_w_out'], 'kv_norm': out['kv_norm'], 'w_kv': out['w_kv'], 'b_norm': out['b_norm'], 'b_w_q': out['b_w_q'], 'b_rel_bias': out['b_rel_bias'], 'b_w_out': out['b_w_out'], 'f_norm': out['f_norm'], 'f_w_up': out['f_w_up'], 'f_conv': out['f_conv'], 'f_conv_b': out['f_conv_b'], 'f_w_down': out['f_w_down'], 'final_norm': out['final_norm'], 'loss_target': out['loss_target'], 'm_a_norm': out['m_a_norm'], 'm_a_w_in': out['m_a_w_in'], 'm_a_conv': out['m_a_conv'], 'm_a_A_log': out['m_a_A_log'], 'm_a_dt_bias': out['m_a_dt_bias'], 'm_a_out_norm': out['m_a_out_norm'], 'm_a_w_out': out['m_a_w_out'], 'm_kv_norm': out['m_kv_norm'], 'm_w_kv': out['m_w_kv'], 'm_b_norm': out['m_b_norm'], 'm_b_w_q': out['m_b_w_q'], 'm_b_rel_bias': out['m_b_rel_bias'], 'm_b_w_out': out['m_b_w_out'], 'm_f_norm': out['m_f_norm'], 'm_f_w_up': out['m_f_w_up'], 'm_f_conv': out['m_f_conv'], 'm_f_conv_b': out['m_f_conv_b'], 'm_f_w_down': out['m_f_w_down'], 'm_final_norm': out['m_final_norm'], 'v_a_norm': out['v_a_norm'], 'v_a_w_in': out['v_a_w_in'], 'v_a_conv': out['v_a_conv'], 'v_a_A_log': out['v_a_A_log'], 'v_a_dt_bias': out['v_a_dt_bias'], 'v_a_out_norm': out['v_a_out_norm'], 'v_a_w_out': out['v_a_w_out'], 'v_kv_norm': out['v_kv_norm'], 'v_w_kv': out['v_w_kv'], 'v_b_norm': out['v_b_norm'], 'v_b_w_q': out['v_b_w_q'], 'v_b_rel_bias': out['v_b_rel_bias'], 'v_b_w_out': out['v_b_w_out'], 'v_f_norm': out['v_f_norm'], 'v_f_w_up': out['v_f_w_up'], 'v_f_conv': out['v_f_conv'], 'v_f_conv_b': out['v_f_conv_b'], 'v_f_w_down': out['v_f_w_down'], 'v_final_norm': out['v_final_norm']}


def _loss(weights, diff, rest, loss_target):
    with _jax.named_scope("forward"):
        args = {**rest, TWIN_DIFF_INPUT: diff, **{k: w.astype(_WEIGHT_DTYPES[k]) for k, w in weights.items()}}
        y = _forward(args)
    with _jax.named_scope("loss_head"):
        err = _jnp.square(y.astype(_jnp.float32) - loss_target)
        return 0.5 * _jnp.sum(_jnp.mean(err, axis=-1)) if err.ndim else 0.5 * err


def _adamw(w, g, m, v):
    m = ADAM_B1 * m + (1.0 - ADAM_B1) * g
    v = ADAM_B2 * v + (1.0 - ADAM_B2) * _jnp.square(g)
    m_hat = m / (1.0 - ADAM_B1 ** ADAM_STEP)
    v_hat = v / (1.0 - ADAM_B2 ** ADAM_STEP)
    delta = -ADAM_LR * (m_hat / (_jnp.sqrt(v_hat) + ADAM_EPS) + ADAM_WD * w)
    return delta, m, v


def reference(x, a_norm, a_w_in, a_conv, a_A_log, a_dt_bias, a_out_norm, a_w_out, kv_norm, w_kv, b_norm, b_w_q, b_rel_bias, b_w_out, f_norm, f_w_up, f_conv, f_conv_b, f_w_down, final_norm, loss_target, m_a_norm, m_a_w_in, m_a_conv, m_a_A_log, m_a_dt_bias, m_a_out_norm, m_a_w_out, m_kv_norm, m_w_kv, m_b_norm, m_b_w_q, m_b_rel_bias, m_b_w_out, m_f_norm, m_f_w_up, m_f_conv, m_f_conv_b, m_f_w_down, m_final_norm, v_a_norm, v_a_w_in, v_a_conv, v_a_A_log, v_a_dt_bias, v_a_out_norm, v_a_w_out, v_kv_norm, v_w_kv, v_b_norm, v_b_w_q, v_b_rel_bias, v_b_w_out, v_f_norm, v_f_w_up, v_f_conv, v_f_conv_b, v_f_w_down, v_final_norm):
    given = dict(x=x, a_norm=a_norm, a_w_in=a_w_in, a_conv=a_conv, a_A_log=a_A_log, a_dt_bias=a_dt_bias, a_out_norm=a_out_norm, a_w_out=a_w_out, kv_norm=kv_norm, w_kv=w_kv, b_norm=b_norm, b_w_q=b_w_q, b_rel_bias=b_rel_bias, b_w_out=b_w_out, f_norm=f_norm, f_w_up=f_w_up, f_conv=f_conv, f_conv_b=f_conv_b, f_w_down=f_w_down, final_norm=final_norm, loss_target=loss_target, m_a_norm=m_a_norm, m_a_w_in=m_a_w_in, m_a_conv=m_a_conv, m_a_A_log=m_a_A_log, m_a_dt_bias=m_a_dt_bias, m_a_out_norm=m_a_out_norm, m_a_w_out=m_a_w_out, m_kv_norm=m_kv_norm, m_w_kv=m_w_kv, m_b_norm=m_b_norm, m_b_w_q=m_b_w_q, m_b_rel_bias=m_b_rel_bias, m_b_w_out=m_b_w_out, m_f_norm=m_f_norm, m_f_w_up=m_f_w_up, m_f_conv=m_f_conv, m_f_conv_b=m_f_conv_b, m_f_w_down=m_f_w_down, m_final_norm=m_final_norm, v_a_norm=v_a_norm, v_a_w_in=v_a_w_in, v_a_conv=v_a_conv, v_a_A_log=v_a_A_log, v_a_dt_bias=v_a_dt_bias, v_a_out_norm=v_a_out_norm, v_a_w_out=v_a_w_out, v_kv_norm=v_kv_norm, v_w_kv=v_w_kv, v_b_norm=v_b_norm, v_b_w_q=v_b_w_q, v_b_rel_bias=v_b_rel_bias, v_b_w_out=v_b_w_out, v_f_norm=v_f_norm, v_f_w_up=v_f_w_up, v_f_conv=v_f_conv, v_f_conv_b=v_f_conv_b, v_f_w_down=v_f_w_down, v_final_norm=v_final_norm)
    weights = {n: given[n] for n in TWIN_WEIGHTS}
    shared = {n: given[n] for n in SHARED_INPUTS}
    per_example = {n: given[n] for n in ['x']}
    grad_fn = _jax.value_and_grad(_loss, argnums=(0, 1))

    def one_microbatch(ex, loss_target):
        ex = dict(ex)
        diff = ex.pop(TWIN_DIFF_INPUT)
        return grad_fn(weights, diff, {**shared, **ex}, loss_target)

    if N_MICROBATCH == 1:
        loss, (grad_w, grad_x) = one_microbatch(per_example, given["loss_target"])
    else:
        def body(carry, xs):
            loss_sum, grad_sum = carry
            l_k, (gw_k, gx_k) = one_microbatch(xs[0], xs[1])
            with _jax.named_scope("update"):
                return (loss_sum + l_k, _jax.tree.map(_jnp.add, grad_sum, gw_k)), gx_k

        init = (_jnp.zeros((), _jnp.float32), _jax.tree.map(_jnp.zeros_like, weights))
        (loss, grad_w), grad_x = _jax.lax.scan(body, init, (per_example, given["loss_target"]))
    with _jax.named_scope("update"):
        delta_w, new_m, new_v = {}, {}, {}
        for n in TWIN_WEIGHTS:
            delta_w[n], new_m[n], new_v[n] = _adamw(weights[n], grad_w[n], given["m_" + n], given["v_" + n])
    return (loss, grad_x, *[grad_w[n] for n in TWIN_WEIGHTS], *[delta_w[n] for n in TWIN_WEIGHTS],
            *[new_m[n] for n in TWIN_WEIGHTS], *[new_v[n] for n in TWIN_WEIGHTS])
```

```python
import functools
import math

import jax
import jax.numpy as jnp
from jax import lax
from jax.experimental import pallas as pl
from jax.experimental.pallas import tpu as pltpu

F32 = jnp.float32
BF16 = jnp.bfloat16
HI = lax.Precision.HIGHEST
MESH = pl.DeviceIdType.MESH

EPS = 1e-6
NEG_INF = -1e30
CHUNK = 64
LEFT_CHUNKS = 8
BAND = (LEFT_CHUNKS + 1) * CHUNK
BAND_PAD = 640
A_CONV = 4
F_CONV = 3
A_HEAD = 128
B_HEAD = 64
LANES = 128
HALO = 8
N_DEV = 8

ADAM_LR = 0.001
ADAM_B1 = 0.9
ADAM_B2 = 0.999
ADAM_EPS = 1e-08
ADAM_WD = 0.01
ADAM_STEP = 10

VMEM_LIMIT_V7X = 56 * 1024 * 1024
FFN_TILE = 128


def _cp(sem=None, vmem=VMEM_LIMIT_V7X):
    kw = dict(vmem_limit_bytes=vmem)
    if sem is not None:
        kw["dimension_semantics"] = sem
    return pltpu.CompilerParams(**kw)


def _pick(n, target, q=LANES):
    best = None
    for t in range(q, min(n, target) + 1, q):
        if n % t == 0:
            best = t
    return best if best is not None else n


def _sig(x):
    return 1.0 / (1.0 + jnp.exp(-x))


def _softplus(x):
    return jnp.maximum(x, 0.0) + jnp.log(1.0 + jnp.exp(-jnp.abs(x)))


def _rms(x, g):
    return x * lax.rsqrt(jnp.mean(x * x, axis=-1, keepdims=True) + EPS) * g


def _rms_bwd(x, g, dxn):
    r = lax.rsqrt(jnp.mean(x * x, axis=-1, keepdims=True) + EPS)
    gd = dxn * g
    dx = r * gd - x * (r * r * r) * jnp.mean(x * gd, axis=-1, keepdims=True)
    dg = jnp.sum(dxn * x * r, axis=0, keepdims=True)
    return dx, dg


def _dot(a, b):
    return jnp.dot(a, b, preferred_element_type=F32)


def _dot_nt(a, b):
    return lax.dot_general(a, b, (((1,), (1,)), ((), ())), preferred_element_type=F32)


def _dot_tn(a, b):
    return lax.dot_general(a, b, (((0,), (0,)), ((), ())), preferred_element_type=F32)


def _hdot(a, b):
    return jnp.dot(a, b, precision=HI, preferred_element_type=F32)


def _hdot_nt(a, b):
    return lax.dot_general(a, b, (((1,), (1,)), ((), ())), precision=HI, preferred_element_type=F32)


def _hdot_tn(a, b):
    return lax.dot_general(a, b, (((0,), (0,)), ((), ())), precision=HI, preferred_element_type=F32)


def _resident(shape, index_map):
    return pl.BlockSpec(shape, index_map, pipeline_mode=pl.Buffered(1))


def norm_matmul(h, gamma, w, name):
    n, d = h.shape
    nc = w.shape[1]
    tm = _pick(n, 512, 8)
    tn = _pick(nc, 1536)

    def body(h_ref, g_ref, w_ref, o_ref):
        xn = _rms(h_ref[...], g_ref[...])
        o_ref[...] = _dot(xn.astype(BF16), w_ref[...])

    return pl.pallas_call(
        body, grid=(nc // tn, n // tm),
        in_specs=[pl.BlockSpec((tm, d), lambda j, i: (i, 0)),
                  pl.BlockSpec((1, d), lambda j, i: (0, 0)),
                  pl.BlockSpec((d, tn), lambda j, i: (0, j))],
        out_specs=pl.BlockSpec((tm, tn), lambda j, i: (i, j)),
        out_shape=jax.ShapeDtypeStruct((n, nc), F32), name=name,
        compiler_params=_cp(("parallel", "parallel")))(h, gamma, w)


def norm_matmul_tn(h, gamma, dy, name):
    n, d = h.shape
    nc = dy.shape[1]
    tm = _pick(n, 512, 8)
    tn = _pick(nc, 1536)

    def body(h_ref, g_ref, dy_ref, o_ref):
        @pl.when(pl.program_id(1) == 0)
        def _():
            o_ref[...] = jnp.zeros_like(o_ref)
        xn = _rms(h_ref[...], g_ref[...])
        o_ref[...] += _dot_tn(xn.astype(BF16), dy_ref[...].astype(BF16))

    return pl.pallas_call(
        body, grid=(nc // tn, n // tm),
        in_specs=[pl.BlockSpec((tm, d), lambda j, i: (i, 0)),
                  pl.BlockSpec((1, d), lambda j, i: (0, 0)),
                  pl.BlockSpec((tm, tn), lambda j, i: (i, j))],
        out_specs=pl.BlockSpec((d, tn), lambda j, i: (0, j)),
        out_shape=jax.ShapeDtypeStruct((d, nc), F32), name=name,
        compiler_params=_cp(("parallel", "arbitrary")))(h, gamma, dy)


def matmul_tn(a, dy, name):
    n, ka = a.shape
    nc = dy.shape[1]
    tm = _pick(n, 512, 8)
    tk = _pick(ka, 1536)
    tn = _pick(nc, 1024)

    def body(a_ref, dy_ref, o_ref):
        @pl.when(pl.program_id(2) == 0)
        def _():
            o_ref[...] = jnp.zeros_like(o_ref)
        o_ref[...] += _dot_tn(a_ref[...].astype(BF16), dy_ref[...].astype(BF16))

    return pl.pallas_call(
        body, grid=(ka // tk, nc // tn, n // tm),
        in_specs=[pl.BlockSpec((tm, tk), lambda k, j, i: (i, k)),
                  pl.BlockSpec((tm, tn), lambda k, j, i: (i, j))],
        out_specs=pl.BlockSpec((tk, tn), lambda k, j, i: (k, j)),
        out_shape=jax.ShapeDtypeStruct((ka, nc), F32), name=name,
        compiler_params=_cp(("parallel", "parallel", "arbitrary")))(a, dy)


def matmul_res(a, w, h, name):
    n, k = a.shape
    d = w.shape[1]
    tm = _pick(n, 512, 8)

    def body(a_ref, w_ref, h_ref, o_ref):
        o_ref[...] = h_ref[...] + _dot(a_ref[...].astype(BF16), w_ref[...])

    return pl.pallas_call(
        body, grid=(n // tm,),
        in_specs=[pl.BlockSpec((tm, k), lambda i: (i, 0)),
                  _resident((k, d), lambda i: (0, 0)),
                  pl.BlockSpec((tm, d), lambda i: (i, 0))],
        out_specs=pl.BlockSpec((tm, d), lambda i: (i, 0)),
        out_shape=jax.ShapeDtypeStruct((n, d), F32), name=name,
        compiler_params=_cp(("parallel",)))(a, w, h)


def matmul_nt(dy, w, name):
    n, k = dy.shape
    d = w.shape[0]
    tm = _pick(n, 512, 8)

    def body(dy_ref, w_ref, o_ref):
        o_ref[...] = _dot_nt(dy_ref[...].astype(BF16), w_ref[...])

    return pl.pallas_call(
        body, grid=(n // tm,),
        in_specs=[pl.BlockSpec((tm, k), lambda i: (i, 0)),
                  _resident((d, k), lambda i: (0, 0))],
        out_specs=pl.BlockSpec((tm, d), lambda i: (i, 0)),
        out_shape=jax.ShapeDtypeStruct((n, d), F32), name=name,
        compiler_params=_cp(("parallel",)))(dy, w)


def dx_norm_bwd(dout, h, gamma, pieces, name):
    n, d = h.shape
    tm = _pick(n, 256, 8)
    np_ = len(pieces)

    def body(*refs):
        dout_ref, h_ref, g_ref = refs[:3]
        dys = refs[3:3 + np_]
        ws = refs[3 + np_:3 + 2 * np_]
        dh_ref, dg_ref = refs[3 + 2 * np_:]
        dxn = _dot_nt(dys[0][...].astype(BF16), ws[0][...])
        for p in range(1, np_):
            dxn = dxn + _dot_nt(dys[p][...].astype(BF16), ws[p][...])
        dx, dg = _rms_bwd(h_ref[...], g_ref[...], dxn)
        dh_ref[...] = dout_ref[...] + dx

        @pl.when(pl.program_id(0) == 0)
        def _():
            dg_ref[...] = jnp.zeros_like(dg_ref)
        dg_ref[...] += dg

    in_specs = [pl.BlockSpec((tm, d), lambda i: (i, 0)),
                pl.BlockSpec((tm, d), lambda i: (i, 0)),
                pl.BlockSpec((1, d), lambda i: (0, 0))]
    in_specs += [pl.BlockSpec((tm, dy.shape[1]), lambda i: (i, 0)) for dy, _ in pieces]
    in_specs += [_resident(w.shape, lambda i: (0, 0)) for _, w in pieces]
    return pl.pallas_call(
        body, grid=(n // tm,), in_specs=in_specs,
        out_specs=[pl.BlockSpec((tm, d), lambda i: (i, 0)), pl.BlockSpec((1, d), lambda i: (0, 0))],
        out_shape=[jax.ShapeDtypeStruct((n, d), F32), jax.ShapeDtypeStruct((1, d), F32)], name=name,
        compiler_params=_cp(("arbitrary",)))(dout, h, gamma, *[p[0] for p in pieces], *[p[1] for p in pieces])


def loss_head(h, gamma, target, name="loss_head"):
    n, d = h.shape
    tm = _pick(n, 512, 8)

    def body(h_ref, g_ref, t_ref, loss_ref, dh_ref, dg_ref):
        @pl.when(pl.program_id(0) == 0)
        def _():
            loss_ref[...] = jnp.zeros_like(loss_ref)
            dg_ref[...] = jnp.zeros_like(dg_ref)
        x = h_ref[...]
        g = g_ref[...]
        e = _rms(x, g) - t_ref[...]
        part = jnp.sum(jnp.sum(e * e, axis=-1, keepdims=True), axis=0, keepdims=True) * (0.5 / d)
        loss_ref[...] += jnp.broadcast_to(part, loss_ref.shape)
        dx, dg = _rms_bwd(x, g, e * (1.0 / d))
        dh_ref[...] = dx
        dg_ref[...] += dg

    return pl.pallas_call(
        body, grid=(n // tm,),
        in_specs=[pl.BlockSpec((tm, d), lambda i: (i, 0)), pl.BlockSpec((1, d), lambda i: (0, 0)),
                  pl.BlockSpec((tm, d), lambda i: (i, 0))],
        out_specs=[pl.BlockSpec((8, LANES), lambda i: (0, 0)), pl.BlockSpec((tm, d), lambda i: (i, 0)),
                   pl.BlockSpec((1, d), lambda i: (0, 0))],
        out_shape=[jax.ShapeDtypeStruct((8, LANES), F32), jax.ShapeDtypeStruct((n, d), F32),
                   jax.ShapeDtypeStruct((1, d), F32)], name=name,
        compiler_params=_cp(("arbitrary",)))(h, gamma, target)


def _prev_halo_map(t):
    return lambda i: (jnp.maximum(i * (t // HALO) - 1, 0), 0)


def _next_halo_map(t, n):
    return lambda i: (jnp.minimum((i + 1) * (t // HALO), n // HALO - 1), 0)


def _causal_conv(xs, w_ref, width, t):
    acc = w_ref[0:1, :] * xs[pl.ds(HALO - (width - 1), t), :]
    for k in range(1, width):
        acc = acc + w_ref[k:k + 1, :] * xs[pl.ds(HALO - (width - 1) + k, t), :]
    return acc


def ffn_act_down(up, conv_w, conv_b, w_down, h, name):
    n, c2 = up.shape
    ff = c2 // 2
    d = h.shape[1]
    t = _pick(n, FFN_TILE, 8)

    def body(up_ref, halo_ref, cw_ref, cb_ref, wd_ref, h_ref, o_ref, act_ref, xs):
        i = pl.program_id(0)
        xs[0:HALO, :] = jnp.where(i > 0, halo_ref[...], 0.0)
        xs[HALO:, :] = up_ref[...]
        hc = _causal_conv(xs, cw_ref, F_CONV, t) + cb_ref[...]
        gate = hc[:, :ff]
        val = hc[:, ff:]
        act = (gate * _sig(gate) * val).astype(BF16)
        act_ref[...] = act
        o_ref[...] = h_ref[...] + _dot(act, wd_ref[...])

    return pl.pallas_call(
        body, grid=(n // t,),
        in_specs=[pl.BlockSpec((t, c2), lambda i: (i, 0)),
                  pl.BlockSpec((HALO, c2), _prev_halo_map(t)),
                  pl.BlockSpec((F_CONV, c2), lambda i: (0, 0)),
                  pl.BlockSpec((1, c2), lambda i: (0, 0)),
                  _resident((ff, d), lambda i: (0, 0)),
                  pl.BlockSpec((t, d), lambda i: (i, 0))],
        out_specs=[pl.BlockSpec((t, d), lambda i: (i, 0)), pl.BlockSpec((t, ff), lambda i: (i, 0))],
        out_shape=[jax.ShapeDtypeStruct((n, d), F32), jax.ShapeDtypeStruct((n, ff), BF16)],
        scratch_shapes=[pltpu.VMEM((t + HALO, c2), F32)], name=name,
        compiler_params=_cp(("parallel",)))(up, up, conv_w, conv_b, w_down, h)


def ffn_bwd_act(dout, up, conv_w, conv_b, w_down, name):
    n, c2 = up.shape
    ff = c2 // 2
    d = dout.shape[1]
    t = _pick(n, FFN_TILE, 8)

    def body(dout_ref, up_ref, halo_ref, cw_ref, cb_ref, wd_ref, dhc_ref, dcb_ref, xs):
        i = pl.program_id(0)
        xs[0:HALO, :] = jnp.where(i > 0, halo_ref[...], 0.0)
        xs[HALO:, :] = up_ref[...]
        hc = _causal_conv(xs, cw_ref, F_CONV, t) + cb_ref[...]
        gate = hc[:, :ff]
        val = hc[:, ff:]
        sg = _sig(gate)
        dact = _dot_nt(dout_ref[...].astype(BF16), wd_ref[...])
        dgate = dact * val * (sg * (1.0 + gate * (1.0 - sg)))
        dval = dact * gate * sg
        dhc_ref[:, :ff] = dgate
        dhc_ref[:, ff:] = dval

        @pl.when(i == 0)
        def _():
            dcb_ref[...] = jnp.zeros_like(dcb_ref)
        dcb_ref[:, :ff] += jnp.sum(dgate, axis=0, keepdims=True)
        dcb_ref[:, ff:] += jnp.sum(dval, axis=0, keepdims=True)

    return pl.pallas_call(
        body, grid=(n // t,),
        in_specs=[pl.BlockSpec((t, d), lambda i: (i, 0)),
                  pl.BlockSpec((t, c2), lambda i: (i, 0)),
                  pl.BlockSpec((HALO, c2), _prev_halo_map(t)),
                  pl.BlockSpec((F_CONV, c2), lambda i: (0, 0)),
                  pl.BlockSpec((1, c2), lambda i: (0, 0)),
                  _resident((ff, d), lambda i: (0, 0))],
        out_specs=[pl.BlockSpec((t, c2), lambda i: (i, 0)), pl.BlockSpec((1, c2), lambda i: (0, 0))],
        out_shape=[jax.ShapeDtypeStruct((n, c2), F32), jax.ShapeDtypeStruct((1, c2), F32)],
        scratch_shapes=[pltpu.VMEM((t + HALO, c2), F32)], name=name,
        compiler_params=_cp(("arbitrary",)))(dout, up, up, conv_w, conv_b, w_down)


def conv_bwd_tail(dy_ref, dnext_ref, x_ref, xprev_ref, cw_ref, dcw_ref, ds, xs, width, t, i, last):
    ds[0:t, :] = dy_ref[...]
    ds[t:, :] = jnp.where(i < last, dnext_ref[...], 0.0)
    xs[0:HALO, :] = jnp.where(i > 0, xprev_ref[...], 0.0)
    xs[HALO:, :] = x_ref[...]
    dy = dy_ref[...]

    @pl.when(i == 0)
    def _():
        dcw_ref[...] = jnp.zeros_like(dcw_ref)
    dx = None
    for k in range(width):
        term = cw_ref[k:k + 1, :] * ds[pl.ds(width - 1 - k, t), :]
        dx = term if dx is None else dx + term
        dcw_ref[k:k + 1, :] += jnp.sum(dy * xs[pl.ds(HALO - (width - 1) + k, t), :], axis=0, keepdims=True)
    return dx


def ffn_bwd_up(dhc, up, conv_w, w_up, h, dout, gamma, name):
    n, c2 = up.shape
    d = h.shape[1]
    t = _pick(n, FFN_TILE, 8)
    last = n // t - 1

    def body(dhc_ref, dnext_ref, up_ref, halo_ref, cw_ref, wu_ref, h_ref, dout_ref, g_ref,
             dh_ref, dup_ref, dcw_ref, dg_ref, ds, xs):
        i = pl.program_id(0)
        dup = conv_bwd_tail(dhc_ref, dnext_ref, up_ref, halo_ref, cw_ref, dcw_ref, ds, xs, F_CONV, t, i, last)
        dupb = dup.astype(BF16)
        dup_ref[...] = dupb
        dxn = _dot_nt(dupb, wu_ref[...])
        dx, dg = _rms_bwd(h_ref[...], g_ref[...], dxn)
        dh_ref[...] = dout_ref[...] + dx

        @pl.when(i == 0)
        def _():
            dg_ref[...] = jnp.zeros_like(dg_ref)
        dg_ref[...] += dg

    return pl.pallas_call(
        body, grid=(n // t,),
        in_specs=[pl.BlockSpec((t, c2), lambda i: (i, 0)),
                  pl.BlockSpec((HALO, c2), _next_halo_map(t, n)),
                  pl.BlockSpec((t, c2), lambda i: (i, 0)),
                  pl.BlockSpec((HALO, c2), _prev_halo_map(t)),
                  pl.BlockSpec((F_CONV, c2), lambda i: (0, 0)),
                  _resident((d, c2), lambda i: (0, 0)),
                  pl.BlockSpec((t, d), lambda i: (i, 0)),
                  pl.BlockSpec((t, d), lambda i: (i, 0)),
                  pl.BlockSpec((1, d), lambda i: (0, 0))],
        out_specs=[pl.BlockSpec((t, d), lambda i: (i, 0)), pl.BlockSpec((t, c2), lambda i: (i, 0)),
                   pl.BlockSpec((F_CONV, c2), lambda i: (0, 0)), pl.BlockSpec((1, d), lambda i: (0, 0))],
        out_shape=[jax.ShapeDtypeStruct((n, d), F32), jax.ShapeDtypeStruct((n, c2), BF16),
                   jax.ShapeDtypeStruct((F_CONV, c2), F32), jax.ShapeDtypeStruct((1, d), F32)],
        scratch_shapes=[pltpu.VMEM((t + HALO, c2), F32), pltpu.VMEM((t + HALO, c2), F32)], name=name,
        compiler_params=_cp(("arbitrary",)))(dhc, dhc, up, up, conv_w, w_up, h, dout, gamma)


def _gdn_head(uq, uk, uv, pba, alog, dtb, head, n_heads):
    lane = lax.broadcasted_iota(jnp.int32, pba.shape, 1)
    sq = uq * _sig(uq)
    q = sq * lax.rsqrt(jnp.sum(sq * sq, axis=-1, keepdims=True) + EPS) * (A_HEAD ** -0.5)
    sk = uk * _sig(uk)
    k = sk * lax.rsqrt(jnp.sum(sk * sk, axis=-1, keepdims=True) + EPS)
    v = uv * _sig(uv)
    beta = jnp.sum(jnp.where(lane == head, _sig(pba), 0.0), axis=-1, keepdims=True)
    g_all = -jnp.exp(alog) * _softplus(pba + dtb)
    g = jnp.sum(jnp.where(lane == n_heads + head, g_all, 0.0), axis=-1, keepdims=True)
    return q, k, v, jnp.broadcast_to(beta, uq.shape), jnp.broadcast_to(g, uq.shape)


def gdn_prep(proj, conv_w, alog, dtb, n_heads, name):
    n = proj.shape[0]
    qk = n_heads * A_HEAD
    cw = 3 * qk
    ba_blk = (cw + qk) // LANES
    t = _pick(n, 256, 8)

    def body(x_ref, halo_ref, pba_ref, cw_ref, al_ref, dt_ref, q_ref, k_ref, v_ref, b_ref, g_ref, xs):
        i = pl.program_id(0)
        xs[0:HALO, :] = jnp.where(i > 0, halo_ref[...], 0.0)
        xs[HALO:, :] = x_ref[...]
        u = _causal_conv(xs, cw_ref, A_CONV, t)
        pba = pba_ref[...]
        for hd in range(n_heads):
            s0 = slice(hd * A_HEAD, (hd + 1) * A_HEAD)
            s1 = slice(qk + hd * A_HEAD, qk + (hd + 1) * A_HEAD)
            s2 = slice(2 * qk + hd * A_HEAD, 2 * qk + (hd + 1) * A_HEAD)
            q, k, v, bb, gb = _gdn_head(u[:, s0], u[:, s1], u[:, s2], pba, al_ref[...], dt_ref[...], hd, n_heads)
            q_ref[:, s0] = q
            k_ref[:, s0] = k
            v_ref[:, s0] = v
            b_ref[:, s0] = bb
            g_ref[:, s0] = gb

    out = jax.ShapeDtypeStruct((n, qk), F32)
    return pl.pallas_call(
        body, grid=(n // t,),
        in_specs=[pl.BlockSpec((t, cw), lambda i: (i, 0)),
                  pl.BlockSpec((HALO, cw), _prev_halo_map(t)),
                  pl.BlockSpec((t, LANES), lambda i: (i, ba_blk)),
                  pl.BlockSpec((A_CONV, cw), lambda i: (0, 0)),
                  pl.BlockSpec((1, LANES), lambda i: (0, 0)),
                  pl.BlockSpec((1, LANES), lambda i: (0, 0))],
        out_specs=[pl.BlockSpec((t, qk), lambda i: (i, 0))] * 5,
        out_shape=[out] * 5,
        scratch_shapes=[pltpu.VMEM((t + HALO, cw), F32)], name=name,
        compiler_params=_cp(("parallel",)))(proj, proj, proj, conv_w, alog, dtb)


def gdn_prep_bwd(proj, conv_w, alog, dtb, dq, dk, dv, dbb, dgb, n_heads, name):
    n = proj.shape[0]
    qk = n_heads * A_HEAD
    cw = 3 * qk
    ba_blk = (cw + qk) // LANES
    t = _pick(n, 256, 8)

    def body(x_ref, halo_ref, pba_ref, cw_ref, al_ref, dt_ref, dq_ref, dk_ref, dv_ref, dbb_ref, dgb_ref,
             du_ref, dba_ref, dal_ref, ddt_ref, xs):
        i = pl.program_id(0)
        xs[0:HALO, :] = jnp.where(i > 0, halo_ref[...], 0.0)
        xs[HALO:, :] = x_ref[...]
        u = _causal_conv(xs, cw_ref, A_CONV, t)
        pba = pba_ref[...]
        lane0 = lax.broadcasted_iota(jnp.int32, (t, A_HEAD), 1) == 0
        dba = jnp.zeros((t, LANES), F32)
        dal = jnp.zeros((1, LANES), F32)
        ddt = jnp.zeros((1, LANES), F32)
        for hd in range(n_heads):
            s0 = slice(hd * A_HEAD, (hd + 1) * A_HEAD)
            s1 = slice(qk + hd * A_HEAD, qk + (hd + 1) * A_HEAD)
            s2 = slice(2 * qk + hd * A_HEAD, 2 * qk + (hd + 1) * A_HEAD)
            fn = functools.partial(_gdn_head, head=hd, n_heads=n_heads)
            _, vjp = jax.vjp(fn, u[:, s0], u[:, s1], u[:, s2], pba, al_ref[...], dt_ref[...])
            cts = (dq_ref[:, s0], dk_ref[:, s0], dv_ref[:, s0],
                   jnp.where(lane0, dbb_ref[:, s0], 0.0), jnp.where(lane0, dgb_ref[:, s0], 0.0))
            duq, duk, duv, dpba, da, dd = vjp(cts)
            du_ref[:, s0] = duq
            du_ref[:, s1] = duk
            du_ref[:, s2] = duv
            dba = dba + dpba
            dal = dal + da
            ddt = ddt + dd
        dba_ref[...] = dba

        @pl.when(i == 0)
        def _():
            dal_ref[...] = jnp.zeros_like(dal_ref)
            ddt_ref[...] = jnp.zeros_like(ddt_ref)
        dal_ref[...] += dal
        ddt_ref[...] += ddt

    tok = pl.BlockSpec((t, qk), lambda i: (i, 0))
    row = pl.BlockSpec((1, LANES), lambda i: (0, 0))
    return pl.pallas_call(
        body, grid=(n // t,),
        in_specs=[pl.BlockSpec((t, cw), lambda i: (i, 0)),
                  pl.BlockSpec((HALO, cw), _prev_halo_map(t)),
                  pl.BlockSpec((t, LANES), lambda i: (i, ba_blk)),
                  pl.BlockSpec((A_CONV, cw), lambda i: (0, 0)), row, row,
                  tok, tok, tok, tok, tok],
        out_specs=[pl.BlockSpec((t, cw), lambda i: (i, 0)), pl.BlockSpec((t, LANES), lambda i: (i, 0)), row, row],
        out_shape=[jax.ShapeDtypeStruct((n, cw), F32), jax.ShapeDtypeStruct((n, LANES), F32),
                   jax.ShapeDtypeStruct((1, LANES), F32), jax.ShapeDtypeStruct((1, LANES), F32)],
        scratch_shapes=[pltpu.VMEM((t + HALO, cw), F32)], name=name,
        compiler_params=_cp(("arbitrary",)))(proj, proj, proj, conv_w, alog, dtb, dq, dk, dv, dbb, dgb)


def conv_bwd(du, x, conv_w, width, name):
    n, cw = du.shape
    t = _pick(n, 256, 8)
    last = n // t - 1

    def body(du_ref, dnext_ref, x_ref, halo_ref, cw_ref, dx_ref, dcw_ref, ds, xs):
        i = pl.program_id(0)
        dx_ref[...] = conv_bwd_tail(du_ref, dnext_ref, x_ref, halo_ref, cw_ref, dcw_ref, ds, xs, width, t, i, last)

    return pl.pallas_call(
        body, grid=(n // t,),
        in_specs=[pl.BlockSpec((t, cw), lambda i: (i, 0)),
                  pl.BlockSpec((HALO, cw), _next_halo_map(t, n)),
                  pl.BlockSpec((t, cw), lambda i: (i, 0)),
                  pl.BlockSpec((HALO, cw), _prev_halo_map(t)),
                  pl.BlockSpec((width, cw), lambda i: (0, 0))],
        out_specs=[pl.BlockSpec((t, cw), lambda i: (i, 0)), pl.BlockSpec((width, cw), lambda i: (0, 0))],
        out_shape=[jax.ShapeDtypeStruct((n, cw), F32), jax.ShapeDtypeStruct((width, cw), F32)],
        scratch_shapes=[pltpu.VMEM((t + HALO, cw), F32), pltpu.VMEM((t + HALO, cw), F32)], name=name,
        compiler_params=_cp(("arbitrary",)))(du, du, x, x, conv_w)


def _gdn_chunk(state, q, k, v, bb, gb):
    c = CHUNK
    ri = lax.broadcasted_iota(jnp.int32, (c, c), 0)
    ci = lax.broadcasted_iota(jnp.int32, (c, c), 1)
    causal = ri >= ci
    strict = ri > ci
    tri = jnp.where(causal, 1.0, 0.0)
    gc = _hdot(tri, gb)
    gc_col = gc[:, :c]
    gc_row = gc.T[:c, :]
    decay = jnp.where(causal, jnp.exp(jnp.where(causal, gc_col - gc_row, 0.0)), 0.0)
    kb = k * bb
    m = jnp.where(strict, _hdot_nt(kb, k) * decay, 0.0)
    eye = jnp.where(ri == ci, 1.0, 0.0)
    tinv = eye - m
    pw = _hdot(m, m)
    for it in range(5):
        tinv = tinv + _hdot(tinv, pw)
        if it < 4:
            pw = _hdot(pw, pw)
    egc = jnp.exp(gc)
    u = _hdot(tinv, v * bb)
    w = _hdot(tinv, kb * egc)
    attn = _hdot_nt(q, k) * decay
    glast = jnp.sum(gb, axis=0, keepdims=True)
    v_new = u - _hdot(w, state)
    o = _hdot(q * egc, state) + _hdot(attn, v_new)
    new_state = state * jnp.exp(glast) + _hdot_tn(k * jnp.exp(glast - gc), v_new)
    return o, new_state


def gdn_fwd(q, k, v, bb, gb, n_heads, name):
    n = q.shape[0]
    nc = n // CHUNK
    cb = min(8, nc)
    rows = cb * CHUNK

    def body(q_ref, k_ref, v_ref, b_ref, g_ref, o_ref, st_ref, s_scr):
        @pl.when(pl.program_id(1) == 0)
        def _():
            s_scr[...] = jnp.zeros_like(s_scr)

        def step(c, state):
            sl = pl.ds(pl.multiple_of(c * CHUNK, CHUNK), CHUNK)
            st_ref[0, pl.ds(c, 1)] = state[None]
            o, state = _gdn_chunk(state, q_ref[sl, :], k_ref[sl, :], v_ref[sl, :], b_ref[sl, :], g_ref[sl, :])
            o_ref[sl, :] = o
            return state

        s_scr[...] = lax.fori_loop(0, cb, step, s_scr[...])

    tok = pl.BlockSpec((rows, A_HEAD), lambda hd, j: (j, hd))
    return pl.pallas_call(
        body, grid=(n_heads, nc // cb),
        in_specs=[tok] * 5,
        out_specs=[tok, pl.BlockSpec((1, cb, A_HEAD, A_HEAD), lambda hd, j: (hd, j, 0, 0))],
        out_shape=[jax.ShapeDtypeStruct(q.shape, F32), jax.ShapeDtypeStruct((n_heads, nc, A_HEAD, A_HEAD), F32)],
        scratch_shapes=[pltpu.VMEM((A_HEAD, A_HEAD), F32)], name=name,
        compiler_params=_cp(("parallel", "arbitrary")))(q, k, v, bb, gb)


def gdn_bwd(q, k, v, bb, gb, states, do, n_heads, name):
    n = q.shape[0]
    nc = n // CHUNK
    cb = min(8, nc)
    rows = cb * CHUNK
    nblk = nc // cb

    def body(q_ref, k_ref, v_ref, b_ref, g_ref, st_ref, do_ref, dq_ref, dk_ref, dv_ref, db_ref, dg_ref, ds_scr):
        @pl.when(pl.program_id(1) == 0)
        def _():
            ds_scr[...] = jnp.zeros_like(ds_scr)

        def step(s, dstate):
            c = cb - 1 - s
            sl = pl.ds(pl.multiple_of(c * CHUNK, CHUNK), CHUNK)
            state = st_ref[0, pl.ds(c, 1)][0]
            _, vjp = jax.vjp(_gdn_chunk, state, q_ref[sl, :], k_ref[sl, :], v_ref[sl, :], b_ref[sl, :], g_ref[sl, :])
            dstate, dq, dk, dv, dbb, dgb = vjp((do_ref[sl, :], dstate))
            dq_ref[sl, :] = dq
            dk_ref[sl, :] = dk
            dv_ref[sl, :] = dv
            db_ref[sl, :] = jnp.broadcast_to(jnp.sum(dbb, axis=-1, keepdims=True), dbb.shape)
            dg_ref[sl, :] = jnp.broadcast_to(jnp.sum(dgb, axis=-1, keepdims=True), dgb.shape)
            return dstate

        ds_scr[...] = lax.fori_loop(0, cb, step, ds_scr[...])

    tok = pl.BlockSpec((rows, A_HEAD), lambda hd, j: (nblk - 1 - j, hd))
    out = jax.ShapeDtypeStruct(q.shape, F32)
    return pl.pallas_call(
        body, grid=(n_heads, nblk),
        in_specs=[tok] * 5 + [pl.BlockSpec((1, cb, A_HEAD, A_HEAD), lambda hd, j: (hd, nblk - 1 - j, 0, 0)), tok],
        out_specs=[tok] * 5, out_shape=[out] * 5,
        scratch_shapes=[pltpu.VMEM((A_HEAD, A_HEAD), F32)], name=name,
        compiler_params=_cp(("parallel", "arbitrary")))(q, k, v, bb, gb, states, do)


def _gdn_gate(oh, zh, w):
    r = lax.rsqrt(jnp.mean(oh * oh, axis=-1, keepdims=True) + EPS)
    return oh * r * w * (zh * _sig(zh))


def gdn_out(o, proj, out_norm, w_out, h, n_heads, name):
    n, vw = o.shape
    d = h.shape[1]
    z_blk = 3 * vw // vw
    t = _pick(n, 512, 8)

    def body(o_ref, z_ref, w_ref, wo_ref, h_ref, out_ref, y_ref):
        for hd in range(n_heads):
            s0 = slice(hd * A_HEAD, (hd + 1) * A_HEAD)
            y_ref[:, s0] = _gdn_gate(o_ref[:, s0], z_ref[:, s0], w_ref[...]).astype(BF16)
        out_ref[...] = h_ref[...] + _dot(y_ref[...], wo_ref[...])

    return pl.pallas_call(
        body, grid=(n // t,),
        in_specs=[pl.BlockSpec((t, vw), lambda i: (i, 0)),
                  pl.BlockSpec((t, vw), lambda i: (i, z_blk)),
                  pl.BlockSpec((1, A_HEAD), lambda i: (0, 0)),
                  _resident((vw, d), lambda i: (0, 0)),
                  pl.BlockSpec((t, d), lambda i: (i, 0))],
        out_specs=[pl.BlockSpec((t, d), lambda i: (i, 0)), pl.BlockSpec((t, vw), lambda i: (i, 0))],
        out_shape=[jax.ShapeDtypeStruct((n, d), F32), jax.ShapeDtypeStruct((n, vw), BF16)], name=name,
        compiler_params=_cp(("parallel",)))(o, proj, out_norm, w_out, h)


def gdn_out_bwd(dout, o, proj, out_norm, w_out, n_heads, name):
    n, vw = o.shape
    d = dout.shape[1]
    z_blk = 3
    t = _pick(n, 512, 8)

    def body(dout_ref, o_ref, z_ref, w_ref, wo_ref, do_ref, dz_ref, dw_ref):
        dy = _dot_nt(dout_ref[...].astype(BF16), wo_ref[...])
        dw = jnp.zeros((1, A_HEAD), F32)
        for hd in range(n_heads):
            s0 = slice(hd * A_HEAD, (hd + 1) * A_HEAD)
            _, vjp = jax.vjp(_gdn_gate, o_ref[:, s0], z_ref[:, s0], w_ref[...])
            doh, dzh, dwh = vjp(dy[:, s0])
            do_ref[:, s0] = doh
            dz_ref[:, s0] = dzh
            dw = dw + dwh

        @pl.when(pl.program_id(0) == 0)
        def _():
            dw_ref[...] = jnp.zeros_like(dw_ref)
        dw_ref[...] += dw

    tok = pl.BlockSpec((t, vw), lambda i: (i, 0))
    return pl.pallas_call(
        body, grid=(n // t,),
        in_specs=[pl.BlockSpec((t, d), lambda i: (i, 0)), tok,
                  pl.BlockSpec((t, vw), lambda i: (i, z_blk)),
                  pl.BlockSpec((1, A_HEAD), lambda i: (0, 0)),
                  _resident((vw, d), lambda i: (0, 0))],
        out_specs=[tok, tok, pl.BlockSpec((1, A_HEAD), lambda i: (0, 0))],
        out_shape=[jax.ShapeDtypeStruct((n, vw), F32), jax.ShapeDtypeStruct((n, vw), F32),
                   jax.ShapeDtypeStruct((1, A_HEAD), F32)], name=name,
        compiler_params=_cp(("arbitrary",)))(dout, o, proj, out_norm, w_out)


def _bias_onehot(i, clip, tbl_pad):
    m = lax.broadcasted_iota(jnp.int32, (BAND_PAD, tbl_pad), 0)
    r = lax.broadcasted_iota(jnp.int32, (BAND_PAD, tbl_pad), 1)
    idx = jnp.clip(i + LEFT_CHUNKS * CHUNK - m, -clip, clip) + clip
    return jnp.where((r == idx) & (m < BAND), 1.0, 0.0)


def bias_expand(tbl, clip, name):
    nh, tp = tbl.shape

    def body(t_ref, o_ref):
        oh = _bias_onehot(pl.program_id(0), clip, tp)
        o_ref[0] = _hdot_nt(t_ref[...], oh)

    return pl.pallas_call(
        body, grid=(CHUNK,),
        in_specs=[pl.BlockSpec((nh, tp), lambda i: (0, 0))],
        out_specs=pl.BlockSpec((1, nh, BAND_PAD), lambda i: (i, 0, 0)),
        out_shape=jax.ShapeDtypeStruct((CHUNK, nh, BAND_PAD), F32), name=name,
        compiler_params=_cp(("parallel",)))(tbl)


def bias_expand_bwd(dbias, clip, tp, name):
    _, nh, _ = dbias.shape

    def body(d_ref, o_ref):
        @pl.when(pl.program_id(0) == 0)
        def _():
            o_ref[...] = jnp.zeros_like(o_ref)
        oh = _bias_onehot(pl.program_id(0), clip, tp)
        o_ref[...] += _hdot(d_ref[0], oh)

    return pl.pallas_call(
        body, grid=(CHUNK,),
        in_specs=[pl.BlockSpec((1, nh, BAND_PAD), lambda i: (i, 0, 0))],
        out_specs=pl.BlockSpec((nh, tp), lambda i: (0, 0)),
        out_shape=jax.ShapeDtypeStruct((nh, tp), F32), name=name,
        compiler_params=_cp(("arbitrary",)))(dbias)


ATT_TILE = LEFT_CHUNKS * CHUNK


def _att_probs(qm, kb, bias, n_chunk):
    s = _dot_nt(qm, kb) * (B_HEAD ** -0.5) + bias
    slot = lax.broadcasted_iota(jnp.int32, s.shape, 1)
    valid = (slot >= (LEFT_CHUNKS - n_chunk) * CHUNK) & (slot < BAND)
    s = jnp.where(valid, s, NEG_INF)
    p = jnp.exp(s - jnp.max(s, axis=-1, keepdims=True))
    return p / jnp.sum(p, axis=-1, keepdims=True)


def _att_specs(n_pairs):
    prev = lambda p, i: (jnp.maximum(i - 1, 0), p)
    cur = lambda p, i: (i, p)
    prev_v = lambda p, i: (jnp.maximum(i - 1, 0), n_pairs + p)
    cur_v = lambda p, i: (i, n_pairs + p)
    blk = (ATT_TILE, LANES)
    return [pl.BlockSpec(blk, prev), pl.BlockSpec(blk, cur), pl.BlockSpec(blk, prev_v), pl.BlockSpec(blk, cur_v)]


def _att_fill(kbuf, vbuf, kp_ref, kc_ref, vp_ref, vc_ref):
    t = ATT_TILE
    kbuf[0:t, :] = kp_ref[...].astype(BF16)
    kbuf[t:2 * t, :] = kc_ref[...].astype(BF16)
    kbuf[2 * t:, :] = jnp.zeros((CHUNK, LANES), BF16)
    vbuf[0:t, :] = vp_ref[...].astype(BF16)
    vbuf[t:2 * t, :] = vc_ref[...].astype(BF16)
    vbuf[2 * t:, :] = jnp.zeros((CHUNK, LANES), BF16)


def attn_fwd(qp, kv, bias, name):
    n, bw = qp.shape
    n_pairs = bw // LANES
    t = ATT_TILE
    cpt = t // CHUNK

    def body(q_ref, kp_ref, kc_ref, vp_ref, vc_ref, b_ref, o_ref, kbuf, vbuf):
        i = pl.program_id(1)
        _att_fill(kbuf, vbuf, kp_ref, kc_ref, vp_ref, vc_ref)
        lane = lax.broadcasted_iota(jnp.int32, (CHUNK, LANES), 1)
        for c in range(cpt):
            qc = q_ref[c * CHUNK:(c + 1) * CHUNK, :]
            kb = kbuf[c * CHUNK:c * CHUNK + BAND_PAD, :]
            vb = vbuf[c * CHUNK:c * CHUNK + BAND_PAD, :]
            outs = []
            for hh in range(2):
                mine = (lane >= hh * B_HEAD) & (lane < (hh + 1) * B_HEAD)
                qm = jnp.where(mine, qc, 0.0).astype(BF16)
                p = _att_probs(qm, kb, b_ref[0, :, hh * BAND_PAD:(hh + 1) * BAND_PAD], i * cpt + c)
                outs.append(_dot(p.astype(BF16), vb))
            o_ref[c * CHUNK:(c + 1) * CHUNK, :] = jnp.where(lane < B_HEAD, outs[0], outs[1])

    return pl.pallas_call(
        body, grid=(n_pairs, n // t),
        in_specs=[pl.BlockSpec((t, LANES), lambda p, i: (i, p))] + _att_specs(n_pairs)
        + [pl.BlockSpec((1, CHUNK, 2 * BAND_PAD), lambda p, i: (p, 0, 0))],
        out_specs=pl.BlockSpec((t, LANES), lambda p, i: (i, p)),
        out_shape=jax.ShapeDtypeStruct((n, bw), F32),
        scratch_shapes=[pltpu.VMEM((2 * t + CHUNK, LANES), BF16), pltpu.VMEM((2 * t + CHUNK, LANES), BF16)],
        name=name, compiler_params=_cp(("parallel", "parallel")))(qp, kv, kv, kv, kv, bias)


def attn_bwd(qp, kv, bias, d_o, dk_in, dv_in, name):
    n, bw = qp.shape
    n_pairs = bw // LANES
    t = ATT_TILE
    cpt = t // CHUNK
    nt = n // t
    have_in = dk_in is not None
    scale = B_HEAD ** -0.5

    def body(*refs):
        q_ref, kp_ref, kc_ref, vp_ref, vc_ref, b_ref, do_ref = refs[:7]
        pos = 7
        if have_in:
            dki_ref, dvi_ref = refs[7:9]
            pos = 9
        dq_ref, dk_ref, dv_ref, db_ref, kbuf, vbuf, dkacc, dvacc = refs[pos:]
        j = pl.program_id(1)
        i = nt - 1 - j
        _att_fill(kbuf, vbuf, kp_ref, kc_ref, vp_ref, vc_ref)

        @pl.when(j == 0)
        def _():
            dkacc[...] = jnp.zeros_like(dkacc)
            dvacc[...] = jnp.zeros_like(dvacc)
            db_ref[...] = jnp.zeros_like(db_ref)

        @pl.when(j > 0)
        def _():
            dkacc[t:2 * t, :] = dkacc[0:t, :]
            dvacc[t:2 * t, :] = dvacc[0:t, :]
            dkacc[0:t, :] = jnp.zeros((t, LANES), F32)
            dvacc[0:t, :] = jnp.zeros((t, LANES), F32)

        lane = lax.broadcasted_iota(jnp.int32, (CHUNK, LANES), 1)
        for c in range(cpt):
            rows = slice(c * CHUNK, (c + 1) * CHUNK)
            band = slice(c * CHUNK, c * CHUNK + BAND_PAD)
            qc = q_ref[rows, :]
            doc = do_ref[rows, :]
            kb = kbuf[band, :]
            vb = vbuf[band, :]
            dqs = []
            for hh in range(2):
                mine = (lane >= hh * B_HEAD) & (lane < (hh + 1) * B_HEAD)
                qm = jnp.where(mine, qc, 0.0).astype(BF16)
                dom = jnp.where(mine, doc, 0.0).astype(BF16)
                bsl = slice(hh * BAND_PAD, (hh + 1) * BAND_PAD)
                p = _att_probs(qm, kb, b_ref[0, :, bsl], i * cpt + c)
                dvacc[band, :] += _dot_tn(p.astype(BF16), dom)
                dp = _dot_nt(dom, vb)
                ds = p * (dp - jnp.sum(dp * p, axis=-1, keepdims=True))
                db_ref[0, :, bsl] += ds
                dsb = (ds * scale).astype(BF16)
                dqs.append(_dot(dsb, kb))
                dkacc[band, :] += _dot_tn(dsb, qm)
            dq_ref[rows, :] = jnp.where(lane < B_HEAD, dqs[0], dqs[1])

        if have_in:
            dk_ref[...] = dkacc[t:2 * t, :] + dki_ref[...]
            dv_ref[...] = dvacc[t:2 * t, :] + dvi_ref[...]
        else:
            dk_ref[...] = dkacc[t:2 * t, :]
            dv_ref[...] = dvacc[t:2 * t, :]

    rev = lambda p, j: (nt - 1 - j, p)
    tok = pl.BlockSpec((t, LANES), rev)
    kv_specs = [pl.BlockSpec((t, LANES), lambda p, j: (jnp.maximum(nt - 2 - j, 0), p)),
                pl.BlockSpec((t, LANES), rev),
                pl.BlockSpec((t, LANES), lambda p, j: (jnp.maximum(nt - 2 - j, 0), n_pairs + p)),
                pl.BlockSpec((t, LANES), lambda p, j: (nt - 1 - j, n_pairs + p))]
    in_specs = [tok] + kv_specs + [pl.BlockSpec((1, CHUNK, 2 * BAND_PAD), lambda p, j: (p, 0, 0)), tok]
    args = [qp, kv, kv, kv, kv, bias, d_o]
    if have_in:
        in_specs += [tok, tok]
        args += [dk_in, dv_in]
    out = jax.ShapeDtypeStruct((n, bw), F32)
    return pl.pallas_call(
        body, grid=(n_pairs, nt), in_specs=in_specs,
        out_specs=[tok, tok, tok, pl.BlockSpec((1, CHUNK, 2 * BAND_PAD), lambda p, j: (p, 0, 0))],
        out_shape=[out, out, out, jax.ShapeDtypeStruct((n_pairs, CHUNK, 2 * BAND_PAD), F32)],
        scratch_shapes=[pltpu.VMEM((2 * t + CHUNK, LANES), BF16), pltpu.VMEM((2 * t + CHUNK, LANES), BF16),
                        pltpu.VMEM((2 * t + CHUNK, LANES), F32), pltpu.VMEM((2 * t + CHUNK, LANES), F32)],
        name=name, compiler_params=_cp(("parallel", "arbitrary")))(*args)


def adamw(w, gstack, m, v, name):
    r, c = w.shape
    s = gstack.shape[0]
    tr = _pick(r, 512, 8)

    def body(w_ref, g_ref, m_ref, v_ref, go_ref, d_ref, mo_ref, vo_ref):
        g = g_ref[0]
        for k in range(1, s):
            g = g + g_ref[k]
        mn = ADAM_B1 * m_ref[...] + (1.0 - ADAM_B1) * g
        vn = ADAM_B2 * v_ref[...] + (1.0 - ADAM_B2) * (g * g)
        m_hat = mn / (1.0 - ADAM_B1 ** ADAM_STEP)
        v_hat = vn / (1.0 - ADAM_B2 ** ADAM_STEP)
        go_ref[...] = g
        d_ref[...] = -ADAM_LR * (m_hat / (jnp.sqrt(v_hat) + ADAM_EPS) + ADAM_WD * w_ref[...])
        mo_ref[...] = mn
        vo_ref[...] = vn

    blk = pl.BlockSpec((tr, c), lambda i: (i, 0))
    out = jax.ShapeDtypeStruct((r, c), F32)
    return pl.pallas_call(
        body, grid=(r // tr,),
        in_specs=[blk, pl.BlockSpec((s, tr, c), lambda i: (0, i, 0)), blk, blk],
        out_specs=[blk] * 4, out_shape=[out] * 4, name=name,
        compiler_params=_cp(("parallel",)))(w, gstack, m, v)


def _place():
    x, y, c = lax.axis_index("x"), lax.axis_index("y"), lax.axis_index("c")
    chips = [(1 - x, y), (x, 1 - y), (1 - x, 1 - y)]
    return x, y, c, chips


def all_gather(shards, name):
    nt = len(shards)
    any_spec = pl.BlockSpec(memory_space=pl.ANY)

    def body(*refs):
        ins = refs[:nt]
        outs = refs[nt:2 * nt]
        send_sems, recv_sems, local_sems = refs[2 * nt:]
        x, y, c, chips = _place()
        me, sibling = (x, y, c), (x, y, 1 - c)

        def slot(t, dev):
            return outs[t].at[4 * dev[0] + 2 * dev[1] + dev[2]]

        def copy(t, k, block, to, src=None):
            return pltpu.make_async_remote_copy(
                src_ref=slot(t, block) if src is None else src, dst_ref=slot(t, block),
                send_sem=send_sems.at[7 * t + k], recv_sem=recv_sems.at[7 * t + k],
                device_id=to, device_id_type=MESH)

        local, remote = [], []
        for t in range(nt):
            mine = pltpu.make_async_copy(ins[t], slot(t, me), local_sems.at[t])
            mine.start()
            local.append(mine)
            first = [copy(t, 0, me, sibling, src=ins[t])]
            first += [copy(t, 1 + j, me, (*chip, c), src=ins[t]) for j, chip in enumerate(chips)]
            for cp in first:
                cp.start()
            remote += first
        for t in range(nt):
            for j, chip in enumerate(chips):
                copy(t, 1 + j, (*chip, c), me).wait_recv()
                fwd = copy(t, 4 + j, (*chip, c), sibling)
                fwd.start()
                remote.append(fwd)
        for t in range(nt):
            copy(t, 0, sibling, me).wait_recv()
            for j, chip in enumerate(chips):
                copy(t, 4 + j, (*chip, 1 - c), me).wait_recv()
        for cp in remote:
            cp.wait_send()
        for cp in local:
            cp.wait()

    return pl.pallas_call(
        body, in_specs=[any_spec] * nt, out_specs=[any_spec] * nt,
        out_shape=[jax.ShapeDtypeStruct((N_DEV, *s.shape), s.dtype) for s in shards],
        scratch_shapes=[pltpu.SemaphoreType.DMA((7 * nt,)), pltpu.SemaphoreType.DMA((7 * nt,)),
                        pltpu.SemaphoreType.DMA((nt,))],
        name=name)(*shards)


def rs_sibling(g8s, name):
    nt = len(g8s)
    any_spec = pl.BlockSpec(memory_space=pl.ANY)

    def body(*refs):
        ins = refs[:nt]
        outs = refs[nt:2 * nt]
        send_sems, recv_sems = refs[2 * nt:]
        x, y, c, _ = _place()
        copies = []
        for t in range(nt):
            for q in range(4):
                cp = pltpu.make_async_remote_copy(
                    src_ref=ins[t].at[2 * q + (1 - c)], dst_ref=outs[t].at[q],
                    send_sem=send_sems.at[4 * t + q], recv_sem=recv_sems.at[4 * t + q],
                    device_id=(x, y, 1 - c), device_id_type=MESH)
                cp.start()
                copies.append(cp)
        for cp in copies:
            cp.wait_recv()
        for cp in copies:
            cp.wait_send()

    return pl.pallas_call(
        body, in_specs=[any_spec] * nt, out_specs=[any_spec] * nt,
        out_shape=[jax.ShapeDtypeStruct((4, *g.shape[1:]), g.dtype) for g in g8s],
        scratch_shapes=[pltpu.SemaphoreType.DMA((4 * nt,)), pltpu.SemaphoreType.DMA((4 * nt,))],
        name=name)(*g8s)


def pair_add(g8, recv, c_idx, name):
    _, r, c = g8.shape
    tr = _pick(r, 512, 8)

    def body(c_ref, g_ref, r_ref, o_ref):
        o_ref[...] = g_ref[...] + r_ref[...]

    return pl.pallas_call(
        body,
        grid_spec=pltpu.PrefetchScalarGridSpec(
            num_scalar_prefetch=1, grid=(4, r // tr),
            in_specs=[pl.BlockSpec((1, tr, c), lambda q, i, cr: (2 * q + cr[0], i, 0)),
                      pl.BlockSpec((1, tr, c), lambda q, i, cr: (q, i, 0))],
            out_specs=pl.BlockSpec((1, tr, c), lambda q, i, cr: (q, i, 0))),
        out_shape=jax.ShapeDtypeStruct((4, r, c), F32), name=name,
        compiler_params=_cp(("parallel", "parallel")))(c_idx, g8, recv)


def rs_chips(parts, name):
    nt = len(parts)
    any_spec = pl.BlockSpec(memory_space=pl.ANY)

    def body(*refs):
        ins = refs[:nt]
        outs = refs[nt:2 * nt]
        send_sems, recv_sems, local_sems = refs[2 * nt:]
        x, y, c, chips = _place()
        mine = 2 * x + y
        local, remote = [], []
        for t in range(nt):
            for j, chip in enumerate(chips):
                cp = pltpu.make_async_remote_copy(
                    src_ref=ins[t].at[2 * chip[0] + chip[1]], dst_ref=outs[t].at[mine],
                    send_sem=send_sems.at[3 * t + j], recv_sem=recv_sems.at[3 * t + j],
                    device_id=(*chip, c), device_id_type=MESH)
                cp.start()
                remote.append(cp)
            own = pltpu.make_async_copy(ins[t].at[mine], outs[t].at[mine], local_sems.at[t])
            own.start()
            local.append(own)
        for t in range(nt):
            for j, chip in enumerate(chips):
                pltpu.make_async_remote_copy(
                    src_ref=ins[t].at[mine], dst_ref=outs[t].at[2 * chip[0] + chip[1]],
                    send_sem=send_sems.at[3 * t + j], recv_sem=recv_sems.at[3 * t + j],
                    device_id=(*chip, c), device_id_type=MESH).wait_recv()
        for cp in remote:
            cp.wait_send()
        for cp in local:
            cp.wait()

    return pl.pallas_call(
        body, in_specs=[any_spec] * nt, out_specs=[any_spec] * nt,
        out_shape=[jax.ShapeDtypeStruct(p.shape, p.dtype) for p in parts],
        scratch_shapes=[pltpu.SemaphoreType.DMA((3 * nt,)), pltpu.SemaphoreType.DMA((3 * nt,)),
                        pltpu.SemaphoreType.DMA((nt,))],
        name=name)(*parts)


def all_reduce_small(pack, name):
    r, c = pack.shape

    def body(x_ref, o_ref, buf, send_sems, recv_sems, local_sem):
        x, y, cc, chips = _place()
        me, sibling = (x, y, cc), (x, y, 1 - cc)

        def slot(dev):
            return buf.at[4 * dev[0] + 2 * dev[1] + dev[2]]

        def copy(k, block, to, src=None):
            return pltpu.make_async_remote_copy(
                src_ref=slot(block) if src is None else src, dst_ref=slot(block),
                send_sem=send_sems.at[k], recv_sem=recv_sems.at[k], device_id=to, device_id_type=MESH)

        mine = pltpu.make_async_copy(x_ref, slot(me), local_sem)
        mine.start()
        first = [copy(0, me, sibling, src=x_ref)]
        first += [copy(1 + j, me, (*chip, cc), src=x_ref) for j, chip in enumerate(chips)]
        for cp in first:
            cp.start()
        passed = [copy(4 + j, (*chip, cc), sibling) for j, chip in enumerate(chips)]
        for j, chip in enumerate(chips):
            copy(1 + j, (*chip, cc), me).wait_recv()
            passed[j].start()
        copy(0, sibling, me).wait_recv()
        for j, chip in enumerate(chips):
            copy(4 + j, (*chip, 1 - cc), me).wait_recv()
        for cp in first + passed:
            cp.wait_send()
        mine.wait()
        acc = buf[0]
        for k in range(1, N_DEV):
            acc = acc + buf[k]
        o_ref[...] = acc

    return pl.pallas_call(
        body, in_specs=[pl.BlockSpec(memory_space=pltpu.VMEM)],
        out_specs=pl.BlockSpec(memory_space=pltpu.VMEM),
        out_shape=jax.ShapeDtypeStruct((r, c), F32),
        scratch_shapes=[pltpu.VMEM((N_DEV, r, c), F32), pltpu.SemaphoreType.DMA((7,)),
                        pltpu.SemaphoreType.DMA((7,)), pltpu.SemaphoreType.DMA],
        name=name, compiler_params=_cp())(pack)


def _row(v):
    return v.reshape(1, -1)


def _lane_row(vals, offset):
    return jnp.pad(vals, (offset, LANES - offset - vals.shape[0])).reshape(1, LANES)


def _bias_to_pairs(b):
    i, nh, bp = b.shape
    return b.reshape(i, nh // 2, 2, bp).transpose(1, 0, 2, 3).reshape(nh // 2, i, 2 * bp)


def _bias_from_pairs(b):
    p, i, bp2 = b.shape
    return b.reshape(p, i, 2, bp2 // 2).transpose(1, 0, 2, 3).reshape(i, 2 * p, bp2 // 2)


def local_step(x, target, W):
    n, d = x.shape
    la, ha = W["a_A_log"].shape
    lb, hb, tbl = W["b_rel_bias"].shape
    depth = W["f_norm"].shape[0]
    clip = (tbl - 1) // 2
    tp = -(-tbl // LANES) * LANES
    qk = ha * A_HEAD
    cw = 3 * qk
    bw = hb * B_HEAD
    a_in = cw + qk + 2 * ha

    h = x
    saves = []
    kv = h_kv = None
    for l in range(depth):
        sv = {"h_in": h}
        if l < la:
            alog = _lane_row(W["a_A_log"][l], ha)
            dtb = _lane_row(W["a_dt_bias"][l], ha)
            proj = norm_matmul(h, _row(W["a_norm"][l]), W["a_w_in"][l], f"a_in_proj")
            q, k, v, bb, gb = gdn_prep(proj, W["a_conv"][l], alog, dtb, ha, "gdn_prep")
            o, states = gdn_fwd(q, k, v, bb, gb, ha, "gdn_fwd")
            h, y = gdn_out(o, proj, _row(W["a_out_norm"][l]), W["a_w_out"][l], h, ha, "gdn_out")
            sv.update(proj=proj, q=q, k=k, v=v, bb=bb, gb=gb, states=states, o=o, y=y, alog=alog, dtb=dtb)
        else:
            j = l - la
            if j == 0:
                h_kv = h
                kv = norm_matmul(h, _row(W["kv_norm"]), W["w_kv"], "kv_proj")
            qp = norm_matmul(h, _row(W["b_norm"][j]), W["b_w_q"][j], "b_q_proj")
            tblp = jnp.pad(W["b_rel_bias"][j], ((0, 0), (0, tp - tbl)))
            bias = _bias_to_pairs(bias_expand(tblp, clip, "bias_expand"))
            o = attn_fwd(qp, kv, bias, "attn_fwd")
            h = matmul_res(o, W["b_w_out"][j], h, "b_out_proj")
            sv.update(qp=qp, bias=bias, o=o)
        sv["h_mid"] = h
        up = norm_matmul(h, _row(W["f_norm"][l]), W["f_w_up"][l], "f_up_proj")
        h, act = ffn_act_down(up, W["f_conv"][l], _row(W["f_conv_b"][l]), W["f_w_down"][l], h, "ffn_act_down")
        sv.update(up=up, act=act)
        saves.append(sv)

    loss, dh, d_final = loss_head(h, _row(W["final_norm"]), target)

    G = {k_: [None] * (la if k_.startswith("a_") else lb if k_.startswith("b_") else depth)
         for k_ in ("a_norm", "a_w_in", "a_conv", "a_A_log", "a_dt_bias", "a_out_norm", "a_w_out",
                    "b_norm", "b_w_q", "b_rel_bias", "b_w_out", "f_norm", "f_w_up", "f_conv", "f_conv_b", "f_w_down")}
    G["final_norm"] = d_final[0]
    dk_acc = dv_acc = None
    for l in reversed(range(depth)):
        sv = saves[l]
        dhc, dcb = ffn_bwd_act(dh, sv["up"], W["f_conv"][l], _row(W["f_conv_b"][l]), W["f_w_down"][l], "ffn_bwd_act")
        G["f_w_down"][l] = matmul_tn(sv["act"], dh, "f_down_wgrad")
        G["f_conv_b"][l] = dcb[0]
        dh, dup, dcw, dg = ffn_bwd_up(dhc, sv["up"], W["f_conv"][l], W["f_w_up"][l], sv["h_mid"], dh,
                                      _row(W["f_norm"][l]), "ffn_bwd_up")
        G["f_conv"][l] = dcw
        G["f_norm"][l] = dg[0]
        G["f_w_up"][l] = norm_matmul_tn(sv["h_mid"], _row(W["f_norm"][l]), dup, "f_up_wgrad")
        if l < la:
            w_in = W["a_w_in"][l]
            do, dz, dwn = gdn_out_bwd(dh, sv["o"], sv["proj"], _row(W["a_out_norm"][l]), W["a_w_out"][l], ha, "gdn_out_bwd")
            G["a_out_norm"][l] = dwn[0]
            G["a_w_out"][l] = matmul_tn(sv["y"], dh, "a_out_wgrad")
            dq, dk, dv, dbb, dgb = gdn_bwd(sv["q"], sv["k"], sv["v"], sv["bb"], sv["gb"], sv["states"], do, ha, "gdn_bwd")
            du, dba, dal, ddt = gdn_prep_bwd(sv["proj"], W["a_conv"][l], sv["alog"], sv["dtb"],
                                             dq, dk, dv, dbb, dgb, ha, "gdn_prep_bwd")
            G["a_A_log"][l] = dal[0, ha:2 * ha]
            G["a_dt_bias"][l] = ddt[0, ha:2 * ha]
            dqkv, dconv = conv_bwd(du, sv["proj"], W["a_conv"][l], A_CONV, "gdn_conv_bwd")
            G["a_conv"][l] = dconv
            gam = _row(W["a_norm"][l])
            pieces = [(dqkv, w_in[:, :cw]), (dz, w_in[:, cw:cw + qk]), (dba, w_in[:, cw + qk:])]
            G["a_w_in"][l] = jnp.concatenate(
                [norm_matmul_tn(sv["h_in"], gam, dqkv, "a_in_wgrad_qkv"),
                 norm_matmul_tn(sv["h_in"], gam, dz, "a_in_wgrad_z"),
                 norm_matmul_tn(sv["h_in"], gam, dba, "a_in_wgrad_ba")[:, :2 * ha]], axis=1)
            dh, dg = dx_norm_bwd(dh, sv["h_in"], gam, pieces, "a_in_dx")
            G["a_norm"][l] = dg[0]
        else:
            j = l - la
            d_o = matmul_nt(dh, W["b_w_out"][j], "b_out_dx")
            G["b_w_out"][j] = matmul_tn(sv["o"], dh, "b_out_wgrad")
            dq, dk_acc, dv_acc, dbias = attn_bwd(sv["qp"], kv, sv["bias"], d_o, dk_acc, dv_acc,
                                                 "attn_bwd" if dk_acc is None else "attn_bwd_acc")
            G["b_rel_bias"][j] = bias_expand_bwd(_bias_from_pairs(dbias), clip, tp, "bias_expand_bwd")[:, :tbl]
            gam = _row(W["b_norm"][j])
            G["b_w_q"][j] = norm_matmul_tn(sv["h_in"], gam, dq, "b_q_wgrad")
            dh, dg = dx_norm_bwd(dh, sv["h_in"], gam, [(dq, W["b_w_q"][j])], "b_q_dx")
            G["b_norm"][j] = dg[0]
            if j == 0:
                gam = _row(W["kv_norm"])
                G["w_kv"] = jnp.concatenate([norm_matmul_tn(h_kv, gam, dk_acc, "kv_wgrad_k"),
                                             norm_matmul_tn(h_kv, gam, dv_acc, "kv_wgrad_v")], axis=1)
                dh, dg = dx_norm_bwd(dh, h_kv, gam, [(dk_acc, W["w_kv"][:, :bw]), (dv_acc, W["w_kv"][:, bw:])], "kv_dx")
                G["kv_norm"] = dg[0]
    out = {k_: (jnp.stack(v_) if isinstance(v_, list) else v_) for k_, v_ in G.items()}
    out["a_w_in"] = out["a_w_in"][:, :, :a_in]
    return loss[0, 0], dh, out


WEIGHTS = ["a_norm", "a_w_in", "a_conv", "a_A_log", "a_dt_bias", "a_out_norm", "a_w_out", "kv_norm", "w_kv",
           "b_norm", "b_w_q", "b_rel_bias", "b_w_out", "f_norm", "f_w_up", "f_conv", "f_conv_b", "f_w_down",
           "final_norm"]
SHARD_AXIS = {"a_norm": 1, "a_w_in": 2, "a_conv": 2, "a_w_out": 1, "w_kv": 1, "b_w_q": 1, "b_w_out": 1,
              "f_w_up": 2, "f_conv": 2, "f_w_down": 1}
BIG = ["a_w_in", "a_w_out", "w_kv", "b_w_q", "b_w_out", "f_w_up", "f_w_down"]
SMALL_SHARDED = ["a_norm", "a_conv", "f_conv"]


def _unstack(g, axis):
    return jnp.concatenate([g[i] for i in range(N_DEV)], axis=axis)


def _to_blocks(full, axis):
    parts = jnp.stack(jnp.split(full, N_DEV, axis=axis))
    return parts.reshape(N_DEV, -1, parts.shape[-1])


def _pack(arrs):
    flat = []
    for a in arrs:
        f = a.reshape(-1)
        flat.append(jnp.pad(f, (0, (-f.shape[0]) % LANES)))
    f = jnp.concatenate(flat)
    f = jnp.pad(f, (0, (-f.shape[0]) % (8 * LANES)))
    return f.reshape(-1, LANES)


def _unpack(pack, shapes):
    flat = pack.reshape(-1)
    out, pos = [], 0
    for s in shapes:
        sz = math.prod(s)
        out.append(flat[pos:pos + sz].reshape(s))
        pos += sz + (-sz) % LANES
    return out


def _as2d(a):
    return a.reshape(1, -1) if a.ndim == 1 else a.reshape(-1, a.shape[-1])


def kernel(x, a_norm, a_w_in, a_conv, a_A_log, a_dt_bias, a_out_norm, a_w_out, kv_norm, w_kv, b_norm, b_w_q, b_rel_bias, b_w_out, f_norm, f_w_up, f_conv, f_conv_b, f_w_down, final_norm, loss_target, m_a_norm, m_a_w_in, m_a_conv, m_a_A_log, m_a_dt_bias, m_a_out_norm, m_a_w_out, m_kv_norm, m_w_kv, m_b_norm, m_b_w_q, m_b_rel_bias, m_b_w_out, m_f_norm, m_f_w_up, m_f_conv, m_f_conv_b, m_f_w_down, m_final_norm, v_a_norm, v_a_w_in, v_a_conv, v_a_A_log, v_a_dt_bias, v_a_out_norm, v_a_w_out, v_kv_norm, v_w_kv, v_b_norm, v_b_w_q, v_b_rel_bias, v_b_w_out, v_f_norm, v_f_w_up, v_f_conv, v_f_conv_b, v_f_w_down, v_final_norm):
    w = dict(a_norm=a_norm, a_w_in=a_w_in, a_conv=a_conv, a_A_log=a_A_log, a_dt_bias=a_dt_bias,
             a_out_norm=a_out_norm, a_w_out=a_w_out, kv_norm=kv_norm, w_kv=w_kv, b_norm=b_norm, b_w_q=b_w_q,
             b_rel_bias=b_rel_bias, b_w_out=b_w_out, f_norm=f_norm, f_w_up=f_w_up, f_conv=f_conv,
             f_conv_b=f_conv_b, f_w_down=f_w_down, final_norm=final_norm)
    mom = dict(a_norm=m_a_norm, a_w_in=m_a_w_in, a_conv=m_a_conv, a_A_log=m_a_A_log, a_dt_bias=m_a_dt_bias,
               a_out_norm=m_a_out_norm, a_w_out=m_a_w_out, kv_norm=m_kv_norm, w_kv=m_w_kv, b_norm=m_b_norm,
               b_w_q=m_b_w_q, b_rel_bias=m_b_rel_bias, b_w_out=m_b_w_out, f_norm=m_f_norm, f_w_up=m_f_w_up,
               f_conv=m_f_conv, f_conv_b=m_f_conv_b, f_w_down=m_f_w_down, final_norm=m_final_norm)
    var = dict(a_norm=v_a_norm, a_w_in=v_a_w_in, a_conv=v_a_conv, a_A_log=v_a_A_log, a_dt_bias=v_a_dt_bias,
               a_out_norm=v_a_out_norm, a_w_out=v_a_w_out, kv_norm=v_kv_norm, w_kv=v_w_kv, b_norm=v_b_norm,
               b_w_q=v_b_w_q, b_rel_bias=v_b_rel_bias, b_w_out=v_b_w_out, f_norm=v_f_norm, f_w_up=v_f_w_up,
               f_conv=v_f_conv, f_conv_b=v_f_conv_b, f_w_down=v_f_w_down, final_norm=v_final_norm)
    me = 4 * lax.axis_index("x") + 2 * lax.axis_index("y") + lax.axis_index("c")

    small_shapes = [w[k].shape for k in SMALL_SHARDED]
    gathered = all_gather([w[k].astype(BF16) for k in BIG] + [_pack([w[k] for k in SMALL_SHARDED])], "weights_all_gather")
    full = dict(w)
    for k, g in zip(BIG, gathered[:-1]):
        full[k] = _unstack(g, SHARD_AXIS[k])
    small = [_unpack(gathered[-1][i], small_shapes) for i in range(N_DEV)]
    for idx, k in enumerate(SMALL_SHARDED):
        full[k] = jnp.concatenate([small[i][idx] for i in range(N_DEV)], axis=SHARD_AXIS[k])
    a_in = full["a_w_in"].shape[2]
    full["a_w_in"] = jnp.pad(full["a_w_in"], ((0, 0), (0, 0), (0, (-a_in) % LANES)))

    loss_part, grad_x, G = local_step(x[0], loss_target[0], full)

    g8 = [_to_blocks(G[k], SHARD_AXIS[k]) for k in BIG]
    c_idx = lax.axis_index("c").astype(jnp.int32).reshape(1)
    from_sibling = rs_sibling(g8, "grads_to_sibling")
    parts = [pair_add(g, r, c_idx, "grads_pair_add") for g, r in zip(g8, from_sibling)]
    stacks = rs_chips(parts, "grads_to_chips")

    small_names = [k for k in WEIGHTS if k not in BIG]
    reduced = _unpack(all_reduce_small(_pack([G[k] for k in small_names] + [loss_part.reshape(1)]), "small_all_reduce"),
                      [G[k].shape for k in small_names] + [(1,)])
    loss = reduced[-1][0]
    small_g = dict(zip(small_names, reduced[:-1]))
    for k in SMALL_SHARDED:
        sz = w[k].shape[SHARD_AXIS[k]]
        small_g[k] = lax.dynamic_slice_in_dim(small_g[k], me * sz, sz, axis=SHARD_AXIS[k])

    res = {}
    for k, st in zip(BIG, stacks):
        outs = adamw(_as2d(w[k]), st, _as2d(mom[k]), _as2d(var[k]), "adamw_" + k)
        res[k] = [o.reshape(w[k].shape) for o in outs]
    for k in small_names:
        outs = adamw(_as2d(w[k]), _as2d(small_g[k])[None], _as2d(mom[k]), _as2d(var[k]), "adamw_" + k)
        res[k] = [o.reshape(w[k].shape) for o in outs]

    return (loss, grad_x[None], *[res[k][0] for k in WEIGHTS], *[res[k][1] for k in WEIGHTS],
            *[res[k][2] for k in WEIGHTS], *[res[k][3] for k in WEIGHTS])
```

```python
import functools
import math

import jax
import jax.numpy as jnp
from jax import lax
from jax.experimental import pallas as pl
from jax.experimental.pallas import tpu as pltpu

F32 = jnp.float32
BF16 = jnp.bfloat16
HI = lax.Precision.HIGHEST
MESH = pl.DeviceIdType.MESH

EPS = 1e-6
NEG_INF = -1e30
CHUNK = 64
LEFT_CHUNKS = 8
BAND = (LEFT_CHUNKS + 1) * CHUNK
BAND_PAD = 640
A_CONV = 4
F_CONV = 3
A_HEAD = 128
B_HEAD = 64
LANES = 128
HALO = 8
N_DEV = 8

ADAM_LR = 0.001
ADAM_B1 = 0.9
ADAM_B2 = 0.999
ADAM_EPS = 1e-08
ADAM_WD = 0.01
ADAM_STEP = 10

VMEM_LIMIT_V7X = 56 * 1024 * 1024
FFN_TILE = 128


def _cp(sem=None, vmem=VMEM_LIMIT_V7X):
    kw = dict(vmem_limit_bytes=vmem)
    if sem is not None:
        kw["dimension_semantics"] = sem
    return pltpu.CompilerParams(**kw)


def _pick(n, target, q=LANES):
    best = None
    for t in range(q, min(n, target) + 1, q):
        if n % t == 0:
            best = t
    return best if best is not None else n


def _sig(x):
    return 1.0 / (1.0 + jnp.exp(-x))


def _softplus(x):
    return jnp.maximum(x, 0.0) + jnp.log(1.0 + jnp.exp(-jnp.abs(x)))


def _rms(x, g):
    return x * lax.rsqrt(jnp.mean(x * x, axis=-1, keepdims=True) + EPS) * g


def _rms_bwd(x, g, dxn):
    r = lax.rsqrt(jnp.mean(x * x, axis=-1, keepdims=True) + EPS)
    gd = dxn * g
    dx = r * gd - x * (r * r * r) * jnp.mean(x * gd, axis=-1, keepdims=True)
    dg = jnp.sum(dxn * x * r, axis=0, keepdims=True)
    return dx, dg


def _dot(a, b):
    return jnp.dot(a, b, preferred_element_type=F32)


def _dot_nt(a, b):
    return lax.dot_general(a, b, (((1,), (1,)), ((), ())), preferred_element_type=F32)


def _dot_tn(a, b):
    return lax.dot_general(a, b, (((0,), (0,)), ((), ())), preferred_element_type=F32)


def _hdot(a, b):
    return jnp.dot(a, b, precision=HI, preferred_element_type=F32)


def _hdot_nt(a, b):
    return lax.dot_general(a, b, (((1,), (1,)), ((), ())), precision=HI, preferred_element_type=F32)


def _hdot_tn(a, b):
    return lax.dot_general(a, b, (((0,), (0,)), ((), ())), precision=HI, preferred_element_type=F32)


def _resident(shape, index_map):
    return pl.BlockSpec(shape, index_map, pipeline_mode=pl.Buffered(1))


def norm_matmul(h, gamma, w, name):
    n, d = h.shape
    nc = w.shape[1]
    tm = _pick(n, 512, 8)
    tn = _pick(nc, 1536)

    def body(h_ref, g_ref, w_ref, o_ref):
        xn = _rms(h_ref[...], g_ref[...])
        o_ref[...] = _dot(xn.astype(BF16), w_ref[...])

    return pl.pallas_call(
        body, grid=(nc // tn, n // tm),
        in_specs=[pl.BlockSpec((tm, d), lambda j, i: (i, 0)),
                  pl.BlockSpec((1, d), lambda j, i: (0, 0)),
                  pl.BlockSpec((d, tn), lambda j, i: (0, j))],
        out_specs=pl.BlockSpec((tm, tn), lambda j, i: (i, j)),
        out_shape=jax.ShapeDtypeStruct((n, nc), F32), name=name,
        compiler_params=_cp(("parallel", "parallel")))(h, gamma, w)


def norm_matmul_tn(h, gamma, dy, name):
    n, d = h.shape
    nc = dy.shape[1]
    tm = _pick(n, 512, 8)
    tn = _pick(nc, 1536)

    def body(h_ref, g_ref, dy_ref, o_ref):
        @pl.when(pl.program_id(1) == 0)
        def _():
            o_ref[...] = jnp.zeros_like(o_ref)
        xn = _rms(h_ref[...], g_ref[...])
        o_ref[...] += _dot_tn(xn.astype(BF16), dy_ref[...].astype(BF16))

    return pl.pallas_call(
        body, grid=(nc // tn, n // tm),
        in_specs=[pl.BlockSpec((tm, d), lambda j, i: (i, 0)),
                  pl.BlockSpec((1, d), lambda j, i: (0, 0)),
                  pl.BlockSpec((tm, tn), lambda j, i: (i, j))],
        out_specs=pl.BlockSpec((d, tn), lambda j, i: (0, j)),
        out_shape=jax.ShapeDtypeStruct((d, nc), F32), name=name,
        compiler_params=_cp(("parallel", "arbitrary")))(h, gamma, dy)


def matmul_tn(a, dy, name):
    n, ka = a.shape
    nc = dy.shape[1]
    tm = _pick(n, 512, 8)
    tk = _pick(ka, 1536)
    tn = _pick(nc, 1024)

    def body(a_ref, dy_ref, o_ref):
        @pl.when(pl.program_id(2) == 0)
        def _():
            o_ref[...] = jnp.zeros_like(o_ref)
        o_ref[...] += _dot_tn(a_ref[...].astype(BF16), dy_ref[...].astype(BF16))

    return pl.pallas_call(
        body, grid=(ka // tk, nc // tn, n // tm),
        in_specs=[pl.BlockSpec((tm, tk), lambda k, j, i: (i, k)),
                  pl.BlockSpec((tm, tn), lambda k, j, i: (i, j))],
        out_specs=pl.BlockSpec((tk, tn), lambda k, j, i: (k, j)),
        out_shape=jax.ShapeDtypeStruct((ka, nc), F32), name=name,
        compiler_params=_cp(("parallel", "parallel", "arbitrary")))(a, dy)


def matmul_res(a, w, h, name):
    n, k = a.shape
    d = w.shape[1]
    tm = _pick(n, 512, 8)

    def body(a_ref, w_ref, h_ref, o_ref):
        o_ref[...] = h_ref[...] + _dot(a_ref[...].astype(BF16), w_ref[...])

    return pl.pallas_call(
        body, grid=(n // tm,),
        in_specs=[pl.BlockSpec((tm, k), lambda i: (i, 0)),
                  _resident((k, d), lambda i: (0, 0)),
                  pl.BlockSpec((tm, d), lambda i: (i, 0))],
        out_specs=pl.BlockSpec((tm, d), lambda i: (i, 0)),
        out_shape=jax.ShapeDtypeStruct((n, d), F32), name=name,
        compiler_params=_cp(("parallel",)))(a, w, h)


def matmul_nt(dy, w, name):
    n, k = dy.shape
    d = w.shape[0]
    tm = _pick(n, 512, 8)

    def body(dy_ref, w_ref, o_ref):
        o_ref[...] = _dot_nt(dy_ref[...].astype(BF16), w_ref[...])

    return pl.pallas_call(
        body, grid=(n // tm,),
        in_specs=[pl.BlockSpec((tm, k), lambda i: (i, 0)),
                  _resident((d, k), lambda i: (0, 0))],
        out_specs=pl.BlockSpec((tm, d), lambda i: (i, 0)),
        out_shape=jax.ShapeDtypeStruct((n, d), F32), name=name,
        compiler_params=_cp(("parallel",)))(dy, w)


def dx_norm_bwd(dout, h, gamma, pieces, name):
    n, d = h.shape
    tm = _pick(n, 256, 8)
    np_ = len(pieces)

    def body(*refs):
        dout_ref, h_ref, g_ref = refs[:3]
        dys = refs[3:3 + np_]
        ws = refs[3 + np_:3 + 2 * np_]
        dh_ref, dg_ref = refs[3 + 2 * np_:]
        dxn = _dot_nt(dys[0][...].astype(BF16), ws[0][...])
        for p in range(1, np_):
            dxn = dxn + _dot_nt(dys[p][...].astype(BF16), ws[p][...])
        dx, dg = _rms_bwd(h_ref[...], g_ref[...], dxn)
        dh_ref[...] = dout_ref[...] + dx

        @pl.when(pl.program_id(0) == 0)
        def _():
            dg_ref[...] = jnp.zeros_like(dg_ref)
        dg_ref[...] += dg

    in_specs = [pl.BlockSpec((tm, d), lambda i: (i, 0)),
                pl.BlockSpec((tm, d), lambda i: (i, 0)),
                pl.BlockSpec((1, d), lambda i: (0, 0))]
    in_specs += [pl.BlockSpec((tm, dy.shape[1]), lambda i: (i, 0)) for dy, _ in pieces]
    in_specs += [_resident(w.shape, lambda i: (0, 0)) for _, w in pieces]
    return pl.pallas_call(
        body, grid=(n // tm,), in_specs=in_specs,
        out_specs=[pl.BlockSpec((tm, d), lambda i: (i, 0)), pl.BlockSpec((1, d), lambda i: (0, 0))],
        out_shape=[jax.ShapeDtypeStruct((n, d), F32), jax.ShapeDtypeStruct((1, d), F32)], name=name,
        compiler_params=_cp(("arbitrary",)))(dout, h, gamma, *[p[0] for p in pieces], *[p[1] for p in pieces])


def loss_head(h, gamma, target, name="loss_head"):
    n, d = h.shape
    tm = _pick(n, 512, 8)

    def body(h_ref, g_ref, t_ref, loss_ref, dh_ref, dg_ref):
        @pl.when(pl.program_id(0) == 0)
        def _():
            loss_ref[...] = jnp.zeros_like(loss_ref)
            dg_ref[...] = jnp.zeros_like(dg_ref)
        x = h_ref[...]
        g = g_ref[...]
        e = _rms(x, g) - t_ref[...]
        part = jnp.sum(jnp.sum(e * e, axis=-1, keepdims=True), axis=0, keepdims=True) * (0.5 / d)
        loss_ref[...] += jnp.broadcast_to(part, loss_ref.shape)
        dx, dg = _rms_bwd(x, g, e * (1.0 / d))
        dh_ref[...] = dx
        dg_ref[...] += dg

    return pl.pallas_call(
        body, grid=(n // tm,),
        in_specs=[pl.BlockSpec((tm, d), lambda i: (i, 0)), pl.BlockSpec((1, d), lambda i: (0, 0)),
                  pl.BlockSpec((tm, d), lambda i: (i, 0))],
        out_specs=[pl.BlockSpec((8, LANES), lambda i: (0, 0)), pl.BlockSpec((tm, d), lambda i: (i, 0)),
                   pl.BlockSpec((1, d), lambda i: (0, 0))],
        out_shape=[jax.ShapeDtypeStruct((8, LANES), F32), jax.ShapeDtypeStruct((n, d), F32),
                   jax.ShapeDtypeStruct((1, d), F32)], name=name,
        compiler_params=_cp(("arbitrary",)))(h, gamma, target)


def _prev_halo_map(t):
    return lambda i: (jnp.maximum(i * (t // HALO) - 1, 0), 0)


def _next_halo_map(t, n):
    return lambda i: (jnp.minimum((i + 1) * (t // HALO), n // HALO - 1), 0)


def _causal_conv(xs, w_ref, width, t):
    acc = w_ref[0:1, :] * xs[pl.ds(HALO - (width - 1), t), :]
    for k in range(1, width):
        acc = acc + w_ref[k:k + 1, :] * xs[pl.ds(HALO - (width - 1) + k, t), :]
    return acc


def ffn_act_down(up, conv_w, conv_b, w_down, h, name):
    n, c2 = up.shape
    ff = c2 // 2
    d = h.shape[1]
    t = _pick(n, FFN_TILE, 8)

    def body(up_ref, halo_ref, cw_ref, cb_ref, wd_ref, h_ref, o_ref, act_ref, xs):
        i = pl.program_id(0)
        xs[0:HALO, :] = jnp.where(i > 0, halo_ref[...], 0.0)
        xs[HALO:, :] = up_ref[...]
        hc = _causal_conv(xs, cw_ref, F_CONV, t) + cb_ref[...]
        gate = hc[:, :ff]
        val = hc[:, ff:]
        act = (gate * _sig(gate) * val).astype(BF16)
        act_ref[...] = act
        o_ref[...] = h_ref[...] + _dot(act, wd_ref[...])

    return pl.pallas_call(
        body, grid=(n // t,),
        in_specs=[pl.BlockSpec((t, c2), lambda i: (i, 0)),
                  pl.BlockSpec((HALO, c2), _prev_halo_map(t)),
                  pl.BlockSpec((F_CONV, c2), lambda i: (0, 0)),
                  pl.BlockSpec((1, c2), lambda i: (0, 0)),
                  _resident((ff, d), lambda i: (0, 0)),
                  pl.BlockSpec((t, d), lambda i: (i, 0))],
        out_specs=[pl.BlockSpec((t, d), lambda i: (i, 0)), pl.BlockSpec((t, ff), lambda i: (i, 0))],
        out_shape=[jax.ShapeDtypeStruct((n, d), F32), jax.ShapeDtypeStruct((n, ff), BF16)],
        scratch_shapes=[pltpu.VMEM((t + HALO, c2), F32)], name=name,
        compiler_params=_cp(("parallel",)))(up, up, conv_w, conv_b, w_down, h)


def ffn_bwd_act(dout, up, conv_w, conv_b, w_down, name):
    n, c2 = up.shape
    ff = c2 // 2
    d = dout.shape[1]
    t = _pick(n, FFN_TILE, 8)

    def body(dout_ref, up_ref, halo_ref, cw_ref, cb_ref, wd_ref, dhc_ref, dcb_ref, xs):
        i = pl.program_id(0)
        xs[0:HALO, :] = jnp.where(i > 0, halo_ref[...], 0.0)
        xs[HALO:, :] = up_ref[...]
        hc = _causal_conv(xs, cw_ref, F_CONV, t) + cb_ref[...]
        gate = hc[:, :ff]
        val = hc[:, ff:]
        sg = _sig(gate)
        dact = _dot_nt(dout_ref[...].astype(BF16), wd_ref[...])
        dgate = dact * val * (sg * (1.0 + gate * (1.0 - sg)))
        dval = dact * gate * sg
        dhc_ref[:, :ff] = dgate
        dhc_ref[:, ff:] = dval

        @pl.when(i == 0)
        def _():
            dcb_ref[...] = jnp.zeros_like(dcb_ref)
        dcb_ref[:, :ff] += jnp.sum(dgate, axis=0, keepdims=True)
        dcb_ref[:, ff:] += jnp.sum(dval, axis=0, keepdims=True)

    return pl.pallas_call(
        body, grid=(n // t,),
        in_specs=[pl.BlockSpec((t, d), lambda i: (i, 0)),
                  pl.BlockSpec((t, c2), lambda i: (i, 0)),
                  pl.BlockSpec((HALO, c2), _prev_halo_map(t)),
                  pl.BlockSpec((F_CONV, c2), lambda i: (0, 0)),
                  pl.BlockSpec((1, c2), lambda i: (0, 0)),
                  _resident((ff, d), lambda i: (0, 0))],
        out_specs=[pl.BlockSpec((t, c2), lambda i: (i, 0)), pl.BlockSpec((1, c2), lambda i: (0, 0))],
        out_shape=[jax.ShapeDtypeStruct((n, c2), F32), jax.ShapeDtypeStruct((1, c2), F32)],
        scratch_shapes=[pltpu.VMEM((t + HALO, c2), F32)], name=name,
        compiler_params=_cp(("arbitrary",)))(dout, up, up, conv_w, conv_b, w_down)


def conv_bwd_tail(dy_ref, dnext_ref, x_ref, xprev_ref, cw_ref, dcw_ref, ds, xs, width, t, i, last):
    ds[0:t, :] = dy_ref[...]
    ds[t:, :] = jnp.where(i < last, dnext_ref[...], 0.0)
    xs[0:HALO, :] = jnp.where(i > 0, xprev_ref[...], 0.0)
    xs[HALO:, :] = x_ref[...]
    dy = dy_ref[...]

    @pl.when(i == 0)
    def _():
        dcw_ref[...] = jnp.zeros_like(dcw_ref)
    dx = None
    for k in range(width):
        term = cw_ref[k:k + 1, :] * ds[pl.ds(width - 1 - k, t), :]
        dx = term if dx is None else dx + term
        dcw_ref[k:k + 1, :] += jnp.sum(dy * xs[pl.ds(HALO - (width - 1) + k, t), :], axis=0, keepdims=True)
    return dx


def ffn_bwd_up(dhc, up, conv_w, w_up, h, dout, gamma, name):
    n, c2 = up.shape
    d = h.shape[1]
    t = _pick(n, FFN_TILE, 8)
    last = n // t - 1

    def body(dhc_ref, dnext_ref, up_ref, halo_ref, cw_ref, wu_ref, h_ref, dout_ref, g_ref,
             dh_ref, dup_ref, dcw_ref, dg_ref, ds, xs):
        i = pl.program_id(0)
        dup = conv_bwd_tail(dhc_ref, dnext_ref, up_ref, halo_ref, cw_ref, dcw_ref, ds, xs, F_CONV, t, i, last)
        dupb = dup.astype(BF16)
        dup_ref[...] = dupb
        dxn = _dot_nt(dupb, wu_ref[...])
        dx, dg = _rms_bwd(h_ref[...], g_ref[...], dxn)
        dh_ref[...] = dout_ref[...] + dx

        @pl.when(i == 0)
        def _():
            dg_ref[...] = jnp.zeros_like(dg_ref)
        dg_ref[...] += dg

    return pl.pallas_call(
        body, grid=(n // t,),
        in_specs=[pl.BlockSpec((t, c2), lambda i: (i, 0)),
                  pl.BlockSpec((HALO, c2), _next_halo_map(t, n)),
                  pl.BlockSpec((t, c2), lambda i: (i, 0)),
                  pl.BlockSpec((HALO, c2), _prev_halo_map(t)),
                  pl.BlockSpec((F_CONV, c2), lambda i: (0, 0)),
                  _resident((d, c2), lambda i: (0, 0)),
                  pl.BlockSpec((t, d), lambda i: (i, 0)),
                  pl.BlockSpec((t, d), lambda i: (i, 0)),
                  pl.BlockSpec((1, d), lambda i: (0, 0))],
        out_specs=[pl.BlockSpec((t, d), lambda i: (i, 0)), pl.BlockSpec((t, c2), lambda i: (i, 0)),
                   pl.BlockSpec((F_CONV, c2), lambda i: (0, 0)), pl.BlockSpec((1, d), lambda i: (0, 0))],
        out_shape=[jax.ShapeDtypeStruct((n, d), F32), jax.ShapeDtypeStruct((n, c2), BF16),
                   jax.ShapeDtypeStruct((F_CONV, c2), F32), jax.ShapeDtypeStruct((1, d), F32)],
        scratch_shapes=[pltpu.VMEM((t + HALO, c2), F32), pltpu.VMEM((t + HALO, c2), F32)], name=name,
        compiler_params=_cp(("arbitrary",)))(dhc, dhc, up, up, conv_w, w_up, h, dout, gamma)


def _gdn_head(uq, uk, uv, pba, alog, dtb, head, n_heads):
    lane = lax.broadcasted_iota(jnp.int32, pba.shape, 1)
    sq = uq * _sig(uq)
    q = sq * lax.rsqrt(jnp.sum(sq * sq, axis=-1, keepdims=True) + EPS) * (A_HEAD ** -0.5)
    sk = uk * _sig(uk)
    k = sk * lax.rsqrt(jnp.sum(sk * sk, axis=-1, keepdims=True) + EPS)
    v = uv * _sig(uv)
    beta = jnp.sum(jnp.where(lane == head, _sig(pba), 0.0), axis=-1, keepdims=True)
    g_all = -jnp.exp(alog) * _softplus(pba + dtb)
    g = jnp.sum(jnp.where(lane == n_heads + head, g_all, 0.0), axis=-1, keepdims=True)
    return q, k, v, jnp.broadcast_to(beta, uq.shape), jnp.broadcast_to(g, uq.shape)


def gdn_prep(proj, conv_w, alog, dtb, n_heads, name):
    n = proj.shape[0]
    qk = n_heads * A_HEAD
    cw = 3 * qk
    ba_blk = (cw + qk) // LANES
    t = _pick(n, 256, 8)

    def body(x_ref, halo_ref, pba_ref, cw_ref, al_ref, dt_ref, q_ref, k_ref, v_ref, b_ref, g_ref, xs):
        i = pl.program_id(0)
        xs[0:HALO, :] = jnp.where(i > 0, halo_ref[...], 0.0)
        xs[HALO:, :] = x_ref[...]
        u = _causal_conv(xs, cw_ref, A_CONV, t)
        pba = pba_ref[...]
        for hd in range(n_heads):
            s0 = slice(hd * A_HEAD, (hd + 1) * A_HEAD)
            s1 = slice(qk + hd * A_HEAD, qk + (hd + 1) * A_HEAD)
            s2 = slice(2 * qk + hd * A_HEAD, 2 * qk + (hd + 1) * A_HEAD)
            q, k, v, bb, gb = _gdn_head(u[:, s0], u[:, s1], u[:, s2], pba, al_ref[...], dt_ref[...], hd, n_heads)
            q_ref[:, s0] = q
            k_ref[:, s0] = k
            v_ref[:, s0] = v
            b_ref[:, s0] = bb
            g_ref[:, s0] = gb

    out = jax.ShapeDtypeStruct((n, qk), F32)
    return pl.pallas_call(
        body, grid=(n // t,),
        in_specs=[pl.BlockSpec((t, cw), lambda i: (i, 0)),
                  pl.BlockSpec((HALO, cw), _prev_halo_map(t)),
                  pl.BlockSpec((t, LANES), lambda i: (i, ba_blk)),
                  pl.BlockSpec((A_CONV, cw), lambda i: (0, 0)),
                  pl.BlockSpec((1, LANES), lambda i: (0, 0)),
                  pl.BlockSpec((1, LANES), lambda i: (0, 0))],
        out_specs=[pl.BlockSpec((t, qk), lambda i: (i, 0))] * 5,
        out_shape=[out] * 5,
        scratch_shapes=[pltpu.VMEM((t + HALO, cw), F32)], name=name,
        compiler_params=_cp(("parallel",)))(proj, proj, proj, conv_w, alog, dtb)


def gdn_prep_bwd(proj, conv_w, alog, dtb, dq, dk, dv, dbb, dgb, n_heads, name):
    n = proj.shape[0]
    qk = n_heads * A_HEAD
    cw = 3 * qk
    ba_blk = (cw + qk) // LANES
    t = _pick(n, 256, 8)

    def body(x_ref, halo_ref, pba_ref, cw_ref, al_ref, dt_ref, dq_ref, dk_ref, dv_ref, dbb_ref, dgb_ref,
             du_ref, dba_ref, dal_ref, ddt_ref, xs):
        i = pl.program_id(0)
        xs[0:HALO, :] = jnp.where(i > 0, halo_ref[...], 0.0)
        xs[HALO:, :] = x_ref[...]
        u = _causal_conv(xs, cw_ref, A_CONV, t)
        pba = pba_ref[...]
        lane0 = lax.broadcasted_iota(jnp.int32, (t, A_HEAD), 1) == 0
        dba = jnp.zeros((t, LANES), F32)
        dal = jnp.zeros((1, LANES), F32)
        ddt = jnp.zeros((1, LANES), F32)
        for hd in range(n_heads):
            s0 = slice(hd * A_HEAD, (hd + 1) * A_HEAD)
            s1 = slice(qk + hd * A_HEAD, qk + (hd + 1) * A_HEAD)
            s2 = slice(2 * qk + hd * A_HEAD, 2 * qk + (hd + 1) * A_HEAD)
            fn = functools.partial(_gdn_head, head=hd, n_heads=n_heads)
            _, vjp = jax.vjp(fn, u[:, s0], u[:, s1], u[:, s2], pba, al_ref[...], dt_ref[...])
            cts = (dq_ref[:, s0], dk_ref[:, s0], dv_ref[:, s0],
                   jnp.where(lane0, dbb_ref[:, s0], 0.0), jnp.where(lane0, dgb_ref[:, s0], 0.0))
            duq, duk, duv, dpba, da, dd = vjp(cts)
            du_ref[:, s0] = duq
            du_ref[:, s1] = duk
            du_ref[:, s2] = duv
            dba = dba + dpba
            dal = dal + da
            ddt = ddt + dd
        dba_ref[...] = dba

        @pl.when(i == 0)
        def _():
            dal_ref[...] = jnp.zeros_like(dal_ref)
            ddt_ref[...] = jnp.zeros_like(ddt_ref)
        dal_ref[...] += dal
        ddt_ref[...] += ddt

    tok = pl.BlockSpec((t, qk), lambda i: (i, 0))
    row = pl.BlockSpec((1, LANES), lambda i: (0, 0))
    return pl.pallas_call(
        body, grid=(n // t,),
        in_specs=[pl.BlockSpec((t, cw), lambda i: (i, 0)),
                  pl.BlockSpec((HALO, cw), _prev_halo_map(t)),
                  pl.BlockSpec((t, LANES), lambda i: (i, ba_blk)),
                  pl.BlockSpec((A_CONV, cw), lambda i: (0, 0)), row, row,
                  tok, tok, tok, tok, tok],
        out_specs=[pl.BlockSpec((t, cw), lambda i: (i, 0)), pl.BlockSpec((t, LANES), lambda i: (i, 0)), row, row],
        out_shape=[jax.ShapeDtypeStruct((n, cw), F32), jax.ShapeDtypeStruct((n, LANES), F32),
                   jax.ShapeDtypeStruct((1, LANES), F32), jax.ShapeDtypeStruct((1, LANES), F32)],
        scratch_shapes=[pltpu.VMEM((t + HALO, cw), F32)], name=name,
        compiler_params=_cp(("arbitrary",)))(proj, proj, proj, conv_w, alog, dtb, dq, dk, dv, dbb, dgb)


def conv_bwd(du, x, conv_w, width, name):
    n, cw = du.shape
    t = _pick(n, 256, 8)
    last = n // t - 1

    def body(du_ref, dnext_ref, x_ref, halo_ref, cw_ref, dx_ref, dcw_ref, ds, xs):
        i = pl.program_id(0)
        dx_ref[...] = conv_bwd_tail(du_ref, dnext_ref, x_ref, halo_ref, cw_ref, dcw_ref, ds, xs, width, t, i, last)

    return pl.pallas_call(
        body, grid=(n // t,),
        in_specs=[pl.BlockSpec((t, cw), lambda i: (i, 0)),
                  pl.BlockSpec((HALO, cw), _next_halo_map(t, n)),
                  pl.BlockSpec((t, cw), lambda i: (i, 0)),
                  pl.BlockSpec((HALO, cw), _prev_halo_map(t)),
                  pl.BlockSpec((width, cw), lambda i: (0, 0))],
        out_specs=[pl.BlockSpec((t, cw), lambda i: (i, 0)), pl.BlockSpec((width, cw), lambda i: (0, 0))],
        out_shape=[jax.ShapeDtypeStruct((n, cw), F32), jax.ShapeDtypeStruct((width, cw), F32)],
        scratch_shapes=[pltpu.VMEM((t + HALO, cw), F32), pltpu.VMEM((t + HALO, cw), F32)], name=name,
        compiler_params=_cp(("arbitrary",)))(du, du, x, x, conv_w)


def _b(x):
    return x.astype(BF16)


def _mm_nn(a, b):
    return _dot(_b(a), _b(b))


def _mm_nt(a, b):
    return _dot_nt(_b(a), _b(b))


def _mm_tn(a, b):
    return _dot_tn(_b(a), _b(b))


@jax.custom_vjp
def _mmg_nn(a, b):
    return _mm_nn(a, b)


_mmg_nn.defvjp(lambda a, b: (_mm_nn(a, b), (a, b)),
               lambda res, dc: (_mm_nt(dc, res[1]), _mm_tn(res[0], dc)))


@jax.custom_vjp
def _mmg_nt(a, b):
    return _mm_nt(a, b)


_mmg_nt.defvjp(lambda a, b: (_mm_nt(a, b), (a, b)),
               lambda res, dc: (_mm_nn(dc, res[1]), _mm_tn(dc, res[0])))


@jax.custom_vjp
def _mmg_tn(a, b):
    return _mm_tn(a, b)


_mmg_tn.defvjp(lambda a, b: (_mm_tn(a, b), (a, b)),
               lambda res, dc: (_mm_nt(res[1], dc), _mm_nn(res[0], dc)))


def _each(f, *lists):
    return [f(*a) for a in zip(*lists)]


def _unit_lower_inv(ms):
    c = ms[0].shape[0]
    eye = jnp.where(lax.broadcasted_iota(jnp.int32, (c, c), 0) == lax.broadcasted_iota(jnp.int32, (c, c), 1), 1.0, 0.0)
    xs = [eye - m for m in ms]
    pws = _each(_mm_nn, ms, ms)
    for it in range(5):
        xs = _each(lambda x, pw: x + _mm_nn(x, pw), xs, pws)
        if it < 4:
            pws = _each(_mm_nn, pws, pws)
    rs = _each(lambda m, x: eye - x - _hdot(m, x), ms, xs)
    return _each(lambda x, r: x + _mm_nn(x, r), xs, rs)


@jax.custom_vjp
def _unit_lower_inv_g(ms):
    return _unit_lower_inv(ms)


_unit_lower_inv_g.defvjp(lambda ms: (lambda xs: (xs, xs))(_unit_lower_inv(ms)),
                         lambda xs, dxs: (_each(lambda t, x: -_mm_nt(t, x), _each(_mm_tn, xs, dxs), xs),))

_GDN_OPS = (_mm_nn, _mm_nt, _mm_tn, _unit_lower_inv)
_GDN_OPS_GRAD = (_mmg_nn, _mmg_nt, _mmg_tn, _unit_lower_inv_g)


def _gdn_chunk(ops, state, q, k, v, bb, gb):
    nn, nt, tn, inv = ops
    c = CHUNK
    ri = lax.broadcasted_iota(jnp.int32, (c, c), 0)
    ci = lax.broadcasted_iota(jnp.int32, (c, c), 1)
    causal = ri >= ci
    strict = ri > ci
    tri = jnp.where(causal, 1.0, 0.0)
    gc = [_hdot(tri, g) for g in gb]
    decay = [jnp.where(causal, jnp.exp(jnp.where(causal, x[:, :c] - x.T[:c, :], 0.0)), 0.0) for x in gc]
    kb = _each(lambda a, b: a * b, k, bb)
    kk = _each(nt, kb, k)
    m = _each(lambda a, d: jnp.where(strict, a * d, 0.0), kk, decay)
    tinv = inv(m)
    egc = [jnp.exp(x) for x in gc]
    u = _each(nn, tinv, _each(lambda a, b: a * b, v, bb))
    w = _each(nn, tinv, _each(lambda a, b: a * b, kb, egc))
    attn = _each(lambda a, d: a * d, _each(nt, q, k), decay)
    glast = [jnp.sum(g, axis=0, keepdims=True) for g in gb]
    ws = _each(nn, w, state)
    v_new = _each(lambda a, b: a - b, u, ws)
    qs = _each(nn, _each(lambda a, b: a * b, q, egc), state)
    av = _each(nn, attn, v_new)
    o = _each(lambda a, b: a + b, qs, av)
    kv = _each(tn, _each(lambda a, gl, x: a * jnp.exp(gl - x), k, glast, gc), v_new)
    new_state = _each(lambda s, gl, a: s * jnp.exp(gl) + a, state, glast, kv)
    return o, new_state


def gdn_fwd(q, k, v, bb, gb, n_heads, name):
    n, w = q.shape
    nc = n // CHUNK
    cb = min(8, nc)
    rows = cb * CHUNK

    def body(q_ref, k_ref, v_ref, b_ref, g_ref, o_ref, st_ref, s_scr):
        @pl.when(pl.program_id(0) == 0)
        def _():
            s_scr[...] = jnp.zeros_like(s_scr)

        def step(c, carry):
            sl = pl.ds(pl.multiple_of(c * CHUNK, CHUNK), CHUNK)
            lanes = [slice(hd * A_HEAD, (hd + 1) * A_HEAD) for hd in range(n_heads)]
            state = [s_scr[hd] for hd in range(n_heads)]
            o, new_state = _gdn_chunk(_GDN_OPS, state, *[[r[sl, ls] for ls in lanes]
                                                        for r in (q_ref, k_ref, v_ref, b_ref, g_ref)])
            for hd, ls in enumerate(lanes):
                st_ref[hd, pl.ds(c, 1)] = state[hd][None]
                o_ref[sl, ls] = o[hd]
                s_scr[hd] = new_state[hd]
            return carry

        lax.fori_loop(0, cb, step, 0)

    tok = pl.BlockSpec((rows, w), lambda j: (j, 0))
    return pl.pallas_call(
        body, grid=(nc // cb,),
        in_specs=[tok] * 5,
        out_specs=[tok, pl.BlockSpec((n_heads, cb, A_HEAD, A_HEAD), lambda j: (0, j, 0, 0))],
        out_shape=[jax.ShapeDtypeStruct(q.shape, F32), jax.ShapeDtypeStruct((n_heads, nc, A_HEAD, A_HEAD), F32)],
        scratch_shapes=[pltpu.VMEM((n_heads, A_HEAD, A_HEAD), F32)], name=name,
        compiler_params=_cp(("arbitrary",)))(q, k, v, bb, gb)


def gdn_bwd(q, k, v, bb, gb, states, do, n_heads, name):
    n, w = q.shape
    nc = n // CHUNK
    cb = min(4, nc)
    rows = cb * CHUNK
    nblk = nc // cb
    chunk_fn = functools.partial(_gdn_chunk, _GDN_OPS_GRAD)

    def body(q_ref, k_ref, v_ref, b_ref, g_ref, st_ref, do_ref, dq_ref, dk_ref, dv_ref, db_ref, dg_ref, ds_scr):
        @pl.when(pl.program_id(0) == 0)
        def _():
            ds_scr[...] = jnp.zeros_like(ds_scr)

        def step(s, carry):
            c = cb - 1 - s
            sl = pl.ds(pl.multiple_of(c * CHUNK, CHUNK), CHUNK)
            lanes = [slice(hd * A_HEAD, (hd + 1) * A_HEAD) for hd in range(n_heads)]
            state = [st_ref[hd, pl.ds(c, 1)][0] for hd in range(n_heads)]
            _, vjp = jax.vjp(chunk_fn, state, *[[r[sl, ls] for ls in lanes]
                                                for r in (q_ref, k_ref, v_ref, b_ref, g_ref)])
            dstate, dq, dk, dv, dbb, dgb = vjp(([do_ref[sl, ls] for ls in lanes],
                                                [ds_scr[hd] for hd in range(n_heads)]))
            for hd, ls in enumerate(lanes):
                ds_scr[hd] = dstate[hd]
                dq_ref[sl, ls] = dq[hd]
                dk_ref[sl, ls] = dk[hd]
                dv_ref[sl, ls] = dv[hd]
                db_ref[sl, ls] = jnp.broadcast_to(jnp.sum(dbb[hd], axis=-1, keepdims=True), dbb[hd].shape)
                dg_ref[sl, ls] = jnp.broadcast_to(jnp.sum(dgb[hd], axis=-1, keepdims=True), dgb[hd].shape)
            return carry

        lax.fori_loop(0, cb, step, 0)

    tok = pl.BlockSpec((rows, w), lambda j: (nblk - 1 - j, 0))
    out = jax.ShapeDtypeStruct(q.shape, F32)
    return pl.pallas_call(
        body, grid=(nblk,),
        in_specs=[tok] * 5 + [pl.BlockSpec((n_heads, cb, A_HEAD, A_HEAD), lambda j: (0, nblk - 1 - j, 0, 0)), tok],
        out_specs=[tok] * 5, out_shape=[out] * 5,
        scratch_shapes=[pltpu.VMEM((n_heads, A_HEAD, A_HEAD), F32)], name=name,
        compiler_params=_cp(("arbitrary",)))(q, k, v, bb, gb, states, do)


def _gdn_gate(oh, zh, w):
    r = lax.rsqrt(jnp.mean(oh * oh, axis=-1, keepdims=True) + EPS)
    return oh * r * w * (zh * _sig(zh))


def gdn_out(o, proj, out_norm, w_out, h, n_heads, name):
    n, vw = o.shape
    d = h.shape[1]
    z_blk = 3 * vw // vw
    t = _pick(n, 512, 8)

    def body(o_ref, z_ref, w_ref, wo_ref, h_ref, out_ref, y_ref):
        for hd in range(n_heads):
            s0 = slice(hd * A_HEAD, (hd + 1) * A_HEAD)
            y_ref[:, s0] = _gdn_gate(o_ref[:, s0], z_ref[:, s0], w_ref[...]).astype(BF16)
        out_ref[...] = h_ref[...] + _dot(y_ref[...], wo_ref[...])

    return pl.pallas_call(
        body, grid=(n // t,),
        in_specs=[pl.BlockSpec((t, vw), lambda i: (i, 0)),
                  pl.BlockSpec((t, vw), lambda i: (i, z_blk)),
                  pl.BlockSpec((1, A_HEAD), lambda i: (0, 0)),
                  _resident((vw, d), lambda i: (0, 0)),
                  pl.BlockSpec((t, d), lambda i: (i, 0))],
        out_specs=[pl.BlockSpec((t, d), lambda i: (i, 0)), pl.BlockSpec((t, vw), lambda i: (i, 0))],
        out_shape=[jax.ShapeDtypeStruct((n, d), F32), jax.ShapeDtypeStruct((n, vw), BF16)], name=name,
        compiler_params=_cp(("parallel",)))(o, proj, out_norm, w_out, h)


def gdn_out_bwd(dout, o, proj, out_norm, w_out, n_heads, name):
    n, vw = o.shape
    d = dout.shape[1]
    z_blk = 3
    t = _pick(n, 512, 8)

    def body(dout_ref, o_ref, z_ref, w_ref, wo_ref, do_ref, dz_ref, dw_ref):
        dy = _dot_nt(dout_ref[...].astype(BF16), wo_ref[...])
        dw = jnp.zeros((1, A_HEAD), F32)
        for hd in range(n_heads):
            s0 = slice(hd * A_HEAD, (hd + 1) * A_HEAD)
            _, vjp = jax.vjp(_gdn_gate, o_ref[:, s0], z_ref[:, s0], w_ref[...])
            doh, dzh, dwh = vjp(dy[:, s0])
            do_ref[:, s0] = doh
            dz_ref[:, s0] = dzh
            dw = dw + dwh

        @pl.when(pl.program_id(0) == 0)
        def _():
            dw_ref[...] = jnp.zeros_like(dw_ref)
        dw_ref[...] += dw

    tok = pl.BlockSpec((t, vw), lambda i: (i, 0))
    return pl.pallas_call(
        body, grid=(n // t,),
        in_specs=[pl.BlockSpec((t, d), lambda i: (i, 0)), tok,
                  pl.BlockSpec((t, vw), lambda i: (i, z_blk)),
                  pl.BlockSpec((1, A_HEAD), lambda i: (0, 0)),
                  _resident((vw, d), lambda i: (0, 0))],
        out_specs=[tok, tok, pl.BlockSpec((1, A_HEAD), lambda i: (0, 0))],
        out_shape=[jax.ShapeDtypeStruct((n, vw), F32), jax.ShapeDtypeStruct((n, vw), F32),
                   jax.ShapeDtypeStruct((1, A_HEAD), F32)], name=name,
        compiler_params=_cp(("arbitrary",)))(dout, o, proj, out_norm, w_out)


def _bias_onehot(i, clip, tbl_pad):
    m = lax.broadcasted_iota(jnp.int32, (BAND_PAD, tbl_pad), 0)
    r = lax.broadcasted_iota(jnp.int32, (BAND_PAD, tbl_pad), 1)
    idx = jnp.clip(i + LEFT_CHUNKS * CHUNK - m, -clip, clip) + clip
    return jnp.where((r == idx) & (m < BAND), 1.0, 0.0)


def bias_expand(tbl, clip, name):
    nh, tp = tbl.shape

    def body(t_ref, o_ref):
        oh = _bias_onehot(pl.program_id(0), clip, tp)
        o_ref[0] = _hdot_nt(t_ref[...], oh)

    return pl.pallas_call(
        body, grid=(CHUNK,),
        in_specs=[pl.BlockSpec((nh, tp), lambda i: (0, 0))],
        out_specs=pl.BlockSpec((1, nh, BAND_PAD), lambda i: (i, 0, 0)),
        out_shape=jax.ShapeDtypeStruct((CHUNK, nh, BAND_PAD), F32), name=name,
        compiler_params=_cp(("parallel",)))(tbl)


def bias_expand_bwd(dbias, clip, tp, name):
    _, nh, _ = dbias.shape

    def body(d_ref, o_ref):
        @pl.when(pl.program_id(0) == 0)
        def _():
            o_ref[...] = jnp.zeros_like(o_ref)
        oh = _bias_onehot(pl.program_id(0), clip, tp)
        o_ref[...] += _hdot(d_ref[0], oh)

    return pl.pallas_call(
        body, grid=(CHUNK,),
        in_specs=[pl.BlockSpec((1, nh, BAND_PAD), lambda i: (i, 0, 0))],
        out_specs=pl.BlockSpec((nh, tp), lambda i: (0, 0)),
        out_shape=jax.ShapeDtypeStruct((nh, tp), F32), name=name,
        compiler_params=_cp(("arbitrary",)))(dbias)


ATT_TILE = LEFT_CHUNKS * CHUNK


ATT_GROUP = 8


def _att_softmax(s, bias, n_chunk):
    s = s * (B_HEAD ** -0.5) + bias
    slot = lax.broadcasted_iota(jnp.int32, s.shape, 1)
    valid = (slot >= (LEFT_CHUNKS - n_chunk) * CHUNK) & (slot < BAND)
    s = jnp.where(valid, s, NEG_INF)
    p = jnp.exp(s - jnp.max(s, axis=-1, keepdims=True))
    return p / jnp.sum(p, axis=-1, keepdims=True)


def _att_specs(n_pairs):
    prev = lambda p, i: (jnp.maximum(i - 1, 0), p)
    cur = lambda p, i: (i, p)
    prev_v = lambda p, i: (jnp.maximum(i - 1, 0), n_pairs + p)
    cur_v = lambda p, i: (i, n_pairs + p)
    blk = (ATT_TILE, LANES)
    return [pl.BlockSpec(blk, prev), pl.BlockSpec(blk, cur), pl.BlockSpec(blk, prev_v), pl.BlockSpec(blk, cur_v)]


def _att_fill(kbuf, vbuf, kp_ref, kc_ref, vp_ref, vc_ref):
    t = ATT_TILE
    kbuf[0:t, :] = kp_ref[...].astype(BF16)
    kbuf[t:2 * t, :] = kc_ref[...].astype(BF16)
    kbuf[2 * t:, :] = jnp.zeros((CHUNK, LANES), BF16)
    vbuf[0:t, :] = vp_ref[...].astype(BF16)
    vbuf[t:2 * t, :] = vc_ref[...].astype(BF16)
    vbuf[2 * t:, :] = jnp.zeros((CHUNK, LANES), BF16)


def attn_fwd(qp, kv, bias, name):
    n, bw = qp.shape
    n_pairs = bw // LANES
    t = ATT_TILE
    cpt = t // CHUNK

    def body(q_ref, kp_ref, kc_ref, vp_ref, vc_ref, b_ref, o_ref, kbuf, vbuf):
        i = pl.program_id(1)
        _att_fill(kbuf, vbuf, kp_ref, kc_ref, vp_ref, vc_ref)
        lane = lax.broadcasted_iota(jnp.int32, (CHUNK, LANES), 1)
        mine = [(lane >= hh * B_HEAD) & (lane < (hh + 1) * B_HEAD) for hh in range(2)]
        for g0 in range(0, cpt, ATT_GROUP):
            units = [(c, hh) for c in range(g0, min(g0 + ATT_GROUP, cpt)) for hh in range(2)]
            band = {c: slice(c * CHUNK, c * CHUNK + BAND_PAD) for c, _ in units}
            s = [_dot_nt(jnp.where(mine[hh], q_ref[c * CHUNK:(c + 1) * CHUNK, :], 0.0).astype(BF16), kbuf[band[c], :])
                 for c, hh in units]
            p = [_att_softmax(s_u, b_ref[0, :, hh * BAND_PAD:(hh + 1) * BAND_PAD], i * cpt + c)
                 for s_u, (c, hh) in zip(s, units)]
            o = [_dot(p_u.astype(BF16), vbuf[band[c], :]) for p_u, (c, hh) in zip(p, units)]
            for u in range(0, len(units), 2):
                c = units[u][0]
                o_ref[c * CHUNK:(c + 1) * CHUNK, :] = jnp.where(lane < B_HEAD, o[u], o[u + 1])

    return pl.pallas_call(
        body, grid=(n_pairs, n // t),
        in_specs=[pl.BlockSpec((t, LANES), lambda p, i: (i, p))] + _att_specs(n_pairs)
        + [pl.BlockSpec((1, CHUNK, 2 * BAND_PAD), lambda p, i: (p, 0, 0))],
        out_specs=pl.BlockSpec((t, LANES), lambda p, i: (i, p)),
        out_shape=jax.ShapeDtypeStruct((n, bw), F32),
        scratch_shapes=[pltpu.VMEM((2 * t + CHUNK, LANES), BF16), pltpu.VMEM((2 * t + CHUNK, LANES), BF16)],
        name=name, compiler_params=_cp(("parallel", "parallel")))(qp, kv, kv, kv, kv, bias)


def attn_bwd(qp, kv, bias, d_o, dk_in, dv_in, name):
    n, bw = qp.shape
    n_pairs = bw // LANES
    t = ATT_TILE
    cpt = t // CHUNK
    nt = n // t
    have_in = dk_in is not None
    scale = B_HEAD ** -0.5

    def body(*refs):
        q_ref, kp_ref, kc_ref, vp_ref, vc_ref, b_ref, do_ref = refs[:7]
        pos = 7
        if have_in:
            dki_ref, dvi_ref = refs[7:9]
            pos = 9
        dq_ref, dk_ref, dv_ref, db_ref, kbuf, vbuf, dkacc, dvacc = refs[pos:]
        j = pl.program_id(1)
        i = nt - 1 - j
        _att_fill(kbuf, vbuf, kp_ref, kc_ref, vp_ref, vc_ref)

        @pl.when(j == 0)
        def _():
            dkacc[...] = jnp.zeros_like(dkacc)
            dvacc[...] = jnp.zeros_like(dvacc)
            db_ref[...] = jnp.zeros_like(db_ref)

        @pl.when(j > 0)
        def _():
            dkacc[t:2 * t, :] = dkacc[0:t, :]
            dvacc[t:2 * t, :] = dvacc[0:t, :]
            dkacc[0:t, :] = jnp.zeros((t, LANES), F32)
            dvacc[0:t, :] = jnp.zeros((t, LANES), F32)

        lane = lax.broadcasted_iota(jnp.int32, (CHUNK, LANES), 1)
        mine = [(lane >= hh * B_HEAD) & (lane < (hh + 1) * B_HEAD) for hh in range(2)]
        for g0 in range(0, cpt, ATT_GROUP):
            units = [(c, hh) for c in range(g0, min(g0 + ATT_GROUP, cpt)) for hh in range(2)]
            rows = {c: slice(c * CHUNK, (c + 1) * CHUNK) for c, _ in units}
            band = {c: slice(c * CHUNK, c * CHUNK + BAND_PAD) for c, _ in units}
            bsl = [slice(hh * BAND_PAD, (hh + 1) * BAND_PAD) for hh in range(2)]
            qm = [jnp.where(mine[hh], q_ref[rows[c], :], 0.0).astype(BF16) for c, hh in units]
            dom = [jnp.where(mine[hh], do_ref[rows[c], :], 0.0).astype(BF16) for c, hh in units]
            s = [_dot_nt(q_u, kbuf[band[c], :]) for q_u, (c, hh) in zip(qm, units)]
            dp = [_dot_nt(d_u, vbuf[band[c], :]) for d_u, (c, hh) in zip(dom, units)]
            p = [_att_softmax(s_u, b_ref[0, :, bsl[hh]], i * cpt + c) for s_u, (c, hh) in zip(s, units)]
            ds = [p_u * (dp_u - jnp.sum(dp_u * p_u, axis=-1, keepdims=True)) for p_u, dp_u in zip(p, dp)]
            dsb = [(d_u * scale).astype(BF16) for d_u in ds]
            dv = [_dot_tn(p_u.astype(BF16), d_u) for p_u, d_u in zip(p, dom)]
            dq = [_dot(d_u, kbuf[band[c], :]) for d_u, (c, hh) in zip(dsb, units)]
            dk = [_dot_tn(d_u, q_u) for d_u, q_u in zip(dsb, qm)]
            for u, (c, hh) in enumerate(units):
                db_ref[0, :, bsl[hh]] += ds[u]
                dvacc[band[c], :] += dv[u]
                dkacc[band[c], :] += dk[u]
            for u in range(0, len(units), 2):
                dq_ref[rows[units[u][0]], :] = jnp.where(lane < B_HEAD, dq[u], dq[u + 1])

        if have_in:
            dk_ref[...] = dkacc[t:2 * t, :] + dki_ref[...]
            dv_ref[...] = dvacc[t:2 * t, :] + dvi_ref[...]
        else:
            dk_ref[...] = dkacc[t:2 * t, :]
            dv_ref[...] = dvacc[t:2 * t, :]

    rev = lambda p, j: (nt - 1 - j, p)
    tok = pl.BlockSpec((t, LANES), rev)
    kv_specs = [pl.BlockSpec((t, LANES), lambda p, j: (jnp.maximum(nt - 2 - j, 0), p)),
                pl.BlockSpec((t, LANES), rev),
                pl.BlockSpec((t, LANES), lambda p, j: (jnp.maximum(nt - 2 - j, 0), n_pairs + p)),
                pl.BlockSpec((t, LANES), lambda p, j: (nt - 1 - j, n_pairs + p))]
    in_specs = [tok] + kv_specs + [pl.BlockSpec((1, CHUNK, 2 * BAND_PAD), lambda p, j: (p, 0, 0)), tok]
    args = [qp, kv, kv, kv, kv, bias, d_o]
    if have_in:
        in_specs += [tok, tok]
        args += [dk_in, dv_in]
    out = jax.ShapeDtypeStruct((n, bw), F32)
    return pl.pallas_call(
        body, grid=(n_pairs, nt), in_specs=in_specs,
        out_specs=[tok, tok, tok, pl.BlockSpec((1, CHUNK, 2 * BAND_PAD), lambda p, j: (p, 0, 0))],
        out_shape=[out, out, out, jax.ShapeDtypeStruct((n_pairs, CHUNK, 2 * BAND_PAD), F32)],
        scratch_shapes=[pltpu.VMEM((2 * t + CHUNK, LANES), BF16), pltpu.VMEM((2 * t + CHUNK, LANES), BF16),
                        pltpu.VMEM((2 * t + CHUNK, LANES), F32), pltpu.VMEM((2 * t + CHUNK, LANES), F32)],
        name=name, compiler_params=_cp(("parallel", "arbitrary")))(*args)


def adamw(w, gstack, m, v, name):
    r, c = w.shape
    s = gstack.shape[0]
    tr = _pick(r, 512, 8)

    def body(w_ref, g_ref, m_ref, v_ref, go_ref, d_ref, mo_ref, vo_ref):
        g = g_ref[0]
        for k in range(1, s):
            g = g + g_ref[k]
        mn = ADAM_B1 * m_ref[...] + (1.0 - ADAM_B1) * g
        vn = ADAM_B2 * v_ref[...] + (1.0 - ADAM_B2) * (g * g)
        m_hat = mn / (1.0 - ADAM_B1 ** ADAM_STEP)
        v_hat = vn / (1.0 - ADAM_B2 ** ADAM_STEP)
        go_ref[...] = g
        d_ref[...] = -ADAM_LR * (m_hat / (jnp.sqrt(v_hat) + ADAM_EPS) + ADAM_WD * w_ref[...])
        mo_ref[...] = mn
        vo_ref[...] = vn

    blk = pl.BlockSpec((tr, c), lambda i: (i, 0))
    out = jax.ShapeDtypeStruct((r, c), F32)
    return pl.pallas_call(
        body, grid=(r // tr,),
        in_specs=[blk, pl.BlockSpec((s, tr, c), lambda i: (0, i, 0)), blk, blk],
        out_specs=[blk] * 4, out_shape=[out] * 4, name=name,
        compiler_params=_cp(("parallel",)))(w, gstack, m, v)


def _place():
    x, y, c = lax.axis_index("x"), lax.axis_index("y"), lax.axis_index("c")
    chips = [(1 - x, y), (x, 1 - y), (1 - x, 1 - y)]
    return x, y, c, chips


def all_gather(shards, name):
    nt = len(shards)
    any_spec = pl.BlockSpec(memory_space=pl.ANY)

    def body(*refs):
        ins = refs[:nt]
        outs = refs[nt:2 * nt]
        send_sems, recv_sems, local_sems = refs[2 * nt:]
        x, y, c, chips = _place()
        me, sibling = (x, y, c), (x, y, 1 - c)

        def slot(t, dev):
            return outs[t].at[4 * dev[0] + 2 * dev[1] + dev[2]]

        def copy(t, k, block, to, src=None):
            return pltpu.make_async_remote_copy(
                src_ref=slot(t, block) if src is None else src, dst_ref=slot(t, block),
                send_sem=send_sems.at[7 * t + k], recv_sem=recv_sems.at[7 * t + k],
                device_id=to, device_id_type=MESH)

        local, remote = [], []
        for t in range(nt):
            mine = pltpu.make_async_copy(ins[t], slot(t, me), local_sems.at[t])
            mine.start()
            local.append(mine)
            first = [copy(t, 0, me, sibling, src=ins[t])]
            first += [copy(t, 1 + j, me, (*chip, c), src=ins[t]) for j, chip in enumerate(chips)]
            for cp in first:
                cp.start()
            remote += first
        for t in range(nt):
            for j, chip in enumerate(chips):
                copy(t, 1 + j, (*chip, c), me).wait_recv()
                fwd = copy(t, 4 + j, (*chip, c), sibling)
                fwd.start()
                remote.append(fwd)
        for t in range(nt):
            copy(t, 0, sibling, me).wait_recv()
            for j, chip in enumerate(chips):
                copy(t, 4 + j, (*chip, 1 - c), me).wait_recv()
        for cp in remote:
            cp.wait_send()
        for cp in local:
            cp.wait()

    return pl.pallas_call(
        body, in_specs=[any_spec] * nt, out_specs=[any_spec] * nt,
        out_shape=[jax.ShapeDtypeStruct((N_DEV, *s.shape), s.dtype) for s in shards],
        scratch_shapes=[pltpu.SemaphoreType.DMA((7 * nt,)), pltpu.SemaphoreType.DMA((7 * nt,)),
                        pltpu.SemaphoreType.DMA((nt,))],
        name=name)(*shards)


def rs_sibling(g8s, name):
    nt = len(g8s)
    any_spec = pl.BlockSpec(memory_space=pl.ANY)

    def body(*refs):
        ins = refs[:nt]
        outs = refs[nt:2 * nt]
        send_sems, recv_sems = refs[2 * nt:]
        x, y, c, _ = _place()
        copies = []
        for t in range(nt):
            for q in range(4):
                cp = pltpu.make_async_remote_copy(
                    src_ref=ins[t].at[2 * q + (1 - c)], dst_ref=outs[t].at[q],
                    send_sem=send_sems.at[4 * t + q], recv_sem=recv_sems.at[4 * t + q],
                    device_id=(x, y, 1 - c), device_id_type=MESH)
                cp.start()
                copies.append(cp)
        for cp in copies:
            cp.wait_recv()
        for cp in copies:
            cp.wait_send()

    return pl.pallas_call(
        body, in_specs=[any_spec] * nt, out_specs=[any_spec] * nt,
        out_shape=[jax.ShapeDtypeStruct((4, *g.shape[1:]), g.dtype) for g in g8s],
        scratch_shapes=[pltpu.SemaphoreType.DMA((4 * nt,)), pltpu.SemaphoreType.DMA((4 * nt,))],
        name=name)(*g8s)


def pair_add(g8, recv, c_idx, name):
    _, r, c = g8.shape
    tr = _pick(r, 512, 8)

    def body(c_ref, g_ref, r_ref, o_ref):
        o_ref[...] = g_ref[...] + r_ref[...]

    return pl.pallas_call(
        body,
        grid_spec=pltpu.PrefetchScalarGridSpec(
            num_scalar_prefetch=1, grid=(4, r // tr),
            in_specs=[pl.BlockSpec((1, tr, c), lambda q, i, cr: (2 * q + cr[0], i, 0)),
                      pl.BlockSpec((1, tr, c), lambda q, i, cr: (q, i, 0))],
            out_specs=pl.BlockSpec((1, tr, c), lambda q, i, cr: (q, i, 0))),
        out_shape=jax.ShapeDtypeStruct((4, r, c), F32), name=name,
        compiler_params=_cp(("parallel", "parallel")))(c_idx, g8, recv)


def rs_chips(parts, name):
    nt = len(parts)
    any_spec = pl.BlockSpec(memory_space=pl.ANY)

    def body(*refs):
        ins = refs[:nt]
        outs = refs[nt:2 * nt]
        send_sems, recv_sems, local_sems = refs[2 * nt:]
        x, y, c, chips = _place()
        mine = 2 * x + y
        local, remote = [], []
        for t in range(nt):
            for j, chip in enumerate(chips):
                cp = pltpu.make_async_remote_copy(
                    src_ref=ins[t].at[2 * chip[0] + chip[1]], dst_ref=outs[t].at[mine],
                    send_sem=send_sems.at[3 * t + j], recv_sem=recv_sems.at[3 * t + j],
                    device_id=(*chip, c), device_id_type=MESH)
                cp.start()
                remote.append(cp)
            own = pltpu.make_async_copy(ins[t].at[mine], outs[t].at[mine], local_sems.at[t])
            own.start()
            local.append(own)
        for t in range(nt):
            for j, chip in enumerate(chips):
                pltpu.make_async_remote_copy(
                    src_ref=ins[t].at[mine], dst_ref=outs[t].at[2 * chip[0] + chip[1]],
                    send_sem=send_sems.at[3 * t + j], recv_sem=recv_sems.at[3 * t + j],
                    device_id=(*chip, c), device_id_type=MESH).wait_recv()
        for cp in remote:
            cp.wait_send()
        for cp in local:
            cp.wait()

    return pl.pallas_call(
        body, in_specs=[any_spec] * nt, out_specs=[any_spec] * nt,
        out_shape=[jax.ShapeDtypeStruct(p.shape, p.dtype) for p in parts],
        scratch_shapes=[pltpu.SemaphoreType.DMA((3 * nt,)), pltpu.SemaphoreType.DMA((3 * nt,)),
                        pltpu.SemaphoreType.DMA((nt,))],
        name=name)(*parts)


def all_reduce_small(pack, name):
    r, c = pack.shape

    def body(x_ref, o_ref, buf, send_sems, recv_sems, local_sem):
        x, y, cc, chips = _place()
        me, sibling = (x, y, cc), (x, y, 1 - cc)

        def slot(dev):
            return buf.at[4 * dev[0] + 2 * dev[1] + dev[2]]

        def copy(k, block, to, src=None):
            return pltpu.make_async_remote_copy(
                src_ref=slot(block) if src is None else src, dst_ref=slot(block),
                send_sem=send_sems.at[k], recv_sem=recv_sems.at[k], device_id=to, device_id_type=MESH)

        mine = pltpu.make_async_copy(x_ref, slot(me), local_sem)
        mine.start()
        first = [copy(0, me, sibling, src=x_ref)]
        first += [copy(1 + j, me, (*chip, cc), src=x_ref) for j, chip in enumerate(chips)]
        for cp in first:
            cp.start()
        passed = [copy(4 + j, (*chip, cc), sibling) for j, chip in enumerate(chips)]
        for j, chip in enumerate(chips):
            copy(1 + j, (*chip, cc), me).wait_recv()
            passed[j].start()
        copy(0, sibling, me).wait_recv()
        for j, chip in enumerate(chips):
            copy(4 + j, (*chip, 1 - cc), me).wait_recv()
        for cp in first + passed:
            cp.wait_send()
        mine.wait()
        acc = buf[0]
        for k in range(1, N_DEV):
            acc = acc + buf[k]
        o_ref[...] = acc

    return pl.pallas_call(
        body, in_specs=[pl.BlockSpec(memory_space=pltpu.VMEM)],
        out_specs=pl.BlockSpec(memory_space=pltpu.VMEM),
        out_shape=jax.ShapeDtypeStruct((r, c), F32),
        scratch_shapes=[pltpu.VMEM((N_DEV, r, c), F32), pltpu.SemaphoreType.DMA((7,)),
                        pltpu.SemaphoreType.DMA((7,)), pltpu.SemaphoreType.DMA],
        name=name, compiler_params=_cp())(pack)


def _row(v):
    return v.reshape(1, -1)


def _lane_row(vals, offset):
    return jnp.pad(vals, (offset, LANES - offset - vals.shape[0])).reshape(1, LANES)


def _bias_to_pairs(b):
    i, nh, bp = b.shape
    return b.reshape(i, nh // 2, 2, bp).transpose(1, 0, 2, 3).reshape(nh // 2, i, 2 * bp)


def _bias_from_pairs(b):
    p, i, bp2 = b.shape
    return b.reshape(p, i, 2, bp2 // 2).transpose(1, 0, 2, 3).reshape(i, 2 * p, bp2 // 2)


def local_step(x, target, W):
    n, d = x.shape
    la, ha = W["a_A_log"].shape
    lb, hb, tbl = W["b_rel_bias"].shape
    depth = W["f_norm"].shape[0]
    clip = (tbl - 1) // 2
    tp = -(-tbl // LANES) * LANES
    qk = ha * A_HEAD
    cw = 3 * qk
    bw = hb * B_HEAD
    a_in = cw + qk + 2 * ha

    h = x
    saves = []
    kv = h_kv = None
    for l in range(depth):
        sv = {"h_in": h}
        if l < la:
            alog = _lane_row(W["a_A_log"][l], ha)
            dtb = _lane_row(W["a_dt_bias"][l], ha)
            proj = norm_matmul(h, _row(W["a_norm"][l]), W["a_w_in"][l], f"a_in_proj")
            q, k, v, bb, gb = gdn_prep(proj, W["a_conv"][l], alog, dtb, ha, "gdn_prep")
            o, states = gdn_fwd(q, k, v, bb, gb, ha, "gdn_fwd")
            h, y = gdn_out(o, proj, _row(W["a_out_norm"][l]), W["a_w_out"][l], h, ha, "gdn_out")
            sv.update(proj=proj, q=q, k=k, v=v, bb=bb, gb=gb, states=states, o=o, y=y, alog=alog, dtb=dtb)
        else:
            j = l - la
            if j == 0:
                h_kv = h
                kv = norm_matmul(h, _row(W["kv_norm"]), W["w_kv"], "kv_proj")
            qp = norm_matmul(h, _row(W["b_norm"][j]), W["b_w_q"][j], "b_q_proj")
            tblp = jnp.pad(W["b_rel_bias"][j], ((0, 0), (0, tp - tbl)))
            bias = _bias_to_pairs(bias_expand(tblp, clip, "bias_expand"))
            o = attn_fwd(qp, kv, bias, "attn_fwd")
            h = matmul_res(o, W["b_w_out"][j], h, "b_out_proj")
            sv.update(qp=qp, bias=bias, o=o)
        sv["h_mid"] = h
        up = norm_matmul(h, _row(W["f_norm"][l]), W["f_w_up"][l], "f_up_proj")
        h, act = ffn_act_down(up, W["f_conv"][l], _row(W["f_conv_b"][l]), W["f_w_down"][l], h, "ffn_act_down")
        sv.update(up=up, act=act)
        saves.append(sv)

    loss, dh, d_final = loss_head(h, _row(W["final_norm"]), target)

    G = {k_: [None] * (la if k_.startswith("a_") else lb if k_.startswith("b_") else depth)
         for k_ in ("a_norm", "a_w_in", "a_conv", "a_A_log", "a_dt_bias", "a_out_norm", "a_w_out",
                    "b_norm", "b_w_q", "b_rel_bias", "b_w_out", "f_norm", "f_w_up", "f_conv", "f_conv_b", "f_w_down")}
    G["final_norm"] = d_final[0]
    dk_acc = dv_acc = None
    for l in reversed(range(depth)):
        sv = saves[l]
        dhc, dcb = ffn_bwd_act(dh, sv["up"], W["f_conv"][l], _row(W["f_conv_b"][l]), W["f_w_down"][l], "ffn_bwd_act")
        G["f_w_down"][l] = matmul_tn(sv["act"], dh, "f_down_wgrad")
        G["f_conv_b"][l] = dcb[0]
        dh, dup, dcw, dg = ffn_bwd_up(dhc, sv["up"], W["f_conv"][l], W["f_w_up"][l], sv["h_mid"], dh,
                                      _row(W["f_norm"][l]), "ffn_bwd_up")
        G["f_conv"][l] = dcw
        G["f_norm"][l] = dg[0]
        G["f_w_up"][l] = norm_matmul_tn(sv["h_mid"], _row(W["f_norm"][l]), dup, "f_up_wgrad")
        if l < la:
            w_in = W["a_w_in"][l]
            do, dz, dwn = gdn_out_bwd(dh, sv["o"], sv["proj"], _row(W["a_out_norm"][l]), W["a_w_out"][l], ha, "gdn_out_bwd")
            G["a_out_norm"][l] = dwn[0]
            G["a_w_out"][l] = matmul_tn(sv["y"], dh, "a_out_wgrad")
            dq, dk, dv, dbb, dgb = gdn_bwd(sv["q"], sv["k"], sv["v"], sv["bb"], sv["gb"], sv["states"], do, ha, "gdn_bwd")
            du, dba, dal, ddt = gdn_prep_bwd(sv["proj"], W["a_conv"][l], sv["alog"], sv["dtb"],
                                             dq, dk, dv, dbb, dgb, ha, "gdn_prep_bwd")
            G["a_A_log"][l] = dal[0, ha:2 * ha]
            G["a_dt_bias"][l] = ddt[0, ha:2 * ha]
            dqkv, dconv = conv_bwd(du, sv["proj"], W["a_conv"][l], A_CONV, "gdn_conv_bwd")
            G["a_conv"][l] = dconv
            gam = _row(W["a_norm"][l])
            pieces = [(dqkv, w_in[:, :cw]), (dz, w_in[:, cw:cw + qk]), (dba, w_in[:, cw + qk:])]
            G["a_w_in"][l] = jnp.concatenate(
                [norm_matmul_tn(sv["h_in"], gam, dqkv, "a_in_wgrad_qkv"),
                 norm_matmul_tn(sv["h_in"], gam, dz, "a_in_wgrad_z"),
                 norm_matmul_tn(sv["h_in"], gam, dba, "a_in_wgrad_ba")[:, :2 * ha]], axis=1)
            dh, dg = dx_norm_bwd(dh, sv["h_in"], gam, pieces, "a_in_dx")
            G["a_norm"][l] = dg[0]
        else:
            j = l - la
            d_o = matmul_nt(dh, W["b_w_out"][j], "b_out_dx")
            G["b_w_out"][j] = matmul_tn(sv["o"], dh, "b_out_wgrad")
            dq, dk_acc, dv_acc, dbias = attn_bwd(sv["qp"], kv, sv["bias"], d_o, dk_acc, dv_acc,
                                                 "attn_bwd" if dk_acc is None else "attn_bwd_acc")
            G["b_rel_bias"][j] = bias_expand_bwd(_bias_from_pairs(dbias), clip, tp, "bias_expand_bwd")[:, :tbl]
            gam = _row(W["b_norm"][j])
            G["b_w_q"][j] = norm_matmul_tn(sv["h_in"], gam, dq, "b_q_wgrad")
            dh, dg = dx_norm_bwd(dh, sv["h_in"], gam, [(dq, W["b_w_q"][j])], "b_q_dx")
            G["b_norm"][j] = dg[0]
            if j == 0:
                gam = _row(W["kv_norm"])
                G["w_kv"] = jnp.concatenate([norm_matmul_tn(h_kv, gam, dk_acc, "kv_wgrad_k"),
                                             norm_matmul_tn(h_kv, gam, dv_acc, "kv_wgrad_v")], axis=1)
                dh, dg = dx_norm_bwd(dh, h_kv, gam, [(dk_acc, W["w_kv"][:, :bw]), (dv_acc, W["w_kv"][:, bw:])], "kv_dx")
                G["kv_norm"] = dg[0]
    out = {k_: (jnp.stack(v_) if isinstance(v_, list) else v_) for k_, v_ in G.items()}
    out["a_w_in"] = out["a_w_in"][:, :, :a_in]
    return loss[0, 0], dh, out


WEIGHTS = ["a_norm", "a_w_in", "a_conv", "a_A_log", "a_dt_bias", "a_out_norm", "a_w_out", "kv_norm", "w_kv",
           "b_norm", "b_w_q", "b_rel_bias", "b_w_out", "f_norm", "f_w_up", "f_conv", "f_conv_b", "f_w_down",
           "final_norm"]
SHARD_AXIS = {"a_norm": 1, "a_w_in": 2, "a_conv": 2, "a_w_out": 1, "w_kv": 1, "b_w_q": 1, "b_w_out": 1,
              "f_w_up": 2, "f_conv": 2, "f_w_down": 1}
BIG = ["a_w_in", "a_w_out", "w_kv", "b_w_q", "b_w_out", "f_w_up", "f_w_down"]
SMALL_SHARDED = ["a_norm", "a_conv", "f_conv"]


def _unstack(g, axis):
    return jnp.concatenate([g[i] for i in range(N_DEV)], axis=axis)


def _to_blocks(full, axis):
    parts = jnp.stack(jnp.split(full, N_DEV, axis=axis))
    return parts.reshape(N_DEV, -1, parts.shape[-1])


def _pack(arrs):
    flat = []
    for a in arrs:
        f = a.reshape(-1)
        flat.append(jnp.pad(f, (0, (-f.shape[0]) % LANES)))
    f = jnp.concatenate(flat)
    f = jnp.pad(f, (0, (-f.shape[0]) % (8 * LANES)))
    return f.reshape(-1, LANES)


def _unpack(pack, shapes):
    flat = pack.reshape(-1)
    out, pos = [], 0
    for s in shapes:
        sz = math.prod(s)
        out.append(flat[pos:pos + sz].reshape(s))
        pos += sz + (-sz) % LANES
    return out


def _as2d(a):
    return a.reshape(1, -1) if a.ndim == 1 else a.reshape(-1, a.shape[-1])


def kernel(x, a_norm, a_w_in, a_conv, a_A_log, a_dt_bias, a_out_norm, a_w_out, kv_norm, w_kv, b_norm, b_w_q, b_rel_bias, b_w_out, f_norm, f_w_up, f_conv, f_conv_b, f_w_down, final_norm, loss_target, m_a_norm, m_a_w_in, m_a_conv, m_a_A_log, m_a_dt_bias, m_a_out_norm, m_a_w_out, m_kv_norm, m_w_kv, m_b_norm, m_b_w_q, m_b_rel_bias, m_b_w_out, m_f_norm, m_f_w_up, m_f_conv, m_f_conv_b, m_f_w_down, m_final_norm, v_a_norm, v_a_w_in, v_a_conv, v_a_A_log, v_a_dt_bias, v_a_out_norm, v_a_w_out, v_kv_norm, v_w_kv, v_b_norm, v_b_w_q, v_b_rel_bias, v_b_w_out, v_f_norm, v_f_w_up, v_f_conv, v_f_conv_b, v_f_w_down, v_final_norm):
    w = dict(a_norm=a_norm, a_w_in=a_w_in, a_conv=a_conv, a_A_log=a_A_log, a_dt_bias=a_dt_bias,
             a_out_norm=a_out_norm, a_w_out=a_w_out, kv_norm=kv_norm, w_kv=w_kv, b_norm=b_norm, b_w_q=b_w_q,
             b_rel_bias=b_rel_bias, b_w_out=b_w_out, f_norm=f_norm, f_w_up=f_w_up, f_conv=f_conv,
             f_conv_b=f_conv_b, f_w_down=f_w_down, final_norm=final_norm)
    mom = dict(a_norm=m_a_norm, a_w_in=m_a_w_in, a_conv=m_a_conv, a_A_log=m_a_A_log, a_dt_bias=m_a_dt_bias,
               a_out_norm=m_a_out_norm, a_w_out=m_a_w_out, kv_norm=m_kv_norm, w_kv=m_w_kv, b_norm=m_b_norm,
               b_w_q=m_b_w_q, b_rel_bias=m_b_rel_bias, b_w_out=m_b_w_out, f_norm=m_f_norm, f_w_up=m_f_w_up,
               f_conv=m_f_conv, f_conv_b=m_f_conv_b, f_w_down=m_f_w_down, final_norm=m_final_norm)
    var = dict(a_norm=v_a_norm, a_w_in=v_a_w_in, a_conv=v_a_conv, a_A_log=v_a_A_log, a_dt_bias=v_a_dt_bias,
               a_out_norm=v_a_out_norm, a_w_out=v_a_w_out, kv_norm=v_kv_norm, w_kv=v_w_kv, b_norm=v_b_norm,
               b_w_q=v_b_w_q, b_rel_bias=v_b_rel_bias, b_w_out=v_b_w_out, f_norm=v_f_norm, f_w_up=v_f_w_up,
               f_conv=v_f_conv, f_conv_b=v_f_conv_b, f_w_down=v_f_w_down, final_norm=v_final_norm)
    me = 4 * lax.axis_index("x") + 2 * lax.axis_index("y") + lax.axis_index("c")

    small_shapes = [w[k].shape for k in SMALL_SHARDED]
    gathered = all_gather([w[k].astype(BF16) for k in BIG] + [_pack([w[k] for k in SMALL_SHARDED])], "weights_all_gather")
    full = dict(w)
    for k, g in zip(BIG, gathered[:-1]):
        full[k] = _unstack(g, SHARD_AXIS[k])
    small = [_unpack(gathered[-1][i], small_shapes) for i in range(N_DEV)]
    for idx, k in enumerate(SMALL_SHARDED):
        full[k] = jnp.concatenate([small[i][idx] for i in range(N_DEV)], axis=SHARD_AXIS[k])
    a_in = full["a_w_in"].shape[2]
    full["a_w_in"] = jnp.pad(full["a_w_in"], ((0, 0), (0, 0), (0, (-a_in) % LANES)))

    loss_part, grad_x, G = local_step(x[0], loss_target[0], full)

    g8 = [_to_blocks(G[k], SHARD_AXIS[k]) for k in BIG]
    c_idx = lax.axis_index("c").astype(jnp.int32).reshape(1)
    from_sibling = rs_sibling(g8, "grads_to_sibling")
    parts = [pair_add(g, r, c_idx, "grads_pair_add") for g, r in zip(g8, from_sibling)]
    stacks = rs_chips(parts, "grads_to_chips")

    small_names = [k for k in WEIGHTS if k not in BIG]
    reduced = _unpack(all_reduce_small(_pack([G[k] for k in small_names] + [loss_part.reshape(1)]), "small_all_reduce"),
                      [G[k].shape for k in small_names] + [(1,)])
    loss = reduced[-1][0]
    small_g = dict(zip(small_names, reduced[:-1]))
    for k in SMALL_SHARDED:
        sz = w[k].shape[SHARD_AXIS[k]]
        small_g[k] = lax.dynamic_slice_in_dim(small_g[k], me * sz, sz, axis=SHARD_AXIS[k])

    res = {}
    for k, st in zip(BIG, stacks):
        outs = adamw(_as2d(w[k]), st, _as2d(mom[k]), _as2d(var[k]), "adamw_" + k)
        res[k] = [o.reshape(w[k].shape) for o in outs]
    for k in small_names:
        outs = adamw(_as2d(w[k]), _as2d(small_g[k])[None], _as2d(mom[k]), _as2d(var[k]), "adamw_" + k)
        res[k] = [o.reshape(w[k].shape) for o in outs]

    return (loss, grad_x[None], *[res[k][0] for k in WEIGHTS], *[res[k][1] for k in WEIGHTS],
            *[res[k][2] for k in WEIGHTS], *[res[k][3] for k in WEIGHTS])
```

```python
import functools
import math

import jax
import jax.numpy as jnp
from jax import lax
from jax.experimental import pallas as pl
from jax.experimental.pallas import tpu as pltpu

F32 = jnp.float32
BF16 = jnp.bfloat16
HI = lax.Precision.HIGHEST
MESH = pl.DeviceIdType.MESH

EPS = 1e-6
NEG_INF = -1e30
CHUNK = 64
LEFT_CHUNKS = 8
BAND = (LEFT_CHUNKS + 1) * CHUNK
BAND_PAD = 640
A_CONV = 4
F_CONV = 3
A_HEAD = 128
B_HEAD = 64
LANES = 128
HALO = 8
N_DEV = 8

ADAM_LR = 0.001
ADAM_B1 = 0.9
ADAM_B2 = 0.999
ADAM_EPS = 1e-08
ADAM_WD = 0.01
ADAM_STEP = 10

VMEM_LIMIT_V7X = 56 * 1024 * 1024
COL_CHUNK = 256
FFN_TILE = 256


def _cp(sem=None, vmem=VMEM_LIMIT_V7X):
    kw = dict(vmem_limit_bytes=vmem)
    if sem is not None:
        kw["dimension_semantics"] = sem
    return pltpu.CompilerParams(**kw)


def _pick(n, target, q=LANES):
    best = None
    for t in range(q, min(n, target) + 1, q):
        if n % t == 0:
            best = t
    return best if best is not None else n


def _sig(x):
    return 1.0 / (1.0 + jnp.exp(-x))


def _softplus(x):
    return jnp.maximum(x, 0.0) + jnp.log(1.0 + jnp.exp(-jnp.abs(x)))


def _rms(x, g):
    return x * lax.rsqrt(jnp.mean(x * x, axis=-1, keepdims=True) + EPS) * g


def _rms_bwd(x, g, dxn):
    r = lax.rsqrt(jnp.mean(x * x, axis=-1, keepdims=True) + EPS)
    gd = dxn * g
    dx = r * gd - x * (r * r * r) * jnp.mean(x * gd, axis=-1, keepdims=True)
    dg = jnp.sum(dxn * x * r, axis=0, keepdims=True)
    return dx, dg


def _dot(a, b):
    return jnp.dot(a, b, preferred_element_type=F32)


def _dot_nt(a, b):
    return lax.dot_general(a, b, (((1,), (1,)), ((), ())), preferred_element_type=F32)


def _dot_tn(a, b):
    return lax.dot_general(a, b, (((0,), (0,)), ((), ())), preferred_element_type=F32)


def _hdot(a, b):
    return jnp.dot(a, b, precision=HI, preferred_element_type=F32)


def _hdot_nt(a, b):
    return lax.dot_general(a, b, (((1,), (1,)), ((), ())), precision=HI, preferred_element_type=F32)


def _hdot_tn(a, b):
    return lax.dot_general(a, b, (((0,), (0,)), ((), ())), precision=HI, preferred_element_type=F32)


def _resident(shape, index_map):
    return pl.BlockSpec(shape, index_map, pipeline_mode=pl.Buffered(1))


def norm_matmul(h, gamma, w, name, out_dtype=F32):
    n, d = h.shape
    nc = w.shape[1]
    tm = _pick(n, 512, 8)
    tn = _pick(nc, 1536)

    def body(h_ref, g_ref, w_ref, o_ref):
        xn = _rms(h_ref[...], g_ref[...])
        o_ref[...] = _dot(xn.astype(BF16), w_ref[...]).astype(out_dtype)

    return pl.pallas_call(
        body, grid=(nc // tn, n // tm),
        in_specs=[pl.BlockSpec((tm, d), lambda j, i: (i, 0)),
                  pl.BlockSpec((1, d), lambda j, i: (0, 0)),
                  pl.BlockSpec((d, tn), lambda j, i: (0, j))],
        out_specs=pl.BlockSpec((tm, tn), lambda j, i: (i, j)),
        out_shape=jax.ShapeDtypeStruct((n, nc), out_dtype), name=name,
        compiler_params=_cp(("parallel", "parallel")))(h, gamma, w)


def norm_matmul_tn(h, gamma, dy, name):
    n, d = h.shape
    nc = dy.shape[1]
    tm = _pick(n, 512, 8)
    tn = _pick(nc, 1536)

    def body(h_ref, g_ref, dy_ref, o_ref):
        @pl.when(pl.program_id(1) == 0)
        def _():
            o_ref[...] = jnp.zeros_like(o_ref)
        xn = _rms(h_ref[...], g_ref[...])
        o_ref[...] += _dot_tn(xn.astype(BF16), dy_ref[...].astype(BF16))

    return pl.pallas_call(
        body, grid=(nc // tn, n // tm),
        in_specs=[pl.BlockSpec((tm, d), lambda j, i: (i, 0)),
                  pl.BlockSpec((1, d), lambda j, i: (0, 0)),
                  pl.BlockSpec((tm, tn), lambda j, i: (i, j))],
        out_specs=pl.BlockSpec((d, tn), lambda j, i: (0, j)),
        out_shape=jax.ShapeDtypeStruct((d, nc), F32), name=name,
        compiler_params=_cp(("parallel", "arbitrary")))(h, gamma, dy)


def matmul_tn(a, dy, name):
    n, ka = a.shape
    nc = dy.shape[1]
    tm = _pick(n, 512, 8)
    tk = _pick(ka, 1536)
    tn = _pick(nc, 1024)

    def body(a_ref, dy_ref, o_ref):
        @pl.when(pl.program_id(2) == 0)
        def _():
            o_ref[...] = jnp.zeros_like(o_ref)
        o_ref[...] += _dot_tn(a_ref[...].astype(BF16), dy_ref[...].astype(BF16))

    return pl.pallas_call(
        body, grid=(ka // tk, nc // tn, n // tm),
        in_specs=[pl.BlockSpec((tm, tk), lambda k, j, i: (i, k)),
                  pl.BlockSpec((tm, tn), lambda k, j, i: (i, j))],
        out_specs=pl.BlockSpec((tk, tn), lambda k, j, i: (k, j)),
        out_shape=jax.ShapeDtypeStruct((ka, nc), F32), name=name,
        compiler_params=_cp(("parallel", "parallel", "arbitrary")))(a, dy)


def matmul_res(a, w, h, name):
    n, k = a.shape
    d = w.shape[1]
    tm = _pick(n, 512, 8)

    def body(a_ref, w_ref, h_ref, o_ref):
        o_ref[...] = h_ref[...] + _dot(a_ref[...].astype(BF16), w_ref[...])

    return pl.pallas_call(
        body, grid=(n // tm,),
        in_specs=[pl.BlockSpec((tm, k), lambda i: (i, 0)),
                  _resident((k, d), lambda i: (0, 0)),
                  pl.BlockSpec((tm, d), lambda i: (i, 0))],
        out_specs=pl.BlockSpec((tm, d), lambda i: (i, 0)),
        out_shape=jax.ShapeDtypeStruct((n, d), F32), name=name,
        compiler_params=_cp(("parallel",)))(a, w, h)


def matmul_nt(dy, w, name):
    n, k = dy.shape
    d = w.shape[0]
    tm = _pick(n, 512, 8)

    def body(dy_ref, w_ref, o_ref):
        o_ref[...] = _dot_nt(dy_ref[...].astype(BF16), w_ref[...])

    return pl.pallas_call(
        body, grid=(n // tm,),
        in_specs=[pl.BlockSpec((tm, k), lambda i: (i, 0)),
                  _resident((d, k), lambda i: (0, 0))],
        out_specs=pl.BlockSpec((tm, d), lambda i: (i, 0)),
        out_shape=jax.ShapeDtypeStruct((n, d), F32), name=name,
        compiler_params=_cp(("parallel",)))(dy, w)


def dx_norm_bwd(dout, h, gamma, pieces, name):
    n, d = h.shape
    tm = _pick(n, 256, 8)
    np_ = len(pieces)

    def body(*refs):
        dout_ref, h_ref, g_ref = refs[:3]
        dys = refs[3:3 + np_]
        ws = refs[3 + np_:3 + 2 * np_]
        dh_ref, dg_ref = refs[3 + 2 * np_:]
        dxn = _dot_nt(dys[0][...].astype(BF16), ws[0][...])
        for p in range(1, np_):
            dxn = dxn + _dot_nt(dys[p][...].astype(BF16), ws[p][...])
        dx, dg = _rms_bwd(h_ref[...], g_ref[...], dxn)
        dh_ref[...] = dout_ref[...] + dx

        @pl.when(pl.program_id(0) == 0)
        def _():
            dg_ref[...] = jnp.zeros_like(dg_ref)
        dg_ref[...] += dg

    in_specs = [pl.BlockSpec((tm, d), lambda i: (i, 0)),
                pl.BlockSpec((tm, d), lambda i: (i, 0)),
                pl.BlockSpec((1, d), lambda i: (0, 0))]
    in_specs += [pl.BlockSpec((tm, dy.shape[1]), lambda i: (i, 0)) for dy, _ in pieces]
    in_specs += [_resident(w.shape, lambda i: (0, 0)) for _, w in pieces]
    return pl.pallas_call(
        body, grid=(n // tm,), in_specs=in_specs,
        out_specs=[pl.BlockSpec((tm, d), lambda i: (i, 0)), pl.BlockSpec((1, d), lambda i: (0, 0))],
        out_shape=[jax.ShapeDtypeStruct((n, d), F32), jax.ShapeDtypeStruct((1, d), F32)], name=name,
        compiler_params=_cp(("arbitrary",)))(dout, h, gamma, *[p[0] for p in pieces], *[p[1] for p in pieces])


def loss_head(h, gamma, target, name="loss_head"):
    n, d = h.shape
    tm = _pick(n, 512, 8)

    def body(h_ref, g_ref, t_ref, loss_ref, dh_ref, dg_ref):
        @pl.when(pl.program_id(0) == 0)
        def _():
            loss_ref[...] = jnp.zeros_like(loss_ref)
            dg_ref[...] = jnp.zeros_like(dg_ref)
        x = h_ref[...]
        g = g_ref[...]
        e = _rms(x, g) - t_ref[...]
        part = jnp.sum(jnp.sum(e * e, axis=-1, keepdims=True), axis=0, keepdims=True) * (0.5 / d)
        loss_ref[...] += jnp.broadcast_to(part, loss_ref.shape)
        dx, dg = _rms_bwd(x, g, e * (1.0 / d))
        dh_ref[...] = dx
        dg_ref[...] += dg

    return pl.pallas_call(
        body, grid=(n // tm,),
        in_specs=[pl.BlockSpec((tm, d), lambda i: (i, 0)), pl.BlockSpec((1, d), lambda i: (0, 0)),
                  pl.BlockSpec((tm, d), lambda i: (i, 0))],
        out_specs=[pl.BlockSpec((8, LANES), lambda i: (0, 0)), pl.BlockSpec((tm, d), lambda i: (i, 0)),
                   pl.BlockSpec((1, d), lambda i: (0, 0))],
        out_shape=[jax.ShapeDtypeStruct((8, LANES), F32), jax.ShapeDtypeStruct((n, d), F32),
                   jax.ShapeDtypeStruct((1, d), F32)], name=name,
        compiler_params=_cp(("arbitrary",)))(h, gamma, target)


def _halo_rows(dtype):
    return HALO * (4 // jnp.dtype(dtype).itemsize)


def _prev_halo_map(t, hb=HALO):
    return lambda i: (jnp.maximum(i * (t // hb) - 1, 0), 0)


def _next_halo_map(t, n, hb=HALO):
    return lambda i: (jnp.minimum((i + 1) * (t // hb), n // hb - 1), 0)


def _fill_prev(xs, main_ref, halo_ref, i, cols=slice(None)):
    hb = halo_ref.shape[0]
    xs[0:HALO, :] = jnp.where(i > 0, halo_ref[hb - HALO:hb, cols].astype(F32), 0.0)
    xs[HALO:, :] = main_ref[:, cols].astype(F32)


def _causal_conv(xs, w_ref, width, t, cols=slice(None)):
    x = xs[...]
    acc = w_ref[width - 1:width, cols] * x[HALO:, :]
    for k in range(width - 1):
        acc = acc + w_ref[k:k + 1, cols] * pltpu.roll(x, width - 1 - k, axis=0)[HALO:, :]
    return acc


def _col_chunks(width, target=COL_CHUNK):
    tc = _pick(width, target)
    return [slice(j * tc, (j + 1) * tc) for j in range(width // tc)]


def ffn_act_down(up, conv_w, conv_b, w_down, h, name):
    n, c2 = up.shape
    ff = c2 // 2
    d = h.shape[1]
    t = _pick(n, 2 * FFN_TILE, 8)
    hb = _halo_rows(up.dtype)
    chunks = _col_chunks(ff)
    tc = chunks[0].stop

    def body(up_ref, halo_ref, cw_ref, cb_ref, wd_ref, h_ref, o_ref, act_ref, hc_ref, xg, xv):
        i = pl.program_id(0)
        acc = h_ref[...]
        for cs in chunks:
            vs = slice(ff + cs.start, ff + cs.stop)
            _fill_prev(xg, up_ref, halo_ref, i, cs)
            _fill_prev(xv, up_ref, halo_ref, i, vs)
            gate = _causal_conv(xg, cw_ref, F_CONV, t, cs) + cb_ref[:, cs]
            val = _causal_conv(xv, cw_ref, F_CONV, t, vs) + cb_ref[:, vs]
            hc_ref[:, cs] = gate.astype(BF16)
            hc_ref[:, vs] = val.astype(BF16)
            act = (gate * _sig(gate) * val).astype(BF16)
            act_ref[:, cs] = act
            acc = acc + _dot(act, wd_ref[cs, :])
        o_ref[...] = acc

    return pl.pallas_call(
        body, grid=(n // t,),
        in_specs=[pl.BlockSpec((t, c2), lambda i: (i, 0)),
                  pl.BlockSpec((hb, c2), _prev_halo_map(t, hb)),
                  pl.BlockSpec((F_CONV, c2), lambda i: (0, 0)),
                  pl.BlockSpec((1, c2), lambda i: (0, 0)),
                  _resident((ff, d), lambda i: (0, 0)),
                  pl.BlockSpec((t, d), lambda i: (i, 0))],
        out_specs=[pl.BlockSpec((t, d), lambda i: (i, 0)), pl.BlockSpec((t, ff), lambda i: (i, 0)),
                   pl.BlockSpec((t, c2), lambda i: (i, 0))],
        out_shape=[jax.ShapeDtypeStruct((n, d), F32), jax.ShapeDtypeStruct((n, ff), BF16),
                   jax.ShapeDtypeStruct((n, c2), BF16)],
        scratch_shapes=[pltpu.VMEM((t + HALO, tc), F32), pltpu.VMEM((t + HALO, tc), F32)], name=name,
        compiler_params=_cp(("parallel",)))(up, up, conv_w, conv_b, w_down, h)


def ffn_bwd_act(dout, hc, w_down, name):
    n, c2 = hc.shape
    ff = c2 // 2
    d = dout.shape[1]
    t = _pick(n, 2 * FFN_TILE, 8)
    chunks = _col_chunks(ff)

    def body(dout_ref, hc_ref, wd_ref, dhc_ref, dcb_ref):
        i = pl.program_id(0)

        @pl.when(i == 0)
        def _():
            dcb_ref[...] = jnp.zeros_like(dcb_ref)
        doutb = dout_ref[...].astype(BF16)
        for cs in chunks:
            vs = slice(ff + cs.start, ff + cs.stop)
            gate = hc_ref[:, cs].astype(F32)
            val = hc_ref[:, vs].astype(F32)
            sg = _sig(gate)
            da = _dot_nt(doutb, wd_ref[cs, :])
            dgate = da * val * (sg * (1.0 + gate * (1.0 - sg)))
            dval = da * gate * sg
            dhc_ref[:, cs] = dgate.astype(BF16)
            dhc_ref[:, vs] = dval.astype(BF16)
            dcb_ref[:, cs] += jnp.sum(dgate, axis=0, keepdims=True)
            dcb_ref[:, vs] += jnp.sum(dval, axis=0, keepdims=True)

    return pl.pallas_call(
        body, grid=(n // t,),
        in_specs=[pl.BlockSpec((t, d), lambda i: (i, 0)),
                  pl.BlockSpec((t, c2), lambda i: (i, 0)),
                  _resident((ff, d), lambda i: (0, 0))],
        out_specs=[pl.BlockSpec((t, c2), lambda i: (i, 0)), pl.BlockSpec((1, c2), lambda i: (0, 0))],
        out_shape=[jax.ShapeDtypeStruct((n, c2), BF16), jax.ShapeDtypeStruct((1, c2), F32)], name=name,
        compiler_params=_cp(("arbitrary",)))(dout, hc, w_down)


def conv_bwd_tail(dy_ref, dnext_ref, x_ref, cw_ref, dcw_ref, ds, width, t, i, last, cols=slice(None)):
    ds[0:t, :] = dy_ref[:, cols].astype(F32)
    ds[t:, :] = jnp.where(i < last, dnext_ref[0:HALO, cols].astype(F32), 0.0)
    x = x_ref[:, cols].astype(F32)
    dall = ds[...]
    dx = None
    for k in range(width):
        off = width - 1 - k
        shifted = dall[0:t, :] if off == 0 else pltpu.roll(dall, t + HALO - off, axis=0)[0:t, :]
        term = cw_ref[k:k + 1, cols] * shifted
        dx = term if dx is None else dx + term
        dcw_ref[k:k + 1, cols] += jnp.sum(shifted * x, axis=0, keepdims=True)
    return dx


def ffn_bwd_up(dhc, up, conv_w, w_up, h, dout, gamma, name):
    n, c2 = up.shape
    d = h.shape[1]
    t = _pick(n, FFN_TILE, 8)
    last = n // t - 1
    chunks = _col_chunks(c2)
    tc = chunks[0].stop

    def body(dhc_ref, dnext_ref, up_ref, cw_ref, wu_ref, h_ref, dout_ref, g_ref,
             dh_ref, dup_ref, dcw_ref, dg_ref, ds):
        i = pl.program_id(0)

        @pl.when(i == 0)
        def _():
            dcw_ref[...] = jnp.zeros_like(dcw_ref)
            dg_ref[...] = jnp.zeros_like(dg_ref)
        dxn = jnp.zeros((t, d), F32)
        for cs in chunks:
            dup = conv_bwd_tail(dhc_ref, dnext_ref, up_ref, cw_ref, dcw_ref, ds, F_CONV, t, i, last, cs)
            dupb = dup.astype(BF16)
            dup_ref[:, cs] = dupb
            dxn = dxn + _dot_nt(dupb, wu_ref[:, cs])
        dx, dg = _rms_bwd(h_ref[...], g_ref[...], dxn)
        dh_ref[...] = dout_ref[...] + dx
        dg_ref[...] += dg

    return pl.pallas_call(
        body, grid=(n // t,),
        in_specs=[pl.BlockSpec((t, c2), lambda i: (i, 0)),
                  pl.BlockSpec((_halo_rows(dhc.dtype), c2), _next_halo_map(t, n, _halo_rows(dhc.dtype))),
                  pl.BlockSpec((t, c2), lambda i: (i, 0)),
                  pl.BlockSpec((F_CONV, c2), lambda i: (0, 0)),
                  _resident((d, c2), lambda i: (0, 0)),
                  pl.BlockSpec((t, d), lambda i: (i, 0)),
                  pl.BlockSpec((t, d), lambda i: (i, 0)),
                  pl.BlockSpec((1, d), lambda i: (0, 0))],
        out_specs=[pl.BlockSpec((t, d), lambda i: (i, 0)), pl.BlockSpec((t, c2), lambda i: (i, 0)),
                   pl.BlockSpec((F_CONV, c2), lambda i: (0, 0)), pl.BlockSpec((1, d), lambda i: (0, 0))],
        out_shape=[jax.ShapeDtypeStruct((n, d), F32), jax.ShapeDtypeStruct((n, c2), BF16),
                   jax.ShapeDtypeStruct((F_CONV, c2), F32), jax.ShapeDtypeStruct((1, d), F32)],
        scratch_shapes=[pltpu.VMEM((t + HALO, tc), F32)], name=name,
        compiler_params=_cp(("arbitrary",)))(dhc, dhc, up, conv_w, w_up, h, dout, gamma)


def _gdn_head(uq, uk, uv, pba, alog, dtb, head, n_heads):
    lane = lax.broadcasted_iota(jnp.int32, pba.shape, 1)
    sq = uq * _sig(uq)
    q = sq * lax.rsqrt(jnp.sum(sq * sq, axis=-1, keepdims=True) + EPS) * (A_HEAD ** -0.5)
    sk = uk * _sig(uk)
    k = sk * lax.rsqrt(jnp.sum(sk * sk, axis=-1, keepdims=True) + EPS)
    v = uv * _sig(uv)
    beta = jnp.sum(jnp.where(lane == head, _sig(pba), 0.0), axis=-1, keepdims=True)
    g_all = -jnp.exp(alog) * _softplus(pba + dtb)
    g = jnp.sum(jnp.where(lane == n_heads + head, g_all, 0.0), axis=-1, keepdims=True)
    return q, k, v, jnp.broadcast_to(beta, uq.shape), jnp.broadcast_to(g, uq.shape)


def gdn_prep(proj, conv_w, alog, dtb, n_heads, name):
    n = proj.shape[0]
    qk = n_heads * A_HEAD
    cw = 3 * qk
    ba_blk = (cw + qk) // LANES
    t = _pick(n, 256, 8)

    def body(x_ref, halo_ref, pba_ref, cw_ref, al_ref, dt_ref, q_ref, k_ref, v_ref, b_ref, g_ref, xs):
        i = pl.program_id(0)
        xs[0:HALO, :] = jnp.where(i > 0, halo_ref[...], 0.0)
        xs[HALO:, :] = x_ref[...]
        u = _causal_conv(xs, cw_ref, A_CONV, t)
        pba = pba_ref[...]
        for hd in range(n_heads):
            s0 = slice(hd * A_HEAD, (hd + 1) * A_HEAD)
            s1 = slice(qk + hd * A_HEAD, qk + (hd + 1) * A_HEAD)
            s2 = slice(2 * qk + hd * A_HEAD, 2 * qk + (hd + 1) * A_HEAD)
            q, k, v, bb, gb = _gdn_head(u[:, s0], u[:, s1], u[:, s2], pba, al_ref[...], dt_ref[...], hd, n_heads)
            q_ref[:, s0] = q
            k_ref[:, s0] = k
            v_ref[:, s0] = v
            b_ref[:, s0] = bb
            g_ref[:, s0] = gb

    out = jax.ShapeDtypeStruct((n, qk), F32)
    return pl.pallas_call(
        body, grid=(n // t,),
        in_specs=[pl.BlockSpec((t, cw), lambda i: (i, 0)),
                  pl.BlockSpec((HALO, cw), _prev_halo_map(t)),
                  pl.BlockSpec((t, LANES), lambda i: (i, ba_blk)),
                  pl.BlockSpec((A_CONV, cw), lambda i: (0, 0)),
                  pl.BlockSpec((1, LANES), lambda i: (0, 0)),
                  pl.BlockSpec((1, LANES), lambda i: (0, 0))],
        out_specs=[pl.BlockSpec((t, qk), lambda i: (i, 0))] * 5,
        out_shape=[out] * 5,
        scratch_shapes=[pltpu.VMEM((t + HALO, cw), F32)], name=name,
        compiler_params=_cp(("parallel",)))(proj, proj, proj, conv_w, alog, dtb)


def gdn_prep_bwd(proj, conv_w, alog, dtb, dq, dk, dv, dbb, dgb, n_heads, name):
    n = proj.shape[0]
    qk = n_heads * A_HEAD
    cw = 3 * qk
    ba_blk = (cw + qk) // LANES
    t = _pick(n, 256, 8)

    def body(x_ref, halo_ref, pba_ref, cw_ref, al_ref, dt_ref, dq_ref, dk_ref, dv_ref, dbb_ref, dgb_ref,
             du_ref, dba_ref, dal_ref, ddt_ref, xs):
        i = pl.program_id(0)
        xs[0:HALO, :] = jnp.where(i > 0, halo_ref[...], 0.0)
        xs[HALO:, :] = x_ref[...]
        u = _causal_conv(xs, cw_ref, A_CONV, t)
        pba = pba_ref[...]
        lane0 = lax.broadcasted_iota(jnp.int32, (t, A_HEAD), 1) == 0
        dba = jnp.zeros((t, LANES), F32)
        dal = jnp.zeros((1, LANES), F32)
        ddt = jnp.zeros((1, LANES), F32)
        for hd in range(n_heads):
            s0 = slice(hd * A_HEAD, (hd + 1) * A_HEAD)
            s1 = slice(qk + hd * A_HEAD, qk + (hd + 1) * A_HEAD)
            s2 = slice(2 * qk + hd * A_HEAD, 2 * qk + (hd + 1) * A_HEAD)
            fn = functools.partial(_gdn_head, head=hd, n_heads=n_heads)
            _, vjp = jax.vjp(fn, u[:, s0], u[:, s1], u[:, s2], pba, al_ref[...], dt_ref[...])
            cts = (dq_ref[:, s0], dk_ref[:, s0], dv_ref[:, s0],
                   jnp.where(lane0, dbb_ref[:, s0], 0.0), jnp.where(lane0, dgb_ref[:, s0], 0.0))
            duq, duk, duv, dpba, da, dd = vjp(cts)
            du_ref[:, s0] = duq
            du_ref[:, s1] = duk
            du_ref[:, s2] = duv
            dba = dba + dpba
            dal = dal + da
            ddt = ddt + dd
        dba_ref[...] = dba

        @pl.when(i == 0)
        def _():
            dal_ref[...] = jnp.zeros_like(dal_ref)
            ddt_ref[...] = jnp.zeros_like(ddt_ref)
        dal_ref[...] += dal
        ddt_ref[...] += ddt

    tok = pl.BlockSpec((t, qk), lambda i: (i, 0))
    row = pl.BlockSpec((1, LANES), lambda i: (0, 0))
    return pl.pallas_call(
        body, grid=(n // t,),
        in_specs=[pl.BlockSpec((t, cw), lambda i: (i, 0)),
                  pl.BlockSpec((HALO, cw), _prev_halo_map(t)),
                  pl.BlockSpec((t, LANES), lambda i: (i, ba_blk)),
                  pl.BlockSpec((A_CONV, cw), lambda i: (0, 0)), row, row,
                  tok, tok, tok, tok, tok],
        out_specs=[pl.BlockSpec((t, cw), lambda i: (i, 0)), pl.BlockSpec((t, LANES), lambda i: (i, 0)), row, row],
        out_shape=[jax.ShapeDtypeStruct((n, cw), F32), jax.ShapeDtypeStruct((n, LANES), F32),
                   jax.ShapeDtypeStruct((1, LANES), F32), jax.ShapeDtypeStruct((1, LANES), F32)],
        scratch_shapes=[pltpu.VMEM((t + HALO, cw), F32)], name=name,
        compiler_params=_cp(("arbitrary",)))(proj, proj, proj, conv_w, alog, dtb, dq, dk, dv, dbb, dgb)


def conv_bwd(du, x, conv_w, width, name):
    n, cw = du.shape
    t = _pick(n, 256, 8)
    last = n // t - 1

    chunks = _col_chunks(cw)
    tc = chunks[0].stop

    def body(du_ref, dnext_ref, x_ref, cw_ref, dx_ref, dcw_ref, ds):
        i = pl.program_id(0)

        @pl.when(i == 0)
        def _():
            dcw_ref[...] = jnp.zeros_like(dcw_ref)
        for cs in chunks:
            dx_ref[:, cs] = conv_bwd_tail(du_ref, dnext_ref, x_ref, cw_ref, dcw_ref, ds, width, t, i, last, cs)

    return pl.pallas_call(
        body, grid=(n // t,),
        in_specs=[pl.BlockSpec((t, cw), lambda i: (i, 0)),
                  pl.BlockSpec((HALO, cw), _next_halo_map(t, n)),
                  pl.BlockSpec((t, cw), lambda i: (i, 0)),
                  pl.BlockSpec((width, cw), lambda i: (0, 0))],
        out_specs=[pl.BlockSpec((t, cw), lambda i: (i, 0)), pl.BlockSpec((width, cw), lambda i: (0, 0))],
        out_shape=[jax.ShapeDtypeStruct((n, cw), F32), jax.ShapeDtypeStruct((width, cw), F32)],
        scratch_shapes=[pltpu.VMEM((t + HALO, tc), F32)], name=name,
        compiler_params=_cp(("arbitrary",)))(du, du, x, conv_w)


def _b(x):
    return x.astype(BF16)


def _mm_nn(a, b):
    return _dot(_b(a), _b(b))


def _mm_nt(a, b):
    return _dot_nt(_b(a), _b(b))


def _mm_tn(a, b):
    return _dot_tn(_b(a), _b(b))


@jax.custom_vjp
def _mmg_nn(a, b):
    return _mm_nn(a, b)


_mmg_nn.defvjp(lambda a, b: (_mm_nn(a, b), (a, b)),
               lambda res, dc: (_mm_nt(dc, res[1]), _mm_tn(res[0], dc)))


@jax.custom_vjp
def _mmg_nt(a, b):
    return _mm_nt(a, b)


_mmg_nt.defvjp(lambda a, b: (_mm_nt(a, b), (a, b)),
               lambda res, dc: (_mm_nn(dc, res[1]), _mm_tn(dc, res[0])))


@jax.custom_vjp
def _mmg_tn(a, b):
    return _mm_tn(a, b)


_mmg_tn.defvjp(lambda a, b: (_mm_tn(a, b), (a, b)),
               lambda res, dc: (_mm_nt(res[1], dc), _mm_nn(res[0], dc)))


def _each(f, *lists):
    return [f(*a) for a in zip(*lists)]


def _unit_lower_inv(ms):
    c = ms[0].shape[0]
    eye = jnp.where(lax.broadcasted_iota(jnp.int32, (c, c), 0) == lax.broadcasted_iota(jnp.int32, (c, c), 1), 1.0, 0.0)
    xs = [eye - m for m in ms]
    pws = _each(_mm_nn, ms, ms)
    for it in range(5):
        xs = _each(lambda x, pw: x + _mm_nn(x, pw), xs, pws)
        if it < 4:
            pws = _each(_mm_nn, pws, pws)
    rs = _each(lambda m, x: eye - x - _hdot(m, x), ms, xs)
    return _each(lambda x, r: x + _mm_nn(x, r), xs, rs)


@jax.custom_vjp
def _unit_lower_inv_g(ms):
    return _unit_lower_inv(ms)


_unit_lower_inv_g.defvjp(lambda ms: (lambda xs: (xs, xs))(_unit_lower_inv(ms)),
                         lambda xs, dxs: (_each(lambda t, x: -_mm_nt(t, x), _each(_mm_tn, xs, dxs), xs),))

_GDN_OPS = (_mm_nn, _mm_nt, _mm_tn, _unit_lower_inv)
_GDN_OPS_GRAD = (_mmg_nn, _mmg_nt, _mmg_tn, _unit_lower_inv_g)


def _gdn_chunk(ops, state, q, k, v, bb, gb):
    nn, nt, tn, inv = ops
    c = CHUNK
    ri = lax.broadcasted_iota(jnp.int32, (c, c), 0)
    ci = lax.broadcasted_iota(jnp.int32, (c, c), 1)
    causal = ri >= ci
    strict = ri > ci
    tri = jnp.where(causal, 1.0, 0.0)
    gc = [_hdot(tri, g) for g in gb]
    decay = [jnp.where(causal, jnp.exp(jnp.where(causal, x[:, :c] - x.T[:c, :], 0.0)), 0.0) for x in gc]
    kb = _each(lambda a, b: a * b, k, bb)
    kk = _each(nt, kb, k)
    m = _each(lambda a, d: jnp.where(strict, a * d, 0.0), kk, decay)
    tinv = inv(m)
    egc = [jnp.exp(x) for x in gc]
    u = _each(nn, tinv, _each(lambda a, b: a * b, v, bb))
    w = _each(nn, tinv, _each(lambda a, b: a * b, kb, egc))
    attn = _each(lambda a, d: a * d, _each(nt, q, k), decay)
    glast = [jnp.sum(g, axis=0, keepdims=True) for g in gb]
    ws = _each(nn, w, state)
    v_new = _each(lambda a, b: a - b, u, ws)
    qs = _each(nn, _each(lambda a, b: a * b, q, egc), state)
    av = _each(nn, attn, v_new)
    o = _each(lambda a, b: a + b, qs, av)
    kv = _each(tn, _each(lambda a, gl, x: a * jnp.exp(gl - x), k, glast, gc), v_new)
    new_state = _each(lambda s, gl, a: s * jnp.exp(gl) + a, state, glast, kv)
    return o, new_state


def gdn_fwd(q, k, v, bb, gb, n_heads, name):
    n, w = q.shape
    nc = n // CHUNK
    cb = min(8, nc)
    rows = cb * CHUNK

    def body(q_ref, k_ref, v_ref, b_ref, g_ref, o_ref, st_ref, s_scr):
        @pl.when(pl.program_id(0) == 0)
        def _():
            s_scr[...] = jnp.zeros_like(s_scr)

        def step(c, carry):
            sl = pl.ds(pl.multiple_of(c * CHUNK, CHUNK), CHUNK)
            lanes = [slice(hd * A_HEAD, (hd + 1) * A_HEAD) for hd in range(n_heads)]
            state = [s_scr[hd] for hd in range(n_heads)]
            o, new_state = _gdn_chunk(_GDN_OPS, state, *[[r[sl, ls] for ls in lanes]
                                                        for r in (q_ref, k_ref, v_ref, b_ref, g_ref)])
            for hd, ls in enumerate(lanes):
                st_ref[hd, pl.ds(c, 1)] = state[hd][None]
                o_ref[sl, ls] = o[hd]
                s_scr[hd] = new_state[hd]
            return carry

        lax.fori_loop(0, cb, step, 0)

    tok = pl.BlockSpec((rows, w), lambda j: (j, 0))
    return pl.pallas_call(
        body, grid=(nc // cb,),
        in_specs=[tok] * 5,
        out_specs=[tok, pl.BlockSpec((n_heads, cb, A_HEAD, A_HEAD), lambda j: (0, j, 0, 0))],
        out_shape=[jax.ShapeDtypeStruct(q.shape, F32), jax.ShapeDtypeStruct((n_heads, nc, A_HEAD, A_HEAD), F32)],
        scratch_shapes=[pltpu.VMEM((n_heads, A_HEAD, A_HEAD), F32)], name=name,
        compiler_params=_cp(("arbitrary",)))(q, k, v, bb, gb)


def gdn_bwd(q, k, v, bb, gb, states, do, n_heads, name):
    n, w = q.shape
    nc = n // CHUNK
    cb = min(4, nc)
    rows = cb * CHUNK
    nblk = nc // cb
    chunk_fn = functools.partial(_gdn_chunk, _GDN_OPS_GRAD)

    def body(q_ref, k_ref, v_ref, b_ref, g_ref, st_ref, do_ref, dq_ref, dk_ref, dv_ref, db_ref, dg_ref, ds_scr):
        @pl.when(pl.program_id(0) == 0)
        def _():
            ds_scr[...] = jnp.zeros_like(ds_scr)

        def step(s, carry):
            c = cb - 1 - s
            sl = pl.ds(pl.multiple_of(c * CHUNK, CHUNK), CHUNK)
            lanes = [slice(hd * A_HEAD, (hd + 1) * A_HEAD) for hd in range(n_heads)]
            state = [st_ref[hd, pl.ds(c, 1)][0] for hd in range(n_heads)]
            _, vjp = jax.vjp(chunk_fn, state, *[[r[sl, ls] for ls in lanes]
                                                for r in (q_ref, k_ref, v_ref, b_ref, g_ref)])
            dstate, dq, dk, dv, dbb, dgb = vjp(([do_ref[sl, ls] for ls in lanes],
                                                [ds_scr[hd] for hd in range(n_heads)]))
            for hd, ls in enumerate(lanes):
                ds_scr[hd] = dstate[hd]
                dq_ref[sl, ls] = dq[hd]
                dk_ref[sl, ls] = dk[hd]
                dv_ref[sl, ls] = dv[hd]
                db_ref[sl, ls] = jnp.broadcast_to(jnp.sum(dbb[hd], axis=-1, keepdims=True), dbb[hd].shape)
                dg_ref[sl, ls] = jnp.broadcast_to(jnp.sum(dgb[hd], axis=-1, keepdims=True), dgb[hd].shape)
            return carry

        lax.fori_loop(0, cb, step, 0)

    tok = pl.BlockSpec((rows, w), lambda j: (nblk - 1 - j, 0))
    out = jax.ShapeDtypeStruct(q.shape, F32)
    return pl.pallas_call(
        body, grid=(nblk,),
        in_specs=[tok] * 5 + [pl.BlockSpec((n_heads, cb, A_HEAD, A_HEAD), lambda j: (0, nblk - 1 - j, 0, 0)), tok],
        out_specs=[tok] * 5, out_shape=[out] * 5,
        scratch_shapes=[pltpu.VMEM((n_heads, A_HEAD, A_HEAD), F32)], name=name,
        compiler_params=_cp(("arbitrary",)))(q, k, v, bb, gb, states, do)


def _gdn_gate(oh, zh, w):
    r = lax.rsqrt(jnp.mean(oh * oh, axis=-1, keepdims=True) + EPS)
    return oh * r * w * (zh * _sig(zh))


def gdn_out(o, proj, out_norm, w_out, h, n_heads, name):
    n, vw = o.shape
    d = h.shape[1]
    z_blk = 3 * vw // vw
    t = _pick(n, 512, 8)

    def body(o_ref, z_ref, w_ref, wo_ref, h_ref, out_ref, y_ref):
        for hd in range(n_heads):
            s0 = slice(hd * A_HEAD, (hd + 1) * A_HEAD)
            y_ref[:, s0] = _gdn_gate(o_ref[:, s0], z_ref[:, s0], w_ref[...]).astype(BF16)
        out_ref[...] = h_ref[...] + _dot(y_ref[...], wo_ref[...])

    return pl.pallas_call(
        body, grid=(n // t,),
        in_specs=[pl.BlockSpec((t, vw), lambda i: (i, 0)),
                  pl.BlockSpec((t, vw), lambda i: (i, z_blk)),
                  pl.BlockSpec((1, A_HEAD), lambda i: (0, 0)),
                  _resident((vw, d), lambda i: (0, 0)),
                  pl.BlockSpec((t, d), lambda i: (i, 0))],
        out_specs=[pl.BlockSpec((t, d), lambda i: (i, 0)), pl.BlockSpec((t, vw), lambda i: (i, 0))],
        out_shape=[jax.ShapeDtypeStruct((n, d), F32), jax.ShapeDtypeStruct((n, vw), BF16)], name=name,
        compiler_params=_cp(("parallel",)))(o, proj, out_norm, w_out, h)


def gdn_out_bwd(dout, o, proj, out_norm, w_out, n_heads, name):
    n, vw = o.shape
    d = dout.shape[1]
    z_blk = 3
    t = _pick(n, 512, 8)

    def body(dout_ref, o_ref, z_ref, w_ref, wo_ref, do_ref, dz_ref, dw_ref):
        dy = _dot_nt(dout_ref[...].astype(BF16), wo_ref[...])
        dw = jnp.zeros((1, A_HEAD), F32)
        for hd in range(n_heads):
            s0 = slice(hd * A_HEAD, (hd + 1) * A_HEAD)
            _, vjp = jax.vjp(_gdn_gate, o_ref[:, s0], z_ref[:, s0], w_ref[...])
            doh, dzh, dwh = vjp(dy[:, s0])
            do_ref[:, s0] = doh
            dz_ref[:, s0] = dzh
            dw = dw + dwh

        @pl.when(pl.program_id(0) == 0)
        def _():
            dw_ref[...] = jnp.zeros_like(dw_ref)
        dw_ref[...] += dw

    tok = pl.BlockSpec((t, vw), lambda i: (i, 0))
    return pl.pallas_call(
        body, grid=(n // t,),
        in_specs=[pl.BlockSpec((t, d), lambda i: (i, 0)), tok,
                  pl.BlockSpec((t, vw), lambda i: (i, z_blk)),
                  pl.BlockSpec((1, A_HEAD), lambda i: (0, 0)),
                  _resident((vw, d), lambda i: (0, 0))],
        out_specs=[tok, tok, pl.BlockSpec((1, A_HEAD), lambda i: (0, 0))],
        out_shape=[jax.ShapeDtypeStruct((n, vw), F32), jax.ShapeDtypeStruct((n, vw), F32),
                   jax.ShapeDtypeStruct((1, A_HEAD), F32)], name=name,
        compiler_params=_cp(("arbitrary",)))(dout, o, proj, out_norm, w_out)


def _bias_onehot(i, clip, tbl_pad):
    m = lax.broadcasted_iota(jnp.int32, (BAND_PAD, tbl_pad), 0)
    r = lax.broadcasted_iota(jnp.int32, (BAND_PAD, tbl_pad), 1)
    idx = jnp.clip(i + LEFT_CHUNKS * CHUNK - m, -clip, clip) + clip
    return jnp.where((r == idx) & (m < BAND), 1.0, 0.0)


def bias_expand(tbl, clip, name):
    nh, tp = tbl.shape

    def body(t_ref, o_ref):
        oh = _bias_onehot(pl.program_id(0), clip, tp)
        o_ref[0] = _hdot_nt(t_ref[...], oh)

    return pl.pallas_call(
        body, grid=(CHUNK,),
        in_specs=[pl.BlockSpec((nh, tp), lambda i: (0, 0))],
        out_specs=pl.BlockSpec((1, nh, BAND_PAD), lambda i: (i, 0, 0)),
        out_shape=jax.ShapeDtypeStruct((CHUNK, nh, BAND_PAD), F32), name=name,
        compiler_params=_cp(("parallel",)))(tbl)


def bias_expand_bwd(dbias, clip, tp, name):
    _, nh, _ = dbias.shape

    def body(d_ref, o_ref):
        @pl.when(pl.program_id(0) == 0)
        def _():
            o_ref[...] = jnp.zeros_like(o_ref)
        oh = _bias_onehot(pl.program_id(0), clip, tp)
        o_ref[...] += _hdot(d_ref[0], oh)

    return pl.pallas_call(
        body, grid=(CHUNK,),
        in_specs=[pl.BlockSpec((1, nh, BAND_PAD), lambda i: (i, 0, 0))],
        out_specs=pl.BlockSpec((nh, tp), lambda i: (0, 0)),
        out_shape=jax.ShapeDtypeStruct((nh, tp), F32), name=name,
        compiler_params=_cp(("arbitrary",)))(dbias)


ATT_TILE = LEFT_CHUNKS * CHUNK


ATT_GROUP = 8


def _att_softmax(s, bias, n_chunk):
    s = s * (B_HEAD ** -0.5) + bias
    slot = lax.broadcasted_iota(jnp.int32, s.shape, 1)
    valid = (slot >= (LEFT_CHUNKS - n_chunk) * CHUNK) & (slot < BAND)
    s = jnp.where(valid, s, NEG_INF)
    p = jnp.exp(s - jnp.max(s, axis=-1, keepdims=True))
    return p / jnp.sum(p, axis=-1, keepdims=True)


def _att_specs(n_pairs):
    prev = lambda p, i: (jnp.maximum(i - 1, 0), p)
    cur = lambda p, i: (i, p)
    prev_v = lambda p, i: (jnp.maximum(i - 1, 0), n_pairs + p)
    cur_v = lambda p, i: (i, n_pairs + p)
    blk = (ATT_TILE, LANES)
    return [pl.BlockSpec(blk, prev), pl.BlockSpec(blk, cur), pl.BlockSpec(blk, prev_v), pl.BlockSpec(blk, cur_v)]


def _att_fill(kbuf, vbuf, kp_ref, kc_ref, vp_ref, vc_ref):
    t = ATT_TILE
    kbuf[0:t, :] = kp_ref[...].astype(BF16)
    kbuf[t:2 * t, :] = kc_ref[...].astype(BF16)
    kbuf[2 * t:, :] = jnp.zeros((CHUNK, LANES), BF16)
    vbuf[0:t, :] = vp_ref[...].astype(BF16)
    vbuf[t:2 * t, :] = vc_ref[...].astype(BF16)
    vbuf[2 * t:, :] = jnp.zeros((CHUNK, LANES), BF16)


def attn_fwd(qp, kv, bias, name):
    n, bw = qp.shape
    n_pairs = bw // LANES
    t = ATT_TILE
    cpt = t // CHUNK

    def body(q_ref, kp_ref, kc_ref, vp_ref, vc_ref, b_ref, o_ref, kbuf, vbuf):
        i = pl.program_id(1)
        _att_fill(kbuf, vbuf, kp_ref, kc_ref, vp_ref, vc_ref)
        lane = lax.broadcasted_iota(jnp.int32, (CHUNK, LANES), 1)
        mine = [(lane >= hh * B_HEAD) & (lane < (hh + 1) * B_HEAD) for hh in range(2)]
        for g0 in range(0, cpt, ATT_GROUP):
            units = [(c, hh) for c in range(g0, min(g0 + ATT_GROUP, cpt)) for hh in range(2)]
            band = {c: slice(c * CHUNK, c * CHUNK + BAND_PAD) for c, _ in units}
            s = [_dot_nt(jnp.where(mine[hh], q_ref[c * CHUNK:(c + 1) * CHUNK, :], 0.0).astype(BF16), kbuf[band[c], :])
                 for c, hh in units]
            p = [_att_softmax(s_u, b_ref[0, :, hh * BAND_PAD:(hh + 1) * BAND_PAD], i * cpt + c)
                 for s_u, (c, hh) in zip(s, units)]
            o = [_dot(p_u.astype(BF16), vbuf[band[c], :]) for p_u, (c, hh) in zip(p, units)]
            for u in range(0, len(units), 2):
                c = units[u][0]
                o_ref[c * CHUNK:(c + 1) * CHUNK, :] = jnp.where(lane < B_HEAD, o[u], o[u + 1])

    return pl.pallas_call(
        body, grid=(n_pairs, n // t),
        in_specs=[pl.BlockSpec((t, LANES), lambda p, i: (i, p))] + _att_specs(n_pairs)
        + [pl.BlockSpec((1, CHUNK, 2 * BAND_PAD), lambda p, i: (p, 0, 0))],
        out_specs=pl.BlockSpec((t, LANES), lambda p, i: (i, p)),
        out_shape=jax.ShapeDtypeStruct((n, bw), F32),
        scratch_shapes=[pltpu.VMEM((2 * t + CHUNK, LANES), BF16), pltpu.VMEM((2 * t + CHUNK, LANES), BF16)],
        name=name, compiler_params=_cp(("parallel", "parallel")))(qp, kv, kv, kv, kv, bias)


def attn_bwd(qp, kv, bias, d_o, dk_in, dv_in, name):
    n, bw = qp.shape
    n_pairs = bw // LANES
    t = ATT_TILE
    cpt = t // CHUNK
    nt = n // t
    have_in = dk_in is not None
    scale = B_HEAD ** -0.5

    def body(*refs):
        q_ref, kp_ref, kc_ref, vp_ref, vc_ref, b_ref, do_ref = refs[:7]
        pos = 7
        if have_in:
            dki_ref, dvi_ref = refs[7:9]
            pos = 9
        dq_ref, dk_ref, dv_ref, db_ref, kbuf, vbuf, dkacc, dvacc = refs[pos:]
        j = pl.program_id(1)
        i = nt - 1 - j
        _att_fill(kbuf, vbuf, kp_ref, kc_ref, vp_ref, vc_ref)

        @pl.when(j == 0)
        def _():
            dkacc[...] = jnp.zeros_like(dkacc)
            dvacc[...] = jnp.zeros_like(dvacc)
            db_ref[...] = jnp.zeros_like(db_ref)

        @pl.when(j > 0)
        def _():
            dkacc[t:2 * t, :] = dkacc[0:t, :]
            dvacc[t:2 * t, :] = dvacc[0:t, :]
            dkacc[0:t, :] = jnp.zeros((t, LANES), F32)
            dvacc[0:t, :] = jnp.zeros((t, LANES), F32)

        lane = lax.broadcasted_iota(jnp.int32, (CHUNK, LANES), 1)
        mine = [(lane >= hh * B_HEAD) & (lane < (hh + 1) * B_HEAD) for hh in range(2)]
        for g0 in range(0, cpt, ATT_GROUP):
            units = [(c, hh) for c in range(g0, min(g0 + ATT_GROUP, cpt)) for hh in range(2)]
            rows = {c: slice(c * CHUNK, (c + 1) * CHUNK) for c, _ in units}
            band = {c: slice(c * CHUNK, c * CHUNK + BAND_PAD) for c, _ in units}
            bsl = [slice(hh * BAND_PAD, (hh + 1) * BAND_PAD) for hh in range(2)]
            qm = [jnp.where(mine[hh], q_ref[rows[c], :], 0.0).astype(BF16) for c, hh in units]
            dom = [jnp.where(mine[hh], do_ref[rows[c], :], 0.0).astype(BF16) for c, hh in units]
            s = [_dot_nt(q_u, kbuf[band[c], :]) for q_u, (c, hh) in zip(qm, units)]
            dp = [_dot_nt(d_u, vbuf[band[c], :]) for d_u, (c, hh) in zip(dom, units)]
            p = [_att_softmax(s_u, b_ref[0, :, bsl[hh]], i * cpt + c) for s_u, (c, hh) in zip(s, units)]
            ds = [p_u * (dp_u - jnp.sum(dp_u * p_u, axis=-1, keepdims=True)) for p_u, dp_u in zip(p, dp)]
            dsb = [(d_u * scale).astype(BF16) for d_u in ds]
            dv = [_dot_tn(p_u.astype(BF16), d_u) for p_u, d_u in zip(p, dom)]
            dq = [_dot(d_u, kbuf[band[c], :]) for d_u, (c, hh) in zip(dsb, units)]
            dk = [_dot_tn(d_u, q_u) for d_u, q_u in zip(dsb, qm)]
            for u, (c, hh) in enumerate(units):
                db_ref[0, :, bsl[hh]] += ds[u]
                dvacc[band[c], :] += dv[u]
                dkacc[band[c], :] += dk[u]
            for u in range(0, len(units), 2):
                dq_ref[rows[units[u][0]], :] = jnp.where(lane < B_HEAD, dq[u], dq[u + 1])

        if have_in:
            dk_ref[...] = dkacc[t:2 * t, :] + dki_ref[...]
            dv_ref[...] = dvacc[t:2 * t, :] + dvi_ref[...]
        else:
            dk_ref[...] = dkacc[t:2 * t, :]
            dv_ref[...] = dvacc[t:2 * t, :]

    rev = lambda p, j: (nt - 1 - j, p)
    tok = pl.BlockSpec((t, LANES), rev)
    kv_specs = [pl.BlockSpec((t, LANES), lambda p, j: (jnp.maximum(nt - 2 - j, 0), p)),
                pl.BlockSpec((t, LANES), rev),
                pl.BlockSpec((t, LANES), lambda p, j: (jnp.maximum(nt - 2 - j, 0), n_pairs + p)),
                pl.BlockSpec((t, LANES), lambda p, j: (nt - 1 - j, n_pairs + p))]
    in_specs = [tok] + kv_specs + [pl.BlockSpec((1, CHUNK, 2 * BAND_PAD), lambda p, j: (p, 0, 0)), tok]
    args = [qp, kv, kv, kv, kv, bias, d_o]
    if have_in:
        in_specs += [tok, tok]
        args += [dk_in, dv_in]
    out = jax.ShapeDtypeStruct((n, bw), F32)
    return pl.pallas_call(
        body, grid=(n_pairs, nt), in_specs=in_specs,
        out_specs=[tok, tok, tok, pl.BlockSpec((1, CHUNK, 2 * BAND_PAD), lambda p, j: (p, 0, 0))],
        out_shape=[out, out, out, jax.ShapeDtypeStruct((n_pairs, CHUNK, 2 * BAND_PAD), F32)],
        scratch_shapes=[pltpu.VMEM((2 * t + CHUNK, LANES), BF16), pltpu.VMEM((2 * t + CHUNK, LANES), BF16),
                        pltpu.VMEM((2 * t + CHUNK, LANES), F32), pltpu.VMEM((2 * t + CHUNK, LANES), F32)],
        name=name, compiler_params=_cp(("parallel", "arbitrary")))(*args)


def adamw(w, gstack, m, v, name):
    r, c = w.shape
    s = gstack.shape[0]
    tr = _pick(r, 512, 8)

    def body(w_ref, g_ref, m_ref, v_ref, go_ref, d_ref, mo_ref, vo_ref):
        g = g_ref[0].astype(F32)
        for k in range(1, s):
            g = g + g_ref[k].astype(F32)
        mn = ADAM_B1 * m_ref[...] + (1.0 - ADAM_B1) * g
        vn = ADAM_B2 * v_ref[...] + (1.0 - ADAM_B2) * (g * g)
        m_hat = mn / (1.0 - ADAM_B1 ** ADAM_STEP)
        v_hat = vn / (1.0 - ADAM_B2 ** ADAM_STEP)
        go_ref[...] = g
        d_ref[...] = -ADAM_LR * (m_hat / (jnp.sqrt(v_hat) + ADAM_EPS) + ADAM_WD * w_ref[...])
        mo_ref[...] = mn
        vo_ref[...] = vn

    blk = pl.BlockSpec((tr, c), lambda i: (i, 0))
    out = jax.ShapeDtypeStruct((r, c), F32)
    return pl.pallas_call(
        body, grid=(r // tr,),
        in_specs=[blk, pl.BlockSpec((s, tr, c), lambda i: (0, i, 0)), blk, blk],
        out_specs=[blk] * 4, out_shape=[out] * 4, name=name,
        compiler_params=_cp(("parallel",)))(w, gstack, m, v)


def _place():
    x, y, c = lax.axis_index("x"), lax.axis_index("y"), lax.axis_index("c")
    chips = [(1 - x, y), (x, 1 - y), (1 - x, 1 - y)]
    return x, y, c, chips


def all_gather(shards, name):
    nt = len(shards)
    any_spec = pl.BlockSpec(memory_space=pl.ANY)

    def body(*refs):
        ins = refs[:nt]
        outs = refs[nt:2 * nt]
        send_sems, recv_sems, local_sems = refs[2 * nt:]
        x, y, c, chips = _place()
        me, sibling = (x, y, c), (x, y, 1 - c)

        def slot(t, dev):
            return outs[t].at[4 * dev[0] + 2 * dev[1] + dev[2]]

        def copy(t, k, block, to, src=None):
            return pltpu.make_async_remote_copy(
                src_ref=slot(t, block) if src is None else src, dst_ref=slot(t, block),
                send_sem=send_sems.at[7 * t + k], recv_sem=recv_sems.at[7 * t + k],
                device_id=to, device_id_type=MESH)

        local, remote = [], []
        for t in range(nt):
            mine = pltpu.make_async_copy(ins[t], slot(t, me), local_sems.at[t])
            mine.start()
            local.append(mine)
            first = [copy(t, 0, me, sibling, src=ins[t])]
            first += [copy(t, 1 + j, me, (*chip, c), src=ins[t]) for j, chip in enumerate(chips)]
            for cp in first:
                cp.start()
            remote += first
        for t in range(nt):
            for j, chip in enumerate(chips):
                copy(t, 1 + j, (*chip, c), me).wait_recv()
                fwd = copy(t, 4 + j, (*chip, c), sibling)
                fwd.start()
                remote.append(fwd)
        for t in range(nt):
            copy(t, 0, sibling, me).wait_recv()
            for j, chip in enumerate(chips):
                copy(t, 4 + j, (*chip, 1 - c), me).wait_recv()
        for cp in remote:
            cp.wait_send()
        for cp in local:
            cp.wait()

    return pl.pallas_call(
        body, in_specs=[any_spec] * nt, out_specs=[any_spec] * nt,
        out_shape=[jax.ShapeDtypeStruct((N_DEV, *s.shape), s.dtype) for s in shards],
        scratch_shapes=[pltpu.SemaphoreType.DMA((7 * nt,)), pltpu.SemaphoreType.DMA((7 * nt,)),
                        pltpu.SemaphoreType.DMA((nt,))],
        name=name)(*shards)


def rs_sibling(g8s, name):
    nt = len(g8s)
    any_spec = pl.BlockSpec(memory_space=pl.ANY)

    def body(*refs):
        ins = refs[:nt]
        outs = refs[nt:2 * nt]
        send_sems, recv_sems = refs[2 * nt:]
        x, y, c, _ = _place()
        copies = []
        for t in range(nt):
            for q in range(4):
                cp = pltpu.make_async_remote_copy(
                    src_ref=ins[t].at[2 * q + (1 - c)], dst_ref=outs[t].at[q],
                    send_sem=send_sems.at[4 * t + q], recv_sem=recv_sems.at[4 * t + q],
                    device_id=(x, y, 1 - c), device_id_type=MESH)
                cp.start()
                copies.append(cp)
        for cp in copies:
            cp.wait_recv()
        for cp in copies:
            cp.wait_send()

    return pl.pallas_call(
        body, in_specs=[any_spec] * nt, out_specs=[any_spec] * nt,
        out_shape=[jax.ShapeDtypeStruct((4, *g.shape[1:]), g.dtype) for g in g8s],
        scratch_shapes=[pltpu.SemaphoreType.DMA((4 * nt,)), pltpu.SemaphoreType.DMA((4 * nt,))],
        name=name)(*g8s)


def pair_add(g8, recv, c_idx, name):
    _, r, c = g8.shape
    tr = _pick(r, 512, 8)

    def body(c_ref, g_ref, r_ref, o_ref):
        o_ref[...] = (g_ref[...] + r_ref[...]).astype(BF16)

    return pl.pallas_call(
        body,
        grid_spec=pltpu.PrefetchScalarGridSpec(
            num_scalar_prefetch=1, grid=(4, r // tr),
            in_specs=[pl.BlockSpec((1, tr, c), lambda q, i, cr: (2 * q + cr[0], i, 0)),
                      pl.BlockSpec((1, tr, c), lambda q, i, cr: (q, i, 0))],
            out_specs=pl.BlockSpec((1, tr, c), lambda q, i, cr: (q, i, 0))),
        out_shape=jax.ShapeDtypeStruct((4, r, c), BF16), name=name,
        compiler_params=_cp(("parallel", "parallel")))(c_idx, g8, recv)


def rs_chips(parts, name):
    nt = len(parts)
    any_spec = pl.BlockSpec(memory_space=pl.ANY)

    def body(*refs):
        ins = refs[:nt]
        outs = refs[nt:2 * nt]
        send_sems, recv_sems, local_sems = refs[2 * nt:]
        x, y, c, chips = _place()
        mine = 2 * x + y
        local, remote = [], []
        for t in range(nt):
            for j, chip in enumerate(chips):
                cp = pltpu.make_async_remote_copy(
                    src_ref=ins[t].at[2 * chip[0] + chip[1]], dst_ref=outs[t].at[mine],
                    send_sem=send_sems.at[3 * t + j], recv_sem=recv_sems.at[3 * t + j],
                    device_id=(*chip, c), device_id_type=MESH)
                cp.start()
                remote.append(cp)
            own = pltpu.make_async_copy(ins[t].at[mine], outs[t].at[mine], local_sems.at[t])
            own.start()
            local.append(own)
        for t in range(nt):
            for j, chip in enumerate(chips):
                pltpu.make_async_remote_copy(
                    src_ref=ins[t].at[mine], dst_ref=outs[t].at[2 * chip[0] + chip[1]],
                    send_sem=send_sems.at[3 * t + j], recv_sem=recv_sems.at[3 * t + j],
                    device_id=(*chip, c), device_id_type=MESH).wait_recv()
        for cp in remote:
            cp.wait_send()
        for cp in local:
            cp.wait()

    return pl.pallas_call(
        body, in_specs=[any_spec] * nt, out_specs=[any_spec] * nt,
        out_shape=[jax.ShapeDtypeStruct(p.shape, p.dtype) for p in parts],
        scratch_shapes=[pltpu.SemaphoreType.DMA((3 * nt,)), pltpu.SemaphoreType.DMA((3 * nt,)),
                        pltpu.SemaphoreType.DMA((nt,))],
        name=name)(*parts)


def all_reduce_small(pack, name):
    r, c = pack.shape

    def body(x_ref, o_ref, buf, send_sems, recv_sems, local_sem):
        x, y, cc, chips = _place()
        me, sibling = (x, y, cc), (x, y, 1 - cc)

        def slot(dev):
            return buf.at[4 * dev[0] + 2 * dev[1] + dev[2]]

        def copy(k, block, to, src=None):
            return pltpu.make_async_remote_copy(
                src_ref=slot(block) if src is None else src, dst_ref=slot(block),
                send_sem=send_sems.at[k], recv_sem=recv_sems.at[k], device_id=to, device_id_type=MESH)

        mine = pltpu.make_async_copy(x_ref, slot(me), local_sem)
        mine.start()
        first = [copy(0, me, sibling, src=x_ref)]
        first += [copy(1 + j, me, (*chip, cc), src=x_ref) for j, chip in enumerate(chips)]
        for cp in first:
            cp.start()
        passed = [copy(4 + j, (*chip, cc), sibling) for j, chip in enumerate(chips)]
        for j, chip in enumerate(chips):
            copy(1 + j, (*chip, cc), me).wait_recv()
            passed[j].start()
        copy(0, sibling, me).wait_recv()
        for j, chip in enumerate(chips):
            copy(4 + j, (*chip, 1 - cc), me).wait_recv()
        for cp in first + passed:
            cp.wait_send()
        mine.wait()
        acc = buf[0]
        for k in range(1, N_DEV):
            acc = acc + buf[k]
        o_ref[...] = acc

    return pl.pallas_call(
        body, in_specs=[pl.BlockSpec(memory_space=pltpu.VMEM)],
        out_specs=pl.BlockSpec(memory_space=pltpu.VMEM),
        out_shape=jax.ShapeDtypeStruct((r, c), F32),
        scratch_shapes=[pltpu.VMEM((N_DEV, r, c), F32), pltpu.SemaphoreType.DMA((7,)),
                        pltpu.SemaphoreType.DMA((7,)), pltpu.SemaphoreType.DMA],
        name=name, compiler_params=_cp())(pack)


def _row(v):
    return v.reshape(1, -1)


def _lane_row(vals, offset):
    return jnp.pad(vals, (offset, LANES - offset - vals.shape[0])).reshape(1, LANES)


def _bias_to_pairs(b):
    i, nh, bp = b.shape
    return b.reshape(i, nh // 2, 2, bp).transpose(1, 0, 2, 3).reshape(nh // 2, i, 2 * bp)


def _bias_from_pairs(b):
    p, i, bp2 = b.shape
    return b.reshape(p, i, 2, bp2 // 2).transpose(1, 0, 2, 3).reshape(i, 2 * p, bp2 // 2)


def local_step(x, target, W):
    n, d = x.shape
    la, ha = W["a_A_log"].shape
    lb, hb, tbl = W["b_rel_bias"].shape
    depth = W["f_norm"].shape[0]
    clip = (tbl - 1) // 2
    tp = -(-tbl // LANES) * LANES
    qk = ha * A_HEAD
    cw = 3 * qk
    bw = hb * B_HEAD
    a_in = cw + qk + 2 * ha

    h = x
    saves = []
    kv = h_kv = None
    for l in range(depth):
        sv = {"h_in": h}
        if l < la:
            alog = _lane_row(W["a_A_log"][l], ha)
            dtb = _lane_row(W["a_dt_bias"][l], ha)
            proj = norm_matmul(h, _row(W["a_norm"][l]), W["a_w_in"][l], f"a_in_proj")
            q, k, v, bb, gb = gdn_prep(proj, W["a_conv"][l], alog, dtb, ha, "gdn_prep")
            o, states = gdn_fwd(q, k, v, bb, gb, ha, "gdn_fwd")
            h, y = gdn_out(o, proj, _row(W["a_out_norm"][l]), W["a_w_out"][l], h, ha, "gdn_out")
            sv.update(proj=proj, q=q, k=k, v=v, bb=bb, gb=gb, states=states, o=o, y=y, alog=alog, dtb=dtb)
        else:
            j = l - la
            if j == 0:
                h_kv = h
                kv = norm_matmul(h, _row(W["kv_norm"]), W["w_kv"], "kv_proj")
            qp = norm_matmul(h, _row(W["b_norm"][j]), W["b_w_q"][j], "b_q_proj")
            tblp = jnp.pad(W["b_rel_bias"][j], ((0, 0), (0, tp - tbl)))
            bias = _bias_to_pairs(bias_expand(tblp, clip, "bias_expand"))
            o = attn_fwd(qp, kv, bias, "attn_fwd")
            h = matmul_res(o, W["b_w_out"][j], h, "b_out_proj")
            sv.update(qp=qp, bias=bias, o=o)
        sv["h_mid"] = h
        up = norm_matmul(h, _row(W["f_norm"][l]), W["f_w_up"][l], "f_up_proj", out_dtype=BF16)
        h, act, hc = ffn_act_down(up, W["f_conv"][l], _row(W["f_conv_b"][l]), W["f_w_down"][l], h, "ffn_act_down")
        sv.update(up=up, act=act, hc=hc)
        saves.append(sv)

    loss, dh, d_final = loss_head(h, _row(W["final_norm"]), target)

    G = {k_: [None] * (la if k_.startswith("a_") else lb if k_.startswith("b_") else depth)
         for k_ in ("a_norm", "a_w_in", "a_conv", "a_A_log", "a_dt_bias", "a_out_norm", "a_w_out",
                    "b_norm", "b_w_q", "b_rel_bias", "b_w_out", "f_norm", "f_w_up", "f_conv", "f_conv_b", "f_w_down")}
    G["final_norm"] = d_final[0]
    dk_acc = dv_acc = None
    for l in reversed(range(depth)):
        sv = saves[l]
        dhc, dcb = ffn_bwd_act(dh, sv["hc"], W["f_w_down"][l], "ffn_bwd_act")
        G["f_w_down"][l] = matmul_tn(sv["act"], dh, "f_down_wgrad")
        G["f_conv_b"][l] = dcb[0]
        dh, dup, dcw, dg = ffn_bwd_up(dhc, sv["up"], W["f_conv"][l], W["f_w_up"][l], sv["h_mid"], dh,
                                      _row(W["f_norm"][l]), "ffn_bwd_up")
        G["f_conv"][l] = dcw
        G["f_norm"][l] = dg[0]
        G["f_w_up"][l] = norm_matmul_tn(sv["h_mid"], _row(W["f_norm"][l]), dup, "f_up_wgrad")
        if l < la:
            w_in = W["a_w_in"][l]
            do, dz, dwn = gdn_out_bwd(dh, sv["o"], sv["proj"], _row(W["a_out_norm"][l]), W["a_w_out"][l], ha, "gdn_out_bwd")
            G["a_out_norm"][l] = dwn[0]
            G["a_w_out"][l] = matmul_tn(sv["y"], dh, "a_out_wgrad")
            dq, dk, dv, dbb, dgb = gdn_bwd(sv["q"], sv["k"], sv["v"], sv["bb"], sv["gb"], sv["states"], do, ha, "gdn_bwd")
            du, dba, dal, ddt = gdn_prep_bwd(sv["proj"], W["a_conv"][l], sv["alog"], sv["dtb"],
                                             dq, dk, dv, dbb, dgb, ha, "gdn_prep_bwd")
            G["a_A_log"][l] = dal[0, ha:2 * ha]
            G["a_dt_bias"][l] = ddt[0, ha:2 * ha]
            dqkv, dconv = conv_bwd(du, sv["proj"], W["a_conv"][l], A_CONV, "gdn_conv_bwd")
            G["a_conv"][l] = dconv
            gam = _row(W["a_norm"][l])
            pieces = [(dqkv, w_in[:, :cw]), (dz, w_in[:, cw:cw + qk]), (dba, w_in[:, cw + qk:])]
            G["a_w_in"][l] = jnp.concatenate(
                [norm_matmul_tn(sv["h_in"], gam, dqkv, "a_in_wgrad_qkv"),
                 norm_matmul_tn(sv["h_in"], gam, dz, "a_in_wgrad_z"),
                 norm_matmul_tn(sv["h_in"], gam, dba, "a_in_wgrad_ba")[:, :2 * ha]], axis=1)
            dh, dg = dx_norm_bwd(dh, sv["h_in"], gam, pieces, "a_in_dx")
            G["a_norm"][l] = dg[0]
        else:
            j = l - la
            d_o = matmul_nt(dh, W["b_w_out"][j], "b_out_dx")
            G["b_w_out"][j] = matmul_tn(sv["o"], dh, "b_out_wgrad")
            dq, dk_acc, dv_acc, dbias = attn_bwd(sv["qp"], kv, sv["bias"], d_o, dk_acc, dv_acc,
                                                 "attn_bwd" if dk_acc is None else "attn_bwd_acc")
            G["b_rel_bias"][j] = bias_expand_bwd(_bias_from_pairs(dbias), clip, tp, "bias_expand_bwd")[:, :tbl]
            gam = _row(W["b_norm"][j])
            G["b_w_q"][j] = norm_matmul_tn(sv["h_in"], gam, dq, "b_q_wgrad")
            dh, dg = dx_norm_bwd(dh, sv["h_in"], gam, [(dq, W["b_w_q"][j])], "b_q_dx")
            G["b_norm"][j] = dg[0]
            if j == 0:
                gam = _row(W["kv_norm"])
                G["w_kv"] = jnp.concatenate([norm_matmul_tn(h_kv, gam, dk_acc, "kv_wgrad_k"),
                                             norm_matmul_tn(h_kv, gam, dv_acc, "kv_wgrad_v")], axis=1)
                dh, dg = dx_norm_bwd(dh, h_kv, gam, [(dk_acc, W["w_kv"][:, :bw]), (dv_acc, W["w_kv"][:, bw:])], "kv_dx")
                G["kv_norm"] = dg[0]
    out = {k_: (jnp.stack(v_) if isinstance(v_, list) else v_) for k_, v_ in G.items()}
    out["a_w_in"] = out["a_w_in"][:, :, :a_in]
    return loss[0, 0], dh, out


WEIGHTS = ["a_norm", "a_w_in", "a_conv", "a_A_log", "a_dt_bias", "a_out_norm", "a_w_out", "kv_norm", "w_kv",
           "b_norm", "b_w_q", "b_rel_bias", "b_w_out", "f_norm", "f_w_up", "f_conv", "f_conv_b", "f_w_down",
           "final_norm"]
SHARD_AXIS = {"a_norm": 1, "a_w_in": 2, "a_conv": 2, "a_w_out": 1, "w_kv": 1, "b_w_q": 1, "b_w_out": 1,
              "f_w_up": 2, "f_conv": 2, "f_w_down": 1}
BIG = ["a_w_in", "a_w_out", "w_kv", "b_w_q", "b_w_out", "f_w_up", "f_w_down"]
SMALL_SHARDED = ["a_norm", "a_conv", "f_conv"]


def _unstack(g, axis):
    return jnp.concatenate([g[i] for i in range(N_DEV)], axis=axis)


def _to_blocks(full, axis):
    parts = jnp.stack(jnp.split(full, N_DEV, axis=axis))
    return parts.reshape(N_DEV, -1, parts.shape[-1])


def _pack(arrs):
    flat = []
    for a in arrs:
        f = a.reshape(-1)
        flat.append(jnp.pad(f, (0, (-f.shape[0]) % LANES)))
    f = jnp.concatenate(flat)
    f = jnp.pad(f, (0, (-f.shape[0]) % (8 * LANES)))
    return f.reshape(-1, LANES)


def _unpack(pack, shapes):
    flat = pack.reshape(-1)
    out, pos = [], 0
    for s in shapes:
        sz = math.prod(s)
        out.append(flat[pos:pos + sz].reshape(s))
        pos += sz + (-sz) % LANES
    return out


def _as2d(a):
    return a.reshape(1, -1) if a.ndim == 1 else a.reshape(-1, a.shape[-1])


def kernel(x, a_norm, a_w_in, a_conv, a_A_log, a_dt_bias, a_out_norm, a_w_out, kv_norm, w_kv, b_norm, b_w_q, b_rel_bias, b_w_out, f_norm, f_w_up, f_conv, f_conv_b, f_w_down, final_norm, loss_target, m_a_norm, m_a_w_in, m_a_conv, m_a_A_log, m_a_dt_bias, m_a_out_norm, m_a_w_out, m_kv_norm, m_w_kv, m_b_norm, m_b_w_q, m_b_rel_bias, m_b_w_out, m_f_norm, m_f_w_up, m_f_conv, m_f_conv_b, m_f_w_down, m_final_norm, v_a_norm, v_a_w_in, v_a_conv, v_a_A_log, v_a_dt_bias, v_a_out_norm, v_a_w_out, v_kv_norm, v_w_kv, v_b_norm, v_b_w_q, v_b_rel_bias, v_b_w_out, v_f_norm, v_f_w_up, v_f_conv, v_f_conv_b, v_f_w_down, v_final_norm):
    w = dict(a_norm=a_norm, a_w_in=a_w_in, a_conv=a_conv, a_A_log=a_A_log, a_dt_bias=a_dt_bias,
             a_out_norm=a_out_norm, a_w_out=a_w_out, kv_norm=kv_norm, w_kv=w_kv, b_norm=b_norm, b_w_q=b_w_q,
             b_rel_bias=b_rel_bias, b_w_out=b_w_out, f_norm=f_norm, f_w_up=f_w_up, f_conv=f_conv,
             f_conv_b=f_conv_b, f_w_down=f_w_down, final_norm=final_norm)
    mom = dict(a_norm=m_a_norm, a_w_in=m_a_w_in, a_conv=m_a_conv, a_A_log=m_a_A_log, a_dt_bias=m_a_dt_bias,
               a_out_norm=m_a_out_norm, a_w_out=m_a_w_out, kv_norm=m_kv_norm, w_kv=m_w_kv, b_norm=m_b_norm,
               b_w_q=m_b_w_q, b_rel_bias=m_b_rel_bias, b_w_out=m_b_w_out, f_norm=m_f_norm, f_w_up=m_f_w_up,
               f_conv=m_f_conv, f_conv_b=m_f_conv_b, f_w_down=m_f_w_down, final_norm=m_final_norm)
    var = dict(a_norm=v_a_norm, a_w_in=v_a_w_in, a_conv=v_a_conv, a_A_log=v_a_A_log, a_dt_bias=v_a_dt_bias,
               a_out_norm=v_a_out_norm, a_w_out=v_a_w_out, kv_norm=v_kv_norm, w_kv=v_w_kv, b_norm=v_b_norm,
               b_w_q=v_b_w_q, b_rel_bias=v_b_rel_bias, b_w_out=v_b_w_out, f_norm=v_f_norm, f_w_up=v_f_w_up,
               f_conv=v_f_conv, f_conv_b=v_f_conv_b, f_w_down=v_f_w_down, final_norm=v_final_norm)
    me = 4 * lax.axis_index("x") + 2 * lax.axis_index("y") + lax.axis_index("c")

    small_shapes = [w[k].shape for k in SMALL_SHARDED]
    gathered = all_gather([w[k].astype(BF16) for k in BIG] + [_pack([w[k] for k in SMALL_SHARDED])], "weights_all_gather")
    full = dict(w)
    for k, g in zip(BIG, gathered[:-1]):
        full[k] = _unstack(g, SHARD_AXIS[k])
    small = [_unpack(gathered[-1][i], small_shapes) for i in range(N_DEV)]
    for idx, k in enumerate(SMALL_SHARDED):
        full[k] = jnp.concatenate([small[i][idx] for i in range(N_DEV)], axis=SHARD_AXIS[k])
    a_in = full["a_w_in"].shape[2]
    full["a_w_in"] = jnp.pad(full["a_w_in"], ((0, 0), (0, 0), (0, (-a_in) % LANES)))

    loss_part, grad_x, G = local_step(x[0], loss_target[0], full)

    g8 = [_to_blocks(G[k], SHARD_AXIS[k]) for k in BIG]
    c_idx = lax.axis_index("c").astype(jnp.int32).reshape(1)
    from_sibling = rs_sibling(g8, "grads_to_sibling")
    parts = [pair_add(g, r, c_idx, "grads_pair_add") for g, r in zip(g8, from_sibling)]
    stacks = rs_chips(parts, "grads_to_chips")

    small_names = [k for k in WEIGHTS if k not in BIG]
    reduced = _unpack(all_reduce_small(_pack([G[k] for k in small_names] + [loss_part.reshape(1)]), "small_all_reduce"),
                      [G[k].shape for k in small_names] + [(1,)])
    loss = reduced[-1][0]
    small_g = dict(zip(small_names, reduced[:-1]))
    for k in SMALL_SHARDED:
        sz = w[k].shape[SHARD_AXIS[k]]
        small_g[k] = lax.dynamic_slice_in_dim(small_g[k], me * sz, sz, axis=SHARD_AXIS[k])

    res = {}
    for k, st in zip(BIG, stacks):
        outs = adamw(_as2d(w[k]), st, _as2d(mom[k]), _as2d(var[k]), "adamw_" + k)
        res[k] = [o.reshape(w[k].shape) for o in outs]
    for k in small_names:
        outs = adamw(_as2d(w[k]), _as2d(small_g[k])[None], _as2d(mom[k]), _as2d(var[k]), "adamw_" + k)
        res[k] = [o.reshape(w[k].shape) for o in outs]

    return (loss, grad_x[None], *[res[k][0] for k in WEIGHTS], *[res[k][1] for k in WEIGHTS],
            *[res[k][2] for k in WEIGHTS], *[res[k][3] for k in WEIGHTS])
```

```python
import functools
import math

import jax
import jax.numpy as jnp
from jax import lax
from jax.experimental import pallas as pl
from jax.experimental.pallas import tpu as pltpu

F32 = jnp.float32
BF16 = jnp.bfloat16
HI = lax.Precision.HIGHEST
MESH = pl.DeviceIdType.MESH

EPS = 1e-6
NEG_INF = -1e30
CHUNK = 64
LEFT_CHUNKS = 8
BAND = (LEFT_CHUNKS + 1) * CHUNK
BAND_PAD = 640
A_CONV = 4
F_CONV = 3
A_HEAD = 128
B_HEAD = 64
LANES = 128
HALO = 8
N_DEV = 8

ADAM_LR = 0.001
ADAM_B1 = 0.9
ADAM_B2 = 0.999
ADAM_EPS = 1e-08
ADAM_WD = 0.01
ADAM_STEP = 10

VMEM_LIMIT_V7X = 56 * 1024 * 1024
GDN_BWD_HEADS = 8
COL_CHUNK = 256
FFN_TILE = 256


def _cp(sem=None, vmem=VMEM_LIMIT_V7X):
    kw = dict(vmem_limit_bytes=vmem)
    if sem is not None:
        kw["dimension_semantics"] = sem
    return pltpu.CompilerParams(**kw)


def _pick(n, target, q=LANES):
    best = None
    for t in range(q, min(n, target) + 1, q):
        if n % t == 0:
            best = t
    return best if best is not None else n


def _sig(x):
    return 1.0 / (1.0 + jnp.exp(-x))


def _softplus(x):
    return jnp.maximum(x, 0.0) + jnp.log(1.0 + jnp.exp(-jnp.abs(x)))


def _rms(x, g):
    return x * lax.rsqrt(jnp.mean(x * x, axis=-1, keepdims=True) + EPS) * g


def _rms_bwd(x, g, dxn):
    r = lax.rsqrt(jnp.mean(x * x, axis=-1, keepdims=True) + EPS)
    gd = dxn * g
    dx = r * gd - x * (r * r * r) * jnp.mean(x * gd, axis=-1, keepdims=True)
    dg = jnp.sum(dxn * x * r, axis=0, keepdims=True)
    return dx, dg


def _dot(a, b):
    return jnp.dot(a, b, preferred_element_type=F32)


def _dot_nt(a, b):
    return lax.dot_general(a, b, (((1,), (1,)), ((), ())), preferred_element_type=F32)


def _dot_tn(a, b):
    return lax.dot_general(a, b, (((0,), (0,)), ((), ())), preferred_element_type=F32)


def _hdot(a, b):
    return jnp.dot(a, b, precision=HI, preferred_element_type=F32)


def _hdot_nt(a, b):
    return lax.dot_general(a, b, (((1,), (1,)), ((), ())), precision=HI, preferred_element_type=F32)


def _hdot_tn(a, b):
    return lax.dot_general(a, b, (((0,), (0,)), ((), ())), precision=HI, preferred_element_type=F32)


def _resident(shape, index_map):
    return pl.BlockSpec(shape, index_map, pipeline_mode=pl.Buffered(1))


def norm_matmul(h, gamma, w, name, out_dtype=F32):
    n, d = h.shape
    nc = w.shape[1]
    tm = _pick(n, 512, 8)
    tn = _pick(nc, 1536)

    def body(h_ref, g_ref, w_ref, o_ref):
        xn = _rms(h_ref[...], g_ref[...])
        o_ref[...] = _dot(xn.astype(BF16), w_ref[...]).astype(out_dtype)

    return pl.pallas_call(
        body, grid=(nc // tn, n // tm),
        in_specs=[pl.BlockSpec((tm, d), lambda j, i: (i, 0)),
                  pl.BlockSpec((1, d), lambda j, i: (0, 0)),
                  pl.BlockSpec((d, tn), lambda j, i: (0, j))],
        out_specs=pl.BlockSpec((tm, tn), lambda j, i: (i, j)),
        out_shape=jax.ShapeDtypeStruct((n, nc), out_dtype), name=name,
        compiler_params=_cp(("parallel", "parallel")))(h, gamma, w)


def norm_matmul_tn(h, gamma, dy, name):
    n, d = h.shape
    nc = dy.shape[1]
    tm = _pick(n, 512, 8)
    tn = _pick(nc, 1536)

    def body(h_ref, g_ref, dy_ref, o_ref):
        @pl.when(pl.program_id(1) == 0)
        def _():
            o_ref[...] = jnp.zeros_like(o_ref)
        xn = _rms(h_ref[...], g_ref[...])
        o_ref[...] += _dot_tn(xn.astype(BF16), dy_ref[...].astype(BF16))

    return pl.pallas_call(
        body, grid=(nc // tn, n // tm),
        in_specs=[pl.BlockSpec((tm, d), lambda j, i: (i, 0)),
                  pl.BlockSpec((1, d), lambda j, i: (0, 0)),
                  pl.BlockSpec((tm, tn), lambda j, i: (i, j))],
        out_specs=pl.BlockSpec((d, tn), lambda j, i: (0, j)),
        out_shape=jax.ShapeDtypeStruct((d, nc), F32), name=name,
        compiler_params=_cp(("parallel", "arbitrary")))(h, gamma, dy)


def matmul_tn(a, dy, name):
    n, ka = a.shape
    nc = dy.shape[1]
    tm = _pick(n, 512, 8)
    tk = _pick(ka, 1536)
    tn = _pick(nc, 1024)

    def body(a_ref, dy_ref, o_ref):
        @pl.when(pl.program_id(2) == 0)
        def _():
            o_ref[...] = jnp.zeros_like(o_ref)
        o_ref[...] += _dot_tn(a_ref[...].astype(BF16), dy_ref[...].astype(BF16))

    return pl.pallas_call(
        body, grid=(ka // tk, nc // tn, n // tm),
        in_specs=[pl.BlockSpec((tm, tk), lambda k, j, i: (i, k)),
                  pl.BlockSpec((tm, tn), lambda k, j, i: (i, j))],
        out_specs=pl.BlockSpec((tk, tn), lambda k, j, i: (k, j)),
        out_shape=jax.ShapeDtypeStruct((ka, nc), F32), name=name,
        compiler_params=_cp(("parallel", "parallel", "arbitrary")))(a, dy)


def matmul_res(a, w, h, name):
    n, k = a.shape
    d = w.shape[1]
    tm = _pick(n, 512, 8)

    def body(a_ref, w_ref, h_ref, o_ref):
        o_ref[...] = h_ref[...] + _dot(a_ref[...].astype(BF16), w_ref[...])

    return pl.pallas_call(
        body, grid=(n // tm,),
        in_specs=[pl.BlockSpec((tm, k), lambda i: (i, 0)),
                  _resident((k, d), lambda i: (0, 0)),
                  pl.BlockSpec((tm, d), lambda i: (i, 0))],
        out_specs=pl.BlockSpec((tm, d), lambda i: (i, 0)),
        out_shape=jax.ShapeDtypeStruct((n, d), F32), name=name,
        compiler_params=_cp(("parallel",)))(a, w, h)


def matmul_nt(dy, w, name):
    n, k = dy.shape
    d = w.shape[0]
    tm = _pick(n, 512, 8)

    def body(dy_ref, w_ref, o_ref):
        o_ref[...] = _dot_nt(dy_ref[...].astype(BF16), w_ref[...])

    return pl.pallas_call(
        body, grid=(n // tm,),
        in_specs=[pl.BlockSpec((tm, k), lambda i: (i, 0)),
                  _resident((d, k), lambda i: (0, 0))],
        out_specs=pl.BlockSpec((tm, d), lambda i: (i, 0)),
        out_shape=jax.ShapeDtypeStruct((n, d), F32), name=name,
        compiler_params=_cp(("parallel",)))(dy, w)


def dx_norm_bwd(dout, h, gamma, pieces, name):
    n, d = h.shape
    tm = _pick(n, 256, 8)
    np_ = len(pieces)

    def body(*refs):
        dout_ref, h_ref, g_ref = refs[:3]
        dys = refs[3:3 + np_]
        ws = refs[3 + np_:3 + 2 * np_]
        dh_ref, dg_ref = refs[3 + 2 * np_:]
        dxn = _dot_nt(dys[0][...].astype(BF16), ws[0][...])
        for p in range(1, np_):
            dxn = dxn + _dot_nt(dys[p][...].astype(BF16), ws[p][...])
        dx, dg = _rms_bwd(h_ref[...], g_ref[...], dxn)
        dh_ref[...] = dout_ref[...] + dx

        @pl.when(pl.program_id(0) == 0)
        def _():
            dg_ref[...] = jnp.zeros_like(dg_ref)
        dg_ref[...] += dg

    in_specs = [pl.BlockSpec((tm, d), lambda i: (i, 0)),
                pl.BlockSpec((tm, d), lambda i: (i, 0)),
                pl.BlockSpec((1, d), lambda i: (0, 0))]
    in_specs += [pl.BlockSpec((tm, dy.shape[1]), lambda i: (i, 0)) for dy, _ in pieces]
    in_specs += [_resident(w.shape, lambda i: (0, 0)) for _, w in pieces]
    return pl.pallas_call(
        body, grid=(n // tm,), in_specs=in_specs,
        out_specs=[pl.BlockSpec((tm, d), lambda i: (i, 0)), pl.BlockSpec((1, d), lambda i: (0, 0))],
        out_shape=[jax.ShapeDtypeStruct((n, d), F32), jax.ShapeDtypeStruct((1, d), F32)], name=name,
        compiler_params=_cp(("arbitrary",)))(dout, h, gamma, *[p[0] for p in pieces], *[p[1] for p in pieces])


def loss_head(h, gamma, target, name="loss_head"):
    n, d = h.shape
    tm = _pick(n, 512, 8)

    def body(h_ref, g_ref, t_ref, loss_ref, dh_ref, dg_ref):
        @pl.when(pl.program_id(0) == 0)
        def _():
            loss_ref[...] = jnp.zeros_like(loss_ref)
            dg_ref[...] = jnp.zeros_like(dg_ref)
        x = h_ref[...]
        g = g_ref[...]
        e = _rms(x, g) - t_ref[...]
        part = jnp.sum(jnp.sum(e * e, axis=-1, keepdims=True), axis=0, keepdims=True) * (0.5 / d)
        loss_ref[...] += jnp.broadcast_to(part, loss_ref.shape)
        dx, dg = _rms_bwd(x, g, e * (1.0 / d))
        dh_ref[...] = dx
        dg_ref[...] += dg

    return pl.pallas_call(
        body, grid=(n // tm,),
        in_specs=[pl.BlockSpec((tm, d), lambda i: (i, 0)), pl.BlockSpec((1, d), lambda i: (0, 0)),
                  pl.BlockSpec((tm, d), lambda i: (i, 0))],
        out_specs=[pl.BlockSpec((8, LANES), lambda i: (0, 0)), pl.BlockSpec((tm, d), lambda i: (i, 0)),
                   pl.BlockSpec((1, d), lambda i: (0, 0))],
        out_shape=[jax.ShapeDtypeStruct((8, LANES), F32), jax.ShapeDtypeStruct((n, d), F32),
                   jax.ShapeDtypeStruct((1, d), F32)], name=name,
        compiler_params=_cp(("arbitrary",)))(h, gamma, target)


def _halo_rows(dtype):
    return HALO * (4 // jnp.dtype(dtype).itemsize)


def _prev_halo_map(t, hb=HALO):
    return lambda i: (jnp.maximum(i * (t // hb) - 1, 0), 0)


def _next_halo_map(t, n, hb=HALO):
    return lambda i: (jnp.minimum((i + 1) * (t // hb), n // hb - 1), 0)


def _fill_prev(xs, main_ref, halo_ref, i, cols=slice(None)):
    hb = halo_ref.shape[0]
    xs[0:HALO, :] = jnp.where(i > 0, halo_ref[hb - HALO:hb, cols].astype(F32), 0.0)
    xs[HALO:, :] = main_ref[:, cols].astype(F32)


def _causal_conv(xs, w_ref, width, t, cols=slice(None)):
    x = xs[...]
    acc = w_ref[width - 1:width, cols] * x[HALO:, :]
    for k in range(width - 1):
        acc = acc + w_ref[k:k + 1, cols] * pltpu.roll(x, width - 1 - k, axis=0)[HALO:, :]
    return acc


def _col_chunks(width, target=COL_CHUNK):
    tc = _pick(width, target)
    return [slice(j * tc, (j + 1) * tc) for j in range(width // tc)]


def ffn_act_down(up, conv_w, conv_b, w_down, h, name):
    n, c2 = up.shape
    ff = c2 // 2
    d = h.shape[1]
    t = _pick(n, 2 * FFN_TILE, 8)
    hb = _halo_rows(up.dtype)
    chunks = _col_chunks(ff)
    tc = chunks[0].stop

    def body(up_ref, halo_ref, cw_ref, cb_ref, wd_ref, h_ref, o_ref, act_ref, hc_ref, xg, xv):
        i = pl.program_id(0)
        acc = h_ref[...]
        for cs in chunks:
            vs = slice(ff + cs.start, ff + cs.stop)
            _fill_prev(xg, up_ref, halo_ref, i, cs)
            _fill_prev(xv, up_ref, halo_ref, i, vs)
            gate = _causal_conv(xg, cw_ref, F_CONV, t, cs) + cb_ref[:, cs]
            val = _causal_conv(xv, cw_ref, F_CONV, t, vs) + cb_ref[:, vs]
            hc_ref[:, cs] = gate.astype(BF16)
            hc_ref[:, vs] = val.astype(BF16)
            act = (gate * _sig(gate) * val).astype(BF16)
            act_ref[:, cs] = act
            acc = acc + _dot(act, wd_ref[cs, :])
        o_ref[...] = acc

    return pl.pallas_call(
        body, grid=(n // t,),
        in_specs=[pl.BlockSpec((t, c2), lambda i: (i, 0)),
                  pl.BlockSpec((hb, c2), _prev_halo_map(t, hb)),
                  pl.BlockSpec((F_CONV, c2), lambda i: (0, 0)),
                  pl.BlockSpec((1, c2), lambda i: (0, 0)),
                  _resident((ff, d), lambda i: (0, 0)),
                  pl.BlockSpec((t, d), lambda i: (i, 0))],
        out_specs=[pl.BlockSpec((t, d), lambda i: (i, 0)), pl.BlockSpec((t, ff), lambda i: (i, 0)),
                   pl.BlockSpec((t, c2), lambda i: (i, 0))],
        out_shape=[jax.ShapeDtypeStruct((n, d), F32), jax.ShapeDtypeStruct((n, ff), BF16),
                   jax.ShapeDtypeStruct((n, c2), BF16)],
        scratch_shapes=[pltpu.VMEM((t + HALO, tc), F32), pltpu.VMEM((t + HALO, tc), F32)], name=name,
        compiler_params=_cp(("parallel",)))(up, up, conv_w, conv_b, w_down, h)


def ffn_bwd_act(dout, hc, w_down, name):
    n, c2 = hc.shape
    ff = c2 // 2
    d = dout.shape[1]
    t = _pick(n, 2 * FFN_TILE, 8)
    chunks = _col_chunks(ff)

    def body(dout_ref, hc_ref, wd_ref, dhc_ref, dcb_ref):
        i = pl.program_id(0)

        @pl.when(i == 0)
        def _():
            dcb_ref[...] = jnp.zeros_like(dcb_ref)
        doutb = dout_ref[...].astype(BF16)
        for cs in chunks:
            vs = slice(ff + cs.start, ff + cs.stop)
            gate = hc_ref[:, cs].astype(F32)
            val = hc_ref[:, vs].astype(F32)
            sg = _sig(gate)
            da = _dot_nt(doutb, wd_ref[cs, :])
            dgate = da * val * (sg * (1.0 + gate * (1.0 - sg)))
            dval = da * gate * sg
            dhc_ref[:, cs] = dgate.astype(BF16)
            dhc_ref[:, vs] = dval.astype(BF16)
            dcb_ref[:, cs] += jnp.sum(dgate, axis=0, keepdims=True)
            dcb_ref[:, vs] += jnp.sum(dval, axis=0, keepdims=True)

    return pl.pallas_call(
        body, grid=(n // t,),
        in_specs=[pl.BlockSpec((t, d), lambda i: (i, 0)),
                  pl.BlockSpec((t, c2), lambda i: (i, 0)),
                  _resident((ff, d), lambda i: (0, 0))],
        out_specs=[pl.BlockSpec((t, c2), lambda i: (i, 0)), pl.BlockSpec((1, c2), lambda i: (0, 0))],
        out_shape=[jax.ShapeDtypeStruct((n, c2), BF16), jax.ShapeDtypeStruct((1, c2), F32)], name=name,
        compiler_params=_cp(("arbitrary",)))(dout, hc, w_down)


def conv_bwd_tail(dy_ref, dnext_ref, x_ref, cw_ref, dcw_ref, ds, width, t, i, last, cols=slice(None)):
    ds[0:t, :] = dy_ref[:, cols].astype(F32)
    ds[t:, :] = jnp.where(i < last, dnext_ref[0:HALO, cols].astype(F32), 0.0)
    x = x_ref[:, cols].astype(F32)
    dall = ds[...]
    dx = None
    for k in range(width):
        off = width - 1 - k
        shifted = dall[0:t, :] if off == 0 else pltpu.roll(dall, t + HALO - off, axis=0)[0:t, :]
        term = cw_ref[k:k + 1, cols] * shifted
        dx = term if dx is None else dx + term
        dcw_ref[k:k + 1, cols] += jnp.sum(shifted * x, axis=0, keepdims=True)
    return dx


def ffn_bwd_up(dhc, up, conv_w, w_up, h, dout, gamma, name, rider=None):
    n, c2 = up.shape
    d = h.shape[1]
    t = _pick(n, FFN_TILE, 8)
    last = n // t - 1
    chunks = _col_chunks(c2)
    tc = chunks[0].stop

    def body(dhc_ref, dnext_ref, up_ref, cw_ref, wu_ref, h_ref, dout_ref, g_ref,
             dh_ref, dup_ref, dcw_ref, dg_ref, ds):
        i = pl.program_id(0)

        @pl.when(i == 0)
        def _():
            dcw_ref[...] = jnp.zeros_like(dcw_ref)
            dg_ref[...] = jnp.zeros_like(dg_ref)
        dxn = jnp.zeros((t, d), F32)
        for cs in chunks:
            dup = conv_bwd_tail(dhc_ref, dnext_ref, up_ref, cw_ref, dcw_ref, ds, F_CONV, t, i, last, cs)
            dupb = dup.astype(BF16)
            dup_ref[:, cs] = dupb
            dxn = dxn + _dot_nt(dupb, wu_ref[:, cs])
        dx, dg = _rms_bwd(h_ref[...], g_ref[...], dxn)
        dh_ref[...] = dout_ref[...] + dx
        dg_ref[...] += dg

    return host_call(
        body, grid=(n // t,), rider=rider, sem=("arbitrary",), args=(dhc, dhc, up, conv_w, w_up, h, dout, gamma),
        in_specs=[pl.BlockSpec((t, c2), lambda i: (i, 0)),
                  pl.BlockSpec((_halo_rows(dhc.dtype), c2), _next_halo_map(t, n, _halo_rows(dhc.dtype))),
                  pl.BlockSpec((t, c2), lambda i: (i, 0)),
                  pl.BlockSpec((F_CONV, c2), lambda i: (0, 0)),
                  _resident((d, c2), lambda i: (0, 0)),
                  pl.BlockSpec((t, d), lambda i: (i, 0)),
                  pl.BlockSpec((t, d), lambda i: (i, 0)),
                  pl.BlockSpec((1, d), lambda i: (0, 0))],
        out_specs=[pl.BlockSpec((t, d), lambda i: (i, 0)), pl.BlockSpec((t, c2), lambda i: (i, 0)),
                   pl.BlockSpec((F_CONV, c2), lambda i: (0, 0)), pl.BlockSpec((1, d), lambda i: (0, 0))],
        out_shape=[jax.ShapeDtypeStruct((n, d), F32), jax.ShapeDtypeStruct((n, c2), BF16),
                   jax.ShapeDtypeStruct((F_CONV, c2), F32), jax.ShapeDtypeStruct((1, d), F32)],
        scratch_shapes=[pltpu.VMEM((t + HALO, tc), F32)], name=name)


def _gdn_head(uq, uk, uv, pba, alog, dtb, head, n_heads):
    lane = lax.broadcasted_iota(jnp.int32, pba.shape, 1)
    sq = uq * _sig(uq)
    q = sq * lax.rsqrt(jnp.sum(sq * sq, axis=-1, keepdims=True) + EPS) * (A_HEAD ** -0.5)
    sk = uk * _sig(uk)
    k = sk * lax.rsqrt(jnp.sum(sk * sk, axis=-1, keepdims=True) + EPS)
    v = uv * _sig(uv)
    beta = jnp.sum(jnp.where(lane == head, _sig(pba), 0.0), axis=-1, keepdims=True)
    g_all = -jnp.exp(alog) * _softplus(pba + dtb)
    g = jnp.sum(jnp.where(lane == n_heads + head, g_all, 0.0), axis=-1, keepdims=True)
    return q, k, v, jnp.broadcast_to(beta, uq.shape), jnp.broadcast_to(g, uq.shape)


def gdn_prep(proj, conv_w, alog, dtb, n_heads, name):
    n = proj.shape[0]
    qk = n_heads * A_HEAD
    cw = 3 * qk
    ba_blk = (cw + qk) // LANES
    t = _pick(n, 256, 8)

    def body(x_ref, halo_ref, pba_ref, cw_ref, al_ref, dt_ref, q_ref, k_ref, v_ref, b_ref, g_ref, xs):
        i = pl.program_id(0)
        xs[0:HALO, :] = jnp.where(i > 0, halo_ref[...], 0.0)
        xs[HALO:, :] = x_ref[...]
        u = _causal_conv(xs, cw_ref, A_CONV, t)
        pba = pba_ref[...]
        for hd in range(n_heads):
            s0 = slice(hd * A_HEAD, (hd + 1) * A_HEAD)
            s1 = slice(qk + hd * A_HEAD, qk + (hd + 1) * A_HEAD)
            s2 = slice(2 * qk + hd * A_HEAD, 2 * qk + (hd + 1) * A_HEAD)
            q, k, v, bb, gb = _gdn_head(u[:, s0], u[:, s1], u[:, s2], pba, al_ref[...], dt_ref[...], hd, n_heads)
            q_ref[:, s0] = q
            k_ref[:, s0] = k
            v_ref[:, s0] = v
            b_ref[:, s0] = bb
            g_ref[:, s0] = gb

    out = jax.ShapeDtypeStruct((n, qk), F32)
    return pl.pallas_call(
        body, grid=(n // t,),
        in_specs=[pl.BlockSpec((t, cw), lambda i: (i, 0)),
                  pl.BlockSpec((HALO, cw), _prev_halo_map(t)),
                  pl.BlockSpec((t, LANES), lambda i: (i, ba_blk)),
                  pl.BlockSpec((A_CONV, cw), lambda i: (0, 0)),
                  pl.BlockSpec((1, LANES), lambda i: (0, 0)),
                  pl.BlockSpec((1, LANES), lambda i: (0, 0))],
        out_specs=[pl.BlockSpec((t, qk), lambda i: (i, 0))] * 5,
        out_shape=[out] * 5,
        scratch_shapes=[pltpu.VMEM((t + HALO, cw), F32)], name=name,
        compiler_params=_cp(("parallel",)))(proj, proj, proj, conv_w, alog, dtb)


def gdn_prep_bwd(proj, conv_w, alog, dtb, dq, dk, dv, dbb, dgb, n_heads, name):
    n = proj.shape[0]
    qk = n_heads * A_HEAD
    cw = 3 * qk
    ba_blk = (cw + qk) // LANES
    t = _pick(n, 256, 8)

    def body(x_ref, halo_ref, pba_ref, cw_ref, al_ref, dt_ref, dq_ref, dk_ref, dv_ref, dbb_ref, dgb_ref,
             du_ref, dba_ref, dal_ref, ddt_ref, xs):
        i = pl.program_id(0)
        xs[0:HALO, :] = jnp.where(i > 0, halo_ref[...], 0.0)
        xs[HALO:, :] = x_ref[...]
        u = _causal_conv(xs, cw_ref, A_CONV, t)
        pba = pba_ref[...]
        lane0 = lax.broadcasted_iota(jnp.int32, (t, A_HEAD), 1) == 0
        dba = jnp.zeros((t, LANES), F32)
        dal = jnp.zeros((1, LANES), F32)
        ddt = jnp.zeros((1, LANES), F32)
        for hd in range(n_heads):
            s0 = slice(hd * A_HEAD, (hd + 1) * A_HEAD)
            s1 = slice(qk + hd * A_HEAD, qk + (hd + 1) * A_HEAD)
            s2 = slice(2 * qk + hd * A_HEAD, 2 * qk + (hd + 1) * A_HEAD)
            fn = functools.partial(_gdn_head, head=hd, n_heads=n_heads)
            _, vjp = jax.vjp(fn, u[:, s0], u[:, s1], u[:, s2], pba, al_ref[...], dt_ref[...])
            cts = (dq_ref[:, s0], dk_ref[:, s0], dv_ref[:, s0],
                   jnp.where(lane0, dbb_ref[:, s0], 0.0), jnp.where(lane0, dgb_ref[:, s0], 0.0))
            duq, duk, duv, dpba, da, dd = vjp(cts)
            du_ref[:, s0] = duq
            du_ref[:, s1] = duk
            du_ref[:, s2] = duv
            dba = dba + dpba
            dal = dal + da
            ddt = ddt + dd
        dba_ref[...] = dba

        @pl.when(i == 0)
        def _():
            dal_ref[...] = jnp.zeros_like(dal_ref)
            ddt_ref[...] = jnp.zeros_like(ddt_ref)
        dal_ref[...] += dal
        ddt_ref[...] += ddt

    tok = pl.BlockSpec((t, qk), lambda i: (i, 0))
    row = pl.BlockSpec((1, LANES), lambda i: (0, 0))
    return pl.pallas_call(
        body, grid=(n // t,),
        in_specs=[pl.BlockSpec((t, cw), lambda i: (i, 0)),
                  pl.BlockSpec((HALO, cw), _prev_halo_map(t)),
                  pl.BlockSpec((t, LANES), lambda i: (i, ba_blk)),
                  pl.BlockSpec((A_CONV, cw), lambda i: (0, 0)), row, row,
                  tok, tok, tok, tok, tok],
        out_specs=[pl.BlockSpec((t, cw), lambda i: (i, 0)), pl.BlockSpec((t, LANES), lambda i: (i, 0)), row, row],
        out_shape=[jax.ShapeDtypeStruct((n, cw), F32), jax.ShapeDtypeStruct((n, LANES), F32),
                   jax.ShapeDtypeStruct((1, LANES), F32), jax.ShapeDtypeStruct((1, LANES), F32)],
        scratch_shapes=[pltpu.VMEM((t + HALO, cw), F32)], name=name,
        compiler_params=_cp(("arbitrary",)))(proj, proj, proj, conv_w, alog, dtb, dq, dk, dv, dbb, dgb)


def conv_bwd(du, x, conv_w, width, name):
    n, cw = du.shape
    t = _pick(n, 256, 8)
    last = n // t - 1

    chunks = _col_chunks(cw)
    tc = chunks[0].stop

    def body(du_ref, dnext_ref, x_ref, cw_ref, dx_ref, dcw_ref, ds):
        i = pl.program_id(0)

        @pl.when(i == 0)
        def _():
            dcw_ref[...] = jnp.zeros_like(dcw_ref)
        for cs in chunks:
            dx_ref[:, cs] = conv_bwd_tail(du_ref, dnext_ref, x_ref, cw_ref, dcw_ref, ds, width, t, i, last, cs)

    return pl.pallas_call(
        body, grid=(n // t,),
        in_specs=[pl.BlockSpec((t, cw), lambda i: (i, 0)),
                  pl.BlockSpec((HALO, cw), _next_halo_map(t, n)),
                  pl.BlockSpec((t, cw), lambda i: (i, 0)),
                  pl.BlockSpec((width, cw), lambda i: (0, 0))],
        out_specs=[pl.BlockSpec((t, cw), lambda i: (i, 0)), pl.BlockSpec((width, cw), lambda i: (0, 0))],
        out_shape=[jax.ShapeDtypeStruct((n, cw), F32), jax.ShapeDtypeStruct((width, cw), F32)],
        scratch_shapes=[pltpu.VMEM((t + HALO, tc), F32)], name=name,
        compiler_params=_cp(("arbitrary",)))(du, du, x, conv_w)


def _b(x):
    return x.astype(BF16)


def _mm_nn(a, b):
    return _dot(_b(a), _b(b))


def _mm_nt(a, b):
    return _dot_nt(_b(a), _b(b))


def _mm_tn(a, b):
    return _dot_tn(_b(a), _b(b))


@jax.custom_vjp
def _mmg_nn(a, b):
    return _mm_nn(a, b)


_mmg_nn.defvjp(lambda a, b: (_mm_nn(a, b), (a, b)),
               lambda res, dc: (_mm_nt(dc, res[1]), _mm_tn(res[0], dc)))


@jax.custom_vjp
def _mmg_nt(a, b):
    return _mm_nt(a, b)


_mmg_nt.defvjp(lambda a, b: (_mm_nt(a, b), (a, b)),
               lambda res, dc: (_mm_nn(dc, res[1]), _mm_tn(dc, res[0])))


@jax.custom_vjp
def _mmg_tn(a, b):
    return _mm_tn(a, b)


_mmg_tn.defvjp(lambda a, b: (_mm_tn(a, b), (a, b)),
               lambda res, dc: (_mm_nt(res[1], dc), _mm_nn(res[0], dc)))


def _each(f, *lists):
    return [f(*a) for a in zip(*lists)]


def _unit_lower_inv(ms):
    c = ms[0].shape[0]
    eye = jnp.where(lax.broadcasted_iota(jnp.int32, (c, c), 0) == lax.broadcasted_iota(jnp.int32, (c, c), 1), 1.0, 0.0)
    xs = [eye - m for m in ms]
    pws = _each(_mm_nn, ms, ms)
    for it in range(5):
        xs = _each(lambda x, pw: x + _mm_nn(x, pw), xs, pws)
        if it < 4:
            pws = _each(_mm_nn, pws, pws)
    rs = _each(lambda m, x: eye - x - _hdot(m, x), ms, xs)
    return _each(lambda x, r: x + _mm_nn(x, r), xs, rs)


@jax.custom_vjp
def _unit_lower_inv_g(ms):
    return _unit_lower_inv(ms)


_unit_lower_inv_g.defvjp(lambda ms: (lambda xs: (xs, xs))(_unit_lower_inv(ms)),
                         lambda xs, dxs: (_each(lambda t, x: -_mm_nt(t, x), _each(_mm_tn, xs, dxs), xs),))

_GDN_OPS = (_mm_nn, _mm_nt, _mm_tn, _unit_lower_inv)
_GDN_OPS_GRAD = (_mmg_nn, _mmg_nt, _mmg_tn, _unit_lower_inv_g)


def _gdn_chunk(ops, state, q, k, v, bb, gb):
    nn, nt, tn, inv = ops
    c = CHUNK
    ri = lax.broadcasted_iota(jnp.int32, (c, c), 0)
    ci = lax.broadcasted_iota(jnp.int32, (c, c), 1)
    causal = ri >= ci
    strict = ri > ci
    tri = jnp.where(causal, 1.0, 0.0)
    gc = [_hdot(tri, g) for g in gb]
    decay = [jnp.where(causal, jnp.exp(jnp.where(causal, x[:, :c] - x.T[:c, :], 0.0)), 0.0) for x in gc]
    kb = _each(lambda a, b: a * b, k, bb)
    kk = _each(nt, kb, k)
    m = _each(lambda a, d: jnp.where(strict, a * d, 0.0), kk, decay)
    tinv = inv(m)
    egc = [jnp.exp(x) for x in gc]
    u = _each(nn, tinv, _each(lambda a, b: a * b, v, bb))
    w = _each(nn, tinv, _each(lambda a, b: a * b, kb, egc))
    attn = _each(lambda a, d: a * d, _each(nt, q, k), decay)
    glast = [jnp.sum(g, axis=0, keepdims=True) for g in gb]
    ws = _each(nn, w, state)
    v_new = _each(lambda a, b: a - b, u, ws)
    qs = _each(nn, _each(lambda a, b: a * b, q, egc), state)
    av = _each(nn, attn, v_new)
    o = _each(lambda a, b: a + b, qs, av)
    kv = _each(tn, _each(lambda a, gl, x: a * jnp.exp(gl - x), k, glast, gc), v_new)
    new_state = _each(lambda s, gl, a: s * jnp.exp(gl) + a, state, glast, kv)
    return o, new_state


def gdn_fwd(q, k, v, bb, gb, n_heads, name, rider=None):
    n, w = q.shape
    nc = n // CHUNK
    cb = min(8, nc)
    rows = cb * CHUNK

    def body(q_ref, k_ref, v_ref, b_ref, g_ref, o_ref, st_ref, s_scr):
        @pl.when(pl.program_id(0) == 0)
        def _():
            s_scr[...] = jnp.zeros_like(s_scr)

        def step(c, carry):
            sl = pl.ds(pl.multiple_of(c * CHUNK, CHUNK), CHUNK)
            lanes = [slice(hd * A_HEAD, (hd + 1) * A_HEAD) for hd in range(n_heads)]
            state = [s_scr[hd] for hd in range(n_heads)]
            o, new_state = _gdn_chunk(_GDN_OPS, state, *[[r[sl, ls] for ls in lanes]
                                                        for r in (q_ref, k_ref, v_ref, b_ref, g_ref)])
            for hd, ls in enumerate(lanes):
                st_ref[hd, pl.ds(c, 1)] = state[hd][None]
                o_ref[sl, ls] = o[hd]
                s_scr[hd] = new_state[hd]
            return carry

        lax.fori_loop(0, cb, step, 0)

    tok = pl.BlockSpec((rows, w), lambda j: (j, 0))
    return host_call(
        body, grid=(nc // cb,),
        in_specs=[tok] * 5,
        out_specs=[tok, pl.BlockSpec((n_heads, cb, A_HEAD, A_HEAD), lambda j: (0, j, 0, 0))],
        out_shape=[jax.ShapeDtypeStruct(q.shape, F32), jax.ShapeDtypeStruct((n_heads, nc, A_HEAD, A_HEAD), F32)],
        scratch_shapes=[pltpu.VMEM((n_heads, A_HEAD, A_HEAD), F32)], name=name,
        sem=("arbitrary",), args=(q, k, v, bb, gb), rider=rider)


def gdn_bwd(q, k, v, bb, gb, states, do, n_heads, name, rider=None):
    n, w = q.shape
    nc = n // CHUNK
    cb = min(4, nc)
    rows = cb * CHUNK
    nblk = nc // cb
    chunk_fn = functools.partial(_gdn_chunk, _GDN_OPS_GRAD)

    def body(q_ref, k_ref, v_ref, b_ref, g_ref, st_ref, do_ref, dq_ref, dk_ref, dv_ref, db_ref, dg_ref, ds_scr):
        @pl.when(pl.program_id(0) == 0)
        def _():
            ds_scr[...] = jnp.zeros_like(ds_scr)

        def step(s, carry):
            c = cb - 1 - s
            sl = pl.ds(pl.multiple_of(c * CHUNK, CHUNK), CHUNK)
            for h0 in range(0, n_heads, GDN_BWD_HEADS):
                heads = list(range(h0, min(h0 + GDN_BWD_HEADS, n_heads)))
                lanes = [slice(hd * A_HEAD, (hd + 1) * A_HEAD) for hd in heads]
                state = [st_ref[hd, pl.ds(c, 1)][0] for hd in heads]
                _, vjp = jax.vjp(chunk_fn, state, *[[r[sl, ls] for ls in lanes]
                                                    for r in (q_ref, k_ref, v_ref, b_ref, g_ref)])
                dstate, dq, dk, dv, dbb, dgb = vjp(([do_ref[sl, ls] for ls in lanes], [ds_scr[hd] for hd in heads]))
                for u, (hd, ls) in enumerate(zip(heads, lanes)):
                    ds_scr[hd] = dstate[u]
                    dq_ref[sl, ls] = dq[u]
                    dk_ref[sl, ls] = dk[u]
                    dv_ref[sl, ls] = dv[u]
                    db_ref[sl, ls] = jnp.broadcast_to(jnp.sum(dbb[u], axis=-1, keepdims=True), dbb[u].shape)
                    dg_ref[sl, ls] = jnp.broadcast_to(jnp.sum(dgb[u], axis=-1, keepdims=True), dgb[u].shape)
            return carry

        lax.fori_loop(0, cb, step, 0)

    tok = pl.BlockSpec((rows, w), lambda j: (nblk - 1 - j, 0))
    out = jax.ShapeDtypeStruct(q.shape, F32)
    return host_call(
        body, grid=(nblk,),
        in_specs=[tok] * 5 + [pl.BlockSpec((n_heads, cb, A_HEAD, A_HEAD), lambda j: (0, nblk - 1 - j, 0, 0)), tok],
        out_specs=[tok] * 5, out_shape=[out] * 5,
        scratch_shapes=[pltpu.VMEM((n_heads, A_HEAD, A_HEAD), F32)], name=name,
        sem=("arbitrary",), args=(q, k, v, bb, gb, states, do), rider=rider)


def _gdn_gate(oh, zh, w):
    r = lax.rsqrt(jnp.mean(oh * oh, axis=-1, keepdims=True) + EPS)
    return oh * r * w * (zh * _sig(zh))


def gdn_out(o, proj, out_norm, w_out, h, n_heads, name):
    n, vw = o.shape
    d = h.shape[1]
    z_blk = 3 * vw // vw
    t = _pick(n, 512, 8)

    def body(o_ref, z_ref, w_ref, wo_ref, h_ref, out_ref, y_ref):
        for hd in range(n_heads):
            s0 = slice(hd * A_HEAD, (hd + 1) * A_HEAD)
            y_ref[:, s0] = _gdn_gate(o_ref[:, s0], z_ref[:, s0], w_ref[...]).astype(BF16)
        out_ref[...] = h_ref[...] + _dot(y_ref[...], wo_ref[...])

    return pl.pallas_call(
        body, grid=(n // t,),
        in_specs=[pl.BlockSpec((t, vw), lambda i: (i, 0)),
                  pl.BlockSpec((t, vw), lambda i: (i, z_blk)),
                  pl.BlockSpec((1, A_HEAD), lambda i: (0, 0)),
                  _resident((vw, d), lambda i: (0, 0)),
                  pl.BlockSpec((t, d), lambda i: (i, 0))],
        out_specs=[pl.BlockSpec((t, d), lambda i: (i, 0)), pl.BlockSpec((t, vw), lambda i: (i, 0))],
        out_shape=[jax.ShapeDtypeStruct((n, d), F32), jax.ShapeDtypeStruct((n, vw), BF16)], name=name,
        compiler_params=_cp(("parallel",)))(o, proj, out_norm, w_out, h)


def gdn_out_bwd(dout, o, proj, out_norm, w_out, n_heads, name):
    n, vw = o.shape
    d = dout.shape[1]
    z_blk = 3
    t = _pick(n, 512, 8)

    def body(dout_ref, o_ref, z_ref, w_ref, wo_ref, do_ref, dz_ref, dw_ref):
        dy = _dot_nt(dout_ref[...].astype(BF16), wo_ref[...])
        dw = jnp.zeros((1, A_HEAD), F32)
        for hd in range(n_heads):
            s0 = slice(hd * A_HEAD, (hd + 1) * A_HEAD)
            _, vjp = jax.vjp(_gdn_gate, o_ref[:, s0], z_ref[:, s0], w_ref[...])
            doh, dzh, dwh = vjp(dy[:, s0])
            do_ref[:, s0] = doh
            dz_ref[:, s0] = dzh
            dw = dw + dwh

        @pl.when(pl.program_id(0) == 0)
        def _():
            dw_ref[...] = jnp.zeros_like(dw_ref)
        dw_ref[...] += dw

    tok = pl.BlockSpec((t, vw), lambda i: (i, 0))
    return pl.pallas_call(
        body, grid=(n // t,),
        in_specs=[pl.BlockSpec((t, d), lambda i: (i, 0)), tok,
                  pl.BlockSpec((t, vw), lambda i: (i, z_blk)),
                  pl.BlockSpec((1, A_HEAD), lambda i: (0, 0)),
                  _resident((vw, d), lambda i: (0, 0))],
        out_specs=[tok, tok, pl.BlockSpec((1, A_HEAD), lambda i: (0, 0))],
        out_shape=[jax.ShapeDtypeStruct((n, vw), F32), jax.ShapeDtypeStruct((n, vw), F32),
                   jax.ShapeDtypeStruct((1, A_HEAD), F32)], name=name,
        compiler_params=_cp(("arbitrary",)))(dout, o, proj, out_norm, w_out)


def _bias_onehot(i, clip, tbl_pad):
    m = lax.broadcasted_iota(jnp.int32, (BAND_PAD, tbl_pad), 0)
    r = lax.broadcasted_iota(jnp.int32, (BAND_PAD, tbl_pad), 1)
    idx = jnp.clip(i + LEFT_CHUNKS * CHUNK - m, -clip, clip) + clip
    return jnp.where((r == idx) & (m < BAND), 1.0, 0.0)


def bias_expand(tbl, clip, name):
    nh, tp = tbl.shape

    def body(t_ref, o_ref):
        oh = _bias_onehot(pl.program_id(0), clip, tp)
        o_ref[0] = _hdot_nt(t_ref[...], oh)

    return pl.pallas_call(
        body, grid=(CHUNK,),
        in_specs=[pl.BlockSpec((nh, tp), lambda i: (0, 0))],
        out_specs=pl.BlockSpec((1, nh, BAND_PAD), lambda i: (i, 0, 0)),
        out_shape=jax.ShapeDtypeStruct((CHUNK, nh, BAND_PAD), F32), name=name,
        compiler_params=_cp(("parallel",)))(tbl)


def bias_expand_bwd(dbias, clip, tp, name):
    _, nh, _ = dbias.shape

    def body(d_ref, o_ref):
        @pl.when(pl.program_id(0) == 0)
        def _():
            o_ref[...] = jnp.zeros_like(o_ref)
        oh = _bias_onehot(pl.program_id(0), clip, tp)
        o_ref[...] += _hdot(d_ref[0], oh)

    return pl.pallas_call(
        body, grid=(CHUNK,),
        in_specs=[pl.BlockSpec((1, nh, BAND_PAD), lambda i: (i, 0, 0))],
        out_specs=pl.BlockSpec((nh, tp), lambda i: (0, 0)),
        out_shape=jax.ShapeDtypeStruct((nh, tp), F32), name=name,
        compiler_params=_cp(("arbitrary",)))(dbias)


ATT_TILE = LEFT_CHUNKS * CHUNK


ATT_GROUP = 8


def _att_softmax(s, bias, n_chunk):
    s = s * (B_HEAD ** -0.5) + bias
    slot = lax.broadcasted_iota(jnp.int32, s.shape, 1)
    valid = (slot >= (LEFT_CHUNKS - n_chunk) * CHUNK) & (slot < BAND)
    s = jnp.where(valid, s, NEG_INF)
    p = jnp.exp(s - jnp.max(s, axis=-1, keepdims=True))
    return p / jnp.sum(p, axis=-1, keepdims=True)


def _att_specs(n_pairs):
    prev = lambda p, i: (jnp.maximum(i - 1, 0), p)
    cur = lambda p, i: (i, p)
    prev_v = lambda p, i: (jnp.maximum(i - 1, 0), n_pairs + p)
    cur_v = lambda p, i: (i, n_pairs + p)
    blk = (ATT_TILE, LANES)
    return [pl.BlockSpec(blk, prev), pl.BlockSpec(blk, cur), pl.BlockSpec(blk, prev_v), pl.BlockSpec(blk, cur_v)]


def _att_fill(kbuf, vbuf, kp_ref, kc_ref, vp_ref, vc_ref):
    t = ATT_TILE
    kbuf[0:t, :] = kp_ref[...].astype(BF16)
    kbuf[t:2 * t, :] = kc_ref[...].astype(BF16)
    kbuf[2 * t:, :] = jnp.zeros((CHUNK, LANES), BF16)
    vbuf[0:t, :] = vp_ref[...].astype(BF16)
    vbuf[t:2 * t, :] = vc_ref[...].astype(BF16)
    vbuf[2 * t:, :] = jnp.zeros((CHUNK, LANES), BF16)


def _stack_heads(x, first):
    return jnp.concatenate([jnp.where(first, x, 0.0), jnp.where(first, 0.0, x)], axis=0).astype(BF16)


def attn_fwd(qp, kv, bias, name, rider=None):
    n, bw = qp.shape
    n_pairs = bw // LANES
    t = ATT_TILE
    cpt = t // CHUNK

    def body(q_ref, kp_ref, kc_ref, vp_ref, vc_ref, b_ref, o_ref, kbuf, vbuf):
        i = pl.program_id(1)
        _att_fill(kbuf, vbuf, kp_ref, kc_ref, vp_ref, vc_ref)
        lane = lax.broadcasted_iota(jnp.int32, (CHUNK, LANES), 1)
        first = lane < B_HEAD
        for g0 in range(0, cpt, ATT_GROUP):
            chunks = list(range(g0, min(g0 + ATT_GROUP, cpt)))
            band = [slice(c * CHUNK, c * CHUNK + BAND_PAD) for c in chunks]
            q2 = [_stack_heads(q_ref[c * CHUNK:(c + 1) * CHUNK, :], first) for c in chunks]
            s = [_dot_nt(q_u, kbuf[b_u, :]) for q_u, b_u in zip(q2, band)]
            p = [_att_softmax(s_u, b_ref[0], i * cpt + c) for s_u, c in zip(s, chunks)]
            o = [_dot(p_u.astype(BF16), vbuf[b_u, :]) for p_u, b_u in zip(p, band)]
            for o_u, c in zip(o, chunks):
                o_ref[c * CHUNK:(c + 1) * CHUNK, :] = jnp.where(first, o_u[:CHUNK], o_u[CHUNK:])

    return host_call(
        body, grid=(n_pairs, n // t),
        in_specs=[pl.BlockSpec((t, LANES), lambda p, i: (i, p))] + _att_specs(n_pairs)
        + [pl.BlockSpec((1, 2 * CHUNK, BAND_PAD), lambda p, i: (p, 0, 0))],
        out_specs=[pl.BlockSpec((t, LANES), lambda p, i: (i, p))],
        out_shape=[jax.ShapeDtypeStruct((n, bw), F32)],
        scratch_shapes=[pltpu.VMEM((2 * t + CHUNK, LANES), BF16), pltpu.VMEM((2 * t + CHUNK, LANES), BF16)],
        name=name, sem=("parallel", "parallel"), args=(qp, kv, kv, kv, kv, bias), rider=rider)


def attn_bwd(qp, kv, bias, d_o, dk_in, dv_in, name, rider=None):
    n, bw = qp.shape
    n_pairs = bw // LANES
    t = ATT_TILE
    cpt = t // CHUNK
    nt = n // t
    have_in = dk_in is not None
    scale = B_HEAD ** -0.5

    def body(*refs):
        q_ref, kp_ref, kc_ref, vp_ref, vc_ref, b_ref, do_ref = refs[:7]
        pos = 7
        if have_in:
            dki_ref, dvi_ref = refs[7:9]
            pos = 9
        dq_ref, dk_ref, dv_ref, db_ref, kbuf, vbuf, dkacc, dvacc = refs[pos:]
        j = pl.program_id(1)
        i = nt - 1 - j
        _att_fill(kbuf, vbuf, kp_ref, kc_ref, vp_ref, vc_ref)

        @pl.when(j == 0)
        def _():
            dkacc[...] = jnp.zeros_like(dkacc)
            dvacc[...] = jnp.zeros_like(dvacc)
            db_ref[...] = jnp.zeros_like(db_ref)

        @pl.when(j > 0)
        def _():
            dkacc[t:2 * t, :] = dkacc[0:t, :]
            dvacc[t:2 * t, :] = dvacc[0:t, :]
            dkacc[0:t, :] = jnp.zeros((t, LANES), F32)
            dvacc[0:t, :] = jnp.zeros((t, LANES), F32)

        lane = lax.broadcasted_iota(jnp.int32, (CHUNK, LANES), 1)
        first = lane < B_HEAD
        for g0 in range(0, cpt, ATT_GROUP):
            chunks = list(range(g0, min(g0 + ATT_GROUP, cpt)))
            rows = [slice(c * CHUNK, (c + 1) * CHUNK) for c in chunks]
            band = [slice(c * CHUNK, c * CHUNK + BAND_PAD) for c in chunks]
            q2 = [_stack_heads(q_ref[r, :], first) for r in rows]
            do2 = [_stack_heads(do_ref[r, :], first) for r in rows]
            s = [_dot_nt(q_u, kbuf[b_u, :]) for q_u, b_u in zip(q2, band)]
            dp = [_dot_nt(d_u, vbuf[b_u, :]) for d_u, b_u in zip(do2, band)]
            p = [_att_softmax(s_u, b_ref[0], i * cpt + c) for s_u, c in zip(s, chunks)]
            ds = [p_u * (dp_u - jnp.sum(dp_u * p_u, axis=-1, keepdims=True)) for p_u, dp_u in zip(p, dp)]
            dsb = [(d_u * scale).astype(BF16) for d_u in ds]
            dv = [_dot_tn(p_u.astype(BF16), d_u) for p_u, d_u in zip(p, do2)]
            dq = [_dot(d_u, kbuf[b_u, :]) for d_u, b_u in zip(dsb, band)]
            dk = [_dot_tn(d_u, q_u) for d_u, q_u in zip(dsb, q2)]
            for u, b_u in enumerate(band):
                db_ref[0] += ds[u]
                dvacc[b_u, :] += dv[u]
                dkacc[b_u, :] += dk[u]
                dq_ref[rows[u], :] = jnp.where(first, dq[u][:CHUNK], dq[u][CHUNK:])

        if have_in:
            dk_ref[...] = dkacc[t:2 * t, :] + dki_ref[...]
            dv_ref[...] = dvacc[t:2 * t, :] + dvi_ref[...]
        else:
            dk_ref[...] = dkacc[t:2 * t, :]
            dv_ref[...] = dvacc[t:2 * t, :]

    rev = lambda p, j: (nt - 1 - j, p)
    tok = pl.BlockSpec((t, LANES), rev)
    kv_specs = [pl.BlockSpec((t, LANES), lambda p, j: (jnp.maximum(nt - 2 - j, 0), p)),
                pl.BlockSpec((t, LANES), rev),
                pl.BlockSpec((t, LANES), lambda p, j: (jnp.maximum(nt - 2 - j, 0), n_pairs + p)),
                pl.BlockSpec((t, LANES), lambda p, j: (nt - 1 - j, n_pairs + p))]
    in_specs = [tok] + kv_specs + [pl.BlockSpec((1, 2 * CHUNK, BAND_PAD), lambda p, j: (p, 0, 0)), tok]
    args = [qp, kv, kv, kv, kv, bias, d_o]
    if have_in:
        in_specs += [tok, tok]
        args += [dk_in, dv_in]
    out = jax.ShapeDtypeStruct((n, bw), F32)
    return host_call(
        body, grid=(n_pairs, nt), in_specs=in_specs,
        out_specs=[tok, tok, tok, pl.BlockSpec((1, 2 * CHUNK, BAND_PAD), lambda p, j: (p, 0, 0))],
        out_shape=[out, out, out, jax.ShapeDtypeStruct((n_pairs, 2 * CHUNK, BAND_PAD), F32)],
        scratch_shapes=[pltpu.VMEM((2 * t + CHUNK, LANES), BF16), pltpu.VMEM((2 * t + CHUNK, LANES), BF16),
                        pltpu.VMEM((2 * t + CHUNK, LANES), F32), pltpu.VMEM((2 * t + CHUNK, LANES), F32)],
        name=name, sem=("parallel", "arbitrary"), args=args, rider=rider)


def adamw(w, gstack, m, v, name):
    r, c = w.shape
    s = gstack.shape[0]
    tr = _pick(r, 512, 8)

    def body(w_ref, g_ref, m_ref, v_ref, go_ref, d_ref, mo_ref, vo_ref):
        g = g_ref[0].astype(F32)
        for k in range(1, s):
            g = g + g_ref[k].astype(F32)
        mn = ADAM_B1 * m_ref[...] + (1.0 - ADAM_B1) * g
        vn = ADAM_B2 * v_ref[...] + (1.0 - ADAM_B2) * (g * g)
        m_hat = mn / (1.0 - ADAM_B1 ** ADAM_STEP)
        v_hat = vn / (1.0 - ADAM_B2 ** ADAM_STEP)
        go_ref[...] = g
        d_ref[...] = -ADAM_LR * (m_hat / (jnp.sqrt(v_hat) + ADAM_EPS) + ADAM_WD * w_ref[...])
        mo_ref[...] = mn
        vo_ref[...] = vn

    blk = pl.BlockSpec((tr, c), lambda i: (i, 0))
    out = jax.ShapeDtypeStruct((r, c), F32)
    return pl.pallas_call(
        body, grid=(r // tr,),
        in_specs=[blk, pl.BlockSpec((s, tr, c), lambda i: (0, i, 0)), blk, blk],
        out_specs=[blk] * 4, out_shape=[out] * 4, name=name,
        compiler_params=_cp(("parallel",)))(w, gstack, m, v)


def _place():
    x, y, c = lax.axis_index("x"), lax.axis_index("y"), lax.axis_index("c")
    chips = [(1 - x, y), (x, 1 - y), (1 - x, 1 - y)]
    return x, y, c, chips


def _ag_copy(outs, send_sems, recv_sems, t, k, block, to, src=None):
    def slot(dev):
        return outs[t].at[4 * dev[0] + 2 * dev[1] + dev[2]]
    return pltpu.make_async_remote_copy(
        src_ref=slot(block) if src is None else src, dst_ref=slot(block),
        send_sem=send_sems.at[7 * t + k], recv_sem=recv_sems.at[7 * t + k], device_id=to, device_id_type=MESH)


def _ag_start(ins, outs, send_sems, recv_sems, local_sems):
    x, y, c, chips = _place()
    me = (x, y, c)
    for t in range(len(ins)):
        pltpu.make_async_copy(ins[t], outs[t].at[4 * x + 2 * y + c], local_sems.at[t]).start()
        _ag_copy(outs, send_sems, recv_sems, t, 0, me, (x, y, 1 - c), src=ins[t]).start()
        for j, chip in enumerate(chips):
            _ag_copy(outs, send_sems, recv_sems, t, 1 + j, me, (*chip, c), src=ins[t]).start()


def _ag_finish(ins, outs, send_sems, recv_sems, local_sems):
    x, y, c, chips = _place()
    me, sibling = (x, y, c), (x, y, 1 - c)
    nt = len(ins)
    for t in range(nt):
        for j, chip in enumerate(chips):
            _ag_copy(outs, send_sems, recv_sems, t, 1 + j, (*chip, c), me).wait_recv()
            _ag_copy(outs, send_sems, recv_sems, t, 4 + j, (*chip, c), sibling).start()
    for t in range(nt):
        _ag_copy(outs, send_sems, recv_sems, t, 0, sibling, me).wait_recv()
        for j, chip in enumerate(chips):
            _ag_copy(outs, send_sems, recv_sems, t, 4 + j, (*chip, 1 - c), me).wait_recv()
    for t in range(nt):
        _ag_copy(outs, send_sems, recv_sems, t, 0, me, sibling, src=ins[t]).wait_send()
        for j, chip in enumerate(chips):
            _ag_copy(outs, send_sems, recv_sems, t, 1 + j, me, (*chip, c), src=ins[t]).wait_send()
            _ag_copy(outs, send_sems, recv_sems, t, 4 + j, (*chip, c), sibling).wait_send()
        pltpu.make_async_copy(ins[t], outs[t].at[4 * x + 2 * y + c], local_sems.at[t]).wait()


def _rs_a_copy(ins, outs, send_sems, recv_sems, t, q):
    x, y, c, _ = _place()
    return pltpu.make_async_remote_copy(
        src_ref=ins[t].at[2 * q + (1 - c)], dst_ref=outs[t].at[q],
        send_sem=send_sems.at[4 * t + q], recv_sem=recv_sems.at[4 * t + q],
        device_id=(x, y, 1 - c), device_id_type=MESH)


def _rs_a_start(ins, outs, send_sems, recv_sems, local_sems):
    for t in range(len(ins)):
        for q in range(4):
            _rs_a_copy(ins, outs, send_sems, recv_sems, t, q).start()


def _rs_a_finish(ins, outs, send_sems, recv_sems, local_sems):
    for t in range(len(ins)):
        for q in range(4):
            _rs_a_copy(ins, outs, send_sems, recv_sems, t, q).wait_recv()
    for t in range(len(ins)):
        for q in range(4):
            _rs_a_copy(ins, outs, send_sems, recv_sems, t, q).wait_send()


def _rs_b_copy(ins, outs, send_sems, recv_sems, t, j, sending):
    x, y, c, chips = _place()
    mine, other = 2 * x + y, 2 * chips[j][0] + chips[j][1]
    return pltpu.make_async_remote_copy(
        src_ref=ins[t].at[other if sending else mine], dst_ref=outs[t].at[mine if sending else other],
        send_sem=send_sems.at[3 * t + j], recv_sem=recv_sems.at[3 * t + j],
        device_id=(*chips[j], c), device_id_type=MESH)


def _rs_b_start(ins, outs, send_sems, recv_sems, local_sems):
    x, y, _, _ = _place()
    for t in range(len(ins)):
        for j in range(3):
            _rs_b_copy(ins, outs, send_sems, recv_sems, t, j, True).start()
        pltpu.make_async_copy(ins[t].at[2 * x + y], outs[t].at[2 * x + y], local_sems.at[t]).start()


def _rs_b_finish(ins, outs, send_sems, recv_sems, local_sems):
    x, y, _, _ = _place()
    for t in range(len(ins)):
        for j in range(3):
            _rs_b_copy(ins, outs, send_sems, recv_sems, t, j, False).wait_recv()
    for t in range(len(ins)):
        for j in range(3):
            _rs_b_copy(ins, outs, send_sems, recv_sems, t, j, True).wait_send()
        pltpu.make_async_copy(ins[t].at[2 * x + y], outs[t].at[2 * x + y], local_sems.at[t]).wait()


_EXCHANGES = {
    "all_gather": (7, lambda a: (N_DEV, *a.shape), _ag_start, _ag_finish),
    "rs_sibling": (4, lambda a: (4, *a.shape[1:]), _rs_a_start, _rs_a_finish),
    "rs_chips": (3, lambda a: a.shape, _rs_b_start, _rs_b_finish),
}


def _exchange_parts(kind, arrays):
    per, shape_of, start, finish = _EXCHANGES[kind]
    n = len(arrays)
    out_shape = [jax.ShapeDtypeStruct(shape_of(a), a.dtype) for a in arrays]
    sems = [pltpu.SemaphoreType.DMA((per * n,)), pltpu.SemaphoreType.DMA((per * n,)), pltpu.SemaphoreType.DMA((n,))]
    return out_shape, sems, start, finish


def exchange(kind, arrays, name):
    n = len(arrays)
    out_shape, sems, start, finish = _exchange_parts(kind, arrays)
    any_spec = pl.BlockSpec(memory_space=pl.ANY)

    def body(*refs):
        ins, outs, sem_refs = refs[:n], refs[n:2 * n], refs[2 * n:]
        start(ins, outs, *sem_refs)
        finish(ins, outs, *sem_refs)

    return pl.pallas_call(body, in_specs=[any_spec] * n, out_specs=[any_spec] * n, out_shape=out_shape,
                          scratch_shapes=sems, name=name)(*arrays)


def host_call(body, *, grid, in_specs, out_specs, out_shape, scratch_shapes, args, name, sem, rider=None):
    if rider is None:
        outs = pl.pallas_call(body, grid=grid, in_specs=in_specs, out_specs=out_specs, out_shape=out_shape,
                              scratch_shapes=scratch_shapes, name=name, compiler_params=_cp(sem))(*args)
        return outs, None
    kind, arrays = rider
    nr, ni, no, ns = len(arrays), len(in_specs), len(out_specs), len(scratch_shapes)
    r_shape, r_sems, start, finish = _exchange_parts(kind, arrays)
    any_spec = pl.BlockSpec(memory_space=pl.ANY)

    def wrapped(*refs):
        ins, r_ins = refs[:ni], refs[ni:ni + nr]
        outs, r_outs = refs[ni + nr:ni + nr + no], refs[ni + nr + no:ni + 2 * nr + no]
        scratch, sem_refs = refs[ni + 2 * nr + no:ni + 2 * nr + no + ns], refs[ni + 2 * nr + no + ns:]
        first = pl.program_id(0) == 0
        last = pl.program_id(0) == grid[0] - 1
        for ax in range(1, len(grid)):
            first = first & (pl.program_id(ax) == 0)
            last = last & (pl.program_id(ax) == grid[ax] - 1)

        @pl.when(first)
        def _():
            start(r_ins, r_outs, *sem_refs)
        body(*ins, *outs, *scratch)

        @pl.when(last)
        def _():
            finish(r_ins, r_outs, *sem_refs)

    outs = pl.pallas_call(
        wrapped, grid=grid, in_specs=list(in_specs) + [any_spec] * nr, out_specs=list(out_specs) + [any_spec] * nr,
        out_shape=list(out_shape) + r_shape, scratch_shapes=list(scratch_shapes) + r_sems, name=name,
        compiler_params=_cp(("arbitrary",) * len(grid)))(*args, *arrays)
    return outs[:no], outs[no:]


def pair_add(g8, recv, c_idx, name):
    _, r, c = g8.shape
    tr = _pick(r, 512, 8)

    def body(c_ref, g_ref, r_ref, o_ref):
        o_ref[...] = (g_ref[...] + r_ref[...]).astype(BF16)

    return pl.pallas_call(
        body,
        grid_spec=pltpu.PrefetchScalarGridSpec(
            num_scalar_prefetch=1, grid=(4, r // tr),
            in_specs=[pl.BlockSpec((1, tr, c), lambda q, i, cr: (2 * q + cr[0], i, 0)),
                      pl.BlockSpec((1, tr, c), lambda q, i, cr: (q, i, 0))],
            out_specs=pl.BlockSpec((1, tr, c), lambda q, i, cr: (q, i, 0))),
        out_shape=jax.ShapeDtypeStruct((4, r, c), BF16), name=name,
        compiler_params=_cp(("parallel", "parallel")))(c_idx, g8, recv)


def all_reduce_small(pack, name):
    r, c = pack.shape

    def body(x_ref, o_ref, buf, send_sems, recv_sems, local_sem):
        x, y, cc, chips = _place()
        me, sibling = (x, y, cc), (x, y, 1 - cc)

        def slot(dev):
            return buf.at[4 * dev[0] + 2 * dev[1] + dev[2]]

        def copy(k, block, to, src=None):
            return pltpu.make_async_remote_copy(
                src_ref=slot(block) if src is None else src, dst_ref=slot(block),
                send_sem=send_sems.at[k], recv_sem=recv_sems.at[k], device_id=to, device_id_type=MESH)

        mine = pltpu.make_async_copy(x_ref, slot(me), local_sem)
        mine.start()
        first = [copy(0, me, sibling, src=x_ref)]
        first += [copy(1 + j, me, (*chip, cc), src=x_ref) for j, chip in enumerate(chips)]
        for cp in first:
            cp.start()
        passed = [copy(4 + j, (*chip, cc), sibling) for j, chip in enumerate(chips)]
        for j, chip in enumerate(chips):
            copy(1 + j, (*chip, cc), me).wait_recv()
            passed[j].start()
        copy(0, sibling, me).wait_recv()
        for j, chip in enumerate(chips):
            copy(4 + j, (*chip, 1 - cc), me).wait_recv()
        for cp in first + passed:
            cp.wait_send()
        mine.wait()
        acc = buf[0]
        for k in range(1, N_DEV):
            acc = acc + buf[k]
        o_ref[...] = acc

    return pl.pallas_call(
        body, in_specs=[pl.BlockSpec(memory_space=pltpu.VMEM)],
        out_specs=pl.BlockSpec(memory_space=pltpu.VMEM),
        out_shape=jax.ShapeDtypeStruct((r, c), F32),
        scratch_shapes=[pltpu.VMEM((N_DEV, r, c), F32), pltpu.SemaphoreType.DMA((7,)),
                        pltpu.SemaphoreType.DMA((7,)), pltpu.SemaphoreType.DMA],
        name=name, compiler_params=_cp())(pack)


def _row(v):
    return v.reshape(1, -1)


def _lane_row(vals, offset):
    return jnp.pad(vals, (offset, LANES - offset - vals.shape[0])).reshape(1, LANES)


def _bias_to_pairs(b):
    i, nh, bp = b.shape
    return b.transpose(1, 0, 2).reshape(nh // 2, 2 * i, bp)


def _bias_from_pairs(b):
    p, i2, bp = b.shape
    return b.reshape(2 * p, i2 // 2, bp).transpose(1, 0, 2)


class LocalWeights:
    def __init__(self, W):
        self.W = W
        self.grads = {}

    def big(self, l, la):
        W = self.W
        out = {"f_w_up": W["f_w_up"][l], "f_w_down": W["f_w_down"][l]}
        if l < la:
            out.update(a_w_in=W["a_w_in"][l], a_w_out=W["a_w_out"][l])
        else:
            out.update(b_w_q=W["b_w_q"][l - la], b_w_out=W["b_w_out"][l - la])
        if l == la:
            out["w_kv"] = W["w_kv"]
        return out

    def fwd_rider(self, l):
        return None

    def fwd_got(self, l, got):
        pass

    def bwd_rider_a(self, l):
        return None

    def bwd_got_a(self, l, got):
        pass

    def bwd_rider_b(self, l):
        return None

    def bwd_got_b(self, l, got):
        pass

    def grads_ready(self, l, grads):
        for k_, g in grads.items():
            self.grads.setdefault(k_, {})[l] = g

    def stacked(self):
        return {k_: (jnp.stack([v_[l] for l in sorted(v_)]) if k_ != "w_kv" else next(iter(v_.values())))
                for k_, v_ in self.grads.items()}


def _named(name, l, rider):
    return name if rider is None else f"{name}_x{l}"


def local_step(x, target, W, comm=None):
    comm = LocalWeights(W) if comm is None else comm
    n, d = x.shape
    la, ha = W["a_A_log"].shape
    lb, hb, tbl = W["b_rel_bias"].shape
    depth = W["f_norm"].shape[0]
    clip = (tbl - 1) // 2
    tp = -(-tbl // LANES) * LANES
    qk = ha * A_HEAD
    cw = 3 * qk
    bw = hb * B_HEAD
    a_in = cw + qk + 2 * ha

    h = x
    saves = []
    kv = h_kv = w_kv = None
    for l in range(depth):
        big = comm.big(l, la)
        sv = {"h_in": h, "big": big}
        rider = comm.fwd_rider(l)
        if l < la:
            alog = _lane_row(W["a_A_log"][l], ha)
            dtb = _lane_row(W["a_dt_bias"][l], ha)
            proj = norm_matmul(h, _row(W["a_norm"][l]), big["a_w_in"], "a_in_proj")
            q, k, v, bb, gb = gdn_prep(proj, W["a_conv"][l], alog, dtb, ha, "gdn_prep")
            (o, states), got = gdn_fwd(q, k, v, bb, gb, ha, _named("gdn_fwd", l, rider), rider)
            h, y = gdn_out(o, proj, _row(W["a_out_norm"][l]), big["a_w_out"], h, ha, "gdn_out")
            sv.update(proj=proj, q=q, k=k, v=v, bb=bb, gb=gb, states=states, o=o, y=y, alog=alog, dtb=dtb)
        else:
            j = l - la
            if j == 0:
                h_kv, w_kv = h, big["w_kv"]
                kv = norm_matmul(h, _row(W["kv_norm"]), w_kv, "kv_proj")
            qp = norm_matmul(h, _row(W["b_norm"][j]), big["b_w_q"], "b_q_proj")
            tblp = jnp.pad(W["b_rel_bias"][j], ((0, 0), (0, tp - tbl)))
            bias = _bias_to_pairs(bias_expand(tblp, clip, "bias_expand"))
            (o,), got = attn_fwd(qp, kv, bias, _named("attn_fwd", l, rider), rider)
            h = matmul_res(o, big["b_w_out"], h, "b_out_proj")
            sv.update(qp=qp, bias=bias, o=o)
        comm.fwd_got(l, got)
        sv["h_mid"] = h
        up = norm_matmul(h, _row(W["f_norm"][l]), big["f_w_up"], "f_up_proj", out_dtype=BF16)
        h, act, hc = ffn_act_down(up, W["f_conv"][l], _row(W["f_conv_b"][l]), big["f_w_down"], h, "ffn_act_down")
        sv.update(up=up, act=act, hc=hc)
        saves.append(sv)

    loss, dh, d_final = loss_head(h, _row(W["final_norm"]), target)

    G = {k_: [None] * (la if k_.startswith("a_") else lb if k_.startswith("b_") else depth)
         for k_ in ("a_norm", "a_conv", "a_A_log", "a_dt_bias", "a_out_norm",
                    "b_norm", "b_rel_bias", "f_norm", "f_conv", "f_conv_b")}
    G["final_norm"] = d_final[0]
    dk_acc = dv_acc = None
    for l in reversed(range(depth)):
        sv = saves[l]
        big = sv["big"]
        gbig = {}
        dhc, dcb = ffn_bwd_act(dh, sv["hc"], big["f_w_down"], "ffn_bwd_act")
        gbig["f_w_down"] = matmul_tn(sv["act"], dh, "f_down_wgrad")
        G["f_conv_b"][l] = dcb[0]
        rider = comm.bwd_rider_a(l)
        (dh, dup, dcw, dg), got = ffn_bwd_up(dhc, sv["up"], W["f_conv"][l], big["f_w_up"], sv["h_mid"], dh,
                                             _row(W["f_norm"][l]), _named("ffn_bwd_up", l, rider), rider)
        comm.bwd_got_a(l, got)
        G["f_conv"][l] = dcw
        G["f_norm"][l] = dg[0]
        gbig["f_w_up"] = norm_matmul_tn(sv["h_mid"], _row(W["f_norm"][l]), dup, "f_up_wgrad")
        rider = comm.bwd_rider_b(l)
        if l < la:
            w_in = big["a_w_in"]
            do, dz, dwn = gdn_out_bwd(dh, sv["o"], sv["proj"], _row(W["a_out_norm"][l]), big["a_w_out"], ha, "gdn_out_bwd")
            G["a_out_norm"][l] = dwn[0]
            gbig["a_w_out"] = matmul_tn(sv["y"], dh, "a_out_wgrad")
            (dq, dk, dv, dbb, dgb), got = gdn_bwd(sv["q"], sv["k"], sv["v"], sv["bb"], sv["gb"], sv["states"], do, ha,
                                                  _named("gdn_bwd", l, rider), rider)
            comm.bwd_got_b(l, got)
            du, dba, dal, ddt = gdn_prep_bwd(sv["proj"], W["a_conv"][l], sv["alog"], sv["dtb"],
                                             dq, dk, dv, dbb, dgb, ha, "gdn_prep_bwd")
            G["a_A_log"][l] = dal[0, ha:2 * ha]
            G["a_dt_bias"][l] = ddt[0, ha:2 * ha]
            dqkv, dconv = conv_bwd(du, sv["proj"], W["a_conv"][l], A_CONV, "gdn_conv_bwd")
            G["a_conv"][l] = dconv
            gam = _row(W["a_norm"][l])
            pieces = [(dqkv, w_in[:, :cw]), (dz, w_in[:, cw:cw + qk]), (dba, w_in[:, cw + qk:])]
            gbig["a_w_in"] = jnp.concatenate(
                [norm_matmul_tn(sv["h_in"], gam, dqkv, "a_in_wgrad_qkv"),
                 norm_matmul_tn(sv["h_in"], gam, dz, "a_in_wgrad_z"),
                 norm_matmul_tn(sv["h_in"], gam, dba, "a_in_wgrad_ba")[:, :2 * ha]], axis=1)
            dh, dg = dx_norm_bwd(dh, sv["h_in"], gam, pieces, "a_in_dx")
            G["a_norm"][l] = dg[0]
        else:
            j = l - la
            d_o = matmul_nt(dh, big["b_w_out"], "b_out_dx")
            gbig["b_w_out"] = matmul_tn(sv["o"], dh, "b_out_wgrad")
            (dq, dk_acc, dv_acc, dbias), got = attn_bwd(
                sv["qp"], kv, sv["bias"], d_o, dk_acc, dv_acc,
                _named("attn_bwd" if dk_acc is None else "attn_bwd_acc", l, rider), rider)
            comm.bwd_got_b(l, got)
            G["b_rel_bias"][j] = bias_expand_bwd(_bias_from_pairs(dbias), clip, tp, "bias_expand_bwd")[:, :tbl]
            gam = _row(W["b_norm"][j])
            gbig["b_w_q"] = norm_matmul_tn(sv["h_in"], gam, dq, "b_q_wgrad")
            dh, dg = dx_norm_bwd(dh, sv["h_in"], gam, [(dq, big["b_w_q"])], "b_q_dx")
            G["b_norm"][j] = dg[0]
            if j == 0:
                gam = _row(W["kv_norm"])
                gbig["w_kv"] = jnp.concatenate([norm_matmul_tn(h_kv, gam, dk_acc, "kv_wgrad_k"),
                                                norm_matmul_tn(h_kv, gam, dv_acc, "kv_wgrad_v")], axis=1)
                dh, dg = dx_norm_bwd(dh, h_kv, gam, [(dk_acc, w_kv[:, :bw]), (dv_acc, w_kv[:, bw:])], "kv_dx")
                G["kv_norm"] = dg[0]
        comm.grads_ready(l, gbig)
    out = {k_: (jnp.stack(v_) if isinstance(v_, list) else v_) for k_, v_ in G.items()}
    return loss[0, 0], dh, out, comm


WEIGHTS = ["a_norm", "a_w_in", "a_conv", "a_A_log", "a_dt_bias", "a_out_norm", "a_w_out", "kv_norm", "w_kv",
           "b_norm", "b_w_q", "b_rel_bias", "b_w_out", "f_norm", "f_w_up", "f_conv", "f_conv_b", "f_w_down",
           "final_norm"]
SHARD_AXIS = {"a_norm": 1, "a_w_in": 2, "a_conv": 2, "a_w_out": 1, "w_kv": 1, "b_w_q": 1, "b_w_out": 1,
              "f_w_up": 2, "f_conv": 2, "f_w_down": 1}
BIG = ["a_w_in", "a_w_out", "w_kv", "b_w_q", "b_w_out", "f_w_up", "f_w_down"]
SMALL_SHARDED = ["a_norm", "a_conv", "f_conv"]


def _unstack(g, axis):
    return jnp.concatenate([g[i] for i in range(N_DEV)], axis=axis)


def _to_blocks(full, axis):
    parts = jnp.stack(jnp.split(full, N_DEV, axis=axis))
    return parts.reshape(N_DEV, -1, parts.shape[-1])


def _pack(arrs):
    flat = []
    for a in arrs:
        f = a.reshape(-1)
        flat.append(jnp.pad(f, (0, (-f.shape[0]) % LANES)))
    f = jnp.concatenate(flat)
    f = jnp.pad(f, (0, (-f.shape[0]) % (8 * LANES)))
    return f.reshape(-1, LANES)


def _unpack(pack, shapes):
    flat = pack.reshape(-1)
    out, pos = [], 0
    for s in shapes:
        sz = math.prod(s)
        out.append(flat[pos:pos + sz].reshape(s))
        pos += sz + (-sz) % LANES
    return out


def _as2d(a):
    return a.reshape(1, -1) if a.ndim == 1 else a.reshape(-1, a.shape[-1])


def kernel(x, a_norm, a_w_in, a_conv, a_A_log, a_dt_bias, a_out_norm, a_w_out, kv_norm, w_kv, b_norm, b_w_q, b_rel_bias, b_w_out, f_norm, f_w_up, f_conv, f_conv_b, f_w_down, final_norm, loss_target, m_a_norm, m_a_w_in, m_a_conv, m_a_A_log, m_a_dt_bias, m_a_out_norm, m_a_w_out, m_kv_norm, m_w_kv, m_b_norm, m_b_w_q, m_b_rel_bias, m_b_w_out, m_f_norm, m_f_w_up, m_f_conv, m_f_conv_b, m_f_w_down, m_final_norm, v_a_norm, v_a_w_in, v_a_conv, v_a_A_log, v_a_dt_bias, v_a_out_norm, v_a_w_out, v_kv_norm, v_w_kv, v_b_norm, v_b_w_q, v_b_rel_bias, v_b_w_out, v_f_norm, v_f_w_up, v_f_conv, v_f_conv_b, v_f_w_down, v_final_norm):
    w = dict(a_norm=a_norm, a_w_in=a_w_in, a_conv=a_conv, a_A_log=a_A_log, a_dt_bias=a_dt_bias,
             a_out_norm=a_out_norm, a_w_out=a_w_out, kv_norm=kv_norm, w_kv=w_kv, b_norm=b_norm, b_w_q=b_w_q,
             b_rel_bias=b_rel_bias, b_w_out=b_w_out, f_norm=f_norm, f_w_up=f_w_up, f_conv=f_conv,
             f_conv_b=f_conv_b, f_w_down=f_w_down, final_norm=final_norm)
    mom = dict(a_norm=m_a_norm, a_w_in=m_a_w_in, a_conv=m_a_conv, a_A_log=m_a_A_log, a_dt_bias=m_a_dt_bias,
               a_out_norm=m_a_out_norm, a_w_out=m_a_w_out, kv_norm=m_kv_norm, w_kv=m_w_kv, b_norm=m_b_norm,
               b_w_q=m_b_w_q, b_rel_bias=m_b_rel_bias, b_w_out=m_b_w_out, f_norm=m_f_norm, f_w_up=m_f_w_up,
               f_conv=m_f_conv, f_conv_b=m_f_conv_b, f_w_down=m_f_w_down, final_norm=m_final_norm)
    var = dict(a_norm=v_a_norm, a_w_in=v_a_w_in, a_conv=v_a_conv, a_A_log=v_a_A_log, a_dt_bias=v_a_dt_bias,
               a_out_norm=v_a_out_norm, a_w_out=v_a_w_out, kv_norm=v_kv_norm, w_kv=v_w_kv, b_norm=v_b_norm,
               b_w_q=v_b_w_q, b_rel_bias=v_b_rel_bias, b_w_out=v_b_w_out, f_norm=v_f_norm, f_w_up=v_f_w_up,
               f_conv=v_f_conv, f_conv_b=v_f_conv_b, f_w_down=v_f_w_down, final_norm=v_final_norm)
    me = 4 * lax.axis_index("x") + 2 * lax.axis_index("y") + lax.axis_index("c")

    la, depth = a_A_log.shape[0], f_norm.shape[0]
    c_idx = lax.axis_index("c").astype(jnp.int32).reshape(1)
    shard_bf16 = {k: w[k].astype(BF16) for k in BIG}

    class Sharded(LocalWeights):
        def __init__(self):
            super().__init__(w)
            self.full = {}
            self.stacks = {}
            self.pending = None
            self.parts = None

        def names(self, l):
            out = ["a_w_in", "a_w_out"] if l < la else ["b_w_q", "b_w_out"]
            return out + ["f_w_up", "f_w_down"] + (["w_kv"] if l == la else [])

        def index(self, k, l):
            return None if k == "w_kv" else (l - la if k.startswith("b_") else l)

        def shards(self, l):
            return [shard_bf16[k] if k == "w_kv" else shard_bf16[k][self.index(k, l)] for k in self.names(l)]

        def install(self, l, gathered):
            out = {}
            for k, g in zip(self.names(l), gathered):
                out[k] = _unstack(g, SHARD_AXIS[k] - (k != "w_kv"))
            if "a_w_in" in out:
                out["a_w_in"] = jnp.pad(out["a_w_in"], ((0, 0), (0, (-out["a_w_in"].shape[1]) % LANES)))
            self.full[l] = out

        def big(self, l, la_):
            return self.full[l]

        def fwd_rider(self, l):
            return ("all_gather", self.shards(l + 1)) if l + 1 < depth else None

        def fwd_got(self, l, got):
            if got is not None:
                self.install(l + 1, got)

        def grads_ready(self, l, grads):
            keys = self.names(l)
            self.pending = (l, keys, [_to_blocks(grads[k], SHARD_AXIS[k] - (k != "w_kv")) for k in keys])

        def bwd_rider_a(self, l):
            return None if self.pending is None else ("rs_sibling", self.pending[2])

        def add_pairs(self, from_sibling):
            self.parts = [pair_add(g, r, c_idx, "grads_pair_add") for g, r in zip(self.pending[2], from_sibling)]

        def bwd_got_a(self, l, got):
            if got is not None:
                self.add_pairs(got)

        def bwd_rider_b(self, l):
            return None if self.parts is None else ("rs_chips", self.parts)

        def keep(self, stacks):
            l, keys, _ = self.pending
            for k, s in zip(keys, stacks):
                self.stacks[(k, l)] = s
            self.pending = self.parts = None

        def bwd_got_b(self, l, got):
            if got is not None:
                self.keep(got)

        def finish(self):
            self.add_pairs(exchange("rs_sibling", self.pending[2], "grads_to_sibling"))
            self.keep(exchange("rs_chips", self.parts, "grads_to_chips"))

    comm = Sharded()

    small_shapes = [w[k].shape for k in SMALL_SHARDED]
    gathered = exchange("all_gather", comm.shards(0) + [_pack([w[k] for k in SMALL_SHARDED])], "weights_all_gather")
    comm.install(0, gathered[:-1])
    full = dict(w)
    small = [_unpack(gathered[-1][i], small_shapes) for i in range(N_DEV)]
    for idx, k in enumerate(SMALL_SHARDED):
        full[k] = jnp.concatenate([small[i][idx] for i in range(N_DEV)], axis=SHARD_AXIS[k])

    loss_part, grad_x, G, _ = local_step(x[0], loss_target[0], full, comm)
    comm.finish()
    stacks = []
    for k in BIG:
        layers = sorted(l for (k_, l) in comm.stacks if k_ == k)
        stacks.append(jnp.concatenate([comm.stacks[(k, l)] for l in layers], axis=1))

    small_names = [k for k in WEIGHTS if k not in BIG]
    reduced = _unpack(all_reduce_small(_pack([G[k] for k in small_names] + [loss_part.reshape(1)]), "small_all_reduce"),
                      [G[k].shape for k in small_names] + [(1,)])
    loss = reduced[-1][0]
    small_g = dict(zip(small_names, reduced[:-1]))
    for k in SMALL_SHARDED:
        sz = w[k].shape[SHARD_AXIS[k]]
        small_g[k] = lax.dynamic_slice_in_dim(small_g[k], me * sz, sz, axis=SHARD_AXIS[k])

    res = {}
    for k, st in zip(BIG, stacks):
        outs = adamw(_as2d(w[k]), st, _as2d(mom[k]), _as2d(var[k]), "adamw_" + k)
        res[k] = [o.reshape(w[k].shape) for o in outs]
    for k in small_names:
        outs = adamw(_as2d(w[k]), _as2d(small_g[k])[None], _as2d(mom[k]), _as2d(var[k]), "adamw_" + k)
        res[k] = [o.reshape(w[k].shape) for o in outs]

    return (loss, grad_x[None], *[res[k][0] for k in WEIGHTS], *[res[k][1] for k in WEIGHTS],
            *[res[k][2] for k in WEIGHTS], *[res[k][3] for k in WEIGHTS])
```

```python
import functools
import math

import jax
import jax.numpy as jnp
from jax import lax
from jax.experimental import pallas as pl
from jax.experimental.pallas import tpu as pltpu

F32 = jnp.float32
BF16 = jnp.bfloat16
HI = lax.Precision.HIGHEST
MESH = pl.DeviceIdType.MESH

EPS = 1e-6
NEG_INF = -1e30
CHUNK = 64
LEFT_CHUNKS = 8
BAND = (LEFT_CHUNKS + 1) * CHUNK
BAND_PAD = 640
A_CONV = 4
F_CONV = 3
A_HEAD = 128
B_HEAD = 64
LANES = 128
HALO = 8
N_DEV = 8

ADAM_LR = 0.001
ADAM_B1 = 0.9
ADAM_B2 = 0.999
ADAM_EPS = 1e-08
ADAM_WD = 0.01
ADAM_STEP = 10

VMEM_LIMIT_V7X = 56 * 1024 * 1024
GDN_BWD_HEADS = 8
COL_CHUNK = 256
FFN_TILE = 256


def _cp(sem=None, vmem=VMEM_LIMIT_V7X):
    kw = dict(vmem_limit_bytes=vmem)
    if sem is not None:
        kw["dimension_semantics"] = sem
    return pltpu.CompilerParams(**kw)


def _pick(n, target, q=LANES):
    best = None
    for t in range(q, min(n, target) + 1, q):
        if n % t == 0:
            best = t
    return best if best is not None else n


def _sig(x):
    return 1.0 / (1.0 + jnp.exp(-x))


def _softplus(x):
    return jnp.maximum(x, 0.0) + jnp.log(1.0 + jnp.exp(-jnp.abs(x)))


def _rms(x, g):
    return x * lax.rsqrt(jnp.mean(x * x, axis=-1, keepdims=True) + EPS) * g


def _rms_bwd(x, g, dxn):
    r = lax.rsqrt(jnp.mean(x * x, axis=-1, keepdims=True) + EPS)
    gd = dxn * g
    dx = r * gd - x * (r * r * r) * jnp.mean(x * gd, axis=-1, keepdims=True)
    dg = jnp.sum(dxn * x * r, axis=0, keepdims=True)
    return dx, dg


def _dot(a, b):
    return jnp.dot(a, b, preferred_element_type=F32)


def _dot_nt(a, b):
    return lax.dot_general(a, b, (((1,), (1,)), ((), ())), preferred_element_type=F32)


def _dot_tn(a, b):
    return lax.dot_general(a, b, (((0,), (0,)), ((), ())), preferred_element_type=F32)


def _hdot(a, b):
    return jnp.dot(a, b, precision=HI, preferred_element_type=F32)


def _hdot_nt(a, b):
    return lax.dot_general(a, b, (((1,), (1,)), ((), ())), precision=HI, preferred_element_type=F32)


def _hdot_tn(a, b):
    return lax.dot_general(a, b, (((0,), (0,)), ((), ())), precision=HI, preferred_element_type=F32)


def _resident(shape, index_map):
    return pl.BlockSpec(shape, index_map, pipeline_mode=pl.Buffered(1))


def norm_matmul(h, gamma, w, name, out_dtype=F32):
    n, d = h.shape
    nc = w.shape[1]
    tm = _pick(n, 512, 8)
    tn = _pick(nc, 1536)

    def body(h_ref, g_ref, w_ref, o_ref):
        xn = _rms(h_ref[...], g_ref[...])
        o_ref[...] = _dot(xn.astype(BF16), w_ref[...]).astype(out_dtype)

    return pl.pallas_call(
        body, grid=(nc // tn, n // tm),
        in_specs=[pl.BlockSpec((tm, d), lambda j, i: (i, 0)),
                  pl.BlockSpec((1, d), lambda j, i: (0, 0)),
                  pl.BlockSpec((d, tn), lambda j, i: (0, j))],
        out_specs=pl.BlockSpec((tm, tn), lambda j, i: (i, j)),
        out_shape=jax.ShapeDtypeStruct((n, nc), out_dtype), name=name,
        compiler_params=_cp(("parallel", "parallel")))(h, gamma, w)


def norm_matmul_tn(h, gamma, dy, name, col_blocks=None):
    n, d = h.shape
    nc = dy.shape[1]
    tm = _pick(n, 512, 8)
    tn = _pick(nc, 1536)
    grid = (nc // tn, n // tm)
    in_specs = [pl.BlockSpec((tm, d), lambda j, i: (i, 0)),
                pl.BlockSpec((1, d), lambda j, i: (0, 0)),
                pl.BlockSpec((tm, tn), lambda j, i: (i, j))]

    if col_blocks is None or tn % (nc // col_blocks):
        def body(h_ref, g_ref, dy_ref, o_ref):
            @pl.when(pl.program_id(1) == 0)
            def _():
                o_ref[...] = jnp.zeros_like(o_ref)
            xn = _rms(h_ref[...], g_ref[...])
            o_ref[...] += _dot_tn(xn.astype(BF16), dy_ref[...].astype(BF16))

        out = pl.pallas_call(
            body, grid=grid, in_specs=in_specs, out_specs=pl.BlockSpec((d, tn), lambda j, i: (0, j)),
            out_shape=jax.ShapeDtypeStruct((d, nc), F32), name=name,
            compiler_params=_cp(("parallel", "arbitrary")))(h, gamma, dy)
        return out if col_blocks is None else jnp.stack(jnp.split(out, col_blocks, axis=1))

    bw = nc // col_blocks
    per = tn // bw

    def body_blocks(h_ref, g_ref, dy_ref, o_ref, acc):
        i = pl.program_id(1)

        @pl.when(i == 0)
        def _():
            acc[...] = jnp.zeros_like(acc)
        xn = _rms(h_ref[...], g_ref[...])
        acc[...] += _dot_tn(xn.astype(BF16), dy_ref[...].astype(BF16))

        @pl.when(i == grid[1] - 1)
        def _():
            for b in range(per):
                o_ref[b] = acc[:, b * bw:(b + 1) * bw]

    return pl.pallas_call(
        body_blocks, grid=grid, in_specs=in_specs, out_specs=pl.BlockSpec((per, d, bw), lambda j, i: (j, 0, 0)),
        out_shape=jax.ShapeDtypeStruct((col_blocks, d, bw), F32), scratch_shapes=[pltpu.VMEM((d, tn), F32)],
        name=name, compiler_params=_cp(("parallel", "arbitrary")))(h, gamma, dy)


def matmul_tn(a, dy, name):
    n, ka = a.shape
    nc = dy.shape[1]
    tm = _pick(n, 512, 8)
    tk = _pick(ka, 1536)
    tn = _pick(nc, 1024)

    def body(a_ref, dy_ref, o_ref):
        @pl.when(pl.program_id(2) == 0)
        def _():
            o_ref[...] = jnp.zeros_like(o_ref)
        o_ref[...] += _dot_tn(a_ref[...].astype(BF16), dy_ref[...].astype(BF16))

    return pl.pallas_call(
        body, grid=(ka // tk, nc // tn, n // tm),
        in_specs=[pl.BlockSpec((tm, tk), lambda k, j, i: (i, k)),
                  pl.BlockSpec((tm, tn), lambda k, j, i: (i, j))],
        out_specs=pl.BlockSpec((tk, tn), lambda k, j, i: (k, j)),
        out_shape=jax.ShapeDtypeStruct((ka, nc), F32), name=name,
        compiler_params=_cp(("parallel", "parallel", "arbitrary")))(a, dy)


def matmul_res(a, w, h, name):
    n, k = a.shape
    d = w.shape[1]
    tm = _pick(n, 512, 8)

    def body(a_ref, w_ref, h_ref, o_ref):
        o_ref[...] = h_ref[...] + _dot(a_ref[...].astype(BF16), w_ref[...])

    return pl.pallas_call(
        body, grid=(n // tm,),
        in_specs=[pl.BlockSpec((tm, k), lambda i: (i, 0)),
                  _resident((k, d), lambda i: (0, 0)),
                  pl.BlockSpec((tm, d), lambda i: (i, 0))],
        out_specs=pl.BlockSpec((tm, d), lambda i: (i, 0)),
        out_shape=jax.ShapeDtypeStruct((n, d), F32), name=name,
        compiler_params=_cp(("parallel",)))(a, w, h)


def matmul_nt(dy, w, name):
    n, k = dy.shape
    d = w.shape[0]
    tm = _pick(n, 512, 8)

    def body(dy_ref, w_ref, o_ref):
        o_ref[...] = _dot_nt(dy_ref[...].astype(BF16), w_ref[...])

    return pl.pallas_call(
        body, grid=(n // tm,),
        in_specs=[pl.BlockSpec((tm, k), lambda i: (i, 0)),
                  _resident((d, k), lambda i: (0, 0))],
        out_specs=pl.BlockSpec((tm, d), lambda i: (i, 0)),
        out_shape=jax.ShapeDtypeStruct((n, d), F32), name=name,
        compiler_params=_cp(("parallel",)))(dy, w)


def dx_norm_bwd(dout, h, gamma, pieces, name):
    n, d = h.shape
    tm = _pick(n, 256, 8)
    np_ = len(pieces)

    def body(*refs):
        dout_ref, h_ref, g_ref = refs[:3]
        dys = refs[3:3 + np_]
        ws = refs[3 + np_:3 + 2 * np_]
        dh_ref, dg_ref = refs[3 + 2 * np_:]
        dxn = _dot_nt(dys[0][...].astype(BF16), ws[0][...])
        for p in range(1, np_):
            dxn = dxn + _dot_nt(dys[p][...].astype(BF16), ws[p][...])
        dx, dg = _rms_bwd(h_ref[...], g_ref[...], dxn)
        dh_ref[...] = dout_ref[...] + dx

        @pl.when(pl.program_id(0) == 0)
        def _():
            dg_ref[...] = jnp.zeros_like(dg_ref)
        dg_ref[...] += dg

    in_specs = [pl.BlockSpec((tm, d), lambda i: (i, 0)),
                pl.BlockSpec((tm, d), lambda i: (i, 0)),
                pl.BlockSpec((1, d), lambda i: (0, 0))]
    in_specs += [pl.BlockSpec((tm, dy.shape[1]), lambda i: (i, 0)) for dy, _ in pieces]
    in_specs += [_resident(w.shape, lambda i: (0, 0)) for _, w in pieces]
    return pl.pallas_call(
        body, grid=(n // tm,), in_specs=in_specs,
        out_specs=[pl.BlockSpec((tm, d), lambda i: (i, 0)), pl.BlockSpec((1, d), lambda i: (0, 0))],
        out_shape=[jax.ShapeDtypeStruct((n, d), F32), jax.ShapeDtypeStruct((1, d), F32)], name=name,
        compiler_params=_cp(("arbitrary",)))(dout, h, gamma, *[p[0] for p in pieces], *[p[1] for p in pieces])


def loss_head(h, gamma, target, name="loss_head"):
    n, d = h.shape
    tm = _pick(n, 512, 8)

    def body(h_ref, g_ref, t_ref, loss_ref, dh_ref, dg_ref):
        @pl.when(pl.program_id(0) == 0)
        def _():
            loss_ref[...] = jnp.zeros_like(loss_ref)
            dg_ref[...] = jnp.zeros_like(dg_ref)
        x = h_ref[...]
        g = g_ref[...]
        e = _rms(x, g) - t_ref[...]
        part = jnp.sum(jnp.sum(e * e, axis=-1, keepdims=True), axis=0, keepdims=True) * (0.5 / d)
        loss_ref[...] += jnp.broadcast_to(part, loss_ref.shape)
        dx, dg = _rms_bwd(x, g, e * (1.0 / d))
        dh_ref[...] = dx
        dg_ref[...] += dg

    return pl.pallas_call(
        body, grid=(n // tm,),
        in_specs=[pl.BlockSpec((tm, d), lambda i: (i, 0)), pl.BlockSpec((1, d), lambda i: (0, 0)),
                  pl.BlockSpec((tm, d), lambda i: (i, 0))],
        out_specs=[pl.BlockSpec((8, LANES), lambda i: (0, 0)), pl.BlockSpec((tm, d), lambda i: (i, 0)),
                   pl.BlockSpec((1, d), lambda i: (0, 0))],
        out_shape=[jax.ShapeDtypeStruct((8, LANES), F32), jax.ShapeDtypeStruct((n, d), F32),
                   jax.ShapeDtypeStruct((1, d), F32)], name=name,
        compiler_params=_cp(("arbitrary",)))(h, gamma, target)


def _halo_rows(dtype):
    return HALO * (4 // jnp.dtype(dtype).itemsize)


def _prev_halo_map(t, hb=HALO):
    return lambda i: (jnp.maximum(i * (t // hb) - 1, 0), 0)


def _next_halo_map(t, n, hb=HALO):
    return lambda i: (jnp.minimum((i + 1) * (t // hb), n // hb - 1), 0)


def _fill_prev(xs, main_ref, halo_ref, i, cols=slice(None)):
    hb = halo_ref.shape[0]
    xs[0:HALO, :] = jnp.where(i > 0, halo_ref[hb - HALO:hb, cols].astype(F32), 0.0)
    xs[HALO:, :] = main_ref[:, cols].astype(F32)


def _causal_conv(xs, w_ref, width, t, cols=slice(None)):
    x = xs[...]
    acc = w_ref[width - 1:width, cols] * x[HALO:, :]
    for k in range(width - 1):
        acc = acc + w_ref[k:k + 1, cols] * pltpu.roll(x, width - 1 - k, axis=0)[HALO:, :]
    return acc


def _col_chunks(width, target=COL_CHUNK):
    tc = _pick(width, target)
    return [slice(j * tc, (j + 1) * tc) for j in range(width // tc)]


def ffn_act_down(up, conv_w, conv_b, w_down, h, name):
    n, c2 = up.shape
    ff = c2 // 2
    d = h.shape[1]
    t = _pick(n, 2 * FFN_TILE, 8)
    hb = _halo_rows(up.dtype)
    chunks = _col_chunks(ff)
    tc = chunks[0].stop

    def body(up_ref, halo_ref, cw_ref, cb_ref, wd_ref, h_ref, o_ref, act_ref, hc_ref, xg, xv):
        i = pl.program_id(0)
        acc = h_ref[...]
        for cs in chunks:
            vs = slice(ff + cs.start, ff + cs.stop)
            _fill_prev(xg, up_ref, halo_ref, i, cs)
            _fill_prev(xv, up_ref, halo_ref, i, vs)
            gate = _causal_conv(xg, cw_ref, F_CONV, t, cs) + cb_ref[:, cs]
            val = _causal_conv(xv, cw_ref, F_CONV, t, vs) + cb_ref[:, vs]
            hc_ref[:, cs] = gate.astype(BF16)
            hc_ref[:, vs] = val.astype(BF16)
            act = (gate * _sig(gate) * val).astype(BF16)
            act_ref[:, cs] = act
            acc = acc + _dot(act, wd_ref[cs, :])
        o_ref[...] = acc

    return pl.pallas_call(
        body, grid=(n // t,),
        in_specs=[pl.BlockSpec((t, c2), lambda i: (i, 0)),
                  pl.BlockSpec((hb, c2), _prev_halo_map(t, hb)),
                  pl.BlockSpec((F_CONV, c2), lambda i: (0, 0)),
                  pl.BlockSpec((1, c2), lambda i: (0, 0)),
                  _resident((ff, d), lambda i: (0, 0)),
                  pl.BlockSpec((t, d), lambda i: (i, 0))],
        out_specs=[pl.BlockSpec((t, d), lambda i: (i, 0)), pl.BlockSpec((t, ff), lambda i: (i, 0)),
                   pl.BlockSpec((t, c2), lambda i: (i, 0))],
        out_shape=[jax.ShapeDtypeStruct((n, d), F32), jax.ShapeDtypeStruct((n, ff), BF16),
                   jax.ShapeDtypeStruct((n, c2), BF16)],
        scratch_shapes=[pltpu.VMEM((t + HALO, tc), F32), pltpu.VMEM((t + HALO, tc), F32)], name=name,
        compiler_params=_cp(("parallel",)))(up, up, conv_w, conv_b, w_down, h)


def ffn_bwd_act(dout, hc, w_down, name):
    n, c2 = hc.shape
    ff = c2 // 2
    d = dout.shape[1]
    t = _pick(n, 2 * FFN_TILE, 8)
    chunks = _col_chunks(ff)

    def body(dout_ref, hc_ref, wd_ref, dhc_ref, dcb_ref):
        i = pl.program_id(0)

        @pl.when(i == 0)
        def _():
            dcb_ref[...] = jnp.zeros_like(dcb_ref)
        doutb = dout_ref[...].astype(BF16)
        for cs in chunks:
            vs = slice(ff + cs.start, ff + cs.stop)
            gate = hc_ref[:, cs].astype(F32)
            val = hc_ref[:, vs].astype(F32)
            sg = _sig(gate)
            da = _dot_nt(doutb, wd_ref[cs, :])
            dgate = da * val * (sg * (1.0 + gate * (1.0 - sg)))
            dval = da * gate * sg
            dhc_ref[:, cs] = dgate.astype(BF16)
            dhc_ref[:, vs] = dval.astype(BF16)
            dcb_ref[:, cs] += jnp.sum(dgate, axis=0, keepdims=True)
            dcb_ref[:, vs] += jnp.sum(dval, axis=0, keepdims=True)

    return pl.pallas_call(
        body, grid=(n // t,),
        in_specs=[pl.BlockSpec((t, d), lambda i: (i, 0)),
                  pl.BlockSpec((t, c2), lambda i: (i, 0)),
                  _resident((ff, d), lambda i: (0, 0))],
        out_specs=[pl.BlockSpec((t, c2), lambda i: (i, 0)), pl.BlockSpec((1, c2), lambda i: (0, 0))],
        out_shape=[jax.ShapeDtypeStruct((n, c2), BF16), jax.ShapeDtypeStruct((1, c2), F32)], name=name,
        compiler_params=_cp(("arbitrary",)))(dout, hc, w_down)


def conv_bwd_tail(dy_ref, dnext_ref, x_ref, cw_ref, dcw_ref, ds, width, t, i, last, cols=slice(None)):
    ds[0:t, :] = dy_ref[:, cols].astype(F32)
    ds[t:, :] = jnp.where(i < last, dnext_ref[0:HALO, cols].astype(F32), 0.0)
    x = x_ref[:, cols].astype(F32)
    dall = ds[...]
    dx = None
    for k in range(width):
        off = width - 1 - k
        shifted = dall[0:t, :] if off == 0 else pltpu.roll(dall, t + HALO - off, axis=0)[0:t, :]
        term = cw_ref[k:k + 1, cols] * shifted
        dx = term if dx is None else dx + term
        dcw_ref[k:k + 1, cols] += jnp.sum(shifted * x, axis=0, keepdims=True)
    return dx


def ffn_bwd_up(dhc, up, conv_w, w_up, h, dout, gamma, name, rider=None):
    n, c2 = up.shape
    d = h.shape[1]
    t = _pick(n, FFN_TILE, 8)
    last = n // t - 1
    chunks = _col_chunks(c2)
    tc = chunks[0].stop

    def body(dhc_ref, dnext_ref, up_ref, cw_ref, wu_ref, h_ref, dout_ref, g_ref,
             dh_ref, dup_ref, dcw_ref, dg_ref, ds):
        i = pl.program_id(0)

        @pl.when(i == 0)
        def _():
            dcw_ref[...] = jnp.zeros_like(dcw_ref)
            dg_ref[...] = jnp.zeros_like(dg_ref)
        dxn = jnp.zeros((t, d), F32)
        for cs in chunks:
            dup = conv_bwd_tail(dhc_ref, dnext_ref, up_ref, cw_ref, dcw_ref, ds, F_CONV, t, i, last, cs)
            dupb = dup.astype(BF16)
            dup_ref[:, cs] = dupb
            dxn = dxn + _dot_nt(dupb, wu_ref[:, cs])
        dx, dg = _rms_bwd(h_ref[...], g_ref[...], dxn)
        dh_ref[...] = dout_ref[...] + dx
        dg_ref[...] += dg

    return host_call(
        body, grid=(n // t,), rider=rider, sem=("arbitrary",), args=(dhc, dhc, up, conv_w, w_up, h, dout, gamma),
        in_specs=[pl.BlockSpec((t, c2), lambda i: (i, 0)),
                  pl.BlockSpec((_halo_rows(dhc.dtype), c2), _next_halo_map(t, n, _halo_rows(dhc.dtype))),
                  pl.BlockSpec((t, c2), lambda i: (i, 0)),
                  pl.BlockSpec((F_CONV, c2), lambda i: (0, 0)),
                  _resident((d, c2), lambda i: (0, 0)),
                  pl.BlockSpec((t, d), lambda i: (i, 0)),
                  pl.BlockSpec((t, d), lambda i: (i, 0)),
                  pl.BlockSpec((1, d), lambda i: (0, 0))],
        out_specs=[pl.BlockSpec((t, d), lambda i: (i, 0)), pl.BlockSpec((t, c2), lambda i: (i, 0)),
                   pl.BlockSpec((F_CONV, c2), lambda i: (0, 0)), pl.BlockSpec((1, d), lambda i: (0, 0))],
        out_shape=[jax.ShapeDtypeStruct((n, d), F32), jax.ShapeDtypeStruct((n, c2), BF16),
                   jax.ShapeDtypeStruct((F_CONV, c2), F32), jax.ShapeDtypeStruct((1, d), F32)],
        scratch_shapes=[pltpu.VMEM((t + HALO, tc), F32)], name=name)


def _gdn_head(uq, uk, uv, pba, alog, dtb, head, n_heads):
    lane = lax.broadcasted_iota(jnp.int32, pba.shape, 1)
    sq = uq * _sig(uq)
    q = sq * lax.rsqrt(jnp.sum(sq * sq, axis=-1, keepdims=True) + EPS) * (A_HEAD ** -0.5)
    sk = uk * _sig(uk)
    k = sk * lax.rsqrt(jnp.sum(sk * sk, axis=-1, keepdims=True) + EPS)
    v = uv * _sig(uv)
    beta = jnp.sum(jnp.where(lane == head, _sig(pba), 0.0), axis=-1, keepdims=True)
    g_all = -jnp.exp(alog) * _softplus(pba + dtb)
    g = jnp.sum(jnp.where(lane == n_heads + head, g_all, 0.0), axis=-1, keepdims=True)
    return q, k, v, jnp.broadcast_to(beta, uq.shape), jnp.broadcast_to(g, uq.shape)


def gdn_prep(proj, conv_w, alog, dtb, n_heads, name):
    n = proj.shape[0]
    qk = n_heads * A_HEAD
    cw = 3 * qk
    ba_blk = (cw + qk) // LANES
    t = _pick(n, 256, 8)

    def body(x_ref, halo_ref, pba_ref, cw_ref, al_ref, dt_ref, q_ref, k_ref, v_ref, b_ref, g_ref, xs):
        i = pl.program_id(0)
        xs[0:HALO, :] = jnp.where(i > 0, halo_ref[...], 0.0)
        xs[HALO:, :] = x_ref[...]
        u = _causal_conv(xs, cw_ref, A_CONV, t)
        pba = pba_ref[...]
        for hd in range(n_heads):
            s0 = slice(hd * A_HEAD, (hd + 1) * A_HEAD)
            s1 = slice(qk + hd * A_HEAD, qk + (hd + 1) * A_HEAD)
            s2 = slice(2 * qk + hd * A_HEAD, 2 * qk + (hd + 1) * A_HEAD)
            q, k, v, bb, gb = _gdn_head(u[:, s0], u[:, s1], u[:, s2], pba, al_ref[...], dt_ref[...], hd, n_heads)
            q_ref[:, s0] = q
            k_ref[:, s0] = k
            v_ref[:, s0] = v
            b_ref[:, s0] = bb
            g_ref[:, s0] = gb

    out = jax.ShapeDtypeStruct((n, qk), F32)
    return pl.pallas_call(
        body, grid=(n // t,),
        in_specs=[pl.BlockSpec((t, cw), lambda i: (i, 0)),
                  pl.BlockSpec((HALO, cw), _prev_halo_map(t)),
                  pl.BlockSpec((t, LANES), lambda i: (i, ba_blk)),
                  pl.BlockSpec((A_CONV, cw), lambda i: (0, 0)),
                  pl.BlockSpec((1, LANES), lambda i: (0, 0)),
                  pl.BlockSpec((1, LANES), lambda i: (0, 0))],
        out_specs=[pl.BlockSpec((t, qk), lambda i: (i, 0))] * 5,
        out_shape=[out] * 5,
        scratch_shapes=[pltpu.VMEM((t + HALO, cw), F32)], name=name,
        compiler_params=_cp(("parallel",)))(proj, proj, proj, conv_w, alog, dtb)


def gdn_prep_bwd(proj, conv_w, alog, dtb, dq, dk, dv, dbb, dgb, n_heads, name):
    n = proj.shape[0]
    qk = n_heads * A_HEAD
    cw = 3 * qk
    ba_blk = (cw + qk) // LANES
    t = _pick(n, 256, 8)

    def body(x_ref, halo_ref, pba_ref, cw_ref, al_ref, dt_ref, dq_ref, dk_ref, dv_ref, dbb_ref, dgb_ref,
             du_ref, dba_ref, dal_ref, ddt_ref, xs):
        i = pl.program_id(0)
        xs[0:HALO, :] = jnp.where(i > 0, halo_ref[...], 0.0)
        xs[HALO:, :] = x_ref[...]
        u = _causal_conv(xs, cw_ref, A_CONV, t)
        pba = pba_ref[...]
        lane0 = lax.broadcasted_iota(jnp.int32, (t, A_HEAD), 1) == 0
        dba = jnp.zeros((t, LANES), F32)
        dal = jnp.zeros((1, LANES), F32)
        ddt = jnp.zeros((1, LANES), F32)
        for hd in range(n_heads):
            s0 = slice(hd * A_HEAD, (hd + 1) * A_HEAD)
            s1 = slice(qk + hd * A_HEAD, qk + (hd + 1) * A_HEAD)
            s2 = slice(2 * qk + hd * A_HEAD, 2 * qk + (hd + 1) * A_HEAD)
            fn = functools.partial(_gdn_head, head=hd, n_heads=n_heads)
            _, vjp = jax.vjp(fn, u[:, s0], u[:, s1], u[:, s2], pba, al_ref[...], dt_ref[...])
            cts = (dq_ref[:, s0], dk_ref[:, s0], dv_ref[:, s0],
                   jnp.where(lane0, dbb_ref[:, s0], 0.0), jnp.where(lane0, dgb_ref[:, s0], 0.0))
            duq, duk, duv, dpba, da, dd = vjp(cts)
            du_ref[:, s0] = duq
            du_ref[:, s1] = duk
            du_ref[:, s2] = duv
            dba = dba + dpba
            dal = dal + da
            ddt = ddt + dd
        dba_ref[...] = dba

        @pl.when(i == 0)
        def _():
            dal_ref[...] = jnp.zeros_like(dal_ref)
            ddt_ref[...] = jnp.zeros_like(ddt_ref)
        dal_ref[...] += dal
        ddt_ref[...] += ddt

    tok = pl.BlockSpec((t, qk), lambda i: (i, 0))
    row = pl.BlockSpec((1, LANES), lambda i: (0, 0))
    return pl.pallas_call(
        body, grid=(n // t,),
        in_specs=[pl.BlockSpec((t, cw), lambda i: (i, 0)),
                  pl.BlockSpec((HALO, cw), _prev_halo_map(t)),
                  pl.BlockSpec((t, LANES), lambda i: (i, ba_blk)),
                  pl.BlockSpec((A_CONV, cw), lambda i: (0, 0)), row, row,
                  tok, tok, tok, tok, tok],
        out_specs=[pl.BlockSpec((t, cw), lambda i: (i, 0)), pl.BlockSpec((t, LANES), lambda i: (i, 0)), row, row],
        out_shape=[jax.ShapeDtypeStruct((n, cw), F32), jax.ShapeDtypeStruct((n, LANES), F32),
                   jax.ShapeDtypeStruct((1, LANES), F32), jax.ShapeDtypeStruct((1, LANES), F32)],
        scratch_shapes=[pltpu.VMEM((t + HALO, cw), F32)], name=name,
        compiler_params=_cp(("arbitrary",)))(proj, proj, proj, conv_w, alog, dtb, dq, dk, dv, dbb, dgb)


def conv_bwd(du, x, conv_w, width, name):
    n, cw = du.shape
    t = _pick(n, 256, 8)
    last = n // t - 1

    chunks = _col_chunks(cw)
    tc = chunks[0].stop

    def body(du_ref, dnext_ref, x_ref, cw_ref, dx_ref, dcw_ref, ds):
        i = pl.program_id(0)

        @pl.when(i == 0)
        def _():
            dcw_ref[...] = jnp.zeros_like(dcw_ref)
        for cs in chunks:
            dx_ref[:, cs] = conv_bwd_tail(du_ref, dnext_ref, x_ref, cw_ref, dcw_ref, ds, width, t, i, last, cs)

    return pl.pallas_call(
        body, grid=(n // t,),
        in_specs=[pl.BlockSpec((t, cw), lambda i: (i, 0)),
                  pl.BlockSpec((HALO, cw), _next_halo_map(t, n)),
                  pl.BlockSpec((t, cw), lambda i: (i, 0)),
                  pl.BlockSpec((width, cw), lambda i: (0, 0))],
        out_specs=[pl.BlockSpec((t, cw), lambda i: (i, 0)), pl.BlockSpec((width, cw), lambda i: (0, 0))],
        out_shape=[jax.ShapeDtypeStruct((n, cw), F32), jax.ShapeDtypeStruct((width, cw), F32)],
        scratch_shapes=[pltpu.VMEM((t + HALO, tc), F32)], name=name,
        compiler_params=_cp(("arbitrary",)))(du, du, x, conv_w)


def _b(x):
    return x.astype(BF16)


def _mm_nn(a, b):
    return _dot(_b(a), _b(b))


def _mm_nt(a, b):
    return _dot_nt(_b(a), _b(b))


def _mm_tn(a, b):
    return _dot_tn(_b(a), _b(b))


@jax.custom_vjp
def _mmg_nn(a, b):
    return _mm_nn(a, b)


_mmg_nn.defvjp(lambda a, b: (_mm_nn(a, b), (a, b)),
               lambda res, dc: (_mm_nt(dc, res[1]), _mm_tn(res[0], dc)))


@jax.custom_vjp
def _mmg_nt(a, b):
    return _mm_nt(a, b)


_mmg_nt.defvjp(lambda a, b: (_mm_nt(a, b), (a, b)),
               lambda res, dc: (_mm_nn(dc, res[1]), _mm_tn(dc, res[0])))


@jax.custom_vjp
def _mmg_tn(a, b):
    return _mm_tn(a, b)


_mmg_tn.defvjp(lambda a, b: (_mm_tn(a, b), (a, b)),
               lambda res, dc: (_mm_nt(res[1], dc), _mm_nn(res[0], dc)))


def _each(f, *lists):
    return [f(*a) for a in zip(*lists)]


def _unit_lower_inv(ms):
    c = ms[0].shape[0]
    eye = jnp.where(lax.broadcasted_iota(jnp.int32, (c, c), 0) == lax.broadcasted_iota(jnp.int32, (c, c), 1), 1.0, 0.0)
    xs = [eye - m for m in ms]
    pws = _each(_mm_nn, ms, ms)
    for it in range(5):
        xs = _each(lambda x, pw: x + _mm_nn(x, pw), xs, pws)
        if it < 4:
            pws = _each(_mm_nn, pws, pws)
    rs = _each(lambda m, x: eye - x - _hdot(m, x), ms, xs)
    return _each(lambda x, r: x + _mm_nn(x, r), xs, rs)


@jax.custom_vjp
def _saved_inv_g(ms, xs):
    return xs


_saved_inv_g.defvjp(lambda ms, xs: (xs, xs),
                    lambda xs, dxs: (_each(lambda t, x: -_mm_nt(t, x), _each(_mm_tn, xs, dxs), xs),
                                     [jnp.zeros_like(x) for x in xs]))


def _gdn_chunk(ops, state, q, k, v, bb, gb):
    nn, nt, tn, inv = ops
    c = CHUNK
    ri = lax.broadcasted_iota(jnp.int32, (c, c), 0)
    ci = lax.broadcasted_iota(jnp.int32, (c, c), 1)
    causal = ri >= ci
    strict = ri > ci
    tri = jnp.where(causal, 1.0, 0.0)
    gc = [_hdot(tri, g) for g in gb]
    decay = [jnp.where(causal, jnp.exp(jnp.where(causal, x[:, :c] - x.T[:c, :], 0.0)), 0.0) for x in gc]
    kb = _each(lambda a, b: a * b, k, bb)
    kk = _each(nt, kb, k)
    m = _each(lambda a, d: jnp.where(strict, a * d, 0.0), kk, decay)
    tinv = inv(m)
    egc = [jnp.exp(x) for x in gc]
    u = _each(nn, tinv, _each(lambda a, b: a * b, v, bb))
    w = _each(nn, tinv, _each(lambda a, b: a * b, kb, egc))
    attn = _each(lambda a, d: a * d, _each(nt, q, k), decay)
    glast = [jnp.sum(g, axis=0, keepdims=True) for g in gb]
    ws = _each(nn, w, state)
    v_new = _each(lambda a, b: a - b, u, ws)
    qs = _each(nn, _each(lambda a, b: a * b, q, egc), state)
    av = _each(nn, attn, v_new)
    o = _each(lambda a, b: a + b, qs, av)
    kv = _each(tn, _each(lambda a, gl, x: a * jnp.exp(gl - x), k, glast, gc), v_new)
    new_state = _each(lambda s, gl, a: s * jnp.exp(gl) + a, state, glast, kv)
    return o, new_state


def gdn_fwd(q, k, v, bb, gb, n_heads, name, rider=None):
    n, w = q.shape
    nc = n // CHUNK
    cb = min(8, nc)
    rows = cb * CHUNK

    def body(q_ref, k_ref, v_ref, b_ref, g_ref, o_ref, st_ref, ti_ref, s_scr):
        @pl.when(pl.program_id(0) == 0)
        def _():
            s_scr[...] = jnp.zeros_like(s_scr)

        def step(c, carry):
            sl = pl.ds(pl.multiple_of(c * CHUNK, CHUNK), CHUNK)
            lanes = [slice(hd * A_HEAD, (hd + 1) * A_HEAD) for hd in range(n_heads)]
            state = [s_scr[hd] for hd in range(n_heads)]
            inverses = []

            def inv(ms):
                inverses.extend(_unit_lower_inv(ms))
                return inverses

            o, new_state = _gdn_chunk((_mm_nn, _mm_nt, _mm_tn, inv), state,
                                      *[[r[sl, ls] for ls in lanes] for r in (q_ref, k_ref, v_ref, b_ref, g_ref)])
            for hd, ls in enumerate(lanes):
                st_ref[hd, pl.ds(c, 1)] = state[hd][None]
                ti_ref[hd, pl.ds(c, 1)] = inverses[hd].astype(BF16)[None]
                o_ref[sl, ls] = o[hd]
                s_scr[hd] = new_state[hd]
            return carry

        lax.fori_loop(0, cb, step, 0)

    tok = pl.BlockSpec((rows, w), lambda j: (j, 0))
    return host_call(
        body, grid=(nc // cb,),
        in_specs=[tok] * 5,
        out_specs=[tok, pl.BlockSpec((n_heads, cb, A_HEAD, A_HEAD), lambda j: (0, j, 0, 0)),
                   pl.BlockSpec((n_heads, cb, CHUNK, CHUNK), lambda j: (0, j, 0, 0))],
        out_shape=[jax.ShapeDtypeStruct(q.shape, F32), jax.ShapeDtypeStruct((n_heads, nc, A_HEAD, A_HEAD), F32),
                   jax.ShapeDtypeStruct((n_heads, nc, CHUNK, CHUNK), BF16)],
        scratch_shapes=[pltpu.VMEM((n_heads, A_HEAD, A_HEAD), F32)], name=name,
        sem=("arbitrary",), args=(q, k, v, bb, gb), rider=rider)


def gdn_bwd(q, k, v, bb, gb, states, tinv, do, n_heads, name, rider=None):
    n, w = q.shape
    nc = n // CHUNK
    cb = min(4, nc)
    rows = cb * CHUNK
    nblk = nc // cb

    def body(q_ref, k_ref, v_ref, b_ref, g_ref, st_ref, ti_ref, do_ref,
             dq_ref, dk_ref, dv_ref, db_ref, dg_ref, ds_scr):
        @pl.when(pl.program_id(0) == 0)
        def _():
            ds_scr[...] = jnp.zeros_like(ds_scr)

        def step(s, carry):
            c = cb - 1 - s
            sl = pl.ds(pl.multiple_of(c * CHUNK, CHUNK), CHUNK)
            for h0 in range(0, n_heads, GDN_BWD_HEADS):
                heads = list(range(h0, min(h0 + GDN_BWD_HEADS, n_heads)))
                lanes = [slice(hd * A_HEAD, (hd + 1) * A_HEAD) for hd in heads]
                state = [st_ref[hd, pl.ds(c, 1)][0] for hd in heads]
                saved = [ti_ref[hd, pl.ds(c, 1)][0].astype(F32) for hd in heads]
                chunk_fn = functools.partial(_gdn_chunk, (_mmg_nn, _mmg_nt, _mmg_tn, lambda ms: _saved_inv_g(ms, saved)))
                _, vjp = jax.vjp(chunk_fn, state, *[[r[sl, ls] for ls in lanes]
                                                    for r in (q_ref, k_ref, v_ref, b_ref, g_ref)])
                dstate, dq, dk, dv, dbb, dgb = vjp(([do_ref[sl, ls] for ls in lanes], [ds_scr[hd] for hd in heads]))
                for u, (hd, ls) in enumerate(zip(heads, lanes)):
                    ds_scr[hd] = dstate[u]
                    dq_ref[sl, ls] = dq[u]
                    dk_ref[sl, ls] = dk[u]
                    dv_ref[sl, ls] = dv[u]
                    db_ref[sl, ls] = jnp.broadcast_to(jnp.sum(dbb[u], axis=-1, keepdims=True), dbb[u].shape)
                    dg_ref[sl, ls] = jnp.broadcast_to(jnp.sum(dgb[u], axis=-1, keepdims=True), dgb[u].shape)
            return carry

        lax.fori_loop(0, cb, step, 0)

    tok = pl.BlockSpec((rows, w), lambda j: (nblk - 1 - j, 0))
    out = jax.ShapeDtypeStruct(q.shape, F32)
    return host_call(
        body, grid=(nblk,),
        in_specs=[tok] * 5 + [pl.BlockSpec((n_heads, cb, A_HEAD, A_HEAD), lambda j: (0, nblk - 1 - j, 0, 0)),
                              pl.BlockSpec((n_heads, cb, CHUNK, CHUNK), lambda j: (0, nblk - 1 - j, 0, 0)), tok],
        out_specs=[tok] * 5, out_shape=[out] * 5,
        scratch_shapes=[pltpu.VMEM((n_heads, A_HEAD, A_HEAD), F32)], name=name,
        sem=("arbitrary",), args=(q, k, v, bb, gb, states, tinv, do), rider=rider)


def _gdn_gate(oh, zh, w):
    r = lax.rsqrt(jnp.mean(oh * oh, axis=-1, keepdims=True) + EPS)
    return oh * r * w * (zh * _sig(zh))


def gdn_out(o, proj, out_norm, w_out, h, n_heads, name):
    n, vw = o.shape
    d = h.shape[1]
    z_blk = 3 * vw // vw
    t = _pick(n, 512, 8)

    def body(o_ref, z_ref, w_ref, wo_ref, h_ref, out_ref, y_ref):
        for hd in range(n_heads):
            s0 = slice(hd * A_HEAD, (hd + 1) * A_HEAD)
            y_ref[:, s0] = _gdn_gate(o_ref[:, s0], z_ref[:, s0], w_ref[...]).astype(BF16)
        out_ref[...] = h_ref[...] + _dot(y_ref[...], wo_ref[...])

    return pl.pallas_call(
        body, grid=(n // t,),
        in_specs=[pl.BlockSpec((t, vw), lambda i: (i, 0)),
                  pl.BlockSpec((t, vw), lambda i: (i, z_blk)),
                  pl.BlockSpec((1, A_HEAD), lambda i: (0, 0)),
                  _resident((vw, d), lambda i: (0, 0)),
                  pl.BlockSpec((t, d), lambda i: (i, 0))],
        out_specs=[pl.BlockSpec((t, d), lambda i: (i, 0)), pl.BlockSpec((t, vw), lambda i: (i, 0))],
        out_shape=[jax.ShapeDtypeStruct((n, d), F32), jax.ShapeDtypeStruct((n, vw), BF16)], name=name,
        compiler_params=_cp(("parallel",)))(o, proj, out_norm, w_out, h)


def gdn_out_bwd(dout, o, proj, out_norm, w_out, n_heads, name):
    n, vw = o.shape
    d = dout.shape[1]
    z_blk = 3
    t = _pick(n, 512, 8)

    def body(dout_ref, o_ref, z_ref, w_ref, wo_ref, do_ref, dz_ref, dw_ref):
        dy = _dot_nt(dout_ref[...].astype(BF16), wo_ref[...])
        dw = jnp.zeros((1, A_HEAD), F32)
        for hd in range(n_heads):
            s0 = slice(hd * A_HEAD, (hd + 1) * A_HEAD)
            _, vjp = jax.vjp(_gdn_gate, o_ref[:, s0], z_ref[:, s0], w_ref[...])
            doh, dzh, dwh = vjp(dy[:, s0])
            do_ref[:, s0] = doh
            dz_ref[:, s0] = dzh
            dw = dw + dwh

        @pl.when(pl.program_id(0) == 0)
        def _():
            dw_ref[...] = jnp.zeros_like(dw_ref)
        dw_ref[...] += dw

    tok = pl.BlockSpec((t, vw), lambda i: (i, 0))
    return pl.pallas_call(
        body, grid=(n // t,),
        in_specs=[pl.BlockSpec((t, d), lambda i: (i, 0)), tok,
                  pl.BlockSpec((t, vw), lambda i: (i, z_blk)),
                  pl.BlockSpec((1, A_HEAD), lambda i: (0, 0)),
                  _resident((vw, d), lambda i: (0, 0))],
        out_specs=[tok, tok, pl.BlockSpec((1, A_HEAD), lambda i: (0, 0))],
        out_shape=[jax.ShapeDtypeStruct((n, vw), F32), jax.ShapeDtypeStruct((n, vw), F32),
                   jax.ShapeDtypeStruct((1, A_HEAD), F32)], name=name,
        compiler_params=_cp(("arbitrary",)))(dout, o, proj, out_norm, w_out)


BIAS_LINE = 768
BIAS_TOP = BAND + CHUNK - 2


def _bias_line_onehot(clip, tbl_pad):
    r = lax.broadcasted_iota(jnp.int32, (tbl_pad, BIAS_LINE), 0)
    v = lax.broadcasted_iota(jnp.int32, (tbl_pad, BIAS_LINE), 1)
    idx = jnp.clip(BIAS_TOP - v - (CHUNK - 1), -clip, clip) + clip
    return jnp.where((r == idx) & (v <= BIAS_TOP), 1.0, 0.0)


def bias_expand(tbl, clip, name):
    nh, tp = tbl.shape

    def body(t_ref, o_ref):
        line = _hdot(t_ref[...], _bias_line_onehot(clip, tp))
        keep = lax.broadcasted_iota(jnp.int32, (nh, BAND_PAD), 1) < BAND
        for i in range(CHUNK):
            s = CHUNK - 1 - i
            rolled = line if s == 0 else pltpu.roll(line, BIAS_LINE - s, axis=1)
            o_ref[i] = jnp.where(keep, rolled[:, :BAND_PAD], 0.0)

    return pl.pallas_call(
        body, in_specs=[pl.BlockSpec(memory_space=pltpu.VMEM)], out_specs=pl.BlockSpec(memory_space=pltpu.VMEM),
        out_shape=jax.ShapeDtypeStruct((CHUNK, nh, BAND_PAD), F32), name=name, compiler_params=_cp())(tbl)


def bias_expand_bwd(dbias, clip, tp, name):
    _, nh, _ = dbias.shape

    def body(d_ref, o_ref):
        keep = lax.broadcasted_iota(jnp.int32, (nh, BAND_PAD), 1) < BAND
        pad = jnp.zeros((nh, BIAS_LINE - BAND_PAD), F32)
        acc = jnp.zeros((nh, BIAS_LINE), F32)
        for i in range(CHUNK):
            s = CHUNK - 1 - i
            d = jnp.concatenate([jnp.where(keep, d_ref[i], 0.0), pad], axis=1)
            acc = acc + (d if s == 0 else pltpu.roll(d, s, axis=1))
        o_ref[...] = _hdot_nt(acc, _bias_line_onehot(clip, tp))

    return pl.pallas_call(
        body, in_specs=[pl.BlockSpec(memory_space=pltpu.VMEM)], out_specs=pl.BlockSpec(memory_space=pltpu.VMEM),
        out_shape=jax.ShapeDtypeStruct((nh, tp), F32), name=name, compiler_params=_cp())(dbias)


ATT_TILE = LEFT_CHUNKS * CHUNK


ATT_GROUP = 8


def _att_softmax(s, bias, n_chunk):
    s = s * (B_HEAD ** -0.5) + bias
    slot = lax.broadcasted_iota(jnp.int32, s.shape, 1)
    valid = (slot >= (LEFT_CHUNKS - n_chunk) * CHUNK) & (slot < BAND)
    s = jnp.where(valid, s, NEG_INF)
    p = jnp.exp(s - jnp.max(s, axis=-1, keepdims=True))
    return p / jnp.sum(p, axis=-1, keepdims=True)


def _att_specs(n_pairs):
    prev = lambda p, i: (jnp.maximum(i - 1, 0), p)
    cur = lambda p, i: (i, p)
    prev_v = lambda p, i: (jnp.maximum(i - 1, 0), n_pairs + p)
    cur_v = lambda p, i: (i, n_pairs + p)
    blk = (ATT_TILE, LANES)
    return [pl.BlockSpec(blk, prev), pl.BlockSpec(blk, cur), pl.BlockSpec(blk, prev_v), pl.BlockSpec(blk, cur_v)]


def _att_fill(kbuf, vbuf, kp_ref, kc_ref, vp_ref, vc_ref):
    t = ATT_TILE
    kbuf[0:t, :] = kp_ref[...].astype(BF16)
    kbuf[t:2 * t, :] = kc_ref[...].astype(BF16)
    kbuf[2 * t:, :] = jnp.zeros((CHUNK, LANES), BF16)
    vbuf[0:t, :] = vp_ref[...].astype(BF16)
    vbuf[t:2 * t, :] = vc_ref[...].astype(BF16)
    vbuf[2 * t:, :] = jnp.zeros((CHUNK, LANES), BF16)


def _stack_heads(x, first):
    return jnp.concatenate([jnp.where(first, x, 0.0), jnp.where(first, 0.0, x)], axis=0).astype(BF16)


def attn_fwd(qp, kv, bias, name, rider=None):
    n, bw = qp.shape
    n_pairs = bw // LANES
    t = ATT_TILE
    cpt = t // CHUNK

    def body(q_ref, kp_ref, kc_ref, vp_ref, vc_ref, b_ref, o_ref, kbuf, vbuf):
        i = pl.program_id(1)
        _att_fill(kbuf, vbuf, kp_ref, kc_ref, vp_ref, vc_ref)
        lane = lax.broadcasted_iota(jnp.int32, (CHUNK, LANES), 1)
        first = lane < B_HEAD
        for g0 in range(0, cpt, ATT_GROUP):
            chunks = list(range(g0, min(g0 + ATT_GROUP, cpt)))
            band = [slice(c * CHUNK, c * CHUNK + BAND_PAD) for c in chunks]
            q2 = [_stack_heads(q_ref[c * CHUNK:(c + 1) * CHUNK, :], first) for c in chunks]
            s = [_dot_nt(q_u, kbuf[b_u, :]) for q_u, b_u in zip(q2, band)]
            p = [_att_softmax(s_u, b_ref[0], i * cpt + c) for s_u, c in zip(s, chunks)]
            o = [_dot(p_u.astype(BF16), vbuf[b_u, :]) for p_u, b_u in zip(p, band)]
            for o_u, c in zip(o, chunks):
                o_ref[c * CHUNK:(c + 1) * CHUNK, :] = jnp.where(first, o_u[:CHUNK], o_u[CHUNK:])

    return host_call(
        body, grid=(n_pairs, n // t),
        in_specs=[pl.BlockSpec((t, LANES), lambda p, i: (i, p))] + _att_specs(n_pairs)
        + [pl.BlockSpec((1, 2 * CHUNK, BAND_PAD), lambda p, i: (p, 0, 0))],
        out_specs=[pl.BlockSpec((t, LANES), lambda p, i: (i, p))],
        out_shape=[jax.ShapeDtypeStruct((n, bw), F32)],
        scratch_shapes=[pltpu.VMEM((2 * t + CHUNK, LANES), BF16), pltpu.VMEM((2 * t + CHUNK, LANES), BF16)],
        name=name, sem=("parallel", "parallel"), args=(qp, kv, kv, kv, kv, bias), rider=rider)


def attn_bwd(qp, kv, bias, d_o, dk_in, dv_in, name, rider=None):
    n, bw = qp.shape
    n_pairs = bw // LANES
    t = ATT_TILE
    cpt = t // CHUNK
    nt = n // t
    have_in = dk_in is not None
    scale = B_HEAD ** -0.5

    def body(*refs):
        q_ref, kp_ref, kc_ref, vp_ref, vc_ref, b_ref, do_ref = refs[:7]
        pos = 7
        if have_in:
            dki_ref, dvi_ref = refs[7:9]
            pos = 9
        dq_ref, dk_ref, dv_ref, db_ref, kbuf, vbuf, dkacc, dvacc = refs[pos:]
        j = pl.program_id(1)
        i = nt - 1 - j
        _att_fill(kbuf, vbuf, kp_ref, kc_ref, vp_ref, vc_ref)

        @pl.when(j == 0)
        def _():
            dkacc[...] = jnp.zeros_like(dkacc)
            dvacc[...] = jnp.zeros_like(dvacc)
            db_ref[...] = jnp.zeros_like(db_ref)

        @pl.when(j > 0)
        def _():
            dkacc[t:2 * t, :] = dkacc[0:t, :]
            dvacc[t:2 * t, :] = dvacc[0:t, :]
            dkacc[0:t, :] = jnp.zeros((t, LANES), F32)
            dvacc[0:t, :] = jnp.zeros((t, LANES), F32)

        lane = lax.broadcasted_iota(jnp.int32, (CHUNK, LANES), 1)
        first = lane < B_HEAD
        for g0 in range(0, cpt, ATT_GROUP):
            chunks = list(range(g0, min(g0 + ATT_GROUP, cpt)))
            rows = [slice(c * CHUNK, (c + 1) * CHUNK) for c in chunks]
            band = [slice(c * CHUNK, c * CHUNK + BAND_PAD) for c in chunks]
            q2 = [_stack_heads(q_ref[r, :], first) for r in rows]
            do2 = [_stack_heads(do_ref[r, :], first) for r in rows]
            s = [_dot_nt(q_u, kbuf[b_u, :]) for q_u, b_u in zip(q2, band)]
            dp = [_dot_nt(d_u, vbuf[b_u, :]) for d_u, b_u in zip(do2, band)]
            p = [_att_softmax(s_u, b_ref[0], i * cpt + c) for s_u, c in zip(s, chunks)]
            ds = [p_u * (dp_u - jnp.sum(dp_u * p_u, axis=-1, keepdims=True)) for p_u, dp_u in zip(p, dp)]
            dsb = [(d_u * scale).astype(BF16) for d_u in ds]
            dv = [_dot_tn(p_u.astype(BF16), d_u) for p_u, d_u in zip(p, do2)]
            dq = [_dot(d_u, kbuf[b_u, :]) for d_u, b_u in zip(dsb, band)]
            dk = [_dot_tn(d_u, q_u) for d_u, q_u in zip(dsb, q2)]
            db_ref[0] += functools.reduce(lambda a, b: a + b, ds)
            for r in range(chunks[0], chunks[-1] + BAND // CHUNK):
                terms = [(u, r - c) for u, c in enumerate(chunks) if 0 <= r - c < BAND // CHUNK]
                blk = slice(r * CHUNK, (r + 1) * CHUNK)
                dvacc[blk, :] += functools.reduce(lambda a, b: a + b, [dv[u][o * CHUNK:(o + 1) * CHUNK] for u, o in terms])
                dkacc[blk, :] += functools.reduce(lambda a, b: a + b, [dk[u][o * CHUNK:(o + 1) * CHUNK] for u, o in terms])
            for u in range(len(chunks)):
                dq_ref[rows[u], :] = jnp.where(first, dq[u][:CHUNK], dq[u][CHUNK:])

        if have_in:
            dk_ref[...] = dkacc[t:2 * t, :] + dki_ref[...]
            dv_ref[...] = dvacc[t:2 * t, :] + dvi_ref[...]
        else:
            dk_ref[...] = dkacc[t:2 * t, :]
            dv_ref[...] = dvacc[t:2 * t, :]

    rev = lambda p, j: (nt - 1 - j, p)
    tok = pl.BlockSpec((t, LANES), rev)
    kv_specs = [pl.BlockSpec((t, LANES), lambda p, j: (jnp.maximum(nt - 2 - j, 0), p)),
                pl.BlockSpec((t, LANES), rev),
                pl.BlockSpec((t, LANES), lambda p, j: (jnp.maximum(nt - 2 - j, 0), n_pairs + p)),
                pl.BlockSpec((t, LANES), lambda p, j: (nt - 1 - j, n_pairs + p))]
    in_specs = [tok] + kv_specs + [pl.BlockSpec((1, 2 * CHUNK, BAND_PAD), lambda p, j: (p, 0, 0)), tok]
    args = [qp, kv, kv, kv, kv, bias, d_o]
    if have_in:
        in_specs += [tok, tok]
        args += [dk_in, dv_in]
    out = jax.ShapeDtypeStruct((n, bw), F32)
    return host_call(
        body, grid=(n_pairs, nt), in_specs=in_specs,
        out_specs=[tok, tok, tok, pl.BlockSpec((1, 2 * CHUNK, BAND_PAD), lambda p, j: (p, 0, 0))],
        out_shape=[out, out, out, jax.ShapeDtypeStruct((n_pairs, 2 * CHUNK, BAND_PAD), F32)],
        scratch_shapes=[pltpu.VMEM((2 * t + CHUNK, LANES), BF16), pltpu.VMEM((2 * t + CHUNK, LANES), BF16),
                        pltpu.VMEM((2 * t + CHUNK, LANES), F32), pltpu.VMEM((2 * t + CHUNK, LANES), F32)],
        name=name, sem=("parallel", "arbitrary"), args=args, rider=rider)


def adamw(w, gstack, m, v, name):
    r, c = w.shape
    s = gstack.shape[0]
    tr = _pick(r, 512, 8)

    def body(w_ref, g_ref, m_ref, v_ref, go_ref, d_ref, mo_ref, vo_ref):
        g = g_ref[0].astype(F32)
        for k in range(1, s):
            g = g + g_ref[k].astype(F32)
        mn = ADAM_B1 * m_ref[...] + (1.0 - ADAM_B1) * g
        vn = ADAM_B2 * v_ref[...] + (1.0 - ADAM_B2) * (g * g)
        m_hat = mn / (1.0 - ADAM_B1 ** ADAM_STEP)
        v_hat = vn / (1.0 - ADAM_B2 ** ADAM_STEP)
        go_ref[...] = g
        d_ref[...] = -ADAM_LR * (m_hat / (jnp.sqrt(v_hat) + ADAM_EPS) + ADAM_WD * w_ref[...])
        mo_ref[...] = mn
        vo_ref[...] = vn

    blk = pl.BlockSpec((tr, c), lambda i: (i, 0))
    out = jax.ShapeDtypeStruct((r, c), F32)
    return pl.pallas_call(
        body, grid=(r // tr,),
        in_specs=[blk, pl.BlockSpec((s, tr, c), lambda i: (0, i, 0)), blk, blk],
        out_specs=[blk] * 4, out_shape=[out] * 4, name=name,
        compiler_params=_cp(("parallel",)))(w, gstack, m, v)


def _place():
    x, y, c = lax.axis_index("x"), lax.axis_index("y"), lax.axis_index("c")
    chips = [(1 - x, y), (x, 1 - y), (1 - x, 1 - y)]
    return x, y, c, chips


def _ag_copy(outs, send_sems, recv_sems, t, k, block, to, src=None):
    def slot(dev):
        return outs[t].at[4 * dev[0] + 2 * dev[1] + dev[2]]
    return pltpu.make_async_remote_copy(
        src_ref=slot(block) if src is None else src, dst_ref=slot(block),
        send_sem=send_sems.at[7 * t + k], recv_sem=recv_sems.at[7 * t + k], device_id=to, device_id_type=MESH)


def _ag_start(ins, outs, send_sems, recv_sems, local_sems):
    x, y, c, chips = _place()
    me = (x, y, c)
    for t in range(len(ins)):
        pltpu.make_async_copy(ins[t], outs[t].at[4 * x + 2 * y + c], local_sems.at[t]).start()
        _ag_copy(outs, send_sems, recv_sems, t, 0, me, (x, y, 1 - c), src=ins[t]).start()
        for j, chip in enumerate(chips):
            _ag_copy(outs, send_sems, recv_sems, t, 1 + j, me, (*chip, c), src=ins[t]).start()


def _ag_finish(ins, outs, send_sems, recv_sems, local_sems):
    x, y, c, chips = _place()
    me, sibling = (x, y, c), (x, y, 1 - c)
    nt = len(ins)
    for t in range(nt):
        for j, chip in enumerate(chips):
            _ag_copy(outs, send_sems, recv_sems, t, 1 + j, (*chip, c), me).wait_recv()
            _ag_copy(outs, send_sems, recv_sems, t, 4 + j, (*chip, c), sibling).start()
    for t in range(nt):
        _ag_copy(outs, send_sems, recv_sems, t, 0, sibling, me).wait_recv()
        for j, chip in enumerate(chips):
            _ag_copy(outs, send_sems, recv_sems, t, 4 + j, (*chip, 1 - c), me).wait_recv()
    for t in range(nt):
        _ag_copy(outs, send_sems, recv_sems, t, 0, me, sibling, src=ins[t]).wait_send()
        for j, chip in enumerate(chips):
            _ag_copy(outs, send_sems, recv_sems, t, 1 + j, me, (*chip, c), src=ins[t]).wait_send()
            _ag_copy(outs, send_sems, recv_sems, t, 4 + j, (*chip, c), sibling).wait_send()
        pltpu.make_async_copy(ins[t], outs[t].at[4 * x + 2 * y + c], local_sems.at[t]).wait()


def _rs_a_copy(ins, outs, send_sems, recv_sems, t, q):
    x, y, c, _ = _place()
    return pltpu.make_async_remote_copy(
        src_ref=ins[t].at[2 * q + (1 - c)], dst_ref=outs[t].at[q],
        send_sem=send_sems.at[4 * t + q], recv_sem=recv_sems.at[4 * t + q],
        device_id=(x, y, 1 - c), device_id_type=MESH)


def _rs_a_start(ins, outs, send_sems, recv_sems, local_sems):
    for t in range(len(ins)):
        for q in range(4):
            _rs_a_copy(ins, outs, send_sems, recv_sems, t, q).start()


def _rs_a_finish(ins, outs, send_sems, recv_sems, local_sems):
    for t in range(len(ins)):
        for q in range(4):
            _rs_a_copy(ins, outs, send_sems, recv_sems, t, q).wait_recv()
    for t in range(len(ins)):
        for q in range(4):
            _rs_a_copy(ins, outs, send_sems, recv_sems, t, q).wait_send()


def _rs_b_copy(ins, outs, send_sems, recv_sems, t, j, sending):
    x, y, c, chips = _place()
    mine, other = 2 * x + y, 2 * chips[j][0] + chips[j][1]
    return pltpu.make_async_remote_copy(
        src_ref=ins[t].at[other if sending else mine], dst_ref=outs[t].at[mine if sending else other],
        send_sem=send_sems.at[3 * t + j], recv_sem=recv_sems.at[3 * t + j],
        device_id=(*chips[j], c), device_id_type=MESH)


def _rs_b_start(ins, outs, send_sems, recv_sems, local_sems):
    x, y, _, _ = _place()
    for t in range(len(ins)):
        for j in range(3):
            _rs_b_copy(ins, outs, send_sems, recv_sems, t, j, True).start()
        pltpu.make_async_copy(ins[t].at[2 * x + y], outs[t].at[2 * x + y], local_sems.at[t]).start()


def _rs_b_finish(ins, outs, send_sems, recv_sems, local_sems):
    x, y, _, _ = _place()
    for t in range(len(ins)):
        for j in range(3):
            _rs_b_copy(ins, outs, send_sems, recv_sems, t, j, False).wait_recv()
    for t in range(len(ins)):
        for j in range(3):
            _rs_b_copy(ins, outs, send_sems, recv_sems, t, j, True).wait_send()
        pltpu.make_async_copy(ins[t].at[2 * x + y], outs[t].at[2 * x + y], local_sems.at[t]).wait()


_EXCHANGES = {
    "all_gather": (7, lambda a: (N_DEV, *a.shape), _ag_start, _ag_finish),
    "rs_sibling": (4, lambda a: (4, *a.shape[1:]), _rs_a_start, _rs_a_finish),
    "rs_chips": (3, lambda a: a.shape, _rs_b_start, _rs_b_finish),
}


def _exchange_parts(kind, arrays):
    per, shape_of, start, finish = _EXCHANGES[kind]
    n = len(arrays)
    out_shape = [jax.ShapeDtypeStruct(shape_of(a), a.dtype) for a in arrays]
    sems = [pltpu.SemaphoreType.DMA((per * n,)), pltpu.SemaphoreType.DMA((per * n,)), pltpu.SemaphoreType.DMA((n,))]
    return out_shape, sems, start, finish


def exchange(kind, arrays, name):
    n = len(arrays)
    out_shape, sems, start, finish = _exchange_parts(kind, arrays)
    any_spec = pl.BlockSpec(memory_space=pl.ANY)

    def body(*refs):
        ins, outs, sem_refs = refs[:n], refs[n:2 * n], refs[2 * n:]
        start(ins, outs, *sem_refs)
        finish(ins, outs, *sem_refs)

    return pl.pallas_call(body, in_specs=[any_spec] * n, out_specs=[any_spec] * n, out_shape=out_shape,
                          scratch_shapes=sems, name=name)(*arrays)


def host_call(body, *, grid, in_specs, out_specs, out_shape, scratch_shapes, args, name, sem, rider=None):
    if rider is None:
        outs = pl.pallas_call(body, grid=grid, in_specs=in_specs, out_specs=out_specs, out_shape=out_shape,
                              scratch_shapes=scratch_shapes, name=name, compiler_params=_cp(sem))(*args)
        return outs, None
    kind, arrays = rider
    nr, ni, no, ns = len(arrays), len(in_specs), len(out_specs), len(scratch_shapes)
    r_shape, r_sems, start, finish = _exchange_parts(kind, arrays)
    any_spec = pl.BlockSpec(memory_space=pl.ANY)

    def wrapped(*refs):
        ins, r_ins = refs[:ni], refs[ni:ni + nr]
        outs, r_outs = refs[ni + nr:ni + nr + no], refs[ni + nr + no:ni + 2 * nr + no]
        scratch, sem_refs = refs[ni + 2 * nr + no:ni + 2 * nr + no + ns], refs[ni + 2 * nr + no + ns:]
        first = pl.program_id(0) == 0
        last = pl.program_id(0) == grid[0] - 1
        for ax in range(1, len(grid)):
            first = first & (pl.program_id(ax) == 0)
            last = last & (pl.program_id(ax) == grid[ax] - 1)

        @pl.when(first)
        def _():
            start(r_ins, r_outs, *sem_refs)
        body(*ins, *outs, *scratch)

        @pl.when(last)
        def _():
            finish(r_ins, r_outs, *sem_refs)

    outs = pl.pallas_call(
        wrapped, grid=grid, in_specs=list(in_specs) + [any_spec] * nr, out_specs=list(out_specs) + [any_spec] * nr,
        out_shape=list(out_shape) + r_shape, scratch_shapes=list(scratch_shapes) + r_sems, name=name,
        compiler_params=_cp(("arbitrary",) * len(grid)))(*args, *arrays)
    return outs[:no], outs[no:]


def pair_add(g8, recv, c_idx, name):
    _, r, c = g8.shape
    tr = _pick(r, 512, 8)

    def body(c_ref, g_ref, r_ref, o_ref):
        o_ref[...] = (g_ref[...] + r_ref[...]).astype(BF16)

    return pl.pallas_call(
        body,
        grid_spec=pltpu.PrefetchScalarGridSpec(
            num_scalar_prefetch=1, grid=(4, r // tr),
            in_specs=[pl.BlockSpec((1, tr, c), lambda q, i, cr: (2 * q + cr[0], i, 0)),
                      pl.BlockSpec((1, tr, c), lambda q, i, cr: (q, i, 0))],
            out_specs=pl.BlockSpec((1, tr, c), lambda q, i, cr: (q, i, 0))),
        out_shape=jax.ShapeDtypeStruct((4, r, c), BF16), name=name,
        compiler_params=_cp(("parallel", "parallel")))(c_idx, g8, recv)


def all_reduce_small(pack, name):
    r, c = pack.shape

    def body(x_ref, o_ref, buf, send_sems, recv_sems, local_sem):
        x, y, cc, chips = _place()
        me, sibling = (x, y, cc), (x, y, 1 - cc)

        def slot(dev):
            return buf.at[4 * dev[0] + 2 * dev[1] + dev[2]]

        def copy(k, block, to, src=None):
            return pltpu.make_async_remote_copy(
                src_ref=slot(block) if src is None else src, dst_ref=slot(block),
                send_sem=send_sems.at[k], recv_sem=recv_sems.at[k], device_id=to, device_id_type=MESH)

        mine = pltpu.make_async_copy(x_ref, slot(me), local_sem)
        mine.start()
        first = [copy(0, me, sibling, src=x_ref)]
        first += [copy(1 + j, me, (*chip, cc), src=x_ref) for j, chip in enumerate(chips)]
        for cp in first:
            cp.start()
        passed = [copy(4 + j, (*chip, cc), sibling) for j, chip in enumerate(chips)]
        for j, chip in enumerate(chips):
            copy(1 + j, (*chip, cc), me).wait_recv()
            passed[j].start()
        copy(0, sibling, me).wait_recv()
        for j, chip in enumerate(chips):
            copy(4 + j, (*chip, 1 - cc), me).wait_recv()
        for cp in first + passed:
            cp.wait_send()
        mine.wait()
        acc = buf[0]
        for k in range(1, N_DEV):
            acc = acc + buf[k]
        o_ref[...] = acc

    return pl.pallas_call(
        body, in_specs=[pl.BlockSpec(memory_space=pltpu.VMEM)],
        out_specs=pl.BlockSpec(memory_space=pltpu.VMEM),
        out_shape=jax.ShapeDtypeStruct((r, c), F32),
        scratch_shapes=[pltpu.VMEM((N_DEV, r, c), F32), pltpu.SemaphoreType.DMA((7,)),
                        pltpu.SemaphoreType.DMA((7,)), pltpu.SemaphoreType.DMA],
        name=name, compiler_params=_cp())(pack)


def _row(v):
    return v.reshape(1, -1)


def _lane_row(vals, offset):
    return jnp.pad(vals, (offset, LANES - offset - vals.shape[0])).reshape(1, LANES)


def _bias_to_pairs(b):
    i, nh, bp = b.shape
    return b.transpose(1, 0, 2).reshape(nh // 2, 2 * i, bp)


def _bias_from_pairs(b):
    p, i2, bp = b.shape
    return b.reshape(2 * p, i2 // 2, bp).transpose(1, 0, 2)


class LocalWeights:
    def __init__(self, W):
        self.W = W
        self.grads = {}

    def big(self, l, la):
        W = self.W
        out = {"f_w_up": W["f_w_up"][l], "f_w_down": W["f_w_down"][l]}
        if l < la:
            out.update(a_w_in=W["a_w_in"][l], a_w_out=W["a_w_out"][l])
        else:
            out.update(b_w_q=W["b_w_q"][l - la], b_w_out=W["b_w_out"][l - la])
        if l == la:
            out["w_kv"] = W["w_kv"]
        return out

    col_blocks = None

    def fwd_rider(self, l):
        return None

    def fwd_got(self, l, got):
        pass

    def bwd_rider_a(self, l):
        return None

    def bwd_got_a(self, l, got):
        pass

    def bwd_rider_b(self, l):
        return None

    def bwd_got_b(self, l, got):
        pass

    def grads_ready(self, l, grads):
        for k_, g in grads.items():
            self.grads.setdefault(k_, {})[l] = g

    def stacked(self):
        return {k_: (jnp.stack([v_[l] for l in sorted(v_)]) if k_ != "w_kv" else next(iter(v_.values())))
                for k_, v_ in self.grads.items()}


def _named(name, l, rider):
    return name if rider is None else f"{name}_x{l}"


def local_step(x, target, W, comm=None):
    comm = LocalWeights(W) if comm is None else comm
    n, d = x.shape
    la, ha = W["a_A_log"].shape
    lb, hb, tbl = W["b_rel_bias"].shape
    depth = W["f_norm"].shape[0]
    clip = (tbl - 1) // 2
    tp = -(-tbl // LANES) * LANES
    qk = ha * A_HEAD
    cw = 3 * qk
    bw = hb * B_HEAD
    a_in = cw + qk + 2 * ha

    h = x
    saves = []
    kv = h_kv = w_kv = None
    for l in range(depth):
        big = comm.big(l, la)
        sv = {"h_in": h, "big": big}
        rider = comm.fwd_rider(l)
        if l < la:
            alog = _lane_row(W["a_A_log"][l], ha)
            dtb = _lane_row(W["a_dt_bias"][l], ha)
            proj = norm_matmul(h, _row(W["a_norm"][l]), big["a_w_in"], "a_in_proj")
            q, k, v, bb, gb = gdn_prep(proj, W["a_conv"][l], alog, dtb, ha, "gdn_prep")
            (o, states, tinv), got = gdn_fwd(q, k, v, bb, gb, ha, _named("gdn_fwd", l, rider), rider)
            h, y = gdn_out(o, proj, _row(W["a_out_norm"][l]), big["a_w_out"], h, ha, "gdn_out")
            sv.update(proj=proj, q=q, k=k, v=v, bb=bb, gb=gb, states=states, tinv=tinv, o=o, y=y, alog=alog, dtb=dtb)
        else:
            j = l - la
            if j == 0:
                h_kv, w_kv = h, big["w_kv"]
                kv = norm_matmul(h, _row(W["kv_norm"]), w_kv, "kv_proj")
            qp = norm_matmul(h, _row(W["b_norm"][j]), big["b_w_q"], "b_q_proj")
            tblp = jnp.pad(W["b_rel_bias"][j], ((0, 0), (0, tp - tbl)))
            bias = _bias_to_pairs(bias_expand(tblp, clip, "bias_expand"))
            (o,), got = attn_fwd(qp, kv, bias, _named("attn_fwd", l, rider), rider)
            h = matmul_res(o, big["b_w_out"], h, "b_out_proj")
            sv.update(qp=qp, bias=bias, o=o)
        comm.fwd_got(l, got)
        sv["h_mid"] = h
        up = norm_matmul(h, _row(W["f_norm"][l]), big["f_w_up"], "f_up_proj", out_dtype=BF16)
        h, act, hc = ffn_act_down(up, W["f_conv"][l], _row(W["f_conv_b"][l]), big["f_w_down"], h, "ffn_act_down")
        sv.update(up=up, act=act, hc=hc)
        saves.append(sv)

    loss, dh, d_final = loss_head(h, _row(W["final_norm"]), target)

    G = {k_: [None] * (la if k_.startswith("a_") else lb if k_.startswith("b_") else depth)
         for k_ in ("a_norm", "a_conv", "a_A_log", "a_dt_bias", "a_out_norm",
                    "b_norm", "b_rel_bias", "f_norm", "f_conv", "f_conv_b")}
    G["final_norm"] = d_final[0]
    dk_acc = dv_acc = None
    for l in reversed(range(depth)):
        sv = saves[l]
        big = sv["big"]
        gbig = {}
        dhc, dcb = ffn_bwd_act(dh, sv["hc"], big["f_w_down"], "ffn_bwd_act")
        gbig["f_w_down"] = matmul_tn(sv["act"], dh, "f_down_wgrad")
        G["f_conv_b"][l] = dcb[0]
        rider = comm.bwd_rider_a(l)
        (dh, dup, dcw, dg), got = ffn_bwd_up(dhc, sv["up"], W["f_conv"][l], big["f_w_up"], sv["h_mid"], dh,
                                             _row(W["f_norm"][l]), _named("ffn_bwd_up", l, rider), rider)
        comm.bwd_got_a(l, got)
        G["f_conv"][l] = dcw
        G["f_norm"][l] = dg[0]
        gbig["f_w_up"] = norm_matmul_tn(sv["h_mid"], _row(W["f_norm"][l]), dup, "f_up_wgrad", comm.col_blocks)
        rider = comm.bwd_rider_b(l)
        if l < la:
            w_in = big["a_w_in"]
            do, dz, dwn = gdn_out_bwd(dh, sv["o"], sv["proj"], _row(W["a_out_norm"][l]), big["a_w_out"], ha, "gdn_out_bwd")
            G["a_out_norm"][l] = dwn[0]
            gbig["a_w_out"] = matmul_tn(sv["y"], dh, "a_out_wgrad")
            (dq, dk, dv, dbb, dgb), got = gdn_bwd(sv["q"], sv["k"], sv["v"], sv["bb"], sv["gb"], sv["states"], sv["tinv"], do, ha,
                                                  _named("gdn_bwd", l, rider), rider)
            comm.bwd_got_b(l, got)
            du, dba, dal, ddt = gdn_prep_bwd(sv["proj"], W["a_conv"][l], sv["alog"], sv["dtb"],
                                             dq, dk, dv, dbb, dgb, ha, "gdn_prep_bwd")
            G["a_A_log"][l] = dal[0, ha:2 * ha]
            G["a_dt_bias"][l] = ddt[0, ha:2 * ha]
            dqkv, dconv = conv_bwd(du, sv["proj"], W["a_conv"][l], A_CONV, "gdn_conv_bwd")
            G["a_conv"][l] = dconv
            gam = _row(W["a_norm"][l])
            pieces = [(dqkv, w_in[:, :cw]), (dz, w_in[:, cw:cw + qk]), (dba, w_in[:, cw + qk:])]
            gbig["a_w_in"] = jnp.concatenate(
                [norm_matmul_tn(sv["h_in"], gam, dqkv, "a_in_wgrad_qkv"),
                 norm_matmul_tn(sv["h_in"], gam, dz, "a_in_wgrad_z"),
                 norm_matmul_tn(sv["h_in"], gam, dba, "a_in_wgrad_ba")[:, :2 * ha]], axis=1)
            dh, dg = dx_norm_bwd(dh, sv["h_in"], gam, pieces, "a_in_dx")
            G["a_norm"][l] = dg[0]
        else:
            j = l - la
            d_o = matmul_nt(dh, big["b_w_out"], "b_out_dx")
            gbig["b_w_out"] = matmul_tn(sv["o"], dh, "b_out_wgrad")
            (dq, dk_acc, dv_acc, dbias), got = attn_bwd(
                sv["qp"], kv, sv["bias"], d_o, dk_acc, dv_acc,
                _named("attn_bwd" if dk_acc is None else "attn_bwd_acc", l, rider), rider)
            comm.bwd_got_b(l, got)
            G["b_rel_bias"][j] = bias_expand_bwd(_bias_from_pairs(dbias), clip, tp, "bias_expand_bwd")[:, :tbl]
            gam = _row(W["b_norm"][j])
            gbig["b_w_q"] = norm_matmul_tn(sv["h_in"], gam, dq, "b_q_wgrad")
            dh, dg = dx_norm_bwd(dh, sv["h_in"], gam, [(dq, big["b_w_q"])], "b_q_dx")
            G["b_norm"][j] = dg[0]
            if j == 0:
                gam = _row(W["kv_norm"])
                half = None if comm.col_blocks is None else comm.col_blocks // 2
                gbig["w_kv"] = jnp.concatenate([norm_matmul_tn(h_kv, gam, dk_acc, "kv_wgrad_k", half),
                                                norm_matmul_tn(h_kv, gam, dv_acc, "kv_wgrad_v", half)],
                                               axis=1 if half is None else 0)
                dh, dg = dx_norm_bwd(dh, h_kv, gam, [(dk_acc, w_kv[:, :bw]), (dv_acc, w_kv[:, bw:])], "kv_dx")
                G["kv_norm"] = dg[0]
        comm.grads_ready(l, gbig)
    out = {k_: (jnp.stack(v_) if isinstance(v_, list) else v_) for k_, v_ in G.items()}
    return loss[0, 0], dh, out, comm


WEIGHTS = ["a_norm", "a_w_in", "a_conv", "a_A_log", "a_dt_bias", "a_out_norm", "a_w_out", "kv_norm", "w_kv",
           "b_norm", "b_w_q", "b_rel_bias", "b_w_out", "f_norm", "f_w_up", "f_conv", "f_conv_b", "f_w_down",
           "final_norm"]
SHARD_AXIS = {"a_norm": 1, "a_w_in": 2, "a_conv": 2, "a_w_out": 1, "w_kv": 1, "b_w_q": 1, "b_w_out": 1,
              "f_w_up": 2, "f_conv": 2, "f_w_down": 1}
BIG = ["a_w_in", "a_w_out", "w_kv", "b_w_q", "b_w_out", "f_w_up", "f_w_down"]
SMALL_SHARDED = ["a_norm", "a_conv", "f_conv"]


def _unstack(g, axis):
    return jnp.concatenate([g[i] for i in range(N_DEV)], axis=axis)


def _to_blocks(full, axis):
    parts = jnp.stack(jnp.split(full, N_DEV, axis=axis))
    return parts.reshape(N_DEV, -1, parts.shape[-1])


def _pack(arrs):
    flat = []
    for a in arrs:
        f = a.reshape(-1)
        flat.append(jnp.pad(f, (0, (-f.shape[0]) % LANES)))
    f = jnp.concatenate(flat)
    f = jnp.pad(f, (0, (-f.shape[0]) % (8 * LANES)))
    return f.reshape(-1, LANES)


def _unpack(pack, shapes):
    flat = pack.reshape(-1)
    out, pos = [], 0
    for s in shapes:
        sz = math.prod(s)
        out.append(flat[pos:pos + sz].reshape(s))
        pos += sz + (-sz) % LANES
    return out


def _as2d(a):
    return a.reshape(1, -1) if a.ndim == 1 else a.reshape(-1, a.shape[-1])


def kernel(x, a_norm, a_w_in, a_conv, a_A_log, a_dt_bias, a_out_norm, a_w_out, kv_norm, w_kv, b_norm, b_w_q, b_rel_bias, b_w_out, f_norm, f_w_up, f_conv, f_conv_b, f_w_down, final_norm, loss_target, m_a_norm, m_a_w_in, m_a_conv, m_a_A_log, m_a_dt_bias, m_a_out_norm, m_a_w_out, m_kv_norm, m_w_kv, m_b_norm, m_b_w_q, m_b_rel_bias, m_b_w_out, m_f_norm, m_f_w_up, m_f_conv, m_f_conv_b, m_f_w_down, m_final_norm, v_a_norm, v_a_w_in, v_a_conv, v_a_A_log, v_a_dt_bias, v_a_out_norm, v_a_w_out, v_kv_norm, v_w_kv, v_b_norm, v_b_w_q, v_b_rel_bias, v_b_w_out, v_f_norm, v_f_w_up, v_f_conv, v_f_conv_b, v_f_w_down, v_final_norm):
    w = dict(a_norm=a_norm, a_w_in=a_w_in, a_conv=a_conv, a_A_log=a_A_log, a_dt_bias=a_dt_bias,
             a_out_norm=a_out_norm, a_w_out=a_w_out, kv_norm=kv_norm, w_kv=w_kv, b_norm=b_norm, b_w_q=b_w_q,
             b_rel_bias=b_rel_bias, b_w_out=b_w_out, f_norm=f_norm, f_w_up=f_w_up, f_conv=f_conv,
             f_conv_b=f_conv_b, f_w_down=f_w_down, final_norm=final_norm)
    mom = dict(a_norm=m_a_norm, a_w_in=m_a_w_in, a_conv=m_a_conv, a_A_log=m_a_A_log, a_dt_bias=m_a_dt_bias,
               a_out_norm=m_a_out_norm, a_w_out=m_a_w_out, kv_norm=m_kv_norm, w_kv=m_w_kv, b_norm=m_b_norm,
               b_w_q=m_b_w_q, b_rel_bias=m_b_rel_bias, b_w_out=m_b_w_out, f_norm=m_f_norm, f_w_up=m_f_w_up,
               f_conv=m_f_conv, f_conv_b=m_f_conv_b, f_w_down=m_f_w_down, final_norm=m_final_norm)
    var = dict(a_norm=v_a_norm, a_w_in=v_a_w_in, a_conv=v_a_conv, a_A_log=v_a_A_log, a_dt_bias=v_a_dt_bias,
               a_out_norm=v_a_out_norm, a_w_out=v_a_w_out, kv_norm=v_kv_norm, w_kv=v_w_kv, b_norm=v_b_norm,
               b_w_q=v_b_w_q, b_rel_bias=v_b_rel_bias, b_w_out=v_b_w_out, f_norm=v_f_norm, f_w_up=v_f_w_up,
               f_conv=v_f_conv, f_conv_b=v_f_conv_b, f_w_down=v_f_w_down, final_norm=v_final_norm)
    me = 4 * lax.axis_index("x") + 2 * lax.axis_index("y") + lax.axis_index("c")

    la, depth = a_A_log.shape[0], f_norm.shape[0]
    c_idx = lax.axis_index("c").astype(jnp.int32).reshape(1)
    shard_bf16 = {k: w[k].astype(BF16) for k in BIG}

    class Sharded(LocalWeights):
        col_blocks = N_DEV

        def __init__(self):
            super().__init__(w)
            self.full = {}
            self.stacks = {}
            self.pending = None
            self.parts = None

        def names(self, l):
            out = ["a_w_in", "a_w_out"] if l < la else ["b_w_q", "b_w_out"]
            return out + ["f_w_up", "f_w_down"] + (["w_kv"] if l == la else [])

        def index(self, k, l):
            return None if k == "w_kv" else (l - la if k.startswith("b_") else l)

        def shards(self, l):
            return [shard_bf16[k] if k == "w_kv" else shard_bf16[k][self.index(k, l)] for k in self.names(l)]

        def install(self, l, gathered):
            out = {}
            for k, g in zip(self.names(l), gathered):
                out[k] = _unstack(g, SHARD_AXIS[k] - (k != "w_kv"))
            if "a_w_in" in out:
                out["a_w_in"] = jnp.pad(out["a_w_in"], ((0, 0), (0, (-out["a_w_in"].shape[1]) % LANES)))
            self.full[l] = out

        def big(self, l, la_):
            return self.full[l]

        def fwd_rider(self, l):
            return ("all_gather", self.shards(l + 1)) if l + 1 < depth else None

        def fwd_got(self, l, got):
            if got is not None:
                self.install(l + 1, got)

        def grads_ready(self, l, grads):
            keys = self.names(l)
            self.pending = (l, keys, [grads[k] if grads[k].ndim == 3 else
                                      _to_blocks(grads[k], SHARD_AXIS[k] - (k != "w_kv")) for k in keys])

        def bwd_rider_a(self, l):
            return None if self.pending is None else ("rs_sibling", self.pending[2])

        def add_pairs(self, from_sibling):
            self.parts = [pair_add(g, r, c_idx, "grads_pair_add") for g, r in zip(self.pending[2], from_sibling)]

        def bwd_got_a(self, l, got):
            if got is not None:
                self.add_pairs(got)

        def bwd_rider_b(self, l):
            return None if self.parts is None else ("rs_chips", self.parts)

        def keep(self, stacks):
            l, keys, _ = self.pending
            for k, s in zip(keys, stacks):
                self.stacks[(k, l)] = s
            self.pending = self.parts = None

        def bwd_got_b(self, l, got):
            if got is not None:
                self.keep(got)

        def finish(self):
            self.add_pairs(exchange("rs_sibling", self.pending[2], "grads_to_sibling"))
            self.keep(exchange("rs_chips", self.parts, "grads_to_chips"))

    comm = Sharded()

    small_shapes = [w[k].shape for k in SMALL_SHARDED]
    gathered = exchange("all_gather", comm.shards(0) + [_pack([w[k] for k in SMALL_SHARDED])], "weights_all_gather")
    comm.install(0, gathered[:-1])
    full = dict(w)
    small = [_unpack(gathered[-1][i], small_shapes) for i in range(N_DEV)]
    for idx, k in enumerate(SMALL_SHARDED):
        full[k] = jnp.concatenate([small[i][idx] for i in range(N_DEV)], axis=SHARD_AXIS[k])

    loss_part, grad_x, G, _ = local_step(x[0], loss_target[0], full, comm)
    comm.finish()
    stacks = []
    for k in BIG:
        layers = sorted(l for (k_, l) in comm.stacks if k_ == k)
        stacks.append(jnp.concatenate([comm.stacks[(k, l)] for l in layers], axis=1))

    small_names = [k for k in WEIGHTS if k not in BIG]
    reduced = _unpack(all_reduce_small(_pack([G[k] for k in small_names] + [loss_part.reshape(1)]), "small_all_reduce"),
                      [G[k].shape for k in small_names] + [(1,)])
    loss = reduced[-1][0]
    small_g = dict(zip(small_names, reduced[:-1]))
    for k in SMALL_SHARDED:
        sz = w[k].shape[SHARD_AXIS[k]]
        small_g[k] = lax.dynamic_slice_in_dim(small_g[k], me * sz, sz, axis=SHARD_AXIS[k])

    res = {}
    for k, st in zip(BIG, stacks):
        outs = adamw(_as2d(w[k]), st, _as2d(mom[k]), _as2d(var[k]), "adamw_" + k)
        res[k] = [o.reshape(w[k].shape) for o in outs]
    for k in small_names:
        outs = adamw(_as2d(w[k]), _as2d(small_g[k])[None], _as2d(mom[k]), _as2d(var[k]), "adamw_" + k)
        res[k] = [o.reshape(w[k].shape) for o in outs]

    return (loss, grad_x[None], *[res[k][0] for k in WEIGHTS], *[res[k][1] for k in WEIGHTS],
            *[res[k][2] for k in WEIGHTS], *[res[k][3] for k in WEIGHTS])
```

```python
import functools
import math

import jax
import jax.numpy as jnp
from jax import lax
from jax.experimental import pallas as pl
from jax.experimental.pallas import tpu as pltpu

F32 = jnp.float32
BF16 = jnp.bfloat16
HI = lax.Precision.HIGHEST
MESH = pl.DeviceIdType.MESH

EPS = 1e-6
NEG_INF = -1e30
CHUNK = 64
LEFT_CHUNKS = 8
BAND = (LEFT_CHUNKS + 1) * CHUNK
BAND_PAD = 640
A_CONV = 4
F_CONV = 3
A_HEAD = 128
B_HEAD = 64
LANES = 128
HALO = 8
N_DEV = 8

ADAM_LR = 0.001
ADAM_B1 = 0.9
ADAM_B2 = 0.999
ADAM_EPS = 1e-08
ADAM_WD = 0.01
ADAM_STEP = 10

VMEM_LIMIT_V7X = 56 * 1024 * 1024
GDN_BWD_HEADS = 8
COL_CHUNK = 256
FFN_TILE = 256


def _cp(sem=None, vmem=VMEM_LIMIT_V7X):
    kw = dict(vmem_limit_bytes=vmem)
    if sem is not None:
        kw["dimension_semantics"] = sem
    return pltpu.CompilerParams(**kw)


def _pick(n, target, q=LANES):
    best = None
    for t in range(q, min(n, target) + 1, q):
        if n % t == 0:
            best = t
    return best if best is not None else n


def _sig(x):
    return 1.0 / (1.0 + jnp.exp(-x))


def _softplus(x):
    return jnp.maximum(x, 0.0) + jnp.log(1.0 + jnp.exp(-jnp.abs(x)))


def _rms(x, g):
    return x * lax.rsqrt(jnp.mean(x * x, axis=-1, keepdims=True) + EPS) * g


def _rms_bwd(x, g, dxn):
    r = lax.rsqrt(jnp.mean(x * x, axis=-1, keepdims=True) + EPS)
    gd = dxn * g
    dx = r * gd - x * (r * r * r) * jnp.mean(x * gd, axis=-1, keepdims=True)
    dg = jnp.sum(dxn * x * r, axis=0, keepdims=True)
    return dx, dg


def _dot(a, b):
    return jnp.dot(a, b, preferred_element_type=F32)


def _dot_nt(a, b):
    return lax.dot_general(a, b, (((1,), (1,)), ((), ())), preferred_element_type=F32)


def _dot_tn(a, b):
    return lax.dot_general(a, b, (((0,), (0,)), ((), ())), preferred_element_type=F32)


def _hdot(a, b):
    return jnp.dot(a, b, precision=HI, preferred_element_type=F32)


def _hdot_nt(a, b):
    return lax.dot_general(a, b, (((1,), (1,)), ((), ())), precision=HI, preferred_element_type=F32)


def _hdot_tn(a, b):
    return lax.dot_general(a, b, (((0,), (0,)), ((), ())), precision=HI, preferred_element_type=F32)


def _resident(shape, index_map):
    return pl.BlockSpec(shape, index_map, pipeline_mode=pl.Buffered(1))


def norm_matmul(h, gamma, w, name, out_dtype=F32):
    n, d = h.shape
    nc = w.shape[1]
    tm = _pick(n, 2048 if out_dtype == BF16 else 1024, 8)
    tn = _pick(nc, 1536)

    def body(h_ref, g_ref, w_ref, o_ref):
        xn = _rms(h_ref[...], g_ref[...])
        o_ref[...] = _dot(xn.astype(BF16), w_ref[...]).astype(out_dtype)

    return pl.pallas_call(
        body, grid=(nc // tn, n // tm),
        in_specs=[pl.BlockSpec((tm, d), lambda j, i: (i, 0)),
                  pl.BlockSpec((1, d), lambda j, i: (0, 0)),
                  pl.BlockSpec((d, tn), lambda j, i: (0, j))],
        out_specs=pl.BlockSpec((tm, tn), lambda j, i: (i, j)),
        out_shape=jax.ShapeDtypeStruct((n, nc), out_dtype), name=name,
        compiler_params=_cp(("parallel", "parallel")))(h, gamma, w)


def norm_matmul_tn(h, gamma, dy, name, col_blocks=None):
    n, d = h.shape
    nc = dy.shape[1]
    tm = _pick(n, 2048 if dy.dtype == BF16 else 1024, 8)
    tn = _pick(nc, 1536)
    grid = (nc // tn, n // tm)
    in_specs = [pl.BlockSpec((tm, d), lambda j, i: (i, 0)),
                pl.BlockSpec((1, d), lambda j, i: (0, 0)),
                pl.BlockSpec((tm, tn), lambda j, i: (i, j))]

    if col_blocks is None or tn % (nc // col_blocks):
        def body(h_ref, g_ref, dy_ref, o_ref):
            @pl.when(pl.program_id(1) == 0)
            def _():
                o_ref[...] = jnp.zeros_like(o_ref)
            xn = _rms(h_ref[...], g_ref[...])
            o_ref[...] += _dot_tn(xn.astype(BF16), dy_ref[...].astype(BF16))

        out = pl.pallas_call(
            body, grid=grid, in_specs=in_specs, out_specs=pl.BlockSpec((d, tn), lambda j, i: (0, j)),
            out_shape=jax.ShapeDtypeStruct((d, nc), F32), name=name,
            compiler_params=_cp(("parallel", "arbitrary")))(h, gamma, dy)
        return out if col_blocks is None else jnp.stack(jnp.split(out, col_blocks, axis=1))

    bw = nc // col_blocks
    per = tn // bw

    def body_blocks(h_ref, g_ref, dy_ref, o_ref, acc):
        i = pl.program_id(1)

        @pl.when(i == 0)
        def _():
            acc[...] = jnp.zeros_like(acc)
        xn = _rms(h_ref[...], g_ref[...])
        acc[...] += _dot_tn(xn.astype(BF16), dy_ref[...].astype(BF16))

        @pl.when(i == grid[1] - 1)
        def _():
            for b in range(per):
                o_ref[b] = acc[:, b * bw:(b + 1) * bw]

    return pl.pallas_call(
        body_blocks, grid=grid, in_specs=in_specs, out_specs=pl.BlockSpec((per, d, bw), lambda j, i: (j, 0, 0)),
        out_shape=jax.ShapeDtypeStruct((col_blocks, d, bw), F32), scratch_shapes=[pltpu.VMEM((d, tn), F32)],
        name=name, compiler_params=_cp(("parallel", "arbitrary")))(h, gamma, dy)


def matmul_tn(a, dy, name):
    n, ka = a.shape
    nc = dy.shape[1]
    tm = _pick(n, 2048 if a.dtype == BF16 else 1024, 8)
    tk = _pick(ka, 1536)
    tn = _pick(nc, 1024)

    def body(a_ref, dy_ref, o_ref):
        @pl.when(pl.program_id(2) == 0)
        def _():
            o_ref[...] = jnp.zeros_like(o_ref)
        o_ref[...] += _dot_tn(a_ref[...].astype(BF16), dy_ref[...].astype(BF16))

    return pl.pallas_call(
        body, grid=(ka // tk, nc // tn, n // tm),
        in_specs=[pl.BlockSpec((tm, tk), lambda k, j, i: (i, k)),
                  pl.BlockSpec((tm, tn), lambda k, j, i: (i, j))],
        out_specs=pl.BlockSpec((tk, tn), lambda k, j, i: (k, j)),
        out_shape=jax.ShapeDtypeStruct((ka, nc), F32), name=name,
        compiler_params=_cp(("parallel", "parallel", "arbitrary")))(a, dy)


def matmul_res(a, w, h, name):
    n, k = a.shape
    d = w.shape[1]
    tm = _pick(n, 512, 8)

    def body(a_ref, w_ref, h_ref, o_ref):
        o_ref[...] = h_ref[...] + _dot(a_ref[...].astype(BF16), w_ref[...])

    return pl.pallas_call(
        body, grid=(n // tm,),
        in_specs=[pl.BlockSpec((tm, k), lambda i: (i, 0)),
                  _resident((k, d), lambda i: (0, 0)),
                  pl.BlockSpec((tm, d), lambda i: (i, 0))],
        out_specs=pl.BlockSpec((tm, d), lambda i: (i, 0)),
        out_shape=jax.ShapeDtypeStruct((n, d), F32), name=name,
        compiler_params=_cp(("parallel",)))(a, w, h)


def matmul_nt(dy, w, name):
    n, k = dy.shape
    d = w.shape[0]
    tm = _pick(n, 512, 8)

    def body(dy_ref, w_ref, o_ref):
        o_ref[...] = _dot_nt(dy_ref[...].astype(BF16), w_ref[...])

    return pl.pallas_call(
        body, grid=(n // tm,),
        in_specs=[pl.BlockSpec((tm, k), lambda i: (i, 0)),
                  _resident((d, k), lambda i: (0, 0))],
        out_specs=pl.BlockSpec((tm, d), lambda i: (i, 0)),
        out_shape=jax.ShapeDtypeStruct((n, d), F32), name=name,
        compiler_params=_cp(("parallel",)))(dy, w)


def dx_norm_bwd(dout, h, gamma, pieces, name, rider=None):
    n, d = h.shape
    tm = _pick(n, 256, 8)
    np_ = len(pieces)

    def body(*refs):
        dout_ref, h_ref, g_ref = refs[:3]
        dys = refs[3:3 + np_]
        ws = refs[3 + np_:3 + 2 * np_]
        dh_ref, dg_ref = refs[3 + 2 * np_:]
        dxn = _dot_nt(dys[0][...].astype(BF16), ws[0][...])
        for p in range(1, np_):
            dxn = dxn + _dot_nt(dys[p][...].astype(BF16), ws[p][...])
        dx, dg = _rms_bwd(h_ref[...], g_ref[...], dxn)
        dh_ref[...] = dout_ref[...] + dx

        @pl.when(pl.program_id(0) == 0)
        def _():
            dg_ref[...] = jnp.zeros_like(dg_ref)
        dg_ref[...] += dg

    in_specs = [pl.BlockSpec((tm, d), lambda i: (i, 0)),
                pl.BlockSpec((tm, d), lambda i: (i, 0)),
                pl.BlockSpec((1, d), lambda i: (0, 0))]
    in_specs += [pl.BlockSpec((tm, dy.shape[1]), lambda i: (i, 0)) for dy, _ in pieces]
    in_specs += [_resident(w.shape, lambda i: (0, 0)) for _, w in pieces]
    (dh, dg), got = host_call(
        body, grid=(n // tm,), in_specs=in_specs,
        out_specs=[pl.BlockSpec((tm, d), lambda i: (i, 0)), pl.BlockSpec((1, d), lambda i: (0, 0))],
        out_shape=[jax.ShapeDtypeStruct((n, d), F32), jax.ShapeDtypeStruct((1, d), F32)], name=name,
        scratch_shapes=[], sem=("arbitrary",), rider=rider,
        args=(dout, h, gamma, *[p[0] for p in pieces], *[p[1] for p in pieces]))
    return (dh, dg) if rider is None else (dh, dg, got)


def loss_head(h, gamma, target, name="loss_head"):
    n, d = h.shape
    tm = _pick(n, 512, 8)

    def body(h_ref, g_ref, t_ref, loss_ref, dh_ref, dg_ref):
        @pl.when(pl.program_id(0) == 0)
        def _():
            loss_ref[...] = jnp.zeros_like(loss_ref)
            dg_ref[...] = jnp.zeros_like(dg_ref)
        x = h_ref[...]
        g = g_ref[...]
        e = _rms(x, g) - t_ref[...]
        part = jnp.sum(jnp.sum(e * e, axis=-1, keepdims=True), axis=0, keepdims=True) * (0.5 / d)
        loss_ref[...] += jnp.broadcast_to(part, loss_ref.shape)
        dx, dg = _rms_bwd(x, g, e * (1.0 / d))
        dh_ref[...] = dx
        dg_ref[...] += dg

    return pl.pallas_call(
        body, grid=(n // tm,),
        in_specs=[pl.BlockSpec((tm, d), lambda i: (i, 0)), pl.BlockSpec((1, d), lambda i: (0, 0)),
                  pl.BlockSpec((tm, d), lambda i: (i, 0))],
        out_specs=[pl.BlockSpec((8, LANES), lambda i: (0, 0)), pl.BlockSpec((tm, d), lambda i: (i, 0)),
                   pl.BlockSpec((1, d), lambda i: (0, 0))],
        out_shape=[jax.ShapeDtypeStruct((8, LANES), F32), jax.ShapeDtypeStruct((n, d), F32),
                   jax.ShapeDtypeStruct((1, d), F32)], name=name,
        compiler_params=_cp(("arbitrary",)))(h, gamma, target)


def _halo_rows(dtype):
    return HALO * (4 // jnp.dtype(dtype).itemsize)


def _prev_halo_map(t, hb=HALO):
    return lambda i: (jnp.maximum(i * (t // hb) - 1, 0), 0)


def _next_halo_map(t, n, hb=HALO):
    return lambda i: (jnp.minimum((i + 1) * (t // hb), n // hb - 1), 0)


def _fill_prev(xs, main_ref, halo_ref, i, cols=slice(None)):
    hb = halo_ref.shape[0]
    xs[0:HALO, :] = jnp.where(i > 0, halo_ref[hb - HALO:hb, cols].astype(F32), 0.0)
    xs[HALO:, :] = main_ref[:, cols].astype(F32)


def _causal_conv(xs, w_ref, width, t, cols=slice(None)):
    x = xs[...]
    acc = w_ref[width - 1:width, cols] * x[HALO:, :]
    for k in range(width - 1):
        acc = acc + w_ref[k:k + 1, cols] * pltpu.roll(x, width - 1 - k, axis=0)[HALO:, :]
    return acc


def _col_chunks(width, target=COL_CHUNK):
    tc = _pick(width, target)
    return [slice(j * tc, (j + 1) * tc) for j in range(width // tc)]


def ffn_act_down(up, conv_w, conv_b, w_down, h, name):
    n, c2 = up.shape
    ff = c2 // 2
    d = h.shape[1]
    t = _pick(n, 2 * FFN_TILE, 8)
    hb = _halo_rows(up.dtype)
    chunks = _col_chunks(ff)
    tc = chunks[0].stop

    def body(up_ref, halo_ref, cw_ref, cb_ref, wd_ref, h_ref, o_ref, act_ref, hc_ref, xg, xv):
        i = pl.program_id(0)
        acc = h_ref[...]
        for cs in chunks:
            vs = slice(ff + cs.start, ff + cs.stop)
            _fill_prev(xg, up_ref, halo_ref, i, cs)
            _fill_prev(xv, up_ref, halo_ref, i, vs)
            gate = _causal_conv(xg, cw_ref, F_CONV, t, cs) + cb_ref[:, cs]
            val = _causal_conv(xv, cw_ref, F_CONV, t, vs) + cb_ref[:, vs]
            hc_ref[:, cs] = gate.astype(BF16)
            hc_ref[:, vs] = val.astype(BF16)
            act = (gate * _sig(gate) * val).astype(BF16)
            act_ref[:, cs] = act
            acc = acc + _dot(act, wd_ref[cs, :])
        o_ref[...] = acc

    return pl.pallas_call(
        body, grid=(n // t,),
        in_specs=[pl.BlockSpec((t, c2), lambda i: (i, 0)),
                  pl.BlockSpec((hb, c2), _prev_halo_map(t, hb)),
                  pl.BlockSpec((F_CONV, c2), lambda i: (0, 0)),
                  pl.BlockSpec((1, c2), lambda i: (0, 0)),
                  _resident((ff, d), lambda i: (0, 0)),
                  pl.BlockSpec((t, d), lambda i: (i, 0))],
        out_specs=[pl.BlockSpec((t, d), lambda i: (i, 0)), pl.BlockSpec((t, ff), lambda i: (i, 0)),
                   pl.BlockSpec((t, c2), lambda i: (i, 0))],
        out_shape=[jax.ShapeDtypeStruct((n, d), F32), jax.ShapeDtypeStruct((n, ff), BF16),
                   jax.ShapeDtypeStruct((n, c2), BF16)],
        scratch_shapes=[pltpu.VMEM((t + HALO, tc), F32), pltpu.VMEM((t + HALO, tc), F32)], name=name,
        compiler_params=_cp(("parallel",)))(up, up, conv_w, conv_b, w_down, h)


def ffn_bwd_act(dout, hc, w_down, name):
    n, c2 = hc.shape
    ff = c2 // 2
    d = dout.shape[1]
    t = _pick(n, 2 * FFN_TILE, 8)
    chunks = _col_chunks(ff)

    def body(dout_ref, hc_ref, wd_ref, dhc_ref, dcb_ref):
        i = pl.program_id(0)

        @pl.when(i == 0)
        def _():
            dcb_ref[...] = jnp.zeros_like(dcb_ref)
        doutb = dout_ref[...].astype(BF16)
        for cs in chunks:
            vs = slice(ff + cs.start, ff + cs.stop)
            gate = hc_ref[:, cs].astype(F32)
            val = hc_ref[:, vs].astype(F32)
            sg = _sig(gate)
            da = _dot_nt(doutb, wd_ref[cs, :])
            dgate = da * val * (sg * (1.0 + gate * (1.0 - sg)))
            dval = da * gate * sg
            dhc_ref[:, cs] = dgate.astype(BF16)
            dhc_ref[:, vs] = dval.astype(BF16)
            dcb_ref[:, cs] += jnp.sum(dgate, axis=0, keepdims=True)
            dcb_ref[:, vs] += jnp.sum(dval, axis=0, keepdims=True)

    return pl.pallas_call(
        body, grid=(n // t,),
        in_specs=[pl.BlockSpec((t, d), lambda i: (i, 0)),
                  pl.BlockSpec((t, c2), lambda i: (i, 0)),
                  _resident((ff, d), lambda i: (0, 0))],
        out_specs=[pl.BlockSpec((t, c2), lambda i: (i, 0)), pl.BlockSpec((1, c2), lambda i: (0, 0))],
        out_shape=[jax.ShapeDtypeStruct((n, c2), BF16), jax.ShapeDtypeStruct((1, c2), F32)], name=name,
        compiler_params=_cp(("arbitrary",)))(dout, hc, w_down)


def conv_bwd_tail(dy_ref, dnext_ref, x_ref, cw_ref, dcw_ref, ds, width, t, i, last, cols=slice(None)):
    ds[0:t, :] = dy_ref[:, cols].astype(F32)
    ds[t:, :] = jnp.where(i < last, dnext_ref[0:HALO, cols].astype(F32), 0.0)
    x = x_ref[:, cols].astype(F32)
    dall = ds[...]
    dx = None
    for k in range(width):
        off = width - 1 - k
        shifted = dall[0:t, :] if off == 0 else pltpu.roll(dall, t + HALO - off, axis=0)[0:t, :]
        term = cw_ref[k:k + 1, cols] * shifted
        dx = term if dx is None else dx + term
        dcw_ref[k:k + 1, cols] += jnp.sum(shifted * x, axis=0, keepdims=True)
    return dx


def ffn_bwd_up(dhc, up, conv_w, w_up, h, dout, gamma, name, rider=None):
    n, c2 = up.shape
    d = h.shape[1]
    t = _pick(n, FFN_TILE, 8)
    last = n // t - 1
    chunks = _col_chunks(c2)
    tc = chunks[0].stop

    def body(dhc_ref, dnext_ref, up_ref, cw_ref, wu_ref, h_ref, dout_ref, g_ref,
             dh_ref, dup_ref, dcw_ref, dg_ref, ds):
        i = pl.program_id(0)

        @pl.when(i == 0)
        def _():
            dcw_ref[...] = jnp.zeros_like(dcw_ref)
            dg_ref[...] = jnp.zeros_like(dg_ref)
        dxn = jnp.zeros((t, d), F32)
        for cs in chunks:
            dup = conv_bwd_tail(dhc_ref, dnext_ref, up_ref, cw_ref, dcw_ref, ds, F_CONV, t, i, last, cs)
            dupb = dup.astype(BF16)
            dup_ref[:, cs] = dupb
            dxn = dxn + _dot_nt(dupb, wu_ref[:, cs])
        dx, dg = _rms_bwd(h_ref[...], g_ref[...], dxn)
        dh_ref[...] = dout_ref[...] + dx
        dg_ref[...] += dg

    return host_call(
        body, grid=(n // t,), rider=rider, sem=("arbitrary",), args=(dhc, dhc, up, conv_w, w_up, h, dout, gamma),
        in_specs=[pl.BlockSpec((t, c2), lambda i: (i, 0)),
                  pl.BlockSpec((_halo_rows(dhc.dtype), c2), _next_halo_map(t, n, _halo_rows(dhc.dtype))),
                  pl.BlockSpec((t, c2), lambda i: (i, 0)),
                  pl.BlockSpec((F_CONV, c2), lambda i: (0, 0)),
                  _resident((d, c2), lambda i: (0, 0)),
                  pl.BlockSpec((t, d), lambda i: (i, 0)),
                  pl.BlockSpec((t, d), lambda i: (i, 0)),
                  pl.BlockSpec((1, d), lambda i: (0, 0))],
        out_specs=[pl.BlockSpec((t, d), lambda i: (i, 0)), pl.BlockSpec((t, c2), lambda i: (i, 0)),
                   pl.BlockSpec((F_CONV, c2), lambda i: (0, 0)), pl.BlockSpec((1, d), lambda i: (0, 0))],
        out_shape=[jax.ShapeDtypeStruct((n, d), F32), jax.ShapeDtypeStruct((n, c2), BF16),
                   jax.ShapeDtypeStruct((F_CONV, c2), F32), jax.ShapeDtypeStruct((1, d), F32)],
        scratch_shapes=[pltpu.VMEM((t + HALO, tc), F32)], name=name)


def _gdn_head(uq, uk, uv, pba, alog, dtb, head, n_heads):
    lane = lax.broadcasted_iota(jnp.int32, pba.shape, 1)
    sq = uq * _sig(uq)
    q = sq * lax.rsqrt(jnp.sum(sq * sq, axis=-1, keepdims=True) + EPS) * (A_HEAD ** -0.5)
    sk = uk * _sig(uk)
    k = sk * lax.rsqrt(jnp.sum(sk * sk, axis=-1, keepdims=True) + EPS)
    v = uv * _sig(uv)
    beta = jnp.sum(jnp.where(lane == head, _sig(pba), 0.0), axis=-1, keepdims=True)
    g_all = -jnp.exp(alog) * _softplus(pba + dtb)
    g = jnp.sum(jnp.where(lane == n_heads + head, g_all, 0.0), axis=-1, keepdims=True)
    return q, k, v, jnp.broadcast_to(beta, uq.shape), jnp.broadcast_to(g, uq.shape)


def gdn_prep(proj, conv_w, alog, dtb, n_heads, name, rider=None):
    n = proj.shape[0]
    qk = n_heads * A_HEAD
    cw = 3 * qk
    ba_blk = (cw + qk) // LANES
    t = _pick(n, 256, 8)

    def body(x_ref, halo_ref, pba_ref, cw_ref, al_ref, dt_ref, q_ref, k_ref, v_ref, b_ref, g_ref, xs):
        i = pl.program_id(0)
        xs[0:HALO, :] = jnp.where(i > 0, halo_ref[...], 0.0)
        xs[HALO:, :] = x_ref[...]
        u = _causal_conv(xs, cw_ref, A_CONV, t)
        pba = pba_ref[...]
        for hd in range(n_heads):
            s0 = slice(hd * A_HEAD, (hd + 1) * A_HEAD)
            s1 = slice(qk + hd * A_HEAD, qk + (hd + 1) * A_HEAD)
            s2 = slice(2 * qk + hd * A_HEAD, 2 * qk + (hd + 1) * A_HEAD)
            q, k, v, bb, gb = _gdn_head(u[:, s0], u[:, s1], u[:, s2], pba, al_ref[...], dt_ref[...], hd, n_heads)
            q_ref[:, s0] = q
            k_ref[:, s0] = k
            v_ref[:, s0] = v
            b_ref[:, s0] = bb
            g_ref[:, s0] = gb

    out = jax.ShapeDtypeStruct((n, qk), F32)
    return host_call(
        body, grid=(n // t,),
        in_specs=[pl.BlockSpec((t, cw), lambda i: (i, 0)),
                  pl.BlockSpec((HALO, cw), _prev_halo_map(t)),
                  pl.BlockSpec((t, LANES), lambda i: (i, ba_blk)),
                  pl.BlockSpec((A_CONV, cw), lambda i: (0, 0)),
                  pl.BlockSpec((1, LANES), lambda i: (0, 0)),
                  pl.BlockSpec((1, LANES), lambda i: (0, 0))],
        out_specs=[pl.BlockSpec((t, qk), lambda i: (i, 0))] * 5,
        out_shape=[out] * 5,
        scratch_shapes=[pltpu.VMEM((t + HALO, cw), F32)], name=name,
        sem=("parallel",), args=(proj, proj, proj, conv_w, alog, dtb), rider=rider)


def gdn_prep_bwd(proj, conv_w, alog, dtb, dq, dk, dv, dbb, dgb, n_heads, name):
    n = proj.shape[0]
    qk = n_heads * A_HEAD
    cw = 3 * qk
    ba_blk = (cw + qk) // LANES
    t = _pick(n, 256, 8)

    def body(x_ref, halo_ref, pba_ref, cw_ref, al_ref, dt_ref, dq_ref, dk_ref, dv_ref, dbb_ref, dgb_ref,
             du_ref, dba_ref, dal_ref, ddt_ref, xs):
        i = pl.program_id(0)
        xs[0:HALO, :] = jnp.where(i > 0, halo_ref[...], 0.0)
        xs[HALO:, :] = x_ref[...]
        u = _causal_conv(xs, cw_ref, A_CONV, t)
        pba = pba_ref[...]
        lane0 = lax.broadcasted_iota(jnp.int32, (t, A_HEAD), 1) == 0
        dba = jnp.zeros((t, LANES), F32)
        dal = jnp.zeros((1, LANES), F32)
        ddt = jnp.zeros((1, LANES), F32)
        for hd in range(n_heads):
            s0 = slice(hd * A_HEAD, (hd + 1) * A_HEAD)
            s1 = slice(qk + hd * A_HEAD, qk + (hd + 1) * A_HEAD)
            s2 = slice(2 * qk + hd * A_HEAD, 2 * qk + (hd + 1) * A_HEAD)
            fn = functools.partial(_gdn_head, head=hd, n_heads=n_heads)
            _, vjp = jax.vjp(fn, u[:, s0], u[:, s1], u[:, s2], pba, al_ref[...], dt_ref[...])
            cts = (dq_ref[:, s0], dk_ref[:, s0], dv_ref[:, s0],
                   jnp.where(lane0, dbb_ref[:, s0], 0.0), jnp.where(lane0, dgb_ref[:, s0], 0.0))
            duq, duk, duv, dpba, da, dd = vjp(cts)
            du_ref[:, s0] = duq
            du_ref[:, s1] = duk
            du_ref[:, s2] = duv
            dba = dba + dpba
            dal = dal + da
            ddt = ddt + dd
        dba_ref[...] = dba

        @pl.when(i == 0)
        def _():
            dal_ref[...] = jnp.zeros_like(dal_ref)
            ddt_ref[...] = jnp.zeros_like(ddt_ref)
        dal_ref[...] += dal
        ddt_ref[...] += ddt

    tok = pl.BlockSpec((t, qk), lambda i: (i, 0))
    row = pl.BlockSpec((1, LANES), lambda i: (0, 0))
    return pl.pallas_call(
        body, grid=(n // t,),
        in_specs=[pl.BlockSpec((t, cw), lambda i: (i, 0)),
                  pl.BlockSpec((HALO, cw), _prev_halo_map(t)),
                  pl.BlockSpec((t, LANES), lambda i: (i, ba_blk)),
                  pl.BlockSpec((A_CONV, cw), lambda i: (0, 0)), row, row,
                  tok, tok, tok, tok, tok],
        out_specs=[pl.BlockSpec((t, cw), lambda i: (i, 0)), pl.BlockSpec((t, LANES), lambda i: (i, 0)), row, row],
        out_shape=[jax.ShapeDtypeStruct((n, cw), F32), jax.ShapeDtypeStruct((n, LANES), F32),
                   jax.ShapeDtypeStruct((1, LANES), F32), jax.ShapeDtypeStruct((1, LANES), F32)],
        scratch_shapes=[pltpu.VMEM((t + HALO, cw), F32)], name=name,
        compiler_params=_cp(("arbitrary",)))(proj, proj, proj, conv_w, alog, dtb, dq, dk, dv, dbb, dgb)


def conv_bwd(du, x, conv_w, width, name):
    n, cw = du.shape
    t = _pick(n, 256, 8)
    last = n // t - 1

    chunks = _col_chunks(cw)
    tc = chunks[0].stop

    def body(du_ref, dnext_ref, x_ref, cw_ref, dx_ref, dcw_ref, ds):
        i = pl.program_id(0)

        @pl.when(i == 0)
        def _():
            dcw_ref[...] = jnp.zeros_like(dcw_ref)
        for cs in chunks:
            dx_ref[:, cs] = conv_bwd_tail(du_ref, dnext_ref, x_ref, cw_ref, dcw_ref, ds, width, t, i, last, cs)

    return pl.pallas_call(
        body, grid=(n // t,),
        in_specs=[pl.BlockSpec((t, cw), lambda i: (i, 0)),
                  pl.BlockSpec((HALO, cw), _next_halo_map(t, n)),
                  pl.BlockSpec((t, cw), lambda i: (i, 0)),
                  pl.BlockSpec((width, cw), lambda i: (0, 0))],
        out_specs=[pl.BlockSpec((t, cw), lambda i: (i, 0)), pl.BlockSpec((width, cw), lambda i: (0, 0))],
        out_shape=[jax.ShapeDtypeStruct((n, cw), F32), jax.ShapeDtypeStruct((width, cw), F32)],
        scratch_shapes=[pltpu.VMEM((t + HALO, tc), F32)], name=name,
        compiler_params=_cp(("arbitrary",)))(du, du, x, conv_w)


def _b(x):
    return x.astype(BF16)


def _mm_nn(a, b):
    return _dot(_b(a), _b(b))


def _mm_nt(a, b):
    return _dot_nt(_b(a), _b(b))


def _mm_tn(a, b):
    return _dot_tn(_b(a), _b(b))


@jax.custom_vjp
def _mmg_nn(a, b):
    return _mm_nn(a, b)


_mmg_nn.defvjp(lambda a, b: (_mm_nn(a, b), (a, b)),
               lambda res, dc: (_mm_nt(dc, res[1]), _mm_tn(res[0], dc)))


@jax.custom_vjp
def _mmg_nt(a, b):
    return _mm_nt(a, b)


_mmg_nt.defvjp(lambda a, b: (_mm_nt(a, b), (a, b)),
               lambda res, dc: (_mm_nn(dc, res[1]), _mm_tn(dc, res[0])))


@jax.custom_vjp
def _mmg_tn(a, b):
    return _mm_tn(a, b)


_mmg_tn.defvjp(lambda a, b: (_mm_tn(a, b), (a, b)),
               lambda res, dc: (_mm_nt(res[1], dc), _mm_nn(res[0], dc)))


def _each(f, *lists):
    return [f(*a) for a in zip(*lists)]


def _unit_lower_inv(ms):
    c = ms[0].shape[0]
    eye = jnp.where(lax.broadcasted_iota(jnp.int32, (c, c), 0) == lax.broadcasted_iota(jnp.int32, (c, c), 1), 1.0, 0.0)
    xs = [eye - m for m in ms]
    pws = _each(_mm_nn, ms, ms)
    for it in range(5):
        xs = _each(lambda x, pw: x + _mm_nn(x, pw), xs, pws)
        if it < 4:
            pws = _each(_mm_nn, pws, pws)
    rs = _each(lambda m, x: eye - x - _hdot(m, x), ms, xs)
    return _each(lambda x, r: x + _mm_nn(x, r), xs, rs)


@jax.custom_vjp
def _saved_inv_g(ms, xs):
    return xs


_saved_inv_g.defvjp(lambda ms, xs: (xs, xs),
                    lambda xs, dxs: (_each(lambda t, x: -_mm_nt(t, x), _each(_mm_tn, xs, dxs), xs),
                                     [jnp.zeros_like(x) for x in xs]))


def _gdn_chunk(ops, state, q, k, v, bb, gb):
    nn, nt, tn, inv = ops
    c = CHUNK
    ri = lax.broadcasted_iota(jnp.int32, (c, c), 0)
    ci = lax.broadcasted_iota(jnp.int32, (c, c), 1)
    causal = ri >= ci
    strict = ri > ci
    tri = jnp.where(causal, 1.0, 0.0)
    gc = [_hdot(tri, g) for g in gb]
    decay = [jnp.where(causal, jnp.exp(jnp.where(causal, x[:, :c] - x.T[:c, :], 0.0)), 0.0) for x in gc]
    kb = _each(lambda a, b: a * b, k, bb)
    kk = _each(nt, kb, k)
    m = _each(lambda a, d: jnp.where(strict, a * d, 0.0), kk, decay)
    tinv = inv(m)
    egc = [jnp.exp(x) for x in gc]
    u = _each(nn, tinv, _each(lambda a, b: a * b, v, bb))
    w = _each(nn, tinv, _each(lambda a, b: a * b, kb, egc))
    attn = _each(lambda a, d: a * d, _each(nt, q, k), decay)
    glast = [jnp.sum(g, axis=0, keepdims=True) for g in gb]
    ws = _each(nn, w, state)
    v_new = _each(lambda a, b: a - b, u, ws)
    qs = _each(nn, _each(lambda a, b: a * b, q, egc), state)
    av = _each(nn, attn, v_new)
    o = _each(lambda a, b: a + b, qs, av)
    kv = _each(tn, _each(lambda a, gl, x: a * jnp.exp(gl - x), k, glast, gc), v_new)
    new_state = _each(lambda s, gl, a: s * jnp.exp(gl) + a, state, glast, kv)
    return o, new_state


def gdn_fwd(q, k, v, bb, gb, n_heads, name, rider=None):
    n, w = q.shape
    nc = n // CHUNK
    cb = min(8, nc)
    rows = cb * CHUNK

    def body(q_ref, k_ref, v_ref, b_ref, g_ref, o_ref, st_ref, ti_ref, s_scr):
        @pl.when(pl.program_id(0) == 0)
        def _():
            s_scr[...] = jnp.zeros_like(s_scr)

        def step(c, carry):
            sl = pl.ds(pl.multiple_of(c * CHUNK, CHUNK), CHUNK)
            lanes = [slice(hd * A_HEAD, (hd + 1) * A_HEAD) for hd in range(n_heads)]
            state = [s_scr[hd] for hd in range(n_heads)]
            inverses = []

            def inv(ms):
                inverses.extend(_unit_lower_inv(ms))
                return inverses

            o, new_state = _gdn_chunk((_mm_nn, _mm_nt, _mm_tn, inv), state,
                                      *[[r[sl, ls] for ls in lanes] for r in (q_ref, k_ref, v_ref, b_ref, g_ref)])
            for hd, ls in enumerate(lanes):
                st_ref[hd, pl.ds(c, 1)] = state[hd][None]
                ti_ref[hd, pl.ds(c, 1)] = inverses[hd].astype(BF16)[None]
                o_ref[sl, ls] = o[hd]
                s_scr[hd] = new_state[hd]
            return carry

        lax.fori_loop(0, cb, step, 0)

    tok = pl.BlockSpec((rows, w), lambda j: (j, 0))
    return host_call(
        body, grid=(nc // cb,),
        in_specs=[tok] * 5,
        out_specs=[tok, pl.BlockSpec((n_heads, cb, A_HEAD, A_HEAD), lambda j: (0, j, 0, 0)),
                   pl.BlockSpec((n_heads, cb, CHUNK, CHUNK), lambda j: (0, j, 0, 0))],
        out_shape=[jax.ShapeDtypeStruct(q.shape, F32), jax.ShapeDtypeStruct((n_heads, nc, A_HEAD, A_HEAD), F32),
                   jax.ShapeDtypeStruct((n_heads, nc, CHUNK, CHUNK), BF16)],
        scratch_shapes=[pltpu.VMEM((n_heads, A_HEAD, A_HEAD), F32)], name=name,
        sem=("arbitrary",), args=(q, k, v, bb, gb), rider=rider)


def gdn_bwd(q, k, v, bb, gb, states, tinv, do, n_heads, name, rider=None):
    n, w = q.shape
    nc = n // CHUNK
    cb = min(4, nc)
    rows = cb * CHUNK
    nblk = nc // cb

    def body(q_ref, k_ref, v_ref, b_ref, g_ref, st_ref, ti_ref, do_ref,
             dq_ref, dk_ref, dv_ref, db_ref, dg_ref, ds_scr):
        @pl.when(pl.program_id(0) == 0)
        def _():
            ds_scr[...] = jnp.zeros_like(ds_scr)

        def step(s, carry):
            c = cb - 1 - s
            sl = pl.ds(pl.multiple_of(c * CHUNK, CHUNK), CHUNK)
            for h0 in range(0, n_heads, GDN_BWD_HEADS):
                heads = list(range(h0, min(h0 + GDN_BWD_HEADS, n_heads)))
                lanes = [slice(hd * A_HEAD, (hd + 1) * A_HEAD) for hd in heads]
                state = [st_ref[hd, pl.ds(c, 1)][0] for hd in heads]
                saved = [ti_ref[hd, pl.ds(c, 1)][0].astype(F32) for hd in heads]
                chunk_fn = functools.partial(_gdn_chunk, (_mmg_nn, _mmg_nt, _mmg_tn, lambda ms: _saved_inv_g(ms, saved)))
                _, vjp = jax.vjp(chunk_fn, state, *[[r[sl, ls] for ls in lanes]
                                                    for r in (q_ref, k_ref, v_ref, b_ref, g_ref)])
                dstate, dq, dk, dv, dbb, dgb = vjp(([do_ref[sl, ls] for ls in lanes], [ds_scr[hd] for hd in heads]))
                for u, (hd, ls) in enumerate(zip(heads, lanes)):
                    ds_scr[hd] = dstate[u]
                    dq_ref[sl, ls] = dq[u]
                    dk_ref[sl, ls] = dk[u]
                    dv_ref[sl, ls] = dv[u]
                    db_ref[sl, ls] = jnp.broadcast_to(jnp.sum(dbb[u], axis=-1, keepdims=True), dbb[u].shape)
                    dg_ref[sl, ls] = jnp.broadcast_to(jnp.sum(dgb[u], axis=-1, keepdims=True), dgb[u].shape)
            return carry

        lax.fori_loop(0, cb, step, 0)

    tok = pl.BlockSpec((rows, w), lambda j: (nblk - 1 - j, 0))
    out = jax.ShapeDtypeStruct(q.shape, F32)
    return host_call(
        body, grid=(nblk,),
        in_specs=[tok] * 5 + [pl.BlockSpec((n_heads, cb, A_HEAD, A_HEAD), lambda j: (0, nblk - 1 - j, 0, 0)),
                              pl.BlockSpec((n_heads, cb, CHUNK, CHUNK), lambda j: (0, nblk - 1 - j, 0, 0)), tok],
        out_specs=[tok] * 5, out_shape=[out] * 5,
        scratch_shapes=[pltpu.VMEM((n_heads, A_HEAD, A_HEAD), F32)], name=name,
        sem=("arbitrary",), args=(q, k, v, bb, gb, states, tinv, do), rider=rider)


def _gdn_gate(oh, zh, w):
    r = lax.rsqrt(jnp.mean(oh * oh, axis=-1, keepdims=True) + EPS)
    return oh * r * w * (zh * _sig(zh))


def gdn_out(o, proj, out_norm, w_out, h, n_heads, name):
    n, vw = o.shape
    d = h.shape[1]
    z_blk = 3 * vw // vw
    t = _pick(n, 512, 8)

    def body(o_ref, z_ref, w_ref, wo_ref, h_ref, out_ref, y_ref):
        for hd in range(n_heads):
            s0 = slice(hd * A_HEAD, (hd + 1) * A_HEAD)
            y_ref[:, s0] = _gdn_gate(o_ref[:, s0], z_ref[:, s0], w_ref[...]).astype(BF16)
        out_ref[...] = h_ref[...] + _dot(y_ref[...], wo_ref[...])

    return pl.pallas_call(
        body, grid=(n // t,),
        in_specs=[pl.BlockSpec((t, vw), lambda i: (i, 0)),
                  pl.BlockSpec((t, vw), lambda i: (i, z_blk)),
                  pl.BlockSpec((1, A_HEAD), lambda i: (0, 0)),
                  _resident((vw, d), lambda i: (0, 0)),
                  pl.BlockSpec((t, d), lambda i: (i, 0))],
        out_specs=[pl.BlockSpec((t, d), lambda i: (i, 0)), pl.BlockSpec((t, vw), lambda i: (i, 0))],
        out_shape=[jax.ShapeDtypeStruct((n, d), F32), jax.ShapeDtypeStruct((n, vw), BF16)], name=name,
        compiler_params=_cp(("parallel",)))(o, proj, out_norm, w_out, h)


def gdn_out_bwd(dout, o, proj, out_norm, w_out, n_heads, name):
    n, vw = o.shape
    d = dout.shape[1]
    z_blk = 3
    t = _pick(n, 512, 8)

    def body(dout_ref, o_ref, z_ref, w_ref, wo_ref, do_ref, dz_ref, dw_ref):
        dy = _dot_nt(dout_ref[...].astype(BF16), wo_ref[...])
        dw = jnp.zeros((1, A_HEAD), F32)
        for hd in range(n_heads):
            s0 = slice(hd * A_HEAD, (hd + 1) * A_HEAD)
            _, vjp = jax.vjp(_gdn_gate, o_ref[:, s0], z_ref[:, s0], w_ref[...])
            doh, dzh, dwh = vjp(dy[:, s0])
            do_ref[:, s0] = doh
            dz_ref[:, s0] = dzh
            dw = dw + dwh

        @pl.when(pl.program_id(0) == 0)
        def _():
            dw_ref[...] = jnp.zeros_like(dw_ref)
        dw_ref[...] += dw

    tok = pl.BlockSpec((t, vw), lambda i: (i, 0))
    return pl.pallas_call(
        body, grid=(n // t,),
        in_specs=[pl.BlockSpec((t, d), lambda i: (i, 0)), tok,
                  pl.BlockSpec((t, vw), lambda i: (i, z_blk)),
                  pl.BlockSpec((1, A_HEAD), lambda i: (0, 0)),
                  _resident((vw, d), lambda i: (0, 0))],
        out_specs=[tok, tok, pl.BlockSpec((1, A_HEAD), lambda i: (0, 0))],
        out_shape=[jax.ShapeDtypeStruct((n, vw), F32), jax.ShapeDtypeStruct((n, vw), F32),
                   jax.ShapeDtypeStruct((1, A_HEAD), F32)], name=name,
        compiler_params=_cp(("arbitrary",)))(dout, o, proj, out_norm, w_out)


BIAS_LINE = 768
BIAS_TOP = BAND + CHUNK - 2


def _bias_line_onehot(clip, tbl_pad):
    r = lax.broadcasted_iota(jnp.int32, (tbl_pad, BIAS_LINE), 0)
    v = lax.broadcasted_iota(jnp.int32, (tbl_pad, BIAS_LINE), 1)
    idx = jnp.clip(BIAS_TOP - v - (CHUNK - 1), -clip, clip) + clip
    return jnp.where((r == idx) & (v <= BIAS_TOP), 1.0, 0.0)


def bias_expand(tbl, clip, name):
    nh, tp = tbl.shape

    def body(t_ref, o_ref):
        line = _hdot(t_ref[...], _bias_line_onehot(clip, tp))
        keep = lax.broadcasted_iota(jnp.int32, (nh, BAND_PAD), 1) < BAND
        for i in range(CHUNK):
            s = CHUNK - 1 - i
            rolled = line if s == 0 else pltpu.roll(line, BIAS_LINE - s, axis=1)
            o_ref[i] = jnp.where(keep, rolled[:, :BAND_PAD], 0.0)

    return pl.pallas_call(
        body, in_specs=[pl.BlockSpec(memory_space=pltpu.VMEM)], out_specs=pl.BlockSpec(memory_space=pltpu.VMEM),
        out_shape=jax.ShapeDtypeStruct((CHUNK, nh, BAND_PAD), F32), name=name, compiler_params=_cp())(tbl)


def bias_expand_bwd(dbias, clip, tp, name):
    _, nh, _ = dbias.shape

    def body(d_ref, o_ref):
        keep = lax.broadcasted_iota(jnp.int32, (nh, BAND_PAD), 1) < BAND
        pad = jnp.zeros((nh, BIAS_LINE - BAND_PAD), F32)
        acc = jnp.zeros((nh, BIAS_LINE), F32)
        for i in range(CHUNK):
            s = CHUNK - 1 - i
            d = jnp.concatenate([jnp.where(keep, d_ref[i], 0.0), pad], axis=1)
            acc = acc + (d if s == 0 else pltpu.roll(d, s, axis=1))
        o_ref[...] = _hdot_nt(acc, _bias_line_onehot(clip, tp))

    return pl.pallas_call(
        body, in_specs=[pl.BlockSpec(memory_space=pltpu.VMEM)], out_specs=pl.BlockSpec(memory_space=pltpu.VMEM),
        out_shape=jax.ShapeDtypeStruct((nh, tp), F32), name=name, compiler_params=_cp())(dbias)


ATT_TILE = LEFT_CHUNKS * CHUNK


ATT_GROUP = 8


def _att_softmax(s, bias, n_chunk):
    s = s * (B_HEAD ** -0.5) + bias
    slot = lax.broadcasted_iota(jnp.int32, s.shape, 1)
    valid = (slot >= (LEFT_CHUNKS - n_chunk) * CHUNK) & (slot < BAND)
    s = jnp.where(valid, s, NEG_INF)
    p = jnp.exp(s - jnp.max(s, axis=-1, keepdims=True))
    return p / jnp.sum(p, axis=-1, keepdims=True)


def _att_specs(n_pairs):
    prev = lambda p, i: (jnp.maximum(i - 1, 0), p)
    cur = lambda p, i: (i, p)
    prev_v = lambda p, i: (jnp.maximum(i - 1, 0), n_pairs + p)
    cur_v = lambda p, i: (i, n_pairs + p)
    blk = (ATT_TILE, LANES)
    return [pl.BlockSpec(blk, prev), pl.BlockSpec(blk, cur), pl.BlockSpec(blk, prev_v), pl.BlockSpec(blk, cur_v)]


def _att_fill(kbuf, vbuf, kp_ref, kc_ref, vp_ref, vc_ref):
    t = ATT_TILE
    kbuf[0:t, :] = kp_ref[...].astype(BF16)
    kbuf[t:2 * t, :] = kc_ref[...].astype(BF16)
    kbuf[2 * t:, :] = jnp.zeros((CHUNK, LANES), BF16)
    vbuf[0:t, :] = vp_ref[...].astype(BF16)
    vbuf[t:2 * t, :] = vc_ref[...].astype(BF16)
    vbuf[2 * t:, :] = jnp.zeros((CHUNK, LANES), BF16)


def _stack_heads(x, first):
    return jnp.concatenate([jnp.where(first, x, 0.0), jnp.where(first, 0.0, x)], axis=0).astype(BF16)


def attn_fwd(qp, kv, bias, name, rider=None):
    n, bw = qp.shape
    n_pairs = bw // LANES
    t = ATT_TILE
    cpt = t // CHUNK

    def body(q_ref, kp_ref, kc_ref, vp_ref, vc_ref, b_ref, o_ref, kbuf, vbuf):
        i = pl.program_id(1)
        _att_fill(kbuf, vbuf, kp_ref, kc_ref, vp_ref, vc_ref)
        lane = lax.broadcasted_iota(jnp.int32, (CHUNK, LANES), 1)
        first = lane < B_HEAD
        for g0 in range(0, cpt, ATT_GROUP):
            chunks = list(range(g0, min(g0 + ATT_GROUP, cpt)))
            band = [slice(c * CHUNK, c * CHUNK + BAND_PAD) for c in chunks]
            q2 = [_stack_heads(q_ref[c * CHUNK:(c + 1) * CHUNK, :], first) for c in chunks]
            s = [_dot_nt(q_u, kbuf[b_u, :]) for q_u, b_u in zip(q2, band)]
            p = [_att_softmax(s_u, b_ref[0], i * cpt + c) for s_u, c in zip(s, chunks)]
            o = [_dot(p_u.astype(BF16), vbuf[b_u, :]) for p_u, b_u in zip(p, band)]
            for o_u, c in zip(o, chunks):
                o_ref[c * CHUNK:(c + 1) * CHUNK, :] = jnp.where(first, o_u[:CHUNK], o_u[CHUNK:])

    return host_call(
        body, grid=(n_pairs, n // t),
        in_specs=[pl.BlockSpec((t, LANES), lambda p, i: (i, p))] + _att_specs(n_pairs)
        + [pl.BlockSpec((1, 2 * CHUNK, BAND_PAD), lambda p, i: (p, 0, 0))],
        out_specs=[pl.BlockSpec((t, LANES), lambda p, i: (i, p))],
        out_shape=[jax.ShapeDtypeStruct((n, bw), F32)],
        scratch_shapes=[pltpu.VMEM((2 * t + CHUNK, LANES), BF16), pltpu.VMEM((2 * t + CHUNK, LANES), BF16)],
        name=name, sem=("parallel", "parallel"), args=(qp, kv, kv, kv, kv, bias), rider=rider)


def attn_bwd(qp, kv, bias, d_o, dk_in, dv_in, name, rider=None):
    n, bw = qp.shape
    n_pairs = bw // LANES
    t = ATT_TILE
    cpt = t // CHUNK
    nt = n // t
    have_in = dk_in is not None
    scale = B_HEAD ** -0.5

    def body(*refs):
        q_ref, kp_ref, kc_ref, vp_ref, vc_ref, b_ref, do_ref = refs[:7]
        pos = 7
        if have_in:
            dki_ref, dvi_ref = refs[7:9]
            pos = 9
        dq_ref, dk_ref, dv_ref, db_ref, kbuf, vbuf, dkacc, dvacc = refs[pos:]
        j = pl.program_id(1)
        i = nt - 1 - j
        _att_fill(kbuf, vbuf, kp_ref, kc_ref, vp_ref, vc_ref)

        @pl.when(j == 0)
        def _():
            dkacc[...] = jnp.zeros_like(dkacc)
            dvacc[...] = jnp.zeros_like(dvacc)
            db_ref[...] = jnp.zeros_like(db_ref)

        @pl.when(j > 0)
        def _():
            dkacc[t:2 * t, :] = dkacc[0:t, :]
            dvacc[t:2 * t, :] = dvacc[0:t, :]
            dkacc[0:t, :] = jnp.zeros((t, LANES), F32)
            dvacc[0:t, :] = jnp.zeros((t, LANES), F32)

        lane = lax.broadcasted_iota(jnp.int32, (CHUNK, LANES), 1)
        first = lane < B_HEAD
        for g0 in range(0, cpt, ATT_GROUP):
            chunks = list(range(g0, min(g0 + ATT_GROUP, cpt)))
            rows = [slice(c * CHUNK, (c + 1) * CHUNK) for c in chunks]
            band = [slice(c * CHUNK, c * CHUNK + BAND_PAD) for c in chunks]
            q2 = [_stack_heads(q_ref[r, :], first) for r in rows]
            do2 = [_stack_heads(do_ref[r, :], first) for r in rows]
            s = [_dot_nt(q_u, kbuf[b_u, :]) for q_u, b_u in zip(q2, band)]
            dp = [_dot_nt(d_u, vbuf[b_u, :]) for d_u, b_u in zip(do2, band)]
            p = [_att_softmax(s_u, b_ref[0], i * cpt + c) for s_u, c in zip(s, chunks)]
            ds = [p_u * (dp_u - jnp.sum(dp_u * p_u, axis=-1, keepdims=True)) for p_u, dp_u in zip(p, dp)]
            dsb = [(d_u * scale).astype(BF16) for d_u in ds]
            dv = [_dot_tn(p_u.astype(BF16), d_u) for p_u, d_u in zip(p, do2)]
            dq = [_dot(d_u, kbuf[b_u, :]) for d_u, b_u in zip(dsb, band)]
            dk = [_dot_tn(d_u, q_u) for d_u, q_u in zip(dsb, q2)]
            db_ref[0] += functools.reduce(lambda a, b: a + b, ds)
            for r in range(chunks[0], chunks[-1] + BAND // CHUNK):
                terms = [(u, r - c) for u, c in enumerate(chunks) if 0 <= r - c < BAND // CHUNK]
                blk = slice(r * CHUNK, (r + 1) * CHUNK)
                dvacc[blk, :] += functools.reduce(lambda a, b: a + b, [dv[u][o * CHUNK:(o + 1) * CHUNK] for u, o in terms])
                dkacc[blk, :] += functools.reduce(lambda a, b: a + b, [dk[u][o * CHUNK:(o + 1) * CHUNK] for u, o in terms])
            for u in range(len(chunks)):
                dq_ref[rows[u], :] = jnp.where(first, dq[u][:CHUNK], dq[u][CHUNK:])

        if have_in:
            dk_ref[...] = dkacc[t:2 * t, :] + dki_ref[...]
            dv_ref[...] = dvacc[t:2 * t, :] + dvi_ref[...]
        else:
            dk_ref[...] = dkacc[t:2 * t, :]
            dv_ref[...] = dvacc[t:2 * t, :]

    rev = lambda p, j: (nt - 1 - j, p)
    tok = pl.BlockSpec((t, LANES), rev)
    kv_specs = [pl.BlockSpec((t, LANES), lambda p, j: (jnp.maximum(nt - 2 - j, 0), p)),
                pl.BlockSpec((t, LANES), rev),
                pl.BlockSpec((t, LANES), lambda p, j: (jnp.maximum(nt - 2 - j, 0), n_pairs + p)),
                pl.BlockSpec((t, LANES), lambda p, j: (nt - 1 - j, n_pairs + p))]
    in_specs = [tok] + kv_specs + [pl.BlockSpec((1, 2 * CHUNK, BAND_PAD), lambda p, j: (p, 0, 0)), tok]
    args = [qp, kv, kv, kv, kv, bias, d_o]
    if have_in:
        in_specs += [tok, tok]
        args += [dk_in, dv_in]
    out = jax.ShapeDtypeStruct((n, bw), F32)
    return host_call(
        body, grid=(n_pairs, nt), in_specs=in_specs,
        out_specs=[tok, tok, tok, pl.BlockSpec((1, 2 * CHUNK, BAND_PAD), lambda p, j: (p, 0, 0))],
        out_shape=[out, out, out, jax.ShapeDtypeStruct((n_pairs, 2 * CHUNK, BAND_PAD), F32)],
        scratch_shapes=[pltpu.VMEM((2 * t + CHUNK, LANES), BF16), pltpu.VMEM((2 * t + CHUNK, LANES), BF16),
                        pltpu.VMEM((2 * t + CHUNK, LANES), F32), pltpu.VMEM((2 * t + CHUNK, LANES), F32)],
        name=name, sem=("parallel", "arbitrary"), args=args, rider=rider)


def adamw(w, gstack, m, v, name):
    r, c = w.shape
    s = gstack.shape[0]
    tr = _pick(r, 512, 8)

    def body(w_ref, g_ref, m_ref, v_ref, go_ref, d_ref, mo_ref, vo_ref):
        g = g_ref[0].astype(F32)
        for k in range(1, s):
            g = g + g_ref[k].astype(F32)
        mn = ADAM_B1 * m_ref[...] + (1.0 - ADAM_B1) * g
        vn = ADAM_B2 * v_ref[...] + (1.0 - ADAM_B2) * (g * g)
        m_hat = mn / (1.0 - ADAM_B1 ** ADAM_STEP)
        v_hat = vn / (1.0 - ADAM_B2 ** ADAM_STEP)
        go_ref[...] = g
        d_ref[...] = -ADAM_LR * (m_hat / (jnp.sqrt(v_hat) + ADAM_EPS) + ADAM_WD * w_ref[...])
        mo_ref[...] = mn
        vo_ref[...] = vn

    blk = pl.BlockSpec((tr, c), lambda i: (i, 0))
    out = jax.ShapeDtypeStruct((r, c), F32)
    return pl.pallas_call(
        body, grid=(r // tr,),
        in_specs=[blk, pl.BlockSpec((s, tr, c), lambda i: (0, i, 0)), blk, blk],
        out_specs=[blk] * 4, out_shape=[out] * 4, name=name,
        compiler_params=_cp(("parallel",)))(w, gstack, m, v)


def _place():
    x, y, c = lax.axis_index("x"), lax.axis_index("y"), lax.axis_index("c")
    chips = [(1 - x, y), (x, 1 - y), (1 - x, 1 - y)]
    return x, y, c, chips


def _ag_copy(outs, send_sems, recv_sems, t, k, block, to, src=None):
    def slot(dev):
        return outs[t].at[4 * dev[0] + 2 * dev[1] + dev[2]]
    return pltpu.make_async_remote_copy(
        src_ref=slot(block) if src is None else src, dst_ref=slot(block),
        send_sem=send_sems.at[7 * t + k], recv_sem=recv_sems.at[7 * t + k], device_id=to, device_id_type=MESH)


def _ag_start(ins, outs, send_sems, recv_sems, local_sems):
    x, y, c, chips = _place()
    me = (x, y, c)
    for t in range(len(ins)):
        pltpu.make_async_copy(ins[t], outs[t].at[4 * x + 2 * y + c], local_sems.at[t]).start()
        _ag_copy(outs, send_sems, recv_sems, t, 0, me, (x, y, 1 - c), src=ins[t]).start()
        for j, chip in enumerate(chips):
            _ag_copy(outs, send_sems, recv_sems, t, 1 + j, me, (*chip, c), src=ins[t]).start()


def _ag_finish(ins, outs, send_sems, recv_sems, local_sems):
    x, y, c, chips = _place()
    me, sibling = (x, y, c), (x, y, 1 - c)
    nt = len(ins)
    for t in range(nt):
        for j, chip in enumerate(chips):
            _ag_copy(outs, send_sems, recv_sems, t, 1 + j, (*chip, c), me).wait_recv()
            _ag_copy(outs, send_sems, recv_sems, t, 4 + j, (*chip, c), sibling).start()
    for t in range(nt):
        _ag_copy(outs, send_sems, recv_sems, t, 0, sibling, me).wait_recv()
        for j, chip in enumerate(chips):
            _ag_copy(outs, send_sems, recv_sems, t, 4 + j, (*chip, 1 - c), me).wait_recv()
    for t in range(nt):
        _ag_copy(outs, send_sems, recv_sems, t, 0, me, sibling, src=ins[t]).wait_send()
        for j, chip in enumerate(chips):
            _ag_copy(outs, send_sems, recv_sems, t, 1 + j, me, (*chip, c), src=ins[t]).wait_send()
            _ag_copy(outs, send_sems, recv_sems, t, 4 + j, (*chip, c), sibling).wait_send()
        pltpu.make_async_copy(ins[t], outs[t].at[4 * x + 2 * y + c], local_sems.at[t]).wait()


def _rs_a_copy(ins, outs, send_sems, recv_sems, t, q):
    x, y, c, _ = _place()
    return pltpu.make_async_remote_copy(
        src_ref=ins[t].at[2 * q + (1 - c)], dst_ref=outs[t].at[q],
        send_sem=send_sems.at[4 * t + q], recv_sem=recv_sems.at[4 * t + q],
        device_id=(x, y, 1 - c), device_id_type=MESH)


def _rs_a_start(ins, outs, send_sems, recv_sems, local_sems):
    for t in range(len(ins)):
        for q in range(4):
            _rs_a_copy(ins, outs, send_sems, recv_sems, t, q).start()


def _rs_a_finish(ins, outs, send_sems, recv_sems, local_sems):
    for t in range(len(ins)):
        for q in range(4):
            _rs_a_copy(ins, outs, send_sems, recv_sems, t, q).wait_recv()
    for t in range(len(ins)):
        for q in range(4):
            _rs_a_copy(ins, outs, send_sems, recv_sems, t, q).wait_send()


def _rs_b_copy(ins, outs, send_sems, recv_sems, t, j, sending):
    x, y, c, chips = _place()
    mine, other = 2 * x + y, 2 * chips[j][0] + chips[j][1]
    return pltpu.make_async_remote_copy(
        src_ref=ins[t].at[other if sending else mine], dst_ref=outs[t].at[mine if sending else other],
        send_sem=send_sems.at[3 * t + j], recv_sem=recv_sems.at[3 * t + j],
        device_id=(*chips[j], c), device_id_type=MESH)


def _rs_b_start(ins, outs, send_sems, recv_sems, local_sems):
    x, y, _, _ = _place()
    for t in range(len(ins)):
        for j in range(3):
            _rs_b_copy(ins, outs, send_sems, recv_sems, t, j, True).start()
        pltpu.make_async_copy(ins[t].at[2 * x + y], outs[t].at[2 * x + y], local_sems.at[t]).start()


def _rs_b_finish(ins, outs, send_sems, recv_sems, local_sems):
    x, y, _, _ = _place()
    for t in range(len(ins)):
        for j in range(3):
            _rs_b_copy(ins, outs, send_sems, recv_sems, t, j, False).wait_recv()
    for t in range(len(ins)):
        for j in range(3):
            _rs_b_copy(ins, outs, send_sems, recv_sems, t, j, True).wait_send()
        pltpu.make_async_copy(ins[t].at[2 * x + y], outs[t].at[2 * x + y], local_sems.at[t]).wait()


_EXCHANGES = {
    "all_gather": (7, lambda a: (N_DEV, *a.shape), _ag_start, _ag_finish),
    "rs_sibling": (4, lambda a: (4, *a.shape[1:]), _rs_a_start, _rs_a_finish),
    "rs_chips": (3, lambda a: a.shape, _rs_b_start, _rs_b_finish),
}


def _exchange_parts(kind, arrays):
    per, shape_of, start, finish = _EXCHANGES[kind]
    n = len(arrays)
    out_shape = [jax.ShapeDtypeStruct(shape_of(a), a.dtype) for a in arrays]
    sems = [pltpu.SemaphoreType.DMA((per * n,)), pltpu.SemaphoreType.DMA((per * n,)), pltpu.SemaphoreType.DMA((n,))]
    return out_shape, sems, start, finish


def exchange(kind, arrays, name):
    n = len(arrays)
    out_shape, sems, start, finish = _exchange_parts(kind, arrays)
    any_spec = pl.BlockSpec(memory_space=pl.ANY)

    def body(*refs):
        ins, outs, sem_refs = refs[:n], refs[n:2 * n], refs[2 * n:]
        start(ins, outs, *sem_refs)
        finish(ins, outs, *sem_refs)

    return pl.pallas_call(body, in_specs=[any_spec] * n, out_specs=[any_spec] * n, out_shape=out_shape,
                          scratch_shapes=sems, name=name)(*arrays)


def host_call(body, *, grid, in_specs, out_specs, out_shape, scratch_shapes, args, name, sem, rider=None):
    if not rider:
        outs = pl.pallas_call(body, grid=grid, in_specs=in_specs, out_specs=out_specs, out_shape=out_shape,
                              scratch_shapes=scratch_shapes, name=name, compiler_params=_cp(sem))(*args)
        return outs, None
    riders = [rider] if isinstance(rider, tuple) else list(rider)
    arrays = [a for _, arrs in riders for a in arrs]
    parts = [_exchange_parts(kind, arrs) for kind, arrs in riders]
    counts = [len(arrs) for _, arrs in riders]
    nr, ni, no, ns = len(arrays), len(in_specs), len(out_specs), len(scratch_shapes)
    any_spec = pl.BlockSpec(memory_space=pl.ANY)

    def wrapped(*refs):
        ins, r_ins = refs[:ni], refs[ni:ni + nr]
        outs, r_outs = refs[ni + nr:ni + nr + no], refs[ni + nr + no:ni + 2 * nr + no]
        scratch, sem_refs = refs[ni + 2 * nr + no:ni + 2 * nr + no + ns], refs[ni + 2 * nr + no + ns:]
        first = pl.program_id(0) == 0
        last = pl.program_id(0) == grid[0] - 1
        for ax in range(1, len(grid)):
            first = first & (pl.program_id(ax) == 0)
            last = last & (pl.program_id(ax) == grid[ax] - 1)

        def each(which):
            pos = 0
            for e, (cnt, part) in enumerate(zip(counts, parts)):
                part[which](r_ins[pos:pos + cnt], r_outs[pos:pos + cnt], *sem_refs[3 * e:3 * e + 3])
                pos += cnt

        @pl.when(first)
        def _():
            each(2)
        body(*ins, *outs, *scratch)

        @pl.when(last)
        def _():
            each(3)

    outs = pl.pallas_call(
        wrapped, grid=grid, in_specs=list(in_specs) + [any_spec] * nr, out_specs=list(out_specs) + [any_spec] * nr,
        out_shape=list(out_shape) + [s for p in parts for s in p[0]],
        scratch_shapes=list(scratch_shapes) + [s for p in parts for s in p[1]], name=name,
        compiler_params=_cp(("arbitrary",) * len(grid)))(*args, *arrays)
    got, pos = [], no
    for cnt in counts:
        got.append(outs[pos:pos + cnt])
        pos += cnt
    return outs[:no], (got[0] if isinstance(rider, tuple) else got)


def pair_add(g8, recv, c_idx, name):
    _, r, c = g8.shape
    tr = _pick(r, 512, 8)

    def body(c_ref, g_ref, r_ref, o_ref):
        o_ref[...] = (g_ref[...] + r_ref[...]).astype(BF16)

    return pl.pallas_call(
        body,
        grid_spec=pltpu.PrefetchScalarGridSpec(
            num_scalar_prefetch=1, grid=(4, r // tr),
            in_specs=[pl.BlockSpec((1, tr, c), lambda q, i, cr: (2 * q + cr[0], i, 0)),
                      pl.BlockSpec((1, tr, c), lambda q, i, cr: (q, i, 0))],
            out_specs=pl.BlockSpec((1, tr, c), lambda q, i, cr: (q, i, 0))),
        out_shape=jax.ShapeDtypeStruct((4, r, c), BF16), name=name,
        compiler_params=_cp(("parallel", "parallel")))(c_idx, g8, recv)


def all_reduce_small(pack, name):
    r, c = pack.shape

    def body(x_ref, o_ref, buf, send_sems, recv_sems, local_sem):
        x, y, cc, chips = _place()
        me, sibling = (x, y, cc), (x, y, 1 - cc)

        def slot(dev):
            return buf.at[4 * dev[0] + 2 * dev[1] + dev[2]]

        def copy(k, block, to, src=None):
            return pltpu.make_async_remote_copy(
                src_ref=slot(block) if src is None else src, dst_ref=slot(block),
                send_sem=send_sems.at[k], recv_sem=recv_sems.at[k], device_id=to, device_id_type=MESH)

        mine = pltpu.make_async_copy(x_ref, slot(me), local_sem)
        mine.start()
        first = [copy(0, me, sibling, src=x_ref)]
        first += [copy(1 + j, me, (*chip, cc), src=x_ref) for j, chip in enumerate(chips)]
        for cp in first:
            cp.start()
        passed = [copy(4 + j, (*chip, cc), sibling) for j, chip in enumerate(chips)]
        for j, chip in enumerate(chips):
            copy(1 + j, (*chip, cc), me).wait_recv()
            passed[j].start()
        copy(0, sibling, me).wait_recv()
        for j, chip in enumerate(chips):
            copy(4 + j, (*chip, 1 - cc), me).wait_recv()
        for cp in first + passed:
            cp.wait_send()
        mine.wait()
        acc = buf[0]
        for k in range(1, N_DEV):
            acc = acc + buf[k]
        o_ref[...] = acc

    return pl.pallas_call(
        body, in_specs=[pl.BlockSpec(memory_space=pltpu.VMEM)],
        out_specs=pl.BlockSpec(memory_space=pltpu.VMEM),
        out_shape=jax.ShapeDtypeStruct((r, c), F32),
        scratch_shapes=[pltpu.VMEM((N_DEV, r, c), F32), pltpu.SemaphoreType.DMA((7,)),
                        pltpu.SemaphoreType.DMA((7,)), pltpu.SemaphoreType.DMA],
        name=name, compiler_params=_cp())(pack)


def _row(v):
    return v.reshape(1, -1)


def _lane_row(vals, offset):
    return jnp.pad(vals, (offset, LANES - offset - vals.shape[0])).reshape(1, LANES)


def _bias_to_pairs(b):
    i, nh, bp = b.shape
    return b.transpose(1, 0, 2).reshape(nh // 2, 2 * i, bp)


def _bias_from_pairs(b):
    p, i2, bp = b.shape
    return b.reshape(2 * p, i2 // 2, bp).transpose(1, 0, 2)


class LocalWeights:
    def __init__(self, W):
        self.W = W
        self.grads = {}

    def big(self, l, la):
        W = self.W
        out = {"f_w_up": W["f_w_up"][l], "f_w_down": W["f_w_down"][l]}
        if l < la:
            out.update(a_w_in=W["a_w_in"][l], a_w_out=W["a_w_out"][l])
        else:
            out.update(b_w_q=W["b_w_q"][l - la], b_w_out=W["b_w_out"][l - la])
        if l == la:
            out["w_kv"] = W["w_kv"]
        return out

    col_blocks = None

    def prep_rider(self, l):
        return None

    def prep_got(self, l, got):
        pass

    def fwd_rider(self, l):
        return None

    def fwd_got(self, l, got):
        pass

    def bwd_rider_a(self, l):
        return None

    def bwd_got_a(self, l, got):
        pass

    def bwd_rider_b(self, l):
        return None

    def bwd_got_b(self, l, got):
        pass

    def bwd_rider_c(self, l):
        return None

    def bwd_got_c(self, l, got):
        pass

    def ffn_grads_ready(self, l, grads):
        pass

    def grads_ready(self, l, grads):
        for k_, g in grads.items():
            self.grads.setdefault(k_, {})[l] = g

    def stacked(self):
        return {k_: (jnp.stack([v_[l] for l in sorted(v_)]) if k_ != "w_kv" else next(iter(v_.values())))
                for k_, v_ in self.grads.items()}


def _named(name, l, rider):
    return name if rider is None else f"{name}_x{l}"


def local_step(x, target, W, comm=None):
    comm = LocalWeights(W) if comm is None else comm
    n, d = x.shape
    la, ha = W["a_A_log"].shape
    lb, hb, tbl = W["b_rel_bias"].shape
    depth = W["f_norm"].shape[0]
    clip = (tbl - 1) // 2
    tp = -(-tbl // LANES) * LANES
    qk = ha * A_HEAD
    cw = 3 * qk
    bw = hb * B_HEAD
    a_in = cw + qk + 2 * ha

    h = x
    saves = []
    kv = h_kv = w_kv = None
    for l in range(depth):
        big = comm.big(l, la)
        sv = {"h_in": h, "big": big}
        rider = comm.fwd_rider(l)
        if l < la:
            alog = _lane_row(W["a_A_log"][l], ha)
            dtb = _lane_row(W["a_dt_bias"][l], ha)
            proj = norm_matmul(h, _row(W["a_norm"][l]), big["a_w_in"], "a_in_proj")
            early = comm.prep_rider(l)
            (q, k, v, bb, gb), got = gdn_prep(proj, W["a_conv"][l], alog, dtb, ha, _named("gdn_prep", l, early), early)
            comm.prep_got(l, got)
            (o, states, tinv), got = gdn_fwd(q, k, v, bb, gb, ha, _named("gdn_fwd", l, rider), rider)
            h, y = gdn_out(o, proj, _row(W["a_out_norm"][l]), big["a_w_out"], h, ha, "gdn_out")
            sv.update(proj=proj, q=q, k=k, v=v, bb=bb, gb=gb, states=states, tinv=tinv, o=o, y=y, alog=alog, dtb=dtb)
        else:
            j = l - la
            if j == 0:
                h_kv, w_kv = h, big["w_kv"]
                kv = norm_matmul(h, _row(W["kv_norm"]), w_kv, "kv_proj")
            qp = norm_matmul(h, _row(W["b_norm"][j]), big["b_w_q"], "b_q_proj")
            tblp = jnp.pad(W["b_rel_bias"][j], ((0, 0), (0, tp - tbl)))
            bias = _bias_to_pairs(bias_expand(tblp, clip, "bias_expand"))
            (o,), got = attn_fwd(qp, kv, bias, _named("attn_fwd", l, rider), rider)
            h = matmul_res(o, big["b_w_out"], h, "b_out_proj")
            sv.update(qp=qp, bias=bias, o=o)
        comm.fwd_got(l, got)
        sv["h_mid"] = h
        up = norm_matmul(h, _row(W["f_norm"][l]), big["f_w_up"], "f_up_proj", out_dtype=BF16)
        h, act, hc = ffn_act_down(up, W["f_conv"][l], _row(W["f_conv_b"][l]), big["f_w_down"], h, "ffn_act_down")
        sv.update(up=up, act=act, hc=hc)
        saves.append(sv)

    loss, dh, d_final = loss_head(h, _row(W["final_norm"]), target)

    G = {k_: [None] * (la if k_.startswith("a_") else lb if k_.startswith("b_") else depth)
         for k_ in ("a_norm", "a_conv", "a_A_log", "a_dt_bias", "a_out_norm",
                    "b_norm", "b_rel_bias", "f_norm", "f_conv", "f_conv_b")}
    G["final_norm"] = d_final[0]
    dk_acc = dv_acc = None
    for l in reversed(range(depth)):
        sv = saves[l]
        big = sv["big"]
        gbig = {}
        dhc, dcb = ffn_bwd_act(dh, sv["hc"], big["f_w_down"], "ffn_bwd_act")
        gbig["f_w_down"] = matmul_tn(sv["act"], dh, "f_down_wgrad")
        G["f_conv_b"][l] = dcb[0]
        rider = comm.bwd_rider_a(l)
        (dh, dup, dcw, dg), got = ffn_bwd_up(dhc, sv["up"], W["f_conv"][l], big["f_w_up"], sv["h_mid"], dh,
                                             _row(W["f_norm"][l]), _named("ffn_bwd_up", l, rider), rider)
        comm.bwd_got_a(l, got)
        G["f_conv"][l] = dcw
        G["f_norm"][l] = dg[0]
        gbig["f_w_up"] = norm_matmul_tn(sv["h_mid"], _row(W["f_norm"][l]), dup, "f_up_wgrad", comm.col_blocks)
        comm.ffn_grads_ready(l, gbig)
        rider = comm.bwd_rider_b(l)
        if l < la:
            w_in = big["a_w_in"]
            do, dz, dwn = gdn_out_bwd(dh, sv["o"], sv["proj"], _row(W["a_out_norm"][l]), big["a_w_out"], ha, "gdn_out_bwd")
            G["a_out_norm"][l] = dwn[0]
            gbig["a_w_out"] = matmul_tn(sv["y"], dh, "a_out_wgrad")
            (dq, dk, dv, dbb, dgb), got = gdn_bwd(sv["q"], sv["k"], sv["v"], sv["bb"], sv["gb"], sv["states"], sv["tinv"], do, ha,
                                                  _named("gdn_bwd", l, rider), rider)
            comm.bwd_got_b(l, got)
            du, dba, dal, ddt = gdn_prep_bwd(sv["proj"], W["a_conv"][l], sv["alog"], sv["dtb"],
                                             dq, dk, dv, dbb, dgb, ha, "gdn_prep_bwd")
            G["a_A_log"][l] = dal[0, ha:2 * ha]
            G["a_dt_bias"][l] = ddt[0, ha:2 * ha]
            dqkv, dconv = conv_bwd(du, sv["proj"], W["a_conv"][l], A_CONV, "gdn_conv_bwd")
            G["a_conv"][l] = dconv
            gam = _row(W["a_norm"][l])
            pieces = [(dqkv, w_in[:, :cw]), (dz, w_in[:, cw:cw + qk]), (dba, w_in[:, cw + qk:])]
            gbig["a_w_in"] = jnp.concatenate(
                [norm_matmul_tn(sv["h_in"], gam, dqkv, "a_in_wgrad_qkv"),
                 norm_matmul_tn(sv["h_in"], gam, dz, "a_in_wgrad_z"),
                 norm_matmul_tn(sv["h_in"], gam, dba, "a_in_wgrad_ba")[:, :2 * ha]], axis=1)
            rider = comm.bwd_rider_c(l)
            if rider is None:
                dh, dg = dx_norm_bwd(dh, sv["h_in"], gam, pieces, "a_in_dx")
            else:
                dh, dg, got = dx_norm_bwd(dh, sv["h_in"], gam, pieces, _named("a_in_dx", l, rider), rider)
                comm.bwd_got_c(l, got)
            G["a_norm"][l] = dg[0]
        else:
            j = l - la
            d_o = matmul_nt(dh, big["b_w_out"], "b_out_dx")
            gbig["b_w_out"] = matmul_tn(sv["o"], dh, "b_out_wgrad")
            (dq, dk_acc, dv_acc, dbias), got = attn_bwd(
                sv["qp"], kv, sv["bias"], d_o, dk_acc, dv_acc,
                _named("attn_bwd" if dk_acc is None else "attn_bwd_acc", l, rider), rider)
            comm.bwd_got_b(l, got)
            G["b_rel_bias"][j] = bias_expand_bwd(_bias_from_pairs(dbias), clip, tp, "bias_expand_bwd")[:, :tbl]
            gam = _row(W["b_norm"][j])
            gbig["b_w_q"] = norm_matmul_tn(sv["h_in"], gam, dq, "b_q_wgrad")
            dh, dg = dx_norm_bwd(dh, sv["h_in"], gam, [(dq, big["b_w_q"])], "b_q_dx")
            G["b_norm"][j] = dg[0]
            if j == 0:
                gam = _row(W["kv_norm"])
                half = None if comm.col_blocks is None else comm.col_blocks // 2
                gbig["w_kv"] = jnp.concatenate([norm_matmul_tn(h_kv, gam, dk_acc, "kv_wgrad_k", half),
                                                norm_matmul_tn(h_kv, gam, dv_acc, "kv_wgrad_v", half)],
                                               axis=1 if half is None else 0)
                dh, dg = dx_norm_bwd(dh, h_kv, gam, [(dk_acc, w_kv[:, :bw]), (dv_acc, w_kv[:, bw:])], "kv_dx")
                G["kv_norm"] = dg[0]
        comm.grads_ready(l, gbig)
    out = {k_: (jnp.stack(v_) if isinstance(v_, list) else v_) for k_, v_ in G.items()}
    return loss[0, 0], dh, out, comm


WEIGHTS = ["a_norm", "a_w_in", "a_conv", "a_A_log", "a_dt_bias", "a_out_norm", "a_w_out", "kv_norm", "w_kv",
           "b_norm", "b_w_q", "b_rel_bias", "b_w_out", "f_norm", "f_w_up", "f_conv", "f_conv_b", "f_w_down",
           "final_norm"]
SHARD_AXIS = {"a_norm": 1, "a_w_in": 2, "a_conv": 2, "a_w_out": 1, "w_kv": 1, "b_w_q": 1, "b_w_out": 1,
              "f_w_up": 2, "f_conv": 2, "f_w_down": 1}
BIG = ["a_w_in", "a_w_out", "w_kv", "b_w_q", "b_w_out", "f_w_up", "f_w_down"]
SMALL_SHARDED = ["a_norm", "a_conv", "f_conv"]


def _unstack(g, axis):
    return jnp.concatenate([g[i] for i in range(N_DEV)], axis=axis)


def _to_blocks(full, axis):
    parts = jnp.stack(jnp.split(full, N_DEV, axis=axis))
    return parts.reshape(N_DEV, -1, parts.shape[-1])


def _pack(arrs):
    flat = []
    for a in arrs:
        f = a.reshape(-1)
        flat.append(jnp.pad(f, (0, (-f.shape[0]) % LANES)))
    f = jnp.concatenate(flat)
    f = jnp.pad(f, (0, (-f.shape[0]) % (8 * LANES)))
    return f.reshape(-1, LANES)


def _unpack(pack, shapes):
    flat = pack.reshape(-1)
    out, pos = [], 0
    for s in shapes:
        sz = math.prod(s)
        out.append(flat[pos:pos + sz].reshape(s))
        pos += sz + (-sz) % LANES
    return out


def _as2d(a):
    return a.reshape(1, -1) if a.ndim == 1 else a.reshape(-1, a.shape[-1])


def kernel(x, a_norm, a_w_in, a_conv, a_A_log, a_dt_bias, a_out_norm, a_w_out, kv_norm, w_kv, b_norm, b_w_q, b_rel_bias, b_w_out, f_norm, f_w_up, f_conv, f_conv_b, f_w_down, final_norm, loss_target, m_a_norm, m_a_w_in, m_a_conv, m_a_A_log, m_a_dt_bias, m_a_out_norm, m_a_w_out, m_kv_norm, m_w_kv, m_b_norm, m_b_w_q, m_b_rel_bias, m_b_w_out, m_f_norm, m_f_w_up, m_f_conv, m_f_conv_b, m_f_w_down, m_final_norm, v_a_norm, v_a_w_in, v_a_conv, v_a_A_log, v_a_dt_bias, v_a_out_norm, v_a_w_out, v_kv_norm, v_w_kv, v_b_norm, v_b_w_q, v_b_rel_bias, v_b_w_out, v_f_norm, v_f_w_up, v_f_conv, v_f_conv_b, v_f_w_down, v_final_norm):
    w = dict(a_norm=a_norm, a_w_in=a_w_in, a_conv=a_conv, a_A_log=a_A_log, a_dt_bias=a_dt_bias,
             a_out_norm=a_out_norm, a_w_out=a_w_out, kv_norm=kv_norm, w_kv=w_kv, b_norm=b_norm, b_w_q=b_w_q,
             b_rel_bias=b_rel_bias, b_w_out=b_w_out, f_norm=f_norm, f_w_up=f_w_up, f_conv=f_conv,
             f_conv_b=f_conv_b, f_w_down=f_w_down, final_norm=final_norm)
    mom = dict(a_norm=m_a_norm, a_w_in=m_a_w_in, a_conv=m_a_conv, a_A_log=m_a_A_log, a_dt_bias=m_a_dt_bias,
               a_out_norm=m_a_out_norm, a_w_out=m_a_w_out, kv_norm=m_kv_norm, w_kv=m_w_kv, b_norm=m_b_norm,
               b_w_q=m_b_w_q, b_rel_bias=m_b_rel_bias, b_w_out=m_b_w_out, f_norm=m_f_norm, f_w_up=m_f_w_up,
               f_conv=m_f_conv, f_conv_b=m_f_conv_b, f_w_down=m_f_w_down, final_norm=m_final_norm)
    var = dict(a_norm=v_a_norm, a_w_in=v_a_w_in, a_conv=v_a_conv, a_A_log=v_a_A_log, a_dt_bias=v_a_dt_bias,
               a_out_norm=v_a_out_norm, a_w_out=v_a_w_out, kv_norm=v_kv_norm, w_kv=v_w_kv, b_norm=v_b_norm,
               b_w_q=v_b_w_q, b_rel_bias=v_b_rel_bias, b_w_out=v_b_w_out, f_norm=v_f_norm, f_w_up=v_f_w_up,
               f_conv=v_f_conv, f_conv_b=v_f_conv_b, f_w_down=v_f_w_down, final_norm=v_final_norm)
    me = 4 * lax.axis_index("x") + 2 * lax.axis_index("y") + lax.axis_index("c")

    la, depth = a_A_log.shape[0], f_norm.shape[0]
    c_idx = lax.axis_index("c").astype(jnp.int32).reshape(1)
    shard_bf16 = {k: w[k].astype(BF16) for k in BIG}

    class Sharded(LocalWeights):
        col_blocks = N_DEV

        def __init__(self):
            super().__init__(w)
            self.full = {}
            self.stacks = {}
            self.pending = None
            self.parts = None

        def names(self, l):
            out = ["a_w_in", "a_w_out"] if l < la else ["b_w_q", "b_w_out"]
            return out + ["f_w_up", "f_w_down"] + (["w_kv"] if l == la else [])

        def index(self, k, l):
            return None if k == "w_kv" else (l - la if k.startswith("b_") else l)

        def shards(self, l, names=None):
            return [shard_bf16[k] if k == "w_kv" else shard_bf16[k][self.index(k, l)]
                    for k in (self.names(l) if names is None else names)]

        def install(self, l, gathered, names=None):
            out = self.full.setdefault(l, {})
            for k, g in zip(self.names(l) if names is None else names, gathered):
                out[k] = _unstack(g, SHARD_AXIS[k] - (k != "w_kv"))
                if k == "a_w_in":
                    out[k] = jnp.pad(out[k], ((0, 0), (0, (-out[k].shape[1]) % LANES)))

        def big(self, l, la_):
            return self.full[l]

        def first_names(self):
            return ["a_w_in"] if la > 0 else self.names(0)

        def prep_rider(self, l):
            rest = [k for k in self.names(0) if k not in self.first_names()]
            return ("all_gather", self.shards(0, rest)) if l == 0 and rest else None

        def prep_got(self, l, got):
            if got is not None:
                self.install(0, got, [k for k in self.names(0) if k not in self.first_names()])

        def fwd_rider(self, l):
            return ("all_gather", self.shards(l + 1)) if l + 1 < depth else None

        def fwd_got(self, l, got):
            if got is not None:
                self.install(l + 1, got)

        def blocks(self, grads, keys):
            return [grads[k] if grads[k].ndim == 3 else _to_blocks(grads[k], SHARD_AXIS[k] - (k != "w_kv"))
                    for k in keys]

        def grads_ready(self, l, grads):
            keys = [k for k in self.names(l) if (k, l) not in self.early_keys]
            self.pending = (l, keys, self.blocks(grads, keys))

        early = early_parts = None
        early_keys = ()

        def ffn_grads_ready(self, l, grads):
            if l == 0 and la > 0:
                keys = ["f_w_up", "f_w_down"]
                self.early = (keys, self.blocks(grads, keys))
                self.early_keys = tuple((k, 0) for k in keys)

        def bwd_rider_a(self, l):
            return None if self.pending is None else ("rs_sibling", self.pending[2])

        def add_pairs(self, g8, from_sibling):
            return [pair_add(g, r, c_idx, "grads_pair_add") for g, r in zip(g8, from_sibling)]

        def bwd_got_a(self, l, got):
            if got is not None:
                self.parts = self.add_pairs(self.pending[2], got)

        def bwd_rider_b(self, l):
            riders = [] if self.parts is None else [("rs_chips", self.parts)]
            if self.early is not None:
                riders.append(("rs_sibling", self.early[1]))
            return riders

        def keep(self, stacks):
            l, keys, _ = self.pending
            for k, s in zip(keys, stacks):
                self.stacks[(k, l)] = s
            self.pending = self.parts = None

        def bwd_got_b(self, l, got):
            got = list(got or [])
            if self.parts is not None:
                self.keep(got.pop(0))
            if self.early is not None and got:
                self.early_parts = self.add_pairs(self.early[1], got.pop(0))

        def bwd_rider_c(self, l):
            return None if self.early_parts is None else ("rs_chips", self.early_parts)

        def bwd_got_c(self, l, got):
            for k, s in zip(self.early[0], got):
                self.stacks[(k, 0)] = s
            self.early = self.early_parts = None

        def finish(self):
            self.parts = self.add_pairs(self.pending[2], exchange("rs_sibling", self.pending[2], "grads_to_sibling"))
            self.keep(exchange("rs_chips", self.parts, "grads_to_chips"))

    comm = Sharded()

    small_shapes = [w[k].shape for k in SMALL_SHARDED]
    gathered = exchange("all_gather", comm.shards(0, comm.first_names()) + [_pack([w[k] for k in SMALL_SHARDED])],
                        "weights_all_gather")
    comm.install(0, gathered[:-1], comm.first_names())
    full = dict(w)
    small = [_unpack(gathered[-1][i], small_shapes) for i in range(N_DEV)]
    for idx, k in enumerate(SMALL_SHARDED):
        full[k] = jnp.concatenate([small[i][idx] for i in range(N_DEV)], axis=SHARD_AXIS[k])

    loss_part, grad_x, G, _ = local_step(x[0], loss_target[0], full, comm)
    comm.finish()
    stacks = []
    for k in BIG:
        layers = sorted(l for (k_, l) in comm.stacks if k_ == k)
        stacks.append(jnp.concatenate([comm.stacks[(k, l)] for l in layers], axis=1))

    small_names = [k for k in WEIGHTS if k not in BIG]
    reduced = _unpack(all_reduce_small(_pack([G[k] for k in small_names] + [loss_part.reshape(1)]), "small_all_reduce"),
                      [G[k].shape for k in small_names] + [(1,)])
    loss = reduced[-1][0]
    small_g = dict(zip(small_names, reduced[:-1]))
    for k in SMALL_SHARDED:
        sz = w[k].shape[SHARD_AXIS[k]]
        small_g[k] = lax.dynamic_slice_in_dim(small_g[k], me * sz, sz, axis=SHARD_AXIS[k])

    res = {}
    for k, st in zip(BIG, stacks):
        outs = adamw(_as2d(w[k]), st, _as2d(mom[k]), _as2d(var[k]), "adamw_" + k)
        res[k] = [o.reshape(w[k].shape) for o in outs]
    for k in small_names:
        outs = adamw(_as2d(w[k]), _as2d(small_g[k])[None], _as2d(mom[k]), _as2d(var[k]), "adamw_" + k)
        res[k] = [o.reshape(w[k].shape) for o in outs]

    return (loss, grad_x[None], *[res[k][0] for k in WEIGHTS], *[res[k][1] for k in WEIGHTS],
            *[res[k][2] for k in WEIGHTS], *[res[k][3] for k in WEIGHTS])
```

```python
import functools
import math

import jax
import jax.numpy as jnp
from jax import lax
from jax.experimental import pallas as pl
from jax.experimental.pallas import tpu as pltpu

F32 = jnp.float32
BF16 = jnp.bfloat16
HI = lax.Precision.HIGHEST
MESH = pl.DeviceIdType.MESH

EPS = 1e-6
NEG_INF = -1e30
CHUNK = 64
LEFT_CHUNKS = 8
BAND = (LEFT_CHUNKS + 1) * CHUNK
BAND_PAD = 640
A_CONV = 4
F_CONV = 3
A_HEAD = 128
B_HEAD = 64
LANES = 128
HALO = 8
N_DEV = 8

ADAM_LR = 0.001
ADAM_B1 = 0.9
ADAM_B2 = 0.999
ADAM_EPS = 1e-08
ADAM_WD = 0.01
ADAM_STEP = 10

VMEM_LIMIT_V7X = 56 * 1024 * 1024
GDN_BWD_HEADS = 8
COL_CHUNK = 256
FFN_TILE = 256


def _cp(sem=None, vmem=VMEM_LIMIT_V7X):
    kw = dict(vmem_limit_bytes=vmem)
    if sem is not None:
        kw["dimension_semantics"] = sem
    return pltpu.CompilerParams(**kw)


def _pick(n, target, q=LANES):
    best = None
    for t in range(q, min(n, target) + 1, q):
        if n % t == 0:
            best = t
    return best if best is not None else n


def _sig(x):
    return 1.0 / (1.0 + jnp.exp(-x))


def _softplus(x):
    return jnp.maximum(x, 0.0) + jnp.log(1.0 + jnp.exp(-jnp.abs(x)))


def _rms(x, g):
    return x * lax.rsqrt(jnp.mean(x * x, axis=-1, keepdims=True) + EPS) * g


def _rms_bwd(x, g, dxn):
    r = lax.rsqrt(jnp.mean(x * x, axis=-1, keepdims=True) + EPS)
    gd = dxn * g
    dx = r * gd - x * (r * r * r) * jnp.mean(x * gd, axis=-1, keepdims=True)
    dg = jnp.sum(dxn * x * r, axis=0, keepdims=True)
    return dx, dg


def _dot(a, b):
    return jnp.dot(a, b, preferred_element_type=F32)


def _dot_nt(a, b):
    return lax.dot_general(a, b, (((1,), (1,)), ((), ())), preferred_element_type=F32)


def _dot_tn(a, b):
    return lax.dot_general(a, b, (((0,), (0,)), ((), ())), preferred_element_type=F32)


def _hdot(a, b):
    return jnp.dot(a, b, precision=HI, preferred_element_type=F32)


def _hdot_nt(a, b):
    return lax.dot_general(a, b, (((1,), (1,)), ((), ())), precision=HI, preferred_element_type=F32)


def _hdot_tn(a, b):
    return lax.dot_general(a, b, (((0,), (0,)), ((), ())), precision=HI, preferred_element_type=F32)


def _resident(shape, index_map):
    return pl.BlockSpec(shape, index_map, pipeline_mode=pl.Buffered(1))


def norm_matmul(h, gamma, w, name, out_dtype=F32):
    n, d = h.shape
    nc = w.shape[1]
    tm = _pick(n, 2048 if out_dtype == BF16 else 1024, 8)
    tn = _pick(nc, 1536)

    def body(h_ref, g_ref, w_ref, o_ref):
        xn = _rms(h_ref[...], g_ref[...])
        o_ref[...] = _dot(xn.astype(BF16), w_ref[...]).astype(out_dtype)

    return pl.pallas_call(
        body, grid=(nc // tn, n // tm),
        in_specs=[pl.BlockSpec((tm, d), lambda j, i: (i, 0)),
                  pl.BlockSpec((1, d), lambda j, i: (0, 0)),
                  pl.BlockSpec((d, tn), lambda j, i: (0, j))],
        out_specs=pl.BlockSpec((tm, tn), lambda j, i: (i, j)),
        out_shape=jax.ShapeDtypeStruct((n, nc), out_dtype), name=name,
        compiler_params=_cp(("parallel", "parallel")))(h, gamma, w)


def norm_matmul_tn(h, gamma, dy, name, col_blocks=None):
    n, d = h.shape
    nc = dy.shape[1]
    tm = _pick(n, 2048 if dy.dtype == BF16 else 1024, 8)
    tn = _pick(nc, 1536)
    grid = (nc // tn, n // tm)
    in_specs = [pl.BlockSpec((tm, d), lambda j, i: (i, 0)),
                pl.BlockSpec((1, d), lambda j, i: (0, 0)),
                pl.BlockSpec((tm, tn), lambda j, i: (i, j))]

    if col_blocks is None or tn % (nc // col_blocks):
        def body(h_ref, g_ref, dy_ref, o_ref):
            @pl.when(pl.program_id(1) == 0)
            def _():
                o_ref[...] = jnp.zeros_like(o_ref)
            xn = _rms(h_ref[...], g_ref[...])
            o_ref[...] += _dot_tn(xn.astype(BF16), dy_ref[...].astype(BF16))

        out = pl.pallas_call(
            body, grid=grid, in_specs=in_specs, out_specs=pl.BlockSpec((d, tn), lambda j, i: (0, j)),
            out_shape=jax.ShapeDtypeStruct((d, nc), F32), name=name,
            compiler_params=_cp(("parallel", "arbitrary")))(h, gamma, dy)
        return out if col_blocks is None else jnp.stack(jnp.split(out, col_blocks, axis=1))

    bw = nc // col_blocks
    per = tn // bw

    def body_blocks(h_ref, g_ref, dy_ref, o_ref, acc):
        i = pl.program_id(1)

        @pl.when(i == 0)
        def _():
            acc[...] = jnp.zeros_like(acc)
        xn = _rms(h_ref[...], g_ref[...])
        acc[...] += _dot_tn(xn.astype(BF16), dy_ref[...].astype(BF16))

        @pl.when(i == grid[1] - 1)
        def _():
            for b in range(per):
                o_ref[b] = acc[:, b * bw:(b + 1) * bw]

    return pl.pallas_call(
        body_blocks, grid=grid, in_specs=in_specs, out_specs=pl.BlockSpec((per, d, bw), lambda j, i: (j, 0, 0)),
        out_shape=jax.ShapeDtypeStruct((col_blocks, d, bw), F32), scratch_shapes=[pltpu.VMEM((d, tn), F32)],
        name=name, compiler_params=_cp(("parallel", "arbitrary")))(h, gamma, dy)


def matmul_tn(a, dy, name):
    n, ka = a.shape
    nc = dy.shape[1]
    tm = _pick(n, 2048 if a.dtype == BF16 else 1024, 8)
    tk = _pick(ka, 1536)
    tn = _pick(nc, 1024)

    def body(a_ref, dy_ref, o_ref):
        @pl.when(pl.program_id(2) == 0)
        def _():
            o_ref[...] = jnp.zeros_like(o_ref)
        o_ref[...] += _dot_tn(a_ref[...].astype(BF16), dy_ref[...].astype(BF16))

    return pl.pallas_call(
        body, grid=(ka // tk, nc // tn, n // tm),
        in_specs=[pl.BlockSpec((tm, tk), lambda k, j, i: (i, k)),
                  pl.BlockSpec((tm, tn), lambda k, j, i: (i, j))],
        out_specs=pl.BlockSpec((tk, tn), lambda k, j, i: (k, j)),
        out_shape=jax.ShapeDtypeStruct((ka, nc), F32), name=name,
        compiler_params=_cp(("parallel", "parallel", "arbitrary")))(a, dy)


def matmul_res(a, w, h, name):
    n, k = a.shape
    d = w.shape[1]
    tm = _pick(n, 512, 8)

    def body(a_ref, w_ref, h_ref, o_ref):
        o_ref[...] = h_ref[...] + _dot(a_ref[...].astype(BF16), w_ref[...])

    return pl.pallas_call(
        body, grid=(n // tm,),
        in_specs=[pl.BlockSpec((tm, k), lambda i: (i, 0)),
                  _resident((k, d), lambda i: (0, 0)),
                  pl.BlockSpec((tm, d), lambda i: (i, 0))],
        out_specs=pl.BlockSpec((tm, d), lambda i: (i, 0)),
        out_shape=jax.ShapeDtypeStruct((n, d), F32), name=name,
        compiler_params=_cp(("parallel",)))(a, w, h)


def matmul_nt(dy, w, name):
    n, k = dy.shape
    d = w.shape[0]
    tm = _pick(n, 512, 8)

    def body(dy_ref, w_ref, o_ref):
        o_ref[...] = _dot_nt(dy_ref[...].astype(BF16), w_ref[...])

    return pl.pallas_call(
        body, grid=(n // tm,),
        in_specs=[pl.BlockSpec((tm, k), lambda i: (i, 0)),
                  _resident((d, k), lambda i: (0, 0))],
        out_specs=pl.BlockSpec((tm, d), lambda i: (i, 0)),
        out_shape=jax.ShapeDtypeStruct((n, d), F32), name=name,
        compiler_params=_cp(("parallel",)))(dy, w)


def dx_norm_bwd(dout, h, gamma, pieces, name, rider=None):
    n, d = h.shape
    tm = _pick(n, 256, 8)
    np_ = len(pieces)

    def body(*refs):
        dout_ref, h_ref, g_ref = refs[:3]
        dys = refs[3:3 + np_]
        ws = refs[3 + np_:3 + 2 * np_]
        dh_ref, dg_ref = refs[3 + 2 * np_:]
        dxn = _dot_nt(dys[0][...].astype(BF16), ws[0][...])
        for p in range(1, np_):
            dxn = dxn + _dot_nt(dys[p][...].astype(BF16), ws[p][...])
        dx, dg = _rms_bwd(h_ref[...], g_ref[...], dxn)
        dh_ref[...] = dout_ref[...] + dx

        @pl.when(pl.program_id(0) == 0)
        def _():
            dg_ref[...] = jnp.zeros_like(dg_ref)
        dg_ref[...] += dg

    in_specs = [pl.BlockSpec((tm, d), lambda i: (i, 0)),
                pl.BlockSpec((tm, d), lambda i: (i, 0)),
                pl.BlockSpec((1, d), lambda i: (0, 0))]
    in_specs += [pl.BlockSpec((tm, dy.shape[1]), lambda i: (i, 0)) for dy, _ in pieces]
    in_specs += [_resident(w.shape, lambda i: (0, 0)) for _, w in pieces]
    (dh, dg), got = host_call(
        body, grid=(n // tm,), in_specs=in_specs,
        out_specs=[pl.BlockSpec((tm, d), lambda i: (i, 0)), pl.BlockSpec((1, d), lambda i: (0, 0))],
        out_shape=[jax.ShapeDtypeStruct((n, d), F32), jax.ShapeDtypeStruct((1, d), F32)], name=name,
        scratch_shapes=[], sem=("arbitrary",), rider=rider,
        args=(dout, h, gamma, *[p[0] for p in pieces], *[p[1] for p in pieces]))
    return (dh, dg) if rider is None else (dh, dg, got)


def loss_head(h, gamma, target, name="loss_head"):
    n, d = h.shape
    tm = _pick(n, 512, 8)

    def body(h_ref, g_ref, t_ref, loss_ref, dh_ref, dg_ref):
        @pl.when(pl.program_id(0) == 0)
        def _():
            loss_ref[...] = jnp.zeros_like(loss_ref)
            dg_ref[...] = jnp.zeros_like(dg_ref)
        x = h_ref[...]
        g = g_ref[...]
        e = _rms(x, g) - t_ref[...]
        part = jnp.sum(jnp.sum(e * e, axis=-1, keepdims=True), axis=0, keepdims=True) * (0.5 / d)
        loss_ref[...] += jnp.broadcast_to(part, loss_ref.shape)
        dx, dg = _rms_bwd(x, g, e * (1.0 / d))
        dh_ref[...] = dx
        dg_ref[...] += dg

    return pl.pallas_call(
        body, grid=(n // tm,),
        in_specs=[pl.BlockSpec((tm, d), lambda i: (i, 0)), pl.BlockSpec((1, d), lambda i: (0, 0)),
                  pl.BlockSpec((tm, d), lambda i: (i, 0))],
        out_specs=[pl.BlockSpec((8, LANES), lambda i: (0, 0)), pl.BlockSpec((tm, d), lambda i: (i, 0)),
                   pl.BlockSpec((1, d), lambda i: (0, 0))],
        out_shape=[jax.ShapeDtypeStruct((8, LANES), F32), jax.ShapeDtypeStruct((n, d), F32),
                   jax.ShapeDtypeStruct((1, d), F32)], name=name,
        compiler_params=_cp(("arbitrary",)))(h, gamma, target)


def _halo_rows(dtype):
    return HALO * (4 // jnp.dtype(dtype).itemsize)


def _prev_halo_map(t, hb=HALO):
    return lambda i: (jnp.maximum(i * (t // hb) - 1, 0), 0)


def _next_halo_map(t, n, hb=HALO):
    return lambda i: (jnp.minimum((i + 1) * (t // hb), n // hb - 1), 0)


def _fill_prev(xs, main_ref, halo_ref, i, cols=slice(None)):
    hb = halo_ref.shape[0]
    xs[0:HALO, :] = jnp.where(i > 0, halo_ref[hb - HALO:hb, cols].astype(F32), 0.0)
    xs[HALO:, :] = main_ref[:, cols].astype(F32)


def _causal_conv(xs, w_ref, width, t, cols=slice(None)):
    x = xs[...]
    acc = w_ref[width - 1:width, cols] * x[HALO:, :]
    for k in range(width - 1):
        acc = acc + w_ref[k:k + 1, cols] * pltpu.roll(x, width - 1 - k, axis=0)[HALO:, :]
    return acc


def _col_chunks(width, target=COL_CHUNK):
    tc = _pick(width, target)
    return [slice(j * tc, (j + 1) * tc) for j in range(width // tc)]


def ffn_act_down(up, conv_w, conv_b, w_down, h, name):
    n, c2 = up.shape
    ff = c2 // 2
    d = h.shape[1]
    t = _pick(n, 2 * FFN_TILE, 8)
    hb = _halo_rows(up.dtype)
    chunks = _col_chunks(ff)
    tc = chunks[0].stop

    def body(up_ref, halo_ref, cw_ref, cb_ref, wd_ref, h_ref, o_ref, act_ref, hc_ref, xg, xv):
        i = pl.program_id(0)
        acc = h_ref[...]
        for cs in chunks:
            vs = slice(ff + cs.start, ff + cs.stop)
            _fill_prev(xg, up_ref, halo_ref, i, cs)
            _fill_prev(xv, up_ref, halo_ref, i, vs)
            gate = _causal_conv(xg, cw_ref, F_CONV, t, cs) + cb_ref[:, cs]
            val = _causal_conv(xv, cw_ref, F_CONV, t, vs) + cb_ref[:, vs]
            hc_ref[:, cs] = gate.astype(BF16)
            hc_ref[:, vs] = val.astype(BF16)
            act = (gate * _sig(gate) * val).astype(BF16)
            act_ref[:, cs] = act
            acc = acc + _dot(act, wd_ref[cs, :])
        o_ref[...] = acc

    return pl.pallas_call(
        body, grid=(n // t,),
        in_specs=[pl.BlockSpec((t, c2), lambda i: (i, 0)),
                  pl.BlockSpec((hb, c2), _prev_halo_map(t, hb)),
                  pl.BlockSpec((F_CONV, c2), lambda i: (0, 0)),
                  pl.BlockSpec((1, c2), lambda i: (0, 0)),
                  _resident((ff, d), lambda i: (0, 0)),
                  pl.BlockSpec((t, d), lambda i: (i, 0))],
        out_specs=[pl.BlockSpec((t, d), lambda i: (i, 0)), pl.BlockSpec((t, ff), lambda i: (i, 0)),
                   pl.BlockSpec((t, c2), lambda i: (i, 0))],
        out_shape=[jax.ShapeDtypeStruct((n, d), F32), jax.ShapeDtypeStruct((n, ff), BF16),
                   jax.ShapeDtypeStruct((n, c2), BF16)],
        scratch_shapes=[pltpu.VMEM((t + HALO, tc), F32), pltpu.VMEM((t + HALO, tc), F32)], name=name,
        compiler_params=_cp(("parallel",)))(up, up, conv_w, conv_b, w_down, h)


def ffn_bwd_act(dout, hc, w_down, name):
    n, c2 = hc.shape
    ff = c2 // 2
    d = dout.shape[1]
    t = _pick(n, 2 * FFN_TILE, 8)
    chunks = _col_chunks(ff)

    def body(dout_ref, hc_ref, wd_ref, dhc_ref, dcb_ref):
        i = pl.program_id(0)

        @pl.when(i == 0)
        def _():
            dcb_ref[...] = jnp.zeros_like(dcb_ref)
        doutb = dout_ref[...].astype(BF16)
        for cs in chunks:
            vs = slice(ff + cs.start, ff + cs.stop)
            gate = hc_ref[:, cs].astype(F32)
            val = hc_ref[:, vs].astype(F32)
            sg = _sig(gate)
            da = _dot_nt(doutb, wd_ref[cs, :])
            dgate = da * val * (sg * (1.0 + gate * (1.0 - sg)))
            dval = da * gate * sg
            dhc_ref[:, cs] = dgate.astype(BF16)
            dhc_ref[:, vs] = dval.astype(BF16)
            dcb_ref[:, cs] += jnp.sum(dgate, axis=0, keepdims=True)
            dcb_ref[:, vs] += jnp.sum(dval, axis=0, keepdims=True)

    return pl.pallas_call(
        body, grid=(n // t,),
        in_specs=[pl.BlockSpec((t, d), lambda i: (i, 0)),
                  pl.BlockSpec((t, c2), lambda i: (i, 0)),
                  _resident((ff, d), lambda i: (0, 0))],
        out_specs=[pl.BlockSpec((t, c2), lambda i: (i, 0)), pl.BlockSpec((1, c2), lambda i: (0, 0))],
        out_shape=[jax.ShapeDtypeStruct((n, c2), BF16), jax.ShapeDtypeStruct((1, c2), F32)], name=name,
        compiler_params=_cp(("arbitrary",)))(dout, hc, w_down)


def conv_bwd_tail(dy_ref, dnext_ref, x_ref, cw_ref, dcw_ref, ds, width, t, i, last, cols=slice(None)):
    ds[0:t, :] = dy_ref[:, cols].astype(F32)
    ds[t:, :] = jnp.where(i < last, dnext_ref[0:HALO, cols].astype(F32), 0.0)
    x = x_ref[:, cols].astype(F32)
    dall = ds[...]
    dx = None
    for k in range(width):
        off = width - 1 - k
        shifted = dall[0:t, :] if off == 0 else pltpu.roll(dall, t + HALO - off, axis=0)[0:t, :]
        term = cw_ref[k:k + 1, cols] * shifted
        dx = term if dx is None else dx + term
        dcw_ref[k:k + 1, cols] += jnp.sum(shifted * x, axis=0, keepdims=True)
    return dx


def ffn_bwd_up(dhc, up, conv_w, w_up, h, dout, gamma, name, rider=None):
    n, c2 = up.shape
    d = h.shape[1]
    t = _pick(n, FFN_TILE, 8)
    last = n // t - 1
    chunks = _col_chunks(c2)
    tc = chunks[0].stop

    def body(dhc_ref, dnext_ref, up_ref, cw_ref, wu_ref, h_ref, dout_ref, g_ref,
             dh_ref, dup_ref, dcw_ref, dg_ref, ds):
        i = pl.program_id(0)

        @pl.when(i == 0)
        def _():
            dcw_ref[...] = jnp.zeros_like(dcw_ref)
            dg_ref[...] = jnp.zeros_like(dg_ref)
        dxn = jnp.zeros((t, d), F32)
        for cs in chunks:
            dup = conv_bwd_tail(dhc_ref, dnext_ref, up_ref, cw_ref, dcw_ref, ds, F_CONV, t, i, last, cs)
            dupb = dup.astype(BF16)
            dup_ref[:, cs] = dupb
            dxn = dxn + _dot_nt(dupb, wu_ref[:, cs])
        dx, dg = _rms_bwd(h_ref[...], g_ref[...], dxn)
        dh_ref[...] = dout_ref[...] + dx
        dg_ref[...] += dg

    return host_call(
        body, grid=(n // t,), rider=rider, sem=("arbitrary",), args=(dhc, dhc, up, conv_w, w_up, h, dout, gamma),
        in_specs=[pl.BlockSpec((t, c2), lambda i: (i, 0)),
                  pl.BlockSpec((_halo_rows(dhc.dtype), c2), _next_halo_map(t, n, _halo_rows(dhc.dtype))),
                  pl.BlockSpec((t, c2), lambda i: (i, 0)),
                  pl.BlockSpec((F_CONV, c2), lambda i: (0, 0)),
                  _resident((d, c2), lambda i: (0, 0)),
                  pl.BlockSpec((t, d), lambda i: (i, 0)),
                  pl.BlockSpec((t, d), lambda i: (i, 0)),
                  pl.BlockSpec((1, d), lambda i: (0, 0))],
        out_specs=[pl.BlockSpec((t, d), lambda i: (i, 0)), pl.BlockSpec((t, c2), lambda i: (i, 0)),
                   pl.BlockSpec((F_CONV, c2), lambda i: (0, 0)), pl.BlockSpec((1, d), lambda i: (0, 0))],
        out_shape=[jax.ShapeDtypeStruct((n, d), F32), jax.ShapeDtypeStruct((n, c2), BF16),
                   jax.ShapeDtypeStruct((F_CONV, c2), F32), jax.ShapeDtypeStruct((1, d), F32)],
        scratch_shapes=[pltpu.VMEM((t + HALO, tc), F32)], name=name)


def _gdn_head(uq, uk, uv, pba, alog, dtb, head, n_heads):
    lane = lax.broadcasted_iota(jnp.int32, pba.shape, 1)
    sq = uq * _sig(uq)
    q = sq * lax.rsqrt(jnp.sum(sq * sq, axis=-1, keepdims=True) + EPS) * (A_HEAD ** -0.5)
    sk = uk * _sig(uk)
    k = sk * lax.rsqrt(jnp.sum(sk * sk, axis=-1, keepdims=True) + EPS)
    v = uv * _sig(uv)
    beta = jnp.sum(jnp.where(lane == head, _sig(pba), 0.0), axis=-1, keepdims=True)
    g_all = -jnp.exp(alog) * _softplus(pba + dtb)
    g = jnp.sum(jnp.where(lane == n_heads + head, g_all, 0.0), axis=-1, keepdims=True)
    return q, k, v, jnp.broadcast_to(beta, uq.shape), jnp.broadcast_to(g, uq.shape)


def gdn_prep(proj, conv_w, alog, dtb, n_heads, name, rider=None):
    n = proj.shape[0]
    qk = n_heads * A_HEAD
    cw = 3 * qk
    ba_blk = (cw + qk) // LANES
    t = _pick(n, 256, 8)

    def body(x_ref, halo_ref, pba_ref, cw_ref, al_ref, dt_ref, q_ref, k_ref, v_ref, b_ref, g_ref, u_ref, xs):
        i = pl.program_id(0)
        xs[0:HALO, :] = jnp.where(i > 0, halo_ref[...], 0.0)
        xs[HALO:, :] = x_ref[...]
        u = _causal_conv(xs, cw_ref, A_CONV, t)
        u_ref[...] = u.astype(BF16)
        pba = pba_ref[...]
        for hd in range(n_heads):
            s0 = slice(hd * A_HEAD, (hd + 1) * A_HEAD)
            s1 = slice(qk + hd * A_HEAD, qk + (hd + 1) * A_HEAD)
            s2 = slice(2 * qk + hd * A_HEAD, 2 * qk + (hd + 1) * A_HEAD)
            q, k, v, bb, gb = _gdn_head(u[:, s0], u[:, s1], u[:, s2], pba, al_ref[...], dt_ref[...], hd, n_heads)
            q_ref[:, s0] = q
            k_ref[:, s0] = k
            v_ref[:, s0] = v
            b_ref[:, s0] = bb
            g_ref[:, s0] = gb

    out = jax.ShapeDtypeStruct((n, qk), F32)
    return host_call(
        body, grid=(n // t,),
        in_specs=[pl.BlockSpec((t, cw), lambda i: (i, 0)),
                  pl.BlockSpec((HALO, cw), _prev_halo_map(t)),
                  pl.BlockSpec((t, LANES), lambda i: (i, ba_blk)),
                  pl.BlockSpec((A_CONV, cw), lambda i: (0, 0)),
                  pl.BlockSpec((1, LANES), lambda i: (0, 0)),
                  pl.BlockSpec((1, LANES), lambda i: (0, 0))],
        out_specs=[pl.BlockSpec((t, qk), lambda i: (i, 0))] * 5 + [pl.BlockSpec((t, cw), lambda i: (i, 0))],
        out_shape=[out] * 5 + [jax.ShapeDtypeStruct((n, cw), BF16)],
        scratch_shapes=[pltpu.VMEM((t + HALO, cw), F32)], name=name,
        sem=("parallel",), args=(proj, proj, proj, conv_w, alog, dtb), rider=rider)


def gdn_prep_bwd(proj, u, alog, dtb, dq, dk, dv, dbb, dgb, n_heads, name):
    n = proj.shape[0]
    qk = n_heads * A_HEAD
    cw = 3 * qk
    ba_blk = (cw + qk) // LANES
    t = _pick(n, 256, 8)

    def body(u_ref, pba_ref, al_ref, dt_ref, dq_ref, dk_ref, dv_ref, dbb_ref, dgb_ref,
             du_ref, dba_ref, dal_ref, ddt_ref):
        i = pl.program_id(0)
        u = u_ref[...].astype(F32)
        pba = pba_ref[...]
        lane0 = lax.broadcasted_iota(jnp.int32, (t, A_HEAD), 1) == 0
        dba = jnp.zeros((t, LANES), F32)
        dal = jnp.zeros((1, LANES), F32)
        ddt = jnp.zeros((1, LANES), F32)
        for hd in range(n_heads):
            s0 = slice(hd * A_HEAD, (hd + 1) * A_HEAD)
            s1 = slice(qk + hd * A_HEAD, qk + (hd + 1) * A_HEAD)
            s2 = slice(2 * qk + hd * A_HEAD, 2 * qk + (hd + 1) * A_HEAD)
            fn = functools.partial(_gdn_head, head=hd, n_heads=n_heads)
            _, vjp = jax.vjp(fn, u[:, s0], u[:, s1], u[:, s2], pba, al_ref[...], dt_ref[...])
            cts = (dq_ref[:, s0], dk_ref[:, s0], dv_ref[:, s0],
                   jnp.where(lane0, dbb_ref[:, s0], 0.0), jnp.where(lane0, dgb_ref[:, s0], 0.0))
            duq, duk, duv, dpba, da, dd = vjp(cts)
            du_ref[:, s0] = duq
            du_ref[:, s1] = duk
            du_ref[:, s2] = duv
            dba = dba + dpba
            dal = dal + da
            ddt = ddt + dd
        dba_ref[...] = dba

        @pl.when(i == 0)
        def _():
            dal_ref[...] = jnp.zeros_like(dal_ref)
            ddt_ref[...] = jnp.zeros_like(ddt_ref)
        dal_ref[...] += dal
        ddt_ref[...] += ddt

    tok = pl.BlockSpec((t, qk), lambda i: (i, 0))
    row = pl.BlockSpec((1, LANES), lambda i: (0, 0))
    return pl.pallas_call(
        body, grid=(n // t,),
        in_specs=[pl.BlockSpec((t, cw), lambda i: (i, 0)),
                  pl.BlockSpec((t, LANES), lambda i: (i, ba_blk)), row, row,
                  tok, tok, tok, tok, tok],
        out_specs=[pl.BlockSpec((t, cw), lambda i: (i, 0)), pl.BlockSpec((t, LANES), lambda i: (i, 0)), row, row],
        out_shape=[jax.ShapeDtypeStruct((n, cw), F32), jax.ShapeDtypeStruct((n, LANES), F32),
                   jax.ShapeDtypeStruct((1, LANES), F32), jax.ShapeDtypeStruct((1, LANES), F32)],
        name=name, compiler_params=_cp(("arbitrary",)))(u, proj, alog, dtb, dq, dk, dv, dbb, dgb)


def conv_bwd(du, x, conv_w, width, name, rider=None):
    n, cw = du.shape
    t = _pick(n, 256, 8)
    last = n // t - 1

    chunks = _col_chunks(cw)
    tc = chunks[0].stop

    def body(du_ref, dnext_ref, x_ref, cw_ref, dx_ref, dcw_ref, ds):
        i = pl.program_id(0)

        @pl.when(i == 0)
        def _():
            dcw_ref[...] = jnp.zeros_like(dcw_ref)
        for cs in chunks:
            dx_ref[:, cs] = conv_bwd_tail(du_ref, dnext_ref, x_ref, cw_ref, dcw_ref, ds, width, t, i, last, cs)

    return host_call(
        body, grid=(n // t,),
        in_specs=[pl.BlockSpec((t, cw), lambda i: (i, 0)),
                  pl.BlockSpec((HALO, cw), _next_halo_map(t, n)),
                  pl.BlockSpec((t, cw), lambda i: (i, 0)),
                  pl.BlockSpec((width, cw), lambda i: (0, 0))],
        out_specs=[pl.BlockSpec((t, cw), lambda i: (i, 0)), pl.BlockSpec((width, cw), lambda i: (0, 0))],
        out_shape=[jax.ShapeDtypeStruct((n, cw), F32), jax.ShapeDtypeStruct((width, cw), F32)],
        scratch_shapes=[pltpu.VMEM((t + HALO, tc), F32)], name=name,
        sem=("arbitrary",), args=(du, du, x, conv_w), rider=rider)


def _b(x):
    return x.astype(BF16)


def _mm_nn(a, b):
    return _dot(_b(a), _b(b))


def _mm_nt(a, b):
    return _dot_nt(_b(a), _b(b))


def _mm_tn(a, b):
    return _dot_tn(_b(a), _b(b))


@jax.custom_vjp
def _mmg_nn(a, b):
    return _mm_nn(a, b)


_mmg_nn.defvjp(lambda a, b: (_mm_nn(a, b), (a, b)),
               lambda res, dc: (_mm_nt(dc, res[1]), _mm_tn(res[0], dc)))


@jax.custom_vjp
def _mmg_nt(a, b):
    return _mm_nt(a, b)


_mmg_nt.defvjp(lambda a, b: (_mm_nt(a, b), (a, b)),
               lambda res, dc: (_mm_nn(dc, res[1]), _mm_tn(dc, res[0])))


@jax.custom_vjp
def _mmg_tn(a, b):
    return _mm_tn(a, b)


_mmg_tn.defvjp(lambda a, b: (_mm_tn(a, b), (a, b)),
               lambda res, dc: (_mm_nt(res[1], dc), _mm_nn(res[0], dc)))


def _each(f, *lists):
    return [f(*a) for a in zip(*lists)]


def _unit_lower_inv(ms):
    c = ms[0].shape[0]
    eye = jnp.where(lax.broadcasted_iota(jnp.int32, (c, c), 0) == lax.broadcasted_iota(jnp.int32, (c, c), 1), 1.0, 0.0)
    xs = [eye - m for m in ms]
    pws = _each(_mm_nn, ms, ms)
    for it in range(5):
        xs = _each(lambda x, pw: x + _mm_nn(x, pw), xs, pws)
        if it < 4:
            pws = _each(_mm_nn, pws, pws)
    rs = _each(lambda m, x: eye - x - _hdot(m, x), ms, xs)
    return _each(lambda x, r: x + _mm_nn(x, r), xs, rs)


@jax.custom_vjp
def _saved_inv_g(ms, xs):
    return xs


_saved_inv_g.defvjp(lambda ms, xs: (xs, xs),
                    lambda xs, dxs: (_each(lambda t, x: -_mm_nt(t, x), _each(_mm_tn, xs, dxs), xs),
                                     [jnp.zeros_like(x) for x in xs]))


def _gdn_chunk(ops, state, q, k, v, bb, gb):
    nn, nt, tn, inv = ops
    c = CHUNK
    ri = lax.broadcasted_iota(jnp.int32, (c, c), 0)
    ci = lax.broadcasted_iota(jnp.int32, (c, c), 1)
    causal = ri >= ci
    strict = ri > ci
    tri = jnp.where(causal, 1.0, 0.0)
    gc = [_hdot(tri, g) for g in gb]
    decay = [jnp.where(causal, jnp.exp(jnp.where(causal, x[:, :c] - x.T[:c, :], 0.0)), 0.0) for x in gc]
    kb = _each(lambda a, b: a * b, k, bb)
    kk = _each(nt, kb, k)
    m = _each(lambda a, d: jnp.where(strict, a * d, 0.0), kk, decay)
    tinv = inv(m)
    egc = [jnp.exp(x) for x in gc]
    u = _each(nn, tinv, _each(lambda a, b: a * b, v, bb))
    w = _each(nn, tinv, _each(lambda a, b: a * b, kb, egc))
    attn = _each(lambda a, d: a * d, _each(nt, q, k), decay)
    glast = [jnp.sum(g, axis=0, keepdims=True) for g in gb]
    ws = _each(nn, w, state)
    v_new = _each(lambda a, b: a - b, u, ws)
    qs = _each(nn, _each(lambda a, b: a * b, q, egc), state)
    av = _each(nn, attn, v_new)
    o = _each(lambda a, b: a + b, qs, av)
    kv = _each(tn, _each(lambda a, gl, x: a * jnp.exp(gl - x), k, glast, gc), v_new)
    new_state = _each(lambda s, gl, a: s * jnp.exp(gl) + a, state, glast, kv)
    return o, new_state


def gdn_fwd(q, k, v, bb, gb, n_heads, name, rider=None):
    n, w = q.shape
    nc = n // CHUNK
    cb = min(8, nc)
    rows = cb * CHUNK

    def body(q_ref, k_ref, v_ref, b_ref, g_ref, o_ref, st_ref, ti_ref, s_scr):
        @pl.when(pl.program_id(0) == 0)
        def _():
            s_scr[...] = jnp.zeros_like(s_scr)

        def step(c, carry):
            sl = pl.ds(pl.multiple_of(c * CHUNK, CHUNK), CHUNK)
            lanes = [slice(hd * A_HEAD, (hd + 1) * A_HEAD) for hd in range(n_heads)]
            state = [s_scr[hd] for hd in range(n_heads)]
            inverses = []

            def inv(ms):
                inverses.extend(_unit_lower_inv(ms))
                return inverses

            o, new_state = _gdn_chunk((_mm_nn, _mm_nt, _mm_tn, inv), state,
                                      *[[r[sl, ls] for ls in lanes] for r in (q_ref, k_ref, v_ref, b_ref, g_ref)])
            for hd, ls in enumerate(lanes):
                st_ref[hd, pl.ds(c, 1)] = state[hd][None]
                ti_ref[hd, pl.ds(c, 1)] = inverses[hd].astype(BF16)[None]
                o_ref[sl, ls] = o[hd]
                s_scr[hd] = new_state[hd]
            return carry

        lax.fori_loop(0, cb, step, 0)

    tok = pl.BlockSpec((rows, w), lambda j: (j, 0))
    return host_call(
        body, grid=(nc // cb,),
        in_specs=[tok] * 5,
        out_specs=[tok, pl.BlockSpec((n_heads, cb, A_HEAD, A_HEAD), lambda j: (0, j, 0, 0)),
                   pl.BlockSpec((n_heads, cb, CHUNK, CHUNK), lambda j: (0, j, 0, 0))],
        out_shape=[jax.ShapeDtypeStruct(q.shape, F32), jax.ShapeDtypeStruct((n_heads, nc, A_HEAD, A_HEAD), F32),
                   jax.ShapeDtypeStruct((n_heads, nc, CHUNK, CHUNK), BF16)],
        scratch_shapes=[pltpu.VMEM((n_heads, A_HEAD, A_HEAD), F32)], name=name,
        sem=("arbitrary",), args=(q, k, v, bb, gb), rider=rider)


def gdn_bwd(q, k, v, bb, gb, states, tinv, do, n_heads, name, rider=None):
    n, w = q.shape
    nc = n // CHUNK
    cb = min(4, nc)
    rows = cb * CHUNK
    nblk = nc // cb

    def body(q_ref, k_ref, v_ref, b_ref, g_ref, st_ref, ti_ref, do_ref,
             dq_ref, dk_ref, dv_ref, db_ref, dg_ref, ds_scr):
        @pl.when(pl.program_id(0) == 0)
        def _():
            ds_scr[...] = jnp.zeros_like(ds_scr)

        def step(s, carry):
            c = cb - 1 - s
            sl = pl.ds(pl.multiple_of(c * CHUNK, CHUNK), CHUNK)
            for h0 in range(0, n_heads, GDN_BWD_HEADS):
                heads = list(range(h0, min(h0 + GDN_BWD_HEADS, n_heads)))
                lanes = [slice(hd * A_HEAD, (hd + 1) * A_HEAD) for hd in heads]
                state = [st_ref[hd, pl.ds(c, 1)][0] for hd in heads]
                saved = [ti_ref[hd, pl.ds(c, 1)][0].astype(F32) for hd in heads]
                chunk_fn = functools.partial(_gdn_chunk, (_mmg_nn, _mmg_nt, _mmg_tn, lambda ms: _saved_inv_g(ms, saved)))
                _, vjp = jax.vjp(chunk_fn, state, *[[r[sl, ls] for ls in lanes]
                                                    for r in (q_ref, k_ref, v_ref, b_ref, g_ref)])
                dstate, dq, dk, dv, dbb, dgb = vjp(([do_ref[sl, ls] for ls in lanes], [ds_scr[hd] for hd in heads]))
                for u, (hd, ls) in enumerate(zip(heads, lanes)):
                    ds_scr[hd] = dstate[u]
                    dq_ref[sl, ls] = dq[u]
                    dk_ref[sl, ls] = dk[u]
                    dv_ref[sl, ls] = dv[u]
                    db_ref[sl, ls] = jnp.broadcast_to(jnp.sum(dbb[u], axis=-1, keepdims=True), dbb[u].shape)
                    dg_ref[sl, ls] = jnp.broadcast_to(jnp.sum(dgb[u], axis=-1, keepdims=True), dgb[u].shape)
            return carry

        lax.fori_loop(0, cb, step, 0)

    tok = pl.BlockSpec((rows, w), lambda j: (nblk - 1 - j, 0))
    out = jax.ShapeDtypeStruct(q.shape, F32)
    return host_call(
        body, grid=(nblk,),
        in_specs=[tok] * 5 + [pl.BlockSpec((n_heads, cb, A_HEAD, A_HEAD), lambda j: (0, nblk - 1 - j, 0, 0)),
                              pl.BlockSpec((n_heads, cb, CHUNK, CHUNK), lambda j: (0, nblk - 1 - j, 0, 0)), tok],
        out_specs=[tok] * 5, out_shape=[out] * 5,
        scratch_shapes=[pltpu.VMEM((n_heads, A_HEAD, A_HEAD), F32)], name=name,
        sem=("arbitrary",), args=(q, k, v, bb, gb, states, tinv, do), rider=rider)


def _gdn_gate(oh, zh, w):
    r = lax.rsqrt(jnp.mean(oh * oh, axis=-1, keepdims=True) + EPS)
    return oh * r * w * (zh * _sig(zh))


def gdn_out(o, proj, out_norm, w_out, h, n_heads, name):
    n, vw = o.shape
    d = h.shape[1]
    z_blk = 3 * vw // vw
    t = _pick(n, 512, 8)

    def body(o_ref, z_ref, w_ref, wo_ref, h_ref, out_ref, y_ref):
        for hd in range(n_heads):
            s0 = slice(hd * A_HEAD, (hd + 1) * A_HEAD)
            y_ref[:, s0] = _gdn_gate(o_ref[:, s0], z_ref[:, s0], w_ref[...]).astype(BF16)
        out_ref[...] = h_ref[...] + _dot(y_ref[...], wo_ref[...])

    return pl.pallas_call(
        body, grid=(n // t,),
        in_specs=[pl.BlockSpec((t, vw), lambda i: (i, 0)),
                  pl.BlockSpec((t, vw), lambda i: (i, z_blk)),
                  pl.BlockSpec((1, A_HEAD), lambda i: (0, 0)),
                  _resident((vw, d), lambda i: (0, 0)),
                  pl.BlockSpec((t, d), lambda i: (i, 0))],
        out_specs=[pl.BlockSpec((t, d), lambda i: (i, 0)), pl.BlockSpec((t, vw), lambda i: (i, 0))],
        out_shape=[jax.ShapeDtypeStruct((n, d), F32), jax.ShapeDtypeStruct((n, vw), BF16)], name=name,
        compiler_params=_cp(("parallel",)))(o, proj, out_norm, w_out, h)


def gdn_out_bwd(dout, o, proj, out_norm, w_out, n_heads, name):
    n, vw = o.shape
    d = dout.shape[1]
    z_blk = 3
    t = _pick(n, 512, 8)

    def body(dout_ref, o_ref, z_ref, w_ref, wo_ref, do_ref, dz_ref, dw_ref):
        dy = _dot_nt(dout_ref[...].astype(BF16), wo_ref[...])
        dw = jnp.zeros((1, A_HEAD), F32)
        for hd in range(n_heads):
            s0 = slice(hd * A_HEAD, (hd + 1) * A_HEAD)
            _, vjp = jax.vjp(_gdn_gate, o_ref[:, s0], z_ref[:, s0], w_ref[...])
            doh, dzh, dwh = vjp(dy[:, s0])
            do_ref[:, s0] = doh
            dz_ref[:, s0] = dzh
            dw = dw + dwh

        @pl.when(pl.program_id(0) == 0)
        def _():
            dw_ref[...] = jnp.zeros_like(dw_ref)
        dw_ref[...] += dw

    tok = pl.BlockSpec((t, vw), lambda i: (i, 0))
    return pl.pallas_call(
        body, grid=(n // t,),
        in_specs=[pl.BlockSpec((t, d), lambda i: (i, 0)), tok,
                  pl.BlockSpec((t, vw), lambda i: (i, z_blk)),
                  pl.BlockSpec((1, A_HEAD), lambda i: (0, 0)),
                  _resident((vw, d), lambda i: (0, 0))],
        out_specs=[tok, tok, pl.BlockSpec((1, A_HEAD), lambda i: (0, 0))],
        out_shape=[jax.ShapeDtypeStruct((n, vw), F32), jax.ShapeDtypeStruct((n, vw), F32),
                   jax.ShapeDtypeStruct((1, A_HEAD), F32)], name=name,
        compiler_params=_cp(("arbitrary",)))(dout, o, proj, out_norm, w_out)


BIAS_LINE = 768
BIAS_TOP = BAND + CHUNK - 2


def _bias_line_onehot(clip, tbl_pad):
    r = lax.broadcasted_iota(jnp.int32, (tbl_pad, BIAS_LINE), 0)
    v = lax.broadcasted_iota(jnp.int32, (tbl_pad, BIAS_LINE), 1)
    idx = jnp.clip(BIAS_TOP - v - (CHUNK - 1), -clip, clip) + clip
    return jnp.where((r == idx) & (v <= BIAS_TOP), 1.0, 0.0)


def bias_expand(tbl, clip, name):
    nh, tp = tbl.shape

    def body(t_ref, o_ref):
        line = _hdot(t_ref[...], _bias_line_onehot(clip, tp))
        keep = lax.broadcasted_iota(jnp.int32, (nh, BAND_PAD), 1) < BAND
        for i in range(CHUNK):
            s = CHUNK - 1 - i
            rolled = line if s == 0 else pltpu.roll(line, BIAS_LINE - s, axis=1)
            o_ref[i] = jnp.where(keep, rolled[:, :BAND_PAD], 0.0)

    return pl.pallas_call(
        body, in_specs=[pl.BlockSpec(memory_space=pltpu.VMEM)], out_specs=pl.BlockSpec(memory_space=pltpu.VMEM),
        out_shape=jax.ShapeDtypeStruct((CHUNK, nh, BAND_PAD), F32), name=name, compiler_params=_cp())(tbl)


def bias_expand_bwd(dbias, clip, tp, name):
    _, nh, _ = dbias.shape

    def body(d_ref, o_ref):
        keep = lax.broadcasted_iota(jnp.int32, (nh, BAND_PAD), 1) < BAND
        pad = jnp.zeros((nh, BIAS_LINE - BAND_PAD), F32)
        acc = jnp.zeros((nh, BIAS_LINE), F32)
        for i in range(CHUNK):
            s = CHUNK - 1 - i
            d = jnp.concatenate([jnp.where(keep, d_ref[i], 0.0), pad], axis=1)
            acc = acc + (d if s == 0 else pltpu.roll(d, s, axis=1))
        o_ref[...] = _hdot_nt(acc, _bias_line_onehot(clip, tp))

    return pl.pallas_call(
        body, in_specs=[pl.BlockSpec(memory_space=pltpu.VMEM)], out_specs=pl.BlockSpec(memory_space=pltpu.VMEM),
        out_shape=jax.ShapeDtypeStruct((nh, tp), F32), name=name, compiler_params=_cp())(dbias)


ATT_TILE = LEFT_CHUNKS * CHUNK


ATT_GROUP = 8


def _att_softmax(s, bias, n_chunk):
    s = s * (B_HEAD ** -0.5) + bias
    slot = lax.broadcasted_iota(jnp.int32, s.shape, 1)
    valid = (slot >= (LEFT_CHUNKS - n_chunk) * CHUNK) & (slot < BAND)
    s = jnp.where(valid, s, NEG_INF)
    p = jnp.exp(s - jnp.max(s, axis=-1, keepdims=True))
    return p / jnp.sum(p, axis=-1, keepdims=True)


def _att_specs(n_pairs):
    prev = lambda p, i: (jnp.maximum(i - 1, 0), p)
    cur = lambda p, i: (i, p)
    prev_v = lambda p, i: (jnp.maximum(i - 1, 0), n_pairs + p)
    cur_v = lambda p, i: (i, n_pairs + p)
    blk = (ATT_TILE, LANES)
    return [pl.BlockSpec(blk, prev), pl.BlockSpec(blk, cur), pl.BlockSpec(blk, prev_v), pl.BlockSpec(blk, cur_v)]


def _att_fill(kbuf, vbuf, kp_ref, kc_ref, vp_ref, vc_ref):
    t = ATT_TILE
    kbuf[0:t, :] = kp_ref[...].astype(BF16)
    kbuf[t:2 * t, :] = kc_ref[...].astype(BF16)
    kbuf[2 * t:, :] = jnp.zeros((CHUNK, LANES), BF16)
    vbuf[0:t, :] = vp_ref[...].astype(BF16)
    vbuf[t:2 * t, :] = vc_ref[...].astype(BF16)
    vbuf[2 * t:, :] = jnp.zeros((CHUNK, LANES), BF16)


def _stack_heads(x, first):
    return jnp.concatenate([jnp.where(first, x, 0.0), jnp.where(first, 0.0, x)], axis=0).astype(BF16)


def attn_fwd(qp, kv, bias, name, rider=None):
    n, bw = qp.shape
    n_pairs = bw // LANES
    t = ATT_TILE
    cpt = t // CHUNK

    def body(q_ref, kp_ref, kc_ref, vp_ref, vc_ref, b_ref, o_ref, kbuf, vbuf):
        i = pl.program_id(1)
        _att_fill(kbuf, vbuf, kp_ref, kc_ref, vp_ref, vc_ref)
        lane = lax.broadcasted_iota(jnp.int32, (CHUNK, LANES), 1)
        first = lane < B_HEAD
        for g0 in range(0, cpt, ATT_GROUP):
            chunks = list(range(g0, min(g0 + ATT_GROUP, cpt)))
            band = [slice(c * CHUNK, c * CHUNK + BAND_PAD) for c in chunks]
            q2 = [_stack_heads(q_ref[c * CHUNK:(c + 1) * CHUNK, :], first) for c in chunks]
            s = [_dot_nt(q_u, kbuf[b_u, :]) for q_u, b_u in zip(q2, band)]
            p = [_att_softmax(s_u, b_ref[0], i * cpt + c) for s_u, c in zip(s, chunks)]
            o = [_dot(p_u.astype(BF16), vbuf[b_u, :]) for p_u, b_u in zip(p, band)]
            for o_u, c in zip(o, chunks):
                o_ref[c * CHUNK:(c + 1) * CHUNK, :] = jnp.where(first, o_u[:CHUNK], o_u[CHUNK:])

    return host_call(
        body, grid=(n_pairs, n // t),
        in_specs=[pl.BlockSpec((t, LANES), lambda p, i: (i, p))] + _att_specs(n_pairs)
        + [pl.BlockSpec((1, 2 * CHUNK, BAND_PAD), lambda p, i: (p, 0, 0))],
        out_specs=[pl.BlockSpec((t, LANES), lambda p, i: (i, p))],
        out_shape=[jax.ShapeDtypeStruct((n, bw), F32)],
        scratch_shapes=[pltpu.VMEM((2 * t + CHUNK, LANES), BF16), pltpu.VMEM((2 * t + CHUNK, LANES), BF16)],
        name=name, sem=("parallel", "parallel"), args=(qp, kv, kv, kv, kv, bias), rider=rider)


def attn_bwd(qp, kv, bias, d_o, dk_in, dv_in, name, rider=None):
    n, bw = qp.shape
    n_pairs = bw // LANES
    t = ATT_TILE
    cpt = t // CHUNK
    nt = n // t
    have_in = dk_in is not None
    scale = B_HEAD ** -0.5

    def body(*refs):
        q_ref, kp_ref, kc_ref, vp_ref, vc_ref, b_ref, do_ref = refs[:7]
        pos = 7
        if have_in:
            dki_ref, dvi_ref = refs[7:9]
            pos = 9
        dq_ref, dk_ref, dv_ref, db_ref, kbuf, vbuf, dkacc, dvacc = refs[pos:]
        j = pl.program_id(1)
        i = nt - 1 - j
        _att_fill(kbuf, vbuf, kp_ref, kc_ref, vp_ref, vc_ref)

        @pl.when(j == 0)
        def _():
            dkacc[...] = jnp.zeros_like(dkacc)
            dvacc[...] = jnp.zeros_like(dvacc)
            db_ref[...] = jnp.zeros_like(db_ref)

        @pl.when(j > 0)
        def _():
            dkacc[t:2 * t, :] = dkacc[0:t, :]
            dvacc[t:2 * t, :] = dvacc[0:t, :]
            dkacc[0:t, :] = jnp.zeros((t, LANES), F32)
            dvacc[0:t, :] = jnp.zeros((t, LANES), F32)

        lane = lax.broadcasted_iota(jnp.int32, (CHUNK, LANES), 1)
        first = lane < B_HEAD
        for g0 in range(0, cpt, ATT_GROUP):
            chunks = list(range(g0, min(g0 + ATT_GROUP, cpt)))
            rows = [slice(c * CHUNK, (c + 1) * CHUNK) for c in chunks]
            band = [slice(c * CHUNK, c * CHUNK + BAND_PAD) for c in chunks]
            q2 = [_stack_heads(q_ref[r, :], first) for r in rows]
            do2 = [_stack_heads(do_ref[r, :], first) for r in rows]
            s = [_dot_nt(q_u, kbuf[b_u, :]) for q_u, b_u in zip(q2, band)]
            dp = [_dot_nt(d_u, vbuf[b_u, :]) for d_u, b_u in zip(do2, band)]
            p = [_att_softmax(s_u, b_ref[0], i * cpt + c) for s_u, c in zip(s, chunks)]
            ds = [p_u * (dp_u - jnp.sum(dp_u * p_u, axis=-1, keepdims=True)) for p_u, dp_u in zip(p, dp)]
            dsb = [(d_u * scale).astype(BF16) for d_u in ds]
            dv = [_dot_tn(p_u.astype(BF16), d_u) for p_u, d_u in zip(p, do2)]
            dq = [_dot(d_u, kbuf[b_u, :]) for d_u, b_u in zip(dsb, band)]
            dk = [_dot_tn(d_u, q_u) for d_u, q_u in zip(dsb, q2)]
            db_ref[0] += functools.reduce(lambda a, b: a + b, ds)
            for r in range(chunks[0], chunks[-1] + BAND // CHUNK):
                terms = [(u, r - c) for u, c in enumerate(chunks) if 0 <= r - c < BAND // CHUNK]
                blk = slice(r * CHUNK, (r + 1) * CHUNK)
                dvacc[blk, :] += functools.reduce(lambda a, b: a + b, [dv[u][o * CHUNK:(o + 1) * CHUNK] for u, o in terms])
                dkacc[blk, :] += functools.reduce(lambda a, b: a + b, [dk[u][o * CHUNK:(o + 1) * CHUNK] for u, o in terms])
            for u in range(len(chunks)):
                dq_ref[rows[u], :] = jnp.where(first, dq[u][:CHUNK], dq[u][CHUNK:])

        if have_in:
            dk_ref[...] = dkacc[t:2 * t, :] + dki_ref[...]
            dv_ref[...] = dvacc[t:2 * t, :] + dvi_ref[...]
        else:
            dk_ref[...] = dkacc[t:2 * t, :]
            dv_ref[...] = dvacc[t:2 * t, :]

    rev = lambda p, j: (nt - 1 - j, p)
    tok = pl.BlockSpec((t, LANES), rev)
    kv_specs = [pl.BlockSpec((t, LANES), lambda p, j: (jnp.maximum(nt - 2 - j, 0), p)),
                pl.BlockSpec((t, LANES), rev),
                pl.BlockSpec((t, LANES), lambda p, j: (jnp.maximum(nt - 2 - j, 0), n_pairs + p)),
                pl.BlockSpec((t, LANES), lambda p, j: (nt - 1 - j, n_pairs + p))]
    in_specs = [tok] + kv_specs + [pl.BlockSpec((1, 2 * CHUNK, BAND_PAD), lambda p, j: (p, 0, 0)), tok]
    args = [qp, kv, kv, kv, kv, bias, d_o]
    if have_in:
        in_specs += [tok, tok]
        args += [dk_in, dv_in]
    out = jax.ShapeDtypeStruct((n, bw), F32)
    return host_call(
        body, grid=(n_pairs, nt), in_specs=in_specs,
        out_specs=[tok, tok, tok, pl.BlockSpec((1, 2 * CHUNK, BAND_PAD), lambda p, j: (p, 0, 0))],
        out_shape=[out, out, out, jax.ShapeDtypeStruct((n_pairs, 2 * CHUNK, BAND_PAD), F32)],
        scratch_shapes=[pltpu.VMEM((2 * t + CHUNK, LANES), BF16), pltpu.VMEM((2 * t + CHUNK, LANES), BF16),
                        pltpu.VMEM((2 * t + CHUNK, LANES), F32), pltpu.VMEM((2 * t + CHUNK, LANES), F32)],
        name=name, sem=("parallel", "arbitrary"), args=args, rider=rider)


def adamw(w, gstack, m, v, name):
    r, c = w.shape
    s = gstack.shape[0]
    tr = _pick(r, 512, 8)

    def body(w_ref, g_ref, m_ref, v_ref, go_ref, d_ref, mo_ref, vo_ref):
        g = g_ref[0].astype(F32)
        for k in range(1, s):
            g = g + g_ref[k].astype(F32)
        mn = ADAM_B1 * m_ref[...] + (1.0 - ADAM_B1) * g
        vn = ADAM_B2 * v_ref[...] + (1.0 - ADAM_B2) * (g * g)
        m_hat = mn / (1.0 - ADAM_B1 ** ADAM_STEP)
        v_hat = vn / (1.0 - ADAM_B2 ** ADAM_STEP)
        go_ref[...] = g
        d_ref[...] = -ADAM_LR * (m_hat / (jnp.sqrt(v_hat) + ADAM_EPS) + ADAM_WD * w_ref[...])
        mo_ref[...] = mn
        vo_ref[...] = vn

    blk = pl.BlockSpec((tr, c), lambda i: (i, 0))
    out = jax.ShapeDtypeStruct((r, c), F32)
    return pl.pallas_call(
        body, grid=(r // tr,),
        in_specs=[blk, pl.BlockSpec((s, tr, c), lambda i: (0, i, 0)), blk, blk],
        out_specs=[blk] * 4, out_shape=[out] * 4, name=name,
        compiler_params=_cp(("parallel",)))(w, gstack, m, v)


def _place():
    x, y, c = lax.axis_index("x"), lax.axis_index("y"), lax.axis_index("c")
    chips = [(1 - x, y), (x, 1 - y), (1 - x, 1 - y)]
    return x, y, c, chips


def _ag_copy(outs, send_sems, recv_sems, t, k, block, to, src=None):
    def slot(dev):
        return outs[t].at[4 * dev[0] + 2 * dev[1] + dev[2]]
    return pltpu.make_async_remote_copy(
        src_ref=slot(block) if src is None else src, dst_ref=slot(block),
        send_sem=send_sems.at[7 * t + k], recv_sem=recv_sems.at[7 * t + k], device_id=to, device_id_type=MESH)


def _ag_start(ins, outs, send_sems, recv_sems, local_sems):
    x, y, c, chips = _place()
    me = (x, y, c)
    for t in range(len(ins)):
        pltpu.make_async_copy(ins[t], outs[t].at[4 * x + 2 * y + c], local_sems.at[t]).start()
        _ag_copy(outs, send_sems, recv_sems, t, 0, me, (x, y, 1 - c), src=ins[t]).start()
        for j, chip in enumerate(chips):
            _ag_copy(outs, send_sems, recv_sems, t, 1 + j, me, (*chip, c), src=ins[t]).start()


def _ag_finish(ins, outs, send_sems, recv_sems, local_sems):
    x, y, c, chips = _place()
    me, sibling = (x, y, c), (x, y, 1 - c)
    nt = len(ins)
    for t in range(nt):
        for j, chip in enumerate(chips):
            _ag_copy(outs, send_sems, recv_sems, t, 1 + j, (*chip, c), me).wait_recv()
            _ag_copy(outs, send_sems, recv_sems, t, 4 + j, (*chip, c), sibling).start()
    for t in range(nt):
        _ag_copy(outs, send_sems, recv_sems, t, 0, sibling, me).wait_recv()
        for j, chip in enumerate(chips):
            _ag_copy(outs, send_sems, recv_sems, t, 4 + j, (*chip, 1 - c), me).wait_recv()
    for t in range(nt):
        _ag_copy(outs, send_sems, recv_sems, t, 0, me, sibling, src=ins[t]).wait_send()
        for j, chip in enumerate(chips):
            _ag_copy(outs, send_sems, recv_sems, t, 1 + j, me, (*chip, c), src=ins[t]).wait_send()
            _ag_copy(outs, send_sems, recv_sems, t, 4 + j, (*chip, c), sibling).wait_send()
        pltpu.make_async_copy(ins[t], outs[t].at[4 * x + 2 * y + c], local_sems.at[t]).wait()


def _rs_a_copy(ins, outs, send_sems, recv_sems, t, q):
    x, y, c, _ = _place()
    return pltpu.make_async_remote_copy(
        src_ref=ins[t].at[2 * q + (1 - c)], dst_ref=outs[t].at[q],
        send_sem=send_sems.at[4 * t + q], recv_sem=recv_sems.at[4 * t + q],
        device_id=(x, y, 1 - c), device_id_type=MESH)


def _rs_a_start(ins, outs, send_sems, recv_sems, local_sems):
    for t in range(len(ins)):
        for q in range(4):
            _rs_a_copy(ins, outs, send_sems, recv_sems, t, q).start()


def _rs_a_finish(ins, outs, send_sems, recv_sems, local_sems):
    for t in range(len(ins)):
        for q in range(4):
            _rs_a_copy(ins, outs, send_sems, recv_sems, t, q).wait_recv()
    for t in range(len(ins)):
        for q in range(4):
            _rs_a_copy(ins, outs, send_sems, recv_sems, t, q).wait_send()


def _rs_b_copy(ins, outs, send_sems, recv_sems, t, j, sending):
    x, y, c, chips = _place()
    mine, other = 2 * x + y, 2 * chips[j][0] + chips[j][1]
    return pltpu.make_async_remote_copy(
        src_ref=ins[t].at[other if sending else mine], dst_ref=outs[t].at[mine if sending else other],
        send_sem=send_sems.at[3 * t + j], recv_sem=recv_sems.at[3 * t + j],
        device_id=(*chips[j], c), device_id_type=MESH)


def _rs_b_start(ins, outs, send_sems, recv_sems, local_sems):
    x, y, _, _ = _place()
    for t in range(len(ins)):
        for j in range(3):
            _rs_b_copy(ins, outs, send_sems, recv_sems, t, j, True).start()
        pltpu.make_async_copy(ins[t].at[2 * x + y], outs[t].at[2 * x + y], local_sems.at[t]).start()


def _rs_b_finish(ins, outs, send_sems, recv_sems, local_sems):
    x, y, _, _ = _place()
    for t in range(len(ins)):
        for j in range(3):
            _rs_b_copy(ins, outs, send_sems, recv_sems, t, j, False).wait_recv()
    for t in range(len(ins)):
        for j in range(3):
            _rs_b_copy(ins, outs, send_sems, recv_sems, t, j, True).wait_send()
        pltpu.make_async_copy(ins[t].at[2 * x + y], outs[t].at[2 * x + y], local_sems.at[t]).wait()


_EXCHANGES = {
    "all_gather": (7, lambda a: (N_DEV, *a.shape), _ag_start, _ag_finish),
    "rs_sibling": (4, lambda a: (4, *a.shape[1:]), _rs_a_start, _rs_a_finish),
    "rs_chips": (3, lambda a: a.shape, _rs_b_start, _rs_b_finish),
}


def _exchange_parts(kind, arrays):
    per, shape_of, start, finish = _EXCHANGES[kind]
    n = len(arrays)
    out_shape = [jax.ShapeDtypeStruct(shape_of(a), a.dtype) for a in arrays]
    sems = [pltpu.SemaphoreType.DMA((per * n,)), pltpu.SemaphoreType.DMA((per * n,)), pltpu.SemaphoreType.DMA((n,))]
    return out_shape, sems, start, finish


def exchange(kind, arrays, name):
    n = len(arrays)
    out_shape, sems, start, finish = _exchange_parts(kind, arrays)
    any_spec = pl.BlockSpec(memory_space=pl.ANY)

    def body(*refs):
        ins, outs, sem_refs = refs[:n], refs[n:2 * n], refs[2 * n:]
        start(ins, outs, *sem_refs)
        finish(ins, outs, *sem_refs)

    return pl.pallas_call(body, in_specs=[any_spec] * n, out_specs=[any_spec] * n, out_shape=out_shape,
                          scratch_shapes=sems, name=name)(*arrays)


def host_call(body, *, grid, in_specs, out_specs, out_shape, scratch_shapes, args, name, sem, rider=None):
    if not rider:
        outs = pl.pallas_call(body, grid=grid, in_specs=in_specs, out_specs=out_specs, out_shape=out_shape,
                              scratch_shapes=scratch_shapes, name=name, compiler_params=_cp(sem))(*args)
        return outs, None
    riders = [rider] if isinstance(rider, tuple) else list(rider)
    arrays = [a for _, arrs in riders for a in arrs]
    parts = [_exchange_parts(kind, arrs) for kind, arrs in riders]
    counts = [len(arrs) for _, arrs in riders]
    nr, ni, no, ns = len(arrays), len(in_specs), len(out_specs), len(scratch_shapes)
    any_spec = pl.BlockSpec(memory_space=pl.ANY)

    def wrapped(*refs):
        ins, r_ins = refs[:ni], refs[ni:ni + nr]
        outs, r_outs = refs[ni + nr:ni + nr + no], refs[ni + nr + no:ni + 2 * nr + no]
        scratch, sem_refs = refs[ni + 2 * nr + no:ni + 2 * nr + no + ns], refs[ni + 2 * nr + no + ns:]
        first = pl.program_id(0) == 0
        last = pl.program_id(0) == grid[0] - 1
        for ax in range(1, len(grid)):
            first = first & (pl.program_id(ax) == 0)
            last = last & (pl.program_id(ax) == grid[ax] - 1)

        def each(which):
            pos = 0
            for e, (cnt, part) in enumerate(zip(counts, parts)):
                part[which](r_ins[pos:pos + cnt], r_outs[pos:pos + cnt], *sem_refs[3 * e:3 * e + 3])
                pos += cnt

        @pl.when(first)
        def _():
            each(2)
        body(*ins, *outs, *scratch)

        @pl.when(last)
        def _():
            each(3)

    outs = pl.pallas_call(
        wrapped, grid=grid, in_specs=list(in_specs) + [any_spec] * nr, out_specs=list(out_specs) + [any_spec] * nr,
        out_shape=list(out_shape) + [s for p in parts for s in p[0]],
        scratch_shapes=list(scratch_shapes) + [s for p in parts for s in p[1]], name=name,
        compiler_params=_cp(("arbitrary",) * len(grid)))(*args, *arrays)
    got, pos = [], no
    for cnt in counts:
        got.append(outs[pos:pos + cnt])
        pos += cnt
    return outs[:no], (got[0] if isinstance(rider, tuple) else got)


def pair_add(g8, recv, c_idx, name):
    _, r, c = g8.shape
    tr = _pick(r, 512, 8)

    def body(c_ref, g_ref, r_ref, o_ref):
        o_ref[...] = (g_ref[...] + r_ref[...]).astype(BF16)

    return pl.pallas_call(
        body,
        grid_spec=pltpu.PrefetchScalarGridSpec(
            num_scalar_prefetch=1, grid=(4, r // tr),
            in_specs=[pl.BlockSpec((1, tr, c), lambda q, i, cr: (2 * q + cr[0], i, 0)),
                      pl.BlockSpec((1, tr, c), lambda q, i, cr: (q, i, 0))],
            out_specs=pl.BlockSpec((1, tr, c), lambda q, i, cr: (q, i, 0))),
        out_shape=jax.ShapeDtypeStruct((4, r, c), BF16), name=name,
        compiler_params=_cp(("parallel", "parallel")))(c_idx, g8, recv)


def all_reduce_small(pack, name):
    r, c = pack.shape

    def body(x_ref, o_ref, buf, send_sems, recv_sems, local_sem):
        x, y, cc, chips = _place()
        me, sibling = (x, y, cc), (x, y, 1 - cc)

        def slot(dev):
            return buf.at[4 * dev[0] + 2 * dev[1] + dev[2]]

        def copy(k, block, to, src=None):
            return pltpu.make_async_remote_copy(
                src_ref=slot(block) if src is None else src, dst_ref=slot(block),
                send_sem=send_sems.at[k], recv_sem=recv_sems.at[k], device_id=to, device_id_type=MESH)

        mine = pltpu.make_async_copy(x_ref, slot(me), local_sem)
        mine.start()
        first = [copy(0, me, sibling, src=x_ref)]
        first += [copy(1 + j, me, (*chip, cc), src=x_ref) for j, chip in enumerate(chips)]
        for cp in first:
            cp.start()
        passed = [copy(4 + j, (*chip, cc), sibling) for j, chip in enumerate(chips)]
        for j, chip in enumerate(chips):
            copy(1 + j, (*chip, cc), me).wait_recv()
            passed[j].start()
        copy(0, sibling, me).wait_recv()
        for j, chip in enumerate(chips):
            copy(4 + j, (*chip, 1 - cc), me).wait_recv()
        for cp in first + passed:
            cp.wait_send()
        mine.wait()
        acc = buf[0]
        for k in range(1, N_DEV):
            acc = acc + buf[k]
        o_ref[...] = acc

    return pl.pallas_call(
        body, in_specs=[pl.BlockSpec(memory_space=pltpu.VMEM)],
        out_specs=pl.BlockSpec(memory_space=pltpu.VMEM),
        out_shape=jax.ShapeDtypeStruct((r, c), F32),
        scratch_shapes=[pltpu.VMEM((N_DEV, r, c), F32), pltpu.SemaphoreType.DMA((7,)),
                        pltpu.SemaphoreType.DMA((7,)), pltpu.SemaphoreType.DMA],
        name=name, compiler_params=_cp())(pack)


def _row(v):
    return v.reshape(1, -1)


def _lane_row(vals, offset):
    return jnp.pad(vals, (offset, LANES - offset - vals.shape[0])).reshape(1, LANES)


def _bias_to_pairs(b):
    i, nh, bp = b.shape
    return b.transpose(1, 0, 2).reshape(nh // 2, 2 * i, bp)


def _bias_from_pairs(b):
    p, i2, bp = b.shape
    return b.reshape(2 * p, i2 // 2, bp).transpose(1, 0, 2)


class LocalWeights:
    def __init__(self, W):
        self.W = W
        self.grads = {}

    def big(self, l, la):
        W = self.W
        out = {"f_w_up": W["f_w_up"][l], "f_w_down": W["f_w_down"][l]}
        if l < la:
            out.update(a_w_in=W["a_w_in"][l], a_w_out=W["a_w_out"][l])
        else:
            out.update(b_w_q=W["b_w_q"][l - la], b_w_out=W["b_w_out"][l - la])
        if l == la:
            out["w_kv"] = W["w_kv"]
        return out

    col_blocks = None

    def prep_rider(self, l):
        return None

    def prep_got(self, l, got):
        pass

    def fwd_rider(self, l):
        return None

    def fwd_got(self, l, got):
        pass

    def bwd_rider_a(self, l):
        return None

    def bwd_got_a(self, l, got):
        pass

    def bwd_rider_b(self, l):
        return None

    def bwd_got_b(self, l, got):
        pass

    def bwd_rider_c(self, l):
        return None

    def bwd_got_c(self, l, got):
        pass

    def ffn_grads_ready(self, l, grads):
        pass

    def grads_ready(self, l, grads):
        for k_, g in grads.items():
            self.grads.setdefault(k_, {})[l] = g

    def stacked(self):
        return {k_: (jnp.stack([v_[l] for l in sorted(v_)]) if k_ != "w_kv" else next(iter(v_.values())))
                for k_, v_ in self.grads.items()}


def _named(name, l, rider):
    return name if rider is None else f"{name}_x{l}"


def local_step(x, target, W, comm=None):
    comm = LocalWeights(W) if comm is None else comm
    n, d = x.shape
    la, ha = W["a_A_log"].shape
    lb, hb, tbl = W["b_rel_bias"].shape
    depth = W["f_norm"].shape[0]
    clip = (tbl - 1) // 2
    tp = -(-tbl // LANES) * LANES
    qk = ha * A_HEAD
    cw = 3 * qk
    bw = hb * B_HEAD
    a_in = cw + qk + 2 * ha

    h = x
    saves = []
    kv = h_kv = w_kv = None
    for l in range(depth):
        big = comm.big(l, la)
        sv = {"h_in": h, "big": big}
        rider = comm.fwd_rider(l)
        if l < la:
            alog = _lane_row(W["a_A_log"][l], ha)
            dtb = _lane_row(W["a_dt_bias"][l], ha)
            proj = norm_matmul(h, _row(W["a_norm"][l]), big["a_w_in"], "a_in_proj")
            early = comm.prep_rider(l)
            (q, k, v, bb, gb, u), got = gdn_prep(proj, W["a_conv"][l], alog, dtb, ha, _named("gdn_prep", l, early), early)
            comm.prep_got(l, got)
            (o, states, tinv), got = gdn_fwd(q, k, v, bb, gb, ha, _named("gdn_fwd", l, rider), rider)
            h, y = gdn_out(o, proj, _row(W["a_out_norm"][l]), big["a_w_out"], h, ha, "gdn_out")
            sv.update(proj=proj, q=q, k=k, v=v, bb=bb, gb=gb, u=u, states=states, tinv=tinv, o=o, y=y, alog=alog, dtb=dtb)
        else:
            j = l - la
            if j == 0:
                h_kv, w_kv = h, big["w_kv"]
                kv = norm_matmul(h, _row(W["kv_norm"]), w_kv, "kv_proj")
            qp = norm_matmul(h, _row(W["b_norm"][j]), big["b_w_q"], "b_q_proj")
            tblp = jnp.pad(W["b_rel_bias"][j], ((0, 0), (0, tp - tbl)))
            bias = _bias_to_pairs(bias_expand(tblp, clip, "bias_expand"))
            (o,), got = attn_fwd(qp, kv, bias, _named("attn_fwd", l, rider), rider)
            h = matmul_res(o, big["b_w_out"], h, "b_out_proj")
            sv.update(qp=qp, bias=bias, o=o)
        comm.fwd_got(l, got)
        sv["h_mid"] = h
        up = norm_matmul(h, _row(W["f_norm"][l]), big["f_w_up"], "f_up_proj", out_dtype=BF16)
        h, act, hc = ffn_act_down(up, W["f_conv"][l], _row(W["f_conv_b"][l]), big["f_w_down"], h, "ffn_act_down")
        sv.update(up=up, act=act, hc=hc)
        saves.append(sv)

    loss, dh, d_final = loss_head(h, _row(W["final_norm"]), target)

    G = {k_: [None] * (la if k_.startswith("a_") else lb if k_.startswith("b_") else depth)
         for k_ in ("a_norm", "a_conv", "a_A_log", "a_dt_bias", "a_out_norm",
                    "b_norm", "b_rel_bias", "f_norm", "f_conv", "f_conv_b")}
    G["final_norm"] = d_final[0]
    dk_acc = dv_acc = None
    for l in reversed(range(depth)):
        sv = saves[l]
        big = sv["big"]
        gbig = {}
        dhc, dcb = ffn_bwd_act(dh, sv["hc"], big["f_w_down"], "ffn_bwd_act")
        gbig["f_w_down"] = matmul_tn(sv["act"], dh, "f_down_wgrad")
        G["f_conv_b"][l] = dcb[0]
        rider = comm.bwd_rider_a(l)
        (dh, dup, dcw, dg), got = ffn_bwd_up(dhc, sv["up"], W["f_conv"][l], big["f_w_up"], sv["h_mid"], dh,
                                             _row(W["f_norm"][l]), _named("ffn_bwd_up", l, rider), rider)
        comm.bwd_got_a(l, got)
        G["f_conv"][l] = dcw
        G["f_norm"][l] = dg[0]
        gbig["f_w_up"] = norm_matmul_tn(sv["h_mid"], _row(W["f_norm"][l]), dup, "f_up_wgrad", comm.col_blocks)
        comm.ffn_grads_ready(l, gbig)
        rider = comm.bwd_rider_b(l)
        if l < la:
            w_in = big["a_w_in"]
            do, dz, dwn = gdn_out_bwd(dh, sv["o"], sv["proj"], _row(W["a_out_norm"][l]), big["a_w_out"], ha, "gdn_out_bwd")
            G["a_out_norm"][l] = dwn[0]
            gbig["a_w_out"] = matmul_tn(sv["y"], dh, "a_out_wgrad")
            (dq, dk, dv, dbb, dgb), got = gdn_bwd(sv["q"], sv["k"], sv["v"], sv["bb"], sv["gb"], sv["states"], sv["tinv"], do, ha,
                                                  _named("gdn_bwd", l, rider), rider)
            comm.bwd_got_b(l, got)
            du, dba, dal, ddt = gdn_prep_bwd(sv["proj"], sv["u"], sv["alog"], sv["dtb"],
                                             dq, dk, dv, dbb, dgb, ha, "gdn_prep_bwd")
            G["a_A_log"][l] = dal[0, ha:2 * ha]
            G["a_dt_bias"][l] = ddt[0, ha:2 * ha]
            rider = comm.bwd_rider_c(l)
            (dqkv, dconv), got = conv_bwd(du, sv["proj"], W["a_conv"][l], A_CONV, _named("gdn_conv_bwd", l, rider), rider)
            if rider is not None:
                comm.bwd_got_c(l, got)
            G["a_conv"][l] = dconv
            gam = _row(W["a_norm"][l])
            pieces = [(dqkv, w_in[:, :cw]), (dz, w_in[:, cw:cw + qk]), (dba, w_in[:, cw + qk:])]
            gbig["a_w_in"] = jnp.concatenate(
                [norm_matmul_tn(sv["h_in"], gam, dqkv, "a_in_wgrad_qkv"),
                 norm_matmul_tn(sv["h_in"], gam, dz, "a_in_wgrad_z"),
                 norm_matmul_tn(sv["h_in"], gam, dba, "a_in_wgrad_ba")[:, :2 * ha]], axis=1)
            dh, dg = dx_norm_bwd(dh, sv["h_in"], gam, pieces, "a_in_dx")
            G["a_norm"][l] = dg[0]
        else:
            j = l - la
            d_o = matmul_nt(dh, big["b_w_out"], "b_out_dx")
            gbig["b_w_out"] = matmul_tn(sv["o"], dh, "b_out_wgrad")
            (dq, dk_acc, dv_acc, dbias), got = attn_bwd(
                sv["qp"], kv, sv["bias"], d_o, dk_acc, dv_acc,
                _named("attn_bwd" if dk_acc is None else "attn_bwd_acc", l, rider), rider)
            comm.bwd_got_b(l, got)
            G["b_rel_bias"][j] = bias_expand_bwd(_bias_from_pairs(dbias), clip, tp, "bias_expand_bwd")[:, :tbl]
            gam = _row(W["b_norm"][j])
            gbig["b_w_q"] = norm_matmul_tn(sv["h_in"], gam, dq, "b_q_wgrad")
            dh, dg = dx_norm_bwd(dh, sv["h_in"], gam, [(dq, big["b_w_q"])], "b_q_dx")
            G["b_norm"][j] = dg[0]
            if j == 0:
                gam = _row(W["kv_norm"])
                half = None if comm.col_blocks is None else comm.col_blocks // 2
                gbig["w_kv"] = jnp.concatenate([norm_matmul_tn(h_kv, gam, dk_acc, "kv_wgrad_k", half),
                                                norm_matmul_tn(h_kv, gam, dv_acc, "kv_wgrad_v", half)],
                                               axis=1 if half is None else 0)
                dh, dg = dx_norm_bwd(dh, h_kv, gam, [(dk_acc, w_kv[:, :bw]), (dv_acc, w_kv[:, bw:])], "kv_dx")
                G["kv_norm"] = dg[0]
        comm.grads_ready(l, gbig)
    out = {k_: (jnp.stack(v_) if isinstance(v_, list) else v_) for k_, v_ in G.items()}
    return loss[0, 0], dh, out, comm


WEIGHTS = ["a_norm", "a_w_in", "a_conv", "a_A_log", "a_dt_bias", "a_out_norm", "a_w_out", "kv_norm", "w_kv",
           "b_norm", "b_w_q", "b_rel_bias", "b_w_out", "f_norm", "f_w_up", "f_conv", "f_conv_b", "f_w_down",
           "final_norm"]
SHARD_AXIS = {"a_norm": 1, "a_w_in": 2, "a_conv": 2, "a_w_out": 1, "w_kv": 1, "b_w_q": 1, "b_w_out": 1,
              "f_w_up": 2, "f_conv": 2, "f_w_down": 1}
BIG = ["a_w_in", "a_w_out", "w_kv", "b_w_q", "b_w_out", "f_w_up", "f_w_down"]
SMALL_SHARDED = ["a_norm", "a_conv", "f_conv"]


def _unstack(g, axis):
    if axis == 0:
        return g.reshape(-1, *g.shape[2:])
    return jnp.concatenate([g[i] for i in range(N_DEV)], axis=axis)


def _to_blocks(full, axis):
    if axis == 0:
        return full.reshape(N_DEV, -1, full.shape[-1])
    return jnp.stack(jnp.split(full, N_DEV, axis=axis))


def _pack(arrs):
    flat = []
    for a in arrs:
        f = a.reshape(-1)
        flat.append(jnp.pad(f, (0, (-f.shape[0]) % LANES)))
    f = jnp.concatenate(flat)
    f = jnp.pad(f, (0, (-f.shape[0]) % (8 * LANES)))
    return f.reshape(-1, LANES)


def _unpack(pack, shapes):
    flat = pack.reshape(-1)
    out, pos = [], 0
    for s in shapes:
        sz = math.prod(s)
        out.append(flat[pos:pos + sz].reshape(s))
        pos += sz + (-sz) % LANES
    return out


def _as2d(a):
    return a.reshape(1, -1) if a.ndim == 1 else a.reshape(-1, a.shape[-1])


def kernel(x, a_norm, a_w_in, a_conv, a_A_log, a_dt_bias, a_out_norm, a_w_out, kv_norm, w_kv, b_norm, b_w_q, b_rel_bias, b_w_out, f_norm, f_w_up, f_conv, f_conv_b, f_w_down, final_norm, loss_target, m_a_norm, m_a_w_in, m_a_conv, m_a_A_log, m_a_dt_bias, m_a_out_norm, m_a_w_out, m_kv_norm, m_w_kv, m_b_norm, m_b_w_q, m_b_rel_bias, m_b_w_out, m_f_norm, m_f_w_up, m_f_conv, m_f_conv_b, m_f_w_down, m_final_norm, v_a_norm, v_a_w_in, v_a_conv, v_a_A_log, v_a_dt_bias, v_a_out_norm, v_a_w_out, v_kv_norm, v_w_kv, v_b_norm, v_b_w_q, v_b_rel_bias, v_b_w_out, v_f_norm, v_f_w_up, v_f_conv, v_f_conv_b, v_f_w_down, v_final_norm):
    w = dict(a_norm=a_norm, a_w_in=a_w_in, a_conv=a_conv, a_A_log=a_A_log, a_dt_bias=a_dt_bias,
             a_out_norm=a_out_norm, a_w_out=a_w_out, kv_norm=kv_norm, w_kv=w_kv, b_norm=b_norm, b_w_q=b_w_q,
             b_rel_bias=b_rel_bias, b_w_out=b_w_out, f_norm=f_norm, f_w_up=f_w_up, f_conv=f_conv,
             f_conv_b=f_conv_b, f_w_down=f_w_down, final_norm=final_norm)
    mom = dict(a_norm=m_a_norm, a_w_in=m_a_w_in, a_conv=m_a_conv, a_A_log=m_a_A_log, a_dt_bias=m_a_dt_bias,
               a_out_norm=m_a_out_norm, a_w_out=m_a_w_out, kv_norm=m_kv_norm, w_kv=m_w_kv, b_norm=m_b_norm,
               b_w_q=m_b_w_q, b_rel_bias=m_b_rel_bias, b_w_out=m_b_w_out, f_norm=m_f_norm, f_w_up=m_f_w_up,
               f_conv=m_f_conv, f_conv_b=m_f_conv_b, f_w_down=m_f_w_down, final_norm=m_final_norm)
    var = dict(a_norm=v_a_norm, a_w_in=v_a_w_in, a_conv=v_a_conv, a_A_log=v_a_A_log, a_dt_bias=v_a_dt_bias,
               a_out_norm=v_a_out_norm, a_w_out=v_a_w_out, kv_norm=v_kv_norm, w_kv=v_w_kv, b_norm=v_b_norm,
               b_w_q=v_b_w_q, b_rel_bias=v_b_rel_bias, b_w_out=v_b_w_out, f_norm=v_f_norm, f_w_up=v_f_w_up,
               f_conv=v_f_conv, f_conv_b=v_f_conv_b, f_w_down=v_f_w_down, final_norm=v_final_norm)
    me = 4 * lax.axis_index("x") + 2 * lax.axis_index("y") + lax.axis_index("c")

    la, depth = a_A_log.shape[0], f_norm.shape[0]
    c_idx = lax.axis_index("c").astype(jnp.int32).reshape(1)
    shard_bf16 = {k: w[k].astype(BF16) for k in BIG}

    class Sharded(LocalWeights):
        col_blocks = N_DEV

        def __init__(self):
            super().__init__(w)
            self.full = {}
            self.stacks = {}
            self.pending = None
            self.parts = None

        def names(self, l):
            out = ["a_w_in", "a_w_out"] if l < la else ["b_w_q", "b_w_out"]
            return out + ["f_w_up", "f_w_down"] + (["w_kv"] if l == la else [])

        def index(self, k, l):
            return None if k == "w_kv" else (l - la if k.startswith("b_") else l)

        def shards(self, l, names=None):
            return [shard_bf16[k] if k == "w_kv" else shard_bf16[k][self.index(k, l)]
                    for k in (self.names(l) if names is None else names)]

        def install(self, l, gathered, names=None):
            out = self.full.setdefault(l, {})
            for k, g in zip(self.names(l) if names is None else names, gathered):
                out[k] = _unstack(g, SHARD_AXIS[k] - (k != "w_kv"))
                if k == "a_w_in":
                    out[k] = jnp.pad(out[k], ((0, 0), (0, (-out[k].shape[1]) % LANES)))

        def big(self, l, la_):
            return self.full[l]

        def first_names(self):
            return ["a_w_in"] if la > 0 else self.names(0)

        def prep_rider(self, l):
            rest = [k for k in self.names(0) if k not in self.first_names()]
            return ("all_gather", self.shards(0, rest)) if l == 0 and rest else None

        def prep_got(self, l, got):
            if got is not None:
                self.install(0, got, [k for k in self.names(0) if k not in self.first_names()])

        def fwd_rider(self, l):
            return ("all_gather", self.shards(l + 1)) if l + 1 < depth else None

        def fwd_got(self, l, got):
            if got is not None:
                self.install(l + 1, got)

        def blocks(self, grads, keys):
            return [grads[k] if grads[k].ndim == 3 else _to_blocks(grads[k], SHARD_AXIS[k] - (k != "w_kv"))
                    for k in keys]

        def grads_ready(self, l, grads):
            keys = [k for k in self.names(l) if (k, l) not in self.early_keys]
            self.pending = (l, keys, self.blocks(grads, keys))

        early = early_parts = None
        early_keys = ()

        def ffn_grads_ready(self, l, grads):
            if l == 0 and la > 0:
                keys = ["f_w_up", "f_w_down"]
                self.early = (keys, self.blocks(grads, keys))
                self.early_keys = tuple((k, 0) for k in keys)

        def bwd_rider_a(self, l):
            return None if self.pending is None else ("rs_sibling", self.pending[2])

        def add_pairs(self, g8, from_sibling):
            return [pair_add(g, r, c_idx, "grads_pair_add") for g, r in zip(g8, from_sibling)]

        def bwd_got_a(self, l, got):
            if got is not None:
                self.parts = self.add_pairs(self.pending[2], got)

        def bwd_rider_b(self, l):
            riders = [] if self.parts is None else [("rs_chips", self.parts)]
            if self.early is not None:
                riders.append(("rs_sibling", self.early[1]))
            return riders

        def keep(self, stacks):
            l, keys, _ = self.pending
            for k, s in zip(keys, stacks):
                self.stacks[(k, l)] = s
            self.pending = self.parts = None

        def bwd_got_b(self, l, got):
            got = list(got or [])
            if self.parts is not None:
                self.keep(got.pop(0))
            if self.early is not None and got:
                self.early_parts = self.add_pairs(self.early[1], got.pop(0))

        def bwd_rider_c(self, l):
            return None if self.early_parts is None else ("rs_chips", self.early_parts)

        def bwd_got_c(self, l, got):
            for k, s in zip(self.early[0], got):
                self.stacks[(k, 0)] = s
            self.early = self.early_parts = None

        def finish(self):
            self.parts = self.add_pairs(self.pending[2], exchange("rs_sibling", self.pending[2], "grads_to_sibling"))
            self.keep(exchange("rs_chips", self.parts, "grads_to_chips"))

    comm = Sharded()

    small_shapes = [w[k].shape for k in SMALL_SHARDED]
    gathered = exchange("all_gather", comm.shards(0, comm.first_names()) + [_pack([w[k] for k in SMALL_SHARDED])],
                        "weights_all_gather")
    comm.install(0, gathered[:-1], comm.first_names())
    full = dict(w)
    small = [_unpack(gathered[-1][i], small_shapes) for i in range(N_DEV)]
    for idx, k in enumerate(SMALL_SHARDED):
        full[k] = jnp.concatenate([small[i][idx] for i in range(N_DEV)], axis=SHARD_AXIS[k])

    loss_part, grad_x, G, _ = local_step(x[0], loss_target[0], full, comm)
    comm.finish()
    stacks = []
    for k in BIG:
        layers = sorted(l for (k_, l) in comm.stacks if k_ == k)
        stacks.append(jnp.concatenate([comm.stacks[(k, l)] for l in layers], axis=1))

    small_names = [k for k in WEIGHTS if k not in BIG]
    reduced = _unpack(all_reduce_small(_pack([G[k] for k in small_names] + [loss_part.reshape(1)]), "small_all_reduce"),
                      [G[k].shape for k in small_names] + [(1,)])
    loss = reduced[-1][0]
    small_g = dict(zip(small_names, reduced[:-1]))
    for k in SMALL_SHARDED:
        sz = w[k].shape[SHARD_AXIS[k]]
        small_g[k] = lax.dynamic_slice_in_dim(small_g[k], me * sz, sz, axis=SHARD_AXIS[k])

    res = {}
    for k, st in zip(BIG, stacks):
        outs = adamw(_as2d(w[k]), st, _as2d(mom[k]), _as2d(var[k]), "adamw_" + k)
        res[k] = [o.reshape(w[k].shape) for o in outs]
    for k in small_names:
        outs = adamw(_as2d(w[k]), _as2d(small_g[k])[None], _as2d(mom[k]), _as2d(var[k]), "adamw_" + k)
        res[k] = [o.reshape(w[k].shape) for o in outs]

    return (loss, grad_x[None], *[res[k][0] for k in WEIGHTS], *[res[k][1] for k in WEIGHTS],
            *[res[k][2] for k in WEIGHTS], *[res[k][3] for k in WEIGHTS])
```

```python
import functools
import math

import jax
import jax.numpy as jnp
from jax import lax
from jax.experimental import pallas as pl
from jax.experimental.pallas import tpu as pltpu

F32 = jnp.float32
BF16 = jnp.bfloat16
HI = lax.Precision.HIGHEST
MESH = pl.DeviceIdType.MESH

EPS = 1e-6
NEG_INF = -1e30
CHUNK = 64
LEFT_CHUNKS = 8
BAND = (LEFT_CHUNKS + 1) * CHUNK
BAND_PAD = 640
A_CONV = 4
F_CONV = 3
A_HEAD = 128
B_HEAD = 64
LANES = 128
HALO = 8
N_DEV = 8

ADAM_LR = 0.001
ADAM_B1 = 0.9
ADAM_B2 = 0.999
ADAM_EPS = 1e-08
ADAM_WD = 0.01
ADAM_STEP = 10

VMEM_LIMIT_V7X = 56 * 1024 * 1024
GDN_BWD_HEADS = 8
COL_CHUNK = 256
FFN_TILE = 256


def _cp(sem=None, vmem=VMEM_LIMIT_V7X):
    kw = dict(vmem_limit_bytes=vmem)
    if sem is not None:
        kw["dimension_semantics"] = sem
    return pltpu.CompilerParams(**kw)


def _pick(n, target, q=LANES):
    best = None
    for t in range(q, min(n, target) + 1, q):
        if n % t == 0:
            best = t
    return best if best is not None else n


def _sig(x):
    return 1.0 / (1.0 + jnp.exp(-x))


def _softplus(x):
    return jnp.maximum(x, 0.0) + jnp.log(1.0 + jnp.exp(-jnp.abs(x)))


def _rms(x, g):
    return x * lax.rsqrt(jnp.mean(x * x, axis=-1, keepdims=True) + EPS) * g


def _rms_bwd(x, g, dxn):
    r = lax.rsqrt(jnp.mean(x * x, axis=-1, keepdims=True) + EPS)
    gd = dxn * g
    dx = r * gd - x * (r * r * r) * jnp.mean(x * gd, axis=-1, keepdims=True)
    dg = jnp.sum(dxn * x * r, axis=0, keepdims=True)
    return dx, dg


def _dot(a, b):
    return jnp.dot(a, b, preferred_element_type=F32)


def _dot_nt(a, b):
    return lax.dot_general(a, b, (((1,), (1,)), ((), ())), preferred_element_type=F32)


def _dot_tn(a, b):
    return lax.dot_general(a, b, (((0,), (0,)), ((), ())), preferred_element_type=F32)


def _hdot(a, b):
    return jnp.dot(a, b, precision=HI, preferred_element_type=F32)


def _hdot_nt(a, b):
    return lax.dot_general(a, b, (((1,), (1,)), ((), ())), precision=HI, preferred_element_type=F32)


def _hdot_tn(a, b):
    return lax.dot_general(a, b, (((0,), (0,)), ((), ())), precision=HI, preferred_element_type=F32)


def _resident(shape, index_map):
    return pl.BlockSpec(shape, index_map, pipeline_mode=pl.Buffered(1))


def norm_matmul(h, gamma, w, name, out_dtype=F32, w_t=False):
    n, d = h.shape
    nc = w.shape[0] if w_t else w.shape[1]
    tm = _pick(n, 2048 if out_dtype == BF16 else 1024, 8)
    tn = _pick(nc, 1536)

    def body(h_ref, g_ref, w_ref, o_ref):
        xn = _rms(h_ref[...], g_ref[...]).astype(BF16)
        o_ref[...] = (_dot_nt(xn, w_ref[...]) if w_t else _dot(xn, w_ref[...])).astype(out_dtype)

    return pl.pallas_call(
        body, grid=(nc // tn, n // tm),
        in_specs=[pl.BlockSpec((tm, d), lambda j, i: (i, 0)),
                  pl.BlockSpec((1, d), lambda j, i: (0, 0)),
                  pl.BlockSpec((tn, d), lambda j, i: (j, 0)) if w_t else pl.BlockSpec((d, tn), lambda j, i: (0, j))],
        out_specs=pl.BlockSpec((tm, tn), lambda j, i: (i, j)),
        out_shape=jax.ShapeDtypeStruct((n, nc), out_dtype), name=name,
        compiler_params=_cp(("parallel", "parallel")))(h, gamma, w)


def norm_matmul_tn(h, gamma, dy, name, transposed=False):
    n, d = h.shape
    nc = dy.shape[1]
    tm = _pick(n, 2048 if dy.dtype == BF16 else 1024, 8)
    tn = _pick(nc, 1536)

    def body(h_ref, g_ref, dy_ref, o_ref):
        @pl.when(pl.program_id(1) == 0)
        def _():
            o_ref[...] = jnp.zeros_like(o_ref)
        xn = _rms(h_ref[...], g_ref[...]).astype(BF16)
        dyb = dy_ref[...].astype(BF16)
        o_ref[...] += _dot_tn(dyb, xn) if transposed else _dot_tn(xn, dyb)

    return pl.pallas_call(
        body, grid=(nc // tn, n // tm),
        in_specs=[pl.BlockSpec((tm, d), lambda j, i: (i, 0)),
                  pl.BlockSpec((1, d), lambda j, i: (0, 0)),
                  pl.BlockSpec((tm, tn), lambda j, i: (i, j))],
        out_specs=pl.BlockSpec((tn, d), lambda j, i: (j, 0)) if transposed else pl.BlockSpec((d, tn), lambda j, i: (0, j)),
        out_shape=jax.ShapeDtypeStruct((nc, d) if transposed else (d, nc), F32), name=name,
        compiler_params=_cp(("parallel", "arbitrary")))(h, gamma, dy)


def matmul_tn(a, dy, name):
    n, ka = a.shape
    nc = dy.shape[1]
    tm = _pick(n, 2048 if a.dtype == BF16 else 1024, 8)
    tk = _pick(ka, 1536)
    tn = _pick(nc, 1024)

    def body(a_ref, dy_ref, o_ref):
        @pl.when(pl.program_id(2) == 0)
        def _():
            o_ref[...] = jnp.zeros_like(o_ref)
        o_ref[...] += _dot_tn(a_ref[...].astype(BF16), dy_ref[...].astype(BF16))

    return pl.pallas_call(
        body, grid=(ka // tk, nc // tn, n // tm),
        in_specs=[pl.BlockSpec((tm, tk), lambda k, j, i: (i, k)),
                  pl.BlockSpec((tm, tn), lambda k, j, i: (i, j))],
        out_specs=pl.BlockSpec((tk, tn), lambda k, j, i: (k, j)),
        out_shape=jax.ShapeDtypeStruct((ka, nc), F32), name=name,
        compiler_params=_cp(("parallel", "parallel", "arbitrary")))(a, dy)


def matmul_res(a, w, h, name):
    n, k = a.shape
    d = w.shape[1]
    tm = _pick(n, 512, 8)

    def body(a_ref, w_ref, h_ref, o_ref):
        o_ref[...] = h_ref[...] + _dot(a_ref[...].astype(BF16), w_ref[...])

    return pl.pallas_call(
        body, grid=(n // tm,),
        in_specs=[pl.BlockSpec((tm, k), lambda i: (i, 0)),
                  _resident((k, d), lambda i: (0, 0)),
                  pl.BlockSpec((tm, d), lambda i: (i, 0))],
        out_specs=pl.BlockSpec((tm, d), lambda i: (i, 0)),
        out_shape=jax.ShapeDtypeStruct((n, d), F32), name=name,
        compiler_params=_cp(("parallel",)))(a, w, h)


def matmul_nt(dy, w, name):
    n, k = dy.shape
    d = w.shape[0]
    tm = _pick(n, 512, 8)

    def body(dy_ref, w_ref, o_ref):
        o_ref[...] = _dot_nt(dy_ref[...].astype(BF16), w_ref[...])

    return pl.pallas_call(
        body, grid=(n // tm,),
        in_specs=[pl.BlockSpec((tm, k), lambda i: (i, 0)),
                  _resident((d, k), lambda i: (0, 0))],
        out_specs=pl.BlockSpec((tm, d), lambda i: (i, 0)),
        out_shape=jax.ShapeDtypeStruct((n, d), F32), name=name,
        compiler_params=_cp(("parallel",)))(dy, w)


def dx_norm_bwd(dout, h, gamma, pieces, name, rider=None, w_t=False):
    n, d = h.shape
    tm = _pick(n, 256, 8)
    np_ = len(pieces)
    mm = _dot if w_t else _dot_nt

    def body(*refs):
        dout_ref, h_ref, g_ref = refs[:3]
        dys = refs[3:3 + np_]
        ws = refs[3 + np_:3 + 2 * np_]
        dh_ref, dg_ref = refs[3 + 2 * np_:]
        dxn = mm(dys[0][...].astype(BF16), ws[0][...])
        for p in range(1, np_):
            dxn = dxn + mm(dys[p][...].astype(BF16), ws[p][...])
        dx, dg = _rms_bwd(h_ref[...], g_ref[...], dxn)
        dh_ref[...] = dout_ref[...] + dx

        @pl.when(pl.program_id(0) == 0)
        def _():
            dg_ref[...] = jnp.zeros_like(dg_ref)
        dg_ref[...] += dg

    in_specs = [pl.BlockSpec((tm, d), lambda i: (i, 0)),
                pl.BlockSpec((tm, d), lambda i: (i, 0)),
                pl.BlockSpec((1, d), lambda i: (0, 0))]
    in_specs += [pl.BlockSpec((tm, dy.shape[1]), lambda i: (i, 0)) for dy, _ in pieces]
    in_specs += [_resident(w.shape, lambda i: (0, 0)) for _, w in pieces]
    (dh, dg), got = host_call(
        body, grid=(n // tm,), in_specs=in_specs,
        out_specs=[pl.BlockSpec((tm, d), lambda i: (i, 0)), pl.BlockSpec((1, d), lambda i: (0, 0))],
        out_shape=[jax.ShapeDtypeStruct((n, d), F32), jax.ShapeDtypeStruct((1, d), F32)], name=name,
        scratch_shapes=[], sem=("arbitrary",), rider=rider,
        args=(dout, h, gamma, *[p[0] for p in pieces], *[p[1] for p in pieces]))
    return (dh, dg) if rider is None else (dh, dg, got)


def loss_head(h, gamma, target, name="loss_head"):
    n, d = h.shape
    tm = _pick(n, 512, 8)

    def body(h_ref, g_ref, t_ref, loss_ref, dh_ref, dg_ref):
        @pl.when(pl.program_id(0) == 0)
        def _():
            loss_ref[...] = jnp.zeros_like(loss_ref)
            dg_ref[...] = jnp.zeros_like(dg_ref)
        x = h_ref[...]
        g = g_ref[...]
        e = _rms(x, g) - t_ref[...]
        part = jnp.sum(jnp.sum(e * e, axis=-1, keepdims=True), axis=0, keepdims=True) * (0.5 / d)
        loss_ref[...] += jnp.broadcast_to(part, loss_ref.shape)
        dx, dg = _rms_bwd(x, g, e * (1.0 / d))
        dh_ref[...] = dx
        dg_ref[...] += dg

    return pl.pallas_call(
        body, grid=(n // tm,),
        in_specs=[pl.BlockSpec((tm, d), lambda i: (i, 0)), pl.BlockSpec((1, d), lambda i: (0, 0)),
                  pl.BlockSpec((tm, d), lambda i: (i, 0))],
        out_specs=[pl.BlockSpec((8, LANES), lambda i: (0, 0)), pl.BlockSpec((tm, d), lambda i: (i, 0)),
                   pl.BlockSpec((1, d), lambda i: (0, 0))],
        out_shape=[jax.ShapeDtypeStruct((8, LANES), F32), jax.ShapeDtypeStruct((n, d), F32),
                   jax.ShapeDtypeStruct((1, d), F32)], name=name,
        compiler_params=_cp(("arbitrary",)))(h, gamma, target)


def _halo_rows(dtype):
    return HALO * (4 // jnp.dtype(dtype).itemsize)


def _prev_halo_map(t, hb=HALO):
    return lambda i: (jnp.maximum(i * (t // hb) - 1, 0), 0)


def _next_halo_map(t, n, hb=HALO):
    return lambda i: (jnp.minimum((i + 1) * (t // hb), n // hb - 1), 0)


def _fill_prev(xs, main_ref, halo_ref, i, cols=slice(None)):
    hb = halo_ref.shape[0]
    xs[0:HALO, :] = jnp.where(i > 0, halo_ref[hb - HALO:hb, cols].astype(F32), 0.0)
    xs[HALO:, :] = main_ref[:, cols].astype(F32)


def _causal_conv(xs, w_ref, width, t, cols=slice(None)):
    x = xs[...]
    acc = w_ref[width - 1:width, cols] * x[HALO:, :]
    for k in range(width - 1):
        acc = acc + w_ref[k:k + 1, cols] * pltpu.roll(x, width - 1 - k, axis=0)[HALO:, :]
    return acc


def _col_chunks(width, target=COL_CHUNK):
    tc = _pick(width, target)
    return [slice(j * tc, (j + 1) * tc) for j in range(width // tc)]


def ffn_act_down(up, conv_w, conv_b, w_down, h, name):
    n, c2 = up.shape
    ff = c2 // 2
    d = h.shape[1]
    t = _pick(n, 2 * FFN_TILE, 8)
    hb = _halo_rows(up.dtype)
    chunks = _col_chunks(ff)
    tc = chunks[0].stop

    def body(up_ref, halo_ref, cw_ref, cb_ref, wd_ref, h_ref, o_ref, act_ref, hc_ref, xg, xv):
        i = pl.program_id(0)
        acc = h_ref[...]
        for cs in chunks:
            vs = slice(ff + cs.start, ff + cs.stop)
            _fill_prev(xg, up_ref, halo_ref, i, cs)
            _fill_prev(xv, up_ref, halo_ref, i, vs)
            gate = _causal_conv(xg, cw_ref, F_CONV, t, cs) + cb_ref[:, cs]
            val = _causal_conv(xv, cw_ref, F_CONV, t, vs) + cb_ref[:, vs]
            hc_ref[:, cs] = gate.astype(BF16)
            hc_ref[:, vs] = val.astype(BF16)
            act = (gate * _sig(gate) * val).astype(BF16)
            act_ref[:, cs] = act
            acc = acc + _dot(act, wd_ref[cs, :])
        o_ref[...] = acc

    return pl.pallas_call(
        body, grid=(n // t,),
        in_specs=[pl.BlockSpec((t, c2), lambda i: (i, 0)),
                  pl.BlockSpec((hb, c2), _prev_halo_map(t, hb)),
                  pl.BlockSpec((F_CONV, c2), lambda i: (0, 0)),
                  pl.BlockSpec((1, c2), lambda i: (0, 0)),
                  _resident((ff, d), lambda i: (0, 0)),
                  pl.BlockSpec((t, d), lambda i: (i, 0))],
        out_specs=[pl.BlockSpec((t, d), lambda i: (i, 0)), pl.BlockSpec((t, ff), lambda i: (i, 0)),
                   pl.BlockSpec((t, c2), lambda i: (i, 0))],
        out_shape=[jax.ShapeDtypeStruct((n, d), F32), jax.ShapeDtypeStruct((n, ff), BF16),
                   jax.ShapeDtypeStruct((n, c2), BF16)],
        scratch_shapes=[pltpu.VMEM((t + HALO, tc), F32), pltpu.VMEM((t + HALO, tc), F32)], name=name,
        compiler_params=_cp(("parallel",)))(up, up, conv_w, conv_b, w_down, h)


def ffn_bwd_act(dout, hc, w_down, name):
    n, c2 = hc.shape
    ff = c2 // 2
    d = dout.shape[1]
    t = _pick(n, 2 * FFN_TILE, 8)
    chunks = _col_chunks(ff)

    def body(dout_ref, hc_ref, wd_ref, dhc_ref, dcb_ref):
        i = pl.program_id(0)

        @pl.when(i == 0)
        def _():
            dcb_ref[...] = jnp.zeros_like(dcb_ref)
        doutb = dout_ref[...].astype(BF16)
        for cs in chunks:
            vs = slice(ff + cs.start, ff + cs.stop)
            gate = hc_ref[:, cs].astype(F32)
            val = hc_ref[:, vs].astype(F32)
            sg = _sig(gate)
            da = _dot_nt(doutb, wd_ref[cs, :])
            dgate = da * val * (sg * (1.0 + gate * (1.0 - sg)))
            dval = da * gate * sg
            dhc_ref[:, cs] = dgate.astype(BF16)
            dhc_ref[:, vs] = dval.astype(BF16)
            dcb_ref[:, cs] += jnp.sum(dgate, axis=0, keepdims=True)
            dcb_ref[:, vs] += jnp.sum(dval, axis=0, keepdims=True)

    return pl.pallas_call(
        body, grid=(n // t,),
        in_specs=[pl.BlockSpec((t, d), lambda i: (i, 0)),
                  pl.BlockSpec((t, c2), lambda i: (i, 0)),
                  _resident((ff, d), lambda i: (0, 0))],
        out_specs=[pl.BlockSpec((t, c2), lambda i: (i, 0)), pl.BlockSpec((1, c2), lambda i: (0, 0))],
        out_shape=[jax.ShapeDtypeStruct((n, c2), BF16), jax.ShapeDtypeStruct((1, c2), F32)], name=name,
        compiler_params=_cp(("arbitrary",)))(dout, hc, w_down)


def conv_bwd_tail(dy_ref, dnext_ref, x_ref, cw_ref, dcw_ref, ds, width, t, i, last, cols=slice(None)):
    ds[0:t, :] = dy_ref[:, cols].astype(F32)
    ds[t:, :] = jnp.where(i < last, dnext_ref[0:HALO, cols].astype(F32), 0.0)
    x = x_ref[:, cols].astype(F32)
    dall = ds[...]
    dx = None
    for k in range(width):
        off = width - 1 - k
        shifted = dall[0:t, :] if off == 0 else pltpu.roll(dall, t + HALO - off, axis=0)[0:t, :]
        term = cw_ref[k:k + 1, cols] * shifted
        dx = term if dx is None else dx + term
        dcw_ref[k:k + 1, cols] += jnp.sum(shifted * x, axis=0, keepdims=True)
    return dx


def ffn_bwd_up(dhc, up, conv_w, w_up, h, dout, gamma, name, rider=None):
    n, c2 = up.shape
    d = h.shape[1]
    t = _pick(n, FFN_TILE, 8)
    last = n // t - 1
    chunks = _col_chunks(c2)
    tc = chunks[0].stop

    def body(dhc_ref, dnext_ref, up_ref, cw_ref, wu_ref, h_ref, dout_ref, g_ref,
             dh_ref, dup_ref, dcw_ref, dg_ref, ds):
        i = pl.program_id(0)

        @pl.when(i == 0)
        def _():
            dcw_ref[...] = jnp.zeros_like(dcw_ref)
            dg_ref[...] = jnp.zeros_like(dg_ref)
        dxn = jnp.zeros((t, d), F32)
        for cs in chunks:
            dup = conv_bwd_tail(dhc_ref, dnext_ref, up_ref, cw_ref, dcw_ref, ds, F_CONV, t, i, last, cs)
            dupb = dup.astype(BF16)
            dup_ref[:, cs] = dupb
            dxn = dxn + _dot(dupb, wu_ref[cs, :])
        dx, dg = _rms_bwd(h_ref[...], g_ref[...], dxn)
        dh_ref[...] = dout_ref[...] + dx
        dg_ref[...] += dg

    return host_call(
        body, grid=(n // t,), rider=rider, sem=("arbitrary",), args=(dhc, dhc, up, conv_w, w_up, h, dout, gamma),
        in_specs=[pl.BlockSpec((t, c2), lambda i: (i, 0)),
                  pl.BlockSpec((_halo_rows(dhc.dtype), c2), _next_halo_map(t, n, _halo_rows(dhc.dtype))),
                  pl.BlockSpec((t, c2), lambda i: (i, 0)),
                  pl.BlockSpec((F_CONV, c2), lambda i: (0, 0)),
                  _resident((c2, d), lambda i: (0, 0)),
                  pl.BlockSpec((t, d), lambda i: (i, 0)),
                  pl.BlockSpec((t, d), lambda i: (i, 0)),
                  pl.BlockSpec((1, d), lambda i: (0, 0))],
        out_specs=[pl.BlockSpec((t, d), lambda i: (i, 0)), pl.BlockSpec((t, c2), lambda i: (i, 0)),
                   pl.BlockSpec((F_CONV, c2), lambda i: (0, 0)), pl.BlockSpec((1, d), lambda i: (0, 0))],
        out_shape=[jax.ShapeDtypeStruct((n, d), F32), jax.ShapeDtypeStruct((n, c2), BF16),
                   jax.ShapeDtypeStruct((F_CONV, c2), F32), jax.ShapeDtypeStruct((1, d), F32)],
        scratch_shapes=[pltpu.VMEM((t + HALO, tc), F32)], name=name)


def _gdn_head(uq, uk, uv, pba, alog, dtb, head, n_heads):
    lane = lax.broadcasted_iota(jnp.int32, pba.shape, 1)
    sq = uq * _sig(uq)
    q = sq * lax.rsqrt(jnp.sum(sq * sq, axis=-1, keepdims=True) + EPS) * (A_HEAD ** -0.5)
    sk = uk * _sig(uk)
    k = sk * lax.rsqrt(jnp.sum(sk * sk, axis=-1, keepdims=True) + EPS)
    v = uv * _sig(uv)
    beta = jnp.sum(jnp.where(lane == head, _sig(pba), 0.0), axis=-1, keepdims=True)
    g_all = -jnp.exp(alog) * _softplus(pba + dtb)
    g = jnp.sum(jnp.where(lane == n_heads + head, g_all, 0.0), axis=-1, keepdims=True)
    return q, k, v, jnp.broadcast_to(beta, uq.shape), jnp.broadcast_to(g, uq.shape)


def gdn_prep(proj, conv_w, alog, dtb, n_heads, name, rider=None):
    n = proj.shape[0]
    qk = n_heads * A_HEAD
    cw = 3 * qk
    ba_blk = (cw + qk) // LANES
    t = _pick(n, 256, 8)

    def body(x_ref, halo_ref, pba_ref, cw_ref, al_ref, dt_ref, q_ref, k_ref, v_ref, b_ref, g_ref, u_ref, xs):
        i = pl.program_id(0)
        xs[0:HALO, :] = jnp.where(i > 0, halo_ref[...], 0.0)
        xs[HALO:, :] = x_ref[...]
        u = _causal_conv(xs, cw_ref, A_CONV, t)
        u_ref[...] = u.astype(BF16)
        pba = pba_ref[...]
        for hd in range(n_heads):
            s0 = slice(hd * A_HEAD, (hd + 1) * A_HEAD)
            s1 = slice(qk + hd * A_HEAD, qk + (hd + 1) * A_HEAD)
            s2 = slice(2 * qk + hd * A_HEAD, 2 * qk + (hd + 1) * A_HEAD)
            q, k, v, bb, gb = _gdn_head(u[:, s0], u[:, s1], u[:, s2], pba, al_ref[...], dt_ref[...], hd, n_heads)
            q_ref[:, s0] = q
            k_ref[:, s0] = k
            v_ref[:, s0] = v
            b_ref[:, s0] = bb
            g_ref[:, s0] = gb

    out = jax.ShapeDtypeStruct((n, qk), F32)
    return host_call(
        body, grid=(n // t,),
        in_specs=[pl.BlockSpec((t, cw), lambda i: (i, 0)),
                  pl.BlockSpec((HALO, cw), _prev_halo_map(t)),
                  pl.BlockSpec((t, LANES), lambda i: (i, ba_blk)),
                  pl.BlockSpec((A_CONV, cw), lambda i: (0, 0)),
                  pl.BlockSpec((1, LANES), lambda i: (0, 0)),
                  pl.BlockSpec((1, LANES), lambda i: (0, 0))],
        out_specs=[pl.BlockSpec((t, qk), lambda i: (i, 0))] * 5 + [pl.BlockSpec((t, cw), lambda i: (i, 0))],
        out_shape=[out] * 5 + [jax.ShapeDtypeStruct((n, cw), BF16)],
        scratch_shapes=[pltpu.VMEM((t + HALO, cw), F32)], name=name,
        sem=("parallel",), args=(proj, proj, proj, conv_w, alog, dtb), rider=rider)


def gdn_prep_bwd(proj, u, alog, dtb, dq, dk, dv, dbb, dgb, n_heads, name):
    n = proj.shape[0]
    qk = n_heads * A_HEAD
    cw = 3 * qk
    ba_blk = (cw + qk) // LANES
    t = _pick(n, 256, 8)

    def body(u_ref, pba_ref, al_ref, dt_ref, dq_ref, dk_ref, dv_ref, dbb_ref, dgb_ref,
             du_ref, dba_ref, dal_ref, ddt_ref):
        i = pl.program_id(0)
        u = u_ref[...].astype(F32)
        pba = pba_ref[...]
        lane0 = lax.broadcasted_iota(jnp.int32, (t, A_HEAD), 1) == 0
        dba = jnp.zeros((t, LANES), F32)
        dal = jnp.zeros((1, LANES), F32)
        ddt = jnp.zeros((1, LANES), F32)
        for hd in range(n_heads):
            s0 = slice(hd * A_HEAD, (hd + 1) * A_HEAD)
            s1 = slice(qk + hd * A_HEAD, qk + (hd + 1) * A_HEAD)
            s2 = slice(2 * qk + hd * A_HEAD, 2 * qk + (hd + 1) * A_HEAD)
            fn = functools.partial(_gdn_head, head=hd, n_heads=n_heads)
            _, vjp = jax.vjp(fn, u[:, s0], u[:, s1], u[:, s2], pba, al_ref[...], dt_ref[...])
            cts = (dq_ref[:, s0], dk_ref[:, s0], dv_ref[:, s0],
                   jnp.where(lane0, dbb_ref[:, s0], 0.0), jnp.where(lane0, dgb_ref[:, s0], 0.0))
            duq, duk, duv, dpba, da, dd = vjp(cts)
            du_ref[:, s0] = duq
            du_ref[:, s1] = duk
            du_ref[:, s2] = duv
            dba = dba + dpba
            dal = dal + da
            ddt = ddt + dd
        dba_ref[...] = dba

        @pl.when(i == 0)
        def _():
            dal_ref[...] = jnp.zeros_like(dal_ref)
            ddt_ref[...] = jnp.zeros_like(ddt_ref)
        dal_ref[...] += dal
        ddt_ref[...] += ddt

    tok = pl.BlockSpec((t, qk), lambda i: (i, 0))
    row = pl.BlockSpec((1, LANES), lambda i: (0, 0))
    return pl.pallas_call(
        body, grid=(n // t,),
        in_specs=[pl.BlockSpec((t, cw), lambda i: (i, 0)),
                  pl.BlockSpec((t, LANES), lambda i: (i, ba_blk)), row, row,
                  tok, tok, tok, tok, tok],
        out_specs=[pl.BlockSpec((t, cw), lambda i: (i, 0)), pl.BlockSpec((t, LANES), lambda i: (i, 0)), row, row],
        out_shape=[jax.ShapeDtypeStruct((n, cw), F32), jax.ShapeDtypeStruct((n, LANES), F32),
                   jax.ShapeDtypeStruct((1, LANES), F32), jax.ShapeDtypeStruct((1, LANES), F32)],
        name=name, compiler_params=_cp(("arbitrary",)))(u, proj, alog, dtb, dq, dk, dv, dbb, dgb)


def conv_bwd(du, x, conv_w, width, name, rider=None):
    n, cw = du.shape
    t = _pick(n, 256, 8)
    last = n // t - 1

    chunks = _col_chunks(cw)
    tc = chunks[0].stop

    def body(du_ref, dnext_ref, x_ref, cw_ref, dx_ref, dcw_ref, ds):
        i = pl.program_id(0)

        @pl.when(i == 0)
        def _():
            dcw_ref[...] = jnp.zeros_like(dcw_ref)
        for cs in chunks:
            dx_ref[:, cs] = conv_bwd_tail(du_ref, dnext_ref, x_ref, cw_ref, dcw_ref, ds, width, t, i, last, cs)

    return host_call(
        body, grid=(n // t,),
        in_specs=[pl.BlockSpec((t, cw), lambda i: (i, 0)),
                  pl.BlockSpec((HALO, cw), _next_halo_map(t, n)),
                  pl.BlockSpec((t, cw), lambda i: (i, 0)),
                  pl.BlockSpec((width, cw), lambda i: (0, 0))],
        out_specs=[pl.BlockSpec((t, cw), lambda i: (i, 0)), pl.BlockSpec((width, cw), lambda i: (0, 0))],
        out_shape=[jax.ShapeDtypeStruct((n, cw), F32), jax.ShapeDtypeStruct((width, cw), F32)],
        scratch_shapes=[pltpu.VMEM((t + HALO, tc), F32)], name=name,
        sem=("arbitrary",), args=(du, du, x, conv_w), rider=rider)


def _b(x):
    return x.astype(BF16)


def _mm_nn(a, b):
    return _dot(_b(a), _b(b))


def _mm_nt(a, b):
    return _dot_nt(_b(a), _b(b))


def _mm_tn(a, b):
    return _dot_tn(_b(a), _b(b))


@jax.custom_vjp
def _mmg_nn(a, b):
    return _mm_nn(a, b)


_mmg_nn.defvjp(lambda a, b: (_mm_nn(a, b), (a, b)),
               lambda res, dc: (_mm_nt(dc, res[1]), _mm_tn(res[0], dc)))


@jax.custom_vjp
def _mmg_nt(a, b):
    return _mm_nt(a, b)


_mmg_nt.defvjp(lambda a, b: (_mm_nt(a, b), (a, b)),
               lambda res, dc: (_mm_nn(dc, res[1]), _mm_tn(dc, res[0])))


@jax.custom_vjp
def _mmg_tn(a, b):
    return _mm_tn(a, b)


_mmg_tn.defvjp(lambda a, b: (_mm_tn(a, b), (a, b)),
               lambda res, dc: (_mm_nt(res[1], dc), _mm_nn(res[0], dc)))


def _each(f, *lists):
    return [f(*a) for a in zip(*lists)]


def _unit_lower_inv(ms):
    c = ms[0].shape[0]
    eye = jnp.where(lax.broadcasted_iota(jnp.int32, (c, c), 0) == lax.broadcasted_iota(jnp.int32, (c, c), 1), 1.0, 0.0)
    xs = [eye - m for m in ms]
    pws = _each(_mm_nn, ms, ms)
    for it in range(5):
        xs = _each(lambda x, pw: x + _mm_nn(x, pw), xs, pws)
        if it < 4:
            pws = _each(_mm_nn, pws, pws)
    rs = _each(lambda m, x: eye - x - _hdot(m, x), ms, xs)
    return _each(lambda x, r: x + _mm_nn(x, r), xs, rs)


@jax.custom_vjp
def _saved_inv_g(ms, xs):
    return xs


_saved_inv_g.defvjp(lambda ms, xs: (xs, xs),
                    lambda xs, dxs: (_each(lambda t, x: -_mm_nt(t, x), _each(_mm_tn, xs, dxs), xs),
                                     [jnp.zeros_like(x) for x in xs]))


def _gdn_chunk(ops, state, q, k, v, bb, gb):
    nn, nt, tn, inv = ops
    c = CHUNK
    ri = lax.broadcasted_iota(jnp.int32, (c, c), 0)
    ci = lax.broadcasted_iota(jnp.int32, (c, c), 1)
    causal = ri >= ci
    strict = ri > ci
    tri = jnp.where(causal, 1.0, 0.0)
    gc = [_hdot(tri, g) for g in gb]
    decay = [jnp.where(causal, jnp.exp(jnp.where(causal, x[:, :c] - x.T[:c, :], 0.0)), 0.0) for x in gc]
    kb = _each(lambda a, b: a * b, k, bb)
    kk = _each(nt, kb, k)
    m = _each(lambda a, d: jnp.where(strict, a * d, 0.0), kk, decay)
    tinv = inv(m)
    egc = [jnp.exp(x) for x in gc]
    u = _each(nn, tinv, _each(lambda a, b: a * b, v, bb))
    w = _each(nn, tinv, _each(lambda a, b: a * b, kb, egc))
    attn = _each(lambda a, d: a * d, _each(nt, q, k), decay)
    glast = [jnp.sum(g, axis=0, keepdims=True) for g in gb]
    ws = _each(nn, w, state)
    v_new = _each(lambda a, b: a - b, u, ws)
    qs = _each(nn, _each(lambda a, b: a * b, q, egc), state)
    av = _each(nn, attn, v_new)
    o = _each(lambda a, b: a + b, qs, av)
    kv = _each(tn, _each(lambda a, gl, x: a * jnp.exp(gl - x), k, glast, gc), v_new)
    new_state = _each(lambda s, gl, a: s * jnp.exp(gl) + a, state, glast, kv)
    return o, new_state


def gdn_fwd(q, k, v, bb, gb, n_heads, name, rider=None):
    n, w = q.shape
    nc = n // CHUNK
    cb = min(8, nc)
    rows = cb * CHUNK

    def body(q_ref, k_ref, v_ref, b_ref, g_ref, o_ref, st_ref, ti_ref, s_scr):
        @pl.when(pl.program_id(0) == 0)
        def _():
            s_scr[...] = jnp.zeros_like(s_scr)

        def step(c, carry):
            sl = pl.ds(pl.multiple_of(c * CHUNK, CHUNK), CHUNK)
            lanes = [slice(hd * A_HEAD, (hd + 1) * A_HEAD) for hd in range(n_heads)]
            state = [s_scr[hd] for hd in range(n_heads)]
            inverses = []

            def inv(ms):
                inverses.extend(_unit_lower_inv(ms))
                return inverses

            o, new_state = _gdn_chunk((_mm_nn, _mm_nt, _mm_tn, inv), state,
                                      *[[r[sl, ls] for ls in lanes] for r in (q_ref, k_ref, v_ref, b_ref, g_ref)])
            for hd, ls in enumerate(lanes):
                st_ref[hd, pl.ds(c, 1)] = state[hd][None]
                ti_ref[hd, pl.ds(c, 1)] = inverses[hd].astype(BF16)[None]
                o_ref[sl, ls] = o[hd]
                s_scr[hd] = new_state[hd]
            return carry

        lax.fori_loop(0, cb, step, 0)

    tok = pl.BlockSpec((rows, w), lambda j: (j, 0))
    return host_call(
        body, grid=(nc // cb,),
        in_specs=[tok] * 5,
        out_specs=[tok, pl.BlockSpec((n_heads, cb, A_HEAD, A_HEAD), lambda j: (0, j, 0, 0)),
                   pl.BlockSpec((n_heads, cb, CHUNK, CHUNK), lambda j: (0, j, 0, 0))],
        out_shape=[jax.ShapeDtypeStruct(q.shape, F32), jax.ShapeDtypeStruct((n_heads, nc, A_HEAD, A_HEAD), F32),
                   jax.ShapeDtypeStruct((n_heads, nc, CHUNK, CHUNK), BF16)],
        scratch_shapes=[pltpu.VMEM((n_heads, A_HEAD, A_HEAD), F32)], name=name,
        sem=("arbitrary",), args=(q, k, v, bb, gb), rider=rider)


def gdn_bwd(q, k, v, bb, gb, states, tinv, do, n_heads, name, rider=None):
    n, w = q.shape
    nc = n // CHUNK
    cb = min(4, nc)
    rows = cb * CHUNK
    nblk = nc // cb

    def body(q_ref, k_ref, v_ref, b_ref, g_ref, st_ref, ti_ref, do_ref,
             dq_ref, dk_ref, dv_ref, db_ref, dg_ref, ds_scr):
        @pl.when(pl.program_id(0) == 0)
        def _():
            ds_scr[...] = jnp.zeros_like(ds_scr)

        def step(s, carry):
            c = cb - 1 - s
            sl = pl.ds(pl.multiple_of(c * CHUNK, CHUNK), CHUNK)
            for h0 in range(0, n_heads, GDN_BWD_HEADS):
                heads = list(range(h0, min(h0 + GDN_BWD_HEADS, n_heads)))
                lanes = [slice(hd * A_HEAD, (hd + 1) * A_HEAD) for hd in heads]
                state = [st_ref[hd, pl.ds(c, 1)][0] for hd in heads]
                saved = [ti_ref[hd, pl.ds(c, 1)][0].astype(F32) for hd in heads]
                chunk_fn = functools.partial(_gdn_chunk, (_mmg_nn, _mmg_nt, _mmg_tn, lambda ms: _saved_inv_g(ms, saved)))
                _, vjp = jax.vjp(chunk_fn, state, *[[r[sl, ls] for ls in lanes]
                                                    for r in (q_ref, k_ref, v_ref, b_ref, g_ref)])
                dstate, dq, dk, dv, dbb, dgb = vjp(([do_ref[sl, ls] for ls in lanes], [ds_scr[hd] for hd in heads]))
                for u, (hd, ls) in enumerate(zip(heads, lanes)):
                    ds_scr[hd] = dstate[u]
                    dq_ref[sl, ls] = dq[u]
                    dk_ref[sl, ls] = dk[u]
                    dv_ref[sl, ls] = dv[u]
                    db_ref[sl, ls] = jnp.broadcast_to(jnp.sum(dbb[u], axis=-1, keepdims=True), dbb[u].shape)
                    dg_ref[sl, ls] = jnp.broadcast_to(jnp.sum(dgb[u], axis=-1, keepdims=True), dgb[u].shape)
            return carry

        lax.fori_loop(0, cb, step, 0)

    tok = pl.BlockSpec((rows, w), lambda j: (nblk - 1 - j, 0))
    out = jax.ShapeDtypeStruct(q.shape, F32)
    return host_call(
        body, grid=(nblk,),
        in_specs=[tok] * 5 + [pl.BlockSpec((n_heads, cb, A_HEAD, A_HEAD), lambda j: (0, nblk - 1 - j, 0, 0)),
                              pl.BlockSpec((n_heads, cb, CHUNK, CHUNK), lambda j: (0, nblk - 1 - j, 0, 0)), tok],
        out_specs=[tok] * 5, out_shape=[out] * 5,
        scratch_shapes=[pltpu.VMEM((n_heads, A_HEAD, A_HEAD), F32)], name=name,
        sem=("arbitrary",), args=(q, k, v, bb, gb, states, tinv, do), rider=rider)


def _gdn_gate(oh, zh, w):
    r = lax.rsqrt(jnp.mean(oh * oh, axis=-1, keepdims=True) + EPS)
    return oh * r * w * (zh * _sig(zh))


def gdn_out(o, proj, out_norm, w_out, h, n_heads, name):
    n, vw = o.shape
    d = h.shape[1]
    z_blk = 3 * vw // vw
    t = _pick(n, 512, 8)

    def body(o_ref, z_ref, w_ref, wo_ref, h_ref, out_ref, y_ref):
        for hd in range(n_heads):
            s0 = slice(hd * A_HEAD, (hd + 1) * A_HEAD)
            y_ref[:, s0] = _gdn_gate(o_ref[:, s0], z_ref[:, s0], w_ref[...]).astype(BF16)
        out_ref[...] = h_ref[...] + _dot(y_ref[...], wo_ref[...])

    return pl.pallas_call(
        body, grid=(n // t,),
        in_specs=[pl.BlockSpec((t, vw), lambda i: (i, 0)),
                  pl.BlockSpec((t, vw), lambda i: (i, z_blk)),
                  pl.BlockSpec((1, A_HEAD), lambda i: (0, 0)),
                  _resident((vw, d), lambda i: (0, 0)),
                  pl.BlockSpec((t, d), lambda i: (i, 0))],
        out_specs=[pl.BlockSpec((t, d), lambda i: (i, 0)), pl.BlockSpec((t, vw), lambda i: (i, 0))],
        out_shape=[jax.ShapeDtypeStruct((n, d), F32), jax.ShapeDtypeStruct((n, vw), BF16)], name=name,
        compiler_params=_cp(("parallel",)))(o, proj, out_norm, w_out, h)


def gdn_out_bwd(dout, o, proj, out_norm, w_out, n_heads, name):
    n, vw = o.shape
    d = dout.shape[1]
    z_blk = 3
    t = _pick(n, 512, 8)

    def body(dout_ref, o_ref, z_ref, w_ref, wo_ref, do_ref, dz_ref, dw_ref):
        dy = _dot_nt(dout_ref[...].astype(BF16), wo_ref[...])
        dw = jnp.zeros((1, A_HEAD), F32)
        for hd in range(n_heads):
            s0 = slice(hd * A_HEAD, (hd + 1) * A_HEAD)
            _, vjp = jax.vjp(_gdn_gate, o_ref[:, s0], z_ref[:, s0], w_ref[...])
            doh, dzh, dwh = vjp(dy[:, s0])
            do_ref[:, s0] = doh
            dz_ref[:, s0] = dzh
            dw = dw + dwh

        @pl.when(pl.program_id(0) == 0)
        def _():
            dw_ref[...] = jnp.zeros_like(dw_ref)
        dw_ref[...] += dw

    tok = pl.BlockSpec((t, vw), lambda i: (i, 0))
    return pl.pallas_call(
        body, grid=(n // t,),
        in_specs=[pl.BlockSpec((t, d), lambda i: (i, 0)), tok,
                  pl.BlockSpec((t, vw), lambda i: (i, z_blk)),
                  pl.BlockSpec((1, A_HEAD), lambda i: (0, 0)),
                  _resident((vw, d), lambda i: (0, 0))],
        out_specs=[tok, tok, pl.BlockSpec((1, A_HEAD), lambda i: (0, 0))],
        out_shape=[jax.ShapeDtypeStruct((n, vw), F32), jax.ShapeDtypeStruct((n, vw), F32),
                   jax.ShapeDtypeStruct((1, A_HEAD), F32)], name=name,
        compiler_params=_cp(("arbitrary",)))(dout, o, proj, out_norm, w_out)


BIAS_LINE = 768
BIAS_TOP = BAND + CHUNK - 2


def _bias_line_onehot(clip, tbl_pad):
    r = lax.broadcasted_iota(jnp.int32, (tbl_pad, BIAS_LINE), 0)
    v = lax.broadcasted_iota(jnp.int32, (tbl_pad, BIAS_LINE), 1)
    idx = jnp.clip(BIAS_TOP - v - (CHUNK - 1), -clip, clip) + clip
    return jnp.where((r == idx) & (v <= BIAS_TOP), 1.0, 0.0)


def bias_expand(tbl, clip, name):
    nh, tp = tbl.shape

    def body(t_ref, o_ref):
        line = _hdot(t_ref[...], _bias_line_onehot(clip, tp))
        keep = lax.broadcasted_iota(jnp.int32, (nh, BAND_PAD), 1) < BAND
        for i in range(CHUNK):
            s = CHUNK - 1 - i
            rolled = line if s == 0 else pltpu.roll(line, BIAS_LINE - s, axis=1)
            o_ref[i] = jnp.where(keep, rolled[:, :BAND_PAD], NEG_INF)

    return pl.pallas_call(
        body, in_specs=[pl.BlockSpec(memory_space=pltpu.VMEM)], out_specs=pl.BlockSpec(memory_space=pltpu.VMEM),
        out_shape=jax.ShapeDtypeStruct((CHUNK, nh, BAND_PAD), F32), name=name, compiler_params=_cp())(tbl)


def bias_expand_bwd(dbias, clip, tp, name):
    _, nh, _ = dbias.shape

    def body(d_ref, o_ref):
        keep = lax.broadcasted_iota(jnp.int32, (nh, BAND_PAD), 1) < BAND
        pad = jnp.zeros((nh, BIAS_LINE - BAND_PAD), F32)
        acc = jnp.zeros((nh, BIAS_LINE), F32)
        for i in range(CHUNK):
            s = CHUNK - 1 - i
            d = jnp.concatenate([jnp.where(keep, d_ref[i], 0.0), pad], axis=1)
            acc = acc + (d if s == 0 else pltpu.roll(d, s, axis=1))
        o_ref[...] = _hdot_nt(acc, _bias_line_onehot(clip, tp))

    return pl.pallas_call(
        body, in_specs=[pl.BlockSpec(memory_space=pltpu.VMEM)], out_specs=pl.BlockSpec(memory_space=pltpu.VMEM),
        out_shape=jax.ShapeDtypeStruct((nh, tp), F32), name=name, compiler_params=_cp())(dbias)


ATT_TILE = LEFT_CHUNKS * CHUNK


ATT_GROUP = 8


def _att_softmax(s, bias, n_chunk):
    slot = lax.broadcasted_iota(jnp.int32, (1, s.shape[1]), 1)
    before_start = jnp.where(slot < (LEFT_CHUNKS - n_chunk) * CHUNK, NEG_INF, 0.0)
    s = s + bias + before_start
    p = jnp.exp(s - jnp.max(s, axis=-1, keepdims=True))
    return p / jnp.sum(p, axis=-1, keepdims=True)


def _att_specs(n_pairs):
    prev = lambda p, i: (jnp.maximum(i - 1, 0), p)
    cur = lambda p, i: (i, p)
    prev_v = lambda p, i: (jnp.maximum(i - 1, 0), n_pairs + p)
    cur_v = lambda p, i: (i, n_pairs + p)
    blk = (ATT_TILE, LANES)
    return [pl.BlockSpec(blk, prev), pl.BlockSpec(blk, cur), pl.BlockSpec(blk, prev_v), pl.BlockSpec(blk, cur_v)]


def _att_fill(kbuf, vbuf, kp_ref, kc_ref, vp_ref, vc_ref):
    t = ATT_TILE
    kbuf[0:t, :] = kp_ref[...].astype(BF16)
    kbuf[t:2 * t, :] = kc_ref[...].astype(BF16)
    kbuf[2 * t:, :] = jnp.zeros((CHUNK, LANES), BF16)
    vbuf[0:t, :] = vp_ref[...].astype(BF16)
    vbuf[t:2 * t, :] = vc_ref[...].astype(BF16)
    vbuf[2 * t:, :] = jnp.zeros((CHUNK, LANES), BF16)


def _stack_heads(x, first):
    return jnp.concatenate([jnp.where(first, x, 0.0), jnp.where(first, 0.0, x)], axis=0).astype(BF16)


def attn_fwd(qp, kv, bias, name, rider=None):
    n, bw = qp.shape
    n_pairs = bw // LANES
    t = ATT_TILE
    cpt = t // CHUNK

    def body(q_ref, kp_ref, kc_ref, vp_ref, vc_ref, b_ref, o_ref, kbuf, vbuf):
        i = pl.program_id(1)
        _att_fill(kbuf, vbuf, kp_ref, kc_ref, vp_ref, vc_ref)
        lane = lax.broadcasted_iota(jnp.int32, (CHUNK, LANES), 1)
        first = lane < B_HEAD
        for g0 in range(0, cpt, ATT_GROUP):
            chunks = list(range(g0, min(g0 + ATT_GROUP, cpt)))
            band = [slice(c * CHUNK, c * CHUNK + BAND_PAD) for c in chunks]
            q2 = [_stack_heads(q_ref[c * CHUNK:(c + 1) * CHUNK, :] * (B_HEAD ** -0.5), first) for c in chunks]
            s = [_dot_nt(q_u, kbuf[b_u, :]) for q_u, b_u in zip(q2, band)]
            p = [_att_softmax(s_u, b_ref[0], i * cpt + c) for s_u, c in zip(s, chunks)]
            o = [_dot(p_u.astype(BF16), vbuf[b_u, :]) for p_u, b_u in zip(p, band)]
            for o_u, c in zip(o, chunks):
                o_ref[c * CHUNK:(c + 1) * CHUNK, :] = jnp.where(first, o_u[:CHUNK], o_u[CHUNK:])

    return host_call(
        body, grid=(n_pairs, n // t),
        in_specs=[pl.BlockSpec((t, LANES), lambda p, i: (i, p))] + _att_specs(n_pairs)
        + [pl.BlockSpec((1, 2 * CHUNK, BAND_PAD), lambda p, i: (p, 0, 0))],
        out_specs=[pl.BlockSpec((t, LANES), lambda p, i: (i, p))],
        out_shape=[jax.ShapeDtypeStruct((n, bw), F32)],
        scratch_shapes=[pltpu.VMEM((2 * t + CHUNK, LANES), BF16), pltpu.VMEM((2 * t + CHUNK, LANES), BF16)],
        name=name, sem=("parallel", "parallel"), args=(qp, kv, kv, kv, kv, bias), rider=rider)


def attn_bwd(qp, kv, bias, d_o, dk_in, dv_in, name, rider=None):
    n, bw = qp.shape
    n_pairs = bw // LANES
    t = ATT_TILE
    cpt = t // CHUNK
    nt = n // t
    have_in = dk_in is not None
    scale = B_HEAD ** -0.5

    def body(*refs):
        q_ref, kp_ref, kc_ref, vp_ref, vc_ref, b_ref, do_ref = refs[:7]
        pos = 7
        if have_in:
            dki_ref, dvi_ref = refs[7:9]
            pos = 9
        dq_ref, dk_ref, dv_ref, db_ref, kbuf, vbuf, dkacc, dvacc = refs[pos:]
        j = pl.program_id(1)
        i = nt - 1 - j
        _att_fill(kbuf, vbuf, kp_ref, kc_ref, vp_ref, vc_ref)

        @pl.when(j == 0)
        def _():
            dkacc[...] = jnp.zeros_like(dkacc)
            dvacc[...] = jnp.zeros_like(dvacc)
            db_ref[...] = jnp.zeros_like(db_ref)

        @pl.when(j > 0)
        def _():
            dkacc[t:2 * t, :] = dkacc[0:t, :]
            dvacc[t:2 * t, :] = dvacc[0:t, :]
            dkacc[0:t, :] = jnp.zeros((t, LANES), F32)
            dvacc[0:t, :] = jnp.zeros((t, LANES), F32)

        lane = lax.broadcasted_iota(jnp.int32, (CHUNK, LANES), 1)
        first = lane < B_HEAD
        for g0 in range(0, cpt, ATT_GROUP):
            chunks = list(range(g0, min(g0 + ATT_GROUP, cpt)))
            rows = [slice(c * CHUNK, (c + 1) * CHUNK) for c in chunks]
            band = [slice(c * CHUNK, c * CHUNK + BAND_PAD) for c in chunks]
            q2 = [_stack_heads(q_ref[r, :] * scale, first) for r in rows]
            do2 = [_stack_heads(do_ref[r, :], first) for r in rows]
            s = [_dot_nt(q_u, kbuf[b_u, :]) for q_u, b_u in zip(q2, band)]
            dp = [_dot_nt(d_u, vbuf[b_u, :]) for d_u, b_u in zip(do2, band)]
            p = [_att_softmax(s_u, b_ref[0], i * cpt + c) for s_u, c in zip(s, chunks)]
            ds = [p_u * (dp_u - jnp.sum(dp_u * p_u, axis=-1, keepdims=True)) for p_u, dp_u in zip(p, dp)]
            dsb = [d_u.astype(BF16) for d_u in ds]
            dv = [_dot_tn(p_u.astype(BF16), d_u) for p_u, d_u in zip(p, do2)]
            dq = [_dot(d_u, kbuf[b_u, :]) * scale for d_u, b_u in zip(dsb, band)]
            dk = [_dot_tn(d_u, q_u) for d_u, q_u in zip(dsb, q2)]
            db_ref[0] += functools.reduce(lambda a, b: a + b, ds)
            for r in range(chunks[0], chunks[-1] + BAND // CHUNK):
                terms = [(u, r - c) for u, c in enumerate(chunks) if 0 <= r - c < BAND // CHUNK]
                blk = slice(r * CHUNK, (r + 1) * CHUNK)
                dvacc[blk, :] += functools.reduce(lambda a, b: a + b, [dv[u][o * CHUNK:(o + 1) * CHUNK] for u, o in terms])
                dkacc[blk, :] += functools.reduce(lambda a, b: a + b, [dk[u][o * CHUNK:(o + 1) * CHUNK] for u, o in terms])
            for u in range(len(chunks)):
                dq_ref[rows[u], :] = jnp.where(first, dq[u][:CHUNK], dq[u][CHUNK:])

        if have_in:
            dk_ref[...] = dkacc[t:2 * t, :] + dki_ref[...]
            dv_ref[...] = dvacc[t:2 * t, :] + dvi_ref[...]
        else:
            dk_ref[...] = dkacc[t:2 * t, :]
            dv_ref[...] = dvacc[t:2 * t, :]

    rev = lambda p, j: (nt - 1 - j, p)
    tok = pl.BlockSpec((t, LANES), rev)
    kv_specs = [pl.BlockSpec((t, LANES), lambda p, j: (jnp.maximum(nt - 2 - j, 0), p)),
                pl.BlockSpec((t, LANES), rev),
                pl.BlockSpec((t, LANES), lambda p, j: (jnp.maximum(nt - 2 - j, 0), n_pairs + p)),
                pl.BlockSpec((t, LANES), lambda p, j: (nt - 1 - j, n_pairs + p))]
    in_specs = [tok] + kv_specs + [pl.BlockSpec((1, 2 * CHUNK, BAND_PAD), lambda p, j: (p, 0, 0)), tok]
    args = [qp, kv, kv, kv, kv, bias, d_o]
    if have_in:
        in_specs += [tok, tok]
        args += [dk_in, dv_in]
    out = jax.ShapeDtypeStruct((n, bw), F32)
    return host_call(
        body, grid=(n_pairs, nt), in_specs=in_specs,
        out_specs=[tok, tok, tok, pl.BlockSpec((1, 2 * CHUNK, BAND_PAD), lambda p, j: (p, 0, 0))],
        out_shape=[out, out, out, jax.ShapeDtypeStruct((n_pairs, 2 * CHUNK, BAND_PAD), F32)],
        scratch_shapes=[pltpu.VMEM((2 * t + CHUNK, LANES), BF16), pltpu.VMEM((2 * t + CHUNK, LANES), BF16),
                        pltpu.VMEM((2 * t + CHUNK, LANES), F32), pltpu.VMEM((2 * t + CHUNK, LANES), F32)],
        name=name, sem=("parallel", "arbitrary"), args=args, rider=rider)


def adamw(w, gstack, m, v, name):
    r, c = w.shape
    s = gstack.shape[0]
    tr = _pick(r, 512, 8)

    def body(w_ref, g_ref, m_ref, v_ref, go_ref, d_ref, mo_ref, vo_ref):
        g = g_ref[0].astype(F32)
        for k in range(1, s):
            g = g + g_ref[k].astype(F32)
        mn = ADAM_B1 * m_ref[...] + (1.0 - ADAM_B1) * g
        vn = ADAM_B2 * v_ref[...] + (1.0 - ADAM_B2) * (g * g)
        m_hat = mn / (1.0 - ADAM_B1 ** ADAM_STEP)
        v_hat = vn / (1.0 - ADAM_B2 ** ADAM_STEP)
        go_ref[...] = g
        d_ref[...] = -ADAM_LR * (m_hat / (jnp.sqrt(v_hat) + ADAM_EPS) + ADAM_WD * w_ref[...])
        mo_ref[...] = mn
        vo_ref[...] = vn

    blk = pl.BlockSpec((tr, c), lambda i: (i, 0))
    out = jax.ShapeDtypeStruct((r, c), F32)
    return pl.pallas_call(
        body, grid=(r // tr,),
        in_specs=[blk, pl.BlockSpec((s, tr, c), lambda i: (0, i, 0)), blk, blk],
        out_specs=[blk] * 4, out_shape=[out] * 4, name=name,
        compiler_params=_cp(("parallel",)))(w, gstack, m, v)


def _place():
    x, y, c = lax.axis_index("x"), lax.axis_index("y"), lax.axis_index("c")
    chips = [(1 - x, y), (x, 1 - y), (1 - x, 1 - y)]
    return x, y, c, chips


def _ag_copy(outs, send_sems, recv_sems, t, k, block, to, src=None):
    def slot(dev):
        return outs[t].at[4 * dev[0] + 2 * dev[1] + dev[2]]
    return pltpu.make_async_remote_copy(
        src_ref=slot(block) if src is None else src, dst_ref=slot(block),
        send_sem=send_sems.at[7 * t + k], recv_sem=recv_sems.at[7 * t + k], device_id=to, device_id_type=MESH)


def _ag_start(ins, outs, send_sems, recv_sems, local_sems):
    x, y, c, chips = _place()
    me = (x, y, c)
    for t in range(len(ins)):
        pltpu.make_async_copy(ins[t], outs[t].at[4 * x + 2 * y + c], local_sems.at[t]).start()
        _ag_copy(outs, send_sems, recv_sems, t, 0, me, (x, y, 1 - c), src=ins[t]).start()
        for j, chip in enumerate(chips):
            _ag_copy(outs, send_sems, recv_sems, t, 1 + j, me, (*chip, c), src=ins[t]).start()


def _ag_finish(ins, outs, send_sems, recv_sems, local_sems):
    x, y, c, chips = _place()
    me, sibling = (x, y, c), (x, y, 1 - c)
    nt = len(ins)
    for t in range(nt):
        for j, chip in enumerate(chips):
            _ag_copy(outs, send_sems, recv_sems, t, 1 + j, (*chip, c), me).wait_recv()
            _ag_copy(outs, send_sems, recv_sems, t, 4 + j, (*chip, c), sibling).start()
    for t in range(nt):
        _ag_copy(outs, send_sems, recv_sems, t, 0, sibling, me).wait_recv()
        for j, chip in enumerate(chips):
            _ag_copy(outs, send_sems, recv_sems, t, 4 + j, (*chip, 1 - c), me).wait_recv()
    for t in range(nt):
        _ag_copy(outs, send_sems, recv_sems, t, 0, me, sibling, src=ins[t]).wait_send()
        for j, chip in enumerate(chips):
            _ag_copy(outs, send_sems, recv_sems, t, 1 + j, me, (*chip, c), src=ins[t]).wait_send()
            _ag_copy(outs, send_sems, recv_sems, t, 4 + j, (*chip, c), sibling).wait_send()
        pltpu.make_async_copy(ins[t], outs[t].at[4 * x + 2 * y + c], local_sems.at[t]).wait()


def _rs_a_copy(ins, outs, send_sems, recv_sems, t, q):
    x, y, c, _ = _place()
    return pltpu.make_async_remote_copy(
        src_ref=ins[t].at[2 * q + (1 - c)], dst_ref=outs[t].at[q],
        send_sem=send_sems.at[4 * t + q], recv_sem=recv_sems.at[4 * t + q],
        device_id=(x, y, 1 - c), device_id_type=MESH)


def _rs_a_start(ins, outs, send_sems, recv_sems, local_sems):
    for t in range(len(ins)):
        for q in range(4):
            _rs_a_copy(ins, outs, send_sems, recv_sems, t, q).start()


def _rs_a_finish(ins, outs, send_sems, recv_sems, local_sems):
    for t in range(len(ins)):
        for q in range(4):
            _rs_a_copy(ins, outs, send_sems, recv_sems, t, q).wait_recv()
    for t in range(len(ins)):
        for q in range(4):
            _rs_a_copy(ins, outs, send_sems, recv_sems, t, q).wait_send()


def _rs_b_copy(ins, outs, send_sems, recv_sems, t, j, sending):
    x, y, c, chips = _place()
    mine, other = 2 * x + y, 2 * chips[j][0] + chips[j][1]
    return pltpu.make_async_remote_copy(
        src_ref=ins[t].at[other if sending else mine], dst_ref=outs[t].at[mine if sending else other],
        send_sem=send_sems.at[3 * t + j], recv_sem=recv_sems.at[3 * t + j],
        device_id=(*chips[j], c), device_id_type=MESH)


def _rs_b_start(ins, outs, send_sems, recv_sems, local_sems):
    x, y, _, _ = _place()
    for t in range(len(ins)):
        for j in range(3):
            _rs_b_copy(ins, outs, send_sems, recv_sems, t, j, True).start()
        pltpu.make_async_copy(ins[t].at[2 * x + y], outs[t].at[2 * x + y], local_sems.at[t]).start()


def _rs_b_finish(ins, outs, send_sems, recv_sems, local_sems):
    x, y, _, _ = _place()
    for t in range(len(ins)):
        for j in range(3):
            _rs_b_copy(ins, outs, send_sems, recv_sems, t, j, False).wait_recv()
    for t in range(len(ins)):
        for j in range(3):
            _rs_b_copy(ins, outs, send_sems, recv_sems, t, j, True).wait_send()
        pltpu.make_async_copy(ins[t].at[2 * x + y], outs[t].at[2 * x + y], local_sems.at[t]).wait()


_EXCHANGES = {
    "all_gather": (7, lambda a: (N_DEV, *a.shape), _ag_start, _ag_finish),
    "rs_sibling": (4, lambda a: (4, *a.shape[1:]), _rs_a_start, _rs_a_finish),
    "rs_chips": (3, lambda a: a.shape, _rs_b_start, _rs_b_finish),
}


def _exchange_parts(kind, arrays):
    per, shape_of, start, finish = _EXCHANGES[kind]
    n = len(arrays)
    out_shape = [jax.ShapeDtypeStruct(shape_of(a), a.dtype) for a in arrays]
    sems = [pltpu.SemaphoreType.DMA((per * n,)), pltpu.SemaphoreType.DMA((per * n,)), pltpu.SemaphoreType.DMA((n,))]
    return out_shape, sems, start, finish


def exchange(kind, arrays, name):
    n = len(arrays)
    out_shape, sems, start, finish = _exchange_parts(kind, arrays)
    any_spec = pl.BlockSpec(memory_space=pl.ANY)

    def body(*refs):
        ins, outs, sem_refs = refs[:n], refs[n:2 * n], refs[2 * n:]
        start(ins, outs, *sem_refs)
        finish(ins, outs, *sem_refs)

    return pl.pallas_call(body, in_specs=[any_spec] * n, out_specs=[any_spec] * n, out_shape=out_shape,
                          scratch_shapes=sems, name=name)(*arrays)


def host_call(body, *, grid, in_specs, out_specs, out_shape, scratch_shapes, args, name, sem, rider=None):
    if not rider:
        outs = pl.pallas_call(body, grid=grid, in_specs=in_specs, out_specs=out_specs, out_shape=out_shape,
                              scratch_shapes=scratch_shapes, name=name, compiler_params=_cp(sem))(*args)
        return outs, None
    riders = [rider] if isinstance(rider, tuple) else list(rider)
    arrays = [a for _, arrs in riders for a in arrs]
    parts = [_exchange_parts(kind, arrs) for kind, arrs in riders]
    counts = [len(arrs) for _, arrs in riders]
    nr, ni, no, ns = len(arrays), len(in_specs), len(out_specs), len(scratch_shapes)
    any_spec = pl.BlockSpec(memory_space=pl.ANY)

    def wrapped(*refs):
        ins, r_ins = refs[:ni], refs[ni:ni + nr]
        outs, r_outs = refs[ni + nr:ni + nr + no], refs[ni + nr + no:ni + 2 * nr + no]
        scratch, sem_refs = refs[ni + 2 * nr + no:ni + 2 * nr + no + ns], refs[ni + 2 * nr + no + ns:]
        first = pl.program_id(0) == 0
        last = pl.program_id(0) == grid[0] - 1
        for ax in range(1, len(grid)):
            first = first & (pl.program_id(ax) == 0)
            last = last & (pl.program_id(ax) == grid[ax] - 1)

        def each(which):
            pos = 0
            for e, (cnt, part) in enumerate(zip(counts, parts)):
                part[which](r_ins[pos:pos + cnt], r_outs[pos:pos + cnt], *sem_refs[3 * e:3 * e + 3])
                pos += cnt

        @pl.when(first)
        def _():
            each(2)
        body(*ins, *outs, *scratch)

        @pl.when(last)
        def _():
            each(3)

    outs = pl.pallas_call(
        wrapped, grid=grid, in_specs=list(in_specs) + [any_spec] * nr, out_specs=list(out_specs) + [any_spec] * nr,
        out_shape=list(out_shape) + [s for p in parts for s in p[0]],
        scratch_shapes=list(scratch_shapes) + [s for p in parts for s in p[1]], name=name,
        compiler_params=_cp(("arbitrary",) * len(grid)))(*args, *arrays)
    got, pos = [], no
    for cnt in counts:
        got.append(outs[pos:pos + cnt])
        pos += cnt
    return outs[:no], (got[0] if isinstance(rider, tuple) else got)


def pair_add(g8, recv, c_idx, name):
    _, r, c = g8.shape
    tr = _pick(r, 512, 8)

    def body(c_ref, g_ref, r_ref, o_ref):
        o_ref[...] = (g_ref[...] + r_ref[...]).astype(BF16)

    return pl.pallas_call(
        body,
        grid_spec=pltpu.PrefetchScalarGridSpec(
            num_scalar_prefetch=1, grid=(4, r // tr),
            in_specs=[pl.BlockSpec((1, tr, c), lambda q, i, cr: (2 * q + cr[0], i, 0)),
                      pl.BlockSpec((1, tr, c), lambda q, i, cr: (q, i, 0))],
            out_specs=pl.BlockSpec((1, tr, c), lambda q, i, cr: (q, i, 0))),
        out_shape=jax.ShapeDtypeStruct((4, r, c), BF16), name=name,
        compiler_params=_cp(("parallel", "parallel")))(c_idx, g8, recv)


def all_reduce_small(pack, name):
    r, c = pack.shape

    def body(x_ref, o_ref, buf, send_sems, recv_sems, local_sem):
        x, y, cc, chips = _place()
        me, sibling = (x, y, cc), (x, y, 1 - cc)

        def slot(dev):
            return buf.at[4 * dev[0] + 2 * dev[1] + dev[2]]

        def copy(k, block, to, src=None):
            return pltpu.make_async_remote_copy(
                src_ref=slot(block) if src is None else src, dst_ref=slot(block),
                send_sem=send_sems.at[k], recv_sem=recv_sems.at[k], device_id=to, device_id_type=MESH)

        mine = pltpu.make_async_copy(x_ref, slot(me), local_sem)
        mine.start()
        first = [copy(0, me, sibling, src=x_ref)]
        first += [copy(1 + j, me, (*chip, cc), src=x_ref) for j, chip in enumerate(chips)]
        for cp in first:
            cp.start()
        passed = [copy(4 + j, (*chip, cc), sibling) for j, chip in enumerate(chips)]
        for j, chip in enumerate(chips):
            copy(1 + j, (*chip, cc), me).wait_recv()
            passed[j].start()
        copy(0, sibling, me).wait_recv()
        for j, chip in enumerate(chips):
            copy(4 + j, (*chip, 1 - cc), me).wait_recv()
        for cp in first + passed:
            cp.wait_send()
        mine.wait()
        acc = buf[0]
        for k in range(1, N_DEV):
            acc = acc + buf[k]
        o_ref[...] = acc

    return pl.pallas_call(
        body, in_specs=[pl.BlockSpec(memory_space=pltpu.VMEM)],
        out_specs=pl.BlockSpec(memory_space=pltpu.VMEM),
        out_shape=jax.ShapeDtypeStruct((r, c), F32),
        scratch_shapes=[pltpu.VMEM((N_DEV, r, c), F32), pltpu.SemaphoreType.DMA((7,)),
                        pltpu.SemaphoreType.DMA((7,)), pltpu.SemaphoreType.DMA],
        name=name, compiler_params=_cp())(pack)


def _row(v):
    return v.reshape(1, -1)


def _lane_row(vals, offset):
    return jnp.pad(vals, (offset, LANES - offset - vals.shape[0])).reshape(1, LANES)


def _bias_to_pairs(b):
    i, nh, bp = b.shape
    return b.transpose(1, 0, 2).reshape(nh // 2, 2 * i, bp)


def _bias_from_pairs(b):
    p, i2, bp = b.shape
    return b.reshape(2 * p, i2 // 2, bp).transpose(1, 0, 2)


class LocalWeights:
    def __init__(self, W):
        self.W = W
        self.grads = {}

    def big(self, l, la):
        W = self.W
        out = {"f_w_up": W["f_w_up"][l].T, "f_w_down": W["f_w_down"][l]}
        if l < la:
            out.update(a_w_in=W["a_w_in"][l], a_w_out=W["a_w_out"][l])
        else:
            out.update(b_w_q=W["b_w_q"][l - la], b_w_out=W["b_w_out"][l - la])
        if l == la:
            out["w_kv"] = W["w_kv"].T
        return out

    def prep_rider(self, l):
        return None

    def prep_got(self, l, got):
        pass

    def fwd_rider(self, l):
        return None

    def fwd_got(self, l, got):
        pass

    def bwd_rider_a(self, l):
        return None

    def bwd_got_a(self, l, got):
        pass

    def bwd_rider_b(self, l):
        return None

    def bwd_got_b(self, l, got):
        pass

    def bwd_rider_c(self, l):
        return None

    def bwd_got_c(self, l, got):
        pass

    def ffn_grads_ready(self, l, grads):
        pass

    def grads_ready(self, l, grads):
        for k_, g in grads.items():
            self.grads.setdefault(k_, {})[l] = g

    def stacked(self):
        out = {k_: (jnp.stack([v_[l] for l in sorted(v_)]) if k_ != "w_kv" else next(iter(v_.values())))
               for k_, v_ in self.grads.items()}
        out["f_w_up"] = jnp.swapaxes(out["f_w_up"], 1, 2)
        out["w_kv"] = out["w_kv"].T
        return out


def _named(name, l, rider):
    return name if rider is None else f"{name}_x{l}"


def local_step(x, target, W, comm=None):
    comm = LocalWeights(W) if comm is None else comm
    n, d = x.shape
    la, ha = W["a_A_log"].shape
    lb, hb, tbl = W["b_rel_bias"].shape
    depth = W["f_norm"].shape[0]
    clip = (tbl - 1) // 2
    tp = -(-tbl // LANES) * LANES
    qk = ha * A_HEAD
    cw = 3 * qk
    bw = hb * B_HEAD
    a_in = cw + qk + 2 * ha

    h = x
    saves = []
    kv = h_kv = w_kv = None
    for l in range(depth):
        big = comm.big(l, la)
        sv = {"h_in": h, "big": big}
        rider = comm.fwd_rider(l)
        if l < la:
            alog = _lane_row(W["a_A_log"][l], ha)
            dtb = _lane_row(W["a_dt_bias"][l], ha)
            proj = norm_matmul(h, _row(W["a_norm"][l]), big["a_w_in"], "a_in_proj")
            early = comm.prep_rider(l)
            (q, k, v, bb, gb, u), got = gdn_prep(proj, W["a_conv"][l], alog, dtb, ha, _named("gdn_prep", l, early), early)
            comm.prep_got(l, got)
            (o, states, tinv), got = gdn_fwd(q, k, v, bb, gb, ha, _named("gdn_fwd", l, rider), rider)
            h, y = gdn_out(o, proj, _row(W["a_out_norm"][l]), big["a_w_out"], h, ha, "gdn_out")
            sv.update(proj=proj, q=q, k=k, v=v, bb=bb, gb=gb, u=u, states=states, tinv=tinv, o=o, y=y, alog=alog, dtb=dtb)
        else:
            j = l - la
            if j == 0:
                h_kv, w_kv = h, big["w_kv"]
                kv = norm_matmul(h, _row(W["kv_norm"]), w_kv, "kv_proj", w_t=True)
            qp = norm_matmul(h, _row(W["b_norm"][j]), big["b_w_q"], "b_q_proj")
            tblp = jnp.pad(W["b_rel_bias"][j], ((0, 0), (0, tp - tbl)))
            bias = _bias_to_pairs(bias_expand(tblp, clip, "bias_expand"))
            (o,), got = attn_fwd(qp, kv, bias, _named("attn_fwd", l, rider), rider)
            h = matmul_res(o, big["b_w_out"], h, "b_out_proj")
            sv.update(qp=qp, bias=bias, o=o)
        comm.fwd_got(l, got)
        sv["h_mid"] = h
        up = norm_matmul(h, _row(W["f_norm"][l]), big["f_w_up"], "f_up_proj", out_dtype=BF16, w_t=True)
        h, act, hc = ffn_act_down(up, W["f_conv"][l], _row(W["f_conv_b"][l]), big["f_w_down"], h, "ffn_act_down")
        sv.update(up=up, act=act, hc=hc)
        saves.append(sv)

    loss, dh, d_final = loss_head(h, _row(W["final_norm"]), target)

    G = {k_: [None] * (la if k_.startswith("a_") else lb if k_.startswith("b_") else depth)
         for k_ in ("a_norm", "a_conv", "a_A_log", "a_dt_bias", "a_out_norm",
                    "b_norm", "b_rel_bias", "f_norm", "f_conv", "f_conv_b")}
    G["final_norm"] = d_final[0]
    dk_acc = dv_acc = None
    for l in reversed(range(depth)):
        sv = saves[l]
        big = sv["big"]
        gbig = {}
        dhc, dcb = ffn_bwd_act(dh, sv["hc"], big["f_w_down"], "ffn_bwd_act")
        gbig["f_w_down"] = matmul_tn(sv["act"], dh, "f_down_wgrad")
        G["f_conv_b"][l] = dcb[0]
        rider = comm.bwd_rider_a(l)
        (dh, dup, dcw, dg), got = ffn_bwd_up(dhc, sv["up"], W["f_conv"][l], big["f_w_up"], sv["h_mid"], dh,
                                             _row(W["f_norm"][l]), _named("ffn_bwd_up", l, rider), rider)
        comm.bwd_got_a(l, got)
        G["f_conv"][l] = dcw
        G["f_norm"][l] = dg[0]
        gbig["f_w_up"] = norm_matmul_tn(sv["h_mid"], _row(W["f_norm"][l]), dup, "f_up_wgrad", transposed=True)
        comm.ffn_grads_ready(l, gbig)
        rider = comm.bwd_rider_b(l)
        if l < la:
            w_in = big["a_w_in"]
            do, dz, dwn = gdn_out_bwd(dh, sv["o"], sv["proj"], _row(W["a_out_norm"][l]), big["a_w_out"], ha, "gdn_out_bwd")
            G["a_out_norm"][l] = dwn[0]
            gbig["a_w_out"] = matmul_tn(sv["y"], dh, "a_out_wgrad")
            (dq, dk, dv, dbb, dgb), got = gdn_bwd(sv["q"], sv["k"], sv["v"], sv["bb"], sv["gb"], sv["states"], sv["tinv"], do, ha,
                                                  _named("gdn_bwd", l, rider), rider)
            comm.bwd_got_b(l, got)
            du, dba, dal, ddt = gdn_prep_bwd(sv["proj"], sv["u"], sv["alog"], sv["dtb"],
                                             dq, dk, dv, dbb, dgb, ha, "gdn_prep_bwd")
            G["a_A_log"][l] = dal[0, ha:2 * ha]
            G["a_dt_bias"][l] = ddt[0, ha:2 * ha]
            rider = comm.bwd_rider_c(l)
            (dqkv, dconv), got = conv_bwd(du, sv["proj"], W["a_conv"][l], A_CONV, _named("gdn_conv_bwd", l, rider), rider)
            if rider is not None:
                comm.bwd_got_c(l, got)
            G["a_conv"][l] = dconv
            gam = _row(W["a_norm"][l])
            pieces = [(dqkv, w_in[:, :cw]), (dz, w_in[:, cw:cw + qk]), (dba, w_in[:, cw + qk:])]
            gbig["a_w_in"] = jnp.concatenate(
                [norm_matmul_tn(sv["h_in"], gam, dqkv, "a_in_wgrad_qkv"),
                 norm_matmul_tn(sv["h_in"], gam, dz, "a_in_wgrad_z"),
                 norm_matmul_tn(sv["h_in"], gam, dba, "a_in_wgrad_ba")[:, :2 * ha]], axis=1)
            dh, dg = dx_norm_bwd(dh, sv["h_in"], gam, pieces, "a_in_dx")
            G["a_norm"][l] = dg[0]
        else:
            j = l - la
            d_o = matmul_nt(dh, big["b_w_out"], "b_out_dx")
            gbig["b_w_out"] = matmul_tn(sv["o"], dh, "b_out_wgrad")
            (dq, dk_acc, dv_acc, dbias), got = attn_bwd(
                sv["qp"], kv, sv["bias"], d_o, dk_acc, dv_acc,
                _named("attn_bwd" if dk_acc is None else "attn_bwd_acc", l, rider), rider)
            comm.bwd_got_b(l, got)
            G["b_rel_bias"][j] = bias_expand_bwd(_bias_from_pairs(dbias), clip, tp, "bias_expand_bwd")[:, :tbl]
            gam = _row(W["b_norm"][j])
            gbig["b_w_q"] = norm_matmul_tn(sv["h_in"], gam, dq, "b_q_wgrad")
            dh, dg = dx_norm_bwd(dh, sv["h_in"], gam, [(dq, big["b_w_q"])], "b_q_dx")
            G["b_norm"][j] = dg[0]
            if j == 0:
                gam = _row(W["kv_norm"])
                gbig["w_kv"] = jnp.concatenate([norm_matmul_tn(h_kv, gam, dk_acc, "kv_wgrad_k", transposed=True),
                                                norm_matmul_tn(h_kv, gam, dv_acc, "kv_wgrad_v", transposed=True)], axis=0)
                dh, dg = dx_norm_bwd(dh, h_kv, gam, [(dk_acc, w_kv[:bw]), (dv_acc, w_kv[bw:])], "kv_dx", w_t=True)
                G["kv_norm"] = dg[0]
        comm.grads_ready(l, gbig)
    out = {k_: (jnp.stack(v_) if isinstance(v_, list) else v_) for k_, v_ in G.items()}
    return loss[0, 0], dh, out, comm


WEIGHTS = ["a_norm", "a_w_in", "a_conv", "a_A_log", "a_dt_bias", "a_out_norm", "a_w_out", "kv_norm", "w_kv",
           "b_norm", "b_w_q", "b_rel_bias", "b_w_out", "f_norm", "f_w_up", "f_conv", "f_conv_b", "f_w_down",
           "final_norm"]
SHARD_AXIS = {"a_norm": 1, "a_w_in": 2, "a_conv": 2, "a_w_out": 1, "w_kv": 1, "b_w_q": 1, "b_w_out": 1,
              "f_w_up": 2, "f_conv": 2, "f_w_down": 1}
BIG = ["a_w_in", "a_w_out", "w_kv", "b_w_q", "b_w_out", "f_w_up", "f_w_down"]
SMALL_SHARDED = ["a_norm", "a_conv", "f_conv"]
TRANSPOSED = ("f_w_up", "w_kv")


def _t_view(k, a):
    return jnp.swapaxes(a, -1, -2) if k in TRANSPOSED else a


def _unstack(g, axis):
    if axis == 0:
        return g.reshape(-1, *g.shape[2:])
    return jnp.concatenate([g[i] for i in range(N_DEV)], axis=axis)


def _to_blocks(full, axis):
    if axis == 0:
        return full.reshape(N_DEV, -1, full.shape[-1])
    return jnp.stack(jnp.split(full, N_DEV, axis=axis))


def _pack(arrs):
    flat = []
    for a in arrs:
        f = a.reshape(-1)
        flat.append(jnp.pad(f, (0, (-f.shape[0]) % LANES)))
    f = jnp.concatenate(flat)
    f = jnp.pad(f, (0, (-f.shape[0]) % (8 * LANES)))
    return f.reshape(-1, LANES)


def _unpack(pack, shapes):
    flat = pack.reshape(-1)
    out, pos = [], 0
    for s in shapes:
        sz = math.prod(s)
        out.append(flat[pos:pos + sz].reshape(s))
        pos += sz + (-sz) % LANES
    return out


def _as2d(a):
    return a.reshape(1, -1) if a.ndim == 1 else a.reshape(-1, a.shape[-1])


def kernel(x, a_norm, a_w_in, a_conv, a_A_log, a_dt_bias, a_out_norm, a_w_out, kv_norm, w_kv, b_norm, b_w_q, b_rel_bias, b_w_out, f_norm, f_w_up, f_conv, f_conv_b, f_w_down, final_norm, loss_target, m_a_norm, m_a_w_in, m_a_conv, m_a_A_log, m_a_dt_bias, m_a_out_norm, m_a_w_out, m_kv_norm, m_w_kv, m_b_norm, m_b_w_q, m_b_rel_bias, m_b_w_out, m_f_norm, m_f_w_up, m_f_conv, m_f_conv_b, m_f_w_down, m_final_norm, v_a_norm, v_a_w_in, v_a_conv, v_a_A_log, v_a_dt_bias, v_a_out_norm, v_a_w_out, v_kv_norm, v_w_kv, v_b_norm, v_b_w_q, v_b_rel_bias, v_b_w_out, v_f_norm, v_f_w_up, v_f_conv, v_f_conv_b, v_f_w_down, v_final_norm):
    w = dict(a_norm=a_norm, a_w_in=a_w_in, a_conv=a_conv, a_A_log=a_A_log, a_dt_bias=a_dt_bias,
             a_out_norm=a_out_norm, a_w_out=a_w_out, kv_norm=kv_norm, w_kv=w_kv, b_norm=b_norm, b_w_q=b_w_q,
             b_rel_bias=b_rel_bias, b_w_out=b_w_out, f_norm=f_norm, f_w_up=f_w_up, f_conv=f_conv,
             f_conv_b=f_conv_b, f_w_down=f_w_down, final_norm=final_norm)
    mom = dict(a_norm=m_a_norm, a_w_in=m_a_w_in, a_conv=m_a_conv, a_A_log=m_a_A_log, a_dt_bias=m_a_dt_bias,
               a_out_norm=m_a_out_norm, a_w_out=m_a_w_out, kv_norm=m_kv_norm, w_kv=m_w_kv, b_norm=m_b_norm,
               b_w_q=m_b_w_q, b_rel_bias=m_b_rel_bias, b_w_out=m_b_w_out, f_norm=m_f_norm, f_w_up=m_f_w_up,
               f_conv=m_f_conv, f_conv_b=m_f_conv_b, f_w_down=m_f_w_down, final_norm=m_final_norm)
    var = dict(a_norm=v_a_norm, a_w_in=v_a_w_in, a_conv=v_a_conv, a_A_log=v_a_A_log, a_dt_bias=v_a_dt_bias,
               a_out_norm=v_a_out_norm, a_w_out=v_a_w_out, kv_norm=v_kv_norm, w_kv=v_w_kv, b_norm=v_b_norm,
               b_w_q=v_b_w_q, b_rel_bias=v_b_rel_bias, b_w_out=v_b_w_out, f_norm=v_f_norm, f_w_up=v_f_w_up,
               f_conv=v_f_conv, f_conv_b=v_f_conv_b, f_w_down=v_f_w_down, final_norm=v_final_norm)
    me = 4 * lax.axis_index("x") + 2 * lax.axis_index("y") + lax.axis_index("c")

    la, depth = a_A_log.shape[0], f_norm.shape[0]
    c_idx = lax.axis_index("c").astype(jnp.int32).reshape(1)
    shard_bf16 = {k: _t_view(k, w[k]).astype(BF16) for k in BIG}
    blk_axis = {k: 0 if k in TRANSPOSED else SHARD_AXIS[k] - (k != "w_kv") for k in BIG}

    class Sharded(LocalWeights):
        def __init__(self):
            super().__init__(w)
            self.full = {}
            self.stacks = {}
            self.pending = None
            self.parts = None

        def names(self, l):
            out = ["a_w_in", "a_w_out"] if l < la else ["b_w_q", "b_w_out"]
            return out + ["f_w_up", "f_w_down"] + (["w_kv"] if l == la else [])

        def index(self, k, l):
            return None if k == "w_kv" else (l - la if k.startswith("b_") else l)

        def shards(self, l, names=None):
            return [shard_bf16[k] if k == "w_kv" else shard_bf16[k][self.index(k, l)]
                    for k in (self.names(l) if names is None else names)]

        def install(self, l, gathered, names=None):
            out = self.full.setdefault(l, {})
            for k, g in zip(self.names(l) if names is None else names, gathered):
                out[k] = _unstack(g, blk_axis[k])
                if k == "a_w_in":
                    out[k] = jnp.pad(out[k], ((0, 0), (0, (-out[k].shape[1]) % LANES)))

        def big(self, l, la_):
            return self.full[l]

        def first_names(self):
            return ["a_w_in"] if la > 0 else self.names(0)

        def prep_rider(self, l):
            rest = [k for k in self.names(0) if k not in self.first_names()]
            return ("all_gather", self.shards(0, rest)) if l == 0 and rest else None

        def prep_got(self, l, got):
            if got is not None:
                self.install(0, got, [k for k in self.names(0) if k not in self.first_names()])

        def fwd_rider(self, l):
            return ("all_gather", self.shards(l + 1)) if l + 1 < depth else None

        def fwd_got(self, l, got):
            if got is not None:
                self.install(l + 1, got)

        def blocks(self, grads, keys):
            return [_to_blocks(grads[k], blk_axis[k]) for k in keys]

        def grads_ready(self, l, grads):
            keys = [k for k in self.names(l) if (k, l) not in self.early_keys]
            self.pending = (l, keys, self.blocks(grads, keys))

        early = early_parts = None
        early_keys = ()

        def ffn_grads_ready(self, l, grads):
            if l == 0 and la > 0:
                keys = ["f_w_up", "f_w_down"]
                self.early = (keys, self.blocks(grads, keys))
                self.early_keys = tuple((k, 0) for k in keys)

        def bwd_rider_a(self, l):
            return None if self.pending is None else ("rs_sibling", self.pending[2])

        def add_pairs(self, g8, from_sibling):
            return [pair_add(g, r, c_idx, "grads_pair_add") for g, r in zip(g8, from_sibling)]

        def bwd_got_a(self, l, got):
            if got is not None:
                self.parts = self.add_pairs(self.pending[2], got)

        def bwd_rider_b(self, l):
            riders = [] if self.parts is None else [("rs_chips", self.parts)]
            if self.early is not None:
                riders.append(("rs_sibling", self.early[1]))
            return riders

        def keep(self, stacks):
            l, keys, _ = self.pending
            for k, s in zip(keys, stacks):
                self.stacks[(k, l)] = s
            self.pending = self.parts = None

        def bwd_got_b(self, l, got):
            got = list(got or [])
            if self.parts is not None:
                self.keep(got.pop(0))
            if self.early is not None and got:
                self.early_parts = self.add_pairs(self.early[1], got.pop(0))

        def bwd_rider_c(self, l):
            return None if self.early_parts is None else ("rs_chips", self.early_parts)

        def bwd_got_c(self, l, got):
            for k, s in zip(self.early[0], got):
                self.stacks[(k, 0)] = s
            self.early = self.early_parts = None

        def finish(self):
            self.parts = self.add_pairs(self.pending[2], exchange("rs_sibling", self.pending[2], "grads_to_sibling"))
            self.keep(exchange("rs_chips", self.parts, "grads_to_chips"))

    comm = Sharded()

    small_shapes = [w[k].shape for k in SMALL_SHARDED]
    gathered = exchange("all_gather", comm.shards(0, comm.first_names()) + [_pack([w[k] for k in SMALL_SHARDED])],
                        "weights_all_gather")
    comm.install(0, gathered[:-1], comm.first_names())
    full = dict(w)
    small = [_unpack(gathered[-1][i], small_shapes) for i in range(N_DEV)]
    for idx, k in enumerate(SMALL_SHARDED):
        full[k] = jnp.concatenate([small[i][idx] for i in range(N_DEV)], axis=SHARD_AXIS[k])

    loss_part, grad_x, G, _ = local_step(x[0], loss_target[0], full, comm)
    comm.finish()
    stacks = []
    for k in BIG:
        layers = sorted(l for (k_, l) in comm.stacks if k_ == k)
        stacks.append(jnp.concatenate([comm.stacks[(k, l)] for l in layers], axis=1))

    small_names = [k for k in WEIGHTS if k not in BIG]
    reduced = _unpack(all_reduce_small(_pack([G[k] for k in small_names] + [loss_part.reshape(1)]), "small_all_reduce"),
                      [G[k].shape for k in small_names] + [(1,)])
    loss = reduced[-1][0]
    small_g = dict(zip(small_names, reduced[:-1]))
    for k in SMALL_SHARDED:
        sz = w[k].shape[SHARD_AXIS[k]]
        small_g[k] = lax.dynamic_slice_in_dim(small_g[k], me * sz, sz, axis=SHARD_AXIS[k])

    res = {}
    for k, st in zip(BIG, stacks):
        tshape = _t_view(k, w[k]).shape
        wt, mt, vt = (_as2d(_t_view(k, a)) for a in (w[k], mom[k], var[k]))
        outs = adamw(wt, st, mt, vt, "adamw_" + k)
        res[k] = [_t_view(k, o.reshape(tshape)) for o in outs]
    for k in small_names:
        outs = adamw(_as2d(w[k]), _as2d(small_g[k])[None], _as2d(mom[k]), _as2d(var[k]), "adamw_" + k)
        res[k] = [o.reshape(w[k].shape) for o in outs]

    return (loss, grad_x[None], *[res[k][0] for k in WEIGHTS], *[res[k][1] for k in WEIGHTS],
            *[res[k][2] for k in WEIGHTS], *[res[k][3] for k in WEIGHTS])
```

```python
import functools
import math

import jax
import jax.numpy as jnp
from jax import lax
from jax.experimental import pallas as pl
from jax.experimental.pallas import tpu as pltpu

F32 = jnp.float32
BF16 = jnp.bfloat16
HI = lax.Precision.HIGHEST
MESH = pl.DeviceIdType.MESH

EPS = 1e-6
NEG_INF = -1e30
CHUNK = 64
LEFT_CHUNKS = 8
BAND = (LEFT_CHUNKS + 1) * CHUNK
BAND_PAD = 640
A_CONV = 4
F_CONV = 3
A_HEAD = 128
B_HEAD = 64
LANES = 128
HALO = 8
N_DEV = 8

ADAM_LR = 0.001
ADAM_B1 = 0.9
ADAM_B2 = 0.999
ADAM_EPS = 1e-08
ADAM_WD = 0.01
ADAM_STEP = 10

VMEM_LIMIT_V7X = 56 * 1024 * 1024
GDN_BWD_HEADS = 8
COL_CHUNK = 256
FFN_TILE = 256


def _cp(sem=None, vmem=VMEM_LIMIT_V7X):
    kw = dict(vmem_limit_bytes=vmem)
    if sem is not None:
        kw["dimension_semantics"] = sem
    return pltpu.CompilerParams(**kw)


def _pick(n, target, q=LANES):
    best = None
    for t in range(q, min(n, target) + 1, q):
        if n % t == 0:
            best = t
    return best if best is not None else n


def _sig(x):
    return 1.0 / (1.0 + jnp.exp(-x))


def _softplus(x):
    return jnp.maximum(x, 0.0) + jnp.log(1.0 + jnp.exp(-jnp.abs(x)))


def _rms(x, g):
    return x * lax.rsqrt(jnp.mean(x * x, axis=-1, keepdims=True) + EPS) * g


def _rms_bwd(x, g, dxn):
    r = lax.rsqrt(jnp.mean(x * x, axis=-1, keepdims=True) + EPS)
    gd = dxn * g
    dx = r * gd - x * (r * r * r) * jnp.mean(x * gd, axis=-1, keepdims=True)
    dg = jnp.sum(dxn * x * r, axis=0, keepdims=True)
    return dx, dg


def _dot(a, b):
    return jnp.dot(a, b, preferred_element_type=F32)


def _dot_nt(a, b):
    return lax.dot_general(a, b, (((1,), (1,)), ((), ())), preferred_element_type=F32)


def _dot_tn(a, b):
    return lax.dot_general(a, b, (((0,), (0,)), ((), ())), preferred_element_type=F32)


def _hdot(a, b):
    return jnp.dot(a, b, precision=HI, preferred_element_type=F32)


def _hdot_nt(a, b):
    return lax.dot_general(a, b, (((1,), (1,)), ((), ())), precision=HI, preferred_element_type=F32)


def _hdot_tn(a, b):
    return lax.dot_general(a, b, (((0,), (0,)), ((), ())), precision=HI, preferred_element_type=F32)


def _resident(shape, index_map):
    return pl.BlockSpec(shape, index_map, pipeline_mode=pl.Buffered(1))


def norm_matmul(h, gamma, w, name, out_dtype=F32, w_t=False):
    n, d = h.shape
    nc = w.shape[0] if w_t else w.shape[1]
    tm = _pick(n, 2048 if out_dtype == BF16 else 1024, 8)
    tn = _pick(nc, 1536)

    def body(h_ref, g_ref, w_ref, o_ref):
        xn = _rms(h_ref[...], g_ref[...]).astype(BF16)
        o_ref[...] = (_dot_nt(xn, w_ref[...]) if w_t else _dot(xn, w_ref[...])).astype(out_dtype)

    return pl.pallas_call(
        body, grid=(nc // tn, n // tm),
        in_specs=[pl.BlockSpec((tm, d), lambda j, i: (i, 0)),
                  pl.BlockSpec((1, d), lambda j, i: (0, 0)),
                  pl.BlockSpec((tn, d), lambda j, i: (j, 0)) if w_t else pl.BlockSpec((d, tn), lambda j, i: (0, j))],
        out_specs=pl.BlockSpec((tm, tn), lambda j, i: (i, j)),
        out_shape=jax.ShapeDtypeStruct((n, nc), out_dtype), name=name,
        compiler_params=_cp(("parallel", "parallel")))(h, gamma, w)


def norm_matmul_tn(h, gamma, dy, name, transposed=False):
    n, d = h.shape
    nc = dy.shape[1]
    tm = _pick(n, 2048 if dy.dtype == BF16 else 1024, 8)
    tn = _pick(nc, 1536)

    def body(h_ref, g_ref, dy_ref, o_ref):
        @pl.when(pl.program_id(1) == 0)
        def _():
            o_ref[...] = jnp.zeros_like(o_ref)
        xn = _rms(h_ref[...], g_ref[...]).astype(BF16)
        dyb = dy_ref[...].astype(BF16)
        o_ref[...] += _dot_tn(dyb, xn) if transposed else _dot_tn(xn, dyb)

    return pl.pallas_call(
        body, grid=(nc // tn, n // tm),
        in_specs=[pl.BlockSpec((tm, d), lambda j, i: (i, 0)),
                  pl.BlockSpec((1, d), lambda j, i: (0, 0)),
                  pl.BlockSpec((tm, tn), lambda j, i: (i, j))],
        out_specs=pl.BlockSpec((tn, d), lambda j, i: (j, 0)) if transposed else pl.BlockSpec((d, tn), lambda j, i: (0, j)),
        out_shape=jax.ShapeDtypeStruct((nc, d) if transposed else (d, nc), F32), name=name,
        compiler_params=_cp(("parallel", "arbitrary")))(h, gamma, dy)


def matmul_tn(a, dy, name):
    n, ka = a.shape
    nc = dy.shape[1]
    tm = _pick(n, 2048 if a.dtype == BF16 else 1024, 8)
    tk = _pick(ka, 1536)
    tn = _pick(nc, 1024)

    def body(a_ref, dy_ref, o_ref):
        @pl.when(pl.program_id(2) == 0)
        def _():
            o_ref[...] = jnp.zeros_like(o_ref)
        o_ref[...] += _dot_tn(a_ref[...].astype(BF16), dy_ref[...].astype(BF16))

    return pl.pallas_call(
        body, grid=(ka // tk, nc // tn, n // tm),
        in_specs=[pl.BlockSpec((tm, tk), lambda k, j, i: (i, k)),
                  pl.BlockSpec((tm, tn), lambda k, j, i: (i, j))],
        out_specs=pl.BlockSpec((tk, tn), lambda k, j, i: (k, j)),
        out_shape=jax.ShapeDtypeStruct((ka, nc), F32), name=name,
        compiler_params=_cp(("parallel", "parallel", "arbitrary")))(a, dy)


def matmul_res(a, w, h, name):
    n, k = a.shape
    d = w.shape[1]
    tm = _pick(n, 512, 8)

    def body(a_ref, w_ref, h_ref, o_ref):
        o_ref[...] = h_ref[...] + _dot(a_ref[...].astype(BF16), w_ref[...])

    return pl.pallas_call(
        body, grid=(n // tm,),
        in_specs=[pl.BlockSpec((tm, k), lambda i: (i, 0)),
                  _resident((k, d), lambda i: (0, 0)),
                  pl.BlockSpec((tm, d), lambda i: (i, 0))],
        out_specs=pl.BlockSpec((tm, d), lambda i: (i, 0)),
        out_shape=jax.ShapeDtypeStruct((n, d), F32), name=name,
        compiler_params=_cp(("parallel",)))(a, w, h)


def matmul_nt(dy, w, name):
    n, k = dy.shape
    d = w.shape[0]
    tm = _pick(n, 512, 8)

    def body(dy_ref, w_ref, o_ref):
        o_ref[...] = _dot_nt(dy_ref[...].astype(BF16), w_ref[...])

    return pl.pallas_call(
        body, grid=(n // tm,),
        in_specs=[pl.BlockSpec((tm, k), lambda i: (i, 0)),
                  _resident((d, k), lambda i: (0, 0))],
        out_specs=pl.BlockSpec((tm, d), lambda i: (i, 0)),
        out_shape=jax.ShapeDtypeStruct((n, d), F32), name=name,
        compiler_params=_cp(("parallel",)))(dy, w)


def dx_norm_bwd(dout, h, gamma, pieces, name, rider=None, w_t=False):
    n, d = h.shape
    tm = _pick(n, 256, 8)
    np_ = len(pieces)
    mm = _dot if w_t else _dot_nt

    def body(*refs):
        dout_ref, h_ref, g_ref = refs[:3]
        dys = refs[3:3 + np_]
        ws = refs[3 + np_:3 + 2 * np_]
        dh_ref, dg_ref = refs[3 + 2 * np_:]
        dxn = mm(dys[0][...].astype(BF16), ws[0][...])
        for p in range(1, np_):
            dxn = dxn + mm(dys[p][...].astype(BF16), ws[p][...])
        dx, dg = _rms_bwd(h_ref[...], g_ref[...], dxn)
        dh_ref[...] = dout_ref[...] + dx

        @pl.when(pl.program_id(0) == 0)
        def _():
            dg_ref[...] = jnp.zeros_like(dg_ref)
        dg_ref[...] += dg

    in_specs = [pl.BlockSpec((tm, d), lambda i: (i, 0)),
                pl.BlockSpec((tm, d), lambda i: (i, 0)),
                pl.BlockSpec((1, d), lambda i: (0, 0))]
    in_specs += [pl.BlockSpec((tm, dy.shape[1]), lambda i: (i, 0)) for dy, _ in pieces]
    in_specs += [_resident(w.shape, lambda i: (0, 0)) for _, w in pieces]
    (dh, dg), got = host_call(
        body, grid=(n // tm,), in_specs=in_specs,
        out_specs=[pl.BlockSpec((tm, d), lambda i: (i, 0)), pl.BlockSpec((1, d), lambda i: (0, 0))],
        out_shape=[jax.ShapeDtypeStruct((n, d), F32), jax.ShapeDtypeStruct((1, d), F32)], name=name,
        scratch_shapes=[], sem=("arbitrary",), rider=rider,
        args=(dout, h, gamma, *[p[0] for p in pieces], *[p[1] for p in pieces]))
    return (dh, dg) if rider is None else (dh, dg, got)


def loss_head(h, gamma, target, name="loss_head"):
    n, d = h.shape
    tm = _pick(n, 512, 8)

    def body(h_ref, g_ref, t_ref, loss_ref, dh_ref, dg_ref):
        @pl.when(pl.program_id(0) == 0)
        def _():
            loss_ref[...] = jnp.zeros_like(loss_ref)
            dg_ref[...] = jnp.zeros_like(dg_ref)
        x = h_ref[...]
        g = g_ref[...]
        e = _rms(x, g) - t_ref[...]
        part = jnp.sum(jnp.sum(e * e, axis=-1, keepdims=True), axis=0, keepdims=True) * (0.5 / d)
        loss_ref[...] += jnp.broadcast_to(part, loss_ref.shape)
        dx, dg = _rms_bwd(x, g, e * (1.0 / d))
        dh_ref[...] = dx
        dg_ref[...] += dg

    return pl.pallas_call(
        body, grid=(n // tm,),
        in_specs=[pl.BlockSpec((tm, d), lambda i: (i, 0)), pl.BlockSpec((1, d), lambda i: (0, 0)),
                  pl.BlockSpec((tm, d), lambda i: (i, 0))],
        out_specs=[pl.BlockSpec((8, LANES), lambda i: (0, 0)), pl.BlockSpec((tm, d), lambda i: (i, 0)),
                   pl.BlockSpec((1, d), lambda i: (0, 0))],
        out_shape=[jax.ShapeDtypeStruct((8, LANES), F32), jax.ShapeDtypeStruct((n, d), F32),
                   jax.ShapeDtypeStruct((1, d), F32)], name=name,
        compiler_params=_cp(("arbitrary",)))(h, gamma, target)


def _halo_rows(dtype):
    return HALO * (4 // jnp.dtype(dtype).itemsize)


def _prev_halo_map(t, hb=HALO):
    return lambda i: (jnp.maximum(i * (t // hb) - 1, 0), 0)


def _next_halo_map(t, n, hb=HALO):
    return lambda i: (jnp.minimum((i + 1) * (t // hb), n // hb - 1), 0)


def _fill_prev(xs, main_ref, halo_ref, i, cols=slice(None)):
    hb = halo_ref.shape[0]
    xs[0:HALO, :] = jnp.where(i > 0, halo_ref[hb - HALO:hb, cols].astype(F32), 0.0)
    xs[HALO:, :] = main_ref[:, cols].astype(F32)


def _causal_conv(xs, w_ref, width, t, cols=slice(None)):
    x = xs[...]
    acc = w_ref[width - 1:width, cols] * x[HALO:, :]
    for k in range(width - 1):
        acc = acc + w_ref[k:k + 1, cols] * pltpu.roll(x, width - 1 - k, axis=0)[HALO:, :]
    return acc


def _col_chunks(width, target=COL_CHUNK):
    tc = _pick(width, target)
    return [slice(j * tc, (j + 1) * tc) for j in range(width // tc)]


def ffn_act_down(up, conv_w, conv_b, w_down, h, name):
    n, c2 = up.shape
    ff = c2 // 2
    d = h.shape[1]
    t = _pick(n, 2 * FFN_TILE, 8)
    hb = _halo_rows(up.dtype)
    chunks = _col_chunks(ff)
    tc = chunks[0].stop

    def body(up_ref, halo_ref, cw_ref, cb_ref, wd_ref, h_ref, o_ref, act_ref, hc_ref, xg, xv):
        i = pl.program_id(0)
        acc = h_ref[...]
        for cs in chunks:
            vs = slice(ff + cs.start, ff + cs.stop)
            _fill_prev(xg, up_ref, halo_ref, i, cs)
            _fill_prev(xv, up_ref, halo_ref, i, vs)
            gate = _causal_conv(xg, cw_ref, F_CONV, t, cs) + cb_ref[:, cs]
            val = _causal_conv(xv, cw_ref, F_CONV, t, vs) + cb_ref[:, vs]
            hc_ref[:, cs] = gate.astype(BF16)
            hc_ref[:, vs] = val.astype(BF16)
            act = (gate * _sig(gate) * val).astype(BF16)
            act_ref[:, cs] = act
            acc = acc + _dot(act, wd_ref[cs, :])
        o_ref[...] = acc

    return pl.pallas_call(
        body, grid=(n // t,),
        in_specs=[pl.BlockSpec((t, c2), lambda i: (i, 0)),
                  pl.BlockSpec((hb, c2), _prev_halo_map(t, hb)),
                  pl.BlockSpec((F_CONV, c2), lambda i: (0, 0)),
                  pl.BlockSpec((1, c2), lambda i: (0, 0)),
                  _resident((ff, d), lambda i: (0, 0)),
                  pl.BlockSpec((t, d), lambda i: (i, 0))],
        out_specs=[pl.BlockSpec((t, d), lambda i: (i, 0)), pl.BlockSpec((t, ff), lambda i: (i, 0)),
                   pl.BlockSpec((t, c2), lambda i: (i, 0))],
        out_shape=[jax.ShapeDtypeStruct((n, d), F32), jax.ShapeDtypeStruct((n, ff), BF16),
                   jax.ShapeDtypeStruct((n, c2), BF16)],
        scratch_shapes=[pltpu.VMEM((t + HALO, tc), F32), pltpu.VMEM((t + HALO, tc), F32)], name=name,
        compiler_params=_cp(("parallel",)))(up, up, conv_w, conv_b, w_down, h)


def ffn_bwd_act(dout, hc, w_down, name):
    n, c2 = hc.shape
    ff = c2 // 2
    d = dout.shape[1]
    t = _pick(n, 2 * FFN_TILE, 8)
    chunks = _col_chunks(ff)

    def body(dout_ref, hc_ref, wd_ref, dhc_ref, dcb_ref):
        i = pl.program_id(0)

        @pl.when(i == 0)
        def _():
            dcb_ref[...] = jnp.zeros_like(dcb_ref)
        doutb = dout_ref[...].astype(BF16)
        for cs in chunks:
            vs = slice(ff + cs.start, ff + cs.stop)
            gate = hc_ref[:, cs].astype(F32)
            val = hc_ref[:, vs].astype(F32)
            sg = _sig(gate)
            da = _dot_nt(doutb, wd_ref[cs, :])
            dgate = da * val * (sg * (1.0 + gate * (1.0 - sg)))
            dval = da * gate * sg
            dhc_ref[:, cs] = dgate.astype(BF16)
            dhc_ref[:, vs] = dval.astype(BF16)
            dcb_ref[:, cs] += jnp.sum(dgate, axis=0, keepdims=True)
            dcb_ref[:, vs] += jnp.sum(dval, axis=0, keepdims=True)

    return pl.pallas_call(
        body, grid=(n // t,),
        in_specs=[pl.BlockSpec((t, d), lambda i: (i, 0)),
                  pl.BlockSpec((t, c2), lambda i: (i, 0)),
                  _resident((ff, d), lambda i: (0, 0))],
        out_specs=[pl.BlockSpec((t, c2), lambda i: (i, 0)), pl.BlockSpec((1, c2), lambda i: (0, 0))],
        out_shape=[jax.ShapeDtypeStruct((n, c2), BF16), jax.ShapeDtypeStruct((1, c2), F32)], name=name,
        compiler_params=_cp(("arbitrary",)))(dout, hc, w_down)


def conv_bwd_tail(dy_ref, dnext_ref, x_ref, cw_ref, dcw_ref, ds, width, t, i, last, cols=slice(None)):
    ds[0:t, :] = dy_ref[:, cols].astype(F32)
    ds[t:, :] = jnp.where(i < last, dnext_ref[0:HALO, cols].astype(F32), 0.0)
    x = x_ref[:, cols].astype(F32)
    dall = ds[...]
    dx = None
    for k in range(width):
        off = width - 1 - k
        shifted = dall[0:t, :] if off == 0 else pltpu.roll(dall, t + HALO - off, axis=0)[0:t, :]
        term = cw_ref[k:k + 1, cols] * shifted
        dx = term if dx is None else dx + term
        dcw_ref[k:k + 1, cols] += jnp.sum(shifted * x, axis=0, keepdims=True)
    return dx


def ffn_bwd_up(dhc, up, conv_w, w_up, h, dout, gamma, name, rider=None):
    n, c2 = up.shape
    d = h.shape[1]
    t = _pick(n, FFN_TILE, 8)
    last = n // t - 1
    chunks = _col_chunks(c2)
    tc = chunks[0].stop

    def body(dhc_ref, dnext_ref, up_ref, cw_ref, wu_ref, h_ref, dout_ref, g_ref,
             dh_ref, dup_ref, dcw_ref, dg_ref, ds):
        i = pl.program_id(0)

        @pl.when(i == 0)
        def _():
            dcw_ref[...] = jnp.zeros_like(dcw_ref)
            dg_ref[...] = jnp.zeros_like(dg_ref)
        dxn = jnp.zeros((t, d), F32)
        for cs in chunks:
            dup = conv_bwd_tail(dhc_ref, dnext_ref, up_ref, cw_ref, dcw_ref, ds, F_CONV, t, i, last, cs)
            dupb = dup.astype(BF16)
            dup_ref[:, cs] = dupb
            dxn = dxn + _dot(dupb, wu_ref[cs, :])
        dx, dg = _rms_bwd(h_ref[...], g_ref[...], dxn)
        dh_ref[...] = dout_ref[...] + dx
        dg_ref[...] += dg

    return host_call(
        body, grid=(n // t,), rider=rider, sem=("arbitrary",), args=(dhc, dhc, up, conv_w, w_up, h, dout, gamma),
        in_specs=[pl.BlockSpec((t, c2), lambda i: (i, 0)),
                  pl.BlockSpec((_halo_rows(dhc.dtype), c2), _next_halo_map(t, n, _halo_rows(dhc.dtype))),
                  pl.BlockSpec((t, c2), lambda i: (i, 0)),
                  pl.BlockSpec((F_CONV, c2), lambda i: (0, 0)),
                  _resident((c2, d), lambda i: (0, 0)),
                  pl.BlockSpec((t, d), lambda i: (i, 0)),
                  pl.BlockSpec((t, d), lambda i: (i, 0)),
                  pl.BlockSpec((1, d), lambda i: (0, 0))],
        out_specs=[pl.BlockSpec((t, d), lambda i: (i, 0)), pl.BlockSpec((t, c2), lambda i: (i, 0)),
                   pl.BlockSpec((F_CONV, c2), lambda i: (0, 0)), pl.BlockSpec((1, d), lambda i: (0, 0))],
        out_shape=[jax.ShapeDtypeStruct((n, d), F32), jax.ShapeDtypeStruct((n, c2), BF16),
                   jax.ShapeDtypeStruct((F_CONV, c2), F32), jax.ShapeDtypeStruct((1, d), F32)],
        scratch_shapes=[pltpu.VMEM((t + HALO, tc), F32)], name=name)


def _gdn_head(uq, uk, uv, pba, alog, dtb, head, n_heads):
    lane = lax.broadcasted_iota(jnp.int32, pba.shape, 1)
    sq = uq * _sig(uq)
    q = sq * lax.rsqrt(jnp.sum(sq * sq, axis=-1, keepdims=True) + EPS) * (A_HEAD ** -0.5)
    sk = uk * _sig(uk)
    k = sk * lax.rsqrt(jnp.sum(sk * sk, axis=-1, keepdims=True) + EPS)
    v = uv * _sig(uv)
    beta = jnp.sum(jnp.where(lane == head, _sig(pba), 0.0), axis=-1, keepdims=True)
    g_all = -jnp.exp(alog) * _softplus(pba + dtb)
    g = jnp.sum(jnp.where(lane == n_heads + head, g_all, 0.0), axis=-1, keepdims=True)
    return q, k, v, jnp.broadcast_to(beta, uq.shape), jnp.broadcast_to(g, uq.shape)


def gdn_prep(proj, conv_w, alog, dtb, n_heads, name, rider=None):
    n = proj.shape[0]
    qk = n_heads * A_HEAD
    cw = 3 * qk
    ba_blk = (cw + qk) // LANES
    t = _pick(n, 256, 8)

    def body(x_ref, halo_ref, pba_ref, cw_ref, al_ref, dt_ref, q_ref, k_ref, v_ref, b_ref, g_ref, u_ref, xs):
        i = pl.program_id(0)
        xs[0:HALO, :] = jnp.where(i > 0, halo_ref[...], 0.0)
        xs[HALO:, :] = x_ref[...]
        u = _causal_conv(xs, cw_ref, A_CONV, t)
        u_ref[...] = u.astype(BF16)
        pba = pba_ref[...]
        for hd in range(n_heads):
            s0 = slice(hd * A_HEAD, (hd + 1) * A_HEAD)
            s1 = slice(qk + hd * A_HEAD, qk + (hd + 1) * A_HEAD)
            s2 = slice(2 * qk + hd * A_HEAD, 2 * qk + (hd + 1) * A_HEAD)
            q, k, v, bb, gb = _gdn_head(u[:, s0], u[:, s1], u[:, s2], pba, al_ref[...], dt_ref[...], hd, n_heads)
            q_ref[:, s0] = q
            k_ref[:, s0] = k
            v_ref[:, s0] = v
            b_ref[:, s0] = bb
            g_ref[:, s0] = gb

    out = jax.ShapeDtypeStruct((n, qk), F32)
    return host_call(
        body, grid=(n // t,),
        in_specs=[pl.BlockSpec((t, cw), lambda i: (i, 0)),
                  pl.BlockSpec((HALO, cw), _prev_halo_map(t)),
                  pl.BlockSpec((t, LANES), lambda i: (i, ba_blk)),
                  pl.BlockSpec((A_CONV, cw), lambda i: (0, 0)),
                  pl.BlockSpec((1, LANES), lambda i: (0, 0)),
                  pl.BlockSpec((1, LANES), lambda i: (0, 0))],
        out_specs=[pl.BlockSpec((t, qk), lambda i: (i, 0))] * 5 + [pl.BlockSpec((t, cw), lambda i: (i, 0))],
        out_shape=[out] * 5 + [jax.ShapeDtypeStruct((n, cw), BF16)],
        scratch_shapes=[pltpu.VMEM((t + HALO, cw), F32)], name=name,
        sem=("parallel",), args=(proj, proj, proj, conv_w, alog, dtb), rider=rider)


def gdn_prep_bwd(proj, u, alog, dtb, dq, dk, dv, dbb, dgb, n_heads, name):
    n = proj.shape[0]
    qk = n_heads * A_HEAD
    cw = 3 * qk
    ba_blk = (cw + qk) // LANES
    t = _pick(n, 256, 8)

    def body(u_ref, pba_ref, al_ref, dt_ref, dq_ref, dk_ref, dv_ref, dbb_ref, dgb_ref,
             du_ref, dba_ref, dal_ref, ddt_ref):
        i = pl.program_id(0)
        u = u_ref[...].astype(F32)
        pba = pba_ref[...]
        lane0 = lax.broadcasted_iota(jnp.int32, (t, A_HEAD), 1) == 0
        dba = jnp.zeros((t, LANES), F32)
        dal = jnp.zeros((1, LANES), F32)
        ddt = jnp.zeros((1, LANES), F32)
        for hd in range(n_heads):
            s0 = slice(hd * A_HEAD, (hd + 1) * A_HEAD)
            s1 = slice(qk + hd * A_HEAD, qk + (hd + 1) * A_HEAD)
            s2 = slice(2 * qk + hd * A_HEAD, 2 * qk + (hd + 1) * A_HEAD)
            fn = functools.partial(_gdn_head, head=hd, n_heads=n_heads)
            _, vjp = jax.vjp(fn, u[:, s0], u[:, s1], u[:, s2], pba, al_ref[...], dt_ref[...])
            cts = (dq_ref[:, s0], dk_ref[:, s0], dv_ref[:, s0],
                   jnp.where(lane0, dbb_ref[:, s0], 0.0), jnp.where(lane0, dgb_ref[:, s0], 0.0))
            duq, duk, duv, dpba, da, dd = vjp(cts)
            du_ref[:, s0] = duq
            du_ref[:, s1] = duk
            du_ref[:, s2] = duv
            dba = dba + dpba
            dal = dal + da
            ddt = ddt + dd
        dba_ref[...] = dba

        @pl.when(i == 0)
        def _():
            dal_ref[...] = jnp.zeros_like(dal_ref)
            ddt_ref[...] = jnp.zeros_like(ddt_ref)
        dal_ref[...] += dal
        ddt_ref[...] += ddt

    tok = pl.BlockSpec((t, qk), lambda i: (i, 0))
    row = pl.BlockSpec((1, LANES), lambda i: (0, 0))
    return pl.pallas_call(
        body, grid=(n // t,),
        in_specs=[pl.BlockSpec((t, cw), lambda i: (i, 0)),
                  pl.BlockSpec((t, LANES), lambda i: (i, ba_blk)), row, row,
                  tok, tok, tok, tok, tok],
        out_specs=[pl.BlockSpec((t, cw), lambda i: (i, 0)), pl.BlockSpec((t, LANES), lambda i: (i, 0)), row, row],
        out_shape=[jax.ShapeDtypeStruct((n, cw), F32), jax.ShapeDtypeStruct((n, LANES), F32),
                   jax.ShapeDtypeStruct((1, LANES), F32), jax.ShapeDtypeStruct((1, LANES), F32)],
        name=name, compiler_params=_cp(("arbitrary",)))(u, proj, alog, dtb, dq, dk, dv, dbb, dgb)


def conv_bwd(du, x, conv_w, width, name, rider=None):
    n, cw = du.shape
    t = _pick(n, 256, 8)
    last = n // t - 1

    chunks = _col_chunks(cw)
    tc = chunks[0].stop

    def body(du_ref, dnext_ref, x_ref, cw_ref, dx_ref, dcw_ref, ds):
        i = pl.program_id(0)

        @pl.when(i == 0)
        def _():
            dcw_ref[...] = jnp.zeros_like(dcw_ref)
        for cs in chunks:
            dx_ref[:, cs] = conv_bwd_tail(du_ref, dnext_ref, x_ref, cw_ref, dcw_ref, ds, width, t, i, last, cs)

    return host_call(
        body, grid=(n // t,),
        in_specs=[pl.BlockSpec((t, cw), lambda i: (i, 0)),
                  pl.BlockSpec((HALO, cw), _next_halo_map(t, n)),
                  pl.BlockSpec((t, cw), lambda i: (i, 0)),
                  pl.BlockSpec((width, cw), lambda i: (0, 0))],
        out_specs=[pl.BlockSpec((t, cw), lambda i: (i, 0)), pl.BlockSpec((width, cw), lambda i: (0, 0))],
        out_shape=[jax.ShapeDtypeStruct((n, cw), F32), jax.ShapeDtypeStruct((width, cw), F32)],
        scratch_shapes=[pltpu.VMEM((t + HALO, tc), F32)], name=name,
        sem=("arbitrary",), args=(du, du, x, conv_w), rider=rider)


def _b(x):
    return x.astype(BF16)


def _mm_nn(a, b):
    return _dot(_b(a), _b(b))


def _mm_nt(a, b):
    return _dot_nt(_b(a), _b(b))


def _mm_tn(a, b):
    return _dot_tn(_b(a), _b(b))


@jax.custom_vjp
def _mmg_nn(a, b):
    return _mm_nn(a, b)


_mmg_nn.defvjp(lambda a, b: (_mm_nn(a, b), (a, b)),
               lambda res, dc: (_mm_nt(dc, res[1]), _mm_tn(res[0], dc)))


@jax.custom_vjp
def _mmg_nt(a, b):
    return _mm_nt(a, b)


_mmg_nt.defvjp(lambda a, b: (_mm_nt(a, b), (a, b)),
               lambda res, dc: (_mm_nn(dc, res[1]), _mm_tn(dc, res[0])))


@jax.custom_vjp
def _mmg_tn(a, b):
    return _mm_tn(a, b)


_mmg_tn.defvjp(lambda a, b: (_mm_tn(a, b), (a, b)),
               lambda res, dc: (_mm_nt(res[1], dc), _mm_nn(res[0], dc)))


def _bf16_parts(x, n):
    parts = []
    for _ in range(n):
        p = x.astype(BF16)
        parts.append(p)
        x = x - p.astype(F32)
    return parts


def _dot_f32ish(a, b):
    (ah, al), (bh, bl) = _bf16_parts(a, 2), _bf16_parts(b, 2)
    return _dot(ah, bh) + _dot(ah, bl) + _dot(al, bh)


def _tri_dot(x, transpose):
    c = x.shape[0]
    low = lax.broadcasted_iota(jnp.int32, (c, c), 0) >= lax.broadcasted_iota(jnp.int32, (c, c), 1)
    tri = jnp.where(low, 1.0, 0.0).astype(BF16)
    mm = _dot_tn if transpose else _dot
    return functools.reduce(lambda a, b: a + b, [mm(tri, p) for p in _bf16_parts(x, 3)])


def _cumsum(x):
    return _tri_dot(x, False)


@jax.custom_vjp
def _cumsum_g(x):
    return _tri_dot(x, False)


_cumsum_g.defvjp(lambda x: (_tri_dot(x, False), None), lambda _, ct: (_tri_dot(ct, True),))


def _each(f, *lists):
    return [f(*a) for a in zip(*lists)]


def _unit_lower_inv(ms):
    c = ms[0].shape[0]
    eye = jnp.where(lax.broadcasted_iota(jnp.int32, (c, c), 0) == lax.broadcasted_iota(jnp.int32, (c, c), 1), 1.0, 0.0)
    xs = [eye - m for m in ms]
    pws = _each(_mm_nn, ms, ms)
    for it in range(5):
        xs = _each(lambda x, pw: x + _mm_nn(x, pw), xs, pws)
        if it < 4:
            pws = _each(_mm_nn, pws, pws)
    rs = _each(lambda m, x: eye - x - _dot_f32ish(m, x), ms, xs)
    return _each(lambda x, r: x + _mm_nn(x, r), xs, rs)


@jax.custom_vjp
def _saved_inv_g(ms, xs):
    return xs


_saved_inv_g.defvjp(lambda ms, xs: (xs, xs),
                    lambda xs, dxs: (_each(lambda t, x: -_mm_nt(t, x), _each(_mm_tn, xs, dxs), xs),
                                     [jnp.zeros_like(x) for x in xs]))


def _gdn_chunk(ops, state, q, k, v, bb, gb):
    nn, nt, tn, inv, cum = ops
    c = CHUNK
    ri = lax.broadcasted_iota(jnp.int32, (c, c), 0)
    ci = lax.broadcasted_iota(jnp.int32, (c, c), 1)
    causal = ri >= ci
    strict = ri > ci
    tri = jnp.where(causal, 1.0, 0.0)
    gc = [cum(g) for g in gb]
    decay = [jnp.where(causal, jnp.exp(jnp.where(causal, x[:, :c] - x.T[:c, :], 0.0)), 0.0) for x in gc]
    kb = _each(lambda a, b: a * b, k, bb)
    kk = _each(nt, kb, k)
    m = _each(lambda a, d: jnp.where(strict, a * d, 0.0), kk, decay)
    tinv = inv(m)
    egc = [jnp.exp(x) for x in gc]
    u = _each(nn, tinv, _each(lambda a, b: a * b, v, bb))
    w = _each(nn, tinv, _each(lambda a, b: a * b, kb, egc))
    attn = _each(lambda a, d: a * d, _each(nt, q, k), decay)
    glast = [jnp.sum(g, axis=0, keepdims=True) for g in gb]
    ws = _each(nn, w, state)
    v_new = _each(lambda a, b: a - b, u, ws)
    qs = _each(nn, _each(lambda a, b: a * b, q, egc), state)
    av = _each(nn, attn, v_new)
    o = _each(lambda a, b: a + b, qs, av)
    kv = _each(tn, _each(lambda a, gl, x: a * jnp.exp(gl - x), k, glast, gc), v_new)
    new_state = _each(lambda s, gl, a: s * jnp.exp(gl) + a, state, glast, kv)
    return o, new_state


def gdn_fwd(q, k, v, bb, gb, n_heads, name, rider=None):
    n, w = q.shape
    nc = n // CHUNK
    cb = min(8, nc)
    rows = cb * CHUNK

    def body(q_ref, k_ref, v_ref, b_ref, g_ref, o_ref, st_ref, ti_ref, s_scr):
        @pl.when(pl.program_id(0) == 0)
        def _():
            s_scr[...] = jnp.zeros_like(s_scr)

        def step(c, carry):
            sl = pl.ds(pl.multiple_of(c * CHUNK, CHUNK), CHUNK)
            lanes = [slice(hd * A_HEAD, (hd + 1) * A_HEAD) for hd in range(n_heads)]
            state = [s_scr[hd] for hd in range(n_heads)]
            inverses = []

            def inv(ms):
                inverses.extend(_unit_lower_inv(ms))
                return inverses

            o, new_state = _gdn_chunk((_mm_nn, _mm_nt, _mm_tn, inv, _cumsum), state,
                                      *[[r[sl, ls] for ls in lanes] for r in (q_ref, k_ref, v_ref, b_ref, g_ref)])
            for hd, ls in enumerate(lanes):
                st_ref[hd, pl.ds(c, 1)] = state[hd][None]
                ti_ref[hd, pl.ds(c, 1)] = inverses[hd].astype(BF16)[None]
                o_ref[sl, ls] = o[hd]
                s_scr[hd] = new_state[hd]
            return carry

        lax.fori_loop(0, cb, step, 0)

    tok = pl.BlockSpec((rows, w), lambda j: (j, 0))
    return host_call(
        body, grid=(nc // cb,),
        in_specs=[tok] * 5,
        out_specs=[tok, pl.BlockSpec((n_heads, cb, A_HEAD, A_HEAD), lambda j: (0, j, 0, 0)),
                   pl.BlockSpec((n_heads, cb, CHUNK, CHUNK), lambda j: (0, j, 0, 0))],
        out_shape=[jax.ShapeDtypeStruct(q.shape, F32), jax.ShapeDtypeStruct((n_heads, nc, A_HEAD, A_HEAD), F32),
                   jax.ShapeDtypeStruct((n_heads, nc, CHUNK, CHUNK), BF16)],
        scratch_shapes=[pltpu.VMEM((n_heads, A_HEAD, A_HEAD), F32)], name=name,
        sem=("arbitrary",), args=(q, k, v, bb, gb), rider=rider)


def gdn_bwd(q, k, v, bb, gb, states, tinv, do, n_heads, name, rider=None):
    n, w = q.shape
    nc = n // CHUNK
    cb = min(4, nc)
    rows = cb * CHUNK
    nblk = nc // cb

    def body(q_ref, k_ref, v_ref, b_ref, g_ref, st_ref, ti_ref, do_ref,
             dq_ref, dk_ref, dv_ref, db_ref, dg_ref, ds_scr):
        @pl.when(pl.program_id(0) == 0)
        def _():
            ds_scr[...] = jnp.zeros_like(ds_scr)

        def step(s, carry):
            c = cb - 1 - s
            sl = pl.ds(pl.multiple_of(c * CHUNK, CHUNK), CHUNK)
            for h0 in range(0, n_heads, GDN_BWD_HEADS):
                heads = list(range(h0, min(h0 + GDN_BWD_HEADS, n_heads)))
                lanes = [slice(hd * A_HEAD, (hd + 1) * A_HEAD) for hd in heads]
                state = [st_ref[hd, pl.ds(c, 1)][0] for hd in heads]
                saved = [ti_ref[hd, pl.ds(c, 1)][0].astype(F32) for hd in heads]
                chunk_fn = functools.partial(
                    _gdn_chunk, (_mmg_nn, _mmg_nt, _mmg_tn, lambda ms: _saved_inv_g(ms, saved), _cumsum_g))
                _, vjp = jax.vjp(chunk_fn, state, *[[r[sl, ls] for ls in lanes]
                                                    for r in (q_ref, k_ref, v_ref, b_ref, g_ref)])
                dstate, dq, dk, dv, dbb, dgb = vjp(([do_ref[sl, ls] for ls in lanes], [ds_scr[hd] for hd in heads]))
                for u, (hd, ls) in enumerate(zip(heads, lanes)):
                    ds_scr[hd] = dstate[u]
                    dq_ref[sl, ls] = dq[u]
                    dk_ref[sl, ls] = dk[u]
                    dv_ref[sl, ls] = dv[u]
                    db_ref[sl, ls] = jnp.broadcast_to(jnp.sum(dbb[u], axis=-1, keepdims=True), dbb[u].shape)
                    dg_ref[sl, ls] = jnp.broadcast_to(jnp.sum(dgb[u], axis=-1, keepdims=True), dgb[u].shape)
            return carry

        lax.fori_loop(0, cb, step, 0)

    tok = pl.BlockSpec((rows, w), lambda j: (nblk - 1 - j, 0))
    out = jax.ShapeDtypeStruct(q.shape, F32)
    return host_call(
        body, grid=(nblk,),
        in_specs=[tok] * 5 + [pl.BlockSpec((n_heads, cb, A_HEAD, A_HEAD), lambda j: (0, nblk - 1 - j, 0, 0)),
                              pl.BlockSpec((n_heads, cb, CHUNK, CHUNK), lambda j: (0, nblk - 1 - j, 0, 0)), tok],
        out_specs=[tok] * 5, out_shape=[out] * 5,
        scratch_shapes=[pltpu.VMEM((n_heads, A_HEAD, A_HEAD), F32)], name=name,
        sem=("arbitrary",), args=(q, k, v, bb, gb, states, tinv, do), rider=rider)


def _gdn_gate(oh, zh, w):
    r = lax.rsqrt(jnp.mean(oh * oh, axis=-1, keepdims=True) + EPS)
    return oh * r * w * (zh * _sig(zh))


def gdn_out(o, proj, out_norm, w_out, h, n_heads, name):
    n, vw = o.shape
    d = h.shape[1]
    z_blk = 3 * vw // vw
    t = _pick(n, 512, 8)

    def body(o_ref, z_ref, w_ref, wo_ref, h_ref, out_ref, y_ref):
        for hd in range(n_heads):
            s0 = slice(hd * A_HEAD, (hd + 1) * A_HEAD)
            y_ref[:, s0] = _gdn_gate(o_ref[:, s0], z_ref[:, s0], w_ref[...]).astype(BF16)
        out_ref[...] = h_ref[...] + _dot(y_ref[...], wo_ref[...])

    return pl.pallas_call(
        body, grid=(n // t,),
        in_specs=[pl.BlockSpec((t, vw), lambda i: (i, 0)),
                  pl.BlockSpec((t, vw), lambda i: (i, z_blk)),
                  pl.BlockSpec((1, A_HEAD), lambda i: (0, 0)),
                  _resident((vw, d), lambda i: (0, 0)),
                  pl.BlockSpec((t, d), lambda i: (i, 0))],
        out_specs=[pl.BlockSpec((t, d), lambda i: (i, 0)), pl.BlockSpec((t, vw), lambda i: (i, 0))],
        out_shape=[jax.ShapeDtypeStruct((n, d), F32), jax.ShapeDtypeStruct((n, vw), BF16)], name=name,
        compiler_params=_cp(("parallel",)))(o, proj, out_norm, w_out, h)


def gdn_out_bwd(dout, o, proj, out_norm, w_out, n_heads, name):
    n, vw = o.shape
    d = dout.shape[1]
    z_blk = 3
    t = _pick(n, 512, 8)

    def body(dout_ref, o_ref, z_ref, w_ref, wo_ref, do_ref, dz_ref, dw_ref):
        dy = _dot_nt(dout_ref[...].astype(BF16), wo_ref[...])
        dw = jnp.zeros((1, A_HEAD), F32)
        for hd in range(n_heads):
            s0 = slice(hd * A_HEAD, (hd + 1) * A_HEAD)
            _, vjp = jax.vjp(_gdn_gate, o_ref[:, s0], z_ref[:, s0], w_ref[...])
            doh, dzh, dwh = vjp(dy[:, s0])
            do_ref[:, s0] = doh
            dz_ref[:, s0] = dzh
            dw = dw + dwh

        @pl.when(pl.program_id(0) == 0)
        def _():
            dw_ref[...] = jnp.zeros_like(dw_ref)
        dw_ref[...] += dw

    tok = pl.BlockSpec((t, vw), lambda i: (i, 0))
    return pl.pallas_call(
        body, grid=(n // t,),
        in_specs=[pl.BlockSpec((t, d), lambda i: (i, 0)), tok,
                  pl.BlockSpec((t, vw), lambda i: (i, z_blk)),
                  pl.BlockSpec((1, A_HEAD), lambda i: (0, 0)),
                  _resident((vw, d), lambda i: (0, 0))],
        out_specs=[tok, tok, pl.BlockSpec((1, A_HEAD), lambda i: (0, 0))],
        out_shape=[jax.ShapeDtypeStruct((n, vw), F32), jax.ShapeDtypeStruct((n, vw), F32),
                   jax.ShapeDtypeStruct((1, A_HEAD), F32)], name=name,
        compiler_params=_cp(("arbitrary",)))(dout, o, proj, out_norm, w_out)


BIAS_LINE = 768
BIAS_TOP = BAND + CHUNK - 2


def _bias_line_onehot(clip, tbl_pad):
    r = lax.broadcasted_iota(jnp.int32, (tbl_pad, BIAS_LINE), 0)
    v = lax.broadcasted_iota(jnp.int32, (tbl_pad, BIAS_LINE), 1)
    idx = jnp.clip(BIAS_TOP - v - (CHUNK - 1), -clip, clip) + clip
    return jnp.where((r == idx) & (v <= BIAS_TOP), 1.0, 0.0)


def bias_expand(tbl, clip, name):
    nh, tp = tbl.shape

    def body(t_ref, o_ref):
        line = _hdot(t_ref[...], _bias_line_onehot(clip, tp))
        keep = lax.broadcasted_iota(jnp.int32, (nh, BAND_PAD), 1) < BAND
        for i in range(CHUNK):
            s = CHUNK - 1 - i
            rolled = line if s == 0 else pltpu.roll(line, BIAS_LINE - s, axis=1)
            o_ref[i] = jnp.where(keep, rolled[:, :BAND_PAD], NEG_INF)

    return pl.pallas_call(
        body, in_specs=[pl.BlockSpec(memory_space=pltpu.VMEM)], out_specs=pl.BlockSpec(memory_space=pltpu.VMEM),
        out_shape=jax.ShapeDtypeStruct((CHUNK, nh, BAND_PAD), F32), name=name, compiler_params=_cp())(tbl)


def bias_expand_bwd(dbias, clip, tp, name):
    _, nh, _ = dbias.shape

    def body(d_ref, o_ref):
        keep = lax.broadcasted_iota(jnp.int32, (nh, BAND_PAD), 1) < BAND
        pad = jnp.zeros((nh, BIAS_LINE - BAND_PAD), F32)
        acc = jnp.zeros((nh, BIAS_LINE), F32)
        for i in range(CHUNK):
            s = CHUNK - 1 - i
            d = jnp.concatenate([jnp.where(keep, d_ref[i], 0.0), pad], axis=1)
            acc = acc + (d if s == 0 else pltpu.roll(d, s, axis=1))
        o_ref[...] = _hdot_nt(acc, _bias_line_onehot(clip, tp))

    return pl.pallas_call(
        body, in_specs=[pl.BlockSpec(memory_space=pltpu.VMEM)], out_specs=pl.BlockSpec(memory_space=pltpu.VMEM),
        out_shape=jax.ShapeDtypeStruct((nh, tp), F32), name=name, compiler_params=_cp())(dbias)


ATT_TILE = LEFT_CHUNKS * CHUNK


ATT_GROUP = 8


def _att_softmax(s, bias, n_chunk):
    slot = lax.broadcasted_iota(jnp.int32, (1, s.shape[1]), 1)
    before_start = jnp.where(slot < (LEFT_CHUNKS - n_chunk) * CHUNK, NEG_INF, 0.0)
    s = s + bias + before_start
    p = jnp.exp(s - jnp.max(s, axis=-1, keepdims=True))
    return p / jnp.sum(p, axis=-1, keepdims=True)


def _att_specs(n_pairs):
    prev = lambda p, i: (jnp.maximum(i - 1, 0), p)
    cur = lambda p, i: (i, p)
    prev_v = lambda p, i: (jnp.maximum(i - 1, 0), n_pairs + p)
    cur_v = lambda p, i: (i, n_pairs + p)
    blk = (ATT_TILE, LANES)
    return [pl.BlockSpec(blk, prev), pl.BlockSpec(blk, cur), pl.BlockSpec(blk, prev_v), pl.BlockSpec(blk, cur_v)]


def _att_fill(kbuf, vbuf, kp_ref, kc_ref, vp_ref, vc_ref):
    t = ATT_TILE
    kbuf[0:t, :] = kp_ref[...].astype(BF16)
    kbuf[t:2 * t, :] = kc_ref[...].astype(BF16)
    kbuf[2 * t:, :] = jnp.zeros((CHUNK, LANES), BF16)
    vbuf[0:t, :] = vp_ref[...].astype(BF16)
    vbuf[t:2 * t, :] = vc_ref[...].astype(BF16)
    vbuf[2 * t:, :] = jnp.zeros((CHUNK, LANES), BF16)


def _stack_heads(x, first):
    return jnp.concatenate([jnp.where(first, x, 0.0), jnp.where(first, 0.0, x)], axis=0).astype(BF16)


def attn_fwd(qp, kv, bias, name, rider=None):
    n, bw = qp.shape
    n_pairs = bw // LANES
    t = ATT_TILE
    cpt = t // CHUNK

    def body(q_ref, kp_ref, kc_ref, vp_ref, vc_ref, b_ref, o_ref, kbuf, vbuf):
        i = pl.program_id(1)
        _att_fill(kbuf, vbuf, kp_ref, kc_ref, vp_ref, vc_ref)
        lane = lax.broadcasted_iota(jnp.int32, (CHUNK, LANES), 1)
        first = lane < B_HEAD
        for g0 in range(0, cpt, ATT_GROUP):
            chunks = list(range(g0, min(g0 + ATT_GROUP, cpt)))
            band = [slice(c * CHUNK, c * CHUNK + BAND_PAD) for c in chunks]
            q2 = [_stack_heads(q_ref[c * CHUNK:(c + 1) * CHUNK, :] * (B_HEAD ** -0.5), first) for c in chunks]
            s = [_dot_nt(q_u, kbuf[b_u, :]) for q_u, b_u in zip(q2, band)]
            p = [_att_softmax(s_u, b_ref[0], i * cpt + c) for s_u, c in zip(s, chunks)]
            o = [_dot(p_u.astype(BF16), vbuf[b_u, :]) for p_u, b_u in zip(p, band)]
            for o_u, c in zip(o, chunks):
                o_ref[c * CHUNK:(c + 1) * CHUNK, :] = jnp.where(first, o_u[:CHUNK], o_u[CHUNK:])

    return host_call(
        body, grid=(n_pairs, n // t),
        in_specs=[pl.BlockSpec((t, LANES), lambda p, i: (i, p))] + _att_specs(n_pairs)
        + [pl.BlockSpec((1, 2 * CHUNK, BAND_PAD), lambda p, i: (p, 0, 0))],
        out_specs=[pl.BlockSpec((t, LANES), lambda p, i: (i, p))],
        out_shape=[jax.ShapeDtypeStruct((n, bw), F32)],
        scratch_shapes=[pltpu.VMEM((2 * t + CHUNK, LANES), BF16), pltpu.VMEM((2 * t + CHUNK, LANES), BF16)],
        name=name, sem=("parallel", "parallel"), args=(qp, kv, kv, kv, kv, bias), rider=rider)


def attn_bwd(qp, kv, bias, d_o, dk_in, dv_in, name, rider=None):
    n, bw = qp.shape
    n_pairs = bw // LANES
    t = ATT_TILE
    cpt = t // CHUNK
    nt = n // t
    have_in = dk_in is not None
    scale = B_HEAD ** -0.5

    def body(*refs):
        q_ref, kp_ref, kc_ref, vp_ref, vc_ref, b_ref, do_ref = refs[:7]
        pos = 7
        if have_in:
            dki_ref, dvi_ref = refs[7:9]
            pos = 9
        dq_ref, dk_ref, dv_ref, db_ref, kbuf, vbuf, dkacc, dvacc = refs[pos:]
        j = pl.program_id(1)
        i = nt - 1 - j
        _att_fill(kbuf, vbuf, kp_ref, kc_ref, vp_ref, vc_ref)

        @pl.when(j == 0)
        def _():
            dkacc[...] = jnp.zeros_like(dkacc)
            dvacc[...] = jnp.zeros_like(dvacc)
            db_ref[...] = jnp.zeros_like(db_ref)

        @pl.when(j > 0)
        def _():
            dkacc[t:2 * t, :] = dkacc[0:t, :]
            dvacc[t:2 * t, :] = dvacc[0:t, :]
            dkacc[0:t, :] = jnp.zeros((t, LANES), F32)
            dvacc[0:t, :] = jnp.zeros((t, LANES), F32)

        lane = lax.broadcasted_iota(jnp.int32, (CHUNK, LANES), 1)
        first = lane < B_HEAD
        for g0 in range(0, cpt, ATT_GROUP):
            chunks = list(range(g0, min(g0 + ATT_GROUP, cpt)))
            rows = [slice(c * CHUNK, (c + 1) * CHUNK) for c in chunks]
            band = [slice(c * CHUNK, c * CHUNK + BAND_PAD) for c in chunks]
            q2 = [_stack_heads(q_ref[r, :] * scale, first) for r in rows]
            do2 = [_stack_heads(do_ref[r, :], first) for r in rows]
            s = [_dot_nt(q_u, kbuf[b_u, :]) for q_u, b_u in zip(q2, band)]
            dp = [_dot_nt(d_u, vbuf[b_u, :]) for d_u, b_u in zip(do2, band)]
            p = [_att_softmax(s_u, b_ref[0], i * cpt + c) for s_u, c in zip(s, chunks)]
            ds = [p_u * (dp_u - jnp.sum(dp_u * p_u, axis=-1, keepdims=True)) for p_u, dp_u in zip(p, dp)]
            dsb = [d_u.astype(BF16) for d_u in ds]
            dv = [_dot_tn(p_u.astype(BF16), d_u) for p_u, d_u in zip(p, do2)]
            dq = [_dot(d_u, kbuf[b_u, :]) * scale for d_u, b_u in zip(dsb, band)]
            dk = [_dot_tn(d_u, q_u) for d_u, q_u in zip(dsb, q2)]
            db_ref[0] += functools.reduce(lambda a, b: a + b, ds)
            for r in range(chunks[0], chunks[-1] + BAND // CHUNK):
                terms = [(u, r - c) for u, c in enumerate(chunks) if 0 <= r - c < BAND // CHUNK]
                blk = slice(r * CHUNK, (r + 1) * CHUNK)
                dvacc[blk, :] += functools.reduce(lambda a, b: a + b, [dv[u][o * CHUNK:(o + 1) * CHUNK] for u, o in terms])
                dkacc[blk, :] += functools.reduce(lambda a, b: a + b, [dk[u][o * CHUNK:(o + 1) * CHUNK] for u, o in terms])
            for u in range(len(chunks)):
                dq_ref[rows[u], :] = jnp.where(first, dq[u][:CHUNK], dq[u][CHUNK:])

        if have_in:
            dk_ref[...] = dkacc[t:2 * t, :] + dki_ref[...]
            dv_ref[...] = dvacc[t:2 * t, :] + dvi_ref[...]
        else:
            dk_ref[...] = dkacc[t:2 * t, :]
            dv_ref[...] = dvacc[t:2 * t, :]

    rev = lambda p, j: (nt - 1 - j, p)
    tok = pl.BlockSpec((t, LANES), rev)
    kv_specs = [pl.BlockSpec((t, LANES), lambda p, j: (jnp.maximum(nt - 2 - j, 0), p)),
                pl.BlockSpec((t, LANES), rev),
                pl.BlockSpec((t, LANES), lambda p, j: (jnp.maximum(nt - 2 - j, 0), n_pairs + p)),
                pl.BlockSpec((t, LANES), lambda p, j: (nt - 1 - j, n_pairs + p))]
    in_specs = [tok] + kv_specs + [pl.BlockSpec((1, 2 * CHUNK, BAND_PAD), lambda p, j: (p, 0, 0)), tok]
    args = [qp, kv, kv, kv, kv, bias, d_o]
    if have_in:
        in_specs += [tok, tok]
        args += [dk_in, dv_in]
    out = jax.ShapeDtypeStruct((n, bw), F32)
    return host_call(
        body, grid=(n_pairs, nt), in_specs=in_specs,
        out_specs=[tok, tok, tok, pl.BlockSpec((1, 2 * CHUNK, BAND_PAD), lambda p, j: (p, 0, 0))],
        out_shape=[out, out, out, jax.ShapeDtypeStruct((n_pairs, 2 * CHUNK, BAND_PAD), F32)],
        scratch_shapes=[pltpu.VMEM((2 * t + CHUNK, LANES), BF16), pltpu.VMEM((2 * t + CHUNK, LANES), BF16),
                        pltpu.VMEM((2 * t + CHUNK, LANES), F32), pltpu.VMEM((2 * t + CHUNK, LANES), F32)],
        name=name, sem=("parallel", "arbitrary"), args=args, rider=rider)


def adamw(w, gstack, m, v, name):
    r, c = w.shape
    s = gstack.shape[0]
    tr = _pick(r, 512, 8)

    def body(w_ref, g_ref, m_ref, v_ref, go_ref, d_ref, mo_ref, vo_ref):
        g = g_ref[0].astype(F32)
        for k in range(1, s):
            g = g + g_ref[k].astype(F32)
        mn = ADAM_B1 * m_ref[...] + (1.0 - ADAM_B1) * g
        vn = ADAM_B2 * v_ref[...] + (1.0 - ADAM_B2) * (g * g)
        m_hat = mn / (1.0 - ADAM_B1 ** ADAM_STEP)
        v_hat = vn / (1.0 - ADAM_B2 ** ADAM_STEP)
        go_ref[...] = g
        d_ref[...] = -ADAM_LR * (m_hat / (jnp.sqrt(v_hat) + ADAM_EPS) + ADAM_WD * w_ref[...])
        mo_ref[...] = mn
        vo_ref[...] = vn

    blk = pl.BlockSpec((tr, c), lambda i: (i, 0))
    out = jax.ShapeDtypeStruct((r, c), F32)
    return pl.pallas_call(
        body, grid=(r // tr,),
        in_specs=[blk, pl.BlockSpec((s, tr, c), lambda i: (0, i, 0)), blk, blk],
        out_specs=[blk] * 4, out_shape=[out] * 4, name=name,
        compiler_params=_cp(("parallel",)))(w, gstack, m, v)


def _place():
    x, y, c = lax.axis_index("x"), lax.axis_index("y"), lax.axis_index("c")
    chips = [(1 - x, y), (x, 1 - y), (1 - x, 1 - y)]
    return x, y, c, chips


def _ag_copy(outs, send_sems, recv_sems, t, k, block, to, src=None):
    def slot(dev):
        return outs[t].at[4 * dev[0] + 2 * dev[1] + dev[2]]
    return pltpu.make_async_remote_copy(
        src_ref=slot(block) if src is None else src, dst_ref=slot(block),
        send_sem=send_sems.at[7 * t + k], recv_sem=recv_sems.at[7 * t + k], device_id=to, device_id_type=MESH)


def _ag_start(ins, outs, send_sems, recv_sems, local_sems):
    x, y, c, chips = _place()
    me = (x, y, c)
    for t in range(len(ins)):
        pltpu.make_async_copy(ins[t], outs[t].at[4 * x + 2 * y + c], local_sems.at[t]).start()
        _ag_copy(outs, send_sems, recv_sems, t, 0, me, (x, y, 1 - c), src=ins[t]).start()
        for j, chip in enumerate(chips):
            _ag_copy(outs, send_sems, recv_sems, t, 1 + j, me, (*chip, c), src=ins[t]).start()


def _ag_finish(ins, outs, send_sems, recv_sems, local_sems):
    x, y, c, chips = _place()
    me, sibling = (x, y, c), (x, y, 1 - c)
    nt = len(ins)
    for t in range(nt):
        for j, chip in enumerate(chips):
            _ag_copy(outs, send_sems, recv_sems, t, 1 + j, (*chip, c), me).wait_recv()
            _ag_copy(outs, send_sems, recv_sems, t, 4 + j, (*chip, c), sibling).start()
    for t in range(nt):
        _ag_copy(outs, send_sems, recv_sems, t, 0, sibling, me).wait_recv()
        for j, chip in enumerate(chips):
            _ag_copy(outs, send_sems, recv_sems, t, 4 + j, (*chip, 1 - c), me).wait_recv()
    for t in range(nt):
        _ag_copy(outs, send_sems, recv_sems, t, 0, me, sibling, src=ins[t]).wait_send()
        for j, chip in enumerate(chips):
            _ag_copy(outs, send_sems, recv_sems, t, 1 + j, me, (*chip, c), src=ins[t]).wait_send()
            _ag_copy(outs, send_sems, recv_sems, t, 4 + j, (*chip, c), sibling).wait_send()
        pltpu.make_async_copy(ins[t], outs[t].at[4 * x + 2 * y + c], local_sems.at[t]).wait()


def _rs_a_copy(ins, outs, send_sems, recv_sems, t, q):
    x, y, c, _ = _place()
    return pltpu.make_async_remote_copy(
        src_ref=ins[t].at[2 * q + (1 - c)], dst_ref=outs[t].at[q],
        send_sem=send_sems.at[4 * t + q], recv_sem=recv_sems.at[4 * t + q],
        device_id=(x, y, 1 - c), device_id_type=MESH)


def _rs_a_start(ins, outs, send_sems, recv_sems, local_sems):
    for t in range(len(ins)):
        for q in range(4):
            _rs_a_copy(ins, outs, send_sems, recv_sems, t, q).start()


def _rs_a_finish(ins, outs, send_sems, recv_sems, local_sems):
    for t in range(len(ins)):
        for q in range(4):
            _rs_a_copy(ins, outs, send_sems, recv_sems, t, q).wait_recv()
    for t in range(len(ins)):
        for q in range(4):
            _rs_a_copy(ins, outs, send_sems, recv_sems, t, q).wait_send()


def _rs_b_copy(ins, outs, send_sems, recv_sems, t, j, sending):
    x, y, c, chips = _place()
    mine, other = 2 * x + y, 2 * chips[j][0] + chips[j][1]
    return pltpu.make_async_remote_copy(
        src_ref=ins[t].at[other if sending else mine], dst_ref=outs[t].at[mine if sending else other],
        send_sem=send_sems.at[3 * t + j], recv_sem=recv_sems.at[3 * t + j],
        device_id=(*chips[j], c), device_id_type=MESH)


def _rs_b_start(ins, outs, send_sems, recv_sems, local_sems):
    x, y, _, _ = _place()
    for t in range(len(ins)):
        for j in range(3):
            _rs_b_copy(ins, outs, send_sems, recv_sems, t, j, True).start()
        pltpu.make_async_copy(ins[t].at[2 * x + y], outs[t].at[2 * x + y], local_sems.at[t]).start()


def _rs_b_finish(ins, outs, send_sems, recv_sems, local_sems):
    x, y, _, _ = _place()
    for t in range(len(ins)):
        for j in range(3):
            _rs_b_copy(ins, outs, send_sems, recv_sems, t, j, False).wait_recv()
    for t in range(len(ins)):
        for j in range(3):
            _rs_b_copy(ins, outs, send_sems, recv_sems, t, j, True).wait_send()
        pltpu.make_async_copy(ins[t].at[2 * x + y], outs[t].at[2 * x + y], local_sems.at[t]).wait()


_EXCHANGES = {
    "all_gather": (7, lambda a: (N_DEV, *a.shape), _ag_start, _ag_finish),
    "rs_sibling": (4, lambda a: (4, *a.shape[1:]), _rs_a_start, _rs_a_finish),
    "rs_chips": (3, lambda a: a.shape, _rs_b_start, _rs_b_finish),
}


def _exchange_parts(kind, arrays):
    per, shape_of, start, finish = _EXCHANGES[kind]
    n = len(arrays)
    out_shape = [jax.ShapeDtypeStruct(shape_of(a), a.dtype) for a in arrays]
    sems = [pltpu.SemaphoreType.DMA((per * n,)), pltpu.SemaphoreType.DMA((per * n,)), pltpu.SemaphoreType.DMA((n,))]
    return out_shape, sems, start, finish


def exchange(kind, arrays, name):
    n = len(arrays)
    out_shape, sems, start, finish = _exchange_parts(kind, arrays)
    any_spec = pl.BlockSpec(memory_space=pl.ANY)

    def body(*refs):
        ins, outs, sem_refs = refs[:n], refs[n:2 * n], refs[2 * n:]
        start(ins, outs, *sem_refs)
        finish(ins, outs, *sem_refs)

    return pl.pallas_call(body, in_specs=[any_spec] * n, out_specs=[any_spec] * n, out_shape=out_shape,
                          scratch_shapes=sems, name=name)(*arrays)


def host_call(body, *, grid, in_specs, out_specs, out_shape, scratch_shapes, args, name, sem, rider=None):
    if not rider:
        outs = pl.pallas_call(body, grid=grid, in_specs=in_specs, out_specs=out_specs, out_shape=out_shape,
                              scratch_shapes=scratch_shapes, name=name, compiler_params=_cp(sem))(*args)
        return outs, None
    riders = [rider] if isinstance(rider, tuple) else list(rider)
    arrays = [a for _, arrs in riders for a in arrs]
    parts = [_exchange_parts(kind, arrs) for kind, arrs in riders]
    counts = [len(arrs) for _, arrs in riders]
    nr, ni, no, ns = len(arrays), len(in_specs), len(out_specs), len(scratch_shapes)
    any_spec = pl.BlockSpec(memory_space=pl.ANY)

    def wrapped(*refs):
        ins, r_ins = refs[:ni], refs[ni:ni + nr]
        outs, r_outs = refs[ni + nr:ni + nr + no], refs[ni + nr + no:ni + 2 * nr + no]
        scratch, sem_refs = refs[ni + 2 * nr + no:ni + 2 * nr + no + ns], refs[ni + 2 * nr + no + ns:]
        first = pl.program_id(0) == 0
        last = pl.program_id(0) == grid[0] - 1
        for ax in range(1, len(grid)):
            first = first & (pl.program_id(ax) == 0)
            last = last & (pl.program_id(ax) == grid[ax] - 1)

        def each(which):
            pos = 0
            for e, (cnt, part) in enumerate(zip(counts, parts)):
                part[which](r_ins[pos:pos + cnt], r_outs[pos:pos + cnt], *sem_refs[3 * e:3 * e + 3])
                pos += cnt

        @pl.when(first)
        def _():
            each(2)
        body(*ins, *outs, *scratch)

        @pl.when(last)
        def _():
            each(3)

    outs = pl.pallas_call(
        wrapped, grid=grid, in_specs=list(in_specs) + [any_spec] * nr, out_specs=list(out_specs) + [any_spec] * nr,
        out_shape=list(out_shape) + [s for p in parts for s in p[0]],
        scratch_shapes=list(scratch_shapes) + [s for p in parts for s in p[1]], name=name,
        compiler_params=_cp(("arbitrary",) * len(grid)))(*args, *arrays)
    got, pos = [], no
    for cnt in counts:
        got.append(outs[pos:pos + cnt])
        pos += cnt
    return outs[:no], (got[0] if isinstance(rider, tuple) else got)


def pair_add(g8, recv, c_idx, name):
    _, r, c = g8.shape
    tr = _pick(r, 512, 8)

    def body(c_ref, g_ref, r_ref, o_ref):
        o_ref[...] = (g_ref[...] + r_ref[...]).astype(BF16)

    return pl.pallas_call(
        body,
        grid_spec=pltpu.PrefetchScalarGridSpec(
            num_scalar_prefetch=1, grid=(4, r // tr),
            in_specs=[pl.BlockSpec((1, tr, c), lambda q, i, cr: (2 * q + cr[0], i, 0)),
                      pl.BlockSpec((1, tr, c), lambda q, i, cr: (q, i, 0))],
            out_specs=pl.BlockSpec((1, tr, c), lambda q, i, cr: (q, i, 0))),
        out_shape=jax.ShapeDtypeStruct((4, r, c), BF16), name=name,
        compiler_params=_cp(("parallel", "parallel")))(c_idx, g8, recv)


def all_reduce_small(pack, name):
    r, c = pack.shape

    def body(x_ref, o_ref, buf, send_sems, recv_sems, local_sem):
        x, y, cc, chips = _place()
        me, sibling = (x, y, cc), (x, y, 1 - cc)

        def slot(dev):
            return buf.at[4 * dev[0] + 2 * dev[1] + dev[2]]

        def copy(k, block, to, src=None):
            return pltpu.make_async_remote_copy(
                src_ref=slot(block) if src is None else src, dst_ref=slot(block),
                send_sem=send_sems.at[k], recv_sem=recv_sems.at[k], device_id=to, device_id_type=MESH)

        mine = pltpu.make_async_copy(x_ref, slot(me), local_sem)
        mine.start()
        first = [copy(0, me, sibling, src=x_ref)]
        first += [copy(1 + j, me, (*chip, cc), src=x_ref) for j, chip in enumerate(chips)]
        for cp in first:
            cp.start()
        passed = [copy(4 + j, (*chip, cc), sibling) for j, chip in enumerate(chips)]
        for j, chip in enumerate(chips):
            copy(1 + j, (*chip, cc), me).wait_recv()
            passed[j].start()
        copy(0, sibling, me).wait_recv()
        for j, chip in enumerate(chips):
            copy(4 + j, (*chip, 1 - cc), me).wait_recv()
        for cp in first + passed:
            cp.wait_send()
        mine.wait()
        acc = buf[0]
        for k in range(1, N_DEV):
            acc = acc + buf[k]
        o_ref[...] = acc

    return pl.pallas_call(
        body, in_specs=[pl.BlockSpec(memory_space=pltpu.VMEM)],
        out_specs=pl.BlockSpec(memory_space=pltpu.VMEM),
        out_shape=jax.ShapeDtypeStruct((r, c), F32),
        scratch_shapes=[pltpu.VMEM((N_DEV, r, c), F32), pltpu.SemaphoreType.DMA((7,)),
                        pltpu.SemaphoreType.DMA((7,)), pltpu.SemaphoreType.DMA],
        name=name, compiler_params=_cp())(pack)


def _row(v):
    return v.reshape(1, -1)


def _lane_row(vals, offset):
    return jnp.pad(vals, (offset, LANES - offset - vals.shape[0])).reshape(1, LANES)


def _bias_to_pairs(b):
    i, nh, bp = b.shape
    return b.transpose(1, 0, 2).reshape(nh // 2, 2 * i, bp)


def _bias_from_pairs(b):
    p, i2, bp = b.shape
    return b.reshape(2 * p, i2 // 2, bp).transpose(1, 0, 2)


class LocalWeights:
    def __init__(self, W):
        self.W = W
        self.grads = {}

    def big(self, l, la):
        W = self.W
        out = {"f_w_up": W["f_w_up"][l].T, "f_w_down": W["f_w_down"][l]}
        if l < la:
            out.update(a_w_in=W["a_w_in"][l], a_w_out=W["a_w_out"][l])
        else:
            out.update(b_w_q=W["b_w_q"][l - la], b_w_out=W["b_w_out"][l - la])
        if l == la:
            out["w_kv"] = W["w_kv"].T
        return out

    def prep_rider(self, l):
        return None

    def prep_got(self, l, got):
        pass

    def fwd_rider(self, l):
        return None

    def fwd_got(self, l, got):
        pass

    def bwd_rider_a(self, l):
        return None

    def bwd_got_a(self, l, got):
        pass

    def bwd_rider_b(self, l):
        return None

    def bwd_got_b(self, l, got):
        pass

    def bwd_rider_c(self, l):
        return None

    def bwd_got_c(self, l, got):
        pass

    def ffn_grads_ready(self, l, grads):
        pass

    def grads_ready(self, l, grads):
        for k_, g in grads.items():
            self.grads.setdefault(k_, {})[l] = g

    def stacked(self):
        out = {k_: (jnp.stack([v_[l] for l in sorted(v_)]) if k_ != "w_kv" else next(iter(v_.values())))
               for k_, v_ in self.grads.items()}
        out["f_w_up"] = jnp.swapaxes(out["f_w_up"], 1, 2)
        out["w_kv"] = out["w_kv"].T
        return out


def _named(name, l, rider):
    return name if rider is None else f"{name}_x{l}"


def local_step(x, target, W, comm=None):
    comm = LocalWeights(W) if comm is None else comm
    n, d = x.shape
    la, ha = W["a_A_log"].shape
    lb, hb, tbl = W["b_rel_bias"].shape
    depth = W["f_norm"].shape[0]
    clip = (tbl - 1) // 2
    tp = -(-tbl // LANES) * LANES
    qk = ha * A_HEAD
    cw = 3 * qk
    bw = hb * B_HEAD
    a_in = cw + qk + 2 * ha

    h = x
    saves = []
    kv = h_kv = w_kv = None
    for l in range(depth):
        big = comm.big(l, la)
        sv = {"h_in": h, "big": big}
        rider = comm.fwd_rider(l)
        if l < la:
            alog = _lane_row(W["a_A_log"][l], ha)
            dtb = _lane_row(W["a_dt_bias"][l], ha)
            proj = norm_matmul(h, _row(W["a_norm"][l]), big["a_w_in"], "a_in_proj")
            early = comm.prep_rider(l)
            (q, k, v, bb, gb, u), got = gdn_prep(proj, W["a_conv"][l], alog, dtb, ha, _named("gdn_prep", l, early), early)
            comm.prep_got(l, got)
            (o, states, tinv), got = gdn_fwd(q, k, v, bb, gb, ha, _named("gdn_fwd", l, rider), rider)
            h, y = gdn_out(o, proj, _row(W["a_out_norm"][l]), big["a_w_out"], h, ha, "gdn_out")
            sv.update(proj=proj, q=q, k=k, v=v, bb=bb, gb=gb, u=u, states=states, tinv=tinv, o=o, y=y, alog=alog, dtb=dtb)
        else:
            j = l - la
            if j == 0:
                h_kv, w_kv = h, big["w_kv"]
                kv = norm_matmul(h, _row(W["kv_norm"]), w_kv, "kv_proj", w_t=True)
            qp = norm_matmul(h, _row(W["b_norm"][j]), big["b_w_q"], "b_q_proj")
            tblp = jnp.pad(W["b_rel_bias"][j], ((0, 0), (0, tp - tbl)))
            bias = _bias_to_pairs(bias_expand(tblp, clip, "bias_expand"))
            (o,), got = attn_fwd(qp, kv, bias, _named("attn_fwd", l, rider), rider)
            h = matmul_res(o, big["b_w_out"], h, "b_out_proj")
            sv.update(qp=qp, bias=bias, o=o)
        comm.fwd_got(l, got)
        sv["h_mid"] = h
        up = norm_matmul(h, _row(W["f_norm"][l]), big["f_w_up"], "f_up_proj", out_dtype=BF16, w_t=True)
        h, act, hc = ffn_act_down(up, W["f_conv"][l], _row(W["f_conv_b"][l]), big["f_w_down"], h, "ffn_act_down")
        sv.update(up=up, act=act, hc=hc)
        saves.append(sv)

    loss, dh, d_final = loss_head(h, _row(W["final_norm"]), target)

    G = {k_: [None] * (la if k_.startswith("a_") else lb if k_.startswith("b_") else depth)
         for k_ in ("a_norm", "a_conv", "a_A_log", "a_dt_bias", "a_out_norm",
                    "b_norm", "b_rel_bias", "f_norm", "f_conv", "f_conv_b")}
    G["final_norm"] = d_final[0]
    dk_acc = dv_acc = None
    for l in reversed(range(depth)):
        sv = saves[l]
        big = sv["big"]
        gbig = {}
        dhc, dcb = ffn_bwd_act(dh, sv["hc"], big["f_w_down"], "ffn_bwd_act")
        gbig["f_w_down"] = matmul_tn(sv["act"], dh, "f_down_wgrad")
        G["f_conv_b"][l] = dcb[0]
        rider = comm.bwd_rider_a(l)
        (dh, dup, dcw, dg), got = ffn_bwd_up(dhc, sv["up"], W["f_conv"][l], big["f_w_up"], sv["h_mid"], dh,
                                             _row(W["f_norm"][l]), _named("ffn_bwd_up", l, rider), rider)
        comm.bwd_got_a(l, got)
        G["f_conv"][l] = dcw
        G["f_norm"][l] = dg[0]
        gbig["f_w_up"] = norm_matmul_tn(sv["h_mid"], _row(W["f_norm"][l]), dup, "f_up_wgrad", transposed=True)
        comm.ffn_grads_ready(l, gbig)
        rider = comm.bwd_rider_b(l)
        if l < la:
            w_in = big["a_w_in"]
            do, dz, dwn = gdn_out_bwd(dh, sv["o"], sv["proj"], _row(W["a_out_norm"][l]), big["a_w_out"], ha, "gdn_out_bwd")
            G["a_out_norm"][l] = dwn[0]
            gbig["a_w_out"] = matmul_tn(sv["y"], dh, "a_out_wgrad")
            (dq, dk, dv, dbb, dgb), got = gdn_bwd(sv["q"], sv["k"], sv["v"], sv["bb"], sv["gb"], sv["states"], sv["tinv"], do, ha,
                                                  _named("gdn_bwd", l, rider), rider)
            comm.bwd_got_b(l, got)
            du, dba, dal, ddt = gdn_prep_bwd(sv["proj"], sv["u"], sv["alog"], sv["dtb"],
                                             dq, dk, dv, dbb, dgb, ha, "gdn_prep_bwd")
            G["a_A_log"][l] = dal[0, ha:2 * ha]
            G["a_dt_bias"][l] = ddt[0, ha:2 * ha]
            rider = comm.bwd_rider_c(l)
            (dqkv, dconv), got = conv_bwd(du, sv["proj"], W["a_conv"][l], A_CONV, _named("gdn_conv_bwd", l, rider), rider)
            if rider is not None:
                comm.bwd_got_c(l, got)
            G["a_conv"][l] = dconv
            gam = _row(W["a_norm"][l])
            pieces = [(dqkv, w_in[:, :cw]), (dz, w_in[:, cw:cw + qk]), (dba, w_in[:, cw + qk:])]
            gbig["a_w_in"] = jnp.concatenate(
                [norm_matmul_tn(sv["h_in"], gam, dqkv, "a_in_wgrad_qkv"),
                 norm_matmul_tn(sv["h_in"], gam, dz, "a_in_wgrad_z"),
                 norm_matmul_tn(sv["h_in"], gam, dba, "a_in_wgrad_ba")[:, :2 * ha]], axis=1)
            dh, dg = dx_norm_bwd(dh, sv["h_in"], gam, pieces, "a_in_dx")
            G["a_norm"][l] = dg[0]
        else:
            j = l - la
            d_o = matmul_nt(dh, big["b_w_out"], "b_out_dx")
            gbig["b_w_out"] = matmul_tn(sv["o"], dh, "b_out_wgrad")
            (dq, dk_acc, dv_acc, dbias), got = attn_bwd(
                sv["qp"], kv, sv["bias"], d_o, dk_acc, dv_acc,
                _named("attn_bwd" if dk_acc is None else "attn_bwd_acc", l, rider), rider)
            comm.bwd_got_b(l, got)
            G["b_rel_bias"][j] = bias_expand_bwd(_bias_from_pairs(dbias), clip, tp, "bias_expand_bwd")[:, :tbl]
            gam = _row(W["b_norm"][j])
            gbig["b_w_q"] = norm_matmul_tn(sv["h_in"], gam, dq, "b_q_wgrad")
            dh, dg = dx_norm_bwd(dh, sv["h_in"], gam, [(dq, big["b_w_q"])], "b_q_dx")
            G["b_norm"][j] = dg[0]
            if j == 0:
                gam = _row(W["kv_norm"])
                gbig["w_kv"] = jnp.concatenate([norm_matmul_tn(h_kv, gam, dk_acc, "kv_wgrad_k", transposed=True),
                                                norm_matmul_tn(h_kv, gam, dv_acc, "kv_wgrad_v", transposed=True)], axis=0)
                dh, dg = dx_norm_bwd(dh, h_kv, gam, [(dk_acc, w_kv[:bw]), (dv_acc, w_kv[bw:])], "kv_dx", w_t=True)
                G["kv_norm"] = dg[0]
        comm.grads_ready(l, gbig)
    out = {k_: (jnp.stack(v_) if isinstance(v_, list) else v_) for k_, v_ in G.items()}
    return loss[0, 0], dh, out, comm


WEIGHTS = ["a_norm", "a_w_in", "a_conv", "a_A_log", "a_dt_bias", "a_out_norm", "a_w_out", "kv_norm", "w_kv",
           "b_norm", "b_w_q", "b_rel_bias", "b_w_out", "f_norm", "f_w_up", "f_conv", "f_conv_b", "f_w_down",
           "final_norm"]
SHARD_AXIS = {"a_norm": 1, "a_w_in": 2, "a_conv": 2, "a_w_out": 1, "w_kv": 1, "b_w_q": 1, "b_w_out": 1,
              "f_w_up": 2, "f_conv": 2, "f_w_down": 1}
BIG = ["a_w_in", "a_w_out", "w_kv", "b_w_q", "b_w_out", "f_w_up", "f_w_down"]
SMALL_SHARDED = ["a_norm", "a_conv", "f_conv"]
TRANSPOSED = ("f_w_up", "w_kv")


def _t_view(k, a):
    return jnp.swapaxes(a, -1, -2) if k in TRANSPOSED else a


def _unstack(g, axis):
    if axis == 0:
        return g.reshape(-1, *g.shape[2:])
    return jnp.concatenate([g[i] for i in range(N_DEV)], axis=axis)


def _to_blocks(full, axis):
    if axis == 0:
        return full.reshape(N_DEV, -1, full.shape[-1])
    return jnp.stack(jnp.split(full, N_DEV, axis=axis))


def _pack(arrs):
    flat = []
    for a in arrs:
        f = a.reshape(-1)
        flat.append(jnp.pad(f, (0, (-f.shape[0]) % LANES)))
    f = jnp.concatenate(flat)
    f = jnp.pad(f, (0, (-f.shape[0]) % (8 * LANES)))
    return f.reshape(-1, LANES)


def _unpack(pack, shapes):
    flat = pack.reshape(-1)
    out, pos = [], 0
    for s in shapes:
        sz = math.prod(s)
        out.append(flat[pos:pos + sz].reshape(s))
        pos += sz + (-sz) % LANES
    return out


def _as2d(a):
    return a.reshape(1, -1) if a.ndim == 1 else a.reshape(-1, a.shape[-1])


def kernel(x, a_norm, a_w_in, a_conv, a_A_log, a_dt_bias, a_out_norm, a_w_out, kv_norm, w_kv, b_norm, b_w_q, b_rel_bias, b_w_out, f_norm, f_w_up, f_conv, f_conv_b, f_w_down, final_norm, loss_target, m_a_norm, m_a_w_in, m_a_conv, m_a_A_log, m_a_dt_bias, m_a_out_norm, m_a_w_out, m_kv_norm, m_w_kv, m_b_norm, m_b_w_q, m_b_rel_bias, m_b_w_out, m_f_norm, m_f_w_up, m_f_conv, m_f_conv_b, m_f_w_down, m_final_norm, v_a_norm, v_a_w_in, v_a_conv, v_a_A_log, v_a_dt_bias, v_a_out_norm, v_a_w_out, v_kv_norm, v_w_kv, v_b_norm, v_b_w_q, v_b_rel_bias, v_b_w_out, v_f_norm, v_f_w_up, v_f_conv, v_f_conv_b, v_f_w_down, v_final_norm):
    w = dict(a_norm=a_norm, a_w_in=a_w_in, a_conv=a_conv, a_A_log=a_A_log, a_dt_bias=a_dt_bias,
             a_out_norm=a_out_norm, a_w_out=a_w_out, kv_norm=kv_norm, w_kv=w_kv, b_norm=b_norm, b_w_q=b_w_q,
             b_rel_bias=b_rel_bias, b_w_out=b_w_out, f_norm=f_norm, f_w_up=f_w_up, f_conv=f_conv,
             f_conv_b=f_conv_b, f_w_down=f_w_down, final_norm=final_norm)
    mom = dict(a_norm=m_a_norm, a_w_in=m_a_w_in, a_conv=m_a_conv, a_A_log=m_a_A_log, a_dt_bias=m_a_dt_bias,
               a_out_norm=m_a_out_norm, a_w_out=m_a_w_out, kv_norm=m_kv_norm, w_kv=m_w_kv, b_norm=m_b_norm,
               b_w_q=m_b_w_q, b_rel_bias=m_b_rel_bias, b_w_out=m_b_w_out, f_norm=m_f_norm, f_w_up=m_f_w_up,
               f_conv=m_f_conv, f_conv_b=m_f_conv_b, f_w_down=m_f_w_down, final_norm=m_final_norm)
    var = dict(a_norm=v_a_norm, a_w_in=v_a_w_in, a_conv=v_a_conv, a_A_log=v_a_A_log, a_dt_bias=v_a_dt_bias,
               a_out_norm=v_a_out_norm, a_w_out=v_a_w_out, kv_norm=v_kv_norm, w_kv=v_w_kv, b_norm=v_b_norm,
               b_w_q=v_b_w_q, b_rel_bias=v_b_rel_bias, b_w_out=v_b_w_out, f_norm=v_f_norm, f_w_up=v_f_w_up,
               f_conv=v_f_conv, f_conv_b=v_f_conv_b, f_w_down=v_f_w_down, final_norm=v_final_norm)
    me = 4 * lax.axis_index("x") + 2 * lax.axis_index("y") + lax.axis_index("c")

    la, depth = a_A_log.shape[0], f_norm.shape[0]
    c_idx = lax.axis_index("c").astype(jnp.int32).reshape(1)
    shard_bf16 = {k: _t_view(k, w[k]).astype(BF16) for k in BIG}
    blk_axis = {k: 0 if k in TRANSPOSED else SHARD_AXIS[k] - (k != "w_kv") for k in BIG}

    class Sharded(LocalWeights):
        def __init__(self):
            super().__init__(w)
            self.full = {}
            self.stacks = {}
            self.pending = None
            self.parts = None

        def names(self, l):
            out = ["a_w_in", "a_w_out"] if l < la else ["b_w_q", "b_w_out"]
            return out + ["f_w_up", "f_w_down"] + (["w_kv"] if l == la else [])

        def index(self, k, l):
            return None if k == "w_kv" else (l - la if k.startswith("b_") else l)

        def shards(self, l, names=None):
            return [shard_bf16[k] if k == "w_kv" else shard_bf16[k][self.index(k, l)]
                    for k in (self.names(l) if names is None else names)]

        def install(self, l, gathered, names=None):
            out = self.full.setdefault(l, {})
            for k, g in zip(self.names(l) if names is None else names, gathered):
                out[k] = _unstack(g, blk_axis[k])
                if k == "a_w_in":
                    out[k] = jnp.pad(out[k], ((0, 0), (0, (-out[k].shape[1]) % LANES)))

        def big(self, l, la_):
            return self.full[l]

        def first_names(self):
            return ["a_w_in"] if la > 0 else self.names(0)

        def prep_rider(self, l):
            rest = [k for k in self.names(0) if k not in self.first_names()]
            return ("all_gather", self.shards(0, rest)) if l == 0 and rest else None

        def prep_got(self, l, got):
            if got is not None:
                self.install(0, got, [k for k in self.names(0) if k not in self.first_names()])

        def fwd_rider(self, l):
            return ("all_gather", self.shards(l + 1)) if l + 1 < depth else None

        def fwd_got(self, l, got):
            if got is not None:
                self.install(l + 1, got)

        def blocks(self, grads, keys):
            return [_to_blocks(grads[k], blk_axis[k]) for k in keys]

        def grads_ready(self, l, grads):
            keys = [k for k in self.names(l) if (k, l) not in self.early_keys]
            self.pending = (l, keys, self.blocks(grads, keys))

        early = early_parts = None
        early_keys = ()

        def ffn_grads_ready(self, l, grads):
            if l == 0 and la > 0:
                keys = ["f_w_up", "f_w_down"]
                self.early = (keys, self.blocks(grads, keys))
                self.early_keys = tuple((k, 0) for k in keys)

        def bwd_rider_a(self, l):
            return None if self.pending is None else ("rs_sibling", self.pending[2])

        def add_pairs(self, g8, from_sibling):
            return [pair_add(g, r, c_idx, "grads_pair_add") for g, r in zip(g8, from_sibling)]

        def bwd_got_a(self, l, got):
            if got is not None:
                self.parts = self.add_pairs(self.pending[2], got)

        def bwd_rider_b(self, l):
            riders = [] if self.parts is None else [("rs_chips", self.parts)]
            if self.early is not None:
                riders.append(("rs_sibling", self.early[1]))
            return riders

        def keep(self, stacks):
            l, keys, _ = self.pending
            for k, s in zip(keys, stacks):
                self.stacks[(k, l)] = s
            self.pending = self.parts = None

        def bwd_got_b(self, l, got):
            got = list(got or [])
            if self.parts is not None:
                self.keep(got.pop(0))
            if self.early is not None and got:
                self.early_parts = self.add_pairs(self.early[1], got.pop(0))

        def bwd_rider_c(self, l):
            return None if self.early_parts is None else ("rs_chips", self.early_parts)

        def bwd_got_c(self, l, got):
            for k, s in zip(self.early[0], got):
                self.stacks[(k, 0)] = s
            self.early = self.early_parts = None

        def finish(self):
            self.parts = self.add_pairs(self.pending[2], exchange("rs_sibling", self.pending[2], "grads_to_sibling"))
            self.keep(exchange("rs_chips", self.parts, "grads_to_chips"))

    comm = Sharded()

    small_shapes = [w[k].shape for k in SMALL_SHARDED]
    gathered = exchange("all_gather", comm.shards(0, comm.first_names()) + [_pack([w[k] for k in SMALL_SHARDED])],
                        "weights_all_gather")
    comm.install(0, gathered[:-1], comm.first_names())
    full = dict(w)
    small = [_unpack(gathered[-1][i], small_shapes) for i in range(N_DEV)]
    for idx, k in enumerate(SMALL_SHARDED):
        full[k] = jnp.concatenate([small[i][idx] for i in range(N_DEV)], axis=SHARD_AXIS[k])

    loss_part, grad_x, G, _ = local_step(x[0], loss_target[0], full, comm)
    comm.finish()
    stacks = []
    for k in BIG:
        layers = sorted(l for (k_, l) in comm.stacks if k_ == k)
        stacks.append(jnp.concatenate([comm.stacks[(k, l)] for l in layers], axis=1))

    small_names = [k for k in WEIGHTS if k not in BIG]
    reduced = _unpack(all_reduce_small(_pack([G[k] for k in small_names] + [loss_part.reshape(1)]), "small_all_reduce"),
                      [G[k].shape for k in small_names] + [(1,)])
    loss = reduced[-1][0]
    small_g = dict(zip(small_names, reduced[:-1]))
    for k in SMALL_SHARDED:
        sz = w[k].shape[SHARD_AXIS[k]]
        small_g[k] = lax.dynamic_slice_in_dim(small_g[k], me * sz, sz, axis=SHARD_AXIS[k])

    res = {}
    for k, st in zip(BIG, stacks):
        tshape = _t_view(k, w[k]).shape
        wt, mt, vt = (_as2d(_t_view(k, a)) for a in (w[k], mom[k], var[k]))
        outs = adamw(wt, st, mt, vt, "adamw_" + k)
        res[k] = [_t_view(k, o.reshape(tshape)) for o in outs]
    for k in small_names:
        outs = adamw(_as2d(w[k]), _as2d(small_g[k])[None], _as2d(mom[k]), _as2d(var[k]), "adamw_" + k)
        res[k] = [o.reshape(w[k].shape) for o in outs]

    return (loss, grad_x[None], *[res[k][0] for k in WEIGHTS], *[res[k][1] for k in WEIGHTS],
            *[res[k][2] for k in WEIGHTS], *[res[k][3] for k in WEIGHTS])
```

```python
import functools
import math

import jax
import jax.numpy as jnp
from jax import lax
from jax.experimental import pallas as pl
from jax.experimental.pallas import tpu as pltpu

F32 = jnp.float32
BF16 = jnp.bfloat16
HI = lax.Precision.HIGHEST
MESH = pl.DeviceIdType.MESH

EPS = 1e-6
NEG_INF = -1e30
CHUNK = 64
LEFT_CHUNKS = 8
BAND = (LEFT_CHUNKS + 1) * CHUNK
BAND_PAD = 640
A_CONV = 4
F_CONV = 3
A_HEAD = 128
B_HEAD = 64
LANES = 128
HALO = 8
N_DEV = 8

ADAM_LR = 0.001
ADAM_B1 = 0.9
ADAM_B2 = 0.999
ADAM_EPS = 1e-08
ADAM_WD = 0.01
ADAM_STEP = 10

VMEM_LIMIT_V7X = 56 * 1024 * 1024
GDN_BWD_HEADS = 8
COL_CHUNK = 256
FFN_TILE = 256


def _cp(sem=None, vmem=VMEM_LIMIT_V7X):
    kw = dict(vmem_limit_bytes=vmem)
    if sem is not None:
        kw["dimension_semantics"] = sem
    return pltpu.CompilerParams(**kw)


def _pick(n, target, q=LANES):
    best = None
    for t in range(q, min(n, target) + 1, q):
        if n % t == 0:
            best = t
    return best if best is not None else n


def _sig(x):
    return 1.0 / (1.0 + jnp.exp(-x))


def _softplus(x):
    return jnp.maximum(x, 0.0) + jnp.log(1.0 + jnp.exp(-jnp.abs(x)))


def _rms(x, g):
    return x * lax.rsqrt(jnp.mean(x * x, axis=-1, keepdims=True) + EPS) * g


def _rms_bwd(x, g, dxn):
    r = lax.rsqrt(jnp.mean(x * x, axis=-1, keepdims=True) + EPS)
    gd = dxn * g
    dx = r * gd - x * (r * r * r) * jnp.mean(x * gd, axis=-1, keepdims=True)
    dg = jnp.sum(dxn * x * r, axis=0, keepdims=True)
    return dx, dg


def _dot(a, b):
    return jnp.dot(a, b, preferred_element_type=F32)


def _dot_nt(a, b):
    return lax.dot_general(a, b, (((1,), (1,)), ((), ())), preferred_element_type=F32)


def _dot_tn(a, b):
    return lax.dot_general(a, b, (((0,), (0,)), ((), ())), preferred_element_type=F32)


def _hdot(a, b):
    return jnp.dot(a, b, precision=HI, preferred_element_type=F32)


def _hdot_nt(a, b):
    return lax.dot_general(a, b, (((1,), (1,)), ((), ())), precision=HI, preferred_element_type=F32)


def _resident(shape, index_map):
    return pl.BlockSpec(shape, index_map, pipeline_mode=pl.Buffered(1))


def norm_matmul(h, gamma, w, name, out_dtype=F32, w_t=False):
    n, d = h.shape
    nc = w.shape[0] if w_t else w.shape[1]
    tm = _pick(n, 2048 if out_dtype == BF16 else 1024, 8)
    tn = _pick(nc, 1536)

    def body(h_ref, g_ref, w_ref, o_ref):
        xn = _rms(h_ref[...], g_ref[...]).astype(BF16)
        o_ref[...] = (_dot_nt(xn, w_ref[...]) if w_t else _dot(xn, w_ref[...])).astype(out_dtype)

    return pl.pallas_call(
        body, grid=(nc // tn, n // tm),
        in_specs=[pl.BlockSpec((tm, d), lambda j, i: (i, 0)),
                  pl.BlockSpec((1, d), lambda j, i: (0, 0)),
                  pl.BlockSpec((tn, d), lambda j, i: (j, 0)) if w_t else pl.BlockSpec((d, tn), lambda j, i: (0, j))],
        out_specs=pl.BlockSpec((tm, tn), lambda j, i: (i, j)),
        out_shape=jax.ShapeDtypeStruct((n, nc), out_dtype), name=name,
        compiler_params=_cp(("parallel", "parallel")))(h, gamma, w)


def norm_matmul_tn(h, gamma, dy, name, transposed=False):
    n, d = h.shape
    nc = dy.shape[1]
    tm = _pick(n, 2048 if dy.dtype == BF16 else 1024, 8)
    tn = _pick(nc, 1536)

    def body(h_ref, g_ref, dy_ref, o_ref):
        @pl.when(pl.program_id(1) == 0)
        def _():
            o_ref[...] = jnp.zeros_like(o_ref)
        xn = _rms(h_ref[...], g_ref[...]).astype(BF16)
        dyb = dy_ref[...].astype(BF16)
        o_ref[...] += _dot_tn(dyb, xn) if transposed else _dot_tn(xn, dyb)

    return pl.pallas_call(
        body, grid=(nc // tn, n // tm),
        in_specs=[pl.BlockSpec((tm, d), lambda j, i: (i, 0)),
                  pl.BlockSpec((1, d), lambda j, i: (0, 0)),
                  pl.BlockSpec((tm, tn), lambda j, i: (i, j))],
        out_specs=pl.BlockSpec((tn, d), lambda j, i: (j, 0)) if transposed else pl.BlockSpec((d, tn), lambda j, i: (0, j)),
        out_shape=jax.ShapeDtypeStruct((nc, d) if transposed else (d, nc), F32), name=name,
        compiler_params=_cp(("parallel", "arbitrary")))(h, gamma, dy)


def matmul_tn(a, dy, name):
    n, ka = a.shape
    nc = dy.shape[1]
    tm = _pick(n, 2048 if a.dtype == BF16 else 1024, 8)
    tk = _pick(ka, 1536)
    tn = _pick(nc, 1024)

    def body(a_ref, dy_ref, o_ref):
        @pl.when(pl.program_id(2) == 0)
        def _():
            o_ref[...] = jnp.zeros_like(o_ref)
        o_ref[...] += _dot_tn(a_ref[...].astype(BF16), dy_ref[...].astype(BF16))

    return pl.pallas_call(
        body, grid=(ka // tk, nc // tn, n // tm),
        in_specs=[pl.BlockSpec((tm, tk), lambda k, j, i: (i, k)),
                  pl.BlockSpec((tm, tn), lambda k, j, i: (i, j))],
        out_specs=pl.BlockSpec((tk, tn), lambda k, j, i: (k, j)),
        out_shape=jax.ShapeDtypeStruct((ka, nc), F32), name=name,
        compiler_params=_cp(("parallel", "parallel", "arbitrary")))(a, dy)


def matmul_res(a, w, h, name):
    n, k = a.shape
    d = w.shape[1]
    tm = _pick(n, 512, 8)

    def body(a_ref, w_ref, h_ref, o_ref):
        o_ref[...] = h_ref[...] + _dot(a_ref[...].astype(BF16), w_ref[...])

    return pl.pallas_call(
        body, grid=(n // tm,),
        in_specs=[pl.BlockSpec((tm, k), lambda i: (i, 0)),
                  _resident((k, d), lambda i: (0, 0)),
                  pl.BlockSpec((tm, d), lambda i: (i, 0))],
        out_specs=pl.BlockSpec((tm, d), lambda i: (i, 0)),
        out_shape=jax.ShapeDtypeStruct((n, d), F32), name=name,
        compiler_params=_cp(("parallel",)))(a, w, h)


def matmul_nt(dy, w, name):
    n, k = dy.shape
    d = w.shape[0]
    tm = _pick(n, 512, 8)

    def body(dy_ref, w_ref, o_ref):
        o_ref[...] = _dot_nt(dy_ref[...].astype(BF16), w_ref[...])

    return pl.pallas_call(
        body, grid=(n // tm,),
        in_specs=[pl.BlockSpec((tm, k), lambda i: (i, 0)),
                  _resident((d, k), lambda i: (0, 0))],
        out_specs=pl.BlockSpec((tm, d), lambda i: (i, 0)),
        out_shape=jax.ShapeDtypeStruct((n, d), F32), name=name,
        compiler_params=_cp(("parallel",)))(dy, w)


def dx_norm_bwd(dout, h, gamma, pieces, name, rider=None, w_t=False):
    n, d = h.shape
    tm = _pick(n, 512, 8)
    np_ = len(pieces)
    mm = _dot if w_t else _dot_nt

    def body(*refs):
        dout_ref, h_ref, g_ref = refs[:3]
        dys = refs[3:3 + np_]
        ws = refs[3 + np_:3 + 2 * np_]
        dh_ref, dg_ref = refs[3 + 2 * np_:]
        dxn = mm(dys[0][...].astype(BF16), ws[0][...])
        for p in range(1, np_):
            dxn = dxn + mm(dys[p][...].astype(BF16), ws[p][...])
        dx, dg = _rms_bwd(h_ref[...], g_ref[...], dxn)
        dh_ref[...] = dout_ref[...] + dx

        @pl.when(pl.program_id(0) == 0)
        def _():
            dg_ref[...] = jnp.zeros_like(dg_ref)
        dg_ref[...] += dg

    in_specs = [pl.BlockSpec((tm, d), lambda i: (i, 0)),
                pl.BlockSpec((tm, d), lambda i: (i, 0)),
                pl.BlockSpec((1, d), lambda i: (0, 0))]
    in_specs += [pl.BlockSpec((tm, dy.shape[1]), lambda i: (i, 0)) for dy, _ in pieces]
    in_specs += [_resident(w.shape, lambda i: (0, 0)) for _, w in pieces]
    (dh, dg), got = host_call(
        body, grid=(n // tm,), in_specs=in_specs,
        out_specs=[pl.BlockSpec((tm, d), lambda i: (i, 0)), pl.BlockSpec((1, d), lambda i: (0, 0))],
        out_shape=[jax.ShapeDtypeStruct((n, d), F32), jax.ShapeDtypeStruct((1, d), F32)], name=name,
        scratch_shapes=[], sem=("arbitrary",), rider=rider,
        args=(dout, h, gamma, *[p[0] for p in pieces], *[p[1] for p in pieces]))
    return (dh, dg) if rider is None else (dh, dg, got)


def loss_head(h, gamma, target, name="loss_head"):
    n, d = h.shape
    tm = _pick(n, 512, 8)

    def body(h_ref, g_ref, t_ref, loss_ref, dh_ref, dg_ref):
        @pl.when(pl.program_id(0) == 0)
        def _():
            loss_ref[...] = jnp.zeros_like(loss_ref)
            dg_ref[...] = jnp.zeros_like(dg_ref)
        x = h_ref[...]
        g = g_ref[...]
        e = _rms(x, g) - t_ref[...]
        part = jnp.sum(jnp.sum(e * e, axis=-1, keepdims=True), axis=0, keepdims=True) * (0.5 / d)
        loss_ref[...] += jnp.broadcast_to(part, loss_ref.shape)
        dx, dg = _rms_bwd(x, g, e * (1.0 / d))
        dh_ref[...] = dx
        dg_ref[...] += dg

    return pl.pallas_call(
        body, grid=(n // tm,),
        in_specs=[pl.BlockSpec((tm, d), lambda i: (i, 0)), pl.BlockSpec((1, d), lambda i: (0, 0)),
                  pl.BlockSpec((tm, d), lambda i: (i, 0))],
        out_specs=[pl.BlockSpec((8, LANES), lambda i: (0, 0)), pl.BlockSpec((tm, d), lambda i: (i, 0)),
                   pl.BlockSpec((1, d), lambda i: (0, 0))],
        out_shape=[jax.ShapeDtypeStruct((8, LANES), F32), jax.ShapeDtypeStruct((n, d), F32),
                   jax.ShapeDtypeStruct((1, d), F32)], name=name,
        compiler_params=_cp(("arbitrary",)))(h, gamma, target)


def _halo_rows(dtype):
    return HALO * (4 // jnp.dtype(dtype).itemsize)


def _prev_halo_map(t, hb=HALO):
    return lambda i: (jnp.maximum(i * (t // hb) - 1, 0), 0)


def _next_halo_map(t, n, hb=HALO):
    return lambda i: (jnp.minimum((i + 1) * (t // hb), n // hb - 1), 0)


def _fill_prev(xs, main_ref, halo_ref, i, cols=slice(None)):
    hb = halo_ref.shape[0]
    xs[0:HALO, :] = jnp.where(i > 0, halo_ref[hb - HALO:hb, cols].astype(F32), 0.0)
    xs[HALO:, :] = main_ref[:, cols].astype(F32)


def _causal_conv(xs, w_ref, width, t, cols=slice(None), xcols=slice(None)):
    x = xs[:, xcols]
    acc = w_ref[width - 1:width, cols] * x[HALO:, :]
    for k in range(width - 1):
        acc = acc + w_ref[k:k + 1, cols] * pltpu.roll(x, width - 1 - k, axis=0)[HALO:, :]
    return acc


def _col_chunks(width, target=COL_CHUNK):
    tc = _pick(width, target)
    return [slice(j * tc, (j + 1) * tc) for j in range(width // tc)]


def ffn_act_down(up, conv_w, conv_b, w_down, h, name):
    n, c2 = up.shape
    ff = c2 // 2
    d = h.shape[1]
    t = _pick(n, 2 * FFN_TILE, 8)
    hb = _halo_rows(up.dtype)
    chunks = _col_chunks(ff)
    tc = chunks[0].stop

    def body(up_ref, halo_ref, cw_ref, cb_ref, wd_ref, h_ref, o_ref, act_ref, hc_ref, xg, xv):
        i = pl.program_id(0)
        acc = h_ref[...]
        for cs in chunks:
            vs = slice(ff + cs.start, ff + cs.stop)
            _fill_prev(xg, up_ref, halo_ref, i, cs)
            _fill_prev(xv, up_ref, halo_ref, i, vs)
            gate = _causal_conv(xg, cw_ref, F_CONV, t, cs) + cb_ref[:, cs]
            val = _causal_conv(xv, cw_ref, F_CONV, t, vs) + cb_ref[:, vs]
            hc_ref[:, cs] = gate.astype(BF16)
            hc_ref[:, vs] = val.astype(BF16)
            act = (gate * _sig(gate) * val).astype(BF16)
            act_ref[:, cs] = act
            acc = acc + _dot(act, wd_ref[cs, :])
        o_ref[...] = acc

    return pl.pallas_call(
        body, grid=(n // t,),
        in_specs=[pl.BlockSpec((t, c2), lambda i: (i, 0)),
                  pl.BlockSpec((hb, c2), _prev_halo_map(t, hb)),
                  pl.BlockSpec((F_CONV, c2), lambda i: (0, 0)),
                  pl.BlockSpec((1, c2), lambda i: (0, 0)),
                  _resident((ff, d), lambda i: (0, 0)),
                  pl.BlockSpec((t, d), lambda i: (i, 0))],
        out_specs=[pl.BlockSpec((t, d), lambda i: (i, 0)), pl.BlockSpec((t, ff), lambda i: (i, 0)),
                   pl.BlockSpec((t, c2), lambda i: (i, 0))],
        out_shape=[jax.ShapeDtypeStruct((n, d), F32), jax.ShapeDtypeStruct((n, ff), BF16),
                   jax.ShapeDtypeStruct((n, c2), BF16)],
        scratch_shapes=[pltpu.VMEM((t + HALO, tc), F32), pltpu.VMEM((t + HALO, tc), F32)], name=name,
        compiler_params=_cp(("parallel",)))(up, up, conv_w, conv_b, w_down, h)


def ffn_bwd_act(dout, hc, w_down, name):
    n, c2 = hc.shape
    ff = c2 // 2
    d = dout.shape[1]
    t = _pick(n, 2 * FFN_TILE, 8)
    chunks = _col_chunks(ff)

    def body(dout_ref, hc_ref, wd_ref, dhc_ref, dcb_ref):
        i = pl.program_id(0)

        @pl.when(i == 0)
        def _():
            dcb_ref[...] = jnp.zeros_like(dcb_ref)
        doutb = dout_ref[...].astype(BF16)
        for cs in chunks:
            vs = slice(ff + cs.start, ff + cs.stop)
            gate = hc_ref[:, cs].astype(F32)
            val = hc_ref[:, vs].astype(F32)
            sg = _sig(gate)
            da = _dot_nt(doutb, wd_ref[cs, :])
            dgate = da * val * (sg * (1.0 + gate * (1.0 - sg)))
            dval = da * gate * sg
            dhc_ref[:, cs] = dgate.astype(BF16)
            dhc_ref[:, vs] = dval.astype(BF16)
            dcb_ref[:, cs] += jnp.sum(dgate, axis=0, keepdims=True)
            dcb_ref[:, vs] += jnp.sum(dval, axis=0, keepdims=True)

    return pl.pallas_call(
        body, grid=(n // t,),
        in_specs=[pl.BlockSpec((t, d), lambda i: (i, 0)),
                  pl.BlockSpec((t, c2), lambda i: (i, 0)),
                  _resident((ff, d), lambda i: (0, 0))],
        out_specs=[pl.BlockSpec((t, c2), lambda i: (i, 0)), pl.BlockSpec((1, c2), lambda i: (0, 0))],
        out_shape=[jax.ShapeDtypeStruct((n, c2), BF16), jax.ShapeDtypeStruct((1, c2), F32)], name=name,
        compiler_params=_cp(("arbitrary",)))(dout, hc, w_down)


def conv_bwd_tail(dy_ref, dnext_ref, x_ref, cw_ref, dcw_ref, ds, width, t, i, last, cols=slice(None)):
    ds[0:t, :] = dy_ref[:, cols].astype(F32)
    ds[t:, :] = jnp.where(i < last, dnext_ref[0:HALO, cols].astype(F32), 0.0)
    x = x_ref[:, cols].astype(F32)
    dall = ds[...]
    dx = None
    for k in range(width):
        off = width - 1 - k
        shifted = dall[0:t, :] if off == 0 else pltpu.roll(dall, t + HALO - off, axis=0)[0:t, :]
        term = cw_ref[k:k + 1, cols] * shifted
        dx = term if dx is None else dx + term
        dcw_ref[k:k + 1, cols] += jnp.sum(shifted * x, axis=0, keepdims=True)
    return dx


def ffn_bwd_up(dhc, up, conv_w, w_up, h, dout, gamma, name, rider=None):
    n, c2 = up.shape
    d = h.shape[1]
    t = _pick(n, FFN_TILE, 8)
    last = n // t - 1
    chunks = _col_chunks(c2)
    tc = chunks[0].stop

    def body(dhc_ref, dnext_ref, up_ref, cw_ref, wu_ref, h_ref, dout_ref, g_ref,
             dh_ref, dup_ref, dcw_ref, dg_ref, ds):
        i = pl.program_id(0)

        @pl.when(i == 0)
        def _():
            dcw_ref[...] = jnp.zeros_like(dcw_ref)
            dg_ref[...] = jnp.zeros_like(dg_ref)
        dxn = jnp.zeros((t, d), F32)
        for cs in chunks:
            dup = conv_bwd_tail(dhc_ref, dnext_ref, up_ref, cw_ref, dcw_ref, ds, F_CONV, t, i, last, cs)
            dupb = dup.astype(BF16)
            dup_ref[:, cs] = dupb
            dxn = dxn + _dot(dupb, wu_ref[cs, :])
        dx, dg = _rms_bwd(h_ref[...], g_ref[...], dxn)
        dh_ref[...] = dout_ref[...] + dx
        dg_ref[...] += dg

    return host_call(
        body, grid=(n // t,), rider=rider, sem=("arbitrary",), args=(dhc, dhc, up, conv_w, w_up, h, dout, gamma),
        in_specs=[pl.BlockSpec((t, c2), lambda i: (i, 0)),
                  pl.BlockSpec((_halo_rows(dhc.dtype), c2), _next_halo_map(t, n, _halo_rows(dhc.dtype))),
                  pl.BlockSpec((t, c2), lambda i: (i, 0)),
                  pl.BlockSpec((F_CONV, c2), lambda i: (0, 0)),
                  _resident((c2, d), lambda i: (0, 0)),
                  pl.BlockSpec((t, d), lambda i: (i, 0)),
                  pl.BlockSpec((t, d), lambda i: (i, 0)),
                  pl.BlockSpec((1, d), lambda i: (0, 0))],
        out_specs=[pl.BlockSpec((t, d), lambda i: (i, 0)), pl.BlockSpec((t, c2), lambda i: (i, 0)),
                   pl.BlockSpec((F_CONV, c2), lambda i: (0, 0)), pl.BlockSpec((1, d), lambda i: (0, 0))],
        out_shape=[jax.ShapeDtypeStruct((n, d), F32), jax.ShapeDtypeStruct((n, c2), BF16),
                   jax.ShapeDtypeStruct((F_CONV, c2), F32), jax.ShapeDtypeStruct((1, d), F32)],
        scratch_shapes=[pltpu.VMEM((t + HALO, tc), F32)], name=name)


def _gdn_head(uq, uk, uv, pba, alog, dtb, head, n_heads):
    lane = lax.broadcasted_iota(jnp.int32, pba.shape, 1)
    sq = uq * _sig(uq)
    q = sq * lax.rsqrt(jnp.sum(sq * sq, axis=-1, keepdims=True) + EPS) * (A_HEAD ** -0.5)
    sk = uk * _sig(uk)
    k = sk * lax.rsqrt(jnp.sum(sk * sk, axis=-1, keepdims=True) + EPS)
    v = uv * _sig(uv)
    beta = jnp.sum(jnp.where(lane == head, _sig(pba), 0.0), axis=-1, keepdims=True)
    g_all = -jnp.exp(alog) * _softplus(pba + dtb)
    g = jnp.sum(jnp.where(lane == n_heads + head, g_all, 0.0), axis=-1, keepdims=True)
    return q, k, v, jnp.broadcast_to(beta, uq.shape), jnp.broadcast_to(g, uq.shape)


def gdn_prep(proj, conv_w, alog, dtb, n_heads, name, rider=None):
    n = proj.shape[0]
    qk = n_heads * A_HEAD
    cw = 3 * qk
    ba_blk = (cw + qk) // LANES
    t = _pick(n, 256, 8)

    def body(x_ref, halo_ref, pba_ref, cw_ref, al_ref, dt_ref, q_ref, k_ref, v_ref, b_ref, g_ref, u_ref, xs):
        i = pl.program_id(0)
        xs[0:HALO, :] = jnp.where(i > 0, halo_ref[...], 0.0)
        xs[HALO:, :] = x_ref[...]
        pba = pba_ref[...]
        for hd in range(n_heads):
            s0 = slice(hd * A_HEAD, (hd + 1) * A_HEAD)
            s1 = slice(qk + hd * A_HEAD, qk + (hd + 1) * A_HEAD)
            s2 = slice(2 * qk + hd * A_HEAD, 2 * qk + (hd + 1) * A_HEAD)
            uq, uk, uv = [_causal_conv(xs, cw_ref, A_CONV, t, s, s) for s in (s0, s1, s2)]
            u_ref[:, s0] = uq.astype(BF16)
            u_ref[:, s1] = uk.astype(BF16)
            u_ref[:, s2] = uv.astype(BF16)
            q, k, v, bb, gb = _gdn_head(uq, uk, uv, pba, al_ref[...], dt_ref[...], hd, n_heads)
            q_ref[:, s0] = q
            k_ref[:, s0] = k
            v_ref[:, s0] = v
            b_ref[:, s0] = bb
            g_ref[:, s0] = gb

    out = jax.ShapeDtypeStruct((n, qk), F32)
    return host_call(
        body, grid=(n // t,),
        in_specs=[pl.BlockSpec((t, cw), lambda i: (i, 0)),
                  pl.BlockSpec((HALO, cw), _prev_halo_map(t)),
                  pl.BlockSpec((t, LANES), lambda i: (i, ba_blk)),
                  pl.BlockSpec((A_CONV, cw), lambda i: (0, 0)),
                  pl.BlockSpec((1, LANES), lambda i: (0, 0)),
                  pl.BlockSpec((1, LANES), lambda i: (0, 0))],
        out_specs=[pl.BlockSpec((t, qk), lambda i: (i, 0))] * 5 + [pl.BlockSpec((t, cw), lambda i: (i, 0))],
        out_shape=[out] * 5 + [jax.ShapeDtypeStruct((n, cw), BF16)],
        scratch_shapes=[pltpu.VMEM((t + HALO, cw), F32)], name=name,
        sem=("parallel",), args=(proj, proj, proj, conv_w, alog, dtb), rider=rider)


def gdn_prep_bwd(proj, u, alog, dtb, dq, dk, dv, dbb, dgb, n_heads, name):
    n = proj.shape[0]
    qk = n_heads * A_HEAD
    cw = 3 * qk
    ba_blk = (cw + qk) // LANES
    t = _pick(n, 256, 8)

    def body(u_ref, pba_ref, al_ref, dt_ref, dq_ref, dk_ref, dv_ref, dbb_ref, dgb_ref,
             du_ref, dba_ref, dal_ref, ddt_ref):
        i = pl.program_id(0)
        u = u_ref[...].astype(F32)
        pba = pba_ref[...]
        lane0 = lax.broadcasted_iota(jnp.int32, (t, A_HEAD), 1) == 0
        dba = jnp.zeros((t, LANES), F32)
        dal = jnp.zeros((1, LANES), F32)
        ddt = jnp.zeros((1, LANES), F32)
        for hd in range(n_heads):
            s0 = slice(hd * A_HEAD, (hd + 1) * A_HEAD)
            s1 = slice(qk + hd * A_HEAD, qk + (hd + 1) * A_HEAD)
            s2 = slice(2 * qk + hd * A_HEAD, 2 * qk + (hd + 1) * A_HEAD)
            fn = functools.partial(_gdn_head, head=hd, n_heads=n_heads)
            _, vjp = jax.vjp(fn, u[:, s0], u[:, s1], u[:, s2], pba, al_ref[...], dt_ref[...])
            cts = (dq_ref[:, s0], dk_ref[:, s0], dv_ref[:, s0],
                   jnp.where(lane0, dbb_ref[:, s0], 0.0), jnp.where(lane0, dgb_ref[:, s0], 0.0))
            duq, duk, duv, dpba, da, dd = vjp(cts)
            du_ref[:, s0] = duq
            du_ref[:, s1] = duk
            du_ref[:, s2] = duv
            dba = dba + dpba
            dal = dal + da
            ddt = ddt + dd
        dba_ref[...] = dba

        @pl.when(i == 0)
        def _():
            dal_ref[...] = jnp.zeros_like(dal_ref)
            ddt_ref[...] = jnp.zeros_like(ddt_ref)
        dal_ref[...] += dal
        ddt_ref[...] += ddt

    tok = pl.BlockSpec((t, qk), lambda i: (i, 0))
    row = pl.BlockSpec((1, LANES), lambda i: (0, 0))
    return pl.pallas_call(
        body, grid=(n // t,),
        in_specs=[pl.BlockSpec((t, cw), lambda i: (i, 0)),
                  pl.BlockSpec((t, LANES), lambda i: (i, ba_blk)), row, row,
                  tok, tok, tok, tok, tok],
        out_specs=[pl.BlockSpec((t, cw), lambda i: (i, 0)), pl.BlockSpec((t, LANES), lambda i: (i, 0)), row, row],
        out_shape=[jax.ShapeDtypeStruct((n, cw), F32), jax.ShapeDtypeStruct((n, LANES), F32),
                   jax.ShapeDtypeStruct((1, LANES), F32), jax.ShapeDtypeStruct((1, LANES), F32)],
        name=name, compiler_params=_cp(("arbitrary",)))(u, proj, alog, dtb, dq, dk, dv, dbb, dgb)


def conv_bwd(du, x, conv_w, width, name, rider=None):
    n, cw = du.shape
    t = _pick(n, 256, 8)
    last = n // t - 1

    chunks = _col_chunks(cw, LANES)
    tc = chunks[0].stop

    def body(du_ref, dnext_ref, x_ref, cw_ref, dx_ref, dcw_ref, ds):
        i = pl.program_id(0)

        @pl.when(i == 0)
        def _():
            dcw_ref[...] = jnp.zeros_like(dcw_ref)
        for cs in chunks:
            dx_ref[:, cs] = conv_bwd_tail(du_ref, dnext_ref, x_ref, cw_ref, dcw_ref, ds, width, t, i, last, cs)

    return host_call(
        body, grid=(n // t,),
        in_specs=[pl.BlockSpec((t, cw), lambda i: (i, 0)),
                  pl.BlockSpec((HALO, cw), _next_halo_map(t, n)),
                  pl.BlockSpec((t, cw), lambda i: (i, 0)),
                  pl.BlockSpec((width, cw), lambda i: (0, 0))],
        out_specs=[pl.BlockSpec((t, cw), lambda i: (i, 0)), pl.BlockSpec((width, cw), lambda i: (0, 0))],
        out_shape=[jax.ShapeDtypeStruct((n, cw), F32), jax.ShapeDtypeStruct((width, cw), F32)],
        scratch_shapes=[pltpu.VMEM((t + HALO, tc), F32)], name=name,
        sem=("arbitrary",), args=(du, du, x, conv_w), rider=rider)


def _b(x):
    return x.astype(BF16)


def _mm_nn(a, b):
    return _dot(_b(a), _b(b))


def _mm_nt(a, b):
    return _dot_nt(_b(a), _b(b))


def _mm_tn(a, b):
    return _dot_tn(_b(a), _b(b))


@jax.custom_vjp
def _mmg_nn(a, b):
    return _mm_nn(a, b)


_mmg_nn.defvjp(lambda a, b: (_mm_nn(a, b), (a, b)),
               lambda res, dc: (_mm_nt(dc, res[1]), _mm_tn(res[0], dc)))


@jax.custom_vjp
def _mmg_nt(a, b):
    return _mm_nt(a, b)


_mmg_nt.defvjp(lambda a, b: (_mm_nt(a, b), (a, b)),
               lambda res, dc: (_mm_nn(dc, res[1]), _mm_tn(dc, res[0])))


@jax.custom_vjp
def _mmg_tn(a, b):
    return _mm_tn(a, b)


_mmg_tn.defvjp(lambda a, b: (_mm_tn(a, b), (a, b)),
               lambda res, dc: (_mm_nt(res[1], dc), _mm_nn(res[0], dc)))


def _bf16_parts(x, n):
    parts = []
    for _ in range(n):
        p = x.astype(BF16)
        parts.append(p)
        x = x - p.astype(F32)
    return parts


def _dot_f32ish(a, b):
    (ah, al), (bh, bl) = _bf16_parts(a, 2), _bf16_parts(b, 2)
    return _dot(ah, bh) + _dot(ah, bl) + _dot(al, bh)


def _tri_dot(x, transpose):
    c = x.shape[0]
    low = lax.broadcasted_iota(jnp.int32, (c, c), 0) >= lax.broadcasted_iota(jnp.int32, (c, c), 1)
    tri = jnp.where(low, 1.0, 0.0).astype(BF16)
    mm = _dot_tn if transpose else _dot
    return functools.reduce(lambda a, b: a + b, [mm(tri, p) for p in _bf16_parts(x, 3)])


def _cumsum(x):
    return _tri_dot(x, False)


@jax.custom_vjp
def _cumsum_g(x):
    return _tri_dot(x, False)


_cumsum_g.defvjp(lambda x: (_tri_dot(x, False), None), lambda _, ct: (_tri_dot(ct, True),))


def _each(f, *lists):
    return [f(*a) for a in zip(*lists)]


def _unit_lower_inv(ms):
    c = ms[0].shape[0]
    eye = jnp.where(lax.broadcasted_iota(jnp.int32, (c, c), 0) == lax.broadcasted_iota(jnp.int32, (c, c), 1), 1.0, 0.0)
    xs = [eye - m for m in ms]
    pws = _each(_mm_nn, ms, ms)
    for it in range(5):
        xs = _each(lambda x, pw: x + _mm_nn(x, pw), xs, pws)
        if it < 4:
            pws = _each(_mm_nn, pws, pws)
    rs = _each(lambda m, x: eye - x - _dot_f32ish(m, x), ms, xs)
    return _each(lambda x, r: x + _mm_nn(x, r), xs, rs)


@jax.custom_vjp
def _saved_inv_g(ms, xs):
    return xs


_saved_inv_g.defvjp(lambda ms, xs: (xs, xs),
                    lambda xs, dxs: (_each(lambda t, x: -_mm_nt(t, x), _each(_mm_tn, xs, dxs), xs),
                                     [jnp.zeros_like(x) for x in xs]))


def _gdn_chunk(ops, state, q, k, v, bb, gb):
    nn, nt, tn, inv, cum = ops
    c = CHUNK
    ri = lax.broadcasted_iota(jnp.int32, (c, c), 0)
    ci = lax.broadcasted_iota(jnp.int32, (c, c), 1)
    causal = ri >= ci
    strict = ri > ci
    gc = [cum(g) for g in gb]
    decay = [jnp.where(causal, jnp.exp(jnp.where(causal, x[:, :c] - x.T[:c, :], 0.0)), 0.0) for x in gc]
    kb = _each(lambda a, b: a * b, k, bb)
    kk = _each(nt, kb, k)
    m = _each(lambda a, d: jnp.where(strict, a * d, 0.0), kk, decay)
    tinv = inv(m)
    egc = [jnp.exp(x) for x in gc]
    u = _each(nn, tinv, _each(lambda a, b: a * b, v, bb))
    w = _each(nn, tinv, _each(lambda a, b: a * b, kb, egc))
    attn = _each(lambda a, d: a * d, _each(nt, q, k), decay)
    glast = [jnp.sum(g, axis=0, keepdims=True) for g in gb]
    ws = _each(nn, w, state)
    v_new = _each(lambda a, b: a - b, u, ws)
    qs = _each(nn, _each(lambda a, b: a * b, q, egc), state)
    av = _each(nn, attn, v_new)
    o = _each(lambda a, b: a + b, qs, av)
    kv = _each(tn, _each(lambda a, gl, x: a * jnp.exp(gl - x), k, glast, gc), v_new)
    new_state = _each(lambda s, gl, a: s * jnp.exp(gl) + a, state, glast, kv)
    return o, new_state


def gdn_fwd(q, k, v, bb, gb, n_heads, name, rider=None):
    n, w = q.shape
    nc = n // CHUNK
    cb = min(8, nc)
    rows = cb * CHUNK

    def body(q_ref, k_ref, v_ref, b_ref, g_ref, o_ref, st_ref, ti_ref, s_scr):
        @pl.when(pl.program_id(0) == 0)
        def _():
            s_scr[...] = jnp.zeros_like(s_scr)

        def step(c, carry):
            sl = pl.ds(pl.multiple_of(c * CHUNK, CHUNK), CHUNK)
            lanes = [slice(hd * A_HEAD, (hd + 1) * A_HEAD) for hd in range(n_heads)]
            state = [s_scr[hd] for hd in range(n_heads)]
            inverses = []

            def inv(ms):
                inverses.extend(_unit_lower_inv(ms))
                return inverses

            o, new_state = _gdn_chunk((_mm_nn, _mm_nt, _mm_tn, inv, _cumsum), state,
                                      *[[r[sl, ls] for ls in lanes] for r in (q_ref, k_ref, v_ref, b_ref, g_ref)])
            for hd, ls in enumerate(lanes):
                st_ref[hd, pl.ds(c, 1)] = state[hd][None]
                ti_ref[hd, pl.ds(c, 1)] = inverses[hd].astype(BF16)[None]
                o_ref[sl, ls] = o[hd]
                s_scr[hd] = new_state[hd]
            return carry

        lax.fori_loop(0, cb, step, 0)

    tok = pl.BlockSpec((rows, w), lambda j: (j, 0))
    return host_call(
        body, grid=(nc // cb,),
        in_specs=[tok] * 5,
        out_specs=[tok, pl.BlockSpec((n_heads, cb, A_HEAD, A_HEAD), lambda j: (0, j, 0, 0)),
                   pl.BlockSpec((n_heads, cb, CHUNK, CHUNK), lambda j: (0, j, 0, 0))],
        out_shape=[jax.ShapeDtypeStruct(q.shape, F32), jax.ShapeDtypeStruct((n_heads, nc, A_HEAD, A_HEAD), F32),
                   jax.ShapeDtypeStruct((n_heads, nc, CHUNK, CHUNK), BF16)],
        scratch_shapes=[pltpu.VMEM((n_heads, A_HEAD, A_HEAD), F32)], name=name,
        sem=("arbitrary",), args=(q, k, v, bb, gb), rider=rider)


def gdn_bwd(q, k, v, bb, gb, states, tinv, do, n_heads, name, rider=None):
    n, w = q.shape
    nc = n // CHUNK
    cb = min(4, nc)
    rows = cb * CHUNK
    nblk = nc // cb

    def body(q_ref, k_ref, v_ref, b_ref, g_ref, st_ref, ti_ref, do_ref,
             dq_ref, dk_ref, dv_ref, db_ref, dg_ref, ds_scr):
        @pl.when(pl.program_id(0) == 0)
        def _():
            ds_scr[...] = jnp.zeros_like(ds_scr)

        def step(s, carry):
            c = cb - 1 - s
            sl = pl.ds(pl.multiple_of(c * CHUNK, CHUNK), CHUNK)
            for h0 in range(0, n_heads, GDN_BWD_HEADS):
                heads = list(range(h0, min(h0 + GDN_BWD_HEADS, n_heads)))
                lanes = [slice(hd * A_HEAD, (hd + 1) * A_HEAD) for hd in heads]
                state = [st_ref[hd, pl.ds(c, 1)][0] for hd in heads]
                saved = [ti_ref[hd, pl.ds(c, 1)][0].astype(F32) for hd in heads]
                chunk_fn = functools.partial(
                    _gdn_chunk, (_mmg_nn, _mmg_nt, _mmg_tn, lambda ms: _saved_inv_g(ms, saved), _cumsum_g))
                _, vjp = jax.vjp(chunk_fn, state, *[[r[sl, ls] for ls in lanes]
                                                    for r in (q_ref, k_ref, v_ref, b_ref, g_ref)])
                dstate, dq, dk, dv, dbb, dgb = vjp(([do_ref[sl, ls] for ls in lanes], [ds_scr[hd] for hd in heads]))
                for u, (hd, ls) in enumerate(zip(heads, lanes)):
                    ds_scr[hd] = dstate[u]
                    dq_ref[sl, ls] = dq[u]
                    dk_ref[sl, ls] = dk[u]
                    dv_ref[sl, ls] = dv[u]
                    db_ref[sl, ls] = jnp.broadcast_to(jnp.sum(dbb[u], axis=-1, keepdims=True), dbb[u].shape)
                    dg_ref[sl, ls] = jnp.broadcast_to(jnp.sum(dgb[u], axis=-1, keepdims=True), dgb[u].shape)
            return carry

        lax.fori_loop(0, cb, step, 0)

    tok = pl.BlockSpec((rows, w), lambda j: (nblk - 1 - j, 0))
    out = jax.ShapeDtypeStruct(q.shape, F32)
    return host_call(
        body, grid=(nblk,),
        in_specs=[tok] * 5 + [pl.BlockSpec((n_heads, cb, A_HEAD, A_HEAD), lambda j: (0, nblk - 1 - j, 0, 0)),
                              pl.BlockSpec((n_heads, cb, CHUNK, CHUNK), lambda j: (0, nblk - 1 - j, 0, 0)), tok],
        out_specs=[tok] * 5, out_shape=[out] * 5,
        scratch_shapes=[pltpu.VMEM((n_heads, A_HEAD, A_HEAD), F32)], name=name,
        sem=("arbitrary",), args=(q, k, v, bb, gb, states, tinv, do), rider=rider)


def _gdn_gate(oh, zh, w):
    r = lax.rsqrt(jnp.mean(oh * oh, axis=-1, keepdims=True) + EPS)
    return oh * r * w * (zh * _sig(zh))


def gdn_out(o, proj, out_norm, w_out, h, n_heads, name):
    n, vw = o.shape
    d = h.shape[1]
    z_blk = 3 * vw // vw
    t = _pick(n, 512, 8)

    def body(o_ref, z_ref, w_ref, wo_ref, h_ref, out_ref, y_ref):
        for hd in range(n_heads):
            s0 = slice(hd * A_HEAD, (hd + 1) * A_HEAD)
            y_ref[:, s0] = _gdn_gate(o_ref[:, s0], z_ref[:, s0], w_ref[...]).astype(BF16)
        out_ref[...] = h_ref[...] + _dot(y_ref[...], wo_ref[...])

    return pl.pallas_call(
        body, grid=(n // t,),
        in_specs=[pl.BlockSpec((t, vw), lambda i: (i, 0)),
                  pl.BlockSpec((t, vw), lambda i: (i, z_blk)),
                  pl.BlockSpec((1, A_HEAD), lambda i: (0, 0)),
                  _resident((vw, d), lambda i: (0, 0)),
                  pl.BlockSpec((t, d), lambda i: (i, 0))],
        out_specs=[pl.BlockSpec((t, d), lambda i: (i, 0)), pl.BlockSpec((t, vw), lambda i: (i, 0))],
        out_shape=[jax.ShapeDtypeStruct((n, d), F32), jax.ShapeDtypeStruct((n, vw), BF16)], name=name,
        compiler_params=_cp(("parallel",)))(o, proj, out_norm, w_out, h)


def gdn_out_bwd(dout, o, proj, out_norm, w_out, n_heads, name):
    n, vw = o.shape
    d = dout.shape[1]
    z_blk = 3
    t = _pick(n, 512, 8)

    def body(dout_ref, o_ref, z_ref, w_ref, wo_ref, do_ref, dz_ref, dw_ref):
        dy = _dot_nt(dout_ref[...].astype(BF16), wo_ref[...])
        dw = jnp.zeros((1, A_HEAD), F32)
        for hd in range(n_heads):
            s0 = slice(hd * A_HEAD, (hd + 1) * A_HEAD)
            _, vjp = jax.vjp(_gdn_gate, o_ref[:, s0], z_ref[:, s0], w_ref[...])
            doh, dzh, dwh = vjp(dy[:, s0])
            do_ref[:, s0] = doh
            dz_ref[:, s0] = dzh
            dw = dw + dwh

        @pl.when(pl.program_id(0) == 0)
        def _():
            dw_ref[...] = jnp.zeros_like(dw_ref)
        dw_ref[...] += dw

    tok = pl.BlockSpec((t, vw), lambda i: (i, 0))
    return pl.pallas_call(
        body, grid=(n // t,),
        in_specs=[pl.BlockSpec((t, d), lambda i: (i, 0)), tok,
                  pl.BlockSpec((t, vw), lambda i: (i, z_blk)),
                  pl.BlockSpec((1, A_HEAD), lambda i: (0, 0)),
                  _resident((vw, d), lambda i: (0, 0))],
        out_specs=[tok, tok, pl.BlockSpec((1, A_HEAD), lambda i: (0, 0))],
        out_shape=[jax.ShapeDtypeStruct((n, vw), F32), jax.ShapeDtypeStruct((n, vw), F32),
                   jax.ShapeDtypeStruct((1, A_HEAD), F32)], name=name,
        compiler_params=_cp(("arbitrary",)))(dout, o, proj, out_norm, w_out)


BIAS_LINE = 768
BIAS_TOP = BAND + CHUNK - 2


def _bias_line_onehot(clip, tbl_pad):
    r = lax.broadcasted_iota(jnp.int32, (tbl_pad, BIAS_LINE), 0)
    v = lax.broadcasted_iota(jnp.int32, (tbl_pad, BIAS_LINE), 1)
    idx = jnp.clip(BIAS_TOP - v - (CHUNK - 1), -clip, clip) + clip
    return jnp.where((r == idx) & (v <= BIAS_TOP), 1.0, 0.0)


def bias_expand(tbl, clip, name):
    nh, tp = tbl.shape

    def body(t_ref, o_ref):
        line = _hdot(t_ref[...], _bias_line_onehot(clip, tp))
        keep = lax.broadcasted_iota(jnp.int32, (nh, BAND_PAD), 1) < BAND
        for i in range(CHUNK):
            s = CHUNK - 1 - i
            rolled = line if s == 0 else pltpu.roll(line, BIAS_LINE - s, axis=1)
            o_ref[i] = jnp.where(keep, rolled[:, :BAND_PAD], NEG_INF)

    return pl.pallas_call(
        body, in_specs=[pl.BlockSpec(memory_space=pltpu.VMEM)], out_specs=pl.BlockSpec(memory_space=pltpu.VMEM),
        out_shape=jax.ShapeDtypeStruct((CHUNK, nh, BAND_PAD), F32), name=name, compiler_params=_cp())(tbl)


def bias_expand_bwd(dbias, clip, tp, name):
    _, nh, _ = dbias.shape

    def body(d_ref, o_ref):
        keep = lax.broadcasted_iota(jnp.int32, (nh, BAND_PAD), 1) < BAND
        pad = jnp.zeros((nh, BIAS_LINE - BAND_PAD), F32)
        acc = jnp.zeros((nh, BIAS_LINE), F32)
        for i in range(CHUNK):
            s = CHUNK - 1 - i
            d = jnp.concatenate([jnp.where(keep, d_ref[i], 0.0), pad], axis=1)
            acc = acc + (d if s == 0 else pltpu.roll(d, s, axis=1))
        o_ref[...] = _hdot_nt(acc, _bias_line_onehot(clip, tp))

    return pl.pallas_call(
        body, in_specs=[pl.BlockSpec(memory_space=pltpu.VMEM)], out_specs=pl.BlockSpec(memory_space=pltpu.VMEM),
        out_shape=jax.ShapeDtypeStruct((nh, tp), F32), name=name, compiler_params=_cp())(dbias)


ATT_TILE = LEFT_CHUNKS * CHUNK


ATT_GROUP = 8


def _att_softmax(s, bias, n_chunk):
    slot = lax.broadcasted_iota(jnp.int32, (1, s.shape[1]), 1)
    before_start = jnp.where(slot < (LEFT_CHUNKS - n_chunk) * CHUNK, NEG_INF, 0.0)
    s = s + bias + before_start
    p = jnp.exp(s - jnp.max(s, axis=-1, keepdims=True))
    return p / jnp.sum(p, axis=-1, keepdims=True)


def _att_specs(n_pairs):
    prev = lambda p, i: (jnp.maximum(i - 1, 0), p)
    cur = lambda p, i: (i, p)
    prev_v = lambda p, i: (jnp.maximum(i - 1, 0), n_pairs + p)
    cur_v = lambda p, i: (i, n_pairs + p)
    blk = (ATT_TILE, LANES)
    return [pl.BlockSpec(blk, prev), pl.BlockSpec(blk, cur), pl.BlockSpec(blk, prev_v), pl.BlockSpec(blk, cur_v)]


def _att_fill(kbuf, vbuf, kp_ref, kc_ref, vp_ref, vc_ref):
    t = ATT_TILE
    kbuf[0:t, :] = kp_ref[...].astype(BF16)
    kbuf[t:2 * t, :] = kc_ref[...].astype(BF16)
    kbuf[2 * t:, :] = jnp.zeros((CHUNK, LANES), BF16)
    vbuf[0:t, :] = vp_ref[...].astype(BF16)
    vbuf[t:2 * t, :] = vc_ref[...].astype(BF16)
    vbuf[2 * t:, :] = jnp.zeros((CHUNK, LANES), BF16)


def _stack_heads(x, first):
    return jnp.concatenate([jnp.where(first, x, 0.0), jnp.where(first, 0.0, x)], axis=0).astype(BF16)


def attn_fwd(qp, kv, bias, name, rider=None):
    n, bw = qp.shape
    n_pairs = bw // LANES
    t = ATT_TILE
    cpt = t // CHUNK

    def body(q_ref, kp_ref, kc_ref, vp_ref, vc_ref, b_ref, o_ref, kbuf, vbuf):
        i = pl.program_id(1)
        _att_fill(kbuf, vbuf, kp_ref, kc_ref, vp_ref, vc_ref)
        lane = lax.broadcasted_iota(jnp.int32, (CHUNK, LANES), 1)
        first = lane < B_HEAD
        for g0 in range(0, cpt, ATT_GROUP):
            chunks = list(range(g0, min(g0 + ATT_GROUP, cpt)))
            band = [slice(c * CHUNK, c * CHUNK + BAND_PAD) for c in chunks]
            q2 = [_stack_heads(q_ref[c * CHUNK:(c + 1) * CHUNK, :] * (B_HEAD ** -0.5), first) for c in chunks]
            s = [_dot_nt(q_u, kbuf[b_u, :]) for q_u, b_u in zip(q2, band)]
            p = [_att_softmax(s_u, b_ref[0], i * cpt + c) for s_u, c in zip(s, chunks)]
            o = [_dot(p_u.astype(BF16), vbuf[b_u, :]) for p_u, b_u in zip(p, band)]
            for o_u, c in zip(o, chunks):
                o_ref[c * CHUNK:(c + 1) * CHUNK, :] = jnp.where(first, o_u[:CHUNK], o_u[CHUNK:])

    return host_call(
        body, grid=(n_pairs, n // t),
        in_specs=[pl.BlockSpec((t, LANES), lambda p, i: (i, p))] + _att_specs(n_pairs)
        + [pl.BlockSpec((1, 2 * CHUNK, BAND_PAD), lambda p, i: (p, 0, 0))],
        out_specs=[pl.BlockSpec((t, LANES), lambda p, i: (i, p))],
        out_shape=[jax.ShapeDtypeStruct((n, bw), F32)],
        scratch_shapes=[pltpu.VMEM((2 * t + CHUNK, LANES), BF16), pltpu.VMEM((2 * t + CHUNK, LANES), BF16)],
        name=name, sem=("parallel", "parallel"), args=(qp, kv, kv, kv, kv, bias), rider=rider)


def attn_bwd(qp, kv, bias, d_o, dk_in, dv_in, name, rider=None):
    n, bw = qp.shape
    n_pairs = bw // LANES
    t = ATT_TILE
    cpt = t // CHUNK
    nt = n // t
    have_in = dk_in is not None
    scale = B_HEAD ** -0.5

    def body(*refs):
        q_ref, kp_ref, kc_ref, vp_ref, vc_ref, b_ref, do_ref = refs[:7]
        pos = 7
        if have_in:
            dki_ref, dvi_ref = refs[7:9]
            pos = 9
        dq_ref, dk_ref, dv_ref, db_ref, kbuf, vbuf, dkacc, dvacc = refs[pos:]
        j = pl.program_id(1)
        i = nt - 1 - j
        _att_fill(kbuf, vbuf, kp_ref, kc_ref, vp_ref, vc_ref)

        @pl.when(j == 0)
        def _():
            dkacc[...] = jnp.zeros_like(dkacc)
            dvacc[...] = jnp.zeros_like(dvacc)
            db_ref[...] = jnp.zeros_like(db_ref)

        @pl.when(j > 0)
        def _():
            dkacc[t:2 * t, :] = dkacc[0:t, :]
            dvacc[t:2 * t, :] = dvacc[0:t, :]
            dkacc[0:t, :] = jnp.zeros((t, LANES), F32)
            dvacc[0:t, :] = jnp.zeros((t, LANES), F32)

        lane = lax.broadcasted_iota(jnp.int32, (CHUNK, LANES), 1)
        first = lane < B_HEAD
        for g0 in range(0, cpt, ATT_GROUP):
            chunks = list(range(g0, min(g0 + ATT_GROUP, cpt)))
            rows = [slice(c * CHUNK, (c + 1) * CHUNK) for c in chunks]
            band = [slice(c * CHUNK, c * CHUNK + BAND_PAD) for c in chunks]
            q2 = [_stack_heads(q_ref[r, :] * scale, first) for r in rows]
            do2 = [_stack_heads(do_ref[r, :], first) for r in rows]
            s = [_dot_nt(q_u, kbuf[b_u, :]) for q_u, b_u in zip(q2, band)]
            dp = [_dot_nt(d_u, vbuf[b_u, :]) for d_u, b_u in zip(do2, band)]
            p = [_att_softmax(s_u, b_ref[0], i * cpt + c) for s_u, c in zip(s, chunks)]
            ds = [p_u * (dp_u - jnp.sum(dp_u * p_u, axis=-1, keepdims=True)) for p_u, dp_u in zip(p, dp)]
            dsb = [d_u.astype(BF16) for d_u in ds]
            dv = [_dot_tn(p_u.astype(BF16), d_u) for p_u, d_u in zip(p, do2)]
            dq = [_dot(d_u, kbuf[b_u, :]) * scale for d_u, b_u in zip(dsb, band)]
            dk = [_dot_tn(d_u, q_u) for d_u, q_u in zip(dsb, q2)]
            db_ref[0] += functools.reduce(lambda a, b: a + b, ds)
            for r in range(chunks[0], chunks[-1] + BAND // CHUNK):
                terms = [(u, r - c) for u, c in enumerate(chunks) if 0 <= r - c < BAND // CHUNK]
                blk = slice(r * CHUNK, (r + 1) * CHUNK)
                dvacc[blk, :] += functools.reduce(lambda a, b: a + b, [dv[u][o * CHUNK:(o + 1) * CHUNK] for u, o in terms])
                dkacc[blk, :] += functools.reduce(lambda a, b: a + b, [dk[u][o * CHUNK:(o + 1) * CHUNK] for u, o in terms])
            for u in range(len(chunks)):
                dq_ref[rows[u], :] = jnp.where(first, dq[u][:CHUNK], dq[u][CHUNK:])

        if have_in:
            dk_ref[...] = dkacc[t:2 * t, :] + dki_ref[...]
            dv_ref[...] = dvacc[t:2 * t, :] + dvi_ref[...]
        else:
            dk_ref[...] = dkacc[t:2 * t, :]
            dv_ref[...] = dvacc[t:2 * t, :]

    rev = lambda p, j: (nt - 1 - j, p)
    tok = pl.BlockSpec((t, LANES), rev)
    kv_specs = [pl.BlockSpec((t, LANES), lambda p, j: (jnp.maximum(nt - 2 - j, 0), p)),
                pl.BlockSpec((t, LANES), rev),
                pl.BlockSpec((t, LANES), lambda p, j: (jnp.maximum(nt - 2 - j, 0), n_pairs + p)),
                pl.BlockSpec((t, LANES), lambda p, j: (nt - 1 - j, n_pairs + p))]
    in_specs = [tok] + kv_specs + [pl.BlockSpec((1, 2 * CHUNK, BAND_PAD), lambda p, j: (p, 0, 0)), tok]
    args = [qp, kv, kv, kv, kv, bias, d_o]
    if have_in:
        in_specs += [tok, tok]
        args += [dk_in, dv_in]
    out = jax.ShapeDtypeStruct((n, bw), F32)
    return host_call(
        body, grid=(n_pairs, nt), in_specs=in_specs,
        out_specs=[tok, tok, tok, pl.BlockSpec((1, 2 * CHUNK, BAND_PAD), lambda p, j: (p, 0, 0))],
        out_shape=[out, out, out, jax.ShapeDtypeStruct((n_pairs, 2 * CHUNK, BAND_PAD), F32)],
        scratch_shapes=[pltpu.VMEM((2 * t + CHUNK, LANES), BF16), pltpu.VMEM((2 * t + CHUNK, LANES), BF16),
                        pltpu.VMEM((2 * t + CHUNK, LANES), F32), pltpu.VMEM((2 * t + CHUNK, LANES), F32)],
        name=name, sem=("parallel", "arbitrary"), args=args, rider=rider)


def adamw(w, gstack, m, v, name):
    r, c = w.shape
    s = gstack.shape[0]
    tr = _pick(r, 512, 8)

    def body(w_ref, g_ref, m_ref, v_ref, go_ref, d_ref, mo_ref, vo_ref):
        g = g_ref[0].astype(F32)
        for k in range(1, s):
            g = g + g_ref[k].astype(F32)
        mn = ADAM_B1 * m_ref[...] + (1.0 - ADAM_B1) * g
        vn = ADAM_B2 * v_ref[...] + (1.0 - ADAM_B2) * (g * g)
        m_hat = mn / (1.0 - ADAM_B1 ** ADAM_STEP)
        v_hat = vn / (1.0 - ADAM_B2 ** ADAM_STEP)
        go_ref[...] = g
        d_ref[...] = -ADAM_LR * (m_hat / (jnp.sqrt(v_hat) + ADAM_EPS) + ADAM_WD * w_ref[...])
        mo_ref[...] = mn
        vo_ref[...] = vn

    blk = pl.BlockSpec((tr, c), lambda i: (i, 0))
    out = jax.ShapeDtypeStruct((r, c), F32)
    return pl.pallas_call(
        body, grid=(r // tr,),
        in_specs=[blk, pl.BlockSpec((s, tr, c), lambda i: (0, i, 0)), blk, blk],
        out_specs=[blk] * 4, out_shape=[out] * 4, name=name,
        compiler_params=_cp(("parallel",)))(w, gstack, m, v)


def _place():
    x, y, c = lax.axis_index("x"), lax.axis_index("y"), lax.axis_index("c")
    chips = [(1 - x, y), (x, 1 - y), (1 - x, 1 - y)]
    return x, y, c, chips


def _ag_copy(outs, send_sems, recv_sems, t, k, block, to, src=None):
    def slot(dev):
        return outs[t].at[4 * dev[0] + 2 * dev[1] + dev[2]]
    return pltpu.make_async_remote_copy(
        src_ref=slot(block) if src is None else src, dst_ref=slot(block),
        send_sem=send_sems.at[7 * t + k], recv_sem=recv_sems.at[7 * t + k], device_id=to, device_id_type=MESH)


def _ag_start(ins, outs, send_sems, recv_sems, local_sems):
    x, y, c, chips = _place()
    me = (x, y, c)
    for t in range(len(ins)):
        pltpu.make_async_copy(ins[t], outs[t].at[4 * x + 2 * y + c], local_sems.at[t]).start()
        _ag_copy(outs, send_sems, recv_sems, t, 0, me, (x, y, 1 - c), src=ins[t]).start()
        for j, chip in enumerate(chips):
            _ag_copy(outs, send_sems, recv_sems, t, 1 + j, me, (*chip, c), src=ins[t]).start()


def _ag_finish(ins, outs, send_sems, recv_sems, local_sems):
    x, y, c, chips = _place()
    me, sibling = (x, y, c), (x, y, 1 - c)
    nt = len(ins)
    for t in range(nt):
        for j, chip in enumerate(chips):
            _ag_copy(outs, send_sems, recv_sems, t, 1 + j, (*chip, c), me).wait_recv()
            _ag_copy(outs, send_sems, recv_sems, t, 4 + j, (*chip, c), sibling).start()
    for t in range(nt):
        _ag_copy(outs, send_sems, recv_sems, t, 0, sibling, me).wait_recv()
        for j, chip in enumerate(chips):
            _ag_copy(outs, send_sems, recv_sems, t, 4 + j, (*chip, 1 - c), me).wait_recv()
    for t in range(nt):
        _ag_copy(outs, send_sems, recv_sems, t, 0, me, sibling, src=ins[t]).wait_send()
        for j, chip in enumerate(chips):
            _ag_copy(outs, send_sems, recv_sems, t, 1 + j, me, (*chip, c), src=ins[t]).wait_send()
            _ag_copy(outs, send_sems, recv_sems, t, 4 + j, (*chip, c), sibling).wait_send()
        pltpu.make_async_copy(ins[t], outs[t].at[4 * x + 2 * y + c], local_sems.at[t]).wait()


def _rs_a_copy(ins, outs, send_sems, recv_sems, t, q):
    x, y, c, _ = _place()
    return pltpu.make_async_remote_copy(
        src_ref=ins[t].at[2 * q + (1 - c)], dst_ref=outs[t].at[q],
        send_sem=send_sems.at[4 * t + q], recv_sem=recv_sems.at[4 * t + q],
        device_id=(x, y, 1 - c), device_id_type=MESH)


def _rs_a_start(ins, outs, send_sems, recv_sems, local_sems):
    for t in range(len(ins)):
        for q in range(4):
            _rs_a_copy(ins, outs, send_sems, recv_sems, t, q).start()


def _rs_a_finish(ins, outs, send_sems, recv_sems, local_sems):
    for t in range(len(ins)):
        for q in range(4):
            _rs_a_copy(ins, outs, send_sems, recv_sems, t, q).wait_recv()
    for t in range(len(ins)):
        for q in range(4):
            _rs_a_copy(ins, outs, send_sems, recv_sems, t, q).wait_send()


def _rs_b_copy(ins, outs, send_sems, recv_sems, t, j, sending):
    x, y, c, chips = _place()
    mine, other = 2 * x + y, 2 * chips[j][0] + chips[j][1]
    return pltpu.make_async_remote_copy(
        src_ref=ins[t].at[other if sending else mine], dst_ref=outs[t].at[mine if sending else other],
        send_sem=send_sems.at[3 * t + j], recv_sem=recv_sems.at[3 * t + j],
        device_id=(*chips[j], c), device_id_type=MESH)


def _rs_b_start(ins, outs, send_sems, recv_sems, local_sems):
    x, y, _, _ = _place()
    for t in range(len(ins)):
        for j in range(3):
            _rs_b_copy(ins, outs, send_sems, recv_sems, t, j, True).start()
        pltpu.make_async_copy(ins[t].at[2 * x + y], outs[t].at[2 * x + y], local_sems.at[t]).start()


def _rs_b_finish(ins, outs, send_sems, recv_sems, local_sems):
    x, y, _, _ = _place()
    for t in range(len(ins)):
        for j in range(3):
            _rs_b_copy(ins, outs, send_sems, recv_sems, t, j, False).wait_recv()
    for t in range(len(ins)):
        for j in range(3):
            _rs_b_copy(ins, outs, send_sems, recv_sems, t, j, True).wait_send()
        pltpu.make_async_copy(ins[t].at[2 * x + y], outs[t].at[2 * x + y], local_sems.at[t]).wait()


_EXCHANGES = {
    "all_gather": (7, lambda a: (N_DEV, *a.shape), _ag_start, _ag_finish),
    "rs_sibling": (4, lambda a: (4, *a.shape[1:]), _rs_a_start, _rs_a_finish),
    "rs_chips": (3, lambda a: a.shape, _rs_b_start, _rs_b_finish),
}


def _exchange_parts(kind, arrays):
    per, shape_of, start, finish = _EXCHANGES[kind]
    n = len(arrays)
    out_shape = [jax.ShapeDtypeStruct(shape_of(a), a.dtype) for a in arrays]
    sems = [pltpu.SemaphoreType.DMA((per * n,)), pltpu.SemaphoreType.DMA((per * n,)), pltpu.SemaphoreType.DMA((n,))]
    return out_shape, sems, start, finish


def exchange(kind, arrays, name):
    n = len(arrays)
    out_shape, sems, start, finish = _exchange_parts(kind, arrays)
    any_spec = pl.BlockSpec(memory_space=pl.ANY)

    def body(*refs):
        ins, outs, sem_refs = refs[:n], refs[n:2 * n], refs[2 * n:]
        start(ins, outs, *sem_refs)
        finish(ins, outs, *sem_refs)

    return pl.pallas_call(body, in_specs=[any_spec] * n, out_specs=[any_spec] * n, out_shape=out_shape,
                          scratch_shapes=sems, name=name)(*arrays)


def host_call(body, *, grid, in_specs, out_specs, out_shape, scratch_shapes, args, name, sem, rider=None):
    if not rider:
        outs = pl.pallas_call(body, grid=grid, in_specs=in_specs, out_specs=out_specs, out_shape=out_shape,
                              scratch_shapes=scratch_shapes, name=name, compiler_params=_cp(sem))(*args)
        return outs, None
    riders = [rider] if isinstance(rider, tuple) else list(rider)
    arrays = [a for _, arrs in riders for a in arrs]
    parts = [_exchange_parts(kind, arrs) for kind, arrs in riders]
    counts = [len(arrs) for _, arrs in riders]
    nr, ni, no, ns = len(arrays), len(in_specs), len(out_specs), len(scratch_shapes)
    any_spec = pl.BlockSpec(memory_space=pl.ANY)

    def wrapped(*refs):
        ins, r_ins = refs[:ni], refs[ni:ni + nr]
        outs, r_outs = refs[ni + nr:ni + nr + no], refs[ni + nr + no:ni + 2 * nr + no]
        scratch, sem_refs = refs[ni + 2 * nr + no:ni + 2 * nr + no + ns], refs[ni + 2 * nr + no + ns:]
        first = pl.program_id(0) == 0
        last = pl.program_id(0) == grid[0] - 1
        for ax in range(1, len(grid)):
            first = first & (pl.program_id(ax) == 0)
            last = last & (pl.program_id(ax) == grid[ax] - 1)

        def each(which):
            pos = 0
            for e, (cnt, part) in enumerate(zip(counts, parts)):
                part[which](r_ins[pos:pos + cnt], r_outs[pos:pos + cnt], *sem_refs[3 * e:3 * e + 3])
                pos += cnt

        @pl.when(first)
        def _():
            each(2)
        body(*ins, *outs, *scratch)

        @pl.when(last)
        def _():
            each(3)

    outs = pl.pallas_call(
        wrapped, grid=grid, in_specs=list(in_specs) + [any_spec] * nr, out_specs=list(out_specs) + [any_spec] * nr,
        out_shape=list(out_shape) + [s for p in parts for s in p[0]],
        scratch_shapes=list(scratch_shapes) + [s for p in parts for s in p[1]], name=name,
        compiler_params=_cp(("arbitrary",) * len(grid)))(*args, *arrays)
    got, pos = [], no
    for cnt in counts:
        got.append(outs[pos:pos + cnt])
        pos += cnt
    return outs[:no], (got[0] if isinstance(rider, tuple) else got)


def pair_add(g8, recv, c_idx, name):
    _, r, c = g8.shape
    tr = _pick(r, 512, 8)

    def body(c_ref, g_ref, r_ref, o_ref):
        o_ref[...] = (g_ref[...] + r_ref[...]).astype(BF16)

    return pl.pallas_call(
        body,
        grid_spec=pltpu.PrefetchScalarGridSpec(
            num_scalar_prefetch=1, grid=(4, r // tr),
            in_specs=[pl.BlockSpec((1, tr, c), lambda q, i, cr: (2 * q + cr[0], i, 0)),
                      pl.BlockSpec((1, tr, c), lambda q, i, cr: (q, i, 0))],
            out_specs=pl.BlockSpec((1, tr, c), lambda q, i, cr: (q, i, 0))),
        out_shape=jax.ShapeDtypeStruct((4, r, c), BF16), name=name,
        compiler_params=_cp(("parallel", "parallel")))(c_idx, g8, recv)


def all_reduce_small(pack, name):
    r, c = pack.shape

    def body(x_ref, o_ref, buf, send_sems, recv_sems, local_sem):
        x, y, cc, chips = _place()
        me, sibling = (x, y, cc), (x, y, 1 - cc)

        def slot(dev):
            return buf.at[4 * dev[0] + 2 * dev[1] + dev[2]]

        def copy(k, block, to, src=None):
            return pltpu.make_async_remote_copy(
                src_ref=slot(block) if src is None else src, dst_ref=slot(block),
                send_sem=send_sems.at[k], recv_sem=recv_sems.at[k], device_id=to, device_id_type=MESH)

        mine = pltpu.make_async_copy(x_ref, slot(me), local_sem)
        mine.start()
        first = [copy(0, me, sibling, src=x_ref)]
        first += [copy(1 + j, me, (*chip, cc), src=x_ref) for j, chip in enumerate(chips)]
        for cp in first:
            cp.start()
        passed = [copy(4 + j, (*chip, cc), sibling) for j, chip in enumerate(chips)]
        for j, chip in enumerate(chips):
            copy(1 + j, (*chip, cc), me).wait_recv()
            passed[j].start()
        copy(0, sibling, me).wait_recv()
        for j, chip in enumerate(chips):
            copy(4 + j, (*chip, 1 - cc), me).wait_recv()
        for cp in first + passed:
            cp.wait_send()
        mine.wait()
        acc = buf[0]
        for k in range(1, N_DEV):
            acc = acc + buf[k]
        o_ref[...] = acc

    return pl.pallas_call(
        body, in_specs=[pl.BlockSpec(memory_space=pltpu.VMEM)],
        out_specs=pl.BlockSpec(memory_space=pltpu.VMEM),
        out_shape=jax.ShapeDtypeStruct((r, c), F32),
        scratch_shapes=[pltpu.VMEM((N_DEV, r, c), F32), pltpu.SemaphoreType.DMA((7,)),
                        pltpu.SemaphoreType.DMA((7,)), pltpu.SemaphoreType.DMA],
        name=name, compiler_params=_cp())(pack)


def _row(v):
    return v.reshape(1, -1)


def _lane_row(vals, offset):
    return jnp.pad(vals, (offset, LANES - offset - vals.shape[0])).reshape(1, LANES)


def _bias_to_pairs(b):
    i, nh, bp = b.shape
    return b.transpose(1, 0, 2).reshape(nh // 2, 2 * i, bp)


def _bias_from_pairs(b):
    p, i2, bp = b.shape
    return b.reshape(2 * p, i2 // 2, bp).transpose(1, 0, 2)


class LocalWeights:
    def __init__(self, W):
        self.W = W
        self.grads = {}

    def big(self, l, la):
        W = self.W
        out = {"f_w_up": W["f_w_up"][l].T, "f_w_down": W["f_w_down"][l]}
        if l < la:
            out.update(a_w_in=W["a_w_in"][l], a_w_out=W["a_w_out"][l])
        else:
            out.update(b_w_q=W["b_w_q"][l - la], b_w_out=W["b_w_out"][l - la])
        if l == la:
            out["w_kv"] = W["w_kv"].T
        return out

    def prep_rider(self, l):
        return None

    def prep_got(self, l, got):
        pass

    def fwd_rider(self, l):
        return None

    def fwd_got(self, l, got):
        pass

    def bwd_rider_a(self, l):
        return None

    def bwd_got_a(self, l, got):
        pass

    def bwd_rider_b(self, l):
        return None

    def bwd_got_b(self, l, got):
        pass

    def bwd_rider_c(self, l):
        return None

    def bwd_got_c(self, l, got):
        pass

    def ffn_grads_ready(self, l, grads):
        pass

    def grads_ready(self, l, grads):
        for k_, g in grads.items():
            self.grads.setdefault(k_, {})[l] = g

    def stacked(self):
        out = {k_: (jnp.stack([v_[l] for l in sorted(v_)]) if k_ != "w_kv" else next(iter(v_.values())))
               for k_, v_ in self.grads.items()}
        out["f_w_up"] = jnp.swapaxes(out["f_w_up"], 1, 2)
        out["w_kv"] = out["w_kv"].T
        return out


def _named(name, l, rider):
    return name if rider is None else f"{name}_x{l}"


def local_step(x, target, W, comm=None):
    comm = LocalWeights(W) if comm is None else comm
    n, d = x.shape
    la, ha = W["a_A_log"].shape
    lb, hb, tbl = W["b_rel_bias"].shape
    depth = W["f_norm"].shape[0]
    clip = (tbl - 1) // 2
    tp = -(-tbl // LANES) * LANES
    qk = ha * A_HEAD
    cw = 3 * qk
    bw = hb * B_HEAD

    h = x
    saves = []
    kv = h_kv = w_kv = None
    for l in range(depth):
        big = comm.big(l, la)
        sv = {"h_in": h, "big": big}
        rider = comm.fwd_rider(l)
        if l < la:
            alog = _lane_row(W["a_A_log"][l], ha)
            dtb = _lane_row(W["a_dt_bias"][l], ha)
            proj = norm_matmul(h, _row(W["a_norm"][l]), big["a_w_in"], "a_in_proj")
            early = comm.prep_rider(l)
            (q, k, v, bb, gb, u), got = gdn_prep(proj, W["a_conv"][l], alog, dtb, ha, _named("gdn_prep", l, early), early)
            comm.prep_got(l, got)
            (o, states, tinv), got = gdn_fwd(q, k, v, bb, gb, ha, _named("gdn_fwd", l, rider), rider)
            h, y = gdn_out(o, proj, _row(W["a_out_norm"][l]), big["a_w_out"], h, ha, "gdn_out")
            sv.update(proj=proj, q=q, k=k, v=v, bb=bb, gb=gb, u=u, states=states, tinv=tinv, o=o, y=y, alog=alog, dtb=dtb)
        else:
            j = l - la
            if j == 0:
                h_kv, w_kv = h, big["w_kv"]
                kv = norm_matmul(h, _row(W["kv_norm"]), w_kv, "kv_proj", w_t=True)
            qp = norm_matmul(h, _row(W["b_norm"][j]), big["b_w_q"], "b_q_proj")
            tblp = jnp.pad(W["b_rel_bias"][j], ((0, 0), (0, tp - tbl)))
            bias = _bias_to_pairs(bias_expand(tblp, clip, "bias_expand"))
            (o,), got = attn_fwd(qp, kv, bias, _named("attn_fwd", l, rider), rider)
            h = matmul_res(o, big["b_w_out"], h, "b_out_proj")
            sv.update(qp=qp, bias=bias, o=o)
        comm.fwd_got(l, got)
        sv["h_mid"] = h
        up = norm_matmul(h, _row(W["f_norm"][l]), big["f_w_up"], "f_up_proj", out_dtype=BF16, w_t=True)
        h, act, hc = ffn_act_down(up, W["f_conv"][l], _row(W["f_conv_b"][l]), big["f_w_down"], h, "ffn_act_down")
        sv.update(up=up, act=act, hc=hc)
        saves.append(sv)

    loss, dh, d_final = loss_head(h, _row(W["final_norm"]), target)

    G = {k_: [None] * (la if k_.startswith("a_") else lb if k_.startswith("b_") else depth)
         for k_ in ("a_norm", "a_conv", "a_A_log", "a_dt_bias", "a_out_norm",
                    "b_norm", "b_rel_bias", "f_norm", "f_conv", "f_conv_b")}
    G["final_norm"] = d_final[0]
    dk_acc = dv_acc = None
    for l in reversed(range(depth)):
        sv = saves[l]
        big = sv["big"]
        gbig = {}
        dhc, dcb = ffn_bwd_act(dh, sv["hc"], big["f_w_down"], "ffn_bwd_act")
        gbig["f_w_down"] = matmul_tn(sv["act"], dh, "f_down_wgrad")
        G["f_conv_b"][l] = dcb[0]
        rider = comm.bwd_rider_a(l)
        (dh, dup, dcw, dg), got = ffn_bwd_up(dhc, sv["up"], W["f_conv"][l], big["f_w_up"], sv["h_mid"], dh,
                                             _row(W["f_norm"][l]), _named("ffn_bwd_up", l, rider), rider)
        comm.bwd_got_a(l, got)
        G["f_conv"][l] = dcw
        G["f_norm"][l] = dg[0]
        gbig["f_w_up"] = norm_matmul_tn(sv["h_mid"], _row(W["f_norm"][l]), dup, "f_up_wgrad", transposed=True)
        comm.ffn_grads_ready(l, gbig)
        rider = comm.bwd_rider_b(l)
        if l < la:
            w_in = big["a_w_in"]
            do, dz, dwn = gdn_out_bwd(dh, sv["o"], sv["proj"], _row(W["a_out_norm"][l]), big["a_w_out"], ha, "gdn_out_bwd")
            G["a_out_norm"][l] = dwn[0]
            gbig["a_w_out"] = matmul_tn(sv["y"], dh, "a_out_wgrad")
            (dq, dk, dv, dbb, dgb), got = gdn_bwd(sv["q"], sv["k"], sv["v"], sv["bb"], sv["gb"], sv["states"], sv["tinv"], do, ha,
                                                  _named("gdn_bwd", l, rider), rider)
            comm.bwd_got_b(l, got)
            du, dba, dal, ddt = gdn_prep_bwd(sv["proj"], sv["u"], sv["alog"], sv["dtb"],
                                             dq, dk, dv, dbb, dgb, ha, "gdn_prep_bwd")
            G["a_A_log"][l] = dal[0, ha:2 * ha]
            G["a_dt_bias"][l] = ddt[0, ha:2 * ha]
            rider = comm.bwd_rider_c(l)
            (dqkv, dconv), got = conv_bwd(du, sv["proj"], W["a_conv"][l], A_CONV, _named("gdn_conv_bwd", l, rider), rider)
            if rider is not None:
                comm.bwd_got_c(l, got)
            G["a_conv"][l] = dconv
            gam = _row(W["a_norm"][l])
            pieces = [(dqkv, w_in[:, :cw]), (dz, w_in[:, cw:cw + qk]), (dba, w_in[:, cw + qk:])]
            gbig["a_w_in"] = jnp.concatenate(
                [norm_matmul_tn(sv["h_in"], gam, dqkv, "a_in_wgrad_qkv"),
                 norm_matmul_tn(sv["h_in"], gam, dz, "a_in_wgrad_z"),
                 norm_matmul_tn(sv["h_in"], gam, dba, "a_in_wgrad_ba")[:, :2 * ha]], axis=1)
            dh, dg = dx_norm_bwd(dh, sv["h_in"], gam, pieces, "a_in_dx")
            G["a_norm"][l] = dg[0]
        else:
            j = l - la
            d_o = matmul_nt(dh, big["b_w_out"], "b_out_dx")
            gbig["b_w_out"] = matmul_tn(sv["o"], dh, "b_out_wgrad")
            (dq, dk_acc, dv_acc, dbias), got = attn_bwd(
                sv["qp"], kv, sv["bias"], d_o, dk_acc, dv_acc,
                _named("attn_bwd" if dk_acc is None else "attn_bwd_acc", l, rider), rider)
            comm.bwd_got_b(l, got)
            G["b_rel_bias"][j] = bias_expand_bwd(_bias_from_pairs(dbias), clip, tp, "bias_expand_bwd")[:, :tbl]
            gam = _row(W["b_norm"][j])
            gbig["b_w_q"] = norm_matmul_tn(sv["h_in"], gam, dq, "b_q_wgrad")
            dh, dg = dx_norm_bwd(dh, sv["h_in"], gam, [(dq, big["b_w_q"])], "b_q_dx")
            G["b_norm"][j] = dg[0]
            if j == 0:
                gam = _row(W["kv_norm"])
                gbig["w_kv"] = jnp.concatenate([norm_matmul_tn(h_kv, gam, dk_acc, "kv_wgrad_k", transposed=True),
                                                norm_matmul_tn(h_kv, gam, dv_acc, "kv_wgrad_v", transposed=True)], axis=0)
                dh, dg = dx_norm_bwd(dh, h_kv, gam, [(dk_acc, w_kv[:bw]), (dv_acc, w_kv[bw:])], "kv_dx", w_t=True)
                G["kv_norm"] = dg[0]
        comm.grads_ready(l, gbig)
    out = {k_: (jnp.stack(v_) if isinstance(v_, list) else v_) for k_, v_ in G.items()}
    return loss[0, 0], dh, out, comm


WEIGHTS = ["a_norm", "a_w_in", "a_conv", "a_A_log", "a_dt_bias", "a_out_norm", "a_w_out", "kv_norm", "w_kv",
           "b_norm", "b_w_q", "b_rel_bias", "b_w_out", "f_norm", "f_w_up", "f_conv", "f_conv_b", "f_w_down",
           "final_norm"]
SHARD_AXIS = {"a_norm": 1, "a_w_in": 2, "a_conv": 2, "a_w_out": 1, "w_kv": 1, "b_w_q": 1, "b_w_out": 1,
              "f_w_up": 2, "f_conv": 2, "f_w_down": 1}
BIG = ["a_w_in", "a_w_out", "w_kv", "b_w_q", "b_w_out", "f_w_up", "f_w_down"]
SMALL_SHARDED = ["a_norm", "a_conv", "f_conv"]
TRANSPOSED = ("f_w_up", "w_kv")


def _t_view(k, a):
    return jnp.swapaxes(a, -1, -2) if k in TRANSPOSED else a


def _unstack(g, axis):
    if axis == 0:
        return g.reshape(-1, *g.shape[2:])
    return jnp.concatenate([g[i] for i in range(N_DEV)], axis=axis)


def _to_blocks(full, axis):
    if axis == 0:
        return full.reshape(N_DEV, -1, full.shape[-1])
    return jnp.stack(jnp.split(full, N_DEV, axis=axis))


def _pack(arrs):
    flat = []
    for a in arrs:
        f = a.reshape(-1)
        flat.append(jnp.pad(f, (0, (-f.shape[0]) % LANES)))
    f = jnp.concatenate(flat)
    f = jnp.pad(f, (0, (-f.shape[0]) % (8 * LANES)))
    return f.reshape(-1, LANES)


def _unpack(pack, shapes):
    flat = pack.reshape(-1)
    out, pos = [], 0
    for s in shapes:
        sz = math.prod(s)
        out.append(flat[pos:pos + sz].reshape(s))
        pos += sz + (-sz) % LANES
    return out


def _as2d(a):
    return a.reshape(1, -1) if a.ndim == 1 else a.reshape(-1, a.shape[-1])


def kernel(x, a_norm, a_w_in, a_conv, a_A_log, a_dt_bias, a_out_norm, a_w_out, kv_norm, w_kv, b_norm, b_w_q, b_rel_bias, b_w_out, f_norm, f_w_up, f_conv, f_conv_b, f_w_down, final_norm, loss_target, m_a_norm, m_a_w_in, m_a_conv, m_a_A_log, m_a_dt_bias, m_a_out_norm, m_a_w_out, m_kv_norm, m_w_kv, m_b_norm, m_b_w_q, m_b_rel_bias, m_b_w_out, m_f_norm, m_f_w_up, m_f_conv, m_f_conv_b, m_f_w_down, m_final_norm, v_a_norm, v_a_w_in, v_a_conv, v_a_A_log, v_a_dt_bias, v_a_out_norm, v_a_w_out, v_kv_norm, v_w_kv, v_b_norm, v_b_w_q, v_b_rel_bias, v_b_w_out, v_f_norm, v_f_w_up, v_f_conv, v_f_conv_b, v_f_w_down, v_final_norm):
    w = dict(a_norm=a_norm, a_w_in=a_w_in, a_conv=a_conv, a_A_log=a_A_log, a_dt_bias=a_dt_bias,
             a_out_norm=a_out_norm, a_w_out=a_w_out, kv_norm=kv_norm, w_kv=w_kv, b_norm=b_norm, b_w_q=b_w_q,
             b_rel_bias=b_rel_bias, b_w_out=b_w_out, f_norm=f_norm, f_w_up=f_w_up, f_conv=f_conv,
             f_conv_b=f_conv_b, f_w_down=f_w_down, final_norm=final_norm)
    mom = dict(a_norm=m_a_norm, a_w_in=m_a_w_in, a_conv=m_a_conv, a_A_log=m_a_A_log, a_dt_bias=m_a_dt_bias,
               a_out_norm=m_a_out_norm, a_w_out=m_a_w_out, kv_norm=m_kv_norm, w_kv=m_w_kv, b_norm=m_b_norm,
               b_w_q=m_b_w_q, b_rel_bias=m_b_rel_bias, b_w_out=m_b_w_out, f_norm=m_f_norm, f_w_up=m_f_w_up,
               f_conv=m_f_conv, f_conv_b=m_f_conv_b, f_w_down=m_f_w_down, final_norm=m_final_norm)
    var = dict(a_norm=v_a_norm, a_w_in=v_a_w_in, a_conv=v_a_conv, a_A_log=v_a_A_log, a_dt_bias=v_a_dt_bias,
               a_out_norm=v_a_out_norm, a_w_out=v_a_w_out, kv_norm=v_kv_norm, w_kv=v_w_kv, b_norm=v_b_norm,
               b_w_q=v_b_w_q, b_rel_bias=v_b_rel_bias, b_w_out=v_b_w_out, f_norm=v_f_norm, f_w_up=v_f_w_up,
               f_conv=v_f_conv, f_conv_b=v_f_conv_b, f_w_down=v_f_w_down, final_norm=v_final_norm)
    me = 4 * lax.axis_index("x") + 2 * lax.axis_index("y") + lax.axis_index("c")

    la, depth = a_A_log.shape[0], f_norm.shape[0]
    c_idx = lax.axis_index("c").astype(jnp.int32).reshape(1)
    shard_bf16 = {k: _t_view(k, w[k]).astype(BF16) for k in BIG}
    blk_axis = {k: 0 if k in TRANSPOSED else SHARD_AXIS[k] - (k != "w_kv") for k in BIG}

    class Sharded(LocalWeights):
        def __init__(self):
            super().__init__(w)
            self.full = {}
            self.stacks = {}
            self.pending = None
            self.parts = None

        def names(self, l):
            out = ["a_w_in", "a_w_out"] if l < la else ["b_w_q", "b_w_out"]
            return out + ["f_w_up", "f_w_down"] + (["w_kv"] if l == la else [])

        def index(self, k, l):
            return None if k == "w_kv" else (l - la if k.startswith("b_") else l)

        def shards(self, l, names=None):
            return [shard_bf16[k] if k == "w_kv" else shard_bf16[k][self.index(k, l)]
                    for k in (self.names(l) if names is None else names)]

        def install(self, l, gathered, names=None):
            out = self.full.setdefault(l, {})
            for k, g in zip(self.names(l) if names is None else names, gathered):
                out[k] = _unstack(g, blk_axis[k])
                if k == "a_w_in":
                    out[k] = jnp.pad(out[k], ((0, 0), (0, (-out[k].shape[1]) % LANES)))

        def big(self, l, la_):
            return self.full[l]

        def first_names(self):
            return ["a_w_in"] if la > 0 else self.names(0)

        def prep_rider(self, l):
            rest = [k for k in self.names(0) if k not in self.first_names()]
            return ("all_gather", self.shards(0, rest)) if l == 0 and rest else None

        def prep_got(self, l, got):
            if got is not None:
                self.install(0, got, [k for k in self.names(0) if k not in self.first_names()])

        def fwd_rider(self, l):
            return ("all_gather", self.shards(l + 1)) if l + 1 < depth else None

        def fwd_got(self, l, got):
            if got is not None:
                self.install(l + 1, got)

        def blocks(self, grads, keys):
            return [_to_blocks(grads[k], blk_axis[k]) for k in keys]

        def grads_ready(self, l, grads):
            keys = [k for k in self.names(l) if (k, l) not in self.early_keys]
            self.pending = (l, keys, self.blocks(grads, keys))

        early = early_parts = None
        early_keys = ()

        def ffn_grads_ready(self, l, grads):
            if l == 0 and la > 0:
                keys = ["f_w_up", "f_w_down"]
                self.early = (keys, self.blocks(grads, keys))
                self.early_keys = tuple((k, 0) for k in keys)

        def bwd_rider_a(self, l):
            return None if self.pending is None else ("rs_sibling", self.pending[2])

        def add_pairs(self, g8, from_sibling):
            return [pair_add(g, r, c_idx, "grads_pair_add") for g, r in zip(g8, from_sibling)]

        def bwd_got_a(self, l, got):
            if got is not None:
                self.parts = self.add_pairs(self.pending[2], got)

        def bwd_rider_b(self, l):
            riders = [] if self.parts is None else [("rs_chips", self.parts)]
            if self.early is not None:
                riders.append(("rs_sibling", self.early[1]))
            return riders

        def keep(self, stacks):
            l, keys, _ = self.pending
            for k, s in zip(keys, stacks):
                self.stacks[(k, l)] = s
            self.pending = self.parts = None

        def bwd_got_b(self, l, got):
            got = list(got or [])
            if self.parts is not None:
                self.keep(got.pop(0))
            if self.early is not None and got:
                self.early_parts = self.add_pairs(self.early[1], got.pop(0))

        def bwd_rider_c(self, l):
            return None if self.early_parts is None else ("rs_chips", self.early_parts)

        def bwd_got_c(self, l, got):
            for k, s in zip(self.early[0], got):
                self.stacks[(k, 0)] = s
            self.early = self.early_parts = None

        def finish(self):
            self.parts = self.add_pairs(self.pending[2], exchange("rs_sibling", self.pending[2], "grads_to_sibling"))
            self.keep(exchange("rs_chips", self.parts, "grads_to_chips"))

    comm = Sharded()

    small_shapes = [w[k].shape for k in SMALL_SHARDED]
    gathered = exchange("all_gather", comm.shards(0, comm.first_names()) + [_pack([w[k] for k in SMALL_SHARDED])],
                        "weights_all_gather")
    comm.install(0, gathered[:-1], comm.first_names())
    full = dict(w)
    small = [_unpack(gathered[-1][i], small_shapes) for i in range(N_DEV)]
    for idx, k in enumerate(SMALL_SHARDED):
        full[k] = jnp.concatenate([small[i][idx] for i in range(N_DEV)], axis=SHARD_AXIS[k])

    loss_part, grad_x, G, _ = local_step(x[0], loss_target[0], full, comm)
    comm.finish()
    stacks = []
    for k in BIG:
        layers = sorted(l for (k_, l) in comm.stacks if k_ == k)
        stacks.append(jnp.concatenate([comm.stacks[(k, l)] for l in layers], axis=1))

    small_names = [k for k in WEIGHTS if k not in BIG]
    reduced = _unpack(all_reduce_small(_pack([G[k] for k in small_names] + [loss_part.reshape(1)]), "small_all_reduce"),
                      [G[k].shape for k in small_names] + [(1,)])
    loss = reduced[-1][0]
    small_g = dict(zip(small_names, reduced[:-1]))
    for k in SMALL_SHARDED:
        sz = w[k].shape[SHARD_AXIS[k]]
        small_g[k] = lax.dynamic_slice_in_dim(small_g[k], me * sz, sz, axis=SHARD_AXIS[k])

    res = {}
    for k, st in zip(BIG, stacks):
        tshape = _t_view(k, w[k]).shape
        wt, mt, vt = (_as2d(_t_view(k, a)) for a in (w[k], mom[k], var[k]))
        outs = adamw(wt, st, mt, vt, "adamw_" + k)
        res[k] = [_t_view(k, o.reshape(tshape)) for o in outs]
    for k in small_names:
        outs = adamw(_as2d(w[k]), _as2d(small_g[k])[None], _as2d(mom[k]), _as2d(var[k]), "adamw_" + k)
        res[k] = [o.reshape(w[k].shape) for o in outs]

    return (loss, grad_x[None], *[res[k][0] for k in WEIGHTS], *[res[k][1] for k in WEIGHTS],
            *[res[k][2] for k in WEIGHTS], *[res[k][3] for k in WEIGHTS])
```

```python
import functools
import math

import jax
import jax.numpy as jnp
from jax import lax
from jax.experimental import pallas as pl
from jax.experimental.pallas import tpu as pltpu

F32 = jnp.float32
BF16 = jnp.bfloat16
HI = lax.Precision.HIGHEST
MESH = pl.DeviceIdType.MESH

EPS = 1e-6
NEG_INF = -1e30
CHUNK = 64
LEFT_CHUNKS = 8
BAND = (LEFT_CHUNKS + 1) * CHUNK
BAND_PAD = 640
A_CONV = 4
F_CONV = 3
A_HEAD = 128
B_HEAD = 64
LANES = 128
HALO = 8
N_DEV = 8

ADAM_LR = 0.001
ADAM_B1 = 0.9
ADAM_B2 = 0.999
ADAM_EPS = 1e-08
ADAM_WD = 0.01
ADAM_STEP = 10

VMEM_LIMIT_V7X = 56 * 1024 * 1024
WGRAD_DTYPE = BF16
GDN_BWD_HEADS = 8
COL_CHUNK = 256
FFN_TILE = 256


def _cp(sem=None, vmem=VMEM_LIMIT_V7X):
    kw = dict(vmem_limit_bytes=vmem)
    if sem is not None:
        kw["dimension_semantics"] = sem
    return pltpu.CompilerParams(**kw)


def _pick(n, target, q=LANES):
    best = None
    for t in range(q, min(n, target) + 1, q):
        if n % t == 0:
            best = t
    return best if best is not None else n


def _sig(x):
    return 1.0 / (1.0 + jnp.exp(-x))


def _softplus(x):
    return jnp.maximum(x, 0.0) + jnp.log(1.0 + jnp.exp(-jnp.abs(x)))


def _rms(x, g):
    return x * lax.rsqrt(jnp.mean(x * x, axis=-1, keepdims=True) + EPS) * g


def _rms_bwd(x, g, dxn):
    r = lax.rsqrt(jnp.mean(x * x, axis=-1, keepdims=True) + EPS)
    gd = dxn * g
    dx = r * gd - x * (r * r * r) * jnp.mean(x * gd, axis=-1, keepdims=True)
    dg = jnp.sum(dxn * x * r, axis=0, keepdims=True)
    return dx, dg


def _dot(a, b):
    return jnp.dot(a, b, preferred_element_type=F32)


def _dot_nt(a, b):
    return lax.dot_general(a, b, (((1,), (1,)), ((), ())), preferred_element_type=F32)


def _dot_tn(a, b):
    return lax.dot_general(a, b, (((0,), (0,)), ((), ())), preferred_element_type=F32)


def _hdot(a, b):
    return jnp.dot(a, b, precision=HI, preferred_element_type=F32)


def _hdot_nt(a, b):
    return lax.dot_general(a, b, (((1,), (1,)), ((), ())), precision=HI, preferred_element_type=F32)


def _resident(shape, index_map):
    return pl.BlockSpec(shape, index_map, pipeline_mode=pl.Buffered(1))


def norm_matmul(h, gamma, w, name, out_dtype=F32, w_t=False):
    n, d = h.shape
    nc = w.shape[0] if w_t else w.shape[1]
    tm = _pick(n, 2048 if out_dtype == BF16 else 1024, 8)
    tn = _pick(nc, 1536)

    def body(h_ref, g_ref, w_ref, o_ref):
        xn = _rms(h_ref[...], g_ref[...]).astype(BF16)
        o_ref[...] = (_dot_nt(xn, w_ref[...]) if w_t else _dot(xn, w_ref[...])).astype(out_dtype)

    return pl.pallas_call(
        body, grid=(nc // tn, n // tm),
        in_specs=[pl.BlockSpec((tm, d), lambda j, i: (i, 0)),
                  pl.BlockSpec((1, d), lambda j, i: (0, 0)),
                  pl.BlockSpec((tn, d), lambda j, i: (j, 0)) if w_t else pl.BlockSpec((d, tn), lambda j, i: (0, j))],
        out_specs=pl.BlockSpec((tm, tn), lambda j, i: (i, j)),
        out_shape=jax.ShapeDtypeStruct((n, nc), out_dtype), name=name,
        compiler_params=_cp(("parallel", "parallel")))(h, gamma, w)


def norm_matmul_tn(h, gamma, dy, name, transposed=False):
    n, d = h.shape
    nc = dy.shape[1]
    tm = _pick(n, 2048 if dy.dtype == BF16 else 1024, 8)
    tn = _pick(nc, 1536)

    steps = n // tm

    def body(h_ref, g_ref, dy_ref, o_ref, acc):
        i = pl.program_id(1)

        @pl.when(i == 0)
        def _():
            acc[...] = jnp.zeros_like(acc)
        xn = _rms(h_ref[...], g_ref[...]).astype(BF16)
        dyb = dy_ref[...].astype(BF16)
        acc[...] += _dot_tn(dyb, xn) if transposed else _dot_tn(xn, dyb)

        @pl.when(i == steps - 1)
        def _():
            o_ref[...] = acc[...].astype(WGRAD_DTYPE)

    return pl.pallas_call(
        body, grid=(nc // tn, steps),
        in_specs=[pl.BlockSpec((tm, d), lambda j, i: (i, 0)),
                  pl.BlockSpec((1, d), lambda j, i: (0, 0)),
                  pl.BlockSpec((tm, tn), lambda j, i: (i, j))],
        out_specs=pl.BlockSpec((tn, d), lambda j, i: (j, 0)) if transposed else pl.BlockSpec((d, tn), lambda j, i: (0, j)),
        out_shape=jax.ShapeDtypeStruct((nc, d) if transposed else (d, nc), WGRAD_DTYPE),
        scratch_shapes=[pltpu.VMEM((tn, d) if transposed else (d, tn), F32)], name=name,
        compiler_params=_cp(("parallel", "arbitrary")))(h, gamma, dy)


def matmul_tn(a, dy, name):
    n, ka = a.shape
    nc = dy.shape[1]
    tm = _pick(n, 2048 if a.dtype == BF16 else 1024, 8)
    tk = _pick(ka, 1536)
    tn = _pick(nc, 1024)
    steps = n // tm

    def body(a_ref, dy_ref, o_ref, acc):
        i = pl.program_id(2)

        @pl.when(i == 0)
        def _():
            acc[...] = jnp.zeros_like(acc)
        acc[...] += _dot_tn(a_ref[...].astype(BF16), dy_ref[...].astype(BF16))

        @pl.when(i == steps - 1)
        def _():
            o_ref[...] = acc[...].astype(WGRAD_DTYPE)

    return pl.pallas_call(
        body, grid=(ka // tk, nc // tn, steps),
        in_specs=[pl.BlockSpec((tm, tk), lambda k, j, i: (i, k)),
                  pl.BlockSpec((tm, tn), lambda k, j, i: (i, j))],
        out_specs=pl.BlockSpec((tk, tn), lambda k, j, i: (k, j)),
        out_shape=jax.ShapeDtypeStruct((ka, nc), WGRAD_DTYPE),
        scratch_shapes=[pltpu.VMEM((tk, tn), F32)], name=name,
        compiler_params=_cp(("parallel", "parallel", "arbitrary")))(a, dy)


def matmul_res(a, w, h, name):
    n, k = a.shape
    d = w.shape[1]
    tm = _pick(n, 512, 8)

    def body(a_ref, w_ref, h_ref, o_ref):
        o_ref[...] = h_ref[...] + _dot(a_ref[...].astype(BF16), w_ref[...])

    return pl.pallas_call(
        body, grid=(n // tm,),
        in_specs=[pl.BlockSpec((tm, k), lambda i: (i, 0)),
                  _resident((k, d), lambda i: (0, 0)),
                  pl.BlockSpec((tm, d), lambda i: (i, 0))],
        out_specs=pl.BlockSpec((tm, d), lambda i: (i, 0)),
        out_shape=jax.ShapeDtypeStruct((n, d), F32), name=name,
        compiler_params=_cp(("parallel",)))(a, w, h)


def matmul_nt(dy, w, name):
    n, k = dy.shape
    d = w.shape[0]
    tm = _pick(n, 512, 8)

    def body(dy_ref, w_ref, o_ref):
        o_ref[...] = _dot_nt(dy_ref[...].astype(BF16), w_ref[...])

    return pl.pallas_call(
        body, grid=(n // tm,),
        in_specs=[pl.BlockSpec((tm, k), lambda i: (i, 0)),
                  _resident((d, k), lambda i: (0, 0))],
        out_specs=pl.BlockSpec((tm, d), lambda i: (i, 0)),
        out_shape=jax.ShapeDtypeStruct((n, d), F32), name=name,
        compiler_params=_cp(("parallel",)))(dy, w)


def dx_norm_bwd(dout, h, gamma, pieces, name, rider=None, w_t=False):
    n, d = h.shape
    tm = _pick(n, 512, 8)
    np_ = len(pieces)
    mm = _dot if w_t else _dot_nt

    def body(*refs):
        dout_ref, h_ref, g_ref = refs[:3]
        dys = refs[3:3 + np_]
        ws = refs[3 + np_:3 + 2 * np_]
        dh_ref, dg_ref = refs[3 + 2 * np_:]
        dxn = mm(dys[0][...].astype(BF16), ws[0][...])
        for p in range(1, np_):
            dxn = dxn + mm(dys[p][...].astype(BF16), ws[p][...])
        dx, dg = _rms_bwd(h_ref[...], g_ref[...], dxn)
        dh_ref[...] = dout_ref[...] + dx

        @pl.when(pl.program_id(0) == 0)
        def _():
            dg_ref[...] = jnp.zeros_like(dg_ref)
        dg_ref[...] += dg

    in_specs = [pl.BlockSpec((tm, d), lambda i: (i, 0)),
                pl.BlockSpec((tm, d), lambda i: (i, 0)),
                pl.BlockSpec((1, d), lambda i: (0, 0))]
    in_specs += [pl.BlockSpec((tm, dy.shape[1]), lambda i: (i, 0)) for dy, _ in pieces]
    in_specs += [_resident(w.shape, lambda i: (0, 0)) for _, w in pieces]
    (dh, dg), got = host_call(
        body, grid=(n // tm,), in_specs=in_specs,
        out_specs=[pl.BlockSpec((tm, d), lambda i: (i, 0)), pl.BlockSpec((1, d), lambda i: (0, 0))],
        out_shape=[jax.ShapeDtypeStruct((n, d), F32), jax.ShapeDtypeStruct((1, d), F32)], name=name,
        scratch_shapes=[], sem=("arbitrary",), rider=rider,
        args=(dout, h, gamma, *[p[0] for p in pieces], *[p[1] for p in pieces]))
    return (dh, dg) if rider is None else (dh, dg, got)


def loss_head(h, gamma, target, name="loss_head"):
    n, d = h.shape
    tm = _pick(n, 512, 8)

    def body(h_ref, g_ref, t_ref, loss_ref, dh_ref, dg_ref):
        @pl.when(pl.program_id(0) == 0)
        def _():
            loss_ref[...] = jnp.zeros_like(loss_ref)
            dg_ref[...] = jnp.zeros_like(dg_ref)
        x = h_ref[...]
        g = g_ref[...]
        e = _rms(x, g) - t_ref[...]
        part = jnp.sum(jnp.sum(e * e, axis=-1, keepdims=True), axis=0, keepdims=True) * (0.5 / d)
        loss_ref[...] += jnp.broadcast_to(part, loss_ref.shape)
        dx, dg = _rms_bwd(x, g, e * (1.0 / d))
        dh_ref[...] = dx
        dg_ref[...] += dg

    return pl.pallas_call(
        body, grid=(n // tm,),
        in_specs=[pl.BlockSpec((tm, d), lambda i: (i, 0)), pl.BlockSpec((1, d), lambda i: (0, 0)),
                  pl.BlockSpec((tm, d), lambda i: (i, 0))],
        out_specs=[pl.BlockSpec((8, LANES), lambda i: (0, 0)), pl.BlockSpec((tm, d), lambda i: (i, 0)),
                   pl.BlockSpec((1, d), lambda i: (0, 0))],
        out_shape=[jax.ShapeDtypeStruct((8, LANES), F32), jax.ShapeDtypeStruct((n, d), F32),
                   jax.ShapeDtypeStruct((1, d), F32)], name=name,
        compiler_params=_cp(("arbitrary",)))(h, gamma, target)


def _halo_rows(dtype):
    return HALO * (4 // jnp.dtype(dtype).itemsize)


def _prev_halo_map(t, hb=HALO):
    return lambda i: (jnp.maximum(i * (t // hb) - 1, 0), 0)


def _next_halo_map(t, n, hb=HALO):
    return lambda i: (jnp.minimum((i + 1) * (t // hb), n // hb - 1), 0)


def _fill_prev(xs, main_ref, halo_ref, i, cols=slice(None)):
    hb = halo_ref.shape[0]
    xs[0:HALO, :] = jnp.where(i > 0, halo_ref[hb - HALO:hb, cols].astype(F32), 0.0)
    xs[HALO:, :] = main_ref[:, cols].astype(F32)


def _causal_conv(xs, w_ref, width, t, cols=slice(None), xcols=slice(None)):
    x = xs[:, xcols]
    acc = w_ref[width - 1:width, cols] * x[HALO:, :]
    for k in range(width - 1):
        acc = acc + w_ref[k:k + 1, cols] * pltpu.roll(x, width - 1 - k, axis=0)[HALO:, :]
    return acc


def _col_chunks(width, target=COL_CHUNK):
    tc = _pick(width, target)
    return [slice(j * tc, (j + 1) * tc) for j in range(width // tc)]


def ffn_act_down(up, conv_w, conv_b, w_down, h, name):
    n, c2 = up.shape
    ff = c2 // 2
    d = h.shape[1]
    t = _pick(n, 2 * FFN_TILE, 8)
    hb = _halo_rows(up.dtype)
    chunks = _col_chunks(ff)
    tc = chunks[0].stop

    def body(up_ref, halo_ref, cw_ref, cb_ref, wd_ref, h_ref, o_ref, act_ref, hc_ref, xg, xv):
        i = pl.program_id(0)
        acc = h_ref[...]
        for cs in chunks:
            vs = slice(ff + cs.start, ff + cs.stop)
            _fill_prev(xg, up_ref, halo_ref, i, cs)
            _fill_prev(xv, up_ref, halo_ref, i, vs)
            gate = _causal_conv(xg, cw_ref, F_CONV, t, cs) + cb_ref[:, cs]
            val = _causal_conv(xv, cw_ref, F_CONV, t, vs) + cb_ref[:, vs]
            hc_ref[:, cs] = gate.astype(BF16)
            hc_ref[:, vs] = val.astype(BF16)
            act = (gate * _sig(gate) * val).astype(BF16)
            act_ref[:, cs] = act
            acc = acc + _dot(act, wd_ref[cs, :])
        o_ref[...] = acc

    return pl.pallas_call(
        body, grid=(n // t,),
        in_specs=[pl.BlockSpec((t, c2), lambda i: (i, 0)),
                  pl.BlockSpec((hb, c2), _prev_halo_map(t, hb)),
                  pl.BlockSpec((F_CONV, c2), lambda i: (0, 0)),
                  pl.BlockSpec((1, c2), lambda i: (0, 0)),
                  _resident((ff, d), lambda i: (0, 0)),
                  pl.BlockSpec((t, d), lambda i: (i, 0))],
        out_specs=[pl.BlockSpec((t, d), lambda i: (i, 0)), pl.BlockSpec((t, ff), lambda i: (i, 0)),
                   pl.BlockSpec((t, c2), lambda i: (i, 0))],
        out_shape=[jax.ShapeDtypeStruct((n, d), F32), jax.ShapeDtypeStruct((n, ff), BF16),
                   jax.ShapeDtypeStruct((n, c2), BF16)],
        scratch_shapes=[pltpu.VMEM((t + HALO, tc), F32), pltpu.VMEM((t + HALO, tc), F32)], name=name,
        compiler_params=_cp(("parallel",)))(up, up, conv_w, conv_b, w_down, h)


def ffn_bwd_act(dout, hc, w_down, name):
    n, c2 = hc.shape
    ff = c2 // 2
    d = dout.shape[1]
    t = _pick(n, 2 * FFN_TILE, 8)
    chunks = _col_chunks(ff)

    def body(dout_ref, hc_ref, wd_ref, dhc_ref, dcb_ref):
        i = pl.program_id(0)

        @pl.when(i == 0)
        def _():
            dcb_ref[...] = jnp.zeros_like(dcb_ref)
        doutb = dout_ref[...].astype(BF16)
        for cs in chunks:
            vs = slice(ff + cs.start, ff + cs.stop)
            gate = hc_ref[:, cs].astype(F32)
            val = hc_ref[:, vs].astype(F32)
            sg = _sig(gate)
            da = _dot_nt(doutb, wd_ref[cs, :])
            dgate = da * val * (sg * (1.0 + gate * (1.0 - sg)))
            dval = da * gate * sg
            dhc_ref[:, cs] = dgate.astype(BF16)
            dhc_ref[:, vs] = dval.astype(BF16)
            dcb_ref[:, cs] += jnp.sum(dgate, axis=0, keepdims=True)
            dcb_ref[:, vs] += jnp.sum(dval, axis=0, keepdims=True)

    return pl.pallas_call(
        body, grid=(n // t,),
        in_specs=[pl.BlockSpec((t, d), lambda i: (i, 0)),
                  pl.BlockSpec((t, c2), lambda i: (i, 0)),
                  _resident((ff, d), lambda i: (0, 0))],
        out_specs=[pl.BlockSpec((t, c2), lambda i: (i, 0)), pl.BlockSpec((1, c2), lambda i: (0, 0))],
        out_shape=[jax.ShapeDtypeStruct((n, c2), BF16), jax.ShapeDtypeStruct((1, c2), F32)], name=name,
        compiler_params=_cp(("arbitrary",)))(dout, hc, w_down)


def conv_bwd_tail(dy_ref, dnext_ref, x_ref, cw_ref, dcw_ref, ds, width, t, i, last, cols=slice(None)):
    ds[0:t, :] = dy_ref[:, cols].astype(F32)
    ds[t:, :] = jnp.where(i < last, dnext_ref[0:HALO, cols].astype(F32), 0.0)
    x = x_ref[:, cols].astype(F32)
    dall = ds[...]
    dx = None
    for k in range(width):
        off = width - 1 - k
        shifted = dall[0:t, :] if off == 0 else pltpu.roll(dall, t + HALO - off, axis=0)[0:t, :]
        term = cw_ref[k:k + 1, cols] * shifted
        dx = term if dx is None else dx + term
        dcw_ref[k:k + 1, cols] += jnp.sum(shifted * x, axis=0, keepdims=True)
    return dx


def ffn_bwd_up(dhc, up, conv_w, w_up, h, dout, gamma, name, rider=None):
    n, c2 = up.shape
    d = h.shape[1]
    t = _pick(n, FFN_TILE, 8)
    last = n // t - 1
    chunks = _col_chunks(c2)
    tc = chunks[0].stop

    def body(dhc_ref, dnext_ref, up_ref, cw_ref, wu_ref, h_ref, dout_ref, g_ref,
             dh_ref, dup_ref, dcw_ref, dg_ref, ds):
        i = pl.program_id(0)

        @pl.when(i == 0)
        def _():
            dcw_ref[...] = jnp.zeros_like(dcw_ref)
            dg_ref[...] = jnp.zeros_like(dg_ref)
        dxn = jnp.zeros((t, d), F32)
        for cs in chunks:
            dup = conv_bwd_tail(dhc_ref, dnext_ref, up_ref, cw_ref, dcw_ref, ds, F_CONV, t, i, last, cs)
            dupb = dup.astype(BF16)
            dup_ref[:, cs] = dupb
            dxn = dxn + _dot(dupb, wu_ref[cs, :])
        dx, dg = _rms_bwd(h_ref[...], g_ref[...], dxn)
        dh_ref[...] = dout_ref[...] + dx
        dg_ref[...] += dg

    return host_call(
        body, grid=(n // t,), rider=rider, sem=("arbitrary",), args=(dhc, dhc, up, conv_w, w_up, h, dout, gamma),
        in_specs=[pl.BlockSpec((t, c2), lambda i: (i, 0)),
                  pl.BlockSpec((_halo_rows(dhc.dtype), c2), _next_halo_map(t, n, _halo_rows(dhc.dtype))),
                  pl.BlockSpec((t, c2), lambda i: (i, 0)),
                  pl.BlockSpec((F_CONV, c2), lambda i: (0, 0)),
                  _resident((c2, d), lambda i: (0, 0)),
                  pl.BlockSpec((t, d), lambda i: (i, 0)),
                  pl.BlockSpec((t, d), lambda i: (i, 0)),
                  pl.BlockSpec((1, d), lambda i: (0, 0))],
        out_specs=[pl.BlockSpec((t, d), lambda i: (i, 0)), pl.BlockSpec((t, c2), lambda i: (i, 0)),
                   pl.BlockSpec((F_CONV, c2), lambda i: (0, 0)), pl.BlockSpec((1, d), lambda i: (0, 0))],
        out_shape=[jax.ShapeDtypeStruct((n, d), F32), jax.ShapeDtypeStruct((n, c2), BF16),
                   jax.ShapeDtypeStruct((F_CONV, c2), F32), jax.ShapeDtypeStruct((1, d), F32)],
        scratch_shapes=[pltpu.VMEM((t + HALO, tc), F32)], name=name)


def _gdn_head(uq, uk, uv, pba, alog, dtb, head, n_heads):
    lane = lax.broadcasted_iota(jnp.int32, pba.shape, 1)
    sq = uq * _sig(uq)
    q = sq * lax.rsqrt(jnp.sum(sq * sq, axis=-1, keepdims=True) + EPS) * (A_HEAD ** -0.5)
    sk = uk * _sig(uk)
    k = sk * lax.rsqrt(jnp.sum(sk * sk, axis=-1, keepdims=True) + EPS)
    v = uv * _sig(uv)
    beta = jnp.sum(jnp.where(lane == head, _sig(pba), 0.0), axis=-1, keepdims=True)
    g_all = -jnp.exp(alog) * _softplus(pba + dtb)
    g = jnp.sum(jnp.where(lane == n_heads + head, g_all, 0.0), axis=-1, keepdims=True)
    return q, k, v, jnp.broadcast_to(beta, uq.shape), jnp.broadcast_to(g, uq.shape)


def gdn_prep(proj, conv_w, alog, dtb, n_heads, name, rider=None):
    n = proj.shape[0]
    qk = n_heads * A_HEAD
    cw = 3 * qk
    ba_blk = (cw + qk) // LANES
    t = _pick(n, 256, 8)

    def body(x_ref, halo_ref, pba_ref, cw_ref, al_ref, dt_ref, q_ref, k_ref, v_ref, b_ref, g_ref, u_ref, xs):
        i = pl.program_id(0)
        xs[0:HALO, :] = jnp.where(i > 0, halo_ref[...], 0.0)
        xs[HALO:, :] = x_ref[...]
        pba = pba_ref[...]
        for hd in range(n_heads):
            s0 = slice(hd * A_HEAD, (hd + 1) * A_HEAD)
            s1 = slice(qk + hd * A_HEAD, qk + (hd + 1) * A_HEAD)
            s2 = slice(2 * qk + hd * A_HEAD, 2 * qk + (hd + 1) * A_HEAD)
            uq, uk, uv = [_causal_conv(xs, cw_ref, A_CONV, t, s, s) for s in (s0, s1, s2)]
            u_ref[:, s0] = uq.astype(BF16)
            u_ref[:, s1] = uk.astype(BF16)
            u_ref[:, s2] = uv.astype(BF16)
            q, k, v, bb, gb = _gdn_head(uq, uk, uv, pba, al_ref[...], dt_ref[...], hd, n_heads)
            q_ref[:, s0] = q
            k_ref[:, s0] = k
            v_ref[:, s0] = v
            b_ref[:, s0] = bb
            g_ref[:, s0] = gb

    out = jax.ShapeDtypeStruct((n, qk), F32)
    return host_call(
        body, grid=(n // t,),
        in_specs=[pl.BlockSpec((t, cw), lambda i: (i, 0)),
                  pl.BlockSpec((HALO, cw), _prev_halo_map(t)),
                  pl.BlockSpec((t, LANES), lambda i: (i, ba_blk)),
                  pl.BlockSpec((A_CONV, cw), lambda i: (0, 0)),
                  pl.BlockSpec((1, LANES), lambda i: (0, 0)),
                  pl.BlockSpec((1, LANES), lambda i: (0, 0))],
        out_specs=[pl.BlockSpec((t, qk), lambda i: (i, 0))] * 5 + [pl.BlockSpec((t, cw), lambda i: (i, 0))],
        out_shape=[out] * 5 + [jax.ShapeDtypeStruct((n, cw), BF16)],
        scratch_shapes=[pltpu.VMEM((t + HALO, cw), F32)], name=name,
        sem=("parallel",), args=(proj, proj, proj, conv_w, alog, dtb), rider=rider)


def gdn_prep_bwd(proj, u, alog, dtb, dq, dk, dv, dbb, dgb, n_heads, name):
    n = proj.shape[0]
    qk = n_heads * A_HEAD
    cw = 3 * qk
    ba_blk = (cw + qk) // LANES
    t = _pick(n, 256, 8)

    def body(u_ref, pba_ref, al_ref, dt_ref, dq_ref, dk_ref, dv_ref, dbb_ref, dgb_ref,
             du_ref, dba_ref, dal_ref, ddt_ref):
        i = pl.program_id(0)
        u = u_ref[...].astype(F32)
        pba = pba_ref[...]
        lane0 = lax.broadcasted_iota(jnp.int32, (t, A_HEAD), 1) == 0
        dba = jnp.zeros((t, LANES), F32)
        dal = jnp.zeros((1, LANES), F32)
        ddt = jnp.zeros((1, LANES), F32)
        for hd in range(n_heads):
            s0 = slice(hd * A_HEAD, (hd + 1) * A_HEAD)
            s1 = slice(qk + hd * A_HEAD, qk + (hd + 1) * A_HEAD)
            s2 = slice(2 * qk + hd * A_HEAD, 2 * qk + (hd + 1) * A_HEAD)
            fn = functools.partial(_gdn_head, head=hd, n_heads=n_heads)
            _, vjp = jax.vjp(fn, u[:, s0], u[:, s1], u[:, s2], pba, al_ref[...], dt_ref[...])
            cts = (dq_ref[:, s0], dk_ref[:, s0], dv_ref[:, s0],
                   jnp.where(lane0, dbb_ref[:, s0], 0.0), jnp.where(lane0, dgb_ref[:, s0], 0.0))
            duq, duk, duv, dpba, da, dd = vjp(cts)
            du_ref[:, s0] = duq
            du_ref[:, s1] = duk
            du_ref[:, s2] = duv
            dba = dba + dpba
            dal = dal + da
            ddt = ddt + dd
        dba_ref[...] = dba

        @pl.when(i == 0)
        def _():
            dal_ref[...] = jnp.zeros_like(dal_ref)
            ddt_ref[...] = jnp.zeros_like(ddt_ref)
        dal_ref[...] += dal
        ddt_ref[...] += ddt

    tok = pl.BlockSpec((t, qk), lambda i: (i, 0))
    row = pl.BlockSpec((1, LANES), lambda i: (0, 0))
    return pl.pallas_call(
        body, grid=(n // t,),
        in_specs=[pl.BlockSpec((t, cw), lambda i: (i, 0)),
                  pl.BlockSpec((t, LANES), lambda i: (i, ba_blk)), row, row,
                  tok, tok, tok, tok, tok],
        out_specs=[pl.BlockSpec((t, cw), lambda i: (i, 0)), pl.BlockSpec((t, LANES), lambda i: (i, 0)), row, row],
        out_shape=[jax.ShapeDtypeStruct((n, cw), F32), jax.ShapeDtypeStruct((n, LANES), F32),
                   jax.ShapeDtypeStruct((1, LANES), F32), jax.ShapeDtypeStruct((1, LANES), F32)],
        name=name, compiler_params=_cp(("arbitrary",)))(u, proj, alog, dtb, dq, dk, dv, dbb, dgb)


def conv_bwd(du, x, conv_w, width, name, rider=None):
    n, cw = du.shape
    t = _pick(n, 256, 8)
    last = n // t - 1

    chunks = _col_chunks(cw, LANES)
    tc = chunks[0].stop

    def body(du_ref, dnext_ref, x_ref, cw_ref, dx_ref, dcw_ref, ds):
        i = pl.program_id(0)

        @pl.when(i == 0)
        def _():
            dcw_ref[...] = jnp.zeros_like(dcw_ref)
        for cs in chunks:
            dx_ref[:, cs] = conv_bwd_tail(du_ref, dnext_ref, x_ref, cw_ref, dcw_ref, ds, width, t, i, last, cs)

    return host_call(
        body, grid=(n // t,),
        in_specs=[pl.BlockSpec((t, cw), lambda i: (i, 0)),
                  pl.BlockSpec((HALO, cw), _next_halo_map(t, n)),
                  pl.BlockSpec((t, cw), lambda i: (i, 0)),
                  pl.BlockSpec((width, cw), lambda i: (0, 0))],
        out_specs=[pl.BlockSpec((t, cw), lambda i: (i, 0)), pl.BlockSpec((width, cw), lambda i: (0, 0))],
        out_shape=[jax.ShapeDtypeStruct((n, cw), F32), jax.ShapeDtypeStruct((width, cw), F32)],
        scratch_shapes=[pltpu.VMEM((t + HALO, tc), F32)], name=name,
        sem=("arbitrary",), args=(du, du, x, conv_w), rider=rider)


def _b(x):
    return x.astype(BF16)


def _mm_nn(a, b):
    return _dot(_b(a), _b(b))


def _mm_nt(a, b):
    return _dot_nt(_b(a), _b(b))


def _mm_tn(a, b):
    return _dot_tn(_b(a), _b(b))


@jax.custom_vjp
def _mmg_nn(a, b):
    return _mm_nn(a, b)


_mmg_nn.defvjp(lambda a, b: (_mm_nn(a, b), (a, b)),
               lambda res, dc: (_mm_nt(dc, res[1]), _mm_tn(res[0], dc)))


@jax.custom_vjp
def _mmg_nt(a, b):
    return _mm_nt(a, b)


_mmg_nt.defvjp(lambda a, b: (_mm_nt(a, b), (a, b)),
               lambda res, dc: (_mm_nn(dc, res[1]), _mm_tn(dc, res[0])))


@jax.custom_vjp
def _mmg_tn(a, b):
    return _mm_tn(a, b)


_mmg_tn.defvjp(lambda a, b: (_mm_tn(a, b), (a, b)),
               lambda res, dc: (_mm_nt(res[1], dc), _mm_nn(res[0], dc)))


def _bf16_parts(x, n):
    parts = []
    for _ in range(n):
        p = x.astype(BF16)
        parts.append(p)
        x = x - p.astype(F32)
    return parts


def _dot_f32ish(a, b):
    (ah, al), (bh, bl) = _bf16_parts(a, 2), _bf16_parts(b, 2)
    return _dot(ah, bh) + _dot(ah, bl) + _dot(al, bh)


def _tri_dot(x, transpose):
    c = x.shape[0]
    low = lax.broadcasted_iota(jnp.int32, (c, c), 0) >= lax.broadcasted_iota(jnp.int32, (c, c), 1)
    tri = jnp.where(low, 1.0, 0.0).astype(BF16)
    mm = _dot_tn if transpose else _dot
    return functools.reduce(lambda a, b: a + b, [mm(tri, p) for p in _bf16_parts(x, 3)])


def _cumsum(x):
    return _tri_dot(x, False)


@jax.custom_vjp
def _cumsum_g(x):
    return _tri_dot(x, False)


_cumsum_g.defvjp(lambda x: (_tri_dot(x, False), None), lambda _, ct: (_tri_dot(ct, True),))


def _each(f, *lists):
    return [f(*a) for a in zip(*lists)]


def _unit_lower_inv(ms):
    c = ms[0].shape[0]
    eye = jnp.where(lax.broadcasted_iota(jnp.int32, (c, c), 0) == lax.broadcasted_iota(jnp.int32, (c, c), 1), 1.0, 0.0)
    xs = [eye - m for m in ms]
    pws = _each(_mm_nn, ms, ms)
    for it in range(5):
        xs = _each(lambda x, pw: x + _mm_nn(x, pw), xs, pws)
        if it < 4:
            pws = _each(_mm_nn, pws, pws)
    rs = _each(lambda m, x: eye - x - _dot_f32ish(m, x), ms, xs)
    return _each(lambda x, r: x + _mm_nn(x, r), xs, rs)


@jax.custom_vjp
def _saved_inv_g(ms, xs):
    return xs


_saved_inv_g.defvjp(lambda ms, xs: (xs, xs),
                    lambda xs, dxs: (_each(lambda t, x: -_mm_nt(t, x), _each(_mm_tn, xs, dxs), xs),
                                     [jnp.zeros_like(x) for x in xs]))


def _gdn_chunk(ops, state, q, k, v, bb, gb):
    nn, nt, tn, inv, cum = ops
    c = CHUNK
    ri = lax.broadcasted_iota(jnp.int32, (c, c), 0)
    ci = lax.broadcasted_iota(jnp.int32, (c, c), 1)
    causal = ri >= ci
    strict = ri > ci
    gc = [cum(g) for g in gb]
    decay = [jnp.where(causal, jnp.exp(jnp.where(causal, x[:, :c] - x.T[:c, :], 0.0)), 0.0) for x in gc]
    kb = _each(lambda a, b: a * b, k, bb)
    kk = _each(nt, kb, k)
    m = _each(lambda a, d: jnp.where(strict, a * d, 0.0), kk, decay)
    tinv = inv(m)
    egc = [jnp.exp(x) for x in gc]
    u = _each(nn, tinv, _each(lambda a, b: a * b, v, bb))
    w = _each(nn, tinv, _each(lambda a, b: a * b, kb, egc))
    attn = _each(lambda a, d: a * d, _each(nt, q, k), decay)
    glast = [jnp.sum(g, axis=0, keepdims=True) for g in gb]
    ws = _each(nn, w, state)
    v_new = _each(lambda a, b: a - b, u, ws)
    qs = _each(nn, _each(lambda a, b: a * b, q, egc), state)
    av = _each(nn, attn, v_new)
    o = _each(lambda a, b: a + b, qs, av)
    kv = _each(tn, _each(lambda a, gl, x: a * jnp.exp(gl - x), k, glast, gc), v_new)
    new_state = _each(lambda s, gl, a: s * jnp.exp(gl) + a, state, glast, kv)
    return o, new_state


def gdn_fwd(q, k, v, bb, gb, n_heads, name, rider=None):
    n, w = q.shape
    nc = n // CHUNK
    cb = min(8, nc)
    rows = cb * CHUNK

    def body(q_ref, k_ref, v_ref, b_ref, g_ref, o_ref, st_ref, ti_ref, s_scr):
        @pl.when(pl.program_id(0) == 0)
        def _():
            s_scr[...] = jnp.zeros_like(s_scr)

        def step(c, carry):
            sl = pl.ds(pl.multiple_of(c * CHUNK, CHUNK), CHUNK)
            lanes = [slice(hd * A_HEAD, (hd + 1) * A_HEAD) for hd in range(n_heads)]
            state = [s_scr[hd] for hd in range(n_heads)]
            inverses = []

            def inv(ms):
                inverses.extend(_unit_lower_inv(ms))
                return inverses

            o, new_state = _gdn_chunk((_mm_nn, _mm_nt, _mm_tn, inv, _cumsum), state,
                                      *[[r[sl, ls] for ls in lanes] for r in (q_ref, k_ref, v_ref, b_ref, g_ref)])
            for hd, ls in enumerate(lanes):
                st_ref[hd, pl.ds(c, 1)] = state[hd][None]
                ti_ref[hd, pl.ds(c, 1)] = inverses[hd].astype(BF16)[None]
                o_ref[sl, ls] = o[hd]
                s_scr[hd] = new_state[hd]
            return carry

        lax.fori_loop(0, cb, step, 0)

    tok = pl.BlockSpec((rows, w), lambda j: (j, 0))
    return host_call(
        body, grid=(nc // cb,),
        in_specs=[tok] * 5,
        out_specs=[tok, pl.BlockSpec((n_heads, cb, A_HEAD, A_HEAD), lambda j: (0, j, 0, 0)),
                   pl.BlockSpec((n_heads, cb, CHUNK, CHUNK), lambda j: (0, j, 0, 0))],
        out_shape=[jax.ShapeDtypeStruct(q.shape, F32), jax.ShapeDtypeStruct((n_heads, nc, A_HEAD, A_HEAD), F32),
                   jax.ShapeDtypeStruct((n_heads, nc, CHUNK, CHUNK), BF16)],
        scratch_shapes=[pltpu.VMEM((n_heads, A_HEAD, A_HEAD), F32)], name=name,
        sem=("arbitrary",), args=(q, k, v, bb, gb), rider=rider)


def gdn_bwd(q, k, v, bb, gb, states, tinv, do, n_heads, name, rider=None):
    n, w = q.shape
    nc = n // CHUNK
    cb = min(4, nc)
    rows = cb * CHUNK
    nblk = nc // cb

    def body(q_ref, k_ref, v_ref, b_ref, g_ref, st_ref, ti_ref, do_ref,
             dq_ref, dk_ref, dv_ref, db_ref, dg_ref, ds_scr):
        @pl.when(pl.program_id(0) == 0)
        def _():
            ds_scr[...] = jnp.zeros_like(ds_scr)

        def step(s, carry):
            c = cb - 1 - s
            sl = pl.ds(pl.multiple_of(c * CHUNK, CHUNK), CHUNK)
            for h0 in range(0, n_heads, GDN_BWD_HEADS):
                heads = list(range(h0, min(h0 + GDN_BWD_HEADS, n_heads)))
                lanes = [slice(hd * A_HEAD, (hd + 1) * A_HEAD) for hd in heads]
                state = [st_ref[hd, pl.ds(c, 1)][0] for hd in heads]
                saved = [ti_ref[hd, pl.ds(c, 1)][0].astype(F32) for hd in heads]
                chunk_fn = functools.partial(
                    _gdn_chunk, (_mmg_nn, _mmg_nt, _mmg_tn, lambda ms: _saved_inv_g(ms, saved), _cumsum_g))
                _, vjp = jax.vjp(chunk_fn, state, *[[r[sl, ls] for ls in lanes]
                                                    for r in (q_ref, k_ref, v_ref, b_ref, g_ref)])
                dstate, dq, dk, dv, dbb, dgb = vjp(([do_ref[sl, ls] for ls in lanes], [ds_scr[hd] for hd in heads]))
                for u, (hd, ls) in enumerate(zip(heads, lanes)):
                    ds_scr[hd] = dstate[u]
                    dq_ref[sl, ls] = dq[u]
                    dk_ref[sl, ls] = dk[u]
                    dv_ref[sl, ls] = dv[u]
                    db_ref[sl, ls] = jnp.broadcast_to(jnp.sum(dbb[u], axis=-1, keepdims=True), dbb[u].shape)
                    dg_ref[sl, ls] = jnp.broadcast_to(jnp.sum(dgb[u], axis=-1, keepdims=True), dgb[u].shape)
            return carry

        lax.fori_loop(0, cb, step, 0)

    tok = pl.BlockSpec((rows, w), lambda j: (nblk - 1 - j, 0))
    out = jax.ShapeDtypeStruct(q.shape, F32)
    return host_call(
        body, grid=(nblk,),
        in_specs=[tok] * 5 + [pl.BlockSpec((n_heads, cb, A_HEAD, A_HEAD), lambda j: (0, nblk - 1 - j, 0, 0)),
                              pl.BlockSpec((n_heads, cb, CHUNK, CHUNK), lambda j: (0, nblk - 1 - j, 0, 0)), tok],
        out_specs=[tok] * 5, out_shape=[out] * 5,
        scratch_shapes=[pltpu.VMEM((n_heads, A_HEAD, A_HEAD), F32)], name=name,
        sem=("arbitrary",), args=(q, k, v, bb, gb, states, tinv, do), rider=rider)


def _gdn_gate(oh, zh, w):
    r = lax.rsqrt(jnp.mean(oh * oh, axis=-1, keepdims=True) + EPS)
    return oh * r * w * (zh * _sig(zh))


def gdn_out(o, proj, out_norm, w_out, h, n_heads, name):
    n, vw = o.shape
    d = h.shape[1]
    z_blk = 3 * vw // vw
    t = _pick(n, 512, 8)

    def body(o_ref, z_ref, w_ref, wo_ref, h_ref, out_ref, y_ref):
        for hd in range(n_heads):
            s0 = slice(hd * A_HEAD, (hd + 1) * A_HEAD)
            y_ref[:, s0] = _gdn_gate(o_ref[:, s0], z_ref[:, s0], w_ref[...]).astype(BF16)
        out_ref[...] = h_ref[...] + _dot(y_ref[...], wo_ref[...])

    return pl.pallas_call(
        body, grid=(n // t,),
        in_specs=[pl.BlockSpec((t, vw), lambda i: (i, 0)),
                  pl.BlockSpec((t, vw), lambda i: (i, z_blk)),
                  pl.BlockSpec((1, A_HEAD), lambda i: (0, 0)),
                  _resident((vw, d), lambda i: (0, 0)),
                  pl.BlockSpec((t, d), lambda i: (i, 0))],
        out_specs=[pl.BlockSpec((t, d), lambda i: (i, 0)), pl.BlockSpec((t, vw), lambda i: (i, 0))],
        out_shape=[jax.ShapeDtypeStruct((n, d), F32), jax.ShapeDtypeStruct((n, vw), BF16)], name=name,
        compiler_params=_cp(("parallel",)))(o, proj, out_norm, w_out, h)


def gdn_out_bwd(dout, o, proj, out_norm, w_out, n_heads, name):
    n, vw = o.shape
    d = dout.shape[1]
    z_blk = 3
    t = _pick(n, 512, 8)

    def body(dout_ref, o_ref, z_ref, w_ref, wo_ref, do_ref, dz_ref, dw_ref):
        dy = _dot_nt(dout_ref[...].astype(BF16), wo_ref[...])
        dw = jnp.zeros((1, A_HEAD), F32)
        for hd in range(n_heads):
            s0 = slice(hd * A_HEAD, (hd + 1) * A_HEAD)
            _, vjp = jax.vjp(_gdn_gate, o_ref[:, s0], z_ref[:, s0], w_ref[...])
            doh, dzh, dwh = vjp(dy[:, s0])
            do_ref[:, s0] = doh
            dz_ref[:, s0] = dzh
            dw = dw + dwh

        @pl.when(pl.program_id(0) == 0)
        def _():
            dw_ref[...] = jnp.zeros_like(dw_ref)
        dw_ref[...] += dw

    tok = pl.BlockSpec((t, vw), lambda i: (i, 0))
    return pl.pallas_call(
        body, grid=(n // t,),
        in_specs=[pl.BlockSpec((t, d), lambda i: (i, 0)), tok,
                  pl.BlockSpec((t, vw), lambda i: (i, z_blk)),
                  pl.BlockSpec((1, A_HEAD), lambda i: (0, 0)),
                  _resident((vw, d), lambda i: (0, 0))],
        out_specs=[tok, tok, pl.BlockSpec((1, A_HEAD), lambda i: (0, 0))],
        out_shape=[jax.ShapeDtypeStruct((n, vw), F32), jax.ShapeDtypeStruct((n, vw), F32),
                   jax.ShapeDtypeStruct((1, A_HEAD), F32)], name=name,
        compiler_params=_cp(("arbitrary",)))(dout, o, proj, out_norm, w_out)


BIAS_LINE = 768
BIAS_TOP = BAND + CHUNK - 2


def _bias_line_onehot(clip, tbl_pad):
    r = lax.broadcasted_iota(jnp.int32, (tbl_pad, BIAS_LINE), 0)
    v = lax.broadcasted_iota(jnp.int32, (tbl_pad, BIAS_LINE), 1)
    idx = jnp.clip(BIAS_TOP - v - (CHUNK - 1), -clip, clip) + clip
    return jnp.where((r == idx) & (v <= BIAS_TOP), 1.0, 0.0)


def bias_expand(tbl, clip, name):
    nh, tp = tbl.shape

    def body(t_ref, o_ref):
        line = _hdot(t_ref[...], _bias_line_onehot(clip, tp))
        keep = lax.broadcasted_iota(jnp.int32, (nh, BAND_PAD), 1) < BAND
        for i in range(CHUNK):
            s = CHUNK - 1 - i
            rolled = line if s == 0 else pltpu.roll(line, BIAS_LINE - s, axis=1)
            o_ref[i] = jnp.where(keep, rolled[:, :BAND_PAD], NEG_INF)

    return pl.pallas_call(
        body, in_specs=[pl.BlockSpec(memory_space=pltpu.VMEM)], out_specs=pl.BlockSpec(memory_space=pltpu.VMEM),
        out_shape=jax.ShapeDtypeStruct((CHUNK, nh, BAND_PAD), F32), name=name, compiler_params=_cp())(tbl)


def bias_expand_bwd(dbias, clip, tp, name):
    _, nh, _ = dbias.shape

    def body(d_ref, o_ref):
        keep = lax.broadcasted_iota(jnp.int32, (nh, BAND_PAD), 1) < BAND
        pad = jnp.zeros((nh, BIAS_LINE - BAND_PAD), F32)
        acc = jnp.zeros((nh, BIAS_LINE), F32)
        for i in range(CHUNK):
            s = CHUNK - 1 - i
            d = jnp.concatenate([jnp.where(keep, d_ref[i], 0.0), pad], axis=1)
            acc = acc + (d if s == 0 else pltpu.roll(d, s, axis=1))
        o_ref[...] = _hdot_nt(acc, _bias_line_onehot(clip, tp))

    return pl.pallas_call(
        body, in_specs=[pl.BlockSpec(memory_space=pltpu.VMEM)], out_specs=pl.BlockSpec(memory_space=pltpu.VMEM),
        out_shape=jax.ShapeDtypeStruct((nh, tp), F32), name=name, compiler_params=_cp())(dbias)


ATT_TILE = LEFT_CHUNKS * CHUNK


ATT_GROUP = 8


def _att_softmax(s, bias, n_chunk):
    slot = lax.broadcasted_iota(jnp.int32, (1, s.shape[1]), 1)
    before_start = jnp.where(slot < (LEFT_CHUNKS - n_chunk) * CHUNK, NEG_INF, 0.0)
    s = s + bias + before_start
    p = jnp.exp(s - jnp.max(s, axis=-1, keepdims=True))
    return p / jnp.sum(p, axis=-1, keepdims=True)


def _att_specs(n_pairs):
    prev = lambda p, i: (jnp.maximum(i - 1, 0), p)
    cur = lambda p, i: (i, p)
    prev_v = lambda p, i: (jnp.maximum(i - 1, 0), n_pairs + p)
    cur_v = lambda p, i: (i, n_pairs + p)
    blk = (ATT_TILE, LANES)
    return [pl.BlockSpec(blk, prev), pl.BlockSpec(blk, cur), pl.BlockSpec(blk, prev_v), pl.BlockSpec(blk, cur_v)]


def _att_fill(kbuf, vbuf, kp_ref, kc_ref, vp_ref, vc_ref):
    t = ATT_TILE
    kbuf[0:t, :] = kp_ref[...].astype(BF16)
    kbuf[t:2 * t, :] = kc_ref[...].astype(BF16)
    kbuf[2 * t:, :] = jnp.zeros((CHUNK, LANES), BF16)
    vbuf[0:t, :] = vp_ref[...].astype(BF16)
    vbuf[t:2 * t, :] = vc_ref[...].astype(BF16)
    vbuf[2 * t:, :] = jnp.zeros((CHUNK, LANES), BF16)


def _stack_heads(x, first):
    return jnp.concatenate([jnp.where(first, x, 0.0), jnp.where(first, 0.0, x)], axis=0).astype(BF16)


def attn_fwd(qp, kv, bias, name, rider=None):
    n, bw = qp.shape
    n_pairs = bw // LANES
    t = ATT_TILE
    cpt = t // CHUNK

    def body(q_ref, kp_ref, kc_ref, vp_ref, vc_ref, b_ref, o_ref, kbuf, vbuf):
        i = pl.program_id(1)
        _att_fill(kbuf, vbuf, kp_ref, kc_ref, vp_ref, vc_ref)
        lane = lax.broadcasted_iota(jnp.int32, (CHUNK, LANES), 1)
        first = lane < B_HEAD
        for g0 in range(0, cpt, ATT_GROUP):
            chunks = list(range(g0, min(g0 + ATT_GROUP, cpt)))
            band = [slice(c * CHUNK, c * CHUNK + BAND_PAD) for c in chunks]
            q2 = [_stack_heads(q_ref[c * CHUNK:(c + 1) * CHUNK, :] * (B_HEAD ** -0.5), first) for c in chunks]
            s = [_dot_nt(q_u, kbuf[b_u, :]) for q_u, b_u in zip(q2, band)]
            p = [_att_softmax(s_u, b_ref[0], i * cpt + c) for s_u, c in zip(s, chunks)]
            o = [_dot(p_u.astype(BF16), vbuf[b_u, :]) for p_u, b_u in zip(p, band)]
            for o_u, c in zip(o, chunks):
                o_ref[c * CHUNK:(c + 1) * CHUNK, :] = jnp.where(first, o_u[:CHUNK], o_u[CHUNK:])

    return host_call(
        body, grid=(n_pairs, n // t),
        in_specs=[pl.BlockSpec((t, LANES), lambda p, i: (i, p))] + _att_specs(n_pairs)
        + [pl.BlockSpec((1, 2 * CHUNK, BAND_PAD), lambda p, i: (p, 0, 0))],
        out_specs=[pl.BlockSpec((t, LANES), lambda p, i: (i, p))],
        out_shape=[jax.ShapeDtypeStruct((n, bw), F32)],
        scratch_shapes=[pltpu.VMEM((2 * t + CHUNK, LANES), BF16), pltpu.VMEM((2 * t + CHUNK, LANES), BF16)],
        name=name, sem=("parallel", "parallel"), args=(qp, kv, kv, kv, kv, bias), rider=rider)


def attn_bwd(qp, kv, bias, d_o, dk_in, dv_in, name, rider=None):
    n, bw = qp.shape
    n_pairs = bw // LANES
    t = ATT_TILE
    cpt = t // CHUNK
    nt = n // t
    have_in = dk_in is not None
    scale = B_HEAD ** -0.5

    def body(*refs):
        q_ref, kp_ref, kc_ref, vp_ref, vc_ref, b_ref, do_ref = refs[:7]
        pos = 7
        if have_in:
            dki_ref, dvi_ref = refs[7:9]
            pos = 9
        dq_ref, dk_ref, dv_ref, db_ref, kbuf, vbuf, dkacc, dvacc = refs[pos:]
        j = pl.program_id(1)
        i = nt - 1 - j
        _att_fill(kbuf, vbuf, kp_ref, kc_ref, vp_ref, vc_ref)

        @pl.when(j == 0)
        def _():
            dkacc[...] = jnp.zeros_like(dkacc)
            dvacc[...] = jnp.zeros_like(dvacc)
            db_ref[...] = jnp.zeros_like(db_ref)

        @pl.when(j > 0)
        def _():
            dkacc[t:2 * t, :] = dkacc[0:t, :]
            dvacc[t:2 * t, :] = dvacc[0:t, :]
            dkacc[0:t, :] = jnp.zeros((t, LANES), F32)
            dvacc[0:t, :] = jnp.zeros((t, LANES), F32)

        lane = lax.broadcasted_iota(jnp.int32, (CHUNK, LANES), 1)
        first = lane < B_HEAD
        for g0 in range(0, cpt, ATT_GROUP):
            chunks = list(range(g0, min(g0 + ATT_GROUP, cpt)))
            rows = [slice(c * CHUNK, (c + 1) * CHUNK) for c in chunks]
            band = [slice(c * CHUNK, c * CHUNK + BAND_PAD) for c in chunks]
            q2 = [_stack_heads(q_ref[r, :] * scale, first) for r in rows]
            do2 = [_stack_heads(do_ref[r, :], first) for r in rows]
            s = [_dot_nt(q_u, kbuf[b_u, :]) for q_u, b_u in zip(q2, band)]
            dp = [_dot_nt(d_u, vbuf[b_u, :]) for d_u, b_u in zip(do2, band)]
            p = [_att_softmax(s_u, b_ref[0], i * cpt + c) for s_u, c in zip(s, chunks)]
            ds = [p_u * (dp_u - jnp.sum(dp_u * p_u, axis=-1, keepdims=True)) for p_u, dp_u in zip(p, dp)]
            dsb = [d_u.astype(BF16) for d_u in ds]
            dv = [_dot_tn(p_u.astype(BF16), d_u) for p_u, d_u in zip(p, do2)]
            dq = [_dot(d_u, kbuf[b_u, :]) * scale for d_u, b_u in zip(dsb, band)]
            dk = [_dot_tn(d_u, q_u) for d_u, q_u in zip(dsb, q2)]
            db_ref[0] += functools.reduce(lambda a, b: a + b, ds)
            for r in range(chunks[0], chunks[-1] + BAND // CHUNK):
                terms = [(u, r - c) for u, c in enumerate(chunks) if 0 <= r - c < BAND // CHUNK]
                blk = slice(r * CHUNK, (r + 1) * CHUNK)
                dvacc[blk, :] += functools.reduce(lambda a, b: a + b, [dv[u][o * CHUNK:(o + 1) * CHUNK] for u, o in terms])
                dkacc[blk, :] += functools.reduce(lambda a, b: a + b, [dk[u][o * CHUNK:(o + 1) * CHUNK] for u, o in terms])
            for u in range(len(chunks)):
                dq_ref[rows[u], :] = jnp.where(first, dq[u][:CHUNK], dq[u][CHUNK:])

        if have_in:
            dk_ref[...] = dkacc[t:2 * t, :] + dki_ref[...]
            dv_ref[...] = dvacc[t:2 * t, :] + dvi_ref[...]
        else:
            dk_ref[...] = dkacc[t:2 * t, :]
            dv_ref[...] = dvacc[t:2 * t, :]

    rev = lambda p, j: (nt - 1 - j, p)
    tok = pl.BlockSpec((t, LANES), rev)
    kv_specs = [pl.BlockSpec((t, LANES), lambda p, j: (jnp.maximum(nt - 2 - j, 0), p)),
                pl.BlockSpec((t, LANES), rev),
                pl.BlockSpec((t, LANES), lambda p, j: (jnp.maximum(nt - 2 - j, 0), n_pairs + p)),
                pl.BlockSpec((t, LANES), lambda p, j: (nt - 1 - j, n_pairs + p))]
    in_specs = [tok] + kv_specs + [pl.BlockSpec((1, 2 * CHUNK, BAND_PAD), lambda p, j: (p, 0, 0)), tok]
    args = [qp, kv, kv, kv, kv, bias, d_o]
    if have_in:
        in_specs += [tok, tok]
        args += [dk_in, dv_in]
    out = jax.ShapeDtypeStruct((n, bw), F32)
    return host_call(
        body, grid=(n_pairs, nt), in_specs=in_specs,
        out_specs=[tok, tok, tok, pl.BlockSpec((1, 2 * CHUNK, BAND_PAD), lambda p, j: (p, 0, 0))],
        out_shape=[out, out, out, jax.ShapeDtypeStruct((n_pairs, 2 * CHUNK, BAND_PAD), F32)],
        scratch_shapes=[pltpu.VMEM((2 * t + CHUNK, LANES), BF16), pltpu.VMEM((2 * t + CHUNK, LANES), BF16),
                        pltpu.VMEM((2 * t + CHUNK, LANES), F32), pltpu.VMEM((2 * t + CHUNK, LANES), F32)],
        name=name, sem=("parallel", "arbitrary"), args=args, rider=rider)


def adamw(w, gstack, m, v, name):
    r, c = w.shape
    s = gstack.shape[0]
    tr = _pick(r, 512, 8)

    def body(w_ref, g_ref, m_ref, v_ref, go_ref, d_ref, mo_ref, vo_ref):
        g = g_ref[0].astype(F32)
        for k in range(1, s):
            g = g + g_ref[k].astype(F32)
        mn = ADAM_B1 * m_ref[...] + (1.0 - ADAM_B1) * g
        vn = ADAM_B2 * v_ref[...] + (1.0 - ADAM_B2) * (g * g)
        m_hat = mn / (1.0 - ADAM_B1 ** ADAM_STEP)
        v_hat = vn / (1.0 - ADAM_B2 ** ADAM_STEP)
        go_ref[...] = g
        d_ref[...] = -ADAM_LR * (m_hat / (jnp.sqrt(v_hat) + ADAM_EPS) + ADAM_WD * w_ref[...])
        mo_ref[...] = mn
        vo_ref[...] = vn

    blk = pl.BlockSpec((tr, c), lambda i: (i, 0))
    out = jax.ShapeDtypeStruct((r, c), F32)
    return pl.pallas_call(
        body, grid=(r // tr,),
        in_specs=[blk, pl.BlockSpec((s, tr, c), lambda i: (0, i, 0)), blk, blk],
        out_specs=[blk] * 4, out_shape=[out] * 4, name=name,
        compiler_params=_cp(("parallel",)))(w, gstack, m, v)


def _place():
    x, y, c = lax.axis_index("x"), lax.axis_index("y"), lax.axis_index("c")
    chips = [(1 - x, y), (x, 1 - y), (1 - x, 1 - y)]
    return x, y, c, chips


def _ag_copy(outs, send_sems, recv_sems, t, k, block, to, src=None):
    def slot(dev):
        return outs[t].at[4 * dev[0] + 2 * dev[1] + dev[2]]
    return pltpu.make_async_remote_copy(
        src_ref=slot(block) if src is None else src, dst_ref=slot(block),
        send_sem=send_sems.at[7 * t + k], recv_sem=recv_sems.at[7 * t + k], device_id=to, device_id_type=MESH)


def _ag_start(ins, outs, send_sems, recv_sems, local_sems):
    x, y, c, chips = _place()
    me = (x, y, c)
    for t in range(len(ins)):
        pltpu.make_async_copy(ins[t], outs[t].at[4 * x + 2 * y + c], local_sems.at[t]).start()
        _ag_copy(outs, send_sems, recv_sems, t, 0, me, (x, y, 1 - c), src=ins[t]).start()
        for j, chip in enumerate(chips):
            _ag_copy(outs, send_sems, recv_sems, t, 1 + j, me, (*chip, c), src=ins[t]).start()


def _ag_finish(ins, outs, send_sems, recv_sems, local_sems):
    x, y, c, chips = _place()
    me, sibling = (x, y, c), (x, y, 1 - c)
    nt = len(ins)
    for t in range(nt):
        for j, chip in enumerate(chips):
            _ag_copy(outs, send_sems, recv_sems, t, 1 + j, (*chip, c), me).wait_recv()
            _ag_copy(outs, send_sems, recv_sems, t, 4 + j, (*chip, c), sibling).start()
    for t in range(nt):
        _ag_copy(outs, send_sems, recv_sems, t, 0, sibling, me).wait_recv()
        for j, chip in enumerate(chips):
            _ag_copy(outs, send_sems, recv_sems, t, 4 + j, (*chip, 1 - c), me).wait_recv()
    for t in range(nt):
        _ag_copy(outs, send_sems, recv_sems, t, 0, me, sibling, src=ins[t]).wait_send()
        for j, chip in enumerate(chips):
            _ag_copy(outs, send_sems, recv_sems, t, 1 + j, me, (*chip, c), src=ins[t]).wait_send()
            _ag_copy(outs, send_sems, recv_sems, t, 4 + j, (*chip, c), sibling).wait_send()
        pltpu.make_async_copy(ins[t], outs[t].at[4 * x + 2 * y + c], local_sems.at[t]).wait()


def _rs_a_copy(ins, outs, send_sems, recv_sems, t, q):
    x, y, c, _ = _place()
    return pltpu.make_async_remote_copy(
        src_ref=ins[t].at[2 * q + (1 - c)], dst_ref=outs[t].at[q],
        send_sem=send_sems.at[4 * t + q], recv_sem=recv_sems.at[4 * t + q],
        device_id=(x, y, 1 - c), device_id_type=MESH)


def _rs_a_start(ins, outs, send_sems, recv_sems, local_sems):
    for t in range(len(ins)):
        for q in range(4):
            _rs_a_copy(ins, outs, send_sems, recv_sems, t, q).start()


def _rs_a_finish(ins, outs, send_sems, recv_sems, local_sems):
    for t in range(len(ins)):
        for q in range(4):
            _rs_a_copy(ins, outs, send_sems, recv_sems, t, q).wait_recv()
    for t in range(len(ins)):
        for q in range(4):
            _rs_a_copy(ins, outs, send_sems, recv_sems, t, q).wait_send()


def _rs_b_copy(ins, outs, send_sems, recv_sems, t, j, sending):
    x, y, c, chips = _place()
    mine, other = 2 * x + y, 2 * chips[j][0] + chips[j][1]
    return pltpu.make_async_remote_copy(
        src_ref=ins[t].at[other if sending else mine], dst_ref=outs[t].at[mine if sending else other],
        send_sem=send_sems.at[3 * t + j], recv_sem=recv_sems.at[3 * t + j],
        device_id=(*chips[j], c), device_id_type=MESH)


def _rs_b_start(ins, outs, send_sems, recv_sems, local_sems):
    x, y, _, _ = _place()
    for t in range(len(ins)):
        for j in range(3):
            _rs_b_copy(ins, outs, send_sems, recv_sems, t, j, True).start()
        pltpu.make_async_copy(ins[t].at[2 * x + y], outs[t].at[2 * x + y], local_sems.at[t]).start()


def _rs_b_finish(ins, outs, send_sems, recv_sems, local_sems):
    x, y, _, _ = _place()
    for t in range(len(ins)):
        for j in range(3):
            _rs_b_copy(ins, outs, send_sems, recv_sems, t, j, False).wait_recv()
    for t in range(len(ins)):
        for j in range(3):
            _rs_b_copy(ins, outs, send_sems, recv_sems, t, j, True).wait_send()
        pltpu.make_async_copy(ins[t].at[2 * x + y], outs[t].at[2 * x + y], local_sems.at[t]).wait()


_EXCHANGES = {
    "all_gather": (7, lambda a: (N_DEV, *a.shape), _ag_start, _ag_finish),
    "rs_sibling": (4, lambda a: (4, *a.shape[1:]), _rs_a_start, _rs_a_finish),
    "rs_chips": (3, lambda a: a.shape, _rs_b_start, _rs_b_finish),
}


def _exchange_parts(kind, arrays):
    per, shape_of, start, finish = _EXCHANGES[kind]
    n = len(arrays)
    out_shape = [jax.ShapeDtypeStruct(shape_of(a), a.dtype) for a in arrays]
    sems = [pltpu.SemaphoreType.DMA((per * n,)), pltpu.SemaphoreType.DMA((per * n,)), pltpu.SemaphoreType.DMA((n,))]
    return out_shape, sems, start, finish


def exchange(kind, arrays, name):
    n = len(arrays)
    out_shape, sems, start, finish = _exchange_parts(kind, arrays)
    any_spec = pl.BlockSpec(memory_space=pl.ANY)

    def body(*refs):
        ins, outs, sem_refs = refs[:n], refs[n:2 * n], refs[2 * n:]
        start(ins, outs, *sem_refs)
        finish(ins, outs, *sem_refs)

    return pl.pallas_call(body, in_specs=[any_spec] * n, out_specs=[any_spec] * n, out_shape=out_shape,
                          scratch_shapes=sems, name=name)(*arrays)


def host_call(body, *, grid, in_specs, out_specs, out_shape, scratch_shapes, args, name, sem, rider=None):
    if not rider:
        outs = pl.pallas_call(body, grid=grid, in_specs=in_specs, out_specs=out_specs, out_shape=out_shape,
                              scratch_shapes=scratch_shapes, name=name, compiler_params=_cp(sem))(*args)
        return outs, None
    riders = [rider] if isinstance(rider, tuple) else list(rider)
    arrays = [a for _, arrs in riders for a in arrs]
    parts = [_exchange_parts(kind, arrs) for kind, arrs in riders]
    counts = [len(arrs) for _, arrs in riders]
    nr, ni, no, ns = len(arrays), len(in_specs), len(out_specs), len(scratch_shapes)
    any_spec = pl.BlockSpec(memory_space=pl.ANY)

    def wrapped(*refs):
        ins, r_ins = refs[:ni], refs[ni:ni + nr]
        outs, r_outs = refs[ni + nr:ni + nr + no], refs[ni + nr + no:ni + 2 * nr + no]
        scratch, sem_refs = refs[ni + 2 * nr + no:ni + 2 * nr + no + ns], refs[ni + 2 * nr + no + ns:]
        first = pl.program_id(0) == 0
        last = pl.program_id(0) == grid[0] - 1
        for ax in range(1, len(grid)):
            first = first & (pl.program_id(ax) == 0)
            last = last & (pl.program_id(ax) == grid[ax] - 1)

        def each(which):
            pos = 0
            for e, (cnt, part) in enumerate(zip(counts, parts)):
                part[which](r_ins[pos:pos + cnt], r_outs[pos:pos + cnt], *sem_refs[3 * e:3 * e + 3])
                pos += cnt

        @pl.when(first)
        def _():
            each(2)
        body(*ins, *outs, *scratch)

        @pl.when(last)
        def _():
            each(3)

    outs = pl.pallas_call(
        wrapped, grid=grid, in_specs=list(in_specs) + [any_spec] * nr, out_specs=list(out_specs) + [any_spec] * nr,
        out_shape=list(out_shape) + [s for p in parts for s in p[0]],
        scratch_shapes=list(scratch_shapes) + [s for p in parts for s in p[1]], name=name,
        compiler_params=_cp(("arbitrary",) * len(grid)))(*args, *arrays)
    got, pos = [], no
    for cnt in counts:
        got.append(outs[pos:pos + cnt])
        pos += cnt
    return outs[:no], (got[0] if isinstance(rider, tuple) else got)


def pair_add(g8, recv, c_idx, name):
    _, r, c = g8.shape
    tr = _pick(r, 512, 8)

    def body(c_ref, g_ref, r_ref, o_ref):
        o_ref[...] = (g_ref[...].astype(F32) + r_ref[...].astype(F32)).astype(BF16)

    return pl.pallas_call(
        body,
        grid_spec=pltpu.PrefetchScalarGridSpec(
            num_scalar_prefetch=1, grid=(4, r // tr),
            in_specs=[pl.BlockSpec((1, tr, c), lambda q, i, cr: (2 * q + cr[0], i, 0)),
                      pl.BlockSpec((1, tr, c), lambda q, i, cr: (q, i, 0))],
            out_specs=pl.BlockSpec((1, tr, c), lambda q, i, cr: (q, i, 0))),
        out_shape=jax.ShapeDtypeStruct((4, r, c), BF16), name=name,
        compiler_params=_cp(("parallel", "parallel")))(c_idx, g8, recv)


def all_reduce_small(pack, name):
    r, c = pack.shape

    def body(x_ref, o_ref, buf, send_sems, recv_sems, local_sem):
        x, y, cc, chips = _place()
        me, sibling = (x, y, cc), (x, y, 1 - cc)

        def slot(dev):
            return buf.at[4 * dev[0] + 2 * dev[1] + dev[2]]

        def copy(k, block, to, src=None):
            return pltpu.make_async_remote_copy(
                src_ref=slot(block) if src is None else src, dst_ref=slot(block),
                send_sem=send_sems.at[k], recv_sem=recv_sems.at[k], device_id=to, device_id_type=MESH)

        mine = pltpu.make_async_copy(x_ref, slot(me), local_sem)
        mine.start()
        first = [copy(0, me, sibling, src=x_ref)]
        first += [copy(1 + j, me, (*chip, cc), src=x_ref) for j, chip in enumerate(chips)]
        for cp in first:
            cp.start()
        passed = [copy(4 + j, (*chip, cc), sibling) for j, chip in enumerate(chips)]
        for j, chip in enumerate(chips):
            copy(1 + j, (*chip, cc), me).wait_recv()
            passed[j].start()
        copy(0, sibling, me).wait_recv()
        for j, chip in enumerate(chips):
            copy(4 + j, (*chip, 1 - cc), me).wait_recv()
        for cp in first + passed:
            cp.wait_send()
        mine.wait()
        acc = buf[0]
        for k in range(1, N_DEV):
            acc = acc + buf[k]
        o_ref[...] = acc

    return pl.pallas_call(
        body, in_specs=[pl.BlockSpec(memory_space=pltpu.VMEM)],
        out_specs=pl.BlockSpec(memory_space=pltpu.VMEM),
        out_shape=jax.ShapeDtypeStruct((r, c), F32),
        scratch_shapes=[pltpu.VMEM((N_DEV, r, c), F32), pltpu.SemaphoreType.DMA((7,)),
                        pltpu.SemaphoreType.DMA((7,)), pltpu.SemaphoreType.DMA],
        name=name, compiler_params=_cp())(pack)


def _row(v):
    return v.reshape(1, -1)


def _lane_row(vals, offset):
    return jnp.pad(vals, (offset, LANES - offset - vals.shape[0])).reshape(1, LANES)


def _bias_to_pairs(b):
    i, nh, bp = b.shape
    return b.transpose(1, 0, 2).reshape(nh // 2, 2 * i, bp)


def _bias_from_pairs(b):
    p, i2, bp = b.shape
    return b.reshape(2 * p, i2 // 2, bp).transpose(1, 0, 2)


class LocalWeights:
    def __init__(self, W):
        self.W = W
        self.grads = {}

    def big(self, l, la):
        W = self.W
        out = {"f_w_up": W["f_w_up"][l].T, "f_w_down": W["f_w_down"][l]}
        if l < la:
            out.update(a_w_in=W["a_w_in"][l], a_w_out=W["a_w_out"][l])
        else:
            out.update(b_w_q=W["b_w_q"][l - la], b_w_out=W["b_w_out"][l - la])
        if l == la:
            out["w_kv"] = W["w_kv"].T
        return out

    def prep_rider(self, l):
        return None

    def prep_got(self, l, got):
        pass

    def fwd_rider(self, l):
        return None

    def fwd_got(self, l, got):
        pass

    def bwd_rider_a(self, l):
        return None

    def bwd_got_a(self, l, got):
        pass

    def bwd_rider_b(self, l):
        return None

    def bwd_got_b(self, l, got):
        pass

    def bwd_rider_c(self, l):
        return None

    def bwd_got_c(self, l, got):
        pass

    def ffn_grads_ready(self, l, grads):
        pass

    def grads_ready(self, l, grads):
        for k_, g in grads.items():
            self.grads.setdefault(k_, {})[l] = g

    def stacked(self):
        out = {k_: (jnp.stack([v_[l] for l in sorted(v_)]) if k_ != "w_kv" else next(iter(v_.values())))
               for k_, v_ in self.grads.items()}
        out["f_w_up"] = jnp.swapaxes(out["f_w_up"], 1, 2)
        out["w_kv"] = out["w_kv"].T
        return out


def _named(name, l, rider):
    return name if rider is None else f"{name}_x{l}"


def local_step(x, target, W, comm=None):
    comm = LocalWeights(W) if comm is None else comm
    n, d = x.shape
    la, ha = W["a_A_log"].shape
    lb, hb, tbl = W["b_rel_bias"].shape
    depth = W["f_norm"].shape[0]
    clip = (tbl - 1) // 2
    tp = -(-tbl // LANES) * LANES
    qk = ha * A_HEAD
    cw = 3 * qk
    bw = hb * B_HEAD

    h = x
    saves = []
    kv = h_kv = w_kv = None
    for l in range(depth):
        big = comm.big(l, la)
        sv = {"h_in": h, "big": big}
        rider = comm.fwd_rider(l)
        if l < la:
            alog = _lane_row(W["a_A_log"][l], ha)
            dtb = _lane_row(W["a_dt_bias"][l], ha)
            proj = norm_matmul(h, _row(W["a_norm"][l]), big["a_w_in"], "a_in_proj")
            early = comm.prep_rider(l)
            (q, k, v, bb, gb, u), got = gdn_prep(proj, W["a_conv"][l], alog, dtb, ha, _named("gdn_prep", l, early), early)
            comm.prep_got(l, got)
            (o, states, tinv), got = gdn_fwd(q, k, v, bb, gb, ha, _named("gdn_fwd", l, rider), rider)
            h, y = gdn_out(o, proj, _row(W["a_out_norm"][l]), big["a_w_out"], h, ha, "gdn_out")
            sv.update(proj=proj, q=q, k=k, v=v, bb=bb, gb=gb, u=u, states=states, tinv=tinv, o=o, y=y, alog=alog, dtb=dtb)
        else:
            j = l - la
            if j == 0:
                h_kv, w_kv = h, big["w_kv"]
                kv = norm_matmul(h, _row(W["kv_norm"]), w_kv, "kv_proj", w_t=True)
            qp = norm_matmul(h, _row(W["b_norm"][j]), big["b_w_q"], "b_q_proj")
            tblp = jnp.pad(W["b_rel_bias"][j], ((0, 0), (0, tp - tbl)))
            bias = _bias_to_pairs(bias_expand(tblp, clip, "bias_expand"))
            (o,), got = attn_fwd(qp, kv, bias, _named("attn_fwd", l, rider), rider)
            h = matmul_res(o, big["b_w_out"], h, "b_out_proj")
            sv.update(qp=qp, bias=bias, o=o)
        comm.fwd_got(l, got)
        sv["h_mid"] = h
        up = norm_matmul(h, _row(W["f_norm"][l]), big["f_w_up"], "f_up_proj", out_dtype=BF16, w_t=True)
        h, act, hc = ffn_act_down(up, W["f_conv"][l], _row(W["f_conv_b"][l]), big["f_w_down"], h, "ffn_act_down")
        sv.update(up=up, act=act, hc=hc)
        saves.append(sv)

    loss, dh, d_final = loss_head(h, _row(W["final_norm"]), target)

    G = {k_: [None] * (la if k_.startswith("a_") else lb if k_.startswith("b_") else depth)
         for k_ in ("a_norm", "a_conv", "a_A_log", "a_dt_bias", "a_out_norm",
                    "b_norm", "b_rel_bias", "f_norm", "f_conv", "f_conv_b")}
    G["final_norm"] = d_final[0]
    dk_acc = dv_acc = None
    for l in reversed(range(depth)):
        sv = saves[l]
        big = sv["big"]
        gbig = {}
        dhc, dcb = ffn_bwd_act(dh, sv["hc"], big["f_w_down"], "ffn_bwd_act")
        gbig["f_w_down"] = matmul_tn(sv["act"], dh, "f_down_wgrad")
        G["f_conv_b"][l] = dcb[0]
        rider = comm.bwd_rider_a(l)
        (dh, dup, dcw, dg), got = ffn_bwd_up(dhc, sv["up"], W["f_conv"][l], big["f_w_up"], sv["h_mid"], dh,
                                             _row(W["f_norm"][l]), _named("ffn_bwd_up", l, rider), rider)
        comm.bwd_got_a(l, got)
        G["f_conv"][l] = dcw
        G["f_norm"][l] = dg[0]
        gbig["f_w_up"] = norm_matmul_tn(sv["h_mid"], _row(W["f_norm"][l]), dup, "f_up_wgrad", transposed=True)
        comm.ffn_grads_ready(l, gbig)
        rider = comm.bwd_rider_b(l)
        if l < la:
            w_in = big["a_w_in"]
            do, dz, dwn = gdn_out_bwd(dh, sv["o"], sv["proj"], _row(W["a_out_norm"][l]), big["a_w_out"], ha, "gdn_out_bwd")
            G["a_out_norm"][l] = dwn[0]
            gbig["a_w_out"] = matmul_tn(sv["y"], dh, "a_out_wgrad")
            (dq, dk, dv, dbb, dgb), got = gdn_bwd(sv["q"], sv["k"], sv["v"], sv["bb"], sv["gb"], sv["states"], sv["tinv"], do, ha,
                                                  _named("gdn_bwd", l, rider), rider)
            comm.bwd_got_b(l, got)
            du, dba, dal, ddt = gdn_prep_bwd(sv["proj"], sv["u"], sv["alog"], sv["dtb"],
                                             dq, dk, dv, dbb, dgb, ha, "gdn_prep_bwd")
            G["a_A_log"][l] = dal[0, ha:2 * ha]
            G["a_dt_bias"][l] = ddt[0, ha:2 * ha]
            rider = comm.bwd_rider_c(l)
            (dqkv, dconv), got = conv_bwd(du, sv["proj"], W["a_conv"][l], A_CONV, _named("gdn_conv_bwd", l, rider), rider)
            if rider is not None:
                comm.bwd_got_c(l, got)
            G["a_conv"][l] = dconv
            gam = _row(W["a_norm"][l])
            pieces = [(dqkv, w_in[:, :cw]), (dz, w_in[:, cw:cw + qk]), (dba, w_in[:, cw + qk:])]
            gbig["a_w_in"] = jnp.concatenate(
                [norm_matmul_tn(sv["h_in"], gam, dqkv, "a_in_wgrad_qkv"),
                 norm_matmul_tn(sv["h_in"], gam, dz, "a_in_wgrad_z"),
                 norm_matmul_tn(sv["h_in"], gam, dba, "a_in_wgrad_ba")[:, :2 * ha]], axis=1)
            dh, dg = dx_norm_bwd(dh, sv["h_in"], gam, pieces, "a_in_dx")
            G["a_norm"][l] = dg[0]
        else:
            j = l - la
            d_o = matmul_nt(dh, big["b_w_out"], "b_out_dx")
            gbig["b_w_out"] = matmul_tn(sv["o"], dh, "b_out_wgrad")
            (dq, dk_acc, dv_acc, dbias), got = attn_bwd(
                sv["qp"], kv, sv["bias"], d_o, dk_acc, dv_acc,
                _named("attn_bwd" if dk_acc is None else "attn_bwd_acc", l, rider), rider)
            comm.bwd_got_b(l, got)
            G["b_rel_bias"][j] = bias_expand_bwd(_bias_from_pairs(dbias), clip, tp, "bias_expand_bwd")[:, :tbl]
            gam = _row(W["b_norm"][j])
            gbig["b_w_q"] = norm_matmul_tn(sv["h_in"], gam, dq, "b_q_wgrad")
            dh, dg = dx_norm_bwd(dh, sv["h_in"], gam, [(dq, big["b_w_q"])], "b_q_dx")
            G["b_norm"][j] = dg[0]
            if j == 0:
                gam = _row(W["kv_norm"])
                gbig["w_kv"] = jnp.concatenate([norm_matmul_tn(h_kv, gam, dk_acc, "kv_wgrad_k", transposed=True),
                                                norm_matmul_tn(h_kv, gam, dv_acc, "kv_wgrad_v", transposed=True)], axis=0)
                dh, dg = dx_norm_bwd(dh, h_kv, gam, [(dk_acc, w_kv[:bw]), (dv_acc, w_kv[bw:])], "kv_dx", w_t=True)
                G["kv_norm"] = dg[0]
        comm.grads_ready(l, gbig)
    out = {k_: (jnp.stack(v_) if isinstance(v_, list) else v_) for k_, v_ in G.items()}
    return loss[0, 0], dh, out, comm


WEIGHTS = ["a_norm", "a_w_in", "a_conv", "a_A_log", "a_dt_bias", "a_out_norm", "a_w_out", "kv_norm", "w_kv",
           "b_norm", "b_w_q", "b_rel_bias", "b_w_out", "f_norm", "f_w_up", "f_conv", "f_conv_b", "f_w_down",
           "final_norm"]
SHARD_AXIS = {"a_norm": 1, "a_w_in": 2, "a_conv": 2, "a_w_out": 1, "w_kv": 1, "b_w_q": 1, "b_w_out": 1,
              "f_w_up": 2, "f_conv": 2, "f_w_down": 1}
BIG = ["a_w_in", "a_w_out", "w_kv", "b_w_q", "b_w_out", "f_w_up", "f_w_down"]
SMALL_SHARDED = ["a_norm", "a_conv", "f_conv"]
TRANSPOSED = ("f_w_up", "w_kv")


def _t_view(k, a):
    return jnp.swapaxes(a, -1, -2) if k in TRANSPOSED else a


def _unstack(g, axis):
    if axis == 0:
        return g.reshape(-1, *g.shape[2:])
    return jnp.concatenate([g[i] for i in range(N_DEV)], axis=axis)


def _to_blocks(full, axis):
    if axis == 0:
        return full.reshape(N_DEV, -1, full.shape[-1])
    return jnp.stack(jnp.split(full, N_DEV, axis=axis))


def _pack(arrs):
    flat = []
    for a in arrs:
        f = a.reshape(-1)
        flat.append(jnp.pad(f, (0, (-f.shape[0]) % LANES)))
    f = jnp.concatenate(flat)
    f = jnp.pad(f, (0, (-f.shape[0]) % (8 * LANES)))
    return f.reshape(-1, LANES)


def _unpack(pack, shapes):
    flat = pack.reshape(-1)
    out, pos = [], 0
    for s in shapes:
        sz = math.prod(s)
        out.append(flat[pos:pos + sz].reshape(s))
        pos += sz + (-sz) % LANES
    return out


def _as2d(a):
    return a.reshape(1, -1) if a.ndim == 1 else a.reshape(-1, a.shape[-1])


def kernel(x, a_norm, a_w_in, a_conv, a_A_log, a_dt_bias, a_out_norm, a_w_out, kv_norm, w_kv, b_norm, b_w_q, b_rel_bias, b_w_out, f_norm, f_w_up, f_conv, f_conv_b, f_w_down, final_norm, loss_target, m_a_norm, m_a_w_in, m_a_conv, m_a_A_log, m_a_dt_bias, m_a_out_norm, m_a_w_out, m_kv_norm, m_w_kv, m_b_norm, m_b_w_q, m_b_rel_bias, m_b_w_out, m_f_norm, m_f_w_up, m_f_conv, m_f_conv_b, m_f_w_down, m_final_norm, v_a_norm, v_a_w_in, v_a_conv, v_a_A_log, v_a_dt_bias, v_a_out_norm, v_a_w_out, v_kv_norm, v_w_kv, v_b_norm, v_b_w_q, v_b_rel_bias, v_b_w_out, v_f_norm, v_f_w_up, v_f_conv, v_f_conv_b, v_f_w_down, v_final_norm):
    w = dict(a_norm=a_norm, a_w_in=a_w_in, a_conv=a_conv, a_A_log=a_A_log, a_dt_bias=a_dt_bias,
             a_out_norm=a_out_norm, a_w_out=a_w_out, kv_norm=kv_norm, w_kv=w_kv, b_norm=b_norm, b_w_q=b_w_q,
             b_rel_bias=b_rel_bias, b_w_out=b_w_out, f_norm=f_norm, f_w_up=f_w_up, f_conv=f_conv,
             f_conv_b=f_conv_b, f_w_down=f_w_down, final_norm=final_norm)
    mom = dict(a_norm=m_a_norm, a_w_in=m_a_w_in, a_conv=m_a_conv, a_A_log=m_a_A_log, a_dt_bias=m_a_dt_bias,
               a_out_norm=m_a_out_norm, a_w_out=m_a_w_out, kv_norm=m_kv_norm, w_kv=m_w_kv, b_norm=m_b_norm,
               b_w_q=m_b_w_q, b_rel_bias=m_b_rel_bias, b_w_out=m_b_w_out, f_norm=m_f_norm, f_w_up=m_f_w_up,
               f_conv=m_f_conv, f_conv_b=m_f_conv_b, f_w_down=m_f_w_down, final_norm=m_final_norm)
    var = dict(a_norm=v_a_norm, a_w_in=v_a_w_in, a_conv=v_a_conv, a_A_log=v_a_A_log, a_dt_bias=v_a_dt_bias,
               a_out_norm=v_a_out_norm, a_w_out=v_a_w_out, kv_norm=v_kv_norm, w_kv=v_w_kv, b_norm=v_b_norm,
               b_w_q=v_b_w_q, b_rel_bias=v_b_rel_bias, b_w_out=v_b_w_out, f_norm=v_f_norm, f_w_up=v_f_w_up,
               f_conv=v_f_conv, f_conv_b=v_f_conv_b, f_w_down=v_f_w_down, final_norm=v_final_norm)
    me = 4 * lax.axis_index("x") + 2 * lax.axis_index("y") + lax.axis_index("c")

    la, depth = a_A_log.shape[0], f_norm.shape[0]
    c_idx = lax.axis_index("c").astype(jnp.int32).reshape(1)
    shard_bf16 = {k: _t_view(k, w[k]).astype(BF16) for k in BIG}
    blk_axis = {k: 0 if k in TRANSPOSED else SHARD_AXIS[k] - (k != "w_kv") for k in BIG}

    class Sharded(LocalWeights):
        def __init__(self):
            super().__init__(w)
            self.full = {}
            self.stacks = {}
            self.pending = None
            self.parts = None

        def names(self, l):
            out = ["a_w_in", "a_w_out"] if l < la else ["b_w_q", "b_w_out"]
            return out + ["f_w_up", "f_w_down"] + (["w_kv"] if l == la else [])

        def index(self, k, l):
            return None if k == "w_kv" else (l - la if k.startswith("b_") else l)

        def shards(self, l, names=None):
            return [shard_bf16[k] if k == "w_kv" else shard_bf16[k][self.index(k, l)]
                    for k in (self.names(l) if names is None else names)]

        def install(self, l, gathered, names=None):
            out = self.full.setdefault(l, {})
            for k, g in zip(self.names(l) if names is None else names, gathered):
                out[k] = _unstack(g, blk_axis[k])
                if k == "a_w_in":
                    out[k] = jnp.pad(out[k], ((0, 0), (0, (-out[k].shape[1]) % LANES)))

        def big(self, l, la_):
            return self.full[l]

        def first_names(self):
            return ["a_w_in"] if la > 0 else self.names(0)

        def prep_rider(self, l):
            rest = [k for k in self.names(0) if k not in self.first_names()]
            return ("all_gather", self.shards(0, rest)) if l == 0 and rest else None

        def prep_got(self, l, got):
            if got is not None:
                self.install(0, got, [k for k in self.names(0) if k not in self.first_names()])

        def fwd_rider(self, l):
            return ("all_gather", self.shards(l + 1)) if l + 1 < depth else None

        def fwd_got(self, l, got):
            if got is not None:
                self.install(l + 1, got)

        def blocks(self, grads, keys):
            return [_to_blocks(grads[k], blk_axis[k]) for k in keys]

        def grads_ready(self, l, grads):
            keys = [k for k in self.names(l) if (k, l) not in self.early_keys]
            self.pending = (l, keys, self.blocks(grads, keys))

        early = early_parts = None
        early_keys = ()

        def ffn_grads_ready(self, l, grads):
            if l == 0 and la > 0:
                keys = ["f_w_up", "f_w_down"]
                self.early = (keys, self.blocks(grads, keys))
                self.early_keys = tuple((k, 0) for k in keys)

        def bwd_rider_a(self, l):
            return None if self.pending is None else ("rs_sibling", self.pending[2])

        def add_pairs(self, g8, from_sibling):
            return [pair_add(g, r, c_idx, "grads_pair_add") for g, r in zip(g8, from_sibling)]

        def bwd_got_a(self, l, got):
            if got is not None:
                self.parts = self.add_pairs(self.pending[2], got)

        def bwd_rider_b(self, l):
            riders = [] if self.parts is None else [("rs_chips", self.parts)]
            if self.early is not None:
                riders.append(("rs_sibling", self.early[1]))
            return riders

        def keep(self, stacks):
            l, keys, _ = self.pending
            for k, s in zip(keys, stacks):
                self.stacks[(k, l)] = s
            self.pending = self.parts = None

        def bwd_got_b(self, l, got):
            got = list(got or [])
            if self.parts is not None:
                self.keep(got.pop(0))
            if self.early is not None and got:
                self.early_parts = self.add_pairs(self.early[1], got.pop(0))

        def bwd_rider_c(self, l):
            return None if self.early_parts is None else ("rs_chips", self.early_parts)

        def bwd_got_c(self, l, got):
            for k, s in zip(self.early[0], got):
                self.stacks[(k, 0)] = s
            self.early = self.early_parts = None

        def finish(self):
            self.parts = self.add_pairs(self.pending[2], exchange("rs_sibling", self.pending[2], "grads_to_sibling"))
            self.keep(exchange("rs_chips", self.parts, "grads_to_chips"))

    comm = Sharded()

    small_shapes = [w[k].shape for k in SMALL_SHARDED]
    gathered = exchange("all_gather", comm.shards(0, comm.first_names()) + [_pack([w[k] for k in SMALL_SHARDED])],
                        "weights_all_gather")
    comm.install(0, gathered[:-1], comm.first_names())
    full = dict(w)
    small = [_unpack(gathered[-1][i], small_shapes) for i in range(N_DEV)]
    for idx, k in enumerate(SMALL_SHARDED):
        full[k] = jnp.concatenate([small[i][idx] for i in range(N_DEV)], axis=SHARD_AXIS[k])

    loss_part, grad_x, G, _ = local_step(x[0], loss_target[0], full, comm)
    comm.finish()
    stacks = []
    for k in BIG:
        layers = sorted(l for (k_, l) in comm.stacks if k_ == k)
        stacks.append(jnp.concatenate([comm.stacks[(k, l)] for l in layers], axis=1))

    small_names = [k for k in WEIGHTS if k not in BIG]
    reduced = _unpack(all_reduce_small(_pack([G[k] for k in small_names] + [loss_part.reshape(1)]), "small_all_reduce"),
                      [G[k].shape for k in small_names] + [(1,)])
    loss = reduced[-1][0]
    small_g = dict(zip(small_names, reduced[:-1]))
    for k in SMALL_SHARDED:
        sz = w[k].shape[SHARD_AXIS[k]]
        small_g[k] = lax.dynamic_slice_in_dim(small_g[k], me * sz, sz, axis=SHARD_AXIS[k])

    res = {}
    for k, st in zip(BIG, stacks):
        tshape = _t_view(k, w[k]).shape
        wt, mt, vt = (_as2d(_t_view(k, a)) for a in (w[k], mom[k], var[k]))
        outs = adamw(wt, st, mt, vt, "adamw_" + k)
        res[k] = [_t_view(k, o.reshape(tshape)) for o in outs]
    for k in small_names:
        outs = adamw(_as2d(w[k]), _as2d(small_g[k])[None], _as2d(mom[k]), _as2d(var[k]), "adamw_" + k)
        res[k] = [o.reshape(w[k].shape) for o in outs]

    return (loss, grad_x[None], *[res[k][0] for k in WEIGHTS], *[res[k][1] for k in WEIGHTS],
            *[res[k][2] for k in WEIGHTS], *[res[k][3] for k in WEIGHTS])
```

```python
import functools
import math

import jax
import jax.numpy as jnp
from jax import lax
from jax.experimental import pallas as pl
from jax.experimental.pallas import tpu as pltpu

F32 = jnp.float32
BF16 = jnp.bfloat16
HI = lax.Precision.HIGHEST
MESH = pl.DeviceIdType.MESH

EPS = 1e-6
NEG_INF = -1e30
CHUNK = 64
LEFT_CHUNKS = 8
BAND = (LEFT_CHUNKS + 1) * CHUNK
BAND_PAD = 640
A_CONV = 4
F_CONV = 3
A_HEAD = 128
B_HEAD = 64
LANES = 128
HALO = 8
N_DEV = 8

ADAM_LR = 0.001
ADAM_B1 = 0.9
ADAM_B2 = 0.999
ADAM_EPS = 1e-08
ADAM_WD = 0.01
ADAM_STEP = 10

VMEM_LIMIT_V7X = 56 * 1024 * 1024
WGRAD_DTYPE = BF16
GDN_BWD_HEADS = 8
COL_CHUNK = 256
FFN_TILE = 256


def _cp(sem=None, vmem=VMEM_LIMIT_V7X):
    kw = dict(vmem_limit_bytes=vmem)
    if sem is not None:
        kw["dimension_semantics"] = sem
    return pltpu.CompilerParams(**kw)


def _pick(n, target, q=LANES):
    best = None
    for t in range(q, min(n, target) + 1, q):
        if n % t == 0:
            best = t
    return best if best is not None else n


def _sig(x):
    return 1.0 / (1.0 + jnp.exp(-x))


def _softplus(x):
    return jnp.maximum(x, 0.0) + jnp.log(1.0 + jnp.exp(-jnp.abs(x)))


def _rms(x, g):
    return x * lax.rsqrt(jnp.mean(x * x, axis=-1, keepdims=True) + EPS) * g


def _rms_bwd(x, g, dxn):
    r = lax.rsqrt(jnp.mean(x * x, axis=-1, keepdims=True) + EPS)
    gd = dxn * g
    dx = r * gd - x * (r * r * r) * jnp.mean(x * gd, axis=-1, keepdims=True)
    dg = jnp.sum(dxn * x * r, axis=0, keepdims=True)
    return dx, dg


def _dot(a, b):
    return jnp.dot(a, b, preferred_element_type=F32)


def _dot_nt(a, b):
    return lax.dot_general(a, b, (((1,), (1,)), ((), ())), preferred_element_type=F32)


def _dot_tn(a, b):
    return lax.dot_general(a, b, (((0,), (0,)), ((), ())), preferred_element_type=F32)


def _hdot(a, b):
    return jnp.dot(a, b, precision=HI, preferred_element_type=F32)


def _hdot_nt(a, b):
    return lax.dot_general(a, b, (((1,), (1,)), ((), ())), precision=HI, preferred_element_type=F32)


def _resident(shape, index_map):
    return pl.BlockSpec(shape, index_map, pipeline_mode=pl.Buffered(1))


def norm_matmul(h, gamma, w, name, out_dtype=F32, w_t=False):
    n, d = h.shape
    nc = w.shape[0] if w_t else w.shape[1]
    rows, cols = (1024, 2816) if out_dtype == BF16 else (512, 4224) if nc > 2048 else (1024, 1536)
    tm = _pick(n, rows, 8)
    tn = _pick(nc, cols)

    def body(h_ref, g_ref, w_ref, o_ref):
        xn = _rms(h_ref[...], g_ref[...]).astype(BF16)
        o_ref[...] = (_dot_nt(xn, w_ref[...]) if w_t else _dot(xn, w_ref[...])).astype(out_dtype)

    return pl.pallas_call(
        body, grid=(nc // tn, n // tm),
        in_specs=[pl.BlockSpec((tm, d), lambda j, i: (i, 0)),
                  pl.BlockSpec((1, d), lambda j, i: (0, 0)),
                  pl.BlockSpec((tn, d), lambda j, i: (j, 0)) if w_t else pl.BlockSpec((d, tn), lambda j, i: (0, j))],
        out_specs=pl.BlockSpec((tm, tn), lambda j, i: (i, j)),
        out_shape=jax.ShapeDtypeStruct((n, nc), out_dtype), name=name,
        compiler_params=_cp(("parallel", "parallel")))(h, gamma, w)


def norm_matmul_tn(h, gamma, dy, name, transposed=False):
    n, d = h.shape
    nc = dy.shape[1]
    tm = _pick(n, 2048 if dy.dtype == BF16 else 1024, 8)
    tn = _pick(nc, 1536)

    steps = n // tm

    def body(h_ref, g_ref, dy_ref, o_ref, acc):
        i = pl.program_id(1)

        @pl.when(i == 0)
        def _():
            acc[...] = jnp.zeros_like(acc)
        xn = _rms(h_ref[...], g_ref[...]).astype(BF16)
        dyb = dy_ref[...].astype(BF16)
        acc[...] += _dot_tn(dyb, xn) if transposed else _dot_tn(xn, dyb)

        @pl.when(i == steps - 1)
        def _():
            o_ref[...] = acc[...].astype(WGRAD_DTYPE)

    return pl.pallas_call(
        body, grid=(nc // tn, steps),
        in_specs=[pl.BlockSpec((tm, d), lambda j, i: (i, 0)),
                  pl.BlockSpec((1, d), lambda j, i: (0, 0)),
                  pl.BlockSpec((tm, tn), lambda j, i: (i, j))],
        out_specs=pl.BlockSpec((tn, d), lambda j, i: (j, 0)) if transposed else pl.BlockSpec((d, tn), lambda j, i: (0, j)),
        out_shape=jax.ShapeDtypeStruct((nc, d) if transposed else (d, nc), WGRAD_DTYPE),
        scratch_shapes=[pltpu.VMEM((tn, d) if transposed else (d, tn), F32)], name=name,
        compiler_params=_cp(("parallel", "arbitrary")))(h, gamma, dy)


def matmul_tn(a, dy, name):
    n, ka = a.shape
    nc = dy.shape[1]
    tm = _pick(n, 2048 if a.dtype == BF16 else 1024, 8)
    tk = _pick(ka, 1536)
    tn = _pick(nc, 1024)
    steps = n // tm

    def body(a_ref, dy_ref, o_ref, acc):
        i = pl.program_id(2)

        @pl.when(i == 0)
        def _():
            acc[...] = jnp.zeros_like(acc)
        acc[...] += _dot_tn(a_ref[...].astype(BF16), dy_ref[...].astype(BF16))

        @pl.when(i == steps - 1)
        def _():
            o_ref[...] = acc[...].astype(WGRAD_DTYPE)

    return pl.pallas_call(
        body, grid=(ka // tk, nc // tn, steps),
        in_specs=[pl.BlockSpec((tm, tk), lambda k, j, i: (i, k)),
                  pl.BlockSpec((tm, tn), lambda k, j, i: (i, j))],
        out_specs=pl.BlockSpec((tk, tn), lambda k, j, i: (k, j)),
        out_shape=jax.ShapeDtypeStruct((ka, nc), WGRAD_DTYPE),
        scratch_shapes=[pltpu.VMEM((tk, tn), F32)], name=name,
        compiler_params=_cp(("parallel", "parallel", "arbitrary")))(a, dy)


def matmul_res(a, w, h, name):
    n, k = a.shape
    d = w.shape[1]
    tm = _pick(n, 512, 8)

    def body(a_ref, w_ref, h_ref, o_ref):
        o_ref[...] = h_ref[...] + _dot(a_ref[...].astype(BF16), w_ref[...])

    return pl.pallas_call(
        body, grid=(n // tm,),
        in_specs=[pl.BlockSpec((tm, k), lambda i: (i, 0)),
                  _resident((k, d), lambda i: (0, 0)),
                  pl.BlockSpec((tm, d), lambda i: (i, 0))],
        out_specs=pl.BlockSpec((tm, d), lambda i: (i, 0)),
        out_shape=jax.ShapeDtypeStruct((n, d), F32), name=name,
        compiler_params=_cp(("parallel",)))(a, w, h)


def matmul_nt(dy, w, name):
    n, k = dy.shape
    d = w.shape[0]
    tm = _pick(n, 512, 8)

    def body(dy_ref, w_ref, o_ref):
        o_ref[...] = _dot_nt(dy_ref[...].astype(BF16), w_ref[...])

    return pl.pallas_call(
        body, grid=(n // tm,),
        in_specs=[pl.BlockSpec((tm, k), lambda i: (i, 0)),
                  _resident((d, k), lambda i: (0, 0))],
        out_specs=pl.BlockSpec((tm, d), lambda i: (i, 0)),
        out_shape=jax.ShapeDtypeStruct((n, d), F32), name=name,
        compiler_params=_cp(("parallel",)))(dy, w)


def dx_norm_bwd(dout, h, gamma, pieces, name, rider=None, w_t=False):
    n, d = h.shape
    tm = _pick(n, 512, 8)
    np_ = len(pieces)
    mm = _dot if w_t else _dot_nt

    def body(*refs):
        dout_ref, h_ref, g_ref = refs[:3]
        dys = refs[3:3 + np_]
        ws = refs[3 + np_:3 + 2 * np_]
        dh_ref, dg_ref = refs[3 + 2 * np_:]
        dxn = mm(dys[0][...].astype(BF16), ws[0][...])
        for p in range(1, np_):
            dxn = dxn + mm(dys[p][...].astype(BF16), ws[p][...])
        dx, dg = _rms_bwd(h_ref[...], g_ref[...], dxn)
        dh_ref[...] = dout_ref[...] + dx

        @pl.when(pl.program_id(0) == 0)
        def _():
            dg_ref[...] = jnp.zeros_like(dg_ref)
        dg_ref[...] += dg

    in_specs = [pl.BlockSpec((tm, d), lambda i: (i, 0)),
                pl.BlockSpec((tm, d), lambda i: (i, 0)),
                pl.BlockSpec((1, d), lambda i: (0, 0))]
    in_specs += [pl.BlockSpec((tm, dy.shape[1]), lambda i: (i, 0)) for dy, _ in pieces]
    in_specs += [_resident(w.shape, lambda i: (0, 0)) for _, w in pieces]
    (dh, dg), got = host_call(
        body, grid=(n // tm,), in_specs=in_specs,
        out_specs=[pl.BlockSpec((tm, d), lambda i: (i, 0)), pl.BlockSpec((1, d), lambda i: (0, 0))],
        out_shape=[jax.ShapeDtypeStruct((n, d), F32), jax.ShapeDtypeStruct((1, d), F32)], name=name,
        scratch_shapes=[], sem=("arbitrary",), rider=rider,
        args=(dout, h, gamma, *[p[0] for p in pieces], *[p[1] for p in pieces]))
    return (dh, dg) if rider is None else (dh, dg, got)


def loss_head(h, gamma, target, name="loss_head"):
    n, d = h.shape
    tm = _pick(n, 512, 8)

    def body(h_ref, g_ref, t_ref, loss_ref, dh_ref, dg_ref):
        @pl.when(pl.program_id(0) == 0)
        def _():
            loss_ref[...] = jnp.zeros_like(loss_ref)
            dg_ref[...] = jnp.zeros_like(dg_ref)
        x = h_ref[...]
        g = g_ref[...]
        e = _rms(x, g) - t_ref[...]
        part = jnp.sum(jnp.sum(e * e, axis=-1, keepdims=True), axis=0, keepdims=True) * (0.5 / d)
        loss_ref[...] += jnp.broadcast_to(part, loss_ref.shape)
        dx, dg = _rms_bwd(x, g, e * (1.0 / d))
        dh_ref[...] = dx
        dg_ref[...] += dg

    return pl.pallas_call(
        body, grid=(n // tm,),
        in_specs=[pl.BlockSpec((tm, d), lambda i: (i, 0)), pl.BlockSpec((1, d), lambda i: (0, 0)),
                  pl.BlockSpec((tm, d), lambda i: (i, 0))],
        out_specs=[pl.BlockSpec((8, LANES), lambda i: (0, 0)), pl.BlockSpec((tm, d), lambda i: (i, 0)),
                   pl.BlockSpec((1, d), lambda i: (0, 0))],
        out_shape=[jax.ShapeDtypeStruct((8, LANES), F32), jax.ShapeDtypeStruct((n, d), F32),
                   jax.ShapeDtypeStruct((1, d), F32)], name=name,
        compiler_params=_cp(("arbitrary",)))(h, gamma, target)


def _halo_rows(dtype):
    return HALO * (4 // jnp.dtype(dtype).itemsize)


def _prev_halo_map(t, hb=HALO):
    return lambda i: (jnp.maximum(i * (t // hb) - 1, 0), 0)


def _next_halo_map(t, n, hb=HALO):
    return lambda i: (jnp.minimum((i + 1) * (t // hb), n // hb - 1), 0)


def _fill_prev(xs, main_ref, halo_ref, i, cols=slice(None)):
    hb = halo_ref.shape[0]
    xs[0:HALO, :] = jnp.where(i > 0, halo_ref[hb - HALO:hb, cols].astype(F32), 0.0)
    xs[HALO:, :] = main_ref[:, cols].astype(F32)


def _causal_conv(xs, w_ref, width, t, cols=slice(None), xcols=slice(None)):
    x = xs[:, xcols]
    acc = w_ref[width - 1:width, cols] * x[HALO:, :]
    for k in range(width - 1):
        acc = acc + w_ref[k:k + 1, cols] * pltpu.roll(x, width - 1 - k, axis=0)[HALO:, :]
    return acc


def _col_chunks(width, target=COL_CHUNK):
    tc = _pick(width, target)
    return [slice(j * tc, (j + 1) * tc) for j in range(width // tc)]


def ffn_act_down(up, conv_w, conv_b, w_down, h, name):
    n, c2 = up.shape
    ff = c2 // 2
    d = h.shape[1]
    t = _pick(n, 2 * FFN_TILE, 8)
    hb = _halo_rows(up.dtype)
    chunks = _col_chunks(ff)
    tc = chunks[0].stop

    def body(up_ref, halo_ref, cw_ref, cb_ref, wd_ref, h_ref, o_ref, act_ref, hc_ref, xg, xv):
        i = pl.program_id(0)
        acc = h_ref[...]
        for cs in chunks:
            vs = slice(ff + cs.start, ff + cs.stop)
            _fill_prev(xg, up_ref, halo_ref, i, cs)
            _fill_prev(xv, up_ref, halo_ref, i, vs)
            gate = _causal_conv(xg, cw_ref, F_CONV, t, cs) + cb_ref[:, cs]
            val = _causal_conv(xv, cw_ref, F_CONV, t, vs) + cb_ref[:, vs]
            hc_ref[:, cs] = gate.astype(BF16)
            hc_ref[:, vs] = val.astype(BF16)
            act = (gate * _sig(gate) * val).astype(BF16)
            act_ref[:, cs] = act
            acc = acc + _dot(act, wd_ref[cs, :])
        o_ref[...] = acc

    return pl.pallas_call(
        body, grid=(n // t,),
        in_specs=[pl.BlockSpec((t, c2), lambda i: (i, 0)),
                  pl.BlockSpec((hb, c2), _prev_halo_map(t, hb)),
                  pl.BlockSpec((F_CONV, c2), lambda i: (0, 0)),
                  pl.BlockSpec((1, c2), lambda i: (0, 0)),
                  _resident((ff, d), lambda i: (0, 0)),
                  pl.BlockSpec((t, d), lambda i: (i, 0))],
        out_specs=[pl.BlockSpec((t, d), lambda i: (i, 0)), pl.BlockSpec((t, ff), lambda i: (i, 0)),
                   pl.BlockSpec((t, c2), lambda i: (i, 0))],
        out_shape=[jax.ShapeDtypeStruct((n, d), F32), jax.ShapeDtypeStruct((n, ff), BF16),
                   jax.ShapeDtypeStruct((n, c2), BF16)],
        scratch_shapes=[pltpu.VMEM((t + HALO, tc), F32), pltpu.VMEM((t + HALO, tc), F32)], name=name,
        compiler_params=_cp(("parallel",)))(up, up, conv_w, conv_b, w_down, h)


def ffn_bwd_act(dout, hc, w_down, name):
    n, c2 = hc.shape
    ff = c2 // 2
    d = dout.shape[1]
    t = _pick(n, 2 * FFN_TILE, 8)
    chunks = _col_chunks(ff)

    def body(dout_ref, hc_ref, wd_ref, dhc_ref, dcb_ref):
        i = pl.program_id(0)

        @pl.when(i == 0)
        def _():
            dcb_ref[...] = jnp.zeros_like(dcb_ref)
        doutb = dout_ref[...].astype(BF16)
        for cs in chunks:
            vs = slice(ff + cs.start, ff + cs.stop)
            gate = hc_ref[:, cs].astype(F32)
            val = hc_ref[:, vs].astype(F32)
            sg = _sig(gate)
            da = _dot_nt(doutb, wd_ref[cs, :])
            dgate = da * val * (sg * (1.0 + gate * (1.0 - sg)))
            dval = da * gate * sg
            dhc_ref[:, cs] = dgate.astype(BF16)
            dhc_ref[:, vs] = dval.astype(BF16)
            dcb_ref[:, cs] += jnp.sum(dgate, axis=0, keepdims=True)
            dcb_ref[:, vs] += jnp.sum(dval, axis=0, keepdims=True)

    return pl.pallas_call(
        body, grid=(n // t,),
        in_specs=[pl.BlockSpec((t, d), lambda i: (i, 0)),
                  pl.BlockSpec((t, c2), lambda i: (i, 0)),
                  _resident((ff, d), lambda i: (0, 0))],
        out_specs=[pl.BlockSpec((t, c2), lambda i: (i, 0)), pl.BlockSpec((1, c2), lambda i: (0, 0))],
        out_shape=[jax.ShapeDtypeStruct((n, c2), BF16), jax.ShapeDtypeStruct((1, c2), F32)], name=name,
        compiler_params=_cp(("arbitrary",)))(dout, hc, w_down)


def conv_bwd_tail(dy_ref, dnext_ref, x_ref, cw_ref, dcw_ref, ds, width, t, i, last, cols=slice(None)):
    ds[0:t, :] = dy_ref[:, cols].astype(F32)
    ds[t:, :] = jnp.where(i < last, dnext_ref[0:HALO, cols].astype(F32), 0.0)
    x = x_ref[:, cols].astype(F32)
    dall = ds[...]
    dx = None
    for k in range(width):
        off = width - 1 - k
        shifted = dall[0:t, :] if off == 0 else pltpu.roll(dall, t + HALO - off, axis=0)[0:t, :]
        term = cw_ref[k:k + 1, cols] * shifted
        dx = term if dx is None else dx + term
        dcw_ref[k:k + 1, cols] += jnp.sum(shifted * x, axis=0, keepdims=True)
    return dx


def ffn_bwd_up(dhc, up, conv_w, w_up, h, dout, gamma, name, rider=None):
    n, c2 = up.shape
    d = h.shape[1]
    t = _pick(n, FFN_TILE, 8)
    last = n // t - 1
    chunks = _col_chunks(c2)
    tc = chunks[0].stop

    def body(dhc_ref, dnext_ref, up_ref, cw_ref, wu_ref, h_ref, dout_ref, g_ref,
             dh_ref, dup_ref, dcw_ref, dg_ref, ds):
        i = pl.program_id(0)

        @pl.when(i == 0)
        def _():
            dcw_ref[...] = jnp.zeros_like(dcw_ref)
            dg_ref[...] = jnp.zeros_like(dg_ref)
        dxn = jnp.zeros((t, d), F32)
        for cs in chunks:
            dup = conv_bwd_tail(dhc_ref, dnext_ref, up_ref, cw_ref, dcw_ref, ds, F_CONV, t, i, last, cs)
            dupb = dup.astype(BF16)
            dup_ref[:, cs] = dupb
            dxn = dxn + _dot(dupb, wu_ref[cs, :])
        dx, dg = _rms_bwd(h_ref[...], g_ref[...], dxn)
        dh_ref[...] = dout_ref[...] + dx
        dg_ref[...] += dg

    return host_call(
        body, grid=(n // t,), rider=rider, sem=("arbitrary",), args=(dhc, dhc, up, conv_w, w_up, h, dout, gamma),
        in_specs=[pl.BlockSpec((t, c2), lambda i: (i, 0)),
                  pl.BlockSpec((_halo_rows(dhc.dtype), c2), _next_halo_map(t, n, _halo_rows(dhc.dtype))),
                  pl.BlockSpec((t, c2), lambda i: (i, 0)),
                  pl.BlockSpec((F_CONV, c2), lambda i: (0, 0)),
                  _resident((c2, d), lambda i: (0, 0)),
                  pl.BlockSpec((t, d), lambda i: (i, 0)),
                  pl.BlockSpec((t, d), lambda i: (i, 0)),
                  pl.BlockSpec((1, d), lambda i: (0, 0))],
        out_specs=[pl.BlockSpec((t, d), lambda i: (i, 0)), pl.BlockSpec((t, c2), lambda i: (i, 0)),
                   pl.BlockSpec((F_CONV, c2), lambda i: (0, 0)), pl.BlockSpec((1, d), lambda i: (0, 0))],
        out_shape=[jax.ShapeDtypeStruct((n, d), F32), jax.ShapeDtypeStruct((n, c2), BF16),
                   jax.ShapeDtypeStruct((F_CONV, c2), F32), jax.ShapeDtypeStruct((1, d), F32)],
        scratch_shapes=[pltpu.VMEM((t + HALO, tc), F32)], name=name)


def _gdn_head(uq, uk, uv, pba, alog, dtb, head, n_heads):
    lane = lax.broadcasted_iota(jnp.int32, pba.shape, 1)
    sq = uq * _sig(uq)
    q = sq * lax.rsqrt(jnp.sum(sq * sq, axis=-1, keepdims=True) + EPS) * (A_HEAD ** -0.5)
    sk = uk * _sig(uk)
    k = sk * lax.rsqrt(jnp.sum(sk * sk, axis=-1, keepdims=True) + EPS)
    v = uv * _sig(uv)
    beta = jnp.sum(jnp.where(lane == head, _sig(pba), 0.0), axis=-1, keepdims=True)
    g_all = -jnp.exp(alog) * _softplus(pba + dtb)
    g = jnp.sum(jnp.where(lane == n_heads + head, g_all, 0.0), axis=-1, keepdims=True)
    return q, k, v, jnp.broadcast_to(beta, uq.shape), jnp.broadcast_to(g, uq.shape)


def gdn_prep(proj, conv_w, alog, dtb, n_heads, name, rider=None):
    n = proj.shape[0]
    qk = n_heads * A_HEAD
    cw = 3 * qk
    ba_blk = (cw + qk) // LANES
    t = _pick(n, 256, 8)

    def body(x_ref, halo_ref, pba_ref, cw_ref, al_ref, dt_ref, q_ref, k_ref, v_ref, b_ref, g_ref, u_ref, xs):
        i = pl.program_id(0)
        xs[0:HALO, :] = jnp.where(i > 0, halo_ref[...], 0.0)
        xs[HALO:, :] = x_ref[...]
        pba = pba_ref[...]
        for hd in range(n_heads):
            s0 = slice(hd * A_HEAD, (hd + 1) * A_HEAD)
            s1 = slice(qk + hd * A_HEAD, qk + (hd + 1) * A_HEAD)
            s2 = slice(2 * qk + hd * A_HEAD, 2 * qk + (hd + 1) * A_HEAD)
            uq, uk, uv = [_causal_conv(xs, cw_ref, A_CONV, t, s, s) for s in (s0, s1, s2)]
            u_ref[:, s0] = uq.astype(BF16)
            u_ref[:, s1] = uk.astype(BF16)
            u_ref[:, s2] = uv.astype(BF16)
            q, k, v, bb, gb = _gdn_head(uq, uk, uv, pba, al_ref[...], dt_ref[...], hd, n_heads)
            q_ref[:, s0] = q
            k_ref[:, s0] = k
            v_ref[:, s0] = v
            b_ref[:, s0] = bb
            g_ref[:, s0] = gb

    out = jax.ShapeDtypeStruct((n, qk), F32)
    return host_call(
        body, grid=(n // t,),
        in_specs=[pl.BlockSpec((t, cw), lambda i: (i, 0)),
                  pl.BlockSpec((HALO, cw), _prev_halo_map(t)),
                  pl.BlockSpec((t, LANES), lambda i: (i, ba_blk)),
                  pl.BlockSpec((A_CONV, cw), lambda i: (0, 0)),
                  pl.BlockSpec((1, LANES), lambda i: (0, 0)),
                  pl.BlockSpec((1, LANES), lambda i: (0, 0))],
        out_specs=[pl.BlockSpec((t, qk), lambda i: (i, 0))] * 5 + [pl.BlockSpec((t, cw), lambda i: (i, 0))],
        out_shape=[out] * 5 + [jax.ShapeDtypeStruct((n, cw), BF16)],
        scratch_shapes=[pltpu.VMEM((t + HALO, cw), F32)], name=name,
        sem=("parallel",), args=(proj, proj, proj, conv_w, alog, dtb), rider=rider)


def gdn_prep_bwd(proj, u, alog, dtb, dq, dk, dv, dbb, dgb, n_heads, name):
    n = proj.shape[0]
    qk = n_heads * A_HEAD
    cw = 3 * qk
    ba_blk = (cw + qk) // LANES
    t = _pick(n, 256, 8)

    def body(u_ref, pba_ref, al_ref, dt_ref, dq_ref, dk_ref, dv_ref, dbb_ref, dgb_ref,
             du_ref, dba_ref, dal_ref, ddt_ref):
        i = pl.program_id(0)
        u = u_ref[...].astype(F32)
        pba = pba_ref[...]
        lane0 = lax.broadcasted_iota(jnp.int32, (t, A_HEAD), 1) == 0
        dba = jnp.zeros((t, LANES), F32)
        dal = jnp.zeros((1, LANES), F32)
        ddt = jnp.zeros((1, LANES), F32)
        for hd in range(n_heads):
            s0 = slice(hd * A_HEAD, (hd + 1) * A_HEAD)
            s1 = slice(qk + hd * A_HEAD, qk + (hd + 1) * A_HEAD)
            s2 = slice(2 * qk + hd * A_HEAD, 2 * qk + (hd + 1) * A_HEAD)
            fn = functools.partial(_gdn_head, head=hd, n_heads=n_heads)
            _, vjp = jax.vjp(fn, u[:, s0], u[:, s1], u[:, s2], pba, al_ref[...], dt_ref[...])
            cts = (dq_ref[:, s0], dk_ref[:, s0], dv_ref[:, s0],
                   jnp.where(lane0, dbb_ref[:, s0], 0.0), jnp.where(lane0, dgb_ref[:, s0], 0.0))
            duq, duk, duv, dpba, da, dd = vjp(cts)
            du_ref[:, s0] = duq
            du_ref[:, s1] = duk
            du_ref[:, s2] = duv
            dba = dba + dpba
            dal = dal + da
            ddt = ddt + dd
        dba_ref[...] = dba

        @pl.when(i == 0)
        def _():
            dal_ref[...] = jnp.zeros_like(dal_ref)
            ddt_ref[...] = jnp.zeros_like(ddt_ref)
        dal_ref[...] += dal
        ddt_ref[...] += ddt

    tok = pl.BlockSpec((t, qk), lambda i: (i, 0))
    row = pl.BlockSpec((1, LANES), lambda i: (0, 0))
    return pl.pallas_call(
        body, grid=(n // t,),
        in_specs=[pl.BlockSpec((t, cw), lambda i: (i, 0)),
                  pl.BlockSpec((t, LANES), lambda i: (i, ba_blk)), row, row,
                  tok, tok, tok, tok, tok],
        out_specs=[pl.BlockSpec((t, cw), lambda i: (i, 0)), pl.BlockSpec((t, LANES), lambda i: (i, 0)), row, row],
        out_shape=[jax.ShapeDtypeStruct((n, cw), F32), jax.ShapeDtypeStruct((n, LANES), F32),
                   jax.ShapeDtypeStruct((1, LANES), F32), jax.ShapeDtypeStruct((1, LANES), F32)],
        name=name, compiler_params=_cp(("arbitrary",)))(u, proj, alog, dtb, dq, dk, dv, dbb, dgb)


def conv_bwd(du, x, conv_w, width, name, rider=None):
    n, cw = du.shape
    t = _pick(n, 256, 8)
    last = n // t - 1

    chunks = _col_chunks(cw, LANES)
    tc = chunks[0].stop

    def body(du_ref, dnext_ref, x_ref, cw_ref, dx_ref, dcw_ref, ds):
        i = pl.program_id(0)

        @pl.when(i == 0)
        def _():
            dcw_ref[...] = jnp.zeros_like(dcw_ref)
        for cs in chunks:
            dx_ref[:, cs] = conv_bwd_tail(du_ref, dnext_ref, x_ref, cw_ref, dcw_ref, ds, width, t, i, last, cs)

    return host_call(
        body, grid=(n // t,),
        in_specs=[pl.BlockSpec((t, cw), lambda i: (i, 0)),
                  pl.BlockSpec((HALO, cw), _next_halo_map(t, n)),
                  pl.BlockSpec((t, cw), lambda i: (i, 0)),
                  pl.BlockSpec((width, cw), lambda i: (0, 0))],
        out_specs=[pl.BlockSpec((t, cw), lambda i: (i, 0)), pl.BlockSpec((width, cw), lambda i: (0, 0))],
        out_shape=[jax.ShapeDtypeStruct((n, cw), F32), jax.ShapeDtypeStruct((width, cw), F32)],
        scratch_shapes=[pltpu.VMEM((t + HALO, tc), F32)], name=name,
        sem=("arbitrary",), args=(du, du, x, conv_w), rider=rider)


def _b(x):
    return x.astype(BF16)


def _mm_nn(a, b):
    return _dot(_b(a), _b(b))


def _mm_nt(a, b):
    return _dot_nt(_b(a), _b(b))


def _mm_tn(a, b):
    return _dot_tn(_b(a), _b(b))


@jax.custom_vjp
def _mmg_nn(a, b):
    return _mm_nn(a, b)


_mmg_nn.defvjp(lambda a, b: (_mm_nn(a, b), (a, b)),
               lambda res, dc: (_mm_nt(dc, res[1]), _mm_tn(res[0], dc)))


@jax.custom_vjp
def _mmg_nt(a, b):
    return _mm_nt(a, b)


_mmg_nt.defvjp(lambda a, b: (_mm_nt(a, b), (a, b)),
               lambda res, dc: (_mm_nn(dc, res[1]), _mm_tn(dc, res[0])))


@jax.custom_vjp
def _mmg_tn(a, b):
    return _mm_tn(a, b)


_mmg_tn.defvjp(lambda a, b: (_mm_tn(a, b), (a, b)),
               lambda res, dc: (_mm_nt(res[1], dc), _mm_nn(res[0], dc)))


def _bf16_parts(x, n):
    parts = []
    for _ in range(n):
        p = x.astype(BF16)
        parts.append(p)
        x = x - p.astype(F32)
    return parts


def _dot_f32ish(a, b):
    (ah, al), (bh, bl) = _bf16_parts(a, 2), _bf16_parts(b, 2)
    return _dot(ah, bh) + _dot(ah, bl) + _dot(al, bh)


def _tri_dot(x, transpose):
    c = x.shape[0]
    low = lax.broadcasted_iota(jnp.int32, (c, c), 0) >= lax.broadcasted_iota(jnp.int32, (c, c), 1)
    tri = jnp.where(low, 1.0, 0.0).astype(BF16)
    mm = _dot_tn if transpose else _dot
    return functools.reduce(lambda a, b: a + b, [mm(tri, p) for p in _bf16_parts(x, 3)])


def _cumsum(x):
    return _tri_dot(x, False)


@jax.custom_vjp
def _cumsum_g(x):
    return _tri_dot(x, False)


_cumsum_g.defvjp(lambda x: (_tri_dot(x, False), None), lambda _, ct: (_tri_dot(ct, True),))


def _each(f, *lists):
    return [f(*a) for a in zip(*lists)]


def _unit_lower_inv(ms):
    c = ms[0].shape[0]
    eye = jnp.where(lax.broadcasted_iota(jnp.int32, (c, c), 0) == lax.broadcasted_iota(jnp.int32, (c, c), 1), 1.0, 0.0)
    xs = [eye - m for m in ms]
    pws = _each(_mm_nn, ms, ms)
    for it in range(5):
        xs = _each(lambda x, pw: x + _mm_nn(x, pw), xs, pws)
        if it < 4:
            pws = _each(_mm_nn, pws, pws)
    rs = _each(lambda m, x: eye - x - _dot_f32ish(m, x), ms, xs)
    return _each(lambda x, r: x + _mm_nn(x, r), xs, rs)


@jax.custom_vjp
def _saved_inv_g(ms, xs):
    return xs


_saved_inv_g.defvjp(lambda ms, xs: (xs, xs),
                    lambda xs, dxs: (_each(lambda t, x: -_mm_nt(t, x), _each(_mm_tn, xs, dxs), xs),
                                     [jnp.zeros_like(x) for x in xs]))


def _gdn_chunk(ops, state, q, k, v, bb, gb):
    nn, nt, tn, inv, cum = ops
    c = CHUNK
    ri = lax.broadcasted_iota(jnp.int32, (c, c), 0)
    ci = lax.broadcasted_iota(jnp.int32, (c, c), 1)
    causal = ri >= ci
    strict = ri > ci
    gc = [cum(g) for g in gb]
    decay = [jnp.where(causal, jnp.exp(jnp.where(causal, x[:, :c] - x.T[:c, :], 0.0)), 0.0) for x in gc]
    kb = _each(lambda a, b: a * b, k, bb)
    kk = _each(nt, kb, k)
    m = _each(lambda a, d: jnp.where(strict, a * d, 0.0), kk, decay)
    tinv = inv(m)
    egc = [jnp.exp(x) for x in gc]
    u = _each(nn, tinv, _each(lambda a, b: a * b, v, bb))
    w = _each(nn, tinv, _each(lambda a, b: a * b, kb, egc))
    attn = _each(lambda a, d: a * d, _each(nt, q, k), decay)
    glast = [jnp.sum(g, axis=0, keepdims=True) for g in gb]
    ws = _each(nn, w, state)
    v_new = _each(lambda a, b: a - b, u, ws)
    qs = _each(nn, _each(lambda a, b: a * b, q, egc), state)
    av = _each(nn, attn, v_new)
    o = _each(lambda a, b: a + b, qs, av)
    kv = _each(tn, _each(lambda a, gl, x: a * jnp.exp(gl - x), k, glast, gc), v_new)
    new_state = _each(lambda s, gl, a: s * jnp.exp(gl) + a, state, glast, kv)
    return o, new_state


def gdn_fwd(q, k, v, bb, gb, n_heads, name, rider=None):
    n, w = q.shape
    nc = n // CHUNK
    cb = min(8, nc)
    rows = cb * CHUNK

    def body(q_ref, k_ref, v_ref, b_ref, g_ref, o_ref, st_ref, ti_ref, s_scr):
        @pl.when(pl.program_id(0) == 0)
        def _():
            s_scr[...] = jnp.zeros_like(s_scr)

        def step(c, carry):
            sl = pl.ds(pl.multiple_of(c * CHUNK, CHUNK), CHUNK)
            lanes = [slice(hd * A_HEAD, (hd + 1) * A_HEAD) for hd in range(n_heads)]
            state = [s_scr[hd] for hd in range(n_heads)]
            inverses = []

            def inv(ms):
                inverses.extend(_unit_lower_inv(ms))
                return inverses

            o, new_state = _gdn_chunk((_mm_nn, _mm_nt, _mm_tn, inv, _cumsum), state,
                                      *[[r[sl, ls] for ls in lanes] for r in (q_ref, k_ref, v_ref, b_ref, g_ref)])
            for hd, ls in enumerate(lanes):
                st_ref[hd, pl.ds(c, 1)] = state[hd][None]
                ti_ref[hd, pl.ds(c, 1)] = inverses[hd].astype(BF16)[None]
                o_ref[sl, ls] = o[hd]
                s_scr[hd] = new_state[hd]
            return carry

        lax.fori_loop(0, cb, step, 0)

    tok = pl.BlockSpec((rows, w), lambda j: (j, 0))
    return host_call(
        body, grid=(nc // cb,),
        in_specs=[tok] * 5,
        out_specs=[tok, pl.BlockSpec((n_heads, cb, A_HEAD, A_HEAD), lambda j: (0, j, 0, 0)),
                   pl.BlockSpec((n_heads, cb, CHUNK, CHUNK), lambda j: (0, j, 0, 0))],
        out_shape=[jax.ShapeDtypeStruct(q.shape, F32), jax.ShapeDtypeStruct((n_heads, nc, A_HEAD, A_HEAD), F32),
                   jax.ShapeDtypeStruct((n_heads, nc, CHUNK, CHUNK), BF16)],
        scratch_shapes=[pltpu.VMEM((n_heads, A_HEAD, A_HEAD), F32)], name=name,
        sem=("arbitrary",), args=(q, k, v, bb, gb), rider=rider)


def gdn_bwd(q, k, v, bb, gb, states, tinv, do, n_heads, name, rider=None):
    n, w = q.shape
    nc = n // CHUNK
    cb = min(4, nc)
    rows = cb * CHUNK
    nblk = nc // cb

    def body(q_ref, k_ref, v_ref, b_ref, g_ref, st_ref, ti_ref, do_ref,
             dq_ref, dk_ref, dv_ref, db_ref, dg_ref, ds_scr):
        @pl.when(pl.program_id(0) == 0)
        def _():
            ds_scr[...] = jnp.zeros_like(ds_scr)

        def step(s, carry):
            c = cb - 1 - s
            sl = pl.ds(pl.multiple_of(c * CHUNK, CHUNK), CHUNK)
            for h0 in range(0, n_heads, GDN_BWD_HEADS):
                heads = list(range(h0, min(h0 + GDN_BWD_HEADS, n_heads)))
                lanes = [slice(hd * A_HEAD, (hd + 1) * A_HEAD) for hd in heads]
                state = [st_ref[hd, pl.ds(c, 1)][0] for hd in heads]
                saved = [ti_ref[hd, pl.ds(c, 1)][0].astype(F32) for hd in heads]
                chunk_fn = functools.partial(
                    _gdn_chunk, (_mmg_nn, _mmg_nt, _mmg_tn, lambda ms: _saved_inv_g(ms, saved), _cumsum_g))
                _, vjp = jax.vjp(chunk_fn, state, *[[r[sl, ls] for ls in lanes]
                                                    for r in (q_ref, k_ref, v_ref, b_ref, g_ref)])
                dstate, dq, dk, dv, dbb, dgb = vjp(([do_ref[sl, ls] for ls in lanes], [ds_scr[hd] for hd in heads]))
                for u, (hd, ls) in enumerate(zip(heads, lanes)):
                    ds_scr[hd] = dstate[u]
                    dq_ref[sl, ls] = dq[u]
                    dk_ref[sl, ls] = dk[u]
                    dv_ref[sl, ls] = dv[u]
                    db_ref[sl, ls] = jnp.broadcast_to(jnp.sum(dbb[u], axis=-1, keepdims=True), dbb[u].shape)
                    dg_ref[sl, ls] = jnp.broadcast_to(jnp.sum(dgb[u], axis=-1, keepdims=True), dgb[u].shape)
            return carry

        lax.fori_loop(0, cb, step, 0)

    tok = pl.BlockSpec((rows, w), lambda j: (nblk - 1 - j, 0))
    out = jax.ShapeDtypeStruct(q.shape, F32)
    return host_call(
        body, grid=(nblk,),
        in_specs=[tok] * 5 + [pl.BlockSpec((n_heads, cb, A_HEAD, A_HEAD), lambda j: (0, nblk - 1 - j, 0, 0)),
                              pl.BlockSpec((n_heads, cb, CHUNK, CHUNK), lambda j: (0, nblk - 1 - j, 0, 0)), tok],
        out_specs=[tok] * 5, out_shape=[out] * 5,
        scratch_shapes=[pltpu.VMEM((n_heads, A_HEAD, A_HEAD), F32)], name=name,
        sem=("arbitrary",), args=(q, k, v, bb, gb, states, tinv, do), rider=rider)


def _gdn_gate(oh, zh, w):
    r = lax.rsqrt(jnp.mean(oh * oh, axis=-1, keepdims=True) + EPS)
    return oh * r * w * (zh * _sig(zh))


def gdn_out(o, proj, out_norm, w_out, h, n_heads, name):
    n, vw = o.shape
    d = h.shape[1]
    z_blk = 3 * vw // vw
    t = _pick(n, 512, 8)

    def body(o_ref, z_ref, w_ref, wo_ref, h_ref, out_ref, y_ref):
        for hd in range(n_heads):
            s0 = slice(hd * A_HEAD, (hd + 1) * A_HEAD)
            y_ref[:, s0] = _gdn_gate(o_ref[:, s0], z_ref[:, s0], w_ref[...]).astype(BF16)
        out_ref[...] = h_ref[...] + _dot(y_ref[...], wo_ref[...])

    return pl.pallas_call(
        body, grid=(n // t,),
        in_specs=[pl.BlockSpec((t, vw), lambda i: (i, 0)),
                  pl.BlockSpec((t, vw), lambda i: (i, z_blk)),
                  pl.BlockSpec((1, A_HEAD), lambda i: (0, 0)),
                  _resident((vw, d), lambda i: (0, 0)),
                  pl.BlockSpec((t, d), lambda i: (i, 0))],
        out_specs=[pl.BlockSpec((t, d), lambda i: (i, 0)), pl.BlockSpec((t, vw), lambda i: (i, 0))],
        out_shape=[jax.ShapeDtypeStruct((n, d), F32), jax.ShapeDtypeStruct((n, vw), BF16)], name=name,
        compiler_params=_cp(("parallel",)))(o, proj, out_norm, w_out, h)


def gdn_out_bwd(dout, o, proj, out_norm, w_out, n_heads, name):
    n, vw = o.shape
    d = dout.shape[1]
    z_blk = 3
    t = _pick(n, 512, 8)

    def body(dout_ref, o_ref, z_ref, w_ref, wo_ref, do_ref, dz_ref, dw_ref):
        dy = _dot_nt(dout_ref[...].astype(BF16), wo_ref[...])
        dw = jnp.zeros((1, A_HEAD), F32)
        for hd in range(n_heads):
            s0 = slice(hd * A_HEAD, (hd + 1) * A_HEAD)
            _, vjp = jax.vjp(_gdn_gate, o_ref[:, s0], z_ref[:, s0], w_ref[...])
            doh, dzh, dwh = vjp(dy[:, s0])
            do_ref[:, s0] = doh
            dz_ref[:, s0] = dzh
            dw = dw + dwh

        @pl.when(pl.program_id(0) == 0)
        def _():
            dw_ref[...] = jnp.zeros_like(dw_ref)
        dw_ref[...] += dw

    tok = pl.BlockSpec((t, vw), lambda i: (i, 0))
    return pl.pallas_call(
        body, grid=(n // t,),
        in_specs=[pl.BlockSpec((t, d), lambda i: (i, 0)), tok,
                  pl.BlockSpec((t, vw), lambda i: (i, z_blk)),
                  pl.BlockSpec((1, A_HEAD), lambda i: (0, 0)),
                  _resident((vw, d), lambda i: (0, 0))],
        out_specs=[tok, tok, pl.BlockSpec((1, A_HEAD), lambda i: (0, 0))],
        out_shape=[jax.ShapeDtypeStruct((n, vw), F32), jax.ShapeDtypeStruct((n, vw), F32),
                   jax.ShapeDtypeStruct((1, A_HEAD), F32)], name=name,
        compiler_params=_cp(("arbitrary",)))(dout, o, proj, out_norm, w_out)


BIAS_LINE = 768
BIAS_TOP = BAND + CHUNK - 2


def _bias_line_onehot(clip, tbl_pad):
    r = lax.broadcasted_iota(jnp.int32, (tbl_pad, BIAS_LINE), 0)
    v = lax.broadcasted_iota(jnp.int32, (tbl_pad, BIAS_LINE), 1)
    idx = jnp.clip(BIAS_TOP - v - (CHUNK - 1), -clip, clip) + clip
    return jnp.where((r == idx) & (v <= BIAS_TOP), 1.0, 0.0)


def bias_expand(tbl, clip, name):
    nh, tp = tbl.shape

    def body(t_ref, o_ref):
        line = _hdot(t_ref[...], _bias_line_onehot(clip, tp))
        keep = lax.broadcasted_iota(jnp.int32, (nh, BAND_PAD), 1) < BAND
        for i in range(CHUNK):
            s = CHUNK - 1 - i
            rolled = line if s == 0 else pltpu.roll(line, BIAS_LINE - s, axis=1)
            o_ref[i] = jnp.where(keep, rolled[:, :BAND_PAD], NEG_INF)

    return pl.pallas_call(
        body, in_specs=[pl.BlockSpec(memory_space=pltpu.VMEM)], out_specs=pl.BlockSpec(memory_space=pltpu.VMEM),
        out_shape=jax.ShapeDtypeStruct((CHUNK, nh, BAND_PAD), F32), name=name, compiler_params=_cp())(tbl)


def bias_expand_bwd(dbias, clip, tp, name):
    _, nh, _ = dbias.shape

    def body(d_ref, o_ref):
        keep = lax.broadcasted_iota(jnp.int32, (nh, BAND_PAD), 1) < BAND
        pad = jnp.zeros((nh, BIAS_LINE - BAND_PAD), F32)
        acc = jnp.zeros((nh, BIAS_LINE), F32)
        for i in range(CHUNK):
            s = CHUNK - 1 - i
            d = jnp.concatenate([jnp.where(keep, d_ref[i], 0.0), pad], axis=1)
            acc = acc + (d if s == 0 else pltpu.roll(d, s, axis=1))
        o_ref[...] = _hdot_nt(acc, _bias_line_onehot(clip, tp))

    return pl.pallas_call(
        body, in_specs=[pl.BlockSpec(memory_space=pltpu.VMEM)], out_specs=pl.BlockSpec(memory_space=pltpu.VMEM),
        out_shape=jax.ShapeDtypeStruct((nh, tp), F32), name=name, compiler_params=_cp())(dbias)


ATT_TILE = LEFT_CHUNKS * CHUNK


ATT_GROUP = 8


def _att_softmax(s, bias, n_chunk):
    slot = lax.broadcasted_iota(jnp.int32, (1, s.shape[1]), 1)
    before_start = jnp.where(slot < (LEFT_CHUNKS - n_chunk) * CHUNK, NEG_INF, 0.0)
    s = s + bias + before_start
    p = jnp.exp(s - jnp.max(s, axis=-1, keepdims=True))
    return p / jnp.sum(p, axis=-1, keepdims=True)


def _att_specs(n_pairs):
    prev = lambda p, i: (jnp.maximum(i - 1, 0), p)
    cur = lambda p, i: (i, p)
    prev_v = lambda p, i: (jnp.maximum(i - 1, 0), n_pairs + p)
    cur_v = lambda p, i: (i, n_pairs + p)
    blk = (ATT_TILE, LANES)
    return [pl.BlockSpec(blk, prev), pl.BlockSpec(blk, cur), pl.BlockSpec(blk, prev_v), pl.BlockSpec(blk, cur_v)]


def _att_fill(kbuf, vbuf, kp_ref, kc_ref, vp_ref, vc_ref):
    t = ATT_TILE
    kbuf[0:t, :] = kp_ref[...].astype(BF16)
    kbuf[t:2 * t, :] = kc_ref[...].astype(BF16)
    kbuf[2 * t:, :] = jnp.zeros((CHUNK, LANES), BF16)
    vbuf[0:t, :] = vp_ref[...].astype(BF16)
    vbuf[t:2 * t, :] = vc_ref[...].astype(BF16)
    vbuf[2 * t:, :] = jnp.zeros((CHUNK, LANES), BF16)


def _stack_heads(x, first):
    return jnp.concatenate([jnp.where(first, x, 0.0), jnp.where(first, 0.0, x)], axis=0).astype(BF16)


def attn_fwd(qp, kv, bias, name, rider=None):
    n, bw = qp.shape
    n_pairs = bw // LANES
    t = ATT_TILE
    cpt = t // CHUNK

    def body(q_ref, kp_ref, kc_ref, vp_ref, vc_ref, b_ref, o_ref, kbuf, vbuf):
        i = pl.program_id(1)
        _att_fill(kbuf, vbuf, kp_ref, kc_ref, vp_ref, vc_ref)
        lane = lax.broadcasted_iota(jnp.int32, (CHUNK, LANES), 1)
        first = lane < B_HEAD
        for g0 in range(0, cpt, ATT_GROUP):
            chunks = list(range(g0, min(g0 + ATT_GROUP, cpt)))
            band = [slice(c * CHUNK, c * CHUNK + BAND_PAD) for c in chunks]
            q2 = [_stack_heads(q_ref[c * CHUNK:(c + 1) * CHUNK, :] * (B_HEAD ** -0.5), first) for c in chunks]
            s = [_dot_nt(q_u, kbuf[b_u, :]) for q_u, b_u in zip(q2, band)]
            p = [_att_softmax(s_u, b_ref[0], i * cpt + c) for s_u, c in zip(s, chunks)]
            o = [_dot(p_u.astype(BF16), vbuf[b_u, :]) for p_u, b_u in zip(p, band)]
            for o_u, c in zip(o, chunks):
                o_ref[c * CHUNK:(c + 1) * CHUNK, :] = jnp.where(first, o_u[:CHUNK], o_u[CHUNK:])

    return host_call(
        body, grid=(n_pairs, n // t),
        in_specs=[pl.BlockSpec((t, LANES), lambda p, i: (i, p))] + _att_specs(n_pairs)
        + [pl.BlockSpec((1, 2 * CHUNK, BAND_PAD), lambda p, i: (p, 0, 0))],
        out_specs=[pl.BlockSpec((t, LANES), lambda p, i: (i, p))],
        out_shape=[jax.ShapeDtypeStruct((n, bw), F32)],
        scratch_shapes=[pltpu.VMEM((2 * t + CHUNK, LANES), BF16), pltpu.VMEM((2 * t + CHUNK, LANES), BF16)],
        name=name, sem=("parallel", "parallel"), args=(qp, kv, kv, kv, kv, bias), rider=rider)


def attn_bwd(qp, kv, bias, d_o, dk_in, dv_in, name, rider=None):
    n, bw = qp.shape
    n_pairs = bw // LANES
    t = ATT_TILE
    cpt = t // CHUNK
    nt = n // t
    have_in = dk_in is not None
    scale = B_HEAD ** -0.5

    def body(*refs):
        q_ref, kp_ref, kc_ref, vp_ref, vc_ref, b_ref, do_ref = refs[:7]
        pos = 7
        if have_in:
            dki_ref, dvi_ref = refs[7:9]
            pos = 9
        dq_ref, dk_ref, dv_ref, db_ref, kbuf, vbuf, dkacc, dvacc = refs[pos:]
        j = pl.program_id(1)
        i = nt - 1 - j
        _att_fill(kbuf, vbuf, kp_ref, kc_ref, vp_ref, vc_ref)

        @pl.when(j == 0)
        def _():
            dkacc[...] = jnp.zeros_like(dkacc)
            dvacc[...] = jnp.zeros_like(dvacc)
            db_ref[...] = jnp.zeros_like(db_ref)

        @pl.when(j > 0)
        def _():
            dkacc[t:2 * t, :] = dkacc[0:t, :]
            dvacc[t:2 * t, :] = dvacc[0:t, :]
            dkacc[0:t, :] = jnp.zeros((t, LANES), F32)
            dvacc[0:t, :] = jnp.zeros((t, LANES), F32)

        lane = lax.broadcasted_iota(jnp.int32, (CHUNK, LANES), 1)
        first = lane < B_HEAD
        for g0 in range(0, cpt, ATT_GROUP):
            chunks = list(range(g0, min(g0 + ATT_GROUP, cpt)))
            rows = [slice(c * CHUNK, (c + 1) * CHUNK) for c in chunks]
            band = [slice(c * CHUNK, c * CHUNK + BAND_PAD) for c in chunks]
            q2 = [_stack_heads(q_ref[r, :] * scale, first) for r in rows]
            do2 = [_stack_heads(do_ref[r, :], first) for r in rows]
            s = [_dot_nt(q_u, kbuf[b_u, :]) for q_u, b_u in zip(q2, band)]
            dp = [_dot_nt(d_u, vbuf[b_u, :]) for d_u, b_u in zip(do2, band)]
            p = [_att_softmax(s_u, b_ref[0], i * cpt + c) for s_u, c in zip(s, chunks)]
            ds = [p_u * (dp_u - jnp.sum(dp_u * p_u, axis=-1, keepdims=True)) for p_u, dp_u in zip(p, dp)]
            dsb = [d_u.astype(BF16) for d_u in ds]
            dv = [_dot_tn(p_u.astype(BF16), d_u) for p_u, d_u in zip(p, do2)]
            dq = [_dot(d_u, kbuf[b_u, :]) * scale for d_u, b_u in zip(dsb, band)]
            dk = [_dot_tn(d_u, q_u) for d_u, q_u in zip(dsb, q2)]
            db_ref[0] += functools.reduce(lambda a, b: a + b, ds)
            for r in range(chunks[0], chunks[-1] + BAND // CHUNK):
                terms = [(u, r - c) for u, c in enumerate(chunks) if 0 <= r - c < BAND // CHUNK]
                blk = slice(r * CHUNK, (r + 1) * CHUNK)
                dvacc[blk, :] += functools.reduce(lambda a, b: a + b, [dv[u][o * CHUNK:(o + 1) * CHUNK] for u, o in terms])
                dkacc[blk, :] += functools.reduce(lambda a, b: a + b, [dk[u][o * CHUNK:(o + 1) * CHUNK] for u, o in terms])
            for u in range(len(chunks)):
                dq_ref[rows[u], :] = jnp.where(first, dq[u][:CHUNK], dq[u][CHUNK:])

        if have_in:
            dk_ref[...] = dkacc[t:2 * t, :] + dki_ref[...]
            dv_ref[...] = dvacc[t:2 * t, :] + dvi_ref[...]
        else:
            dk_ref[...] = dkacc[t:2 * t, :]
            dv_ref[...] = dvacc[t:2 * t, :]

    rev = lambda p, j: (nt - 1 - j, p)
    tok = pl.BlockSpec((t, LANES), rev)
    kv_specs = [pl.BlockSpec((t, LANES), lambda p, j: (jnp.maximum(nt - 2 - j, 0), p)),
                pl.BlockSpec((t, LANES), rev),
                pl.BlockSpec((t, LANES), lambda p, j: (jnp.maximum(nt - 2 - j, 0), n_pairs + p)),
                pl.BlockSpec((t, LANES), lambda p, j: (nt - 1 - j, n_pairs + p))]
    in_specs = [tok] + kv_specs + [pl.BlockSpec((1, 2 * CHUNK, BAND_PAD), lambda p, j: (p, 0, 0)), tok]
    args = [qp, kv, kv, kv, kv, bias, d_o]
    if have_in:
        in_specs += [tok, tok]
        args += [dk_in, dv_in]
    out = jax.ShapeDtypeStruct((n, bw), F32)
    return host_call(
        body, grid=(n_pairs, nt), in_specs=in_specs,
        out_specs=[tok, tok, tok, pl.BlockSpec((1, 2 * CHUNK, BAND_PAD), lambda p, j: (p, 0, 0))],
        out_shape=[out, out, out, jax.ShapeDtypeStruct((n_pairs, 2 * CHUNK, BAND_PAD), F32)],
        scratch_shapes=[pltpu.VMEM((2 * t + CHUNK, LANES), BF16), pltpu.VMEM((2 * t + CHUNK, LANES), BF16),
                        pltpu.VMEM((2 * t + CHUNK, LANES), F32), pltpu.VMEM((2 * t + CHUNK, LANES), F32)],
        name=name, sem=("parallel", "arbitrary"), args=args, rider=rider)


def adamw(w, gstack, m, v, name):
    r, c = w.shape
    s = gstack.shape[0]
    tr = _pick(r, 512, 8)

    def body(w_ref, g_ref, m_ref, v_ref, go_ref, d_ref, mo_ref, vo_ref):
        g = g_ref[0].astype(F32)
        for k in range(1, s):
            g = g + g_ref[k].astype(F32)
        mn = ADAM_B1 * m_ref[...] + (1.0 - ADAM_B1) * g
        vn = ADAM_B2 * v_ref[...] + (1.0 - ADAM_B2) * (g * g)
        m_hat = mn / (1.0 - ADAM_B1 ** ADAM_STEP)
        v_hat = vn / (1.0 - ADAM_B2 ** ADAM_STEP)
        go_ref[...] = g
        d_ref[...] = -ADAM_LR * (m_hat / (jnp.sqrt(v_hat) + ADAM_EPS) + ADAM_WD * w_ref[...])
        mo_ref[...] = mn
        vo_ref[...] = vn

    blk = pl.BlockSpec((tr, c), lambda i: (i, 0))
    out = jax.ShapeDtypeStruct((r, c), F32)
    return pl.pallas_call(
        body, grid=(r // tr,),
        in_specs=[blk, pl.BlockSpec((s, tr, c), lambda i: (0, i, 0)), blk, blk],
        out_specs=[blk] * 4, out_shape=[out] * 4, name=name,
        compiler_params=_cp(("parallel",)))(w, gstack, m, v)


def _place():
    x, y, c = lax.axis_index("x"), lax.axis_index("y"), lax.axis_index("c")
    chips = [(1 - x, y), (x, 1 - y), (1 - x, 1 - y)]
    return x, y, c, chips


def _ag_copy(outs, send_sems, recv_sems, t, k, block, to, src=None):
    def slot(dev):
        return outs[t].at[4 * dev[0] + 2 * dev[1] + dev[2]]
    return pltpu.make_async_remote_copy(
        src_ref=slot(block) if src is None else src, dst_ref=slot(block),
        send_sem=send_sems.at[7 * t + k], recv_sem=recv_sems.at[7 * t + k], device_id=to, device_id_type=MESH)


def _ag_start(ins, outs, send_sems, recv_sems, local_sems):
    x, y, c, chips = _place()
    me = (x, y, c)
    for t in range(len(ins)):
        pltpu.make_async_copy(ins[t], outs[t].at[4 * x + 2 * y + c], local_sems.at[t]).start()
        _ag_copy(outs, send_sems, recv_sems, t, 0, me, (x, y, 1 - c), src=ins[t]).start()
        for j, chip in enumerate(chips):
            _ag_copy(outs, send_sems, recv_sems, t, 1 + j, me, (*chip, c), src=ins[t]).start()


def _ag_finish(ins, outs, send_sems, recv_sems, local_sems):
    x, y, c, chips = _place()
    me, sibling = (x, y, c), (x, y, 1 - c)
    nt = len(ins)
    for t in range(nt):
        for j, chip in enumerate(chips):
            _ag_copy(outs, send_sems, recv_sems, t, 1 + j, (*chip, c), me).wait_recv()
            _ag_copy(outs, send_sems, recv_sems, t, 4 + j, (*chip, c), sibling).start()
    for t in range(nt):
        _ag_copy(outs, send_sems, recv_sems, t, 0, sibling, me).wait_recv()
        for j, chip in enumerate(chips):
            _ag_copy(outs, send_sems, recv_sems, t, 4 + j, (*chip, 1 - c), me).wait_recv()
    for t in range(nt):
        _ag_copy(outs, send_sems, recv_sems, t, 0, me, sibling, src=ins[t]).wait_send()
        for j, chip in enumerate(chips):
            _ag_copy(outs, send_sems, recv_sems, t, 1 + j, me, (*chip, c), src=ins[t]).wait_send()
            _ag_copy(outs, send_sems, recv_sems, t, 4 + j, (*chip, c), sibling).wait_send()
        pltpu.make_async_copy(ins[t], outs[t].at[4 * x + 2 * y + c], local_sems.at[t]).wait()


def _rs_a_copy(ins, outs, send_sems, recv_sems, t, q):
    x, y, c, _ = _place()
    return pltpu.make_async_remote_copy(
        src_ref=ins[t].at[2 * q + (1 - c)], dst_ref=outs[t].at[q],
        send_sem=send_sems.at[4 * t + q], recv_sem=recv_sems.at[4 * t + q],
        device_id=(x, y, 1 - c), device_id_type=MESH)


def _rs_a_start(ins, outs, send_sems, recv_sems, local_sems):
    for t in range(len(ins)):
        for q in range(4):
            _rs_a_copy(ins, outs, send_sems, recv_sems, t, q).start()


def _rs_a_finish(ins, outs, send_sems, recv_sems, local_sems):
    for t in range(len(ins)):
        for q in range(4):
            _rs_a_copy(ins, outs, send_sems, recv_sems, t, q).wait_recv()
    for t in range(len(ins)):
        for q in range(4):
            _rs_a_copy(ins, outs, send_sems, recv_sems, t, q).wait_send()


def _rs_b_copy(ins, outs, send_sems, recv_sems, t, j, sending):
    x, y, c, chips = _place()
    mine, other = 2 * x + y, 2 * chips[j][0] + chips[j][1]
    return pltpu.make_async_remote_copy(
        src_ref=ins[t].at[other if sending else mine], dst_ref=outs[t].at[mine if sending else other],
        send_sem=send_sems.at[3 * t + j], recv_sem=recv_sems.at[3 * t + j],
        device_id=(*chips[j], c), device_id_type=MESH)


def _rs_b_start(ins, outs, send_sems, recv_sems, local_sems):
    x, y, _, _ = _place()
    for t in range(len(ins)):
        for j in range(3):
            _rs_b_copy(ins, outs, send_sems, recv_sems, t, j, True).start()
        pltpu.make_async_copy(ins[t].at[2 * x + y], outs[t].at[2 * x + y], local_sems.at[t]).start()


def _rs_b_finish(ins, outs, send_sems, recv_sems, local_sems):
    x, y, _, _ = _place()
    for t in range(len(ins)):
        for j in range(3):
            _rs_b_copy(ins, outs, send_sems, recv_sems, t, j, False).wait_recv()
    for t in range(len(ins)):
        for j in range(3):
            _rs_b_copy(ins, outs, send_sems, recv_sems, t, j, True).wait_send()
        pltpu.make_async_copy(ins[t].at[2 * x + y], outs[t].at[2 * x + y], local_sems.at[t]).wait()


_EXCHANGES = {
    "all_gather": (7, lambda a: (N_DEV, *a.shape), _ag_start, _ag_finish),
    "rs_sibling": (4, lambda a: (4, *a.shape[1:]), _rs_a_start, _rs_a_finish),
    "rs_chips": (3, lambda a: a.shape, _rs_b_start, _rs_b_finish),
}


def _exchange_parts(kind, arrays):
    per, shape_of, start, finish = _EXCHANGES[kind]
    n = len(arrays)
    out_shape = [jax.ShapeDtypeStruct(shape_of(a), a.dtype) for a in arrays]
    sems = [pltpu.SemaphoreType.DMA((per * n,)), pltpu.SemaphoreType.DMA((per * n,)), pltpu.SemaphoreType.DMA((n,))]
    return out_shape, sems, start, finish


def exchange(kind, arrays, name):
    n = len(arrays)
    out_shape, sems, start, finish = _exchange_parts(kind, arrays)
    any_spec = pl.BlockSpec(memory_space=pl.ANY)

    def body(*refs):
        ins, outs, sem_refs = refs[:n], refs[n:2 * n], refs[2 * n:]
        start(ins, outs, *sem_refs)
        finish(ins, outs, *sem_refs)

    return pl.pallas_call(body, in_specs=[any_spec] * n, out_specs=[any_spec] * n, out_shape=out_shape,
                          scratch_shapes=sems, name=name)(*arrays)


def host_call(body, *, grid, in_specs, out_specs, out_shape, scratch_shapes, args, name, sem, rider=None):
    if not rider:
        outs = pl.pallas_call(body, grid=grid, in_specs=in_specs, out_specs=out_specs, out_shape=out_shape,
                              scratch_shapes=scratch_shapes, name=name, compiler_params=_cp(sem))(*args)
        return outs, None
    riders = [rider] if isinstance(rider, tuple) else list(rider)
    arrays = [a for _, arrs in riders for a in arrs]
    parts = [_exchange_parts(kind, arrs) for kind, arrs in riders]
    counts = [len(arrs) for _, arrs in riders]
    nr, ni, no, ns = len(arrays), len(in_specs), len(out_specs), len(scratch_shapes)
    any_spec = pl.BlockSpec(memory_space=pl.ANY)

    def wrapped(*refs):
        ins, r_ins = refs[:ni], refs[ni:ni + nr]
        outs, r_outs = refs[ni + nr:ni + nr + no], refs[ni + nr + no:ni + 2 * nr + no]
        scratch, sem_refs = refs[ni + 2 * nr + no:ni + 2 * nr + no + ns], refs[ni + 2 * nr + no + ns:]
        first = pl.program_id(0) == 0
        last = pl.program_id(0) == grid[0] - 1
        for ax in range(1, len(grid)):
            first = first & (pl.program_id(ax) == 0)
            last = last & (pl.program_id(ax) == grid[ax] - 1)

        def each(which):
            pos = 0
            for e, (cnt, part) in enumerate(zip(counts, parts)):
                part[which](r_ins[pos:pos + cnt], r_outs[pos:pos + cnt], *sem_refs[3 * e:3 * e + 3])
                pos += cnt

        @pl.when(first)
        def _():
            each(2)
        body(*ins, *outs, *scratch)

        @pl.when(last)
        def _():
            each(3)

    outs = pl.pallas_call(
        wrapped, grid=grid, in_specs=list(in_specs) + [any_spec] * nr, out_specs=list(out_specs) + [any_spec] * nr,
        out_shape=list(out_shape) + [s for p in parts for s in p[0]],
        scratch_shapes=list(scratch_shapes) + [s for p in parts for s in p[1]], name=name,
        compiler_params=_cp(("arbitrary",) * len(grid)))(*args, *arrays)
    got, pos = [], no
    for cnt in counts:
        got.append(outs[pos:pos + cnt])
        pos += cnt
    return outs[:no], (got[0] if isinstance(rider, tuple) else got)


def pair_add(g8, recv, c_idx, name):
    _, r, c = g8.shape
    tr = _pick(r, 512, 8)

    def body(c_ref, g_ref, r_ref, o_ref):
        o_ref[...] = (g_ref[...].astype(F32) + r_ref[...].astype(F32)).astype(BF16)

    return pl.pallas_call(
        body,
        grid_spec=pltpu.PrefetchScalarGridSpec(
            num_scalar_prefetch=1, grid=(4, r // tr),
            in_specs=[pl.BlockSpec((1, tr, c), lambda q, i, cr: (2 * q + cr[0], i, 0)),
                      pl.BlockSpec((1, tr, c), lambda q, i, cr: (q, i, 0))],
            out_specs=pl.BlockSpec((1, tr, c), lambda q, i, cr: (q, i, 0))),
        out_shape=jax.ShapeDtypeStruct((4, r, c), BF16), name=name,
        compiler_params=_cp(("parallel", "parallel")))(c_idx, g8, recv)


def all_reduce_small(pack, name):
    r, c = pack.shape

    def body(x_ref, o_ref, buf, send_sems, recv_sems, local_sem):
        x, y, cc, chips = _place()
        me, sibling = (x, y, cc), (x, y, 1 - cc)

        def slot(dev):
            return buf.at[4 * dev[0] + 2 * dev[1] + dev[2]]

        def copy(k, block, to, src=None):
            return pltpu.make_async_remote_copy(
                src_ref=slot(block) if src is None else src, dst_ref=slot(block),
                send_sem=send_sems.at[k], recv_sem=recv_sems.at[k], device_id=to, device_id_type=MESH)

        mine = pltpu.make_async_copy(x_ref, slot(me), local_sem)
        mine.start()
        first = [copy(0, me, sibling, src=x_ref)]
        first += [copy(1 + j, me, (*chip, cc), src=x_ref) for j, chip in enumerate(chips)]
        for cp in first:
            cp.start()
        passed = [copy(4 + j, (*chip, cc), sibling) for j, chip in enumerate(chips)]
        for j, chip in enumerate(chips):
            copy(1 + j, (*chip, cc), me).wait_recv()
            passed[j].start()
        copy(0, sibling, me).wait_recv()
        for j, chip in enumerate(chips):
            copy(4 + j, (*chip, 1 - cc), me).wait_recv()
        for cp in first + passed:
            cp.wait_send()
        mine.wait()
        acc = buf[0]
        for k in range(1, N_DEV):
            acc = acc + buf[k]
        o_ref[...] = acc

    return pl.pallas_call(
        body, in_specs=[pl.BlockSpec(memory_space=pltpu.VMEM)],
        out_specs=pl.BlockSpec(memory_space=pltpu.VMEM),
        out_shape=jax.ShapeDtypeStruct((r, c), F32),
        scratch_shapes=[pltpu.VMEM((N_DEV, r, c), F32), pltpu.SemaphoreType.DMA((7,)),
                        pltpu.SemaphoreType.DMA((7,)), pltpu.SemaphoreType.DMA],
        name=name, compiler_params=_cp())(pack)


def _row(v):
    return v.reshape(1, -1)


def _lane_row(vals, offset):
    return jnp.pad(vals, (offset, LANES - offset - vals.shape[0])).reshape(1, LANES)


def _bias_to_pairs(b):
    i, nh, bp = b.shape
    return b.transpose(1, 0, 2).reshape(nh // 2, 2 * i, bp)


def _bias_from_pairs(b):
    p, i2, bp = b.shape
    return b.reshape(2 * p, i2 // 2, bp).transpose(1, 0, 2)


class LocalWeights:
    def __init__(self, W):
        self.W = W
        self.grads = {}

    def big(self, l, la):
        W = self.W
        out = {"f_w_up": W["f_w_up"][l].T, "f_w_down": W["f_w_down"][l]}
        if l < la:
            out.update(a_w_in=W["a_w_in"][l], a_w_out=W["a_w_out"][l])
        else:
            out.update(b_w_q=W["b_w_q"][l - la], b_w_out=W["b_w_out"][l - la])
        if l == la:
            out["w_kv"] = W["w_kv"].T
        return out

    def prep_rider(self, l):
        return None

    def prep_got(self, l, got):
        pass

    def fwd_rider(self, l):
        return None

    def fwd_got(self, l, got):
        pass

    def bwd_rider_a(self, l):
        return None

    def bwd_got_a(self, l, got):
        pass

    def bwd_rider_b(self, l):
        return None

    def bwd_got_b(self, l, got):
        pass

    def bwd_rider_c(self, l):
        return None

    def bwd_got_c(self, l, got):
        pass

    def ffn_grads_ready(self, l, grads):
        pass

    def grads_ready(self, l, grads):
        for k_, g in grads.items():
            self.grads.setdefault(k_, {})[l] = g

    def stacked(self):
        out = {k_: (jnp.stack([v_[l] for l in sorted(v_)]) if k_ != "w_kv" else next(iter(v_.values())))
               for k_, v_ in self.grads.items()}
        out["f_w_up"] = jnp.swapaxes(out["f_w_up"], 1, 2)
        out["w_kv"] = out["w_kv"].T
        return out


def _named(name, l, rider):
    return name if rider is None else f"{name}_x{l}"


def local_step(x, target, W, comm=None):
    comm = LocalWeights(W) if comm is None else comm
    n, d = x.shape
    la, ha = W["a_A_log"].shape
    lb, hb, tbl = W["b_rel_bias"].shape
    depth = W["f_norm"].shape[0]
    clip = (tbl - 1) // 2
    tp = -(-tbl // LANES) * LANES
    qk = ha * A_HEAD
    cw = 3 * qk
    bw = hb * B_HEAD

    h = x
    saves = []
    kv = h_kv = w_kv = None
    for l in range(depth):
        big = comm.big(l, la)
        sv = {"h_in": h, "big": big}
        rider = comm.fwd_rider(l)
        if l < la:
            alog = _lane_row(W["a_A_log"][l], ha)
            dtb = _lane_row(W["a_dt_bias"][l], ha)
            proj = norm_matmul(h, _row(W["a_norm"][l]), big["a_w_in"], "a_in_proj")
            early = comm.prep_rider(l)
            (q, k, v, bb, gb, u), got = gdn_prep(proj, W["a_conv"][l], alog, dtb, ha, _named("gdn_prep", l, early), early)
            comm.prep_got(l, got)
            (o, states, tinv), got = gdn_fwd(q, k, v, bb, gb, ha, _named("gdn_fwd", l, rider), rider)
            h, y = gdn_out(o, proj, _row(W["a_out_norm"][l]), big["a_w_out"], h, ha, "gdn_out")
            sv.update(proj=proj, q=q, k=k, v=v, bb=bb, gb=gb, u=u, states=states, tinv=tinv, o=o, y=y, alog=alog, dtb=dtb)
        else:
            j = l - la
            if j == 0:
                h_kv, w_kv = h, big["w_kv"]
                kv = norm_matmul(h, _row(W["kv_norm"]), w_kv, "kv_proj", w_t=True)
            qp = norm_matmul(h, _row(W["b_norm"][j]), big["b_w_q"], "b_q_proj")
            tblp = jnp.pad(W["b_rel_bias"][j], ((0, 0), (0, tp - tbl)))
            bias = _bias_to_pairs(bias_expand(tblp, clip, "bias_expand"))
            (o,), got = attn_fwd(qp, kv, bias, _named("attn_fwd", l, rider), rider)
            h = matmul_res(o, big["b_w_out"], h, "b_out_proj")
            sv.update(qp=qp, bias=bias, o=o)
        comm.fwd_got(l, got)
        sv["h_mid"] = h
        up = norm_matmul(h, _row(W["f_norm"][l]), big["f_w_up"], "f_up_proj", out_dtype=BF16, w_t=True)
        h, act, hc = ffn_act_down(up, W["f_conv"][l], _row(W["f_conv_b"][l]), big["f_w_down"], h, "ffn_act_down")
        sv.update(up=up, act=act, hc=hc)
        saves.append(sv)

    loss, dh, d_final = loss_head(h, _row(W["final_norm"]), target)

    G = {k_: [None] * (la if k_.startswith("a_") else lb if k_.startswith("b_") else depth)
         for k_ in ("a_norm", "a_conv", "a_A_log", "a_dt_bias", "a_out_norm",
                    "b_norm", "b_rel_bias", "f_norm", "f_conv", "f_conv_b")}
    G["final_norm"] = d_final[0]
    dk_acc = dv_acc = None
    for l in reversed(range(depth)):
        sv = saves[l]
        big = sv["big"]
        gbig = {}
        dhc, dcb = ffn_bwd_act(dh, sv["hc"], big["f_w_down"], "ffn_bwd_act")
        gbig["f_w_down"] = matmul_tn(sv["act"], dh, "f_down_wgrad")
        G["f_conv_b"][l] = dcb[0]
        rider = comm.bwd_rider_a(l)
        (dh, dup, dcw, dg), got = ffn_bwd_up(dhc, sv["up"], W["f_conv"][l], big["f_w_up"], sv["h_mid"], dh,
                                             _row(W["f_norm"][l]), _named("ffn_bwd_up", l, rider), rider)
        comm.bwd_got_a(l, got)
        G["f_conv"][l] = dcw
        G["f_norm"][l] = dg[0]
        gbig["f_w_up"] = norm_matmul_tn(sv["h_mid"], _row(W["f_norm"][l]), dup, "f_up_wgrad", transposed=True)
        comm.ffn_grads_ready(l, gbig)
        rider = comm.bwd_rider_b(l)
        if l < la:
            w_in = big["a_w_in"]
            do, dz, dwn = gdn_out_bwd(dh, sv["o"], sv["proj"], _row(W["a_out_norm"][l]), big["a_w_out"], ha, "gdn_out_bwd")
            G["a_out_norm"][l] = dwn[0]
            gbig["a_w_out"] = matmul_tn(sv["y"], dh, "a_out_wgrad")
            (dq, dk, dv, dbb, dgb), got = gdn_bwd(sv["q"], sv["k"], sv["v"], sv["bb"], sv["gb"], sv["states"], sv["tinv"], do, ha,
                                                  _named("gdn_bwd", l, rider), rider)
            comm.bwd_got_b(l, got)
            du, dba, dal, ddt = gdn_prep_bwd(sv["proj"], sv["u"], sv["alog"], sv["dtb"],
                                             dq, dk, dv, dbb, dgb, ha, "gdn_prep_bwd")
            G["a_A_log"][l] = dal[0, ha:2 * ha]
            G["a_dt_bias"][l] = ddt[0, ha:2 * ha]
            rider = comm.bwd_rider_c(l)
            (dqkv, dconv), got = conv_bwd(du, sv["proj"], W["a_conv"][l], A_CONV, _named("gdn_conv_bwd", l, rider), rider)
            if rider is not None:
                comm.bwd_got_c(l, got)
            G["a_conv"][l] = dconv
            gam = _row(W["a_norm"][l])
            pieces = [(dqkv, w_in[:, :cw]), (dz, w_in[:, cw:cw + qk]), (dba, w_in[:, cw + qk:])]
            gbig["a_w_in"] = jnp.concatenate(
                [norm_matmul_tn(sv["h_in"], gam, dqkv, "a_in_wgrad_qkv"),
                 norm_matmul_tn(sv["h_in"], gam, dz, "a_in_wgrad_z"),
                 norm_matmul_tn(sv["h_in"], gam, dba, "a_in_wgrad_ba")[:, :2 * ha]], axis=1)
            dh, dg = dx_norm_bwd(dh, sv["h_in"], gam, pieces, "a_in_dx")
            G["a_norm"][l] = dg[0]
        else:
            j = l - la
            d_o = matmul_nt(dh, big["b_w_out"], "b_out_dx")
            gbig["b_w_out"] = matmul_tn(sv["o"], dh, "b_out_wgrad")
            (dq, dk_acc, dv_acc, dbias), got = attn_bwd(
                sv["qp"], kv, sv["bias"], d_o, dk_acc, dv_acc,
                _named("attn_bwd" if dk_acc is None else "attn_bwd_acc", l, rider), rider)
            comm.bwd_got_b(l, got)
            G["b_rel_bias"][j] = bias_expand_bwd(_bias_from_pairs(dbias), clip, tp, "bias_expand_bwd")[:, :tbl]
            gam = _row(W["b_norm"][j])
            gbig["b_w_q"] = norm_matmul_tn(sv["h_in"], gam, dq, "b_q_wgrad")
            dh, dg = dx_norm_bwd(dh, sv["h_in"], gam, [(dq, big["b_w_q"])], "b_q_dx")
            G["b_norm"][j] = dg[0]
            if j == 0:
                gam = _row(W["kv_norm"])
                gbig["w_kv"] = jnp.concatenate([norm_matmul_tn(h_kv, gam, dk_acc, "kv_wgrad_k", transposed=True),
                                                norm_matmul_tn(h_kv, gam, dv_acc, "kv_wgrad_v", transposed=True)], axis=0)
                dh, dg = dx_norm_bwd(dh, h_kv, gam, [(dk_acc, w_kv[:bw]), (dv_acc, w_kv[bw:])], "kv_dx", w_t=True)
                G["kv_norm"] = dg[0]
        comm.grads_ready(l, gbig)
    out = {k_: (jnp.stack(v_) if isinstance(v_, list) else v_) for k_, v_ in G.items()}
    return loss[0, 0], dh, out, comm


WEIGHTS = ["a_norm", "a_w_in", "a_conv", "a_A_log", "a_dt_bias", "a_out_norm", "a_w_out", "kv_norm", "w_kv",
           "b_norm", "b_w_q", "b_rel_bias", "b_w_out", "f_norm", "f_w_up", "f_conv", "f_conv_b", "f_w_down",
           "final_norm"]
SHARD_AXIS = {"a_norm": 1, "a_w_in": 2, "a_conv": 2, "a_w_out": 1, "w_kv": 1, "b_w_q": 1, "b_w_out": 1,
              "f_w_up": 2, "f_conv": 2, "f_w_down": 1}
BIG = ["a_w_in", "a_w_out", "w_kv", "b_w_q", "b_w_out", "f_w_up", "f_w_down"]
SMALL_SHARDED = ["a_norm", "a_conv", "f_conv"]
TRANSPOSED = ("f_w_up", "w_kv")


def _t_view(k, a):
    return jnp.swapaxes(a, -1, -2) if k in TRANSPOSED else a


def _unstack(g, axis):
    if axis == 0:
        return g.reshape(-1, *g.shape[2:])
    return jnp.concatenate([g[i] for i in range(N_DEV)], axis=axis)


def _to_blocks(full, axis):
    if axis == 0:
        return full.reshape(N_DEV, -1, full.shape[-1])
    return jnp.stack(jnp.split(full, N_DEV, axis=axis))


def _pack(arrs):
    flat = []
    for a in arrs:
        f = a.reshape(-1)
        flat.append(jnp.pad(f, (0, (-f.shape[0]) % LANES)))
    f = jnp.concatenate(flat)
    f = jnp.pad(f, (0, (-f.shape[0]) % (8 * LANES)))
    return f.reshape(-1, LANES)


def _unpack(pack, shapes):
    flat = pack.reshape(-1)
    out, pos = [], 0
    for s in shapes:
        sz = math.prod(s)
        out.append(flat[pos:pos + sz].reshape(s))
        pos += sz + (-sz) % LANES
    return out


def _as2d(a):
    return a.reshape(1, -1) if a.ndim == 1 else a.reshape(-1, a.shape[-1])


def kernel(x, a_norm, a_w_in, a_conv, a_A_log, a_dt_bias, a_out_norm, a_w_out, kv_norm, w_kv, b_norm, b_w_q, b_rel_bias, b_w_out, f_norm, f_w_up, f_conv, f_conv_b, f_w_down, final_norm, loss_target, m_a_norm, m_a_w_in, m_a_conv, m_a_A_log, m_a_dt_bias, m_a_out_norm, m_a_w_out, m_kv_norm, m_w_kv, m_b_norm, m_b_w_q, m_b_rel_bias, m_b_w_out, m_f_norm, m_f_w_up, m_f_conv, m_f_conv_b, m_f_w_down, m_final_norm, v_a_norm, v_a_w_in, v_a_conv, v_a_A_log, v_a_dt_bias, v_a_out_norm, v_a_w_out, v_kv_norm, v_w_kv, v_b_norm, v_b_w_q, v_b_rel_bias, v_b_w_out, v_f_norm, v_f_w_up, v_f_conv, v_f_conv_b, v_f_w_down, v_final_norm):
    w = dict(a_norm=a_norm, a_w_in=a_w_in, a_conv=a_conv, a_A_log=a_A_log, a_dt_bias=a_dt_bias,
             a_out_norm=a_out_norm, a_w_out=a_w_out, kv_norm=kv_norm, w_kv=w_kv, b_norm=b_norm, b_w_q=b_w_q,
             b_rel_bias=b_rel_bias, b_w_out=b_w_out, f_norm=f_norm, f_w_up=f_w_up, f_conv=f_conv,
             f_conv_b=f_conv_b, f_w_down=f_w_down, final_norm=final_norm)
    mom = dict(a_norm=m_a_norm, a_w_in=m_a_w_in, a_conv=m_a_conv, a_A_log=m_a_A_log, a_dt_bias=m_a_dt_bias,
               a_out_norm=m_a_out_norm, a_w_out=m_a_w_out, kv_norm=m_kv_norm, w_kv=m_w_kv, b_norm=m_b_norm,
               b_w_q=m_b_w_q, b_rel_bias=m_b_rel_bias, b_w_out=m_b_w_out, f_norm=m_f_norm, f_w_up=m_f_w_up,
               f_conv=m_f_conv, f_conv_b=m_f_conv_b, f_w_down=m_f_w_down, final_norm=m_final_norm)
    var = dict(a_norm=v_a_norm, a_w_in=v_a_w_in, a_conv=v_a_conv, a_A_log=v_a_A_log, a_dt_bias=v_a_dt_bias,
               a_out_norm=v_a_out_norm, a_w_out=v_a_w_out, kv_norm=v_kv_norm, w_kv=v_w_kv, b_norm=v_b_norm,
               b_w_q=v_b_w_q, b_rel_bias=v_b_rel_bias, b_w_out=v_b_w_out, f_norm=v_f_norm, f_w_up=v_f_w_up,
               f_conv=v_f_conv, f_conv_b=v_f_conv_b, f_w_down=v_f_w_down, final_norm=v_final_norm)
    me = 4 * lax.axis_index("x") + 2 * lax.axis_index("y") + lax.axis_index("c")

    la, depth = a_A_log.shape[0], f_norm.shape[0]
    c_idx = lax.axis_index("c").astype(jnp.int32).reshape(1)
    shard_bf16 = {k: _t_view(k, w[k]).astype(BF16) for k in BIG}
    blk_axis = {k: 0 if k in TRANSPOSED else SHARD_AXIS[k] - (k != "w_kv") for k in BIG}

    class Sharded(LocalWeights):
        def __init__(self):
            super().__init__(w)
            self.full = {}
            self.stacks = {}
            self.pending = None
            self.parts = None

        def names(self, l):
            out = ["a_w_in", "a_w_out"] if l < la else ["b_w_q", "b_w_out"]
            return out + ["f_w_up", "f_w_down"] + (["w_kv"] if l == la else [])

        def index(self, k, l):
            return None if k == "w_kv" else (l - la if k.startswith("b_") else l)

        def shards(self, l, names=None):
            return [shard_bf16[k] if k == "w_kv" else shard_bf16[k][self.index(k, l)]
                    for k in (self.names(l) if names is None else names)]

        def install(self, l, gathered, names=None):
            out = self.full.setdefault(l, {})
            for k, g in zip(self.names(l) if names is None else names, gathered):
                out[k] = _unstack(g, blk_axis[k])
                if k == "a_w_in":
                    out[k] = jnp.pad(out[k], ((0, 0), (0, (-out[k].shape[1]) % LANES)))

        def big(self, l, la_):
            return self.full[l]

        def first_names(self):
            return ["a_w_in"] if la > 0 else self.names(0)

        def prep_rider(self, l):
            rest = [k for k in self.names(0) if k not in self.first_names()]
            return ("all_gather", self.shards(0, rest)) if l == 0 and rest else None

        def prep_got(self, l, got):
            if got is not None:
                self.install(0, got, [k for k in self.names(0) if k not in self.first_names()])

        def fwd_rider(self, l):
            return ("all_gather", self.shards(l + 1)) if l + 1 < depth else None

        def fwd_got(self, l, got):
            if got is not None:
                self.install(l + 1, got)

        def blocks(self, grads, keys):
            return [_to_blocks(grads[k], blk_axis[k]) for k in keys]

        def grads_ready(self, l, grads):
            keys = [k for k in self.names(l) if (k, l) not in self.early_keys]
            self.pending = (l, keys, self.blocks(grads, keys))

        early = early_parts = None
        early_keys = ()

        def ffn_grads_ready(self, l, grads):
            if l == 0 and la > 0:
                keys = ["f_w_up", "f_w_down"]
                self.early = (keys, self.blocks(grads, keys))
                self.early_keys = tuple((k, 0) for k in keys)

        def bwd_rider_a(self, l):
            return None if self.pending is None else ("rs_sibling", self.pending[2])

        def add_pairs(self, g8, from_sibling):
            return [pair_add(g, r, c_idx, "grads_pair_add") for g, r in zip(g8, from_sibling)]

        def bwd_got_a(self, l, got):
            if got is not None:
                self.parts = self.add_pairs(self.pending[2], got)

        def bwd_rider_b(self, l):
            riders = [] if self.parts is None else [("rs_chips", self.parts)]
            if self.early is not None:
                riders.append(("rs_sibling", self.early[1]))
            return riders

        def keep(self, stacks):
            l, keys, _ = self.pending
            for k, s in zip(keys, stacks):
                self.stacks[(k, l)] = s
            self.pending = self.parts = None

        def bwd_got_b(self, l, got):
            got = list(got or [])
            if self.parts is not None:
                self.keep(got.pop(0))
            if self.early is not None and got:
                self.early_parts = self.add_pairs(self.early[1], got.pop(0))

        def bwd_rider_c(self, l):
            return None if self.early_parts is None else ("rs_chips", self.early_parts)

        def bwd_got_c(self, l, got):
            for k, s in zip(self.early[0], got):
                self.stacks[(k, 0)] = s
            self.early = self.early_parts = None

        def finish(self):
            self.parts = self.add_pairs(self.pending[2], exchange("rs_sibling", self.pending[2], "grads_to_sibling"))
            self.keep(exchange("rs_chips", self.parts, "grads_to_chips"))

    comm = Sharded()

    small_shapes = [w[k].shape for k in SMALL_SHARDED]
    gathered = exchange("all_gather", comm.shards(0, comm.first_names()) + [_pack([w[k] for k in SMALL_SHARDED])],
                        "weights_all_gather")
    comm.install(0, gathered[:-1], comm.first_names())
    full = dict(w)
    small = [_unpack(gathered[-1][i], small_shapes) for i in range(N_DEV)]
    for idx, k in enumerate(SMALL_SHARDED):
        full[k] = jnp.concatenate([small[i][idx] for i in range(N_DEV)], axis=SHARD_AXIS[k])

    loss_part, grad_x, G, _ = local_step(x[0], loss_target[0], full, comm)
    comm.finish()
    stacks = []
    for k in BIG:
        layers = sorted(l for (k_, l) in comm.stacks if k_ == k)
        stacks.append(jnp.concatenate([comm.stacks[(k, l)] for l in layers], axis=1))

    small_names = [k for k in WEIGHTS if k not in BIG]
    reduced = _unpack(all_reduce_small(_pack([G[k] for k in small_names] + [loss_part.reshape(1)]), "small_all_reduce"),
                      [G[k].shape for k in small_names] + [(1,)])
    loss = reduced[-1][0]
    small_g = dict(zip(small_names, reduced[:-1]))
    for k in SMALL_SHARDED:
        sz = w[k].shape[SHARD_AXIS[k]]
        small_g[k] = lax.dynamic_slice_in_dim(small_g[k], me * sz, sz, axis=SHARD_AXIS[k])

    res = {}
    for k, st in zip(BIG, stacks):
        tshape = _t_view(k, w[k]).shape
        wt, mt, vt = (_as2d(_t_view(k, a)) for a in (w[k], mom[k], var[k]))
        outs = adamw(wt, st, mt, vt, "adamw_" + k)
        res[k] = [_t_view(k, o.reshape(tshape)) for o in outs]
    for k in small_names:
        outs = adamw(_as2d(w[k]), _as2d(small_g[k])[None], _as2d(mom[k]), _as2d(var[k]), "adamw_" + k)
        res[k] = [o.reshape(w[k].shape) for o in outs]

    return (loss, grad_x[None], *[res[k][0] for k in WEIGHTS], *[res[k][1] for k in WEIGHTS],
            *[res[k][2] for k in WEIGHTS], *[res[k][3] for k in WEIGHTS])
```

```python
import functools
import math

import jax
import jax.numpy as jnp
from jax import lax
from jax.experimental import pallas as pl
from jax.experimental.pallas import tpu as pltpu

F32 = jnp.float32
BF16 = jnp.bfloat16
HI = lax.Precision.HIGHEST
MESH = pl.DeviceIdType.MESH

EPS = 1e-6
NEG_INF = -1e30
CHUNK = 64
LEFT_CHUNKS = 8
BAND = (LEFT_CHUNKS + 1) * CHUNK
BAND_PAD = 640
A_CONV = 4
F_CONV = 3
A_HEAD = 128
B_HEAD = 64
LANES = 128
HALO = 8
N_DEV = 8

ADAM_LR = 0.001
ADAM_B1 = 0.9
ADAM_B2 = 0.999
ADAM_EPS = 1e-08
ADAM_WD = 0.01
ADAM_STEP = 10

VMEM_LIMIT_V7X = 56 * 1024 * 1024
WGRAD_DTYPE = BF16
GDN_BWD_HEADS = 8
COL_CHUNK = 256
FFN_TILE = 256


def _cp(sem=None, vmem=VMEM_LIMIT_V7X):
    kw = dict(vmem_limit_bytes=vmem)
    if sem is not None:
        kw["dimension_semantics"] = sem
    return pltpu.CompilerParams(**kw)


def _pick(n, target, q=LANES):
    best = None
    for t in range(q, min(n, target) + 1, q):
        if n % t == 0:
            best = t
    return best if best is not None else n


def _sig(x):
    return 1.0 / (1.0 + jnp.exp(-x))


def _softplus(x):
    return jnp.maximum(x, 0.0) + jnp.log(1.0 + jnp.exp(-jnp.abs(x)))


def _rms(x, g):
    return x * lax.rsqrt(jnp.mean(x * x, axis=-1, keepdims=True) + EPS) * g


def _rms_bwd(x, g, dxn):
    r = lax.rsqrt(jnp.mean(x * x, axis=-1, keepdims=True) + EPS)
    gd = dxn * g
    dx = r * gd - x * (r * r * r) * jnp.mean(x * gd, axis=-1, keepdims=True)
    dg = jnp.sum(dxn * x * r, axis=0, keepdims=True)
    return dx, dg


def _dot(a, b):
    return jnp.dot(a, b, preferred_element_type=F32)


def _dot_nt(a, b):
    return lax.dot_general(a, b, (((1,), (1,)), ((), ())), preferred_element_type=F32)


def _dot_tn(a, b):
    return lax.dot_general(a, b, (((0,), (0,)), ((), ())), preferred_element_type=F32)


def _hdot(a, b):
    return jnp.dot(a, b, precision=HI, preferred_element_type=F32)


def _hdot_nt(a, b):
    return lax.dot_general(a, b, (((1,), (1,)), ((), ())), precision=HI, preferred_element_type=F32)


def _resident(shape, index_map):
    return pl.BlockSpec(shape, index_map, pipeline_mode=pl.Buffered(1))


def norm_matmul(h, gamma, w, name, out_dtype=F32, w_t=False):
    n, d = h.shape
    nc = w.shape[0] if w_t else w.shape[1]
    rows, cols = (1024, 2816) if out_dtype == BF16 else (512, 4224) if nc > 1024 else (1024, 1536)
    tm = _pick(n, rows, 8)
    tn = _pick(nc, cols)

    def body(h_ref, g_ref, w_ref, o_ref):
        xn = _rms(h_ref[...], g_ref[...]).astype(BF16)
        o_ref[...] = (_dot_nt(xn, w_ref[...]) if w_t else _dot(xn, w_ref[...])).astype(out_dtype)

    return pl.pallas_call(
        body, grid=(nc // tn, n // tm),
        in_specs=[pl.BlockSpec((tm, d), lambda j, i: (i, 0)),
                  pl.BlockSpec((1, d), lambda j, i: (0, 0)),
                  pl.BlockSpec((tn, d), lambda j, i: (j, 0)) if w_t else pl.BlockSpec((d, tn), lambda j, i: (0, j))],
        out_specs=pl.BlockSpec((tm, tn), lambda j, i: (i, j)),
        out_shape=jax.ShapeDtypeStruct((n, nc), out_dtype), name=name,
        compiler_params=_cp(("parallel", "parallel")))(h, gamma, w)


def norm_matmul_tn(h, gamma, dy, name, transposed=False):
    n, d = h.shape
    nc = dy.shape[1]
    tm = _pick(n, 2048 if dy.dtype == BF16 else 1024, 8)
    tn = _pick(nc, 1536)

    steps = n // tm

    def body(h_ref, g_ref, dy_ref, o_ref, acc):
        i = pl.program_id(1)

        @pl.when(i == 0)
        def _():
            acc[...] = jnp.zeros_like(acc)
        xn = _rms(h_ref[...], g_ref[...]).astype(BF16)
        dyb = dy_ref[...].astype(BF16)
        acc[...] += _dot_tn(dyb, xn) if transposed else _dot_tn(xn, dyb)

        @pl.when(i == steps - 1)
        def _():
            o_ref[...] = acc[...].astype(WGRAD_DTYPE)

    return pl.pallas_call(
        body, grid=(nc // tn, steps),
        in_specs=[pl.BlockSpec((tm, d), lambda j, i: (i, 0)),
                  pl.BlockSpec((1, d), lambda j, i: (0, 0)),
                  pl.BlockSpec((tm, tn), lambda j, i: (i, j))],
        out_specs=pl.BlockSpec((tn, d), lambda j, i: (j, 0)) if transposed else pl.BlockSpec((d, tn), lambda j, i: (0, j)),
        out_shape=jax.ShapeDtypeStruct((nc, d) if transposed else (d, nc), WGRAD_DTYPE),
        scratch_shapes=[pltpu.VMEM((tn, d) if transposed else (d, tn), F32)], name=name,
        compiler_params=_cp(("parallel", "arbitrary")))(h, gamma, dy)


def matmul_tn(a, dy, name):
    n, ka = a.shape
    nc = dy.shape[1]
    tm = _pick(n, 2048 if a.dtype == BF16 else 1024, 8)
    tk = _pick(ka, 1536)
    tn = _pick(nc, 1024)
    steps = n // tm

    def body(a_ref, dy_ref, o_ref, acc):
        i = pl.program_id(2)

        @pl.when(i == 0)
        def _():
            acc[...] = jnp.zeros_like(acc)
        acc[...] += _dot_tn(a_ref[...].astype(BF16), dy_ref[...].astype(BF16))

        @pl.when(i == steps - 1)
        def _():
            o_ref[...] = acc[...].astype(WGRAD_DTYPE)

    return pl.pallas_call(
        body, grid=(ka // tk, nc // tn, steps),
        in_specs=[pl.BlockSpec((tm, tk), lambda k, j, i: (i, k)),
                  pl.BlockSpec((tm, tn), lambda k, j, i: (i, j))],
        out_specs=pl.BlockSpec((tk, tn), lambda k, j, i: (k, j)),
        out_shape=jax.ShapeDtypeStruct((ka, nc), WGRAD_DTYPE),
        scratch_shapes=[pltpu.VMEM((tk, tn), F32)], name=name,
        compiler_params=_cp(("parallel", "parallel", "arbitrary")))(a, dy)


def matmul_res(a, w, h, name):
    n, k = a.shape
    d = w.shape[1]
    tm = _pick(n, 1024, 8)

    def body(a_ref, w_ref, h_ref, o_ref):
        o_ref[...] = h_ref[...] + _dot(a_ref[...].astype(BF16), w_ref[...])

    return pl.pallas_call(
        body, grid=(n // tm,),
        in_specs=[pl.BlockSpec((tm, k), lambda i: (i, 0)),
                  _resident((k, d), lambda i: (0, 0)),
                  pl.BlockSpec((tm, d), lambda i: (i, 0))],
        out_specs=pl.BlockSpec((tm, d), lambda i: (i, 0)),
        out_shape=jax.ShapeDtypeStruct((n, d), F32), name=name,
        compiler_params=_cp(("parallel",)))(a, w, h)


def matmul_nt(dy, w, name):
    n, k = dy.shape
    d = w.shape[0]
    tm = _pick(n, 1024, 8)

    def body(dy_ref, w_ref, o_ref):
        o_ref[...] = _dot_nt(dy_ref[...].astype(BF16), w_ref[...])

    return pl.pallas_call(
        body, grid=(n // tm,),
        in_specs=[pl.BlockSpec((tm, k), lambda i: (i, 0)),
                  _resident((d, k), lambda i: (0, 0))],
        out_specs=pl.BlockSpec((tm, d), lambda i: (i, 0)),
        out_shape=jax.ShapeDtypeStruct((n, d), F32), name=name,
        compiler_params=_cp(("parallel",)))(dy, w)


def dx_norm_bwd(dout, h, gamma, pieces, name, rider=None, w_t=False):
    n, d = h.shape
    tm = _pick(n, 512, 8)
    np_ = len(pieces)
    mm = _dot if w_t else _dot_nt

    def body(*refs):
        dout_ref, h_ref, g_ref = refs[:3]
        dys = refs[3:3 + np_]
        ws = refs[3 + np_:3 + 2 * np_]
        dh_ref, dg_ref = refs[3 + 2 * np_:]
        dxn = mm(dys[0][...].astype(BF16), ws[0][...])
        for p in range(1, np_):
            dxn = dxn + mm(dys[p][...].astype(BF16), ws[p][...])
        dx, dg = _rms_bwd(h_ref[...], g_ref[...], dxn)
        dh_ref[...] = dout_ref[...] + dx

        @pl.when(pl.program_id(0) == 0)
        def _():
            dg_ref[...] = jnp.zeros_like(dg_ref)
        dg_ref[...] += dg

    in_specs = [pl.BlockSpec((tm, d), lambda i: (i, 0)),
                pl.BlockSpec((tm, d), lambda i: (i, 0)),
                pl.BlockSpec((1, d), lambda i: (0, 0))]
    in_specs += [pl.BlockSpec((tm, dy.shape[1]), lambda i: (i, 0)) for dy, _ in pieces]
    in_specs += [_resident(w.shape, lambda i: (0, 0)) for _, w in pieces]
    (dh, dg), got = host_call(
        body, grid=(n // tm,), in_specs=in_specs,
        out_specs=[pl.BlockSpec((tm, d), lambda i: (i, 0)), pl.BlockSpec((1, d), lambda i: (0, 0))],
        out_shape=[jax.ShapeDtypeStruct((n, d), F32), jax.ShapeDtypeStruct((1, d), F32)], name=name,
        scratch_shapes=[], sem=("arbitrary",), rider=rider,
        args=(dout, h, gamma, *[p[0] for p in pieces], *[p[1] for p in pieces]))
    return (dh, dg) if rider is None else (dh, dg, got)


def loss_head(h, gamma, target, name="loss_head"):
    n, d = h.shape
    tm = _pick(n, 512, 8)

    def body(h_ref, g_ref, t_ref, loss_ref, dh_ref, dg_ref):
        @pl.when(pl.program_id(0) == 0)
        def _():
            loss_ref[...] = jnp.zeros_like(loss_ref)
            dg_ref[...] = jnp.zeros_like(dg_ref)
        x = h_ref[...]
        g = g_ref[...]
        e = _rms(x, g) - t_ref[...]
        part = jnp.sum(jnp.sum(e * e, axis=-1, keepdims=True), axis=0, keepdims=True) * (0.5 / d)
        loss_ref[...] += jnp.broadcast_to(part, loss_ref.shape)
        dx, dg = _rms_bwd(x, g, e * (1.0 / d))
        dh_ref[...] = dx
        dg_ref[...] += dg

    return pl.pallas_call(
        body, grid=(n // tm,),
        in_specs=[pl.BlockSpec((tm, d), lambda i: (i, 0)), pl.BlockSpec((1, d), lambda i: (0, 0)),
                  pl.BlockSpec((tm, d), lambda i: (i, 0))],
        out_specs=[pl.BlockSpec((8, LANES), lambda i: (0, 0)), pl.BlockSpec((tm, d), lambda i: (i, 0)),
                   pl.BlockSpec((1, d), lambda i: (0, 0))],
        out_shape=[jax.ShapeDtypeStruct((8, LANES), F32), jax.ShapeDtypeStruct((n, d), F32),
                   jax.ShapeDtypeStruct((1, d), F32)], name=name,
        compiler_params=_cp(("arbitrary",)))(h, gamma, target)


def _halo_rows(dtype):
    return HALO * (4 // jnp.dtype(dtype).itemsize)


def _prev_halo_map(t, hb=HALO):
    return lambda i: (jnp.maximum(i * (t // hb) - 1, 0), 0)


def _next_halo_map(t, n, hb=HALO):
    return lambda i: (jnp.minimum((i + 1) * (t // hb), n // hb - 1), 0)


def _fill_prev(xs, main_ref, halo_ref, i, cols=slice(None)):
    hb = halo_ref.shape[0]
    xs[0:HALO, :] = jnp.where(i > 0, halo_ref[hb - HALO:hb, cols].astype(F32), 0.0)
    xs[HALO:, :] = main_ref[:, cols].astype(F32)


def _causal_conv(xs, w_ref, width, t, cols=slice(None), xcols=slice(None)):
    x = xs[:, xcols]
    acc = w_ref[width - 1:width, cols] * x[HALO:, :]
    for k in range(width - 1):
        acc = acc + w_ref[k:k + 1, cols] * pltpu.roll(x, width - 1 - k, axis=0)[HALO:, :]
    return acc


def _col_chunks(width, target=COL_CHUNK):
    tc = _pick(width, target)
    return [slice(j * tc, (j + 1) * tc) for j in range(width // tc)]


def ffn_act_down(up, conv_w, conv_b, w_down, h, name):
    n, c2 = up.shape
    ff = c2 // 2
    d = h.shape[1]
    t = _pick(n, 2 * FFN_TILE, 8)
    hb = _halo_rows(up.dtype)
    chunks = _col_chunks(ff)
    tc = chunks[0].stop

    def body(up_ref, halo_ref, cw_ref, cb_ref, wd_ref, h_ref, o_ref, act_ref, hc_ref, xg, xv):
        i = pl.program_id(0)
        acc = h_ref[...]
        for cs in chunks:
            vs = slice(ff + cs.start, ff + cs.stop)
            _fill_prev(xg, up_ref, halo_ref, i, cs)
            _fill_prev(xv, up_ref, halo_ref, i, vs)
            gate = _causal_conv(xg, cw_ref, F_CONV, t, cs) + cb_ref[:, cs]
            val = _causal_conv(xv, cw_ref, F_CONV, t, vs) + cb_ref[:, vs]
            hc_ref[:, cs] = gate.astype(BF16)
            hc_ref[:, vs] = val.astype(BF16)
            act = (gate * _sig(gate) * val).astype(BF16)
            act_ref[:, cs] = act
            acc = acc + _dot(act, wd_ref[cs, :])
        o_ref[...] = acc

    return pl.pallas_call(
        body, grid=(n // t,),
        in_specs=[pl.BlockSpec((t, c2), lambda i: (i, 0)),
                  pl.BlockSpec((hb, c2), _prev_halo_map(t, hb)),
                  pl.BlockSpec((F_CONV, c2), lambda i: (0, 0)),
                  pl.BlockSpec((1, c2), lambda i: (0, 0)),
                  _resident((ff, d), lambda i: (0, 0)),
                  pl.BlockSpec((t, d), lambda i: (i, 0))],
        out_specs=[pl.BlockSpec((t, d), lambda i: (i, 0)), pl.BlockSpec((t, ff), lambda i: (i, 0)),
                   pl.BlockSpec((t, c2), lambda i: (i, 0))],
        out_shape=[jax.ShapeDtypeStruct((n, d), F32), jax.ShapeDtypeStruct((n, ff), BF16),
                   jax.ShapeDtypeStruct((n, c2), BF16)],
        scratch_shapes=[pltpu.VMEM((t + HALO, tc), F32), pltpu.VMEM((t + HALO, tc), F32)], name=name,
        compiler_params=_cp(("parallel",)))(up, up, conv_w, conv_b, w_down, h)


def ffn_bwd_act(dout, hc, w_down, name):
    n, c2 = hc.shape
    ff = c2 // 2
    d = dout.shape[1]
    t = _pick(n, 2 * FFN_TILE, 8)
    chunks = _col_chunks(ff)

    def body(dout_ref, hc_ref, wd_ref, dhc_ref, dcb_ref):
        i = pl.program_id(0)

        @pl.when(i == 0)
        def _():
            dcb_ref[...] = jnp.zeros_like(dcb_ref)
        doutb = dout_ref[...].astype(BF16)
        for cs in chunks:
            vs = slice(ff + cs.start, ff + cs.stop)
            gate = hc_ref[:, cs].astype(F32)
            val = hc_ref[:, vs].astype(F32)
            sg = _sig(gate)
            da = _dot_nt(doutb, wd_ref[cs, :])
            dgate = da * val * (sg * (1.0 + gate * (1.0 - sg)))
            dval = da * gate * sg
            dhc_ref[:, cs] = dgate.astype(BF16)
            dhc_ref[:, vs] = dval.astype(BF16)
            dcb_ref[:, cs] += jnp.sum(dgate, axis=0, keepdims=True)
            dcb_ref[:, vs] += jnp.sum(dval, axis=0, keepdims=True)

    return pl.pallas_call(
        body, grid=(n // t,),
        in_specs=[pl.BlockSpec((t, d), lambda i: (i, 0)),
                  pl.BlockSpec((t, c2), lambda i: (i, 0)),
                  _resident((ff, d), lambda i: (0, 0))],
        out_specs=[pl.BlockSpec((t, c2), lambda i: (i, 0)), pl.BlockSpec((1, c2), lambda i: (0, 0))],
        out_shape=[jax.ShapeDtypeStruct((n, c2), BF16), jax.ShapeDtypeStruct((1, c2), F32)], name=name,
        compiler_params=_cp(("arbitrary",)))(dout, hc, w_down)


def conv_bwd_tail(dy_ref, dnext_ref, x_ref, cw_ref, dcw_ref, ds, width, t, i, last, cols=slice(None)):
    ds[0:t, :] = dy_ref[:, cols].astype(F32)
    ds[t:, :] = jnp.where(i < last, dnext_ref[0:HALO, cols].astype(F32), 0.0)
    x = x_ref[:, cols].astype(F32)
    dall = ds[...]
    dx = None
    for k in range(width):
        off = width - 1 - k
        shifted = dall[0:t, :] if off == 0 else pltpu.roll(dall, t + HALO - off, axis=0)[0:t, :]
        term = cw_ref[k:k + 1, cols] * shifted
        dx = term if dx is None else dx + term
        dcw_ref[k:k + 1, cols] += jnp.sum(shifted * x, axis=0, keepdims=True)
    return dx


def ffn_bwd_up(dhc, up, conv_w, w_up, h, dout, gamma, name, rider=None):
    n, c2 = up.shape
    d = h.shape[1]
    t = _pick(n, FFN_TILE, 8)
    last = n // t - 1
    chunks = _col_chunks(c2)
    tc = chunks[0].stop

    def body(dhc_ref, dnext_ref, up_ref, cw_ref, wu_ref, h_ref, dout_ref, g_ref,
             dh_ref, dup_ref, dcw_ref, dg_ref, ds):
        i = pl.program_id(0)

        @pl.when(i == 0)
        def _():
            dcw_ref[...] = jnp.zeros_like(dcw_ref)
            dg_ref[...] = jnp.zeros_like(dg_ref)
        dxn = jnp.zeros((t, d), F32)
        for cs in chunks:
            dup = conv_bwd_tail(dhc_ref, dnext_ref, up_ref, cw_ref, dcw_ref, ds, F_CONV, t, i, last, cs)
            dupb = dup.astype(BF16)
            dup_ref[:, cs] = dupb
            dxn = dxn + _dot(dupb, wu_ref[cs, :])
        dx, dg = _rms_bwd(h_ref[...], g_ref[...], dxn)
        dh_ref[...] = dout_ref[...] + dx
        dg_ref[...] += dg

    return host_call(
        body, grid=(n // t,), rider=rider, sem=("arbitrary",), args=(dhc, dhc, up, conv_w, w_up, h, dout, gamma),
        in_specs=[pl.BlockSpec((t, c2), lambda i: (i, 0)),
                  pl.BlockSpec((_halo_rows(dhc.dtype), c2), _next_halo_map(t, n, _halo_rows(dhc.dtype))),
                  pl.BlockSpec((t, c2), lambda i: (i, 0)),
                  pl.BlockSpec((F_CONV, c2), lambda i: (0, 0)),
                  _resident((c2, d), lambda i: (0, 0)),
                  pl.BlockSpec((t, d), lambda i: (i, 0)),
                  pl.BlockSpec((t, d), lambda i: (i, 0)),
                  pl.BlockSpec((1, d), lambda i: (0, 0))],
        out_specs=[pl.BlockSpec((t, d), lambda i: (i, 0)), pl.BlockSpec((t, c2), lambda i: (i, 0)),
                   pl.BlockSpec((F_CONV, c2), lambda i: (0, 0)), pl.BlockSpec((1, d), lambda i: (0, 0))],
        out_shape=[jax.ShapeDtypeStruct((n, d), F32), jax.ShapeDtypeStruct((n, c2), BF16),
                   jax.ShapeDtypeStruct((F_CONV, c2), F32), jax.ShapeDtypeStruct((1, d), F32)],
        scratch_shapes=[pltpu.VMEM((t + HALO, tc), F32)], name=name)


def _gdn_head(uq, uk, uv, pba, alog, dtb, head, n_heads):
    lane = lax.broadcasted_iota(jnp.int32, pba.shape, 1)
    sq = uq * _sig(uq)
    q = sq * lax.rsqrt(jnp.sum(sq * sq, axis=-1, keepdims=True) + EPS) * (A_HEAD ** -0.5)
    sk = uk * _sig(uk)
    k = sk * lax.rsqrt(jnp.sum(sk * sk, axis=-1, keepdims=True) + EPS)
    v = uv * _sig(uv)
    beta = jnp.sum(jnp.where(lane == head, _sig(pba), 0.0), axis=-1, keepdims=True)
    g_all = -jnp.exp(alog) * _softplus(pba + dtb)
    g = jnp.sum(jnp.where(lane == n_heads + head, g_all, 0.0), axis=-1, keepdims=True)
    return q, k, v, jnp.broadcast_to(beta, uq.shape), jnp.broadcast_to(g, uq.shape)


def gdn_prep(proj, conv_w, alog, dtb, n_heads, name, rider=None):
    n = proj.shape[0]
    qk = n_heads * A_HEAD
    cw = 3 * qk
    ba_blk = (cw + qk) // LANES
    t = _pick(n, 256, 8)

    def body(x_ref, halo_ref, pba_ref, cw_ref, al_ref, dt_ref, q_ref, k_ref, v_ref, b_ref, g_ref, u_ref, xs):
        i = pl.program_id(0)
        xs[0:HALO, :] = jnp.where(i > 0, halo_ref[...], 0.0)
        xs[HALO:, :] = x_ref[...]
        pba = pba_ref[...]
        for hd in range(n_heads):
            s0 = slice(hd * A_HEAD, (hd + 1) * A_HEAD)
            s1 = slice(qk + hd * A_HEAD, qk + (hd + 1) * A_HEAD)
            s2 = slice(2 * qk + hd * A_HEAD, 2 * qk + (hd + 1) * A_HEAD)
            uq, uk, uv = [_causal_conv(xs, cw_ref, A_CONV, t, s, s) for s in (s0, s1, s2)]
            u_ref[:, s0] = uq.astype(BF16)
            u_ref[:, s1] = uk.astype(BF16)
            u_ref[:, s2] = uv.astype(BF16)
            q, k, v, bb, gb = _gdn_head(uq, uk, uv, pba, al_ref[...], dt_ref[...], hd, n_heads)
            q_ref[:, s0] = q
            k_ref[:, s0] = k
            v_ref[:, s0] = v
            b_ref[:, s0] = bb
            g_ref[:, s0] = gb

    out = jax.ShapeDtypeStruct((n, qk), F32)
    return host_call(
        body, grid=(n // t,),
        in_specs=[pl.BlockSpec((t, cw), lambda i: (i, 0)),
                  pl.BlockSpec((HALO, cw), _prev_halo_map(t)),
                  pl.BlockSpec((t, LANES), lambda i: (i, ba_blk)),
                  pl.BlockSpec((A_CONV, cw), lambda i: (0, 0)),
                  pl.BlockSpec((1, LANES), lambda i: (0, 0)),
                  pl.BlockSpec((1, LANES), lambda i: (0, 0))],
        out_specs=[pl.BlockSpec((t, qk), lambda i: (i, 0))] * 5 + [pl.BlockSpec((t, cw), lambda i: (i, 0))],
        out_shape=[out] * 5 + [jax.ShapeDtypeStruct((n, cw), BF16)],
        scratch_shapes=[pltpu.VMEM((t + HALO, cw), F32)], name=name,
        sem=("parallel",), args=(proj, proj, proj, conv_w, alog, dtb), rider=rider)


def gdn_prep_bwd(proj, u, alog, dtb, dq, dk, dv, dbb, dgb, n_heads, name):
    n = proj.shape[0]
    qk = n_heads * A_HEAD
    cw = 3 * qk
    ba_blk = (cw + qk) // LANES
    t = _pick(n, 256, 8)

    def body(u_ref, pba_ref, al_ref, dt_ref, dq_ref, dk_ref, dv_ref, dbb_ref, dgb_ref,
             du_ref, dba_ref, dal_ref, ddt_ref):
        i = pl.program_id(0)
        u = u_ref[...].astype(F32)
        pba = pba_ref[...]
        lane0 = lax.broadcasted_iota(jnp.int32, (t, A_HEAD), 1) == 0
        dba = jnp.zeros((t, LANES), F32)
        dal = jnp.zeros((1, LANES), F32)
        ddt = jnp.zeros((1, LANES), F32)
        for hd in range(n_heads):
            s0 = slice(hd * A_HEAD, (hd + 1) * A_HEAD)
            s1 = slice(qk + hd * A_HEAD, qk + (hd + 1) * A_HEAD)
            s2 = slice(2 * qk + hd * A_HEAD, 2 * qk + (hd + 1) * A_HEAD)
            fn = functools.partial(_gdn_head, head=hd, n_heads=n_heads)
            _, vjp = jax.vjp(fn, u[:, s0], u[:, s1], u[:, s2], pba, al_ref[...], dt_ref[...])
            cts = (dq_ref[:, s0], dk_ref[:, s0], dv_ref[:, s0],
                   jnp.where(lane0, dbb_ref[:, s0], 0.0), jnp.where(lane0, dgb_ref[:, s0], 0.0))
            duq, duk, duv, dpba, da, dd = vjp(cts)
            du_ref[:, s0] = duq
            du_ref[:, s1] = duk
            du_ref[:, s2] = duv
            dba = dba + dpba
            dal = dal + da
            ddt = ddt + dd
        dba_ref[...] = dba

        @pl.when(i == 0)
        def _():
            dal_ref[...] = jnp.zeros_like(dal_ref)
            ddt_ref[...] = jnp.zeros_like(ddt_ref)
        dal_ref[...] += dal
        ddt_ref[...] += ddt

    tok = pl.BlockSpec((t, qk), lambda i: (i, 0))
    row = pl.BlockSpec((1, LANES), lambda i: (0, 0))
    return pl.pallas_call(
        body, grid=(n // t,),
        in_specs=[pl.BlockSpec((t, cw), lambda i: (i, 0)),
                  pl.BlockSpec((t, LANES), lambda i: (i, ba_blk)), row, row,
                  tok, tok, tok, tok, tok],
        out_specs=[pl.BlockSpec((t, cw), lambda i: (i, 0)), pl.BlockSpec((t, LANES), lambda i: (i, 0)), row, row],
        out_shape=[jax.ShapeDtypeStruct((n, cw), F32), jax.ShapeDtypeStruct((n, LANES), F32),
                   jax.ShapeDtypeStruct((1, LANES), F32), jax.ShapeDtypeStruct((1, LANES), F32)],
        name=name, compiler_params=_cp(("arbitrary",)))(u, proj, alog, dtb, dq, dk, dv, dbb, dgb)


def conv_bwd(du, x, conv_w, width, name, rider=None):
    n, cw = du.shape
    t = _pick(n, 256, 8)
    last = n // t - 1

    chunks = _col_chunks(cw, LANES)
    tc = chunks[0].stop

    def body(du_ref, dnext_ref, x_ref, cw_ref, dx_ref, dcw_ref, ds):
        i = pl.program_id(0)

        @pl.when(i == 0)
        def _():
            dcw_ref[...] = jnp.zeros_like(dcw_ref)
        for cs in chunks:
            dx_ref[:, cs] = conv_bwd_tail(du_ref, dnext_ref, x_ref, cw_ref, dcw_ref, ds, width, t, i, last, cs)

    return host_call(
        body, grid=(n // t,),
        in_specs=[pl.BlockSpec((t, cw), lambda i: (i, 0)),
                  pl.BlockSpec((HALO, cw), _next_halo_map(t, n)),
                  pl.BlockSpec((t, cw), lambda i: (i, 0)),
                  pl.BlockSpec((width, cw), lambda i: (0, 0))],
        out_specs=[pl.BlockSpec((t, cw), lambda i: (i, 0)), pl.BlockSpec((width, cw), lambda i: (0, 0))],
        out_shape=[jax.ShapeDtypeStruct((n, cw), F32), jax.ShapeDtypeStruct((width, cw), F32)],
        scratch_shapes=[pltpu.VMEM((t + HALO, tc), F32)], name=name,
        sem=("arbitrary",), args=(du, du, x, conv_w), rider=rider)


def _b(x):
    return x.astype(BF16)


def _mm_nn(a, b):
    return _dot(_b(a), _b(b))


def _mm_nt(a, b):
    return _dot_nt(_b(a), _b(b))


def _mm_tn(a, b):
    return _dot_tn(_b(a), _b(b))


@jax.custom_vjp
def _mmg_nn(a, b):
    return _mm_nn(a, b)


_mmg_nn.defvjp(lambda a, b: (_mm_nn(a, b), (a, b)),
               lambda res, dc: (_mm_nt(dc, res[1]), _mm_tn(res[0], dc)))


@jax.custom_vjp
def _mmg_nt(a, b):
    return _mm_nt(a, b)


_mmg_nt.defvjp(lambda a, b: (_mm_nt(a, b), (a, b)),
               lambda res, dc: (_mm_nn(dc, res[1]), _mm_tn(dc, res[0])))


@jax.custom_vjp
def _mmg_tn(a, b):
    return _mm_tn(a, b)


_mmg_tn.defvjp(lambda a, b: (_mm_tn(a, b), (a, b)),
               lambda res, dc: (_mm_nt(res[1], dc), _mm_nn(res[0], dc)))


def _bf16_parts(x, n):
    parts = []
    for _ in range(n):
        p = x.astype(BF16)
        parts.append(p)
        x = x - p.astype(F32)
    return parts


def _dot_f32ish(a, b):
    (ah, al), (bh, bl) = _bf16_parts(a, 2), _bf16_parts(b, 2)
    return _dot(ah, bh) + _dot(ah, bl) + _dot(al, bh)


def _tri_dot(x, transpose):
    c = x.shape[0]
    low = lax.broadcasted_iota(jnp.int32, (c, c), 0) >= lax.broadcasted_iota(jnp.int32, (c, c), 1)
    tri = jnp.where(low, 1.0, 0.0).astype(BF16)
    mm = _dot_tn if transpose else _dot
    return functools.reduce(lambda a, b: a + b, [mm(tri, p) for p in _bf16_parts(x, 3)])


def _cumsum(x):
    return _tri_dot(x, False)


@jax.custom_vjp
def _cumsum_g(x):
    return _tri_dot(x, False)


_cumsum_g.defvjp(lambda x: (_tri_dot(x, False), None), lambda _, ct: (_tri_dot(ct, True),))


def _each(f, *lists):
    return [f(*a) for a in zip(*lists)]


def _unit_lower_inv(ms):
    c = ms[0].shape[0]
    eye = jnp.where(lax.broadcasted_iota(jnp.int32, (c, c), 0) == lax.broadcasted_iota(jnp.int32, (c, c), 1), 1.0, 0.0)
    xs = [eye - m for m in ms]
    pws = _each(_mm_nn, ms, ms)
    for it in range(5):
        xs = _each(lambda x, pw: x + _mm_nn(x, pw), xs, pws)
        if it < 4:
            pws = _each(_mm_nn, pws, pws)
    rs = _each(lambda m, x: eye - x - _dot_f32ish(m, x), ms, xs)
    return _each(lambda x, r: x + _mm_nn(x, r), xs, rs)


@jax.custom_vjp
def _saved_inv_g(ms, xs):
    return xs


_saved_inv_g.defvjp(lambda ms, xs: (xs, xs),
                    lambda xs, dxs: (_each(lambda t, x: -_mm_nt(t, x), _each(_mm_tn, xs, dxs), xs),
                                     [jnp.zeros_like(x) for x in xs]))


def _gdn_chunk(ops, state, q, k, v, bb, gb):
    nn, nt, tn, inv, cum = ops
    c = CHUNK
    ri = lax.broadcasted_iota(jnp.int32, (c, c), 0)
    ci = lax.broadcasted_iota(jnp.int32, (c, c), 1)
    causal = ri >= ci
    strict = ri > ci
    gc = [cum(g) for g in gb]
    decay = [jnp.where(causal, jnp.exp(jnp.where(causal, x[:, :c] - x.T[:c, :], 0.0)), 0.0) for x in gc]
    kb = _each(lambda a, b: a * b, k, bb)
    kk = _each(nt, kb, k)
    m = _each(lambda a, d: jnp.where(strict, a * d, 0.0), kk, decay)
    tinv = inv(m)
    egc = [jnp.exp(x) for x in gc]
    u = _each(nn, tinv, _each(lambda a, b: a * b, v, bb))
    w = _each(nn, tinv, _each(lambda a, b: a * b, kb, egc))
    attn = _each(lambda a, d: a * d, _each(nt, q, k), decay)
    glast = [jnp.sum(g, axis=0, keepdims=True) for g in gb]
    ws = _each(nn, w, state)
    v_new = _each(lambda a, b: a - b, u, ws)
    qs = _each(nn, _each(lambda a, b: a * b, q, egc), state)
    av = _each(nn, attn, v_new)
    o = _each(lambda a, b: a + b, qs, av)
    kv = _each(tn, _each(lambda a, gl, x: a * jnp.exp(gl - x), k, glast, gc), v_new)
    new_state = _each(lambda s, gl, a: s * jnp.exp(gl) + a, state, glast, kv)
    return o, new_state


def gdn_fwd(q, k, v, bb, gb, n_heads, name, rider=None):
    n, w = q.shape
    nc = n // CHUNK
    cb = min(8, nc)
    rows = cb * CHUNK

    def body(q_ref, k_ref, v_ref, b_ref, g_ref, o_ref, st_ref, ti_ref, s_scr):
        @pl.when(pl.program_id(0) == 0)
        def _():
            s_scr[...] = jnp.zeros_like(s_scr)

        def step(c, carry):
            sl = pl.ds(pl.multiple_of(c * CHUNK, CHUNK), CHUNK)
            lanes = [slice(hd * A_HEAD, (hd + 1) * A_HEAD) for hd in range(n_heads)]
            state = [s_scr[hd] for hd in range(n_heads)]
            inverses = []

            def inv(ms):
                inverses.extend(_unit_lower_inv(ms))
                return inverses

            o, new_state = _gdn_chunk((_mm_nn, _mm_nt, _mm_tn, inv, _cumsum), state,
                                      *[[r[sl, ls] for ls in lanes] for r in (q_ref, k_ref, v_ref, b_ref, g_ref)])
            for hd, ls in enumerate(lanes):
                st_ref[hd, pl.ds(c, 1)] = state[hd][None]
                ti_ref[hd, pl.ds(c, 1)] = inverses[hd].astype(BF16)[None]
                o_ref[sl, ls] = o[hd]
                s_scr[hd] = new_state[hd]
            return carry

        lax.fori_loop(0, cb, step, 0)

    tok = pl.BlockSpec((rows, w), lambda j: (j, 0))
    return host_call(
        body, grid=(nc // cb,),
        in_specs=[tok] * 5,
        out_specs=[tok, pl.BlockSpec((n_heads, cb, A_HEAD, A_HEAD), lambda j: (0, j, 0, 0)),
                   pl.BlockSpec((n_heads, cb, CHUNK, CHUNK), lambda j: (0, j, 0, 0))],
        out_shape=[jax.ShapeDtypeStruct(q.shape, F32), jax.ShapeDtypeStruct((n_heads, nc, A_HEAD, A_HEAD), F32),
                   jax.ShapeDtypeStruct((n_heads, nc, CHUNK, CHUNK), BF16)],
        scratch_shapes=[pltpu.VMEM((n_heads, A_HEAD, A_HEAD), F32)], name=name,
        sem=("arbitrary",), args=(q, k, v, bb, gb), rider=rider)


def gdn_bwd(q, k, v, bb, gb, states, tinv, do, n_heads, name, rider=None):
    n, w = q.shape
    nc = n // CHUNK
    cb = min(4, nc)
    rows = cb * CHUNK
    nblk = nc // cb

    def body(q_ref, k_ref, v_ref, b_ref, g_ref, st_ref, ti_ref, do_ref,
             dq_ref, dk_ref, dv_ref, db_ref, dg_ref, ds_scr):
        @pl.when(pl.program_id(0) == 0)
        def _():
            ds_scr[...] = jnp.zeros_like(ds_scr)

        def step(s, carry):
            c = cb - 1 - s
            sl = pl.ds(pl.multiple_of(c * CHUNK, CHUNK), CHUNK)
            for h0 in range(0, n_heads, GDN_BWD_HEADS):
                heads = list(range(h0, min(h0 + GDN_BWD_HEADS, n_heads)))
                lanes = [slice(hd * A_HEAD, (hd + 1) * A_HEAD) for hd in heads]
                state = [st_ref[hd, pl.ds(c, 1)][0] for hd in heads]
                saved = [ti_ref[hd, pl.ds(c, 1)][0].astype(F32) for hd in heads]
                chunk_fn = functools.partial(
                    _gdn_chunk, (_mmg_nn, _mmg_nt, _mmg_tn, lambda ms: _saved_inv_g(ms, saved), _cumsum_g))
                _, vjp = jax.vjp(chunk_fn, state, *[[r[sl, ls] for ls in lanes]
                                                    for r in (q_ref, k_ref, v_ref, b_ref, g_ref)])
                dstate, dq, dk, dv, dbb, dgb = vjp(([do_ref[sl, ls] for ls in lanes], [ds_scr[hd] for hd in heads]))
                for u, (hd, ls) in enumerate(zip(heads, lanes)):
                    ds_scr[hd] = dstate[u]
                    dq_ref[sl, ls] = dq[u]
                    dk_ref[sl, ls] = dk[u]
                    dv_ref[sl, ls] = dv[u]
                    db_ref[sl, ls] = jnp.broadcast_to(jnp.sum(dbb[u], axis=-1, keepdims=True), dbb[u].shape)
                    dg_ref[sl, ls] = jnp.broadcast_to(jnp.sum(dgb[u], axis=-1, keepdims=True), dgb[u].shape)
            return carry

        lax.fori_loop(0, cb, step, 0)

    tok = pl.BlockSpec((rows, w), lambda j: (nblk - 1 - j, 0))
    out = jax.ShapeDtypeStruct(q.shape, F32)
    return host_call(
        body, grid=(nblk,),
        in_specs=[tok] * 5 + [pl.BlockSpec((n_heads, cb, A_HEAD, A_HEAD), lambda j: (0, nblk - 1 - j, 0, 0)),
                              pl.BlockSpec((n_heads, cb, CHUNK, CHUNK), lambda j: (0, nblk - 1 - j, 0, 0)), tok],
        out_specs=[tok] * 5, out_shape=[out] * 5,
        scratch_shapes=[pltpu.VMEM((n_heads, A_HEAD, A_HEAD), F32)], name=name,
        sem=("arbitrary",), args=(q, k, v, bb, gb, states, tinv, do), rider=rider)


def _gdn_gate(oh, zh, w):
    r = lax.rsqrt(jnp.mean(oh * oh, axis=-1, keepdims=True) + EPS)
    return oh * r * w * (zh * _sig(zh))


def gdn_out(o, proj, out_norm, w_out, h, n_heads, name):
    n, vw = o.shape
    d = h.shape[1]
    z_blk = 3 * vw // vw
    t = _pick(n, 512, 8)

    def body(o_ref, z_ref, w_ref, wo_ref, h_ref, out_ref, y_ref):
        for hd in range(n_heads):
            s0 = slice(hd * A_HEAD, (hd + 1) * A_HEAD)
            y_ref[:, s0] = _gdn_gate(o_ref[:, s0], z_ref[:, s0], w_ref[...]).astype(BF16)
        out_ref[...] = h_ref[...] + _dot(y_ref[...], wo_ref[...])

    return pl.pallas_call(
        body, grid=(n // t,),
        in_specs=[pl.BlockSpec((t, vw), lambda i: (i, 0)),
                  pl.BlockSpec((t, vw), lambda i: (i, z_blk)),
                  pl.BlockSpec((1, A_HEAD), lambda i: (0, 0)),
                  _resident((vw, d), lambda i: (0, 0)),
                  pl.BlockSpec((t, d), lambda i: (i, 0))],
        out_specs=[pl.BlockSpec((t, d), lambda i: (i, 0)), pl.BlockSpec((t, vw), lambda i: (i, 0))],
        out_shape=[jax.ShapeDtypeStruct((n, d), F32), jax.ShapeDtypeStruct((n, vw), BF16)], name=name,
        compiler_params=_cp(("parallel",)))(o, proj, out_norm, w_out, h)


def gdn_out_bwd(dout, o, proj, out_norm, w_out, n_heads, name):
    n, vw = o.shape
    d = dout.shape[1]
    z_blk = 3
    t = _pick(n, 512, 8)

    def body(dout_ref, o_ref, z_ref, w_ref, wo_ref, do_ref, dz_ref, dw_ref):
        dy = _dot_nt(dout_ref[...].astype(BF16), wo_ref[...])
        dw = jnp.zeros((1, A_HEAD), F32)
        for hd in range(n_heads):
            s0 = slice(hd * A_HEAD, (hd + 1) * A_HEAD)
            _, vjp = jax.vjp(_gdn_gate, o_ref[:, s0], z_ref[:, s0], w_ref[...])
            doh, dzh, dwh = vjp(dy[:, s0])
            do_ref[:, s0] = doh
            dz_ref[:, s0] = dzh
            dw = dw + dwh

        @pl.when(pl.program_id(0) == 0)
        def _():
            dw_ref[...] = jnp.zeros_like(dw_ref)
        dw_ref[...] += dw

    tok = pl.BlockSpec((t, vw), lambda i: (i, 0))
    return pl.pallas_call(
        body, grid=(n // t,),
        in_specs=[pl.BlockSpec((t, d), lambda i: (i, 0)), tok,
                  pl.BlockSpec((t, vw), lambda i: (i, z_blk)),
                  pl.BlockSpec((1, A_HEAD), lambda i: (0, 0)),
                  _resident((vw, d), lambda i: (0, 0))],
        out_specs=[tok, tok, pl.BlockSpec((1, A_HEAD), lambda i: (0, 0))],
        out_shape=[jax.ShapeDtypeStruct((n, vw), F32), jax.ShapeDtypeStruct((n, vw), F32),
                   jax.ShapeDtypeStruct((1, A_HEAD), F32)], name=name,
        compiler_params=_cp(("arbitrary",)))(dout, o, proj, out_norm, w_out)


BIAS_LINE = 768
BIAS_TOP = BAND + CHUNK - 2


def _bias_line_onehot(clip, tbl_pad):
    r = lax.broadcasted_iota(jnp.int32, (tbl_pad, BIAS_LINE), 0)
    v = lax.broadcasted_iota(jnp.int32, (tbl_pad, BIAS_LINE), 1)
    idx = jnp.clip(BIAS_TOP - v - (CHUNK - 1), -clip, clip) + clip
    return jnp.where((r == idx) & (v <= BIAS_TOP), 1.0, 0.0)


def bias_expand(tbl, clip, name):
    nh, tp = tbl.shape

    def body(t_ref, o_ref):
        line = _hdot(t_ref[...], _bias_line_onehot(clip, tp))
        keep = lax.broadcasted_iota(jnp.int32, (nh, BAND_PAD), 1) < BAND
        for i in range(CHUNK):
            s = CHUNK - 1 - i
            rolled = line if s == 0 else pltpu.roll(line, BIAS_LINE - s, axis=1)
            o_ref[i] = jnp.where(keep, rolled[:, :BAND_PAD], NEG_INF)

    return pl.pallas_call(
        body, in_specs=[pl.BlockSpec(memory_space=pltpu.VMEM)], out_specs=pl.BlockSpec(memory_space=pltpu.VMEM),
        out_shape=jax.ShapeDtypeStruct((CHUNK, nh, BAND_PAD), F32), name=name, compiler_params=_cp())(tbl)


def bias_expand_bwd(dbias, clip, tp, name):
    _, nh, _ = dbias.shape

    def body(d_ref, o_ref):
        keep = lax.broadcasted_iota(jnp.int32, (nh, BAND_PAD), 1) < BAND
        pad = jnp.zeros((nh, BIAS_LINE - BAND_PAD), F32)
        acc = jnp.zeros((nh, BIAS_LINE), F32)
        for i in range(CHUNK):
            s = CHUNK - 1 - i
            d = jnp.concatenate([jnp.where(keep, d_ref[i], 0.0), pad], axis=1)
            acc = acc + (d if s == 0 else pltpu.roll(d, s, axis=1))
        o_ref[...] = _hdot_nt(acc, _bias_line_onehot(clip, tp))

    return pl.pallas_call(
        body, in_specs=[pl.BlockSpec(memory_space=pltpu.VMEM)], out_specs=pl.BlockSpec(memory_space=pltpu.VMEM),
        out_shape=jax.ShapeDtypeStruct((nh, tp), F32), name=name, compiler_params=_cp())(dbias)


ATT_TILE = LEFT_CHUNKS * CHUNK


ATT_GROUP = 8


def _att_softmax(s, bias, n_chunk):
    slot = lax.broadcasted_iota(jnp.int32, (1, s.shape[1]), 1)
    before_start = jnp.where(slot < (LEFT_CHUNKS - n_chunk) * CHUNK, NEG_INF, 0.0)
    s = s + bias + before_start
    p = jnp.exp(s - jnp.max(s, axis=-1, keepdims=True))
    return p / jnp.sum(p, axis=-1, keepdims=True)


def _att_specs(n_pairs):
    prev = lambda p, i: (jnp.maximum(i - 1, 0), p)
    cur = lambda p, i: (i, p)
    prev_v = lambda p, i: (jnp.maximum(i - 1, 0), n_pairs + p)
    cur_v = lambda p, i: (i, n_pairs + p)
    blk = (ATT_TILE, LANES)
    return [pl.BlockSpec(blk, prev), pl.BlockSpec(blk, cur), pl.BlockSpec(blk, prev_v), pl.BlockSpec(blk, cur_v)]


def _att_fill(kbuf, vbuf, kp_ref, kc_ref, vp_ref, vc_ref):
    t = ATT_TILE
    kbuf[0:t, :] = kp_ref[...].astype(BF16)
    kbuf[t:2 * t, :] = kc_ref[...].astype(BF16)
    kbuf[2 * t:, :] = jnp.zeros((CHUNK, LANES), BF16)
    vbuf[0:t, :] = vp_ref[...].astype(BF16)
    vbuf[t:2 * t, :] = vc_ref[...].astype(BF16)
    vbuf[2 * t:, :] = jnp.zeros((CHUNK, LANES), BF16)


def _stack_heads(x, first):
    return jnp.concatenate([jnp.where(first, x, 0.0), jnp.where(first, 0.0, x)], axis=0).astype(BF16)


def attn_fwd(qp, kv, bias, name, rider=None):
    n, bw = qp.shape
    n_pairs = bw // LANES
    t = ATT_TILE
    cpt = t // CHUNK

    def body(q_ref, kp_ref, kc_ref, vp_ref, vc_ref, b_ref, o_ref, kbuf, vbuf):
        i = pl.program_id(1)
        _att_fill(kbuf, vbuf, kp_ref, kc_ref, vp_ref, vc_ref)
        lane = lax.broadcasted_iota(jnp.int32, (CHUNK, LANES), 1)
        first = lane < B_HEAD
        for g0 in range(0, cpt, ATT_GROUP):
            chunks = list(range(g0, min(g0 + ATT_GROUP, cpt)))
            band = [slice(c * CHUNK, c * CHUNK + BAND_PAD) for c in chunks]
            q2 = [_stack_heads(q_ref[c * CHUNK:(c + 1) * CHUNK, :] * (B_HEAD ** -0.5), first) for c in chunks]
            s = [_dot_nt(q_u, kbuf[b_u, :]) for q_u, b_u in zip(q2, band)]
            p = [_att_softmax(s_u, b_ref[0], i * cpt + c) for s_u, c in zip(s, chunks)]
            o = [_dot(p_u.astype(BF16), vbuf[b_u, :]) for p_u, b_u in zip(p, band)]
            for o_u, c in zip(o, chunks):
                o_ref[c * CHUNK:(c + 1) * CHUNK, :] = jnp.where(first, o_u[:CHUNK], o_u[CHUNK:])

    return host_call(
        body, grid=(n_pairs, n // t),
        in_specs=[pl.BlockSpec((t, LANES), lambda p, i: (i, p))] + _att_specs(n_pairs)
        + [pl.BlockSpec((1, 2 * CHUNK, BAND_PAD), lambda p, i: (p, 0, 0))],
        out_specs=[pl.BlockSpec((t, LANES), lambda p, i: (i, p))],
        out_shape=[jax.ShapeDtypeStruct((n, bw), F32)],
        scratch_shapes=[pltpu.VMEM((2 * t + CHUNK, LANES), BF16), pltpu.VMEM((2 * t + CHUNK, LANES), BF16)],
        name=name, sem=("parallel", "parallel"), args=(qp, kv, kv, kv, kv, bias), rider=rider)


def attn_bwd(qp, kv, bias, d_o, dk_in, dv_in, name, rider=None):
    n, bw = qp.shape
    n_pairs = bw // LANES
    t = ATT_TILE
    cpt = t // CHUNK
    nt = n // t
    have_in = dk_in is not None
    scale = B_HEAD ** -0.5

    def body(*refs):
        q_ref, kp_ref, kc_ref, vp_ref, vc_ref, b_ref, do_ref = refs[:7]
        pos = 7
        if have_in:
            dki_ref, dvi_ref = refs[7:9]
            pos = 9
        dq_ref, dk_ref, dv_ref, db_ref, kbuf, vbuf, dkacc, dvacc = refs[pos:]
        j = pl.program_id(1)
        i = nt - 1 - j
        _att_fill(kbuf, vbuf, kp_ref, kc_ref, vp_ref, vc_ref)

        @pl.when(j == 0)
        def _():
            dkacc[...] = jnp.zeros_like(dkacc)
            dvacc[...] = jnp.zeros_like(dvacc)
            db_ref[...] = jnp.zeros_like(db_ref)

        @pl.when(j > 0)
        def _():
            dkacc[t:2 * t, :] = dkacc[0:t, :]
            dvacc[t:2 * t, :] = dvacc[0:t, :]
            dkacc[0:t, :] = jnp.zeros((t, LANES), F32)
            dvacc[0:t, :] = jnp.zeros((t, LANES), F32)

        lane = lax.broadcasted_iota(jnp.int32, (CHUNK, LANES), 1)
        first = lane < B_HEAD
        for g0 in range(0, cpt, ATT_GROUP):
            chunks = list(range(g0, min(g0 + ATT_GROUP, cpt)))
            rows = [slice(c * CHUNK, (c + 1) * CHUNK) for c in chunks]
            band = [slice(c * CHUNK, c * CHUNK + BAND_PAD) for c in chunks]
            q2 = [_stack_heads(q_ref[r, :] * scale, first) for r in rows]
            do2 = [_stack_heads(do_ref[r, :], first) for r in rows]
            s = [_dot_nt(q_u, kbuf[b_u, :]) for q_u, b_u in zip(q2, band)]
            dp = [_dot_nt(d_u, vbuf[b_u, :]) for d_u, b_u in zip(do2, band)]
            p = [_att_softmax(s_u, b_ref[0], i * cpt + c) for s_u, c in zip(s, chunks)]
            ds = [p_u * (dp_u - jnp.sum(dp_u * p_u, axis=-1, keepdims=True)) for p_u, dp_u in zip(p, dp)]
            dsb = [d_u.astype(BF16) for d_u in ds]
            dv = [_dot_tn(p_u.astype(BF16), d_u) for p_u, d_u in zip(p, do2)]
            dq = [_dot(d_u, kbuf[b_u, :]) * scale for d_u, b_u in zip(dsb, band)]
            dk = [_dot_tn(d_u, q_u) for d_u, q_u in zip(dsb, q2)]
            db_ref[0] += functools.reduce(lambda a, b: a + b, ds)
            for r in range(chunks[0], chunks[-1] + BAND // CHUNK):
                terms = [(u, r - c) for u, c in enumerate(chunks) if 0 <= r - c < BAND // CHUNK]
                blk = slice(r * CHUNK, (r + 1) * CHUNK)
                dvacc[blk, :] += functools.reduce(lambda a, b: a + b, [dv[u][o * CHUNK:(o + 1) * CHUNK] for u, o in terms])
                dkacc[blk, :] += functools.reduce(lambda a, b: a + b, [dk[u][o * CHUNK:(o + 1) * CHUNK] for u, o in terms])
            for u in range(len(chunks)):
                dq_ref[rows[u], :] = jnp.where(first, dq[u][:CHUNK], dq[u][CHUNK:])

        if have_in:
            dk_ref[...] = dkacc[t:2 * t, :] + dki_ref[...]
            dv_ref[...] = dvacc[t:2 * t, :] + dvi_ref[...]
        else:
            dk_ref[...] = dkacc[t:2 * t, :]
            dv_ref[...] = dvacc[t:2 * t, :]

    rev = lambda p, j: (nt - 1 - j, p)
    tok = pl.BlockSpec((t, LANES), rev)
    kv_specs = [pl.BlockSpec((t, LANES), lambda p, j: (jnp.maximum(nt - 2 - j, 0), p)),
                pl.BlockSpec((t, LANES), rev),
                pl.BlockSpec((t, LANES), lambda p, j: (jnp.maximum(nt - 2 - j, 0), n_pairs + p)),
                pl.BlockSpec((t, LANES), lambda p, j: (nt - 1 - j, n_pairs + p))]
    in_specs = [tok] + kv_specs + [pl.BlockSpec((1, 2 * CHUNK, BAND_PAD), lambda p, j: (p, 0, 0)), tok]
    args = [qp, kv, kv, kv, kv, bias, d_o]
    if have_in:
        in_specs += [tok, tok]
        args += [dk_in, dv_in]
    out = jax.ShapeDtypeStruct((n, bw), F32)
    return host_call(
        body, grid=(n_pairs, nt), in_specs=in_specs,
        out_specs=[tok, tok, tok, pl.BlockSpec((1, 2 * CHUNK, BAND_PAD), lambda p, j: (p, 0, 0))],
        out_shape=[out, out, out, jax.ShapeDtypeStruct((n_pairs, 2 * CHUNK, BAND_PAD), F32)],
        scratch_shapes=[pltpu.VMEM((2 * t + CHUNK, LANES), BF16), pltpu.VMEM((2 * t + CHUNK, LANES), BF16),
                        pltpu.VMEM((2 * t + CHUNK, LANES), F32), pltpu.VMEM((2 * t + CHUNK, LANES), F32)],
        name=name, sem=("parallel", "arbitrary"), args=args, rider=rider)


def adamw(w, gstack, m, v, name):
    r, c = w.shape
    s = gstack.shape[0]
    tr = _pick(r, 512, 8)

    def body(w_ref, g_ref, m_ref, v_ref, go_ref, d_ref, mo_ref, vo_ref):
        g = g_ref[0].astype(F32)
        for k in range(1, s):
            g = g + g_ref[k].astype(F32)
        mn = ADAM_B1 * m_ref[...] + (1.0 - ADAM_B1) * g
        vn = ADAM_B2 * v_ref[...] + (1.0 - ADAM_B2) * (g * g)
        m_hat = mn / (1.0 - ADAM_B1 ** ADAM_STEP)
        v_hat = vn / (1.0 - ADAM_B2 ** ADAM_STEP)
        go_ref[...] = g
        d_ref[...] = -ADAM_LR * (m_hat / (jnp.sqrt(v_hat) + ADAM_EPS) + ADAM_WD * w_ref[...])
        mo_ref[...] = mn
        vo_ref[...] = vn

    blk = pl.BlockSpec((tr, c), lambda i: (i, 0))
    out = jax.ShapeDtypeStruct((r, c), F32)
    return pl.pallas_call(
        body, grid=(r // tr,),
        in_specs=[blk, pl.BlockSpec((s, tr, c), lambda i: (0, i, 0)), blk, blk],
        out_specs=[blk] * 4, out_shape=[out] * 4, name=name,
        compiler_params=_cp(("parallel",)))(w, gstack, m, v)


def _place():
    x, y, c = lax.axis_index("x"), lax.axis_index("y"), lax.axis_index("c")
    chips = [(1 - x, y), (x, 1 - y), (1 - x, 1 - y)]
    return x, y, c, chips


def _ag_copy(outs, send_sems, recv_sems, t, k, block, to, src=None):
    def slot(dev):
        return outs[t].at[4 * dev[0] + 2 * dev[1] + dev[2]]
    return pltpu.make_async_remote_copy(
        src_ref=slot(block) if src is None else src, dst_ref=slot(block),
        send_sem=send_sems.at[7 * t + k], recv_sem=recv_sems.at[7 * t + k], device_id=to, device_id_type=MESH)


def _ag_start(ins, outs, send_sems, recv_sems, local_sems):
    x, y, c, chips = _place()
    me = (x, y, c)
    for t in range(len(ins)):
        pltpu.make_async_copy(ins[t], outs[t].at[4 * x + 2 * y + c], local_sems.at[t]).start()
        _ag_copy(outs, send_sems, recv_sems, t, 0, me, (x, y, 1 - c), src=ins[t]).start()
        for j, chip in enumerate(chips):
            _ag_copy(outs, send_sems, recv_sems, t, 1 + j, me, (*chip, c), src=ins[t]).start()


def _ag_finish(ins, outs, send_sems, recv_sems, local_sems):
    x, y, c, chips = _place()
    me, sibling = (x, y, c), (x, y, 1 - c)
    nt = len(ins)
    for t in range(nt):
        for j, chip in enumerate(chips):
            _ag_copy(outs, send_sems, recv_sems, t, 1 + j, (*chip, c), me).wait_recv()
            _ag_copy(outs, send_sems, recv_sems, t, 4 + j, (*chip, c), sibling).start()
    for t in range(nt):
        _ag_copy(outs, send_sems, recv_sems, t, 0, sibling, me).wait_recv()
        for j, chip in enumerate(chips):
            _ag_copy(outs, send_sems, recv_sems, t, 4 + j, (*chip, 1 - c), me).wait_recv()
    for t in range(nt):
        _ag_copy(outs, send_sems, recv_sems, t, 0, me, sibling, src=ins[t]).wait_send()
        for j, chip in enumerate(chips):
            _ag_copy(outs, send_sems, recv_sems, t, 1 + j, me, (*chip, c), src=ins[t]).wait_send()
            _ag_copy(outs, send_sems, recv_sems, t, 4 + j, (*chip, c), sibling).wait_send()
        pltpu.make_async_copy(ins[t], outs[t].at[4 * x + 2 * y + c], local_sems.at[t]).wait()


def _rs_a_copy(ins, outs, send_sems, recv_sems, t, q):
    x, y, c, _ = _place()
    return pltpu.make_async_remote_copy(
        src_ref=ins[t].at[2 * q + (1 - c)], dst_ref=outs[t].at[q],
        send_sem=send_sems.at[4 * t + q], recv_sem=recv_sems.at[4 * t + q],
        device_id=(x, y, 1 - c), device_id_type=MESH)


def _rs_a_start(ins, outs, send_sems, recv_sems, local_sems):
    for t in range(len(ins)):
        for q in range(4):
            _rs_a_copy(ins, outs, send_sems, recv_sems, t, q).start()


def _rs_a_finish(ins, outs, send_sems, recv_sems, local_sems):
    for t in range(len(ins)):
        for q in range(4):
            _rs_a_copy(ins, outs, send_sems, recv_sems, t, q).wait_recv()
    for t in range(len(ins)):
        for q in range(4):
            _rs_a_copy(ins, outs, send_sems, recv_sems, t, q).wait_send()


def _rs_b_copy(ins, outs, send_sems, recv_sems, t, j, sending):
    x, y, c, chips = _place()
    mine, other = 2 * x + y, 2 * chips[j][0] + chips[j][1]
    return pltpu.make_async_remote_copy(
        src_ref=ins[t].at[other if sending else mine], dst_ref=outs[t].at[mine if sending else other],
        send_sem=send_sems.at[3 * t + j], recv_sem=recv_sems.at[3 * t + j],
        device_id=(*chips[j], c), device_id_type=MESH)


def _rs_b_start(ins, outs, send_sems, recv_sems, local_sems):
    x, y, _, _ = _place()
    for t in range(len(ins)):
        for j in range(3):
            _rs_b_copy(ins, outs, send_sems, recv_sems, t, j, True).start()
        pltpu.make_async_copy(ins[t].at[2 * x + y], outs[t].at[2 * x + y], local_sems.at[t]).start()


def _rs_b_finish(ins, outs, send_sems, recv_sems, local_sems):
    x, y, _, _ = _place()
    for t in range(len(ins)):
        for j in range(3):
            _rs_b_copy(ins, outs, send_sems, recv_sems, t, j, False).wait_recv()
    for t in range(len(ins)):
        for j in range(3):
            _rs_b_copy(ins, outs, send_sems, recv_sems, t, j, True).wait_send()
        pltpu.make_async_copy(ins[t].at[2 * x + y], outs[t].at[2 * x + y], local_sems.at[t]).wait()


_EXCHANGES = {
    "all_gather": (7, lambda a: (N_DEV, *a.shape), _ag_start, _ag_finish),
    "rs_sibling": (4, lambda a: (4, *a.shape[1:]), _rs_a_start, _rs_a_finish),
    "rs_chips": (3, lambda a: a.shape, _rs_b_start, _rs_b_finish),
}


def _exchange_parts(kind, arrays):
    per, shape_of, start, finish = _EXCHANGES[kind]
    n = len(arrays)
    out_shape = [jax.ShapeDtypeStruct(shape_of(a), a.dtype) for a in arrays]
    sems = [pltpu.SemaphoreType.DMA((per * n,)), pltpu.SemaphoreType.DMA((per * n,)), pltpu.SemaphoreType.DMA((n,))]
    return out_shape, sems, start, finish


def exchange(kind, arrays, name):
    n = len(arrays)
    out_shape, sems, start, finish = _exchange_parts(kind, arrays)
    any_spec = pl.BlockSpec(memory_space=pl.ANY)

    def body(*refs):
        ins, outs, sem_refs = refs[:n], refs[n:2 * n], refs[2 * n:]
        start(ins, outs, *sem_refs)
        finish(ins, outs, *sem_refs)

    return pl.pallas_call(body, in_specs=[any_spec] * n, out_specs=[any_spec] * n, out_shape=out_shape,
                          scratch_shapes=sems, name=name)(*arrays)


def host_call(body, *, grid, in_specs, out_specs, out_shape, scratch_shapes, args, name, sem, rider=None):
    if not rider:
        outs = pl.pallas_call(body, grid=grid, in_specs=in_specs, out_specs=out_specs, out_shape=out_shape,
                              scratch_shapes=scratch_shapes, name=name, compiler_params=_cp(sem))(*args)
        return outs, None
    riders = [rider] if isinstance(rider, tuple) else list(rider)
    arrays = [a for _, arrs in riders for a in arrs]
    parts = [_exchange_parts(kind, arrs) for kind, arrs in riders]
    counts = [len(arrs) for _, arrs in riders]
    nr, ni, no, ns = len(arrays), len(in_specs), len(out_specs), len(scratch_shapes)
    any_spec = pl.BlockSpec(memory_space=pl.ANY)

    def wrapped(*refs):
        ins, r_ins = refs[:ni], refs[ni:ni + nr]
        outs, r_outs = refs[ni + nr:ni + nr + no], refs[ni + nr + no:ni + 2 * nr + no]
        scratch, sem_refs = refs[ni + 2 * nr + no:ni + 2 * nr + no + ns], refs[ni + 2 * nr + no + ns:]
        first = pl.program_id(0) == 0
        last = pl.program_id(0) == grid[0] - 1
        for ax in range(1, len(grid)):
            first = first & (pl.program_id(ax) == 0)
            last = last & (pl.program_id(ax) == grid[ax] - 1)

        def each(which):
            pos = 0
            for e, (cnt, part) in enumerate(zip(counts, parts)):
                part[which](r_ins[pos:pos + cnt], r_outs[pos:pos + cnt], *sem_refs[3 * e:3 * e + 3])
                pos += cnt

        @pl.when(first)
        def _():
            each(2)
        body(*ins, *outs, *scratch)

        @pl.when(last)
        def _():
            each(3)

    outs = pl.pallas_call(
        wrapped, grid=grid, in_specs=list(in_specs) + [any_spec] * nr, out_specs=list(out_specs) + [any_spec] * nr,
        out_shape=list(out_shape) + [s for p in parts for s in p[0]],
        scratch_shapes=list(scratch_shapes) + [s for p in parts for s in p[1]], name=name,
        compiler_params=_cp(("arbitrary",) * len(grid)))(*args, *arrays)
    got, pos = [], no
    for cnt in counts:
        got.append(outs[pos:pos + cnt])
        pos += cnt
    return outs[:no], (got[0] if isinstance(rider, tuple) else got)


def pair_add(g8, recv, c_idx, name):
    _, r, c = g8.shape
    tr = _pick(r, 512, 8)

    def body(c_ref, g_ref, r_ref, o_ref):
        o_ref[...] = (g_ref[...].astype(F32) + r_ref[...].astype(F32)).astype(BF16)

    return pl.pallas_call(
        body,
        grid_spec=pltpu.PrefetchScalarGridSpec(
            num_scalar_prefetch=1, grid=(4, r // tr),
            in_specs=[pl.BlockSpec((1, tr, c), lambda q, i, cr: (2 * q + cr[0], i, 0)),
                      pl.BlockSpec((1, tr, c), lambda q, i, cr: (q, i, 0))],
            out_specs=pl.BlockSpec((1, tr, c), lambda q, i, cr: (q, i, 0))),
        out_shape=jax.ShapeDtypeStruct((4, r, c), BF16), name=name,
        compiler_params=_cp(("parallel", "parallel")))(c_idx, g8, recv)


def all_reduce_small(pack, name):
    r, c = pack.shape

    def body(x_ref, o_ref, buf, send_sems, recv_sems, local_sem):
        x, y, cc, chips = _place()
        me, sibling = (x, y, cc), (x, y, 1 - cc)

        def slot(dev):
            return buf.at[4 * dev[0] + 2 * dev[1] + dev[2]]

        def copy(k, block, to, src=None):
            return pltpu.make_async_remote_copy(
                src_ref=slot(block) if src is None else src, dst_ref=slot(block),
                send_sem=send_sems.at[k], recv_sem=recv_sems.at[k], device_id=to, device_id_type=MESH)

        mine = pltpu.make_async_copy(x_ref, slot(me), local_sem)
        mine.start()
        first = [copy(0, me, sibling, src=x_ref)]
        first += [copy(1 + j, me, (*chip, cc), src=x_ref) for j, chip in enumerate(chips)]
        for cp in first:
            cp.start()
        passed = [copy(4 + j, (*chip, cc), sibling) for j, chip in enumerate(chips)]
        for j, chip in enumerate(chips):
            copy(1 + j, (*chip, cc), me).wait_recv()
            passed[j].start()
        copy(0, sibling, me).wait_recv()
        for j, chip in enumerate(chips):
            copy(4 + j, (*chip, 1 - cc), me).wait_recv()
        for cp in first + passed:
            cp.wait_send()
        mine.wait()
        acc = buf[0]
        for k in range(1, N_DEV):
            acc = acc + buf[k]
        o_ref[...] = acc

    return pl.pallas_call(
        body, in_specs=[pl.BlockSpec(memory_space=pltpu.VMEM)],
        out_specs=pl.BlockSpec(memory_space=pltpu.VMEM),
        out_shape=jax.ShapeDtypeStruct((r, c), F32),
        scratch_shapes=[pltpu.VMEM((N_DEV, r, c), F32), pltpu.SemaphoreType.DMA((7,)),
                        pltpu.SemaphoreType.DMA((7,)), pltpu.SemaphoreType.DMA],
        name=name, compiler_params=_cp())(pack)


def _row(v):
    return v.reshape(1, -1)


def _lane_row(vals, offset):
    return jnp.pad(vals, (offset, LANES - offset - vals.shape[0])).reshape(1, LANES)


def _bias_to_pairs(b):
    i, nh, bp = b.shape
    return b.transpose(1, 0, 2).reshape(nh // 2, 2 * i, bp)


def _bias_from_pairs(b):
    p, i2, bp = b.shape
    return b.reshape(2 * p, i2 // 2, bp).transpose(1, 0, 2)


class LocalWeights:
    def __init__(self, W):
        self.W = W
        self.grads = {}

    def big(self, l, la):
        W = self.W
        out = {"f_w_up": W["f_w_up"][l].T, "f_w_down": W["f_w_down"][l]}
        if l < la:
            out.update(a_w_in=W["a_w_in"][l], a_w_out=W["a_w_out"][l])
        else:
            out.update(b_w_q=W["b_w_q"][l - la], b_w_out=W["b_w_out"][l - la])
        if l == la:
            out["w_kv"] = W["w_kv"].T
        return out

    def prep_rider(self, l):
        return None

    def prep_got(self, l, got):
        pass

    def fwd_rider(self, l):
        return None

    def fwd_got(self, l, got):
        pass

    def bwd_rider_a(self, l):
        return None

    def bwd_got_a(self, l, got):
        pass

    def bwd_rider_b(self, l):
        return None

    def bwd_got_b(self, l, got):
        pass

    def bwd_rider_c(self, l):
        return None

    def bwd_got_c(self, l, got):
        pass

    def ffn_grads_ready(self, l, grads):
        pass

    def grads_ready(self, l, grads):
        for k_, g in grads.items():
            self.grads.setdefault(k_, {})[l] = g

    def stacked(self):
        out = {k_: (jnp.stack([v_[l] for l in sorted(v_)]) if k_ != "w_kv" else next(iter(v_.values())))
               for k_, v_ in self.grads.items()}
        out["f_w_up"] = jnp.swapaxes(out["f_w_up"], 1, 2)
        out["w_kv"] = out["w_kv"].T
        return out


def _named(name, l, rider):
    return name if rider is None else f"{name}_x{l}"


def local_step(x, target, W, comm=None):
    comm = LocalWeights(W) if comm is None else comm
    n, d = x.shape
    la, ha = W["a_A_log"].shape
    lb, hb, tbl = W["b_rel_bias"].shape
    depth = W["f_norm"].shape[0]
    clip = (tbl - 1) // 2
    tp = -(-tbl // LANES) * LANES
    qk = ha * A_HEAD
    cw = 3 * qk
    bw = hb * B_HEAD

    h = x
    saves = []
    kv = h_kv = w_kv = None
    for l in range(depth):
        big = comm.big(l, la)
        sv = {"h_in": h, "big": big}
        rider = comm.fwd_rider(l)
        if l < la:
            alog = _lane_row(W["a_A_log"][l], ha)
            dtb = _lane_row(W["a_dt_bias"][l], ha)
            proj = norm_matmul(h, _row(W["a_norm"][l]), big["a_w_in"], "a_in_proj")
            early = comm.prep_rider(l)
            (q, k, v, bb, gb, u), got = gdn_prep(proj, W["a_conv"][l], alog, dtb, ha, _named("gdn_prep", l, early), early)
            comm.prep_got(l, got)
            (o, states, tinv), got = gdn_fwd(q, k, v, bb, gb, ha, _named("gdn_fwd", l, rider), rider)
            h, y = gdn_out(o, proj, _row(W["a_out_norm"][l]), big["a_w_out"], h, ha, "gdn_out")
            sv.update(proj=proj, q=q, k=k, v=v, bb=bb, gb=gb, u=u, states=states, tinv=tinv, o=o, y=y, alog=alog, dtb=dtb)
        else:
            j = l - la
            if j == 0:
                h_kv, w_kv = h, big["w_kv"]
                kv = norm_matmul(h, _row(W["kv_norm"]), w_kv, "kv_proj", w_t=True)
            qp = norm_matmul(h, _row(W["b_norm"][j]), big["b_w_q"], "b_q_proj")
            tblp = jnp.pad(W["b_rel_bias"][j], ((0, 0), (0, tp - tbl)))
            bias = _bias_to_pairs(bias_expand(tblp, clip, "bias_expand"))
            (o,), got = attn_fwd(qp, kv, bias, _named("attn_fwd", l, rider), rider)
            h = matmul_res(o, big["b_w_out"], h, "b_out_proj")
            sv.update(qp=qp, bias=bias, o=o)
        comm.fwd_got(l, got)
        sv["h_mid"] = h
        up = norm_matmul(h, _row(W["f_norm"][l]), big["f_w_up"], "f_up_proj", out_dtype=BF16, w_t=True)
        h, act, hc = ffn_act_down(up, W["f_conv"][l], _row(W["f_conv_b"][l]), big["f_w_down"], h, "ffn_act_down")
        sv.update(up=up, act=act, hc=hc)
        saves.append(sv)

    loss, dh, d_final = loss_head(h, _row(W["final_norm"]), target)

    G = {k_: [None] * (la if k_.startswith("a_") else lb if k_.startswith("b_") else depth)
         for k_ in ("a_norm", "a_conv", "a_A_log", "a_dt_bias", "a_out_norm",
                    "b_norm", "b_rel_bias", "f_norm", "f_conv", "f_conv_b")}
    G["final_norm"] = d_final[0]
    dk_acc = dv_acc = None
    for l in reversed(range(depth)):
        sv = saves[l]
        big = sv["big"]
        gbig = {}
        dhc, dcb = ffn_bwd_act(dh, sv["hc"], big["f_w_down"], "ffn_bwd_act")
        gbig["f_w_down"] = matmul_tn(sv["act"], dh, "f_down_wgrad")
        G["f_conv_b"][l] = dcb[0]
        rider = comm.bwd_rider_a(l)
        (dh, dup, dcw, dg), got = ffn_bwd_up(dhc, sv["up"], W["f_conv"][l], big["f_w_up"], sv["h_mid"], dh,
                                             _row(W["f_norm"][l]), _named("ffn_bwd_up", l, rider), rider)
        comm.bwd_got_a(l, got)
        G["f_conv"][l] = dcw
        G["f_norm"][l] = dg[0]
        gbig["f_w_up"] = norm_matmul_tn(sv["h_mid"], _row(W["f_norm"][l]), dup, "f_up_wgrad", transposed=True)
        comm.ffn_grads_ready(l, gbig)
        rider = comm.bwd_rider_b(l)
        if l < la:
            w_in = big["a_w_in"]
            do, dz, dwn = gdn_out_bwd(dh, sv["o"], sv["proj"], _row(W["a_out_norm"][l]), big["a_w_out"], ha, "gdn_out_bwd")
            G["a_out_norm"][l] = dwn[0]
            gbig["a_w_out"] = matmul_tn(sv["y"], dh, "a_out_wgrad")
            (dq, dk, dv, dbb, dgb), got = gdn_bwd(sv["q"], sv["k"], sv["v"], sv["bb"], sv["gb"], sv["states"], sv["tinv"], do, ha,
                                                  _named("gdn_bwd", l, rider), rider)
            comm.bwd_got_b(l, got)
            du, dba, dal, ddt = gdn_prep_bwd(sv["proj"], sv["u"], sv["alog"], sv["dtb"],
                                             dq, dk, dv, dbb, dgb, ha, "gdn_prep_bwd")
            G["a_A_log"][l] = dal[0, ha:2 * ha]
            G["a_dt_bias"][l] = ddt[0, ha:2 * ha]
            rider = comm.bwd_rider_c(l)
            (dqkv, dconv), got = conv_bwd(du, sv["proj"], W["a_conv"][l], A_CONV, _named("gdn_conv_bwd", l, rider), rider)
            if rider is not None:
                comm.bwd_got_c(l, got)
            G["a_conv"][l] = dconv
            gam = _row(W["a_norm"][l])
            pieces = [(dqkv, w_in[:, :cw]), (dz, w_in[:, cw:cw + qk]), (dba, w_in[:, cw + qk:])]
            gbig["a_w_in"] = jnp.concatenate(
                [norm_matmul_tn(sv["h_in"], gam, dqkv, "a_in_wgrad_qkv"),
                 norm_matmul_tn(sv["h_in"], gam, dz, "a_in_wgrad_z"),
                 norm_matmul_tn(sv["h_in"], gam, dba, "a_in_wgrad_ba")[:, :2 * ha]], axis=1)
            dh, dg = dx_norm_bwd(dh, sv["h_in"], gam, pieces, "a_in_dx")
            G["a_norm"][l] = dg[0]
        else:
            j = l - la
            d_o = matmul_nt(dh, big["b_w_out"], "b_out_dx")
            gbig["b_w_out"] = matmul_tn(sv["o"], dh, "b_out_wgrad")
            (dq, dk_acc, dv_acc, dbias), got = attn_bwd(
                sv["qp"], kv, sv["bias"], d_o, dk_acc, dv_acc,
                _named("attn_bwd" if dk_acc is None else "attn_bwd_acc", l, rider), rider)
            comm.bwd_got_b(l, got)
            G["b_rel_bias"][j] = bias_expand_bwd(_bias_from_pairs(dbias), clip, tp, "bias_expand_bwd")[:, :tbl]
            gam = _row(W["b_norm"][j])
            gbig["b_w_q"] = norm_matmul_tn(sv["h_in"], gam, dq, "b_q_wgrad")
            dh, dg = dx_norm_bwd(dh, sv["h_in"], gam, [(dq, big["b_w_q"])], "b_q_dx")
            G["b_norm"][j] = dg[0]
            if j == 0:
                gam = _row(W["kv_norm"])
                gbig["w_kv"] = jnp.concatenate([norm_matmul_tn(h_kv, gam, dk_acc, "kv_wgrad_k", transposed=True),
                                                norm_matmul_tn(h_kv, gam, dv_acc, "kv_wgrad_v", transposed=True)], axis=0)
                dh, dg = dx_norm_bwd(dh, h_kv, gam, [(dk_acc, w_kv[:bw]), (dv_acc, w_kv[bw:])], "kv_dx", w_t=True)
                G["kv_norm"] = dg[0]
        comm.grads_ready(l, gbig)
    out = {k_: (jnp.stack(v_) if isinstance(v_, list) else v_) for k_, v_ in G.items()}
    return loss[0, 0], dh, out, comm


WEIGHTS = ["a_norm", "a_w_in", "a_conv", "a_A_log", "a_dt_bias", "a_out_norm", "a_w_out", "kv_norm", "w_kv",
           "b_norm", "b_w_q", "b_rel_bias", "b_w_out", "f_norm", "f_w_up", "f_conv", "f_conv_b", "f_w_down",
           "final_norm"]
SHARD_AXIS = {"a_norm": 1, "a_w_in": 2, "a_conv": 2, "a_w_out": 1, "w_kv": 1, "b_w_q": 1, "b_w_out": 1,
              "f_w_up": 2, "f_conv": 2, "f_w_down": 1}
BIG = ["a_w_in", "a_w_out", "w_kv", "b_w_q", "b_w_out", "f_w_up", "f_w_down"]
SMALL_SHARDED = ["a_norm", "a_conv", "f_conv"]
TRANSPOSED = ("f_w_up", "w_kv")


def _t_view(k, a):
    return jnp.swapaxes(a, -1, -2) if k in TRANSPOSED else a


def _unstack(g, axis):
    if axis == 0:
        return g.reshape(-1, *g.shape[2:])
    return jnp.concatenate([g[i] for i in range(N_DEV)], axis=axis)


def _to_blocks(full, axis):
    if axis == 0:
        return full.reshape(N_DEV, -1, full.shape[-1])
    return jnp.stack(jnp.split(full, N_DEV, axis=axis))


def _pack(arrs):
    flat = []
    for a in arrs:
        f = a.reshape(-1)
        flat.append(jnp.pad(f, (0, (-f.shape[0]) % LANES)))
    f = jnp.concatenate(flat)
    f = jnp.pad(f, (0, (-f.shape[0]) % (8 * LANES)))
    return f.reshape(-1, LANES)


def _unpack(pack, shapes):
    flat = pack.reshape(-1)
    out, pos = [], 0
    for s in shapes:
        sz = math.prod(s)
        out.append(flat[pos:pos + sz].reshape(s))
        pos += sz + (-sz) % LANES
    return out


def _as2d(a):
    return a.reshape(1, -1) if a.ndim == 1 else a.reshape(-1, a.shape[-1])


def kernel(x, a_norm, a_w_in, a_conv, a_A_log, a_dt_bias, a_out_norm, a_w_out, kv_norm, w_kv, b_norm, b_w_q, b_rel_bias, b_w_out, f_norm, f_w_up, f_conv, f_conv_b, f_w_down, final_norm, loss_target, m_a_norm, m_a_w_in, m_a_conv, m_a_A_log, m_a_dt_bias, m_a_out_norm, m_a_w_out, m_kv_norm, m_w_kv, m_b_norm, m_b_w_q, m_b_rel_bias, m_b_w_out, m_f_norm, m_f_w_up, m_f_conv, m_f_conv_b, m_f_w_down, m_final_norm, v_a_norm, v_a_w_in, v_a_conv, v_a_A_log, v_a_dt_bias, v_a_out_norm, v_a_w_out, v_kv_norm, v_w_kv, v_b_norm, v_b_w_q, v_b_rel_bias, v_b_w_out, v_f_norm, v_f_w_up, v_f_conv, v_f_conv_b, v_f_w_down, v_final_norm):
    w = dict(a_norm=a_norm, a_w_in=a_w_in, a_conv=a_conv, a_A_log=a_A_log, a_dt_bias=a_dt_bias,
             a_out_norm=a_out_norm, a_w_out=a_w_out, kv_norm=kv_norm, w_kv=w_kv, b_norm=b_norm, b_w_q=b_w_q,
             b_rel_bias=b_rel_bias, b_w_out=b_w_out, f_norm=f_norm, f_w_up=f_w_up, f_conv=f_conv,
             f_conv_b=f_conv_b, f_w_down=f_w_down, final_norm=final_norm)
    mom = dict(a_norm=m_a_norm, a_w_in=m_a_w_in, a_conv=m_a_conv, a_A_log=m_a_A_log, a_dt_bias=m_a_dt_bias,
               a_out_norm=m_a_out_norm, a_w_out=m_a_w_out, kv_norm=m_kv_norm, w_kv=m_w_kv, b_norm=m_b_norm,
               b_w_q=m_b_w_q, b_rel_bias=m_b_rel_bias, b_w_out=m_b_w_out, f_norm=m_f_norm, f_w_up=m_f_w_up,
               f_conv=m_f_conv, f_conv_b=m_f_conv_b, f_w_down=m_f_w_down, final_norm=m_final_norm)
    var = dict(a_norm=v_a_norm, a_w_in=v_a_w_in, a_conv=v_a_conv, a_A_log=v_a_A_log, a_dt_bias=v_a_dt_bias,
               a_out_norm=v_a_out_norm, a_w_out=v_a_w_out, kv_norm=v_kv_norm, w_kv=v_w_kv, b_norm=v_b_norm,
               b_w_q=v_b_w_q, b_rel_bias=v_b_rel_bias, b_w_out=v_b_w_out, f_norm=v_f_norm, f_w_up=v_f_w_up,
               f_conv=v_f_conv, f_conv_b=v_f_conv_b, f_w_down=v_f_w_down, final_norm=v_final_norm)
    me = 4 * lax.axis_index("x") + 2 * lax.axis_index("y") + lax.axis_index("c")

    la, depth = a_A_log.shape[0], f_norm.shape[0]
    c_idx = lax.axis_index("c").astype(jnp.int32).reshape(1)
    shard_bf16 = {k: _t_view(k, w[k]).astype(BF16) for k in BIG}
    blk_axis = {k: 0 if k in TRANSPOSED else SHARD_AXIS[k] - (k != "w_kv") for k in BIG}

    class Sharded(LocalWeights):
        def __init__(self):
            super().__init__(w)
            self.full = {}
            self.stacks = {}
            self.pending = None
            self.parts = None

        def names(self, l):
            out = ["a_w_in", "a_w_out"] if l < la else ["b_w_q", "b_w_out"]
            return out + ["f_w_up", "f_w_down"] + (["w_kv"] if l == la else [])

        def index(self, k, l):
            return None if k == "w_kv" else (l - la if k.startswith("b_") else l)

        def shards(self, l, names=None):
            return [shard_bf16[k] if k == "w_kv" else shard_bf16[k][self.index(k, l)]
                    for k in (self.names(l) if names is None else names)]

        def install(self, l, gathered, names=None):
            out = self.full.setdefault(l, {})
            for k, g in zip(self.names(l) if names is None else names, gathered):
                out[k] = _unstack(g, blk_axis[k])
                if k == "a_w_in":
                    out[k] = jnp.pad(out[k], ((0, 0), (0, (-out[k].shape[1]) % LANES)))

        def big(self, l, la_):
            return self.full[l]

        def first_names(self):
            return ["a_w_in"] if la > 0 else self.names(0)

        def prep_rider(self, l):
            rest = [k for k in self.names(0) if k not in self.first_names()]
            return ("all_gather", self.shards(0, rest)) if l == 0 and rest else None

        def prep_got(self, l, got):
            if got is not None:
                self.install(0, got, [k for k in self.names(0) if k not in self.first_names()])

        def fwd_rider(self, l):
            return ("all_gather", self.shards(l + 1)) if l + 1 < depth else None

        def fwd_got(self, l, got):
            if got is not None:
                self.install(l + 1, got)

        def blocks(self, grads, keys):
            return [_to_blocks(grads[k], blk_axis[k]) for k in keys]

        def grads_ready(self, l, grads):
            keys = [k for k in self.names(l) if (k, l) not in self.early_keys]
            self.pending = (l, keys, self.blocks(grads, keys))

        early = early_parts = None
        early_keys = ()

        def ffn_grads_ready(self, l, grads):
            if l == 0 and la > 0:
                keys = ["f_w_up", "f_w_down"]
                self.early = (keys, self.blocks(grads, keys))
                self.early_keys = tuple((k, 0) for k in keys)

        def bwd_rider_a(self, l):
            return None if self.pending is None else ("rs_sibling", self.pending[2])

        def add_pairs(self, g8, from_sibling):
            return [pair_add(g, r, c_idx, "grads_pair_add") for g, r in zip(g8, from_sibling)]

        def bwd_got_a(self, l, got):
            if got is not None:
                self.parts = self.add_pairs(self.pending[2], got)

        def bwd_rider_b(self, l):
            riders = [] if self.parts is None else [("rs_chips", self.parts)]
            if self.early is not None:
                riders.append(("rs_sibling", self.early[1]))
            return riders

        def keep(self, stacks):
            l, keys, _ = self.pending
            for k, s in zip(keys, stacks):
                self.stacks[(k, l)] = s
            self.pending = self.parts = None

        def bwd_got_b(self, l, got):
            got = list(got or [])
            if self.parts is not None:
                self.keep(got.pop(0))
            if self.early is not None and got:
                self.early_parts = self.add_pairs(self.early[1], got.pop(0))

        def bwd_rider_c(self, l):
            return None if self.early_parts is None else ("rs_chips", self.early_parts)

        def bwd_got_c(self, l, got):
            for k, s in zip(self.early[0], got):
                self.stacks[(k, 0)] = s
            self.early = self.early_parts = None

        def finish(self):
            self.parts = self.add_pairs(self.pending[2], exchange("rs_sibling", self.pending[2], "grads_to_sibling"))
            self.keep(exchange("rs_chips", self.parts, "grads_to_chips"))

    comm = Sharded()

    small_shapes = [w[k].shape for k in SMALL_SHARDED]
    gathered = exchange("all_gather", comm.shards(0, comm.first_names()) + [_pack([w[k] for k in SMALL_SHARDED])],
                        "weights_all_gather")
    comm.install(0, gathered[:-1], comm.first_names())
    full = dict(w)
    small = [_unpack(gathered[-1][i], small_shapes) for i in range(N_DEV)]
    for idx, k in enumerate(SMALL_SHARDED):
        full[k] = jnp.concatenate([small[i][idx] for i in range(N_DEV)], axis=SHARD_AXIS[k])

    loss_part, grad_x, G, _ = local_step(x[0], loss_target[0], full, comm)
    comm.finish()
    stacks = []
    for k in BIG:
        layers = sorted(l for (k_, l) in comm.stacks if k_ == k)
        stacks.append(jnp.concatenate([comm.stacks[(k, l)] for l in layers], axis=1))

    small_names = [k for k in WEIGHTS if k not in BIG]
    reduced = _unpack(all_reduce_small(_pack([G[k] for k in small_names] + [loss_part.reshape(1)]), "small_all_reduce"),
                      [G[k].shape for k in small_names] + [(1,)])
    loss = reduced[-1][0]
    small_g = dict(zip(small_names, reduced[:-1]))
    for k in SMALL_SHARDED:
        sz = w[k].shape[SHARD_AXIS[k]]
        small_g[k] = lax.dynamic_slice_in_dim(small_g[k], me * sz, sz, axis=SHARD_AXIS[k])

    res = {}
    for k, st in zip(BIG, stacks):
        tshape = _t_view(k, w[k]).shape
        wt, mt, vt = (_as2d(_t_view(k, a)) for a in (w[k], mom[k], var[k]))
        outs = adamw(wt, st, mt, vt, "adamw_" + k)
        res[k] = [_t_view(k, o.reshape(tshape)) for o in outs]
    for k in small_names:
        outs = adamw(_as2d(w[k]), _as2d(small_g[k])[None], _as2d(mom[k]), _as2d(var[k]), "adamw_" + k)
        res[k] = [o.reshape(w[k].shape) for o in outs]

    return (loss, grad_x[None], *[res[k][0] for k in WEIGHTS], *[res[k][1] for k in WEIGHTS],
            *[res[k][2] for k in WEIGHTS], *[res[k][3] for k in WEIGHTS])
```

```python
import functools
import math

import jax
import jax.numpy as jnp
from jax import lax
from jax.experimental import pallas as pl
from jax.experimental.pallas import tpu as pltpu

F32 = jnp.float32
BF16 = jnp.bfloat16
HI = lax.Precision.HIGHEST
MESH = pl.DeviceIdType.MESH

EPS = 1e-6
NEG_INF = -1e30
CHUNK = 64
LEFT_CHUNKS = 8
BAND = (LEFT_CHUNKS + 1) * CHUNK
BAND_PAD = 640
A_CONV = 4
F_CONV = 3
A_HEAD = 128
B_HEAD = 64
LANES = 128
HALO = 8
N_DEV = 8

ADAM_LR = 0.001
ADAM_B1 = 0.9
ADAM_B2 = 0.999
ADAM_EPS = 1e-08
ADAM_WD = 0.01
ADAM_STEP = 10

VMEM_LIMIT_V7X = 56 * 1024 * 1024
WGRAD_DTYPE = BF16
GDN_BWD_HEADS = 8
COL_CHUNK = 256
FFN_TILE = 256


def _cp(sem=None, vmem=VMEM_LIMIT_V7X):
    kw = dict(vmem_limit_bytes=vmem)
    if sem is not None:
        kw["dimension_semantics"] = sem
    return pltpu.CompilerParams(**kw)


def _pick(n, target, q=LANES):
    best = None
    for t in range(q, min(n, target) + 1, q):
        if n % t == 0:
            best = t
    return best if best is not None else n


def _sig(x):
    return 1.0 / (1.0 + jnp.exp(-x))


def _softplus(x):
    return jnp.maximum(x, 0.0) + jnp.log(1.0 + jnp.exp(-jnp.abs(x)))


def _rms(x, g):
    return x * lax.rsqrt(jnp.mean(x * x, axis=-1, keepdims=True) + EPS) * g


def _rms_bwd(x, g, dxn):
    r = lax.rsqrt(jnp.mean(x * x, axis=-1, keepdims=True) + EPS)
    gd = dxn * g
    dx = r * gd - x * (r * r * r) * jnp.mean(x * gd, axis=-1, keepdims=True)
    dg = jnp.sum(dxn * x * r, axis=0, keepdims=True)
    return dx, dg


def _dot(a, b):
    return jnp.dot(a, b, preferred_element_type=F32)


def _dot_nt(a, b):
    return lax.dot_general(a, b, (((1,), (1,)), ((), ())), preferred_element_type=F32)


def _dot_tn(a, b):
    return lax.dot_general(a, b, (((0,), (0,)), ((), ())), preferred_element_type=F32)


def _hdot(a, b):
    return jnp.dot(a, b, precision=HI, preferred_element_type=F32)


def _hdot_nt(a, b):
    return lax.dot_general(a, b, (((1,), (1,)), ((), ())), precision=HI, preferred_element_type=F32)


def _resident(shape, index_map):
    return pl.BlockSpec(shape, index_map, pipeline_mode=pl.Buffered(1))


def norm_matmul(h, gamma, w, name, out_dtype=F32, w_t=False):
    n, d = h.shape
    nc = w.shape[0] if w_t else w.shape[1]
    rows, cols = (1024, 2816) if out_dtype == BF16 else (512, 4224) if nc > 1024 else (1024, 1536)
    tm = _pick(n, rows, 8)
    tn = _pick(nc, cols)

    def body(h_ref, g_ref, w_ref, o_ref):
        xn = _rms(h_ref[...], g_ref[...]).astype(BF16)
        o_ref[...] = (_dot_nt(xn, w_ref[...]) if w_t else _dot(xn, w_ref[...])).astype(out_dtype)

    return pl.pallas_call(
        body, grid=(nc // tn, n // tm),
        in_specs=[pl.BlockSpec((tm, d), lambda j, i: (i, 0)),
                  pl.BlockSpec((1, d), lambda j, i: (0, 0)),
                  pl.BlockSpec((tn, d), lambda j, i: (j, 0)) if w_t else pl.BlockSpec((d, tn), lambda j, i: (0, j))],
        out_specs=pl.BlockSpec((tm, tn), lambda j, i: (i, j)),
        out_shape=jax.ShapeDtypeStruct((n, nc), out_dtype), name=name,
        compiler_params=_cp(("parallel", "parallel")))(h, gamma, w)


def norm_matmul_tn(h, gamma, dy, name, transposed=False):
    n, d = h.shape
    nc = dy.shape[1]
    tm = _pick(n, 2048 if dy.dtype == BF16 else 1024, 8)
    tn = _pick(nc, 1536)

    steps = n // tm

    def body(h_ref, g_ref, dy_ref, o_ref, acc):
        i = pl.program_id(1)

        @pl.when(i == 0)
        def _():
            acc[...] = jnp.zeros_like(acc)
        xn = _rms(h_ref[...], g_ref[...]).astype(BF16)
        dyb = dy_ref[...].astype(BF16)
        acc[...] += _dot_tn(dyb, xn) if transposed else _dot_tn(xn, dyb)

        @pl.when(i == steps - 1)
        def _():
            o_ref[...] = acc[...].astype(WGRAD_DTYPE)

    return pl.pallas_call(
        body, grid=(nc // tn, steps),
        in_specs=[pl.BlockSpec((tm, d), lambda j, i: (i, 0)),
                  pl.BlockSpec((1, d), lambda j, i: (0, 0)),
                  pl.BlockSpec((tm, tn), lambda j, i: (i, j))],
        out_specs=pl.BlockSpec((tn, d), lambda j, i: (j, 0)) if transposed else pl.BlockSpec((d, tn), lambda j, i: (0, j)),
        out_shape=jax.ShapeDtypeStruct((nc, d) if transposed else (d, nc), WGRAD_DTYPE),
        scratch_shapes=[pltpu.VMEM((tn, d) if transposed else (d, tn), F32)], name=name,
        compiler_params=_cp(("parallel", "arbitrary")))(h, gamma, dy)


def matmul_tn(a, dy, name):
    n, ka = a.shape
    nc = dy.shape[1]
    tm = _pick(n, 2048 if a.dtype == BF16 else 1024, 8)
    tk = _pick(ka, 1536)
    tn = _pick(nc, 1024)
    steps = n // tm

    def body(a_ref, dy_ref, o_ref, acc):
        i = pl.program_id(2)

        @pl.when(i == 0)
        def _():
            acc[...] = jnp.zeros_like(acc)
        acc[...] += _dot_tn(a_ref[...].astype(BF16), dy_ref[...].astype(BF16))

        @pl.when(i == steps - 1)
        def _():
            o_ref[...] = acc[...].astype(WGRAD_DTYPE)

    return pl.pallas_call(
        body, grid=(ka // tk, nc // tn, steps),
        in_specs=[pl.BlockSpec((tm, tk), lambda k, j, i: (i, k)),
                  pl.BlockSpec((tm, tn), lambda k, j, i: (i, j))],
        out_specs=pl.BlockSpec((tk, tn), lambda k, j, i: (k, j)),
        out_shape=jax.ShapeDtypeStruct((ka, nc), WGRAD_DTYPE),
        scratch_shapes=[pltpu.VMEM((tk, tn), F32)], name=name,
        compiler_params=_cp(("parallel", "parallel", "arbitrary")))(a, dy)


def matmul_res(a, w, h, name):
    n, k = a.shape
    d = w.shape[1]
    tm = _pick(n, 1024, 8)

    def body(a_ref, w_ref, h_ref, o_ref):
        o_ref[...] = h_ref[...] + _dot(a_ref[...].astype(BF16), w_ref[...])

    return pl.pallas_call(
        body, grid=(n // tm,),
        in_specs=[pl.BlockSpec((tm, k), lambda i: (i, 0)),
                  _resident((k, d), lambda i: (0, 0)),
                  pl.BlockSpec((tm, d), lambda i: (i, 0))],
        out_specs=pl.BlockSpec((tm, d), lambda i: (i, 0)),
        out_shape=jax.ShapeDtypeStruct((n, d), F32), name=name,
        compiler_params=_cp(("parallel",)))(a, w, h)


def matmul_nt(dy, w, name):
    n, k = dy.shape
    d = w.shape[0]
    tm = _pick(n, 1024, 8)

    def body(dy_ref, w_ref, o_ref):
        o_ref[...] = _dot_nt(dy_ref[...].astype(BF16), w_ref[...])

    return pl.pallas_call(
        body, grid=(n // tm,),
        in_specs=[pl.BlockSpec((tm, k), lambda i: (i, 0)),
                  _resident((d, k), lambda i: (0, 0))],
        out_specs=pl.BlockSpec((tm, d), lambda i: (i, 0)),
        out_shape=jax.ShapeDtypeStruct((n, d), F32), name=name,
        compiler_params=_cp(("parallel",)))(dy, w)


def dx_norm_bwd(dout, h, gamma, pieces, name, rider=None, w_t=False):
    n, d = h.shape
    tm = _pick(n, 512, 8)
    np_ = len(pieces)
    mm = _dot if w_t else _dot_nt

    def body(*refs):
        dout_ref, h_ref, g_ref = refs[:3]
        dys = refs[3:3 + np_]
        ws = refs[3 + np_:3 + 2 * np_]
        dh_ref, dg_ref = refs[3 + 2 * np_:]
        dxn = mm(dys[0][...].astype(BF16), ws[0][...])
        for p in range(1, np_):
            dxn = dxn + mm(dys[p][...].astype(BF16), ws[p][...])
        dx, dg = _rms_bwd(h_ref[...], g_ref[...], dxn)
        dh_ref[...] = dout_ref[...] + dx

        @pl.when(pl.program_id(0) == 0)
        def _():
            dg_ref[...] = jnp.zeros_like(dg_ref)
        dg_ref[...] += dg

    in_specs = [pl.BlockSpec((tm, d), lambda i: (i, 0)),
                pl.BlockSpec((tm, d), lambda i: (i, 0)),
                pl.BlockSpec((1, d), lambda i: (0, 0))]
    in_specs += [pl.BlockSpec((tm, dy.shape[1]), lambda i: (i, 0)) for dy, _ in pieces]
    in_specs += [_resident(w.shape, lambda i: (0, 0)) for _, w in pieces]
    (dh, dg), got = host_call(
        body, grid=(n // tm,), in_specs=in_specs,
        out_specs=[pl.BlockSpec((tm, d), lambda i: (i, 0)), pl.BlockSpec((1, d), lambda i: (0, 0))],
        out_shape=[jax.ShapeDtypeStruct((n, d), F32), jax.ShapeDtypeStruct((1, d), F32)], name=name,
        scratch_shapes=[], sem=("arbitrary",), rider=rider,
        args=(dout, h, gamma, *[p[0] for p in pieces], *[p[1] for p in pieces]))
    return (dh, dg) if rider is None else (dh, dg, got)


def loss_head(h, gamma, target, name="loss_head"):
    n, d = h.shape
    tm = _pick(n, 512, 8)

    def body(h_ref, g_ref, t_ref, loss_ref, dh_ref, dg_ref):
        @pl.when(pl.program_id(0) == 0)
        def _():
            loss_ref[...] = jnp.zeros_like(loss_ref)
            dg_ref[...] = jnp.zeros_like(dg_ref)
        x = h_ref[...]
        g = g_ref[...]
        e = _rms(x, g) - t_ref[...]
        part = jnp.sum(jnp.sum(e * e, axis=-1, keepdims=True), axis=0, keepdims=True) * (0.5 / d)
        loss_ref[...] += jnp.broadcast_to(part, loss_ref.shape)
        dx, dg = _rms_bwd(x, g, e * (1.0 / d))
        dh_ref[...] = dx
        dg_ref[...] += dg

    return pl.pallas_call(
        body, grid=(n // tm,),
        in_specs=[pl.BlockSpec((tm, d), lambda i: (i, 0)), pl.BlockSpec((1, d), lambda i: (0, 0)),
                  pl.BlockSpec((tm, d), lambda i: (i, 0))],
        out_specs=[pl.BlockSpec((8, LANES), lambda i: (0, 0)), pl.BlockSpec((tm, d), lambda i: (i, 0)),
                   pl.BlockSpec((1, d), lambda i: (0, 0))],
        out_shape=[jax.ShapeDtypeStruct((8, LANES), F32), jax.ShapeDtypeStruct((n, d), F32),
                   jax.ShapeDtypeStruct((1, d), F32)], name=name,
        compiler_params=_cp(("arbitrary",)))(h, gamma, target)


def _halo_rows(dtype):
    return HALO * (4 // jnp.dtype(dtype).itemsize)


def _prev_halo_map(t, hb=HALO):
    return lambda i: (jnp.maximum(i * (t // hb) - 1, 0), 0)


def _next_halo_map(t, n, hb=HALO):
    return lambda i: (jnp.minimum((i + 1) * (t // hb), n // hb - 1), 0)


def _fill_prev(xs, main_ref, halo_ref, i, cols=slice(None)):
    hb = halo_ref.shape[0]
    xs[0:HALO, :] = jnp.where(i > 0, halo_ref[hb - HALO:hb, cols].astype(F32), 0.0)
    xs[HALO:, :] = main_ref[:, cols].astype(F32)


def _causal_conv(xs, w_ref, width, t, cols=slice(None), xcols=slice(None)):
    x = xs[:, xcols]
    acc = w_ref[width - 1:width, cols] * x[HALO:, :]
    for k in range(width - 1):
        acc = acc + w_ref[k:k + 1, cols] * pltpu.roll(x, width - 1 - k, axis=0)[HALO:, :]
    return acc


def _col_chunks(width, target=COL_CHUNK):
    tc = _pick(width, target)
    return [slice(j * tc, (j + 1) * tc) for j in range(width // tc)]


def ffn_act_down(up, conv_w, conv_b, w_down, h, name):
    n, c2 = up.shape
    ff = c2 // 2
    d = h.shape[1]
    t = _pick(n, 2 * FFN_TILE, 8)
    hb = _halo_rows(up.dtype)
    chunks = _col_chunks(ff)
    tc = chunks[0].stop

    def body(up_ref, halo_ref, cw_ref, cb_ref, wd_ref, h_ref, o_ref, act_ref, hc_ref, xg, xv):
        i = pl.program_id(0)
        acc = h_ref[...]
        for cs in chunks:
            vs = slice(ff + cs.start, ff + cs.stop)
            _fill_prev(xg, up_ref, halo_ref, i, cs)
            _fill_prev(xv, up_ref, halo_ref, i, vs)
            gate = _causal_conv(xg, cw_ref, F_CONV, t, cs) + cb_ref[:, cs]
            val = _causal_conv(xv, cw_ref, F_CONV, t, vs) + cb_ref[:, vs]
            hc_ref[:, cs] = gate.astype(BF16)
            hc_ref[:, vs] = val.astype(BF16)
            act = (gate * _sig(gate) * val).astype(BF16)
            act_ref[:, cs] = act
            acc = acc + _dot(act, wd_ref[cs, :])
        o_ref[...] = acc

    return pl.pallas_call(
        body, grid=(n // t,),
        in_specs=[pl.BlockSpec((t, c2), lambda i: (i, 0)),
                  pl.BlockSpec((hb, c2), _prev_halo_map(t, hb)),
                  pl.BlockSpec((F_CONV, c2), lambda i: (0, 0)),
                  pl.BlockSpec((1, c2), lambda i: (0, 0)),
                  _resident((ff, d), lambda i: (0, 0)),
                  pl.BlockSpec((t, d), lambda i: (i, 0))],
        out_specs=[pl.BlockSpec((t, d), lambda i: (i, 0)), pl.BlockSpec((t, ff), lambda i: (i, 0)),
                   pl.BlockSpec((t, c2), lambda i: (i, 0))],
        out_shape=[jax.ShapeDtypeStruct((n, d), F32), jax.ShapeDtypeStruct((n, ff), BF16),
                   jax.ShapeDtypeStruct((n, c2), BF16)],
        scratch_shapes=[pltpu.VMEM((t + HALO, tc), F32), pltpu.VMEM((t + HALO, tc), F32)], name=name,
        compiler_params=_cp(("parallel",)))(up, up, conv_w, conv_b, w_down, h)


def ffn_bwd_act(dout, hc, w_down, name):
    n, c2 = hc.shape
    ff = c2 // 2
    d = dout.shape[1]
    t = _pick(n, 2 * FFN_TILE, 8)
    chunks = _col_chunks(ff)

    def body(dout_ref, hc_ref, wd_ref, dhc_ref, dcb_ref):
        i = pl.program_id(0)

        @pl.when(i == 0)
        def _():
            dcb_ref[...] = jnp.zeros_like(dcb_ref)
        doutb = dout_ref[...].astype(BF16)
        for cs in chunks:
            vs = slice(ff + cs.start, ff + cs.stop)
            gate = hc_ref[:, cs].astype(F32)
            val = hc_ref[:, vs].astype(F32)
            sg = _sig(gate)
            da = _dot_nt(doutb, wd_ref[cs, :])
            dgate = da * val * (sg * (1.0 + gate * (1.0 - sg)))
            dval = da * gate * sg
            dhc_ref[:, cs] = dgate.astype(BF16)
            dhc_ref[:, vs] = dval.astype(BF16)
            dcb_ref[:, cs] += jnp.sum(dgate, axis=0, keepdims=True)
            dcb_ref[:, vs] += jnp.sum(dval, axis=0, keepdims=True)

    return pl.pallas_call(
        body, grid=(n // t,),
        in_specs=[pl.BlockSpec((t, d), lambda i: (i, 0)),
                  pl.BlockSpec((t, c2), lambda i: (i, 0)),
                  _resident((ff, d), lambda i: (0, 0))],
        out_specs=[pl.BlockSpec((t, c2), lambda i: (i, 0)), pl.BlockSpec((1, c2), lambda i: (0, 0))],
        out_shape=[jax.ShapeDtypeStruct((n, c2), BF16), jax.ShapeDtypeStruct((1, c2), F32)], name=name,
        compiler_params=_cp(("arbitrary",)))(dout, hc, w_down)


def conv_bwd_tail(dy_ref, dnext_ref, x_ref, cw_ref, dcw_ref, ds, width, t, i, last, cols=slice(None)):
    ds[0:t, :] = dy_ref[:, cols].astype(F32)
    ds[t:, :] = jnp.where(i < last, dnext_ref[0:HALO, cols].astype(F32), 0.0)
    x = x_ref[:, cols].astype(F32)
    dall = ds[...]
    dx = None
    for k in range(width):
        off = width - 1 - k
        shifted = dall[0:t, :] if off == 0 else pltpu.roll(dall, t + HALO - off, axis=0)[0:t, :]
        term = cw_ref[k:k + 1, cols] * shifted
        dx = term if dx is None else dx + term
        dcw_ref[k:k + 1, cols] += jnp.sum(shifted * x, axis=0, keepdims=True)
    return dx


def ffn_bwd_up(dhc, up, conv_w, w_up, h, dout, gamma, name, rider=None):
    n, c2 = up.shape
    d = h.shape[1]
    t = _pick(n, FFN_TILE, 8)
    last = n // t - 1
    chunks = _col_chunks(c2)
    tc = chunks[0].stop

    def body(dhc_ref, dnext_ref, up_ref, cw_ref, wu_ref, h_ref, dout_ref, g_ref,
             dh_ref, dup_ref, dcw_ref, dg_ref, ds):
        i = pl.program_id(0)

        @pl.when(i == 0)
        def _():
            dcw_ref[...] = jnp.zeros_like(dcw_ref)
            dg_ref[...] = jnp.zeros_like(dg_ref)
        dxn = jnp.zeros((t, d), F32)
        for cs in chunks:
            dup = conv_bwd_tail(dhc_ref, dnext_ref, up_ref, cw_ref, dcw_ref, ds, F_CONV, t, i, last, cs)
            dupb = dup.astype(BF16)
            dup_ref[:, cs] = dupb
            dxn = dxn + _dot_nt(dupb, wu_ref[:, cs])
        dx, dg = _rms_bwd(h_ref[...], g_ref[...], dxn)
        dh_ref[...] = dout_ref[...] + dx
        dg_ref[...] += dg

    return host_call(
        body, grid=(n // t,), rider=rider, sem=("arbitrary",), args=(dhc, dhc, up, conv_w, w_up, h, dout, gamma),
        in_specs=[pl.BlockSpec((t, c2), lambda i: (i, 0)),
                  pl.BlockSpec((_halo_rows(dhc.dtype), c2), _next_halo_map(t, n, _halo_rows(dhc.dtype))),
                  pl.BlockSpec((t, c2), lambda i: (i, 0)),
                  pl.BlockSpec((F_CONV, c2), lambda i: (0, 0)),
                  _resident((d, c2), lambda i: (0, 0)),
                  pl.BlockSpec((t, d), lambda i: (i, 0)),
                  pl.BlockSpec((t, d), lambda i: (i, 0)),
                  pl.BlockSpec((1, d), lambda i: (0, 0))],
        out_specs=[pl.BlockSpec((t, d), lambda i: (i, 0)), pl.BlockSpec((t, c2), lambda i: (i, 0)),
                   pl.BlockSpec((F_CONV, c2), lambda i: (0, 0)), pl.BlockSpec((1, d), lambda i: (0, 0))],
        out_shape=[jax.ShapeDtypeStruct((n, d), F32), jax.ShapeDtypeStruct((n, c2), BF16),
                   jax.ShapeDtypeStruct((F_CONV, c2), F32), jax.ShapeDtypeStruct((1, d), F32)],
        scratch_shapes=[pltpu.VMEM((t + HALO, tc), F32)], name=name)


def _gdn_head(uq, uk, uv, pba, alog, dtb, head, n_heads):
    lane = lax.broadcasted_iota(jnp.int32, pba.shape, 1)
    sq = uq * _sig(uq)
    q = sq * lax.rsqrt(jnp.sum(sq * sq, axis=-1, keepdims=True) + EPS) * (A_HEAD ** -0.5)
    sk = uk * _sig(uk)
    k = sk * lax.rsqrt(jnp.sum(sk * sk, axis=-1, keepdims=True) + EPS)
    v = uv * _sig(uv)
    beta = jnp.sum(jnp.where(lane == head, _sig(pba), 0.0), axis=-1, keepdims=True)
    g_all = -jnp.exp(alog) * _softplus(pba + dtb)
    g = jnp.sum(jnp.where(lane == n_heads + head, g_all, 0.0), axis=-1, keepdims=True)
    return q, k, v, jnp.broadcast_to(beta, uq.shape), jnp.broadcast_to(g, uq.shape)


def gdn_prep(proj, conv_w, alog, dtb, n_heads, name, rider=None):
    n = proj.shape[0]
    qk = n_heads * A_HEAD
    cw = 3 * qk
    ba_blk = (cw + qk) // LANES
    t = _pick(n, 256, 8)

    def body(x_ref, halo_ref, pba_ref, cw_ref, al_ref, dt_ref, q_ref, k_ref, v_ref, b_ref, g_ref, u_ref, xs):
        i = pl.program_id(0)
        xs[0:HALO, :] = jnp.where(i > 0, halo_ref[...], 0.0)
        xs[HALO:, :] = x_ref[...]
        pba = pba_ref[...]
        for hd in range(n_heads):
            s0 = slice(hd * A_HEAD, (hd + 1) * A_HEAD)
            s1 = slice(qk + hd * A_HEAD, qk + (hd + 1) * A_HEAD)
            s2 = slice(2 * qk + hd * A_HEAD, 2 * qk + (hd + 1) * A_HEAD)
            uq, uk, uv = [_causal_conv(xs, cw_ref, A_CONV, t, s, s) for s in (s0, s1, s2)]
            u_ref[:, s0] = uq.astype(BF16)
            u_ref[:, s1] = uk.astype(BF16)
            u_ref[:, s2] = uv.astype(BF16)
            q, k, v, bb, gb = _gdn_head(uq, uk, uv, pba, al_ref[...], dt_ref[...], hd, n_heads)
            q_ref[:, s0] = q
            k_ref[:, s0] = k
            v_ref[:, s0] = v
            b_ref[:, s0] = bb
            g_ref[:, s0] = gb

    out = jax.ShapeDtypeStruct((n, qk), F32)
    return host_call(
        body, grid=(n // t,),
        in_specs=[pl.BlockSpec((t, cw), lambda i: (i, 0)),
                  pl.BlockSpec((HALO, cw), _prev_halo_map(t)),
                  pl.BlockSpec((t, LANES), lambda i: (i, ba_blk)),
                  pl.BlockSpec((A_CONV, cw), lambda i: (0, 0)),
                  pl.BlockSpec((1, LANES), lambda i: (0, 0)),
                  pl.BlockSpec((1, LANES), lambda i: (0, 0))],
        out_specs=[pl.BlockSpec((t, qk), lambda i: (i, 0))] * 5 + [pl.BlockSpec((t, cw), lambda i: (i, 0))],
        out_shape=[out] * 5 + [jax.ShapeDtypeStruct((n, cw), BF16)],
        scratch_shapes=[pltpu.VMEM((t + HALO, cw), F32)], name=name,
        sem=("parallel",), args=(proj, proj, proj, conv_w, alog, dtb), rider=rider)


def gdn_prep_bwd(proj, u, alog, dtb, dq, dk, dv, dbb, dgb, n_heads, name):
    n = proj.shape[0]
    qk = n_heads * A_HEAD
    cw = 3 * qk
    ba_blk = (cw + qk) // LANES
    t = _pick(n, 256, 8)

    def body(u_ref, pba_ref, al_ref, dt_ref, dq_ref, dk_ref, dv_ref, dbb_ref, dgb_ref,
             du_ref, dba_ref, dal_ref, ddt_ref):
        i = pl.program_id(0)
        u = u_ref[...].astype(F32)
        pba = pba_ref[...]
        lane0 = lax.broadcasted_iota(jnp.int32, (t, A_HEAD), 1) == 0
        dba = jnp.zeros((t, LANES), F32)
        dal = jnp.zeros((1, LANES), F32)
        ddt = jnp.zeros((1, LANES), F32)
        for hd in range(n_heads):
            s0 = slice(hd * A_HEAD, (hd + 1) * A_HEAD)
            s1 = slice(qk + hd * A_HEAD, qk + (hd + 1) * A_HEAD)
            s2 = slice(2 * qk + hd * A_HEAD, 2 * qk + (hd + 1) * A_HEAD)
            fn = functools.partial(_gdn_head, head=hd, n_heads=n_heads)
            _, vjp = jax.vjp(fn, u[:, s0], u[:, s1], u[:, s2], pba, al_ref[...], dt_ref[...])
            cts = (dq_ref[:, s0], dk_ref[:, s0], dv_ref[:, s0],
                   jnp.where(lane0, dbb_ref[:, s0], 0.0), jnp.where(lane0, dgb_ref[:, s0], 0.0))
            duq, duk, duv, dpba, da, dd = vjp(cts)
            du_ref[:, s0] = duq
            du_ref[:, s1] = duk
            du_ref[:, s2] = duv
            dba = dba + dpba
            dal = dal + da
            ddt = ddt + dd
        dba_ref[...] = dba

        @pl.when(i == 0)
        def _():
            dal_ref[...] = jnp.zeros_like(dal_ref)
            ddt_ref[...] = jnp.zeros_like(ddt_ref)
        dal_ref[...] += dal
        ddt_ref[...] += ddt

    tok = pl.BlockSpec((t, qk), lambda i: (i, 0))
    row = pl.BlockSpec((1, LANES), lambda i: (0, 0))
    return pl.pallas_call(
        body, grid=(n // t,),
        in_specs=[pl.BlockSpec((t, cw), lambda i: (i, 0)),
                  pl.BlockSpec((t, LANES), lambda i: (i, ba_blk)), row, row,
                  tok, tok, tok, tok, tok],
        out_specs=[pl.BlockSpec((t, cw), lambda i: (i, 0)), pl.BlockSpec((t, LANES), lambda i: (i, 0)), row, row],
        out_shape=[jax.ShapeDtypeStruct((n, cw), F32), jax.ShapeDtypeStruct((n, LANES), F32),
                   jax.ShapeDtypeStruct((1, LANES), F32), jax.ShapeDtypeStruct((1, LANES), F32)],
        name=name, compiler_params=_cp(("arbitrary",)))(u, proj, alog, dtb, dq, dk, dv, dbb, dgb)


def conv_bwd(du, x, conv_w, width, name, rider=None):
    n, cw = du.shape
    t = _pick(n, 256, 8)
    last = n // t - 1

    chunks = _col_chunks(cw, LANES)
    tc = chunks[0].stop

    def body(du_ref, dnext_ref, x_ref, cw_ref, dx_ref, dcw_ref, ds):
        i = pl.program_id(0)

        @pl.when(i == 0)
        def _():
            dcw_ref[...] = jnp.zeros_like(dcw_ref)
        for cs in chunks:
            dx_ref[:, cs] = conv_bwd_tail(du_ref, dnext_ref, x_ref, cw_ref, dcw_ref, ds, width, t, i, last, cs)

    return host_call(
        body, grid=(n // t,),
        in_specs=[pl.BlockSpec((t, cw), lambda i: (i, 0)),
                  pl.BlockSpec((HALO, cw), _next_halo_map(t, n)),
                  pl.BlockSpec((t, cw), lambda i: (i, 0)),
                  pl.BlockSpec((width, cw), lambda i: (0, 0))],
        out_specs=[pl.BlockSpec((t, cw), lambda i: (i, 0)), pl.BlockSpec((width, cw), lambda i: (0, 0))],
        out_shape=[jax.ShapeDtypeStruct((n, cw), F32), jax.ShapeDtypeStruct((width, cw), F32)],
        scratch_shapes=[pltpu.VMEM((t + HALO, tc), F32)], name=name,
        sem=("arbitrary",), args=(du, du, x, conv_w), rider=rider)


def _b(x):
    return x.astype(BF16)


def _mm_nn(a, b):
    return _dot(_b(a), _b(b))


def _mm_nt(a, b):
    return _dot_nt(_b(a), _b(b))


def _mm_tn(a, b):
    return _dot_tn(_b(a), _b(b))


@jax.custom_vjp
def _mmg_nn(a, b):
    return _mm_nn(a, b)


_mmg_nn.defvjp(lambda a, b: (_mm_nn(a, b), (a, b)),
               lambda res, dc: (_mm_nt(dc, res[1]), _mm_tn(res[0], dc)))


@jax.custom_vjp
def _mmg_nt(a, b):
    return _mm_nt(a, b)


_mmg_nt.defvjp(lambda a, b: (_mm_nt(a, b), (a, b)),
               lambda res, dc: (_mm_nn(dc, res[1]), _mm_tn(dc, res[0])))


@jax.custom_vjp
def _mmg_tn(a, b):
    return _mm_tn(a, b)


_mmg_tn.defvjp(lambda a, b: (_mm_tn(a, b), (a, b)),
               lambda res, dc: (_mm_nt(res[1], dc), _mm_nn(res[0], dc)))


def _bf16_parts(x, n):
    parts = []
    for _ in range(n):
        p = x.astype(BF16)
        parts.append(p)
        x = x - p.astype(F32)
    return parts


def _dot_f32ish(a, b):
    (ah, al), (bh, bl) = _bf16_parts(a, 2), _bf16_parts(b, 2)
    return _dot(ah, bh) + _dot(ah, bl) + _dot(al, bh)


def _tri_dot(x, transpose):
    c = x.shape[0]
    low = lax.broadcasted_iota(jnp.int32, (c, c), 0) >= lax.broadcasted_iota(jnp.int32, (c, c), 1)
    tri = jnp.where(low, 1.0, 0.0).astype(BF16)
    mm = _dot_tn if transpose else _dot
    return functools.reduce(lambda a, b: a + b, [mm(tri, p) for p in _bf16_parts(x, 3)])


def _cumsum(x):
    return _tri_dot(x, False)


@jax.custom_vjp
def _cumsum_g(x):
    return _tri_dot(x, False)


_cumsum_g.defvjp(lambda x: (_tri_dot(x, False), None), lambda _, ct: (_tri_dot(ct, True),))


def _each(f, *lists):
    return [f(*a) for a in zip(*lists)]


def _unit_lower_inv(ms):
    c = ms[0].shape[0]
    eye = jnp.where(lax.broadcasted_iota(jnp.int32, (c, c), 0) == lax.broadcasted_iota(jnp.int32, (c, c), 1), 1.0, 0.0)
    xs = [eye - m for m in ms]
    pws = _each(_mm_nn, ms, ms)
    for it in range(5):
        xs = _each(lambda x, pw: x + _mm_nn(x, pw), xs, pws)
        if it < 4:
            pws = _each(_mm_nn, pws, pws)
    rs = _each(lambda m, x: eye - x - _dot_f32ish(m, x), ms, xs)
    return _each(lambda x, r: x + _mm_nn(x, r), xs, rs)


@jax.custom_vjp
def _saved_inv_g(ms, xs):
    return xs


_saved_inv_g.defvjp(lambda ms, xs: (xs, xs),
                    lambda xs, dxs: (_each(lambda t, x: -_mm_nt(t, x), _each(_mm_tn, xs, dxs), xs),
                                     [jnp.zeros_like(x) for x in xs]))


def _gdn_chunk(ops, state, q, k, v, bb, gb):
    nn, nt, tn, inv, cum = ops
    c = CHUNK
    ri = lax.broadcasted_iota(jnp.int32, (c, c), 0)
    ci = lax.broadcasted_iota(jnp.int32, (c, c), 1)
    causal = ri >= ci
    strict = ri > ci
    gc = [cum(g) for g in gb]
    decay = [jnp.where(causal, jnp.exp(jnp.where(causal, x[:, :c] - x.T[:c, :], 0.0)), 0.0) for x in gc]
    kb = _each(lambda a, b: a * b, k, bb)
    kk = _each(nt, kb, k)
    m = _each(lambda a, d: jnp.where(strict, a * d, 0.0), kk, decay)
    tinv = inv(m)
    egc = [jnp.exp(x) for x in gc]
    u = _each(nn, tinv, _each(lambda a, b: a * b, v, bb))
    w = _each(nn, tinv, _each(lambda a, b: a * b, kb, egc))
    attn = _each(lambda a, d: a * d, _each(nt, q, k), decay)
    glast = [jnp.sum(g, axis=0, keepdims=True) for g in gb]
    ws = _each(nn, w, state)
    v_new = _each(lambda a, b: a - b, u, ws)
    qs = _each(nn, _each(lambda a, b: a * b, q, egc), state)
    av = _each(nn, attn, v_new)
    o = _each(lambda a, b: a + b, qs, av)
    kv = _each(tn, _each(lambda a, gl, x: a * jnp.exp(gl - x), k, glast, gc), v_new)
    new_state = _each(lambda s, gl, a: s * jnp.exp(gl) + a, state, glast, kv)
    return o, new_state


def gdn_fwd(q, k, v, bb, gb, n_heads, name, rider=None):
    n, w = q.shape
    nc = n // CHUNK
    cb = min(8, nc)
    rows = cb * CHUNK

    def body(q_ref, k_ref, v_ref, b_ref, g_ref, o_ref, st_ref, ti_ref, s_scr):
        @pl.when(pl.program_id(0) == 0)
        def _():
            s_scr[...] = jnp.zeros_like(s_scr)

        def step(c, carry):
            sl = pl.ds(pl.multiple_of(c * CHUNK, CHUNK), CHUNK)
            lanes = [slice(hd * A_HEAD, (hd + 1) * A_HEAD) for hd in range(n_heads)]
            state = [s_scr[hd] for hd in range(n_heads)]
            inverses = []

            def inv(ms):
                inverses.extend(_unit_lower_inv(ms))
                return inverses

            o, new_state = _gdn_chunk((_mm_nn, _mm_nt, _mm_tn, inv, _cumsum), state,
                                      *[[r[sl, ls] for ls in lanes] for r in (q_ref, k_ref, v_ref, b_ref, g_ref)])
            for hd, ls in enumerate(lanes):
                st_ref[hd, pl.ds(c, 1)] = state[hd][None]
                ti_ref[hd, pl.ds(c, 1)] = inverses[hd].astype(BF16)[None]
                o_ref[sl, ls] = o[hd]
                s_scr[hd] = new_state[hd]
            return carry

        lax.fori_loop(0, cb, step, 0)

    tok = pl.BlockSpec((rows, w), lambda j: (j, 0))
    return host_call(
        body, grid=(nc // cb,),
        in_specs=[tok] * 5,
        out_specs=[tok, pl.BlockSpec((n_heads, cb, A_HEAD, A_HEAD), lambda j: (0, j, 0, 0)),
                   pl.BlockSpec((n_heads, cb, CHUNK, CHUNK), lambda j: (0, j, 0, 0))],
        out_shape=[jax.ShapeDtypeStruct(q.shape, F32), jax.ShapeDtypeStruct((n_heads, nc, A_HEAD, A_HEAD), F32),
                   jax.ShapeDtypeStruct((n_heads, nc, CHUNK, CHUNK), BF16)],
        scratch_shapes=[pltpu.VMEM((n_heads, A_HEAD, A_HEAD), F32)], name=name,
        sem=("arbitrary",), args=(q, k, v, bb, gb), rider=rider)


def gdn_bwd(q, k, v, bb, gb, states, tinv, do, n_heads, name, rider=None):
    n, w = q.shape
    nc = n // CHUNK
    cb = min(4, nc)
    rows = cb * CHUNK
    nblk = nc // cb

    def body(q_ref, k_ref, v_ref, b_ref, g_ref, st_ref, ti_ref, do_ref,
             dq_ref, dk_ref, dv_ref, db_ref, dg_ref, ds_scr):
        @pl.when(pl.program_id(0) == 0)
        def _():
            ds_scr[...] = jnp.zeros_like(ds_scr)

        def step(s, carry):
            c = cb - 1 - s
            sl = pl.ds(pl.multiple_of(c * CHUNK, CHUNK), CHUNK)
            for h0 in range(0, n_heads, GDN_BWD_HEADS):
                heads = list(range(h0, min(h0 + GDN_BWD_HEADS, n_heads)))
                lanes = [slice(hd * A_HEAD, (hd + 1) * A_HEAD) for hd in heads]
                state = [st_ref[hd, pl.ds(c, 1)][0] for hd in heads]
                saved = [ti_ref[hd, pl.ds(c, 1)][0].astype(F32) for hd in heads]
                chunk_fn = functools.partial(
                    _gdn_chunk, (_mmg_nn, _mmg_nt, _mmg_tn, lambda ms: _saved_inv_g(ms, saved), _cumsum_g))
                _, vjp = jax.vjp(chunk_fn, state, *[[r[sl, ls] for ls in lanes]
                                                    for r in (q_ref, k_ref, v_ref, b_ref, g_ref)])
                dstate, dq, dk, dv, dbb, dgb = vjp(([do_ref[sl, ls] for ls in lanes], [ds_scr[hd] for hd in heads]))
                for u, (hd, ls) in enumerate(zip(heads, lanes)):
                    ds_scr[hd] = dstate[u]
                    dq_ref[sl, ls] = dq[u]
                    dk_ref[sl, ls] = dk[u]
                    dv_ref[sl, ls] = dv[u]
                    db_ref[sl, ls] = jnp.broadcast_to(jnp.sum(dbb[u], axis=-1, keepdims=True), dbb[u].shape)
                    dg_ref[sl, ls] = jnp.broadcast_to(jnp.sum(dgb[u], axis=-1, keepdims=True), dgb[u].shape)
            return carry

        lax.fori_loop(0, cb, step, 0)

    tok = pl.BlockSpec((rows, w), lambda j: (nblk - 1 - j, 0))
    out = jax.ShapeDtypeStruct(q.shape, F32)
    return host_call(
        body, grid=(nblk,),
        in_specs=[tok] * 5 + [pl.BlockSpec((n_heads, cb, A_HEAD, A_HEAD), lambda j: (0, nblk - 1 - j, 0, 0)),
                              pl.BlockSpec((n_heads, cb, CHUNK, CHUNK), lambda j: (0, nblk - 1 - j, 0, 0)), tok],
        out_specs=[tok] * 5, out_shape=[out] * 5,
        scratch_shapes=[pltpu.VMEM((n_heads, A_HEAD, A_HEAD), F32)], name=name,
        sem=("arbitrary",), args=(q, k, v, bb, gb, states, tinv, do), rider=rider)


def _gdn_gate(oh, zh, w):
    r = lax.rsqrt(jnp.mean(oh * oh, axis=-1, keepdims=True) + EPS)
    return oh * r * w * (zh * _sig(zh))


def gdn_out(o, proj, out_norm, w_out, h, n_heads, name):
    n, vw = o.shape
    d = h.shape[1]
    z_blk = 3 * vw // vw
    t = _pick(n, 512, 8)

    def body(o_ref, z_ref, w_ref, wo_ref, h_ref, out_ref, y_ref):
        for hd in range(n_heads):
            s0 = slice(hd * A_HEAD, (hd + 1) * A_HEAD)
            y_ref[:, s0] = _gdn_gate(o_ref[:, s0], z_ref[:, s0], w_ref[...]).astype(BF16)
        out_ref[...] = h_ref[...] + _dot(y_ref[...], wo_ref[...])

    return pl.pallas_call(
        body, grid=(n // t,),
        in_specs=[pl.BlockSpec((t, vw), lambda i: (i, 0)),
                  pl.BlockSpec((t, vw), lambda i: (i, z_blk)),
                  pl.BlockSpec((1, A_HEAD), lambda i: (0, 0)),
                  _resident((vw, d), lambda i: (0, 0)),
                  pl.BlockSpec((t, d), lambda i: (i, 0))],
        out_specs=[pl.BlockSpec((t, d), lambda i: (i, 0)), pl.BlockSpec((t, vw), lambda i: (i, 0))],
        out_shape=[jax.ShapeDtypeStruct((n, d), F32), jax.ShapeDtypeStruct((n, vw), BF16)], name=name,
        compiler_params=_cp(("parallel",)))(o, proj, out_norm, w_out, h)


def gdn_out_bwd(dout, o, proj, out_norm, w_out, n_heads, name):
    n, vw = o.shape
    d = dout.shape[1]
    z_blk = 3
    t = _pick(n, 512, 8)

    def body(dout_ref, o_ref, z_ref, w_ref, wo_ref, do_ref, dz_ref, dw_ref):
        dy = _dot_nt(dout_ref[...].astype(BF16), wo_ref[...])
        dw = jnp.zeros((1, A_HEAD), F32)
        for hd in range(n_heads):
            s0 = slice(hd * A_HEAD, (hd + 1) * A_HEAD)
            _, vjp = jax.vjp(_gdn_gate, o_ref[:, s0], z_ref[:, s0], w_ref[...])
            doh, dzh, dwh = vjp(dy[:, s0])
            do_ref[:, s0] = doh
            dz_ref[:, s0] = dzh
            dw = dw + dwh

        @pl.when(pl.program_id(0) == 0)
        def _():
            dw_ref[...] = jnp.zeros_like(dw_ref)
        dw_ref[...] += dw

    tok = pl.BlockSpec((t, vw), lambda i: (i, 0))
    return pl.pallas_call(
        body, grid=(n // t,),
        in_specs=[pl.BlockSpec((t, d), lambda i: (i, 0)), tok,
                  pl.BlockSpec((t, vw), lambda i: (i, z_blk)),
                  pl.BlockSpec((1, A_HEAD), lambda i: (0, 0)),
                  _resident((vw, d), lambda i: (0, 0))],
        out_specs=[tok, tok, pl.BlockSpec((1, A_HEAD), lambda i: (0, 0))],
        out_shape=[jax.ShapeDtypeStruct((n, vw), F32), jax.ShapeDtypeStruct((n, vw), F32),
                   jax.ShapeDtypeStruct((1, A_HEAD), F32)], name=name,
        compiler_params=_cp(("arbitrary",)))(dout, o, proj, out_norm, w_out)


BIAS_LINE = 768
BIAS_TOP = BAND + CHUNK - 2


def _bias_line_onehot(clip, tbl_pad):
    r = lax.broadcasted_iota(jnp.int32, (tbl_pad, BIAS_LINE), 0)
    v = lax.broadcasted_iota(jnp.int32, (tbl_pad, BIAS_LINE), 1)
    idx = jnp.clip(BIAS_TOP - v - (CHUNK - 1), -clip, clip) + clip
    return jnp.where((r == idx) & (v <= BIAS_TOP), 1.0, 0.0)


def bias_expand(tbl, clip, name):
    nh, tp = tbl.shape

    def body(t_ref, o_ref):
        line = _hdot(t_ref[...], _bias_line_onehot(clip, tp))
        keep = lax.broadcasted_iota(jnp.int32, (nh, BAND_PAD), 1) < BAND
        for i in range(CHUNK):
            s = CHUNK - 1 - i
            rolled = line if s == 0 else pltpu.roll(line, BIAS_LINE - s, axis=1)
            o_ref[i] = jnp.where(keep, rolled[:, :BAND_PAD], NEG_INF)

    return pl.pallas_call(
        body, in_specs=[pl.BlockSpec(memory_space=pltpu.VMEM)], out_specs=pl.BlockSpec(memory_space=pltpu.VMEM),
        out_shape=jax.ShapeDtypeStruct((CHUNK, nh, BAND_PAD), F32), name=name, compiler_params=_cp())(tbl)


def bias_expand_bwd(dbias, clip, tp, name):
    _, nh, _ = dbias.shape

    def body(d_ref, o_ref):
        keep = lax.broadcasted_iota(jnp.int32, (nh, BAND_PAD), 1) < BAND
        pad = jnp.zeros((nh, BIAS_LINE - BAND_PAD), F32)
        acc = jnp.zeros((nh, BIAS_LINE), F32)
        for i in range(CHUNK):
            s = CHUNK - 1 - i
            d = jnp.concatenate([jnp.where(keep, d_ref[i], 0.0), pad], axis=1)
            acc = acc + (d if s == 0 else pltpu.roll(d, s, axis=1))
        o_ref[...] = _hdot_nt(acc, _bias_line_onehot(clip, tp))

    return pl.pallas_call(
        body, in_specs=[pl.BlockSpec(memory_space=pltpu.VMEM)], out_specs=pl.BlockSpec(memory_space=pltpu.VMEM),
        out_shape=jax.ShapeDtypeStruct((nh, tp), F32), name=name, compiler_params=_cp())(dbias)


ATT_TILE = LEFT_CHUNKS * CHUNK


ATT_GROUP = 8


def _att_softmax(s, bias, n_chunk):
    slot = lax.broadcasted_iota(jnp.int32, (1, s.shape[1]), 1)
    before_start = jnp.where(slot < (LEFT_CHUNKS - n_chunk) * CHUNK, NEG_INF, 0.0)
    s = s + bias + before_start
    p = jnp.exp(s - jnp.max(s, axis=-1, keepdims=True))
    return p / jnp.sum(p, axis=-1, keepdims=True)


def _att_specs(n_pairs):
    prev = lambda p, i: (jnp.maximum(i - 1, 0), p)
    cur = lambda p, i: (i, p)
    prev_v = lambda p, i: (jnp.maximum(i - 1, 0), n_pairs + p)
    cur_v = lambda p, i: (i, n_pairs + p)
    blk = (ATT_TILE, LANES)
    return [pl.BlockSpec(blk, prev), pl.BlockSpec(blk, cur), pl.BlockSpec(blk, prev_v), pl.BlockSpec(blk, cur_v)]


def _att_fill(kbuf, vbuf, kp_ref, kc_ref, vp_ref, vc_ref):
    t = ATT_TILE
    kbuf[0:t, :] = kp_ref[...].astype(BF16)
    kbuf[t:2 * t, :] = kc_ref[...].astype(BF16)
    kbuf[2 * t:, :] = jnp.zeros((CHUNK, LANES), BF16)
    vbuf[0:t, :] = vp_ref[...].astype(BF16)
    vbuf[t:2 * t, :] = vc_ref[...].astype(BF16)
    vbuf[2 * t:, :] = jnp.zeros((CHUNK, LANES), BF16)


def _stack_heads(x, first):
    return jnp.concatenate([jnp.where(first, x, 0.0), jnp.where(first, 0.0, x)], axis=0).astype(BF16)


def attn_fwd(qp, kv, bias, name, rider=None):
    n, bw = qp.shape
    n_pairs = bw // LANES
    t = ATT_TILE
    cpt = t // CHUNK

    def body(q_ref, kp_ref, kc_ref, vp_ref, vc_ref, b_ref, o_ref, kbuf, vbuf):
        i = pl.program_id(1)
        _att_fill(kbuf, vbuf, kp_ref, kc_ref, vp_ref, vc_ref)
        lane = lax.broadcasted_iota(jnp.int32, (CHUNK, LANES), 1)
        first = lane < B_HEAD
        for g0 in range(0, cpt, ATT_GROUP):
            chunks = list(range(g0, min(g0 + ATT_GROUP, cpt)))
            band = [slice(c * CHUNK, c * CHUNK + BAND_PAD) for c in chunks]
            q2 = [_stack_heads(q_ref[c * CHUNK:(c + 1) * CHUNK, :] * (B_HEAD ** -0.5), first) for c in chunks]
            s = [_dot_nt(q_u, kbuf[b_u, :]) for q_u, b_u in zip(q2, band)]
            p = [_att_softmax(s_u, b_ref[0], i * cpt + c) for s_u, c in zip(s, chunks)]
            o = [_dot(p_u.astype(BF16), vbuf[b_u, :]) for p_u, b_u in zip(p, band)]
            for o_u, c in zip(o, chunks):
                o_ref[c * CHUNK:(c + 1) * CHUNK, :] = jnp.where(first, o_u[:CHUNK], o_u[CHUNK:])

    return host_call(
        body, grid=(n_pairs, n // t),
        in_specs=[pl.BlockSpec((t, LANES), lambda p, i: (i, p))] + _att_specs(n_pairs)
        + [pl.BlockSpec((1, 2 * CHUNK, BAND_PAD), lambda p, i: (p, 0, 0))],
        out_specs=[pl.BlockSpec((t, LANES), lambda p, i: (i, p))],
        out_shape=[jax.ShapeDtypeStruct((n, bw), F32)],
        scratch_shapes=[pltpu.VMEM((2 * t + CHUNK, LANES), BF16), pltpu.VMEM((2 * t + CHUNK, LANES), BF16)],
        name=name, sem=("parallel", "parallel"), args=(qp, kv, kv, kv, kv, bias), rider=rider)


def attn_bwd(qp, kv, bias, d_o, dk_in, dv_in, name, rider=None):
    n, bw = qp.shape
    n_pairs = bw // LANES
    t = ATT_TILE
    cpt = t // CHUNK
    nt = n // t
    have_in = dk_in is not None
    scale = B_HEAD ** -0.5

    def body(*refs):
        q_ref, kp_ref, kc_ref, vp_ref, vc_ref, b_ref, do_ref = refs[:7]
        pos = 7
        if have_in:
            dki_ref, dvi_ref = refs[7:9]
            pos = 9
        dq_ref, dk_ref, dv_ref, db_ref, kbuf, vbuf, dkacc, dvacc = refs[pos:]
        j = pl.program_id(1)
        i = nt - 1 - j
        _att_fill(kbuf, vbuf, kp_ref, kc_ref, vp_ref, vc_ref)

        @pl.when(j == 0)
        def _():
            dkacc[...] = jnp.zeros_like(dkacc)
            dvacc[...] = jnp.zeros_like(dvacc)
            db_ref[...] = jnp.zeros_like(db_ref)

        @pl.when(j > 0)
        def _():
            dkacc[t:2 * t, :] = dkacc[0:t, :]
            dvacc[t:2 * t, :] = dvacc[0:t, :]
            dkacc[0:t, :] = jnp.zeros((t, LANES), F32)
            dvacc[0:t, :] = jnp.zeros((t, LANES), F32)

        lane = lax.broadcasted_iota(jnp.int32, (CHUNK, LANES), 1)
        first = lane < B_HEAD
        for g0 in range(0, cpt, ATT_GROUP):
            chunks = list(range(g0, min(g0 + ATT_GROUP, cpt)))
            rows = [slice(c * CHUNK, (c + 1) * CHUNK) for c in chunks]
            band = [slice(c * CHUNK, c * CHUNK + BAND_PAD) for c in chunks]
            q2 = [_stack_heads(q_ref[r, :] * scale, first) for r in rows]
            do2 = [_stack_heads(do_ref[r, :], first) for r in rows]
            s = [_dot_nt(q_u, kbuf[b_u, :]) for q_u, b_u in zip(q2, band)]
            dp = [_dot_nt(d_u, vbuf[b_u, :]) for d_u, b_u in zip(do2, band)]
            p = [_att_softmax(s_u, b_ref[0], i * cpt + c) for s_u, c in zip(s, chunks)]
            ds = [p_u * (dp_u - jnp.sum(dp_u * p_u, axis=-1, keepdims=True)) for p_u, dp_u in zip(p, dp)]
            dsb = [d_u.astype(BF16) for d_u in ds]
            dv = [_dot_tn(p_u.astype(BF16), d_u) for p_u, d_u in zip(p, do2)]
            dq = [_dot(d_u, kbuf[b_u, :]) * scale for d_u, b_u in zip(dsb, band)]
            dk = [_dot_tn(d_u, q_u) for d_u, q_u in zip(dsb, q2)]
            db_ref[0] += functools.reduce(lambda a, b: a + b, ds)
            for r in range(chunks[0], chunks[-1] + BAND // CHUNK):
                terms = [(u, r - c) for u, c in enumerate(chunks) if 0 <= r - c < BAND // CHUNK]
                blk = slice(r * CHUNK, (r + 1) * CHUNK)
                dvacc[blk, :] += functools.reduce(lambda a, b: a + b, [dv[u][o * CHUNK:(o + 1) * CHUNK] for u, o in terms])
                dkacc[blk, :] += functools.reduce(lambda a, b: a + b, [dk[u][o * CHUNK:(o + 1) * CHUNK] for u, o in terms])
            for u in range(len(chunks)):
                dq_ref[rows[u], :] = jnp.where(first, dq[u][:CHUNK], dq[u][CHUNK:])

        if have_in:
            dk_ref[...] = dkacc[t:2 * t, :] + dki_ref[...]
            dv_ref[...] = dvacc[t:2 * t, :] + dvi_ref[...]
        else:
            dk_ref[...] = dkacc[t:2 * t, :]
            dv_ref[...] = dvacc[t:2 * t, :]

    rev = lambda p, j: (nt - 1 - j, p)
    tok = pl.BlockSpec((t, LANES), rev)
    kv_specs = [pl.BlockSpec((t, LANES), lambda p, j: (jnp.maximum(nt - 2 - j, 0), p)),
                pl.BlockSpec((t, LANES), rev),
                pl.BlockSpec((t, LANES), lambda p, j: (jnp.maximum(nt - 2 - j, 0), n_pairs + p)),
                pl.BlockSpec((t, LANES), lambda p, j: (nt - 1 - j, n_pairs + p))]
    in_specs = [tok] + kv_specs + [pl.BlockSpec((1, 2 * CHUNK, BAND_PAD), lambda p, j: (p, 0, 0)), tok]
    args = [qp, kv, kv, kv, kv, bias, d_o]
    if have_in:
        in_specs += [tok, tok]
        args += [dk_in, dv_in]
    out = jax.ShapeDtypeStruct((n, bw), F32)
    return host_call(
        body, grid=(n_pairs, nt), in_specs=in_specs,
        out_specs=[tok, tok, tok, pl.BlockSpec((1, 2 * CHUNK, BAND_PAD), lambda p, j: (p, 0, 0))],
        out_shape=[out, out, out, jax.ShapeDtypeStruct((n_pairs, 2 * CHUNK, BAND_PAD), F32)],
        scratch_shapes=[pltpu.VMEM((2 * t + CHUNK, LANES), BF16), pltpu.VMEM((2 * t + CHUNK, LANES), BF16),
                        pltpu.VMEM((2 * t + CHUNK, LANES), F32), pltpu.VMEM((2 * t + CHUNK, LANES), F32)],
        name=name, sem=("parallel", "arbitrary"), args=args, rider=rider)


def adamw(w, gstack, m, v, name):
    r, c = w.shape
    s = gstack.shape[0]
    tr = _pick(r, 512, 8)

    def body(w_ref, g_ref, m_ref, v_ref, go_ref, d_ref, mo_ref, vo_ref):
        g = g_ref[0].astype(F32)
        for k in range(1, s):
            g = g + g_ref[k].astype(F32)
        mn = ADAM_B1 * m_ref[...] + (1.0 - ADAM_B1) * g
        vn = ADAM_B2 * v_ref[...] + (1.0 - ADAM_B2) * (g * g)
        m_hat = mn / (1.0 - ADAM_B1 ** ADAM_STEP)
        v_hat = vn / (1.0 - ADAM_B2 ** ADAM_STEP)
        go_ref[...] = g
        d_ref[...] = -ADAM_LR * (m_hat / (jnp.sqrt(v_hat) + ADAM_EPS) + ADAM_WD * w_ref[...])
        mo_ref[...] = mn
        vo_ref[...] = vn

    blk = pl.BlockSpec((tr, c), lambda i: (i, 0))
    out = jax.ShapeDtypeStruct((r, c), F32)
    return pl.pallas_call(
        body, grid=(r // tr,),
        in_specs=[blk, pl.BlockSpec((s, tr, c), lambda i: (0, i, 0)), blk, blk],
        out_specs=[blk] * 4, out_shape=[out] * 4, name=name,
        compiler_params=_cp(("parallel",)))(w, gstack, m, v)


def _place():
    x, y, c = lax.axis_index("x"), lax.axis_index("y"), lax.axis_index("c")
    chips = [(1 - x, y), (x, 1 - y), (1 - x, 1 - y)]
    return x, y, c, chips


def _ag_copy(outs, send_sems, recv_sems, t, k, block, to, src=None):
    def slot(dev):
        return outs[t].at[4 * dev[0] + 2 * dev[1] + dev[2]]
    return pltpu.make_async_remote_copy(
        src_ref=slot(block) if src is None else src, dst_ref=slot(block),
        send_sem=send_sems.at[7 * t + k], recv_sem=recv_sems.at[7 * t + k], device_id=to, device_id_type=MESH)


def _ag_start(ins, outs, send_sems, recv_sems, local_sems):
    x, y, c, chips = _place()
    me = (x, y, c)
    for t in range(len(ins)):
        pltpu.make_async_copy(ins[t], outs[t].at[4 * x + 2 * y + c], local_sems.at[t]).start()
        _ag_copy(outs, send_sems, recv_sems, t, 0, me, (x, y, 1 - c), src=ins[t]).start()
        for j, chip in enumerate(chips):
            _ag_copy(outs, send_sems, recv_sems, t, 1 + j, me, (*chip, c), src=ins[t]).start()


def _ag_finish(ins, outs, send_sems, recv_sems, local_sems):
    x, y, c, chips = _place()
    me, sibling = (x, y, c), (x, y, 1 - c)
    nt = len(ins)
    for t in range(nt):
        for j, chip in enumerate(chips):
            _ag_copy(outs, send_sems, recv_sems, t, 1 + j, (*chip, c), me).wait_recv()
            _ag_copy(outs, send_sems, recv_sems, t, 4 + j, (*chip, c), sibling).start()
    for t in range(nt):
        _ag_copy(outs, send_sems, recv_sems, t, 0, sibling, me).wait_recv()
        for j, chip in enumerate(chips):
            _ag_copy(outs, send_sems, recv_sems, t, 4 + j, (*chip, 1 - c), me).wait_recv()
    for t in range(nt):
        _ag_copy(outs, send_sems, recv_sems, t, 0, me, sibling, src=ins[t]).wait_send()
        for j, chip in enumerate(chips):
            _ag_copy(outs, send_sems, recv_sems, t, 1 + j, me, (*chip, c), src=ins[t]).wait_send()
            _ag_copy(outs, send_sems, recv_sems, t, 4 + j, (*chip, c), sibling).wait_send()
        pltpu.make_async_copy(ins[t], outs[t].at[4 * x + 2 * y + c], local_sems.at[t]).wait()


def _rs_a_copy(ins, outs, send_sems, recv_sems, t, q):
    x, y, c, _ = _place()
    return pltpu.make_async_remote_copy(
        src_ref=ins[t].at[2 * q + (1 - c)], dst_ref=outs[t].at[q],
        send_sem=send_sems.at[4 * t + q], recv_sem=recv_sems.at[4 * t + q],
        device_id=(x, y, 1 - c), device_id_type=MESH)


def _rs_a_start(ins, outs, send_sems, recv_sems, local_sems):
    for t in range(len(ins)):
        for q in range(4):
            _rs_a_copy(ins, outs, send_sems, recv_sems, t, q).start()


def _rs_a_finish(ins, outs, send_sems, recv_sems, local_sems):
    for t in range(len(ins)):
        for q in range(4):
            _rs_a_copy(ins, outs, send_sems, recv_sems, t, q).wait_recv()
    for t in range(len(ins)):
        for q in range(4):
            _rs_a_copy(ins, outs, send_sems, recv_sems, t, q).wait_send()


def _rs_b_copy(ins, outs, send_sems, recv_sems, t, j, sending):
    x, y, c, chips = _place()
    mine, other = 2 * x + y, 2 * chips[j][0] + chips[j][1]
    return pltpu.make_async_remote_copy(
        src_ref=ins[t].at[other if sending else mine], dst_ref=outs[t].at[mine if sending else other],
        send_sem=send_sems.at[3 * t + j], recv_sem=recv_sems.at[3 * t + j],
        device_id=(*chips[j], c), device_id_type=MESH)


def _rs_b_start(ins, outs, send_sems, recv_sems, local_sems):
    x, y, _, _ = _place()
    for t in range(len(ins)):
        for j in range(3):
            _rs_b_copy(ins, outs, send_sems, recv_sems, t, j, True).start()
        pltpu.make_async_copy(ins[t].at[2 * x + y], outs[t].at[2 * x + y], local_sems.at[t]).start()


def _rs_b_finish(ins, outs, send_sems, recv_sems, local_sems):
    x, y, _, _ = _place()
    for t in range(len(ins)):
        for j in range(3):
            _rs_b_copy(ins, outs, send_sems, recv_sems, t, j, False).wait_recv()
    for t in range(len(ins)):
        for j in range(3):
            _rs_b_copy(ins, outs, send_sems, recv_sems, t, j, True).wait_send()
        pltpu.make_async_copy(ins[t].at[2 * x + y], outs[t].at[2 * x + y], local_sems.at[t]).wait()


_EXCHANGES = {
    "all_gather": (7, lambda a: (N_DEV, *a.shape), _ag_start, _ag_finish),
    "rs_sibling": (4, lambda a: (4, *a.shape[1:]), _rs_a_start, _rs_a_finish),
    "rs_chips": (3, lambda a: a.shape, _rs_b_start, _rs_b_finish),
}


def _exchange_parts(kind, arrays):
    per, shape_of, start, finish = _EXCHANGES[kind]
    n = len(arrays)
    out_shape = [jax.ShapeDtypeStruct(shape_of(a), a.dtype) for a in arrays]
    sems = [pltpu.SemaphoreType.DMA((per * n,)), pltpu.SemaphoreType.DMA((per * n,)), pltpu.SemaphoreType.DMA((n,))]
    return out_shape, sems, start, finish


def exchange(kind, arrays, name):
    n = len(arrays)
    out_shape, sems, start, finish = _exchange_parts(kind, arrays)
    any_spec = pl.BlockSpec(memory_space=pl.ANY)

    def body(*refs):
        ins, outs, sem_refs = refs[:n], refs[n:2 * n], refs[2 * n:]
        start(ins, outs, *sem_refs)
        finish(ins, outs, *sem_refs)

    return pl.pallas_call(body, in_specs=[any_spec] * n, out_specs=[any_spec] * n, out_shape=out_shape,
                          scratch_shapes=sems, name=name)(*arrays)


def host_call(body, *, grid, in_specs, out_specs, out_shape, scratch_shapes, args, name, sem, rider=None):
    if not rider:
        outs = pl.pallas_call(body, grid=grid, in_specs=in_specs, out_specs=out_specs, out_shape=out_shape,
                              scratch_shapes=scratch_shapes, name=name, compiler_params=_cp(sem))(*args)
        return outs, None
    riders = [rider] if isinstance(rider, tuple) else list(rider)
    arrays = [a for _, arrs in riders for a in arrs]
    parts = [_exchange_parts(kind, arrs) for kind, arrs in riders]
    counts = [len(arrs) for _, arrs in riders]
    nr, ni, no, ns = len(arrays), len(in_specs), len(out_specs), len(scratch_shapes)
    any_spec = pl.BlockSpec(memory_space=pl.ANY)

    def wrapped(*refs):
        ins, r_ins = refs[:ni], refs[ni:ni + nr]
        outs, r_outs = refs[ni + nr:ni + nr + no], refs[ni + nr + no:ni + 2 * nr + no]
        scratch, sem_refs = refs[ni + 2 * nr + no:ni + 2 * nr + no + ns], refs[ni + 2 * nr + no + ns:]
        first = pl.program_id(0) == 0
        last = pl.program_id(0) == grid[0] - 1
        for ax in range(1, len(grid)):
            first = first & (pl.program_id(ax) == 0)
            last = last & (pl.program_id(ax) == grid[ax] - 1)

        def each(which):
            pos = 0
            for e, (cnt, part) in enumerate(zip(counts, parts)):
                part[which](r_ins[pos:pos + cnt], r_outs[pos:pos + cnt], *sem_refs[3 * e:3 * e + 3])
                pos += cnt

        @pl.when(first)
        def _():
            each(2)
        body(*ins, *outs, *scratch)

        @pl.when(last)
        def _():
            each(3)

    outs = pl.pallas_call(
        wrapped, grid=grid, in_specs=list(in_specs) + [any_spec] * nr, out_specs=list(out_specs) + [any_spec] * nr,
        out_shape=list(out_shape) + [s for p in parts for s in p[0]],
        scratch_shapes=list(scratch_shapes) + [s for p in parts for s in p[1]], name=name,
        compiler_params=_cp(("arbitrary",) * len(grid)))(*args, *arrays)
    got, pos = [], no
    for cnt in counts:
        got.append(outs[pos:pos + cnt])
        pos += cnt
    return outs[:no], (got[0] if isinstance(rider, tuple) else got)


def pair_add(g8, recv, c_idx, name):
    _, r, c = g8.shape
    tr = _pick(r, 512, 8)

    def body(c_ref, g_ref, r_ref, o_ref):
        o_ref[...] = (g_ref[...].astype(F32) + r_ref[...].astype(F32)).astype(BF16)

    return pl.pallas_call(
        body,
        grid_spec=pltpu.PrefetchScalarGridSpec(
            num_scalar_prefetch=1, grid=(4, r // tr),
            in_specs=[pl.BlockSpec((1, tr, c), lambda q, i, cr: (2 * q + cr[0], i, 0)),
                      pl.BlockSpec((1, tr, c), lambda q, i, cr: (q, i, 0))],
            out_specs=pl.BlockSpec((1, tr, c), lambda q, i, cr: (q, i, 0))),
        out_shape=jax.ShapeDtypeStruct((4, r, c), BF16), name=name,
        compiler_params=_cp(("parallel", "parallel")))(c_idx, g8, recv)


def all_reduce_small(pack, name):
    r, c = pack.shape

    def body(x_ref, o_ref, buf, send_sems, recv_sems, local_sem):
        x, y, cc, chips = _place()
        me, sibling = (x, y, cc), (x, y, 1 - cc)

        def slot(dev):
            return buf.at[4 * dev[0] + 2 * dev[1] + dev[2]]

        def copy(k, block, to, src=None):
            return pltpu.make_async_remote_copy(
                src_ref=slot(block) if src is None else src, dst_ref=slot(block),
                send_sem=send_sems.at[k], recv_sem=recv_sems.at[k], device_id=to, device_id_type=MESH)

        mine = pltpu.make_async_copy(x_ref, slot(me), local_sem)
        mine.start()
        first = [copy(0, me, sibling, src=x_ref)]
        first += [copy(1 + j, me, (*chip, cc), src=x_ref) for j, chip in enumerate(chips)]
        for cp in first:
            cp.start()
        passed = [copy(4 + j, (*chip, cc), sibling) for j, chip in enumerate(chips)]
        for j, chip in enumerate(chips):
            copy(1 + j, (*chip, cc), me).wait_recv()
            passed[j].start()
        copy(0, sibling, me).wait_recv()
        for j, chip in enumerate(chips):
            copy(4 + j, (*chip, 1 - cc), me).wait_recv()
        for cp in first + passed:
            cp.wait_send()
        mine.wait()
        acc = buf[0]
        for k in range(1, N_DEV):
            acc = acc + buf[k]
        o_ref[...] = acc

    return pl.pallas_call(
        body, in_specs=[pl.BlockSpec(memory_space=pltpu.VMEM)],
        out_specs=pl.BlockSpec(memory_space=pltpu.VMEM),
        out_shape=jax.ShapeDtypeStruct((r, c), F32),
        scratch_shapes=[pltpu.VMEM((N_DEV, r, c), F32), pltpu.SemaphoreType.DMA((7,)),
                        pltpu.SemaphoreType.DMA((7,)), pltpu.SemaphoreType.DMA],
        name=name, compiler_params=_cp())(pack)


def _row(v):
    return v.reshape(1, -1)


def _lane_row(vals, offset):
    return jnp.pad(vals, (offset, LANES - offset - vals.shape[0])).reshape(1, LANES)


def _bias_to_pairs(b):
    i, nh, bp = b.shape
    return b.transpose(1, 0, 2).reshape(nh // 2, 2 * i, bp)


def _bias_from_pairs(b):
    p, i2, bp = b.shape
    return b.reshape(2 * p, i2 // 2, bp).transpose(1, 0, 2)


class LocalWeights:
    def __init__(self, W):
        self.W = W
        self.grads = {}

    def big(self, l, la):
        W = self.W
        out = {"f_w_up": W["f_w_up"][l].T, "f_w_up_n": W["f_w_up"][l], "f_w_down": W["f_w_down"][l]}
        if l < la:
            out.update(a_w_in=W["a_w_in"][l], a_w_out=W["a_w_out"][l])
        else:
            out.update(b_w_q=W["b_w_q"][l - la], b_w_out=W["b_w_out"][l - la])
        if l == la:
            out["w_kv"] = W["w_kv"].T
        return out

    def prep_rider(self, l):
        return None

    def prep_got(self, l, got):
        pass

    def fwd_rider(self, l):
        return None

    def fwd_got(self, l, got):
        pass

    def bwd_rider_a(self, l):
        return None

    def bwd_got_a(self, l, got):
        pass

    def bwd_rider_b(self, l):
        return None

    def bwd_got_b(self, l, got):
        pass

    def bwd_rider_c(self, l):
        return None

    def bwd_got_c(self, l, got):
        pass

    def ffn_grads_ready(self, l, grads):
        pass

    def grads_ready(self, l, grads):
        for k_, g in grads.items():
            self.grads.setdefault(k_, {})[l] = g

    def stacked(self):
        out = {k_: (jnp.stack([v_[l] for l in sorted(v_)]) if k_ != "w_kv" else next(iter(v_.values())))
               for k_, v_ in self.grads.items()}
        out["f_w_up"] = jnp.swapaxes(out["f_w_up"], 1, 2)
        out["w_kv"] = out["w_kv"].T
        return out


def _named(name, l, rider):
    return name if rider is None else f"{name}_x{l}"


def local_step(x, target, W, comm=None):
    comm = LocalWeights(W) if comm is None else comm
    n, d = x.shape
    la, ha = W["a_A_log"].shape
    lb, hb, tbl = W["b_rel_bias"].shape
    depth = W["f_norm"].shape[0]
    clip = (tbl - 1) // 2
    tp = -(-tbl // LANES) * LANES
    qk = ha * A_HEAD
    cw = 3 * qk
    bw = hb * B_HEAD

    h = x
    saves = []
    kv = h_kv = w_kv = None
    for l in range(depth):
        big = comm.big(l, la)
        sv = {"h_in": h, "big": big}
        rider = comm.fwd_rider(l)
        if l < la:
            alog = _lane_row(W["a_A_log"][l], ha)
            dtb = _lane_row(W["a_dt_bias"][l], ha)
            proj = norm_matmul(h, _row(W["a_norm"][l]), big["a_w_in"], "a_in_proj")
            early = comm.prep_rider(l)
            (q, k, v, bb, gb, u), got = gdn_prep(proj, W["a_conv"][l], alog, dtb, ha, _named("gdn_prep", l, early), early)
            comm.prep_got(l, got)
            (o, states, tinv), got = gdn_fwd(q, k, v, bb, gb, ha, _named("gdn_fwd", l, rider), rider)
            h, y = gdn_out(o, proj, _row(W["a_out_norm"][l]), big["a_w_out"], h, ha, "gdn_out")
            sv.update(proj=proj, q=q, k=k, v=v, bb=bb, gb=gb, u=u, states=states, tinv=tinv, o=o, y=y, alog=alog, dtb=dtb)
        else:
            j = l - la
            if j == 0:
                h_kv, w_kv = h, big["w_kv"]
                kv = norm_matmul(h, _row(W["kv_norm"]), w_kv, "kv_proj", w_t=True)
            qp = norm_matmul(h, _row(W["b_norm"][j]), big["b_w_q"], "b_q_proj")
            tblp = jnp.pad(W["b_rel_bias"][j], ((0, 0), (0, tp - tbl)))
            bias = _bias_to_pairs(bias_expand(tblp, clip, "bias_expand"))
            (o,), got = attn_fwd(qp, kv, bias, _named("attn_fwd", l, rider), rider)
            h = matmul_res(o, big["b_w_out"], h, "b_out_proj")
            sv.update(qp=qp, bias=bias, o=o)
        comm.fwd_got(l, got)
        sv["h_mid"] = h
        up = norm_matmul(h, _row(W["f_norm"][l]), big["f_w_up"], "f_up_proj", out_dtype=BF16, w_t=True)
        h, act, hc = ffn_act_down(up, W["f_conv"][l], _row(W["f_conv_b"][l]), big["f_w_down"], h, "ffn_act_down")
        sv.update(up=up, act=act, hc=hc)
        saves.append(sv)

    loss, dh, d_final = loss_head(h, _row(W["final_norm"]), target)

    G = {k_: [None] * (la if k_.startswith("a_") else lb if k_.startswith("b_") else depth)
         for k_ in ("a_norm", "a_conv", "a_A_log", "a_dt_bias", "a_out_norm",
                    "b_norm", "b_rel_bias", "f_norm", "f_conv", "f_conv_b")}
    G["final_norm"] = d_final[0]
    dk_acc = dv_acc = None
    for l in reversed(range(depth)):
        sv = saves[l]
        big = sv["big"]
        gbig = {}
        dhc, dcb = ffn_bwd_act(dh, sv["hc"], big["f_w_down"], "ffn_bwd_act")
        gbig["f_w_down"] = matmul_tn(sv["act"], dh, "f_down_wgrad")
        G["f_conv_b"][l] = dcb[0]
        rider = comm.bwd_rider_a(l)
        (dh, dup, dcw, dg), got = ffn_bwd_up(dhc, sv["up"], W["f_conv"][l], big["f_w_up_n"], sv["h_mid"], dh,
                                             _row(W["f_norm"][l]), _named("ffn_bwd_up", l, rider), rider)
        comm.bwd_got_a(l, got)
        G["f_conv"][l] = dcw
        G["f_norm"][l] = dg[0]
        gbig["f_w_up"] = norm_matmul_tn(sv["h_mid"], _row(W["f_norm"][l]), dup, "f_up_wgrad", transposed=True)
        comm.ffn_grads_ready(l, gbig)
        rider = comm.bwd_rider_b(l)
        if l < la:
            w_in = big["a_w_in"]
            do, dz, dwn = gdn_out_bwd(dh, sv["o"], sv["proj"], _row(W["a_out_norm"][l]), big["a_w_out"], ha, "gdn_out_bwd")
            G["a_out_norm"][l] = dwn[0]
            gbig["a_w_out"] = matmul_tn(sv["y"], dh, "a_out_wgrad")
            (dq, dk, dv, dbb, dgb), got = gdn_bwd(sv["q"], sv["k"], sv["v"], sv["bb"], sv["gb"], sv["states"], sv["tinv"], do, ha,
                                                  _named("gdn_bwd", l, rider), rider)
            comm.bwd_got_b(l, got)
            du, dba, dal, ddt = gdn_prep_bwd(sv["proj"], sv["u"], sv["alog"], sv["dtb"],
                                             dq, dk, dv, dbb, dgb, ha, "gdn_prep_bwd")
            G["a_A_log"][l] = dal[0, ha:2 * ha]
            G["a_dt_bias"][l] = ddt[0, ha:2 * ha]
            rider = comm.bwd_rider_c(l)
            (dqkv, dconv), got = conv_bwd(du, sv["proj"], W["a_conv"][l], A_CONV, _named("gdn_conv_bwd", l, rider), rider)
            if rider is not None:
                comm.bwd_got_c(l, got)
            G["a_conv"][l] = dconv
            gam = _row(W["a_norm"][l])
            pieces = [(dqkv, w_in[:, :cw]), (dz, w_in[:, cw:cw + qk]), (dba, w_in[:, cw + qk:])]
            gbig["a_w_in"] = jnp.concatenate(
                [norm_matmul_tn(sv["h_in"], gam, dqkv, "a_in_wgrad_qkv"),
                 norm_matmul_tn(sv["h_in"], gam, dz, "a_in_wgrad_z"),
                 norm_matmul_tn(sv["h_in"], gam, dba, "a_in_wgrad_ba")[:, :2 * ha]], axis=1)
            dh, dg = dx_norm_bwd(dh, sv["h_in"], gam, pieces, "a_in_dx")
            G["a_norm"][l] = dg[0]
        else:
            j = l - la
            d_o = matmul_nt(dh, big["b_w_out"], "b_out_dx")
            gbig["b_w_out"] = matmul_tn(sv["o"], dh, "b_out_wgrad")
            (dq, dk_acc, dv_acc, dbias), got = attn_bwd(
                sv["qp"], kv, sv["bias"], d_o, dk_acc, dv_acc,
                _named("attn_bwd" if dk_acc is None else "attn_bwd_acc", l, rider), rider)
            comm.bwd_got_b(l, got)
            G["b_rel_bias"][j] = bias_expand_bwd(_bias_from_pairs(dbias), clip, tp, "bias_expand_bwd")[:, :tbl]
            gam = _row(W["b_norm"][j])
            gbig["b_w_q"] = norm_matmul_tn(sv["h_in"], gam, dq, "b_q_wgrad")
            dh, dg = dx_norm_bwd(dh, sv["h_in"], gam, [(dq, big["b_w_q"])], "b_q_dx")
            G["b_norm"][j] = dg[0]
            if j == 0:
                gam = _row(W["kv_norm"])
                gbig["w_kv"] = jnp.concatenate([norm_matmul_tn(h_kv, gam, dk_acc, "kv_wgrad_k", transposed=True),
                                                norm_matmul_tn(h_kv, gam, dv_acc, "kv_wgrad_v", transposed=True)], axis=0)
                dh, dg = dx_norm_bwd(dh, h_kv, gam, [(dk_acc, w_kv[:bw]), (dv_acc, w_kv[bw:])], "kv_dx", w_t=True)
                G["kv_norm"] = dg[0]
        comm.grads_ready(l, gbig)
    out = {k_: (jnp.stack(v_) if isinstance(v_, list) else v_) for k_, v_ in G.items()}
    return loss[0, 0], dh, out, comm


WEIGHTS = ["a_norm", "a_w_in", "a_conv", "a_A_log", "a_dt_bias", "a_out_norm", "a_w_out", "kv_norm", "w_kv",
           "b_norm", "b_w_q", "b_rel_bias", "b_w_out", "f_norm", "f_w_up", "f_conv", "f_conv_b", "f_w_down",
           "final_norm"]
SHARD_AXIS = {"a_norm": 1, "a_w_in": 2, "a_conv": 2, "a_w_out": 1, "w_kv": 1, "b_w_q": 1, "b_w_out": 1,
              "f_w_up": 2, "f_conv": 2, "f_w_down": 1}
BIG = ["a_w_in", "a_w_out", "w_kv", "b_w_q", "b_w_out", "f_w_up", "f_w_down"]
SMALL_SHARDED = ["a_norm", "a_conv", "f_conv"]
TRANSPOSED = ("f_w_up", "w_kv")


def _t_view(k, a):
    return jnp.swapaxes(a, -1, -2) if k in TRANSPOSED else a


def _unstack(g, axis):
    if axis == 0:
        return g.reshape(-1, *g.shape[2:])
    return jnp.concatenate([g[i] for i in range(N_DEV)], axis=axis)


def _to_blocks(full, axis):
    if axis == 0:
        return full.reshape(N_DEV, -1, full.shape[-1])
    return jnp.stack(jnp.split(full, N_DEV, axis=axis))


def _pack(arrs):
    flat = []
    for a in arrs:
        f = a.reshape(-1)
        flat.append(jnp.pad(f, (0, (-f.shape[0]) % LANES)))
    f = jnp.concatenate(flat)
    f = jnp.pad(f, (0, (-f.shape[0]) % (8 * LANES)))
    return f.reshape(-1, LANES)


def _unpack(pack, shapes):
    flat = pack.reshape(-1)
    out, pos = [], 0
    for s in shapes:
        sz = math.prod(s)
        out.append(flat[pos:pos + sz].reshape(s))
        pos += sz + (-sz) % LANES
    return out


def _as2d(a):
    return a.reshape(1, -1) if a.ndim == 1 else a.reshape(-1, a.shape[-1])


def kernel(x, a_norm, a_w_in, a_conv, a_A_log, a_dt_bias, a_out_norm, a_w_out, kv_norm, w_kv, b_norm, b_w_q, b_rel_bias, b_w_out, f_norm, f_w_up, f_conv, f_conv_b, f_w_down, final_norm, loss_target, m_a_norm, m_a_w_in, m_a_conv, m_a_A_log, m_a_dt_bias, m_a_out_norm, m_a_w_out, m_kv_norm, m_w_kv, m_b_norm, m_b_w_q, m_b_rel_bias, m_b_w_out, m_f_norm, m_f_w_up, m_f_conv, m_f_conv_b, m_f_w_down, m_final_norm, v_a_norm, v_a_w_in, v_a_conv, v_a_A_log, v_a_dt_bias, v_a_out_norm, v_a_w_out, v_kv_norm, v_w_kv, v_b_norm, v_b_w_q, v_b_rel_bias, v_b_w_out, v_f_norm, v_f_w_up, v_f_conv, v_f_conv_b, v_f_w_down, v_final_norm):
    w = dict(a_norm=a_norm, a_w_in=a_w_in, a_conv=a_conv, a_A_log=a_A_log, a_dt_bias=a_dt_bias,
             a_out_norm=a_out_norm, a_w_out=a_w_out, kv_norm=kv_norm, w_kv=w_kv, b_norm=b_norm, b_w_q=b_w_q,
             b_rel_bias=b_rel_bias, b_w_out=b_w_out, f_norm=f_norm, f_w_up=f_w_up, f_conv=f_conv,
             f_conv_b=f_conv_b, f_w_down=f_w_down, final_norm=final_norm)
    mom = dict(a_norm=m_a_norm, a_w_in=m_a_w_in, a_conv=m_a_conv, a_A_log=m_a_A_log, a_dt_bias=m_a_dt_bias,
               a_out_norm=m_a_out_norm, a_w_out=m_a_w_out, kv_norm=m_kv_norm, w_kv=m_w_kv, b_norm=m_b_norm,
               b_w_q=m_b_w_q, b_rel_bias=m_b_rel_bias, b_w_out=m_b_w_out, f_norm=m_f_norm, f_w_up=m_f_w_up,
               f_conv=m_f_conv, f_conv_b=m_f_conv_b, f_w_down=m_f_w_down, final_norm=m_final_norm)
    var = dict(a_norm=v_a_norm, a_w_in=v_a_w_in, a_conv=v_a_conv, a_A_log=v_a_A_log, a_dt_bias=v_a_dt_bias,
               a_out_norm=v_a_out_norm, a_w_out=v_a_w_out, kv_norm=v_kv_norm, w_kv=v_w_kv, b_norm=v_b_norm,
               b_w_q=v_b_w_q, b_rel_bias=v_b_rel_bias, b_w_out=v_b_w_out, f_norm=v_f_norm, f_w_up=v_f_w_up,
               f_conv=v_f_conv, f_conv_b=v_f_conv_b, f_w_down=v_f_w_down, final_norm=v_final_norm)
    me = 4 * lax.axis_index("x") + 2 * lax.axis_index("y") + lax.axis_index("c")

    la, depth = a_A_log.shape[0], f_norm.shape[0]
    c_idx = lax.axis_index("c").astype(jnp.int32).reshape(1)
    shard_bf16 = {k: _t_view(k, w[k]).astype(BF16) for k in BIG}
    blk_axis = {k: 0 if k in TRANSPOSED else SHARD_AXIS[k] - (k != "w_kv") for k in BIG}

    class Sharded(LocalWeights):
        def __init__(self):
            super().__init__(w)
            self.full = {}
            self.stacks = {}
            self.pending = None
            self.parts = None

        def names(self, l):
            out = ["a_w_in", "a_w_out"] if l < la else ["b_w_q", "b_w_out"]
            return out + ["f_w_up", "f_w_down"] + (["w_kv"] if l == la else [])

        def index(self, k, l):
            return None if k == "w_kv" else (l - la if k.startswith("b_") else l)

        def shards(self, l, names=None):
            return [shard_bf16[k] if k == "w_kv" else shard_bf16[k][self.index(k, l)]
                    for k in (self.names(l) if names is None else names)]

        def install(self, l, gathered, names=None):
            out = self.full.setdefault(l, {})
            for k, g in zip(self.names(l) if names is None else names, gathered):
                out[k] = _unstack(g, blk_axis[k])
                if k == "f_w_up":
                    out["f_w_up_n"] = out[k].T
                if k == "a_w_in":
                    out[k] = jnp.pad(out[k], ((0, 0), (0, (-out[k].shape[1]) % LANES)))

        def big(self, l, la_):
            return self.full[l]

        def first_names(self):
            return ["a_w_in"] if la > 0 else self.names(0)

        def prep_rider(self, l):
            rest = [k for k in self.names(0) if k not in self.first_names()]
            return ("all_gather", self.shards(0, rest)) if l == 0 and rest else None

        def prep_got(self, l, got):
            if got is not None:
                self.install(0, got, [k for k in self.names(0) if k not in self.first_names()])

        def fwd_rider(self, l):
            return ("all_gather", self.shards(l + 1)) if l + 1 < depth else None

        def fwd_got(self, l, got):
            if got is not None:
                self.install(l + 1, got)

        def blocks(self, grads, keys):
            return [_to_blocks(grads[k], blk_axis[k]) for k in keys]

        def grads_ready(self, l, grads):
            keys = [k for k in self.names(l) if (k, l) not in self.early_keys]
            self.pending = (l, keys, self.blocks(grads, keys))

        early = early_parts = None
        early_keys = ()

        def ffn_grads_ready(self, l, grads):
            if l == 0 and la > 0:
                keys = ["f_w_up", "f_w_down"]
                self.early = (keys, self.blocks(grads, keys))
                self.early_keys = tuple((k, 0) for k in keys)

        def bwd_rider_a(self, l):
            return None if self.pending is None else ("rs_sibling", self.pending[2])

        def add_pairs(self, g8, from_sibling):
            return [pair_add(g, r, c_idx, "grads_pair_add") for g, r in zip(g8, from_sibling)]

        def bwd_got_a(self, l, got):
            if got is not None:
                self.parts = self.add_pairs(self.pending[2], got)

        def bwd_rider_b(self, l):
            riders = [] if self.parts is None else [("rs_chips", self.parts)]
            if self.early is not None:
                riders.append(("rs_sibling", self.early[1]))
            return riders

        def keep(self, stacks):
            l, keys, _ = self.pending
            for k, s in zip(keys, stacks):
                self.stacks[(k, l)] = s
            self.pending = self.parts = None

        def bwd_got_b(self, l, got):
            got = list(got or [])
            if self.parts is not None:
                self.keep(got.pop(0))
            if self.early is not None and got:
                self.early_parts = self.add_pairs(self.early[1], got.pop(0))

        def bwd_rider_c(self, l):
            return None if self.early_parts is None else ("rs_chips", self.early_parts)

        def bwd_got_c(self, l, got):
            for k, s in zip(self.early[0], got):
                self.stacks[(k, 0)] = s
            self.early = self.early_parts = None

        def finish(self):
            self.parts = self.add_pairs(self.pending[2], exchange("rs_sibling", self.pending[2], "grads_to_sibling"))
            self.keep(exchange("rs_chips", self.parts, "grads_to_chips"))

    comm = Sharded()

    small_shapes = [w[k].shape for k in SMALL_SHARDED]
    gathered = exchange("all_gather", comm.shards(0, comm.first_names()) + [_pack([w[k] for k in SMALL_SHARDED])],
                        "weights_all_gather")
    comm.install(0, gathered[:-1], comm.first_names())
    full = dict(w)
    small = [_unpack(gathered[-1][i], small_shapes) for i in range(N_DEV)]
    for idx, k in enumerate(SMALL_SHARDED):
        full[k] = jnp.concatenate([small[i][idx] for i in range(N_DEV)], axis=SHARD_AXIS[k])

    loss_part, grad_x, G, _ = local_step(x[0], loss_target[0], full, comm)
    comm.finish()
    stacks = []
    for k in BIG:
        layers = sorted(l for (k_, l) in comm.stacks if k_ == k)
        stacks.append(jnp.concatenate([comm.stacks[(k, l)] for l in layers], axis=1))

    small_names = [k for k in WEIGHTS if k not in BIG]
    reduced = _unpack(all_reduce_small(_pack([G[k] for k in small_names] + [loss_part.reshape(1)]), "small_all_reduce"),
                      [G[k].shape for k in small_names] + [(1,)])
    loss = reduced[-1][0]
    small_g = dict(zip(small_names, reduced[:-1]))
    for k in SMALL_SHARDED:
        sz = w[k].shape[SHARD_AXIS[k]]
        small_g[k] = lax.dynamic_slice_in_dim(small_g[k], me * sz, sz, axis=SHARD_AXIS[k])

    res = {}
    for k, st in zip(BIG, stacks):
        tshape = _t_view(k, w[k]).shape
        wt, mt, vt = (_as2d(_t_view(k, a)) for a in (w[k], mom[k], var[k]))
        outs = adamw(wt, st, mt, vt, "adamw_" + k)
        res[k] = [_t_view(k, o.reshape(tshape)) for o in outs]
    for k in small_names:
        outs = adamw(_as2d(w[k]), _as2d(small_g[k])[None], _as2d(mom[k]), _as2d(var[k]), "adamw_" + k)
        res[k] = [o.reshape(w[k].shape) for o in outs]

    return (loss, grad_x[None], *[res[k][0] for k in WEIGHTS], *[res[k][1] for k in WEIGHTS],
            *[res[k][2] for k in WEIGHTS], *[res[k][3] for k in WEIGHTS])
```
